```python
import math
import jax, jax.numpy as jnp
from jax import lax
import numpy as np

D_MODEL = 1024
BATCH = 8
SEQ = 4096
DEPTH = 4

MIX_WIDTH = D_MODEL
BRANCH = MIX_WIDTH // 4
CONF_KERNEL = 31
S5_GROUP = 16
S5_GROUPS = BRANCH // S5_GROUP
S5_STATE = 64
SC_KERNEL = 3
DN_HEADS = 4
DN_HEAD_DIM = BRANCH // DN_HEADS
DN_CONV = 4
DN_CHUNK = 64
NORM_EPS = 1e-6
IN_WIDTHS = (BRANCH, BRANCH, BRANCH,
             BRANCH, BRANCH,
             BRANCH, BRANCH, BRANCH, BRANCH,
             BRANCH, BRANCH, BRANCH, DN_HEADS, DN_HEADS, BRANCH)
IN_COLS = 13 * BRANCH + 2 * DN_HEADS

kernel_name = "hybrid_parallel_conv_s5_shortconv_deltanet"


def rms_norm(x, g):
    x32 = x.astype(jnp.float32)
    y = x32 * lax.rsqrt(jnp.mean(x32 * x32, axis=-1, keepdims=True) + NORM_EPS)
    return (y * g.astype(jnp.float32)).astype(x.dtype)


def layer_norm(x, g, b):
    x32 = x.astype(jnp.float32)
    mu = jnp.mean(x32, axis=-1, keepdims=True)
    xc = x32 - mu
    y = xc * lax.rsqrt(jnp.mean(xc * xc, axis=-1, keepdims=True) + NORM_EPS)
    return (y * g.astype(jnp.float32) + b.astype(jnp.float32)).astype(x.dtype)


def l2_normalize(x):
    return x * lax.rsqrt(jnp.sum(x * x, axis=-1, keepdims=True) + NORM_EPS)


def causal_depthwise_conv(x, w):
    K, C = w.shape
    return lax.conv_general_dilated(
        x, w[:, None, :], window_strides=(1,), padding=[(K - 1, 0)],
        dimension_numbers=("NWC", "WIO", "NWC"), feature_group_count=C)


def split_columns(p):
    outs, start = [], 0
    for w in IN_WIDTHS:
        outs.append(p[..., start:start + w])
        start += w
    return outs


def conformer_conv_branch(val, gate, conv_w, conv_b, ln_g, ln_b, pw_w, pw_b):
    a = val * jax.nn.sigmoid(gate)
    a = causal_depthwise_conv(a, conv_w) + conv_b
    a = layer_norm(a, ln_g, ln_b)
    a = jax.nn.silu(a)
    return a @ pw_w + pw_b


def s5_branch(u, lam_re, lam_im, b_re, b_im, c_re, c_im, d_skip, log_dt, glu_w, glu_b):
    bsz, L, _ = u.shape
    f32 = jnp.float32
    u32 = u.astype(f32)
    lam = lax.complex(jnp.minimum(lam_re.astype(f32), -1e-4), lam_im.astype(f32))
    dt = jnp.exp(log_dt.astype(f32))[:, None]
    lam_bar = jnp.exp(lam * dt)
    b = lax.complex(b_re.astype(f32), b_im.astype(f32))
    b_bar = ((lam_bar - 1.0) / lam)[..., None] * b
    ug = u32.reshape(bsz, L, S5_GROUPS, S5_GROUP).astype(jnp.complex64)
    bu = jnp.einsum("blgh,gph->blgp", ug, b_bar)
    a = jnp.broadcast_to(lam_bar, bu.shape)

    def combine(e1, e2):
        a1, s1 = e1
        a2, s2 = e2
        return a1 * a2, a2 * s1 + s2

    _, states = lax.associative_scan(combine, (a, bu), axis=1)
    c = lax.complex(c_re.astype(f32), c_im.astype(f32))
    y = jnp.real(jnp.einsum("blgp,ghp->blgh", states, c)).reshape(bsz, L, BRANCH)
    y = y + d_skip.astype(f32) * u32
    y = jax.nn.gelu(y).astype(u.dtype)
    return y * jax.nn.sigmoid(y @ glu_w + glu_b)


def short_conv_branch(bg, cg, xc, conv_w):
    return bg * causal_depthwise_conv(cg * xc, conv_w)


def gated_delta_rule_chunked(q, k, v, beta, g):
    bsz, L, H, dk = q.shape
    dv = v.shape[-1]
    C = DN_CHUNK
    N = L // C

    def chunk(t):
        t = t.reshape(bsz, N, C, H, t.shape[-1])
        return jnp.transpose(t, (1, 0, 3, 2, 4))

    q, k, v = chunk(q), chunk(k), chunk(v)
    beta = chunk(beta[..., None])[..., 0]
    g = chunk(g[..., None])[..., 0]
    gc = jnp.cumsum(g, axis=-1)
    idx = jnp.arange(C)
    causal = idx[:, None] >= idx[None, :]
    strict = idx[:, None] > idx[None, :]
    decay = jnp.exp(jnp.where(causal, gc[..., :, None] - gc[..., None, :], -jnp.inf))
    k_beta = k * beta[..., None]
    lmat = jnp.where(strict, jnp.einsum("nbhid,nbhjd->nbhij", k_beta, k) * decay, 0.0)
    rhs = jnp.concatenate([v * beta[..., None], k_beta * jnp.exp(gc)[..., None]], axis=-1)
    sol = lax.linalg.triangular_solve(lmat + jnp.eye(C, dtype=lmat.dtype), rhs,
                                      left_side=True, lower=True)
    u, w = sol[..., :dv], sol[..., dv:]
    attn = jnp.einsum("nbhid,nbhjd->nbhij", q, k) * decay
    q_dec = q * jnp.exp(gc)[..., None]
    g_last = gc[..., -1]
    k_dec = k * jnp.exp(g_last[..., None] - gc)[..., None]

    def step(S, inp):
        attn_c, u_c, w_c, qd, kd, gl = inp
        v_new = u_c - jnp.einsum("bhik,bhkv->bhiv", w_c, S)
        o = jnp.einsum("bhik,bhkv->bhiv", qd, S) + jnp.einsum("bhij,bhjv->bhiv", attn_c, v_new)
        S = S * jnp.exp(gl)[..., None, None] + jnp.einsum("bhik,bhiv->bhkv", kd, v_new)
        return S, o

    S0 = jnp.zeros((bsz, H, dk, dv), jnp.float32)
    _, o = lax.scan(step, S0, (attn, u, w, q_dec, k_dec, g_last))
    return jnp.transpose(o, (1, 0, 3, 2, 4)).reshape(bsz, L, H, dv)


def deltanet_branch(q, k, v, alpha, beta_logit, conv_w, a_log, dt_bias, norm_g):
    bsz, L, _ = q.shape
    f32 = jnp.float32
    qkv = jax.nn.silu(causal_depthwise_conv(jnp.concatenate([q, k, v], axis=-1), conv_w))
    q, k, v = qkv[..., :BRANCH], qkv[..., BRANCH:2 * BRANCH], qkv[..., 2 * BRANCH:]
    q = q.astype(f32).reshape(bsz, L, DN_HEADS, DN_HEAD_DIM)
    k = k.astype(f32).reshape(bsz, L, DN_HEADS, DN_HEAD_DIM)
    v = v.astype(f32).reshape(bsz, L, DN_HEADS, DN_HEAD_DIM)
    q = l2_normalize(q) * (DN_HEAD_DIM ** -0.5)
    k = l2_normalize(k)
    beta = jax.nn.sigmoid(beta_logit.astype(f32))
    g = -jnp.exp(a_log.astype(f32)) * jax.nn.softplus(alpha.astype(f32) + dt_bias.astype(f32))
    o = gated_delta_rule_chunked(q, k, v, beta, g)
    o = rms_norm(o, norm_g)
    return o.reshape(bsz, L, BRANCH)


def _fwd_setup_inputs(seed: int = 0) -> dict:
    key = jax.random.key(seed)
    ks = jax.random.split(key, 32)
    f32 = jnp.float32

    def nrm(k, shape, s):
        return jax.random.normal(k, shape, f32) * s

    G, P, H = S5_GROUPS, S5_STATE, S5_GROUP
    x = nrm(ks[0], (BATCH, SEQ, D_MODEL), 1.0)
    norm_g = 1.0 + nrm(ks[1], (DEPTH, D_MODEL), 0.02)
    w_in = nrm(ks[2], (DEPTH, D_MODEL, IN_COLS), D_MODEL ** -0.5)
    a_conv_w = nrm(ks[3], (DEPTH, CONF_KERNEL, BRANCH), CONF_KERNEL ** -0.5)
    a_conv_b = nrm(ks[4], (DEPTH, BRANCH), 0.02)
    a_ln_g = 1.0 + nrm(ks[5], (DEPTH, BRANCH), 0.02)
    a_ln_b = nrm(ks[6], (DEPTH, BRANCH), 0.02)
    a_pw_w = nrm(ks[7], (DEPTH, BRANCH, BRANCH), BRANCH ** -0.5)
    a_pw_b = nrm(ks[8], (DEPTH, BRANCH), 0.02)
    n_idx = jnp.arange(P, dtype=f32)
    s5_lambda_re = -0.5 + nrm(ks[9], (DEPTH, G, P), 0.01)
    s5_lambda_im = math.pi * n_idx + nrm(ks[10], (DEPTH, G, P), 0.01)
    s5_b_re = nrm(ks[11], (DEPTH, G, P, H), (2.0 * H) ** -0.5)
    s5_b_im = nrm(ks[12], (DEPTH, G, P, H), (2.0 * H) ** -0.5)
    s5_c_re = nrm(ks[13], (DEPTH, G, H, P), (2.0 * P) ** -0.5)
    s5_c_im = nrm(ks[14], (DEPTH, G, H, P), (2.0 * P) ** -0.5)
    s5_d = nrm(ks[15], (DEPTH, BRANCH), 0.5)
    s5_log_dt = jax.random.uniform(ks[16], (DEPTH, G), f32, math.log(1e-3), math.log(1e-1))
    s5_glu_w = nrm(ks[17], (DEPTH, BRANCH, BRANCH), BRANCH ** -0.5)
    s5_glu_b = nrm(ks[18], (DEPTH, BRANCH), 0.02)
    c_conv_w = nrm(ks[19], (DEPTH, SC_KERNEL, BRANCH), SC_KERNEL ** -0.5)
    d_conv_w = nrm(ks[20], (DEPTH, DN_CONV, 3 * BRANCH), DN_CONV ** -0.5)
    d_a_log = jnp.log(jax.random.uniform(ks[21], (DEPTH, DN_HEADS), f32, 1.0, 16.0))
    dt0 = jnp.exp(jax.random.uniform(ks[22], (DEPTH, DN_HEADS), f32, math.log(1e-3), math.log(1e-1)))
    d_dt_bias = dt0 + jnp.log(-jnp.expm1(-dt0))
    d_norm_g = 1.0 + nrm(ks[23], (DEPTH, DN_HEAD_DIM), 0.02)
    w_out = nrm(ks[24], (DEPTH, MIX_WIDTH, D_MODEL), MIX_WIDTH ** -0.5)
    final_g = 1.0 + nrm(ks[25], (D_MODEL,), 0.02)
    return {"x": x, "norm_g": norm_g, "w_in": w_in,
            "a_conv_w": a_conv_w, "a_conv_b": a_conv_b, "a_ln_g": a_ln_g, "a_ln_b": a_ln_b,
            "a_pw_w": a_pw_w, "a_pw_b": a_pw_b,
            "s5_lambda_re": s5_lambda_re, "s5_lambda_im": s5_lambda_im,
            "s5_b_re": s5_b_re, "s5_b_im": s5_b_im, "s5_c_re": s5_c_re, "s5_c_im": s5_c_im,
            "s5_d": s5_d, "s5_log_dt": s5_log_dt, "s5_glu_w": s5_glu_w, "s5_glu_b": s5_glu_b,
            "c_conv_w": c_conv_w,
            "d_conv_w": d_conv_w, "d_a_log": d_a_log, "d_dt_bias": d_dt_bias, "d_norm_g": d_norm_g,
            "w_out": w_out, "final_g": final_g}


def _fwd_reference(x, norm_g, w_in, a_conv_w, a_conv_b, a_ln_g, a_ln_b, a_pw_w, a_pw_b,
              s5_lambda_re, s5_lambda_im, s5_b_re, s5_b_im, s5_c_re, s5_c_im,
              s5_d, s5_log_dt, s5_glu_w, s5_glu_b, c_conv_w,
              d_conv_w, d_a_log, d_dt_bias, d_norm_g, w_out, final_g):
    for l in range(DEPTH):
        h = rms_norm(x, norm_g[l])
        proj = h @ w_in[l]
        (a_val, a_gate, a_z, b_u, b_z, c_b, c_c, c_x, c_z,
         d_q, d_k, d_v, d_alpha, d_beta, d_z) = split_columns(proj)
        ya = conformer_conv_branch(a_val, a_gate, a_conv_w[l], a_conv_b[l], a_ln_g[l],
                                   a_ln_b[l], a_pw_w[l], a_pw_b[l]) * jax.nn.silu(a_z)
        yb = s5_branch(b_u, s5_lambda_re[l], s5_lambda_im[l], s5_b_re[l], s5_b_im[l],
                       s5_c_re[l], s5_c_im[l], s5_d[l], s5_log_dt[l],
                       s5_glu_w[l], s5_glu_b[l]) * jax.nn.silu(b_z)
        yc = short_conv_branch(c_b, c_c, c_x, c_conv_w[l]) * jax.nn.silu(c_z)
        yd = deltanet_branch(d_q, d_k, d_v, d_alpha, d_beta, d_conv_w[l], d_a_log[l],
                             d_dt_bias[l], d_norm_g[l]).astype(x.dtype) * jax.nn.silu(d_z)
        mixed = jnp.concatenate([ya, yb.astype(x.dtype), yc, yd], axis=-1)
        x = x + mixed @ w_out[l]
    return rms_norm(x, final_g)


import jax as _jax
import jax.numpy as _jnp

TWIN_FORMAT = 'train_step'
FWD_PARAMS = ['x', 'norm_g', 'w_in', 'a_conv_w', 'a_conv_b', 'a_ln_g', 'a_ln_b', 'a_pw_w', 'a_pw_b', 's5_lambda_re', 's5_lambda_im', 's5_b_re', 's5_b_im', 's5_c_re', 's5_c_im', 's5_d', 's5_log_dt', 's5_glu_w', 's5_glu_b', 'c_conv_w', 'd_conv_w', 'd_a_log', 'd_dt_bias', 'd_norm_g', 'w_out', 'final_g']
TWIN_WEIGHTS = ['norm_g', 'w_in', 'a_conv_w', 'a_conv_b', 'a_ln_g', 'a_ln_b', 'a_pw_w', 'a_pw_b', 's5_lambda_re', 's5_lambda_im', 's5_b_re', 's5_b_im', 's5_c_re', 's5_c_im', 's5_d', 's5_log_dt', 's5_glu_w', 's5_glu_b', 'c_conv_w', 'd_conv_w', 'd_a_log', 'd_dt_bias', 'd_norm_g', 'w_out', 'final_g']
TWIN_DIFF_INPUT = 'x'
TWIN_INPUTS = ['x', 'norm_g', 'w_in', 'a_conv_w', 'a_conv_b', 'a_ln_g', 'a_ln_b', 'a_pw_w', 'a_pw_b', 's5_lambda_re', 's5_lambda_im', 's5_b_re', 's5_b_im', 's5_c_re', 's5_c_im', 's5_d', 's5_log_dt', 's5_glu_w', 's5_glu_b', 'c_conv_w', 'd_conv_w', 'd_a_log', 'd_dt_bias', 'd_norm_g', 'w_out', 'final_g', 'loss_target', 'm_norm_g', 'm_w_in', 'm_a_conv_w', 'm_a_conv_b', 'm_a_ln_g', 'm_a_ln_b', 'm_a_pw_w', 'm_a_pw_b', 'm_s5_lambda_re', 'm_s5_lambda_im', 'm_s5_b_re', 'm_s5_b_im', 'm_s5_c_re', 'm_s5_c_im', 'm_s5_d', 'm_s5_log_dt', 'm_s5_glu_w', 'm_s5_glu_b', 'm_c_conv_w', 'm_d_conv_w', 'm_d_a_log', 'm_d_dt_bias', 'm_d_norm_g', 'm_w_out', 'm_final_g', 'v_norm_g', 'v_w_in', 'v_a_conv_w', 'v_a_conv_b', 'v_a_ln_g', 'v_a_ln_b', 'v_a_pw_w', 'v_a_pw_b', 'v_s5_lambda_re', 'v_s5_lambda_im', 'v_s5_b_re', 'v_s5_b_im', 'v_s5_c_re', 'v_s5_c_im', 'v_s5_d', 'v_s5_log_dt', 'v_s5_glu_w', 'v_s5_glu_b', 'v_c_conv_w', 'v_d_conv_w', 'v_d_a_log', 'v_d_dt_bias', 'v_d_norm_g', 'v_w_out', 'v_final_g']
TWIN_OUTPUTS = ['loss', 'grad_x', 'grad_norm_g', 'grad_w_in', 'grad_a_conv_w', 'grad_a_conv_b', 'grad_a_ln_g', 'grad_a_ln_b', 'grad_a_pw_w', 'grad_a_pw_b', 'grad_s5_lambda_re', 'grad_s5_lambda_im', 'grad_s5_b_re', 'grad_s5_b_im', 'grad_s5_c_re', 'grad_s5_c_im', 'grad_s5_d', 'grad_s5_log_dt', 'grad_s5_glu_w', 'grad_s5_glu_b', 'grad_c_conv_w', 'grad_d_conv_w', 'grad_d_a_log', 'grad_d_dt_bias', 'grad_d_norm_g', 'grad_w_out', 'grad_final_g', 'delta_norm_g', 'delta_w_in', 'delta_a_conv_w', 'delta_a_conv_b', 'delta_a_ln_g', 'delta_a_ln_b', 'delta_a_pw_w', 'delta_a_pw_b', 'delta_s5_lambda_re', 'delta_s5_lambda_im', 'delta_s5_b_re', 'delta_s5_b_im', 'delta_s5_c_re', 'delta_s5_c_im', 'delta_s5_d', 'delta_s5_log_dt', 'delta_s5_glu_w', 'delta_s5_glu_b', 'delta_c_conv_w', 'delta_d_conv_w', 'delta_d_a_log', 'delta_d_dt_bias', 'delta_d_norm_g', 'delta_w_out', 'delta_final_g', 'new_m_norm_g', 'new_m_w_in', 'new_m_a_conv_w', 'new_m_a_conv_b', 'new_m_a_ln_g', 'new_m_a_ln_b', 'new_m_a_pw_w', 'new_m_a_pw_b', 'new_m_s5_lambda_re', 'new_m_s5_lambda_im', 'new_m_s5_b_re', 'new_m_s5_b_im', 'new_m_s5_c_re', 'new_m_s5_c_im', 'new_m_s5_d', 'new_m_s5_log_dt', 'new_m_s5_glu_w', 'new_m_s5_glu_b', 'new_m_c_conv_w', 'new_m_d_conv_w', 'new_m_d_a_log', 'new_m_d_dt_bias', 'new_m_d_norm_g', 'new_m_w_out', 'new_m_final_g', 'new_v_norm_g', 'new_v_w_in', 'new_v_a_conv_w', 'new_v_a_conv_b', 'new_v_a_ln_g', 'new_v_a_ln_b', 'new_v_a_pw_w', 'new_v_a_pw_b', 'new_v_s5_lambda_re', 'new_v_s5_lambda_im', 'new_v_s5_b_re', 'new_v_s5_b_im', 'new_v_s5_c_re', 'new_v_s5_c_im', 'new_v_s5_d', 'new_v_s5_log_dt', 'new_v_s5_glu_w', 'new_v_s5_glu_b', 'new_v_c_conv_w', 'new_v_d_conv_w', 'new_v_d_a_log', 'new_v_d_dt_bias', 'new_v_d_norm_g', 'new_v_w_out', 'new_v_final_g']
TWIN_LEAF_KINDS = {'loss': 'loss', 'grad_x': 'grad_x', 'grad_norm_g': 'grad_w', 'grad_w_in': 'grad_w', 'grad_a_conv_w': 'grad_w', 'grad_a_conv_b': 'grad_w', 'grad_a_ln_g': 'grad_w', 'grad_a_ln_b': 'grad_w', 'grad_a_pw_w': 'grad_w', 'grad_a_pw_b': 'grad_w', 'grad_s5_lambda_re': 'grad_w', 'grad_s5_lambda_im': 'grad_w', 'grad_s5_b_re': 'grad_w', 'grad_s5_b_im': 'grad_w', 'grad_s5_c_re': 'grad_w', 'grad_s5_c_im': 'grad_w', 'grad_s5_d': 'grad_w', 'grad_s5_log_dt': 'grad_w', 'grad_s5_glu_w': 'grad_w', 'grad_s5_glu_b': 'grad_w', 'grad_c_conv_w': 'grad_w', 'grad_d_conv_w': 'grad_w', 'grad_d_a_log': 'grad_w', 'grad_d_dt_bias': 'grad_w', 'grad_d_norm_g': 'grad_w', 'grad_w_out': 'grad_w', 'grad_final_g': 'grad_w', 'delta_norm_g': 'delta_w', 'delta_w_in': 'delta_w', 'delta_a_conv_w': 'delta_w', 'delta_a_conv_b': 'delta_w', 'delta_a_ln_g': 'delta_w', 'delta_a_ln_b': 'delta_w', 'delta_a_pw_w': 'delta_w', 'delta_a_pw_b': 'delta_w', 'delta_s5_lambda_re': 'delta_w', 'delta_s5_lambda_im': 'delta_w', 'delta_s5_b_re': 'delta_w', 'delta_s5_b_im': 'delta_w', 'delta_s5_c_re': 'delta_w', 'delta_s5_c_im': 'delta_w', 'delta_s5_d': 'delta_w', 'delta_s5_log_dt': 'delta_w', 'delta_s5_glu_w': 'delta_w', 'delta_s5_glu_b': 'delta_w', 'delta_c_conv_w': 'delta_w', 'delta_d_conv_w': 'delta_w', 'delta_d_a_log': 'delta_w', 'delta_d_dt_bias': 'delta_w', 'delta_d_norm_g': 'delta_w', 'delta_w_out': 'delta_w', 'delta_final_g': 'delta_w', 'new_m_norm_g': 'new_m', 'new_m_w_in': 'new_m', 'new_m_a_conv_w': 'new_m', 'new_m_a_conv_b': 'new_m', 'new_m_a_ln_g': 'new_m', 'new_m_a_ln_b': 'new_m', 'new_m_a_pw_w': 'new_m', 'new_m_a_pw_b': 'new_m', 'new_m_s5_lambda_re': 'new_m', 'new_m_s5_lambda_im': 'new_m', 'new_m_s5_b_re': 'new_m', 'new_m_s5_b_im': 'new_m', 'new_m_s5_c_re': 'new_m', 'new_m_s5_c_im': 'new_m', 'new_m_s5_d': 'new_m', 'new_m_s5_log_dt': 'new_m', 'new_m_s5_glu_w': 'new_m', 'new_m_s5_glu_b': 'new_m', 'new_m_c_conv_w': 'new_m', 'new_m_d_conv_w': 'new_m', 'new_m_d_a_log': 'new_m', 'new_m_d_dt_bias': 'new_m', 'new_m_d_norm_g': 'new_m', 'new_m_w_out': 'new_m', 'new_m_final_g': 'new_m', 'new_v_norm_g': 'new_v', 'new_v_w_in': 'new_v', 'new_v_a_conv_w': 'new_v', 'new_v_a_conv_b': 'new_v', 'new_v_a_ln_g': 'new_v', 'new_v_a_ln_b': 'new_v', 'new_v_a_pw_w': 'new_v', 'new_v_a_pw_b': 'new_v', 'new_v_s5_lambda_re': 'new_v', 'new_v_s5_lambda_im': 'new_v', 'new_v_s5_b_re': 'new_v', 'new_v_s5_b_im': 'new_v', 'new_v_s5_c_re': 'new_v', 'new_v_s5_c_im': 'new_v', 'new_v_s5_d': 'new_v', 'new_v_s5_log_dt': 'new_v', 'new_v_s5_glu_w': 'new_v', 'new_v_s5_glu_b': 'new_v', 'new_v_c_conv_w': 'new_v', 'new_v_d_conv_w': 'new_v', 'new_v_d_a_log': 'new_v', 'new_v_d_dt_bias': 'new_v', 'new_v_d_norm_g': 'new_v', 'new_v_w_out': 'new_v', 'new_v_final_g': 'new_v'}


def _forward(args):
    return _fwd_reference(*[args[k] for k in FWD_PARAMS])


def _output_shape():
    def fwd():
        inp = _fwd_setup_inputs(0)
        return _fwd_reference(*[inp[k] for k in FWD_PARAMS])
    out = _jax.eval_shape(fwd)
    return out.shape, out.dtype

N_MICROBATCH = 1
ADAM_LR = 0.001
ADAM_B1 = 0.9
ADAM_B2 = 0.999
ADAM_EPS = 1e-08
ADAM_WD = 0.01
ADAM_STEP = 10
PER_EXAMPLE_BATCH_AXIS = {'x': 0, 'loss_target': 0}
SHARED_INPUTS = []
_WEIGHT_DTYPES = {'norm_g': _jnp.float32, 'w_in': _jnp.float32, 'a_conv_w': _jnp.float32, 'a_conv_b': _jnp.float32, 'a_ln_g': _jnp.float32, 'a_ln_b': _jnp.float32, 'a_pw_w': _jnp.float32, 'a_pw_b': _jnp.float32, 's5_lambda_re': _jnp.float32, 's5_lambda_im': _jnp.float32, 's5_b_re': _jnp.float32, 's5_b_im': _jnp.float32, 's5_c_re': _jnp.float32, 's5_c_im': _jnp.float32, 's5_d': _jnp.float32, 's5_log_dt': _jnp.float32, 's5_glu_w': _jnp.float32, 's5_glu_b': _jnp.float32, 'c_conv_w': _jnp.float32, 'd_conv_w': _jnp.float32, 'd_a_log': _jnp.float32, 'd_dt_bias': _jnp.float32, 'd_norm_g': _jnp.float32, 'w_out': _jnp.float32, 'final_g': _jnp.float32}
MOMENT_SCALE = {'norm_g': 1.664648e-01, 'w_in': 9.195575e-02, 'a_conv_w': 7.065971e-02, 'a_conv_b': 1.606763e-01, 'a_ln_g': 8.288498e-02, 'a_ln_b': 7.227824e-02, 'a_pw_w': 6.843780e-02, 'a_pw_b': 1.168931e-01, 's5_lambda_re': 1.726862e-03, 's5_lambda_im': 1.598592e-03, 's5_b_re': 1.191989e-03, 's5_b_im': 1.195595e-03, 's5_c_re': 2.344893e-03, 's5_c_im': 2.377052e-03, 's5_d': 4.101119e-02, 's5_log_dt': 1.184542e+00, 's5_glu_w': 2.982314e-03, 's5_glu_b': 9.998052e-03, 'c_conv_w': 1.127775e-01, 'd_conv_w': 1.087398e-01, 'd_a_log': 4.305783e-01, 'd_dt_bias': 4.025380e-01, 'd_norm_g': 2.193750e-01, 'w_out': 8.623871e-02, 'final_g': 3.201784e+01}


def _to_microbatches(a, axis):
    t = _jnp.moveaxis(a, axis, 0)
    t = t.reshape((N_MICROBATCH, t.shape[0] // N_MICROBATCH) + t.shape[1:])
    return _jnp.moveaxis(t, 1, axis + 1)


def setup_inputs(seed: int = 0) -> dict:
    inp = _fwd_setup_inputs(seed)
    key = _jax.random.fold_in(_jax.random.key(seed), 7919)
    shape, _ = _output_shape()
    out = dict(inp)
    out["loss_target"] = _jax.random.normal(_jax.random.fold_in(key, 0), shape, _jnp.float32)
    for i, name in enumerate(TWIN_WEIGHTS):
        w = inp[name].astype(_jnp.float32)
        if MOMENT_SCALE is None:
            s = _jnp.sqrt(_jnp.mean(_jnp.square(w)) + 1e-30)
        else:
            s = MOMENT_SCALE[name]
        km, kv = _jax.random.split(_jax.random.fold_in(key, i + 1))
        out[name] = w
        out["m_" + name] = s * _jax.random.normal(km, w.shape, _jnp.float32)
        out["v_" + name] = (s * s) * _jax.random.uniform(kv, w.shape, _jnp.float32, 0.5, 1.5)
    if N_MICROBATCH > 1:
        for name, axis in PER_EXAMPLE_BATCH_AXIS.items():
            out[name] = _to_microbatches(out[name], axis)
    return {'x': out['x'], 'norm_g': out['norm_g'], 'w_in': out['w_in'], 'a_conv_w': out['a_conv_w'], 'a_conv_b': out['a_conv_b'], 'a_ln_g': out['a_ln_g'], 'a_ln_b': out['a_ln_b'], 'a_pw_w': out['a_pw_w'], 'a_pw_b': out['a_pw_b'], 's5_lambda_re': out['s5_lambda_re'], 's5_lambda_im': out['s5_lambda_im'], 's5_b_re': out['s5_b_re'], 's5_b_im': out['s5_b_im'], 's5_c_re': out['s5_c_re'], 's5_c_im': out['s5_c_im'], 's5_d': out['s5_d'], 's5_log_dt': out['s5_log_dt'], 's5_glu_w': out['s5_glu_w'], 's5_glu_b': out['s5_glu_b'], 'c_conv_w': out['c_conv_w'], 'd_conv_w': out['d_conv_w'], 'd_a_log': out['d_a_log'], 'd_dt_bias': out['d_dt_bias'], 'd_norm_g': out['d_norm_g'], 'w_out': out['w_out'], 'final_g': out['final_g'], 'loss_target': out['loss_target'], 'm_norm_g': out['m_norm_g'], 'm_w_in': out['m_w_in'], 'm_a_conv_w': out['m_a_conv_w'], 'm_a_conv_b': out['m_a_conv_b'], 'm_a_ln_g': out['m_a_ln_g'], 'm_a_ln_b': out['m_a_ln_b'], 'm_a_pw_w': out['m_a_pw_w'], 'm_a_pw_b': out['m_a_pw_b'], 'm_s5_lambda_re': out['m_s5_lambda_re'], 'm_s5_lambda_im': out['m_s5_lambda_im'], 'm_s5_b_re': out['m_s5_b_re'], 'm_s5_b_im': out['m_s5_b_im'], 'm_s5_c_re': out['m_s5_c_re'], 'm_s5_c_im': out['m_s5_c_im'], 'm_s5_d': out['m_s5_d'], 'm_s5_log_dt': out['m_s5_log_dt'], 'm_s5_glu_w': out['m_s5_glu_w'], 'm_s5_glu_b': out['m_s5_glu_b'], 'm_c_conv_w': out['m_c_conv_w'], 'm_d_conv_w': out['m_d_conv_w'], 'm_d_a_log': out['m_d_a_log'], 'm_d_dt_bias': out['m_d_dt_bias'], 'm_d_norm_g': out['m_d_norm_g'], 'm_w_out': out['m_w_out'], 'm_final_g': out['m_final_g'], 'v_norm_g': out['v_norm_g'], 'v_w_in': out['v_w_in'], 'v_a_conv_w': out['v_a_conv_w'], 'v_a_conv_b': out['v_a_conv_b'], 'v_a_ln_g': out['v_a_ln_g'], 'v_a_ln_b': out['v_a_ln_b'], 'v_a_pw_w': out['v_a_pw_w'], 'v_a_pw_b': out['v_a_pw_b'], 'v_s5_lambda_re': out['v_s5_lambda_re'], 'v_s5_lambda_im': out['v_s5_lambda_im'], 'v_s5_b_re': out['v_s5_b_re'], 'v_s5_b_im': out['v_s5_b_im'], 'v_s5_c_re': out['v_s5_c_re'], 'v_s5_c_im': out['v_s5_c_im'], 'v_s5_d': out['v_s5_d'], 'v_s5_log_dt': out['v_s5_log_dt'], 'v_s5_glu_w': out['v_s5_glu_w'], 'v_s5_glu_b': out['v_s5_glu_b'], 'v_c_conv_w': out['v_c_conv_w'], 'v_d_conv_w': out['v_d_conv_w'], 'v_d_a_log': out['v_d_a_log'], 'v_d_dt_bias': out['v_d_dt_bias'], 'v_d_norm_g': out['v_d_norm_g'], 'v_w_out': out['v_w_out'], 'v_final_g': out['v_final_g']}


def _loss(weights, diff, rest, loss_target):
    with _jax.named_scope("forward"):
        args = {**rest, TWIN_DIFF_INPUT: diff, **{k: w.astype(_WEIGHT_DTYPES[k]) for k, w in weights.items()}}
        y = _forward(args)
    with _jax.named_scope("loss_head"):
        err = _jnp.square(y.astype(_jnp.float32) - loss_target)
        return 0.5 * _jnp.sum(_jnp.mean(err, axis=-1)) if err.ndim else 0.5 * err


def _adamw(w, g, m, v):
    m = ADAM_B1 * m + (1.0 - ADAM_B1) * g
    v = ADAM_B2 * v + (1.0 - ADAM_B2) * _jnp.square(g)
    m_hat = m / (1.0 - ADAM_B1 ** ADAM_STEP)
    v_hat = v / (1.0 - ADAM_B2 ** ADAM_STEP)
    delta = -ADAM_LR * (m_hat / (_jnp.sqrt(v_hat) + ADAM_EPS) + ADAM_WD * w)
    return delta, m, v


def reference(x, norm_g, w_in, a_conv_w, a_conv_b, a_ln_g, a_ln_b, a_pw_w, a_pw_b, s5_lambda_re, s5_lambda_im, s5_b_re, s5_b_im, s5_c_re, s5_c_im, s5_d, s5_log_dt, s5_glu_w, s5_glu_b, c_conv_w, d_conv_w, d_a_log, d_dt_bias, d_norm_g, w_out, final_g, loss_target, m_norm_g, m_w_in, m_a_conv_w, m_a_conv_b, m_a_ln_g, m_a_ln_b, m_a_pw_w, m_a_pw_b, m_s5_lambda_re, m_s5_lambda_im, m_s5_b_re, m_s5_b_im, m_s5_c_re, m_s5_c_im, m_s5_d, m_s5_log_dt, m_s5_glu_w, m_s5_glu_b, m_c_conv_w, m_d_conv_w, m_d_a_log, m_d_dt_bias, m_d_norm_g, m_w_out, m_final_g, v_norm_g, v_w_in, v_a_conv_w, v_a_conv_b, v_a_ln_g, v_a_ln_b, v_a_pw_w, v_a_pw_b, v_s5_lambda_re, v_s5_lambda_im, v_s5_b_re, v_s5_b_im, v_s5_c_re, v_s5_c_im, v_s5_d, v_s5_log_dt, v_s5_glu_w, v_s5_glu_b, v_c_conv_w, v_d_conv_w, v_d_a_log, v_d_dt_bias, v_d_norm_g, v_w_out, v_final_g):
    given = dict(x=x, norm_g=norm_g, w_in=w_in, a_conv_w=a_conv_w, a_conv_b=a_conv_b, a_ln_g=a_ln_g, a_ln_b=a_ln_b, a_pw_w=a_pw_w, a_pw_b=a_pw_b, s5_lambda_re=s5_lambda_re, s5_lambda_im=s5_lambda_im, s5_b_re=s5_b_re, s5_b_im=s5_b_im, s5_c_re=s5_c_re, s5_c_im=s5_c_im, s5_d=s5_d, s5_log_dt=s5_log_dt, s5_glu_w=s5_glu_w, s5_glu_b=s5_glu_b, c_conv_w=c_conv_w, d_conv_w=d_conv_w, d_a_log=d_a_log, d_dt_bias=d_dt_bias, d_norm_g=d_norm_g, w_out=w_out, final_g=final_g, loss_target=loss_target, m_norm_g=m_norm_g, m_w_in=m_w_in, m_a_conv_w=m_a_conv_w, m_a_conv_b=m_a_conv_b, m_a_ln_g=m_a_ln_g, m_a_ln_b=m_a_ln_b, m_a_pw_w=m_a_pw_w, m_a_pw_b=m_a_pw_b, m_s5_lambda_re=m_s5_lambda_re, m_s5_lambda_im=m_s5_lambda_im, m_s5_b_re=m_s5_b_re, m_s5_b_im=m_s5_b_im, m_s5_c_re=m_s5_c_re, m_s5_c_im=m_s5_c_im, m_s5_d=m_s5_d, m_s5_log_dt=m_s5_log_dt, m_s5_glu_w=m_s5_glu_w, m_s5_glu_b=m_s5_glu_b, m_c_conv_w=m_c_conv_w, m_d_conv_w=m_d_conv_w, m_d_a_log=m_d_a_log, m_d_dt_bias=m_d_dt_bias, m_d_norm_g=m_d_norm_g, m_w_out=m_w_out, m_final_g=m_final_g, v_norm_g=v_norm_g, v_w_in=v_w_in, v_a_conv_w=v_a_conv_w, v_a_conv_b=v_a_conv_b, v_a_ln_g=v_a_ln_g, v_a_ln_b=v_a_ln_b, v_a_pw_w=v_a_pw_w, v_a_pw_b=v_a_pw_b, v_s5_lambda_re=v_s5_lambda_re, v_s5_lambda_im=v_s5_lambda_im, v_s5_b_re=v_s5_b_re, v_s5_b_im=v_s5_b_im, v_s5_c_re=v_s5_c_re, v_s5_c_im=v_s5_c_im, v_s5_d=v_s5_d, v_s5_log_dt=v_s5_log_dt, v_s5_glu_w=v_s5_glu_w, v_s5_glu_b=v_s5_glu_b, v_c_conv_w=v_c_conv_w, v_d_conv_w=v_d_conv_w, v_d_a_log=v_d_a_log, v_d_dt_bias=v_d_dt_bias, v_d_norm_g=v_d_norm_g, v_w_out=v_w_out, v_final_g=v_final_g)
    weights = {n: given[n] for n in TWIN_WEIGHTS}
    shared = {n: given[n] for n in SHARED_INPUTS}
    per_example = {n: given[n] for n in ['x']}
    grad_fn = _jax.value_and_grad(_loss, argnums=(0, 1))

    def one_microbatch(ex, loss_target):
        ex = dict(ex)
        diff = ex.pop(TWIN_DIFF_INPUT)
        return grad_fn(weights, diff, {**shared, **ex}, loss_target)

    if N_MICROBATCH == 1:
        loss, (grad_w, grad_x) = one_microbatch(per_example, given["loss_target"])
    else:
        def body(carry, xs):
            loss_sum, grad_sum = carry
            l_k, (gw_k, gx_k) = one_microbatch(xs[0], xs[1])
            with _jax.named_scope("update"):
                return (loss_sum + l_k, _jax.tree.map(_jnp.add, grad_sum, gw_k)), gx_k

        init = (_jnp.zeros((), _jnp.float32), _jax.tree.map(_jnp.zeros_like, weights))
        (loss, grad_w), grad_x = _jax.lax.scan(body, init, (per_example, given["loss_target"]))
    with _jax.named_scope("update"):
        delta_w, new_m, new_v = {}, {}, {}
        for n in TWIN_WEIGHTS:
            delta_w[n], new_m[n], new_v[n] = _adamw(weights[n], grad_w[n], given["m_" + n], given["v_" + n])
    return (loss, grad_x, *[grad_w[n] for n in TWIN_WEIGHTS], *[delta_w[n] for n in TWIN_WEIGHTS],
            *[new_m[n] for n in TWIN_WEIGHTS], *[new_v[n] for n in TWIN_WEIGHTS])
```

```python
import functools

import jax
import jax.numpy as jnp
from jax import lax
from jax.experimental import pallas as pl
from jax.experimental.pallas import tpu as pltpu

F32, BF = jnp.float32, jnp.bfloat16
HI = lax.Precision.HIGHEST
MESH = pl.DeviceIdType.MESH

D_MODEL = 1024
BR = 256
DEPTH = 4
N_IN = 3336
N_INP = 3456
COL_A, COL_B, COL_C, COL_D = 0, 768, 1280, 2304
W_A, W_B, W_C, W_D = 768, 512, 1024, 1152
S5_G, S5_H, S5_P = 16, 16, 64
NS = S5_G * S5_P
DN_H, DN_D, DN_C = 4, 64, 64
K_A, K_C, K_DN = 31, 3, 4
HALO_A, HALO_S = 32, 8
EPS = 1e-6
TL = 256
VMEM_LIMIT = 56 * 1024 * 1024

ADAM_LR, ADAM_B1, ADAM_B2, ADAM_EPS, ADAM_WD, ADAM_STEP = 0.001, 0.9, 0.999, 1e-08, 0.01, 10


def _cp(*sem):
    return pltpu.CompilerParams(dimension_semantics=sem, vmem_limit_bytes=VMEM_LIMIT)


def _sigmoid(x):
    return jax.nn.sigmoid(x)


def _silu(x):
    return x * jax.nn.sigmoid(x)


def _rmsnorm(x, g):
    return x * lax.rsqrt(jnp.mean(x * x, axis=-1, keepdims=True) + EPS) * g


@jax.custom_vjp
def _mm(a, w):
    return jnp.dot(a.astype(BF), w.astype(BF), preferred_element_type=F32)


def _mm_f(a, w):
    return _mm(a, w), (a, w)


def _mm_b(res, g):
    a, w = res
    gb = g.astype(BF)
    da = lax.dot_general(gb, w.astype(BF), (((1,), (1,)), ((), ())), preferred_element_type=F32)
    dw = lax.dot_general(a.astype(BF), gb, (((0,), (0,)), ((), ())), preferred_element_type=F32)
    return da, dw


_mm.defvjp(_mm_f, _mm_b)


@jax.custom_vjp
def _mm_nt(a, b):
    return lax.dot_general(a.astype(BF), b.astype(BF), (((1,), (1,)), ((), ())), preferred_element_type=F32)


def _mm_nt_f(a, b):
    return _mm_nt(a, b), (a, b)


def _mm_nt_b(res, g):
    a, b = res
    gb = g.astype(BF)
    da = jnp.dot(gb, b.astype(BF), preferred_element_type=F32)
    db = lax.dot_general(gb, a.astype(BF), (((0,), (0,)), ((), ())), preferred_element_type=F32)
    return da, db


_mm_nt.defvjp(_mm_nt_f, _mm_nt_b)


@jax.custom_vjp
def _mm_tn(a, b):
    return lax.dot_general(a.astype(BF), b.astype(BF), (((0,), (0,)), ((), ())), preferred_element_type=F32)


def _mm_tn_f(a, b):
    return _mm_tn(a, b), (a, b)


def _mm_tn_b(res, g):
    a, b = res
    gb = g.astype(BF)
    da = lax.dot_general(b.astype(BF), gb, (((1,), (1,)), ((), ())), preferred_element_type=F32)
    db = jnp.dot(a.astype(BF), gb, preferred_element_type=F32)
    return da, db


_mm_tn.defvjp(_mm_tn_f, _mm_tn_b)


def _dot_hi(a, b):
    return jnp.dot(a, b, precision=HI, preferred_element_type=F32)


@jax.custom_vjp
def _unit_lower_inv(lm):
    n = lm.shape[0]
    row = lax.broadcasted_iota(jnp.int32, lm.shape, 0)
    col = lax.broadcasted_iota(jnp.int32, lm.shape, 1)
    eye = (row == col).astype(F32)
    acc = eye - lm
    pw = lm
    k = 2
    while k < n:
        pw = _dot_hi(pw, pw)
        acc = acc + _dot_hi(acc, pw)
        k *= 2
    return acc


def _uli_f(lm):
    a = _unit_lower_inv(lm)
    return a, a


def _uli_b(a, g):
    at = a.T
    return (-_dot_hi(_dot_hi(at, g), at),)


_unit_lower_inv.defvjp(_uli_f, _uli_b)


def _roll(x, s):
    n = x.shape[0]
    s = s % n
    return x if s == 0 else pltpu.roll(x, s, 0)


def _conv_taps(ext, w_ref, halo, k_taps, tl):
    acc = None
    for k in range(k_taps):
        term = _roll(ext, (k_taps - 1) - k)[halo:halo + tl] * w_ref[k:k + 1, :]
        acc = term if acc is None else acc + term
    return acc


def _conv_taps_bwd(ext, w_ref, dw_ref, dacc, halo, k_taps, tl):
    dpad = jnp.concatenate([dacc, jnp.zeros((halo, dacc.shape[1]), F32)], axis=0)
    dext = None
    for k in range(k_taps):
        r = _roll(ext, (k_taps - 1) - k)[halo:halo + tl]
        dw_ref[k:k + 1, :] += jnp.sum(r * dacc, axis=0, keepdims=True)
        term = _roll(dpad, halo - (k_taps - 1) + k) * w_ref[k:k + 1, :]
        dext = term if dext is None else dext + term
    return dext


def _add_tail(x, tail):
    tl, h = x.shape[0], tail.shape[0]
    return x + jnp.concatenate([jnp.zeros((tl - h, x.shape[1]), F32), tail], axis=0)


def _proj_fwd(x, g, wp):
    L = x.shape[0]

    def body(x_ref, g_ref, w_ref, p_ref, h_ref):
        hb = _rmsnorm(x_ref[...], g_ref[...]).astype(BF)
        h_ref[...] = hb
        p_ref[...] = jnp.dot(hb, w_ref[...], preferred_element_type=F32)

    return pl.pallas_call(
        body, grid=(L // TL,),
        in_specs=[pl.BlockSpec((TL, D_MODEL), lambda i: (i, 0)),
                  pl.BlockSpec((1, D_MODEL), lambda i: (0, 0)),
                  pl.BlockSpec((D_MODEL, N_INP), lambda i: (0, 0))],
        out_specs=[pl.BlockSpec((TL, N_INP), lambda i: (i, 0)),
                   pl.BlockSpec((TL, D_MODEL), lambda i: (i, 0))],
        out_shape=[jax.ShapeDtypeStruct((L, N_INP), F32), jax.ShapeDtypeStruct((L, D_MODEL), BF)],
        name="proj_fwd", compiler_params=_cp("parallel"))(x, g, wp)


def _proj_bwd_x(x, g, dpa, dpb, dpc, dpd, wp, dxo):
    L = x.shape[0]

    def body(x_ref, g_ref, a_ref, b_ref, c_ref, d_ref, w_ref, dxo_ref, dx_ref, dg_ref):
        dh = None
        for ref, c0, wd in ((a_ref, COL_A, W_A), (b_ref, COL_B, W_B), (c_ref, COL_C, W_C), (d_ref, COL_D, W_D)):
            t = lax.dot_general(ref[...].astype(BF), w_ref[:, c0:c0 + wd], (((1,), (1,)), ((), ())),
                                preferred_element_type=F32)
            dh = t if dh is None else dh + t
        _, vj = jax.vjp(_rmsnorm, x_ref[...], g_ref[...])
        dx, dg = vj(dh)
        dx_ref[...] = dxo_ref[...] + dx

        @pl.when(pl.program_id(0) == 0)
        def _():
            dg_ref[...] = jnp.zeros_like(dg_ref)

        dg_ref[...] += dg

    def rows(wd):
        return pl.BlockSpec((TL, wd), lambda i: (i, 0))

    return pl.pallas_call(
        body, grid=(L // TL,),
        in_specs=[rows(D_MODEL), pl.BlockSpec((1, D_MODEL), lambda i: (0, 0)),
                  rows(W_A), rows(W_B), rows(W_C), rows(W_D),
                  pl.BlockSpec((D_MODEL, N_INP), lambda i: (0, 0)), rows(D_MODEL)],
        out_specs=[rows(D_MODEL), pl.BlockSpec((1, D_MODEL), lambda i: (0, 0))],
        out_shape=[jax.ShapeDtypeStruct((L, D_MODEL), F32), jax.ShapeDtypeStruct((1, D_MODEL), F32)],
        name="proj_bwd_x", compiler_params=_cp("arbitrary"))(x, g, dpa, dpb, dpc, dpd, wp, dxo)


def _mm_tn_acc(a, b, tn, name):
    L, ka = a.shape
    nb = b.shape[1]
    tk = min(512, L)

    def body(a_ref, b_ref, o_ref):
        @pl.when(pl.program_id(1) == 0)
        def _():
            o_ref[...] = jnp.zeros_like(o_ref)

        o_ref[...] += lax.dot_general(a_ref[...].astype(BF), b_ref[...].astype(BF), (((0,), (0,)), ((), ())),
                                      preferred_element_type=F32)

    return pl.pallas_call(
        body, grid=(nb // tn, L // tk),
        in_specs=[pl.BlockSpec((tk, ka), lambda j, t: (t, 0)), pl.BlockSpec((tk, tn), lambda j, t: (t, j))],
        out_specs=pl.BlockSpec((ka, tn), lambda j, t: (0, j)),
        out_shape=jax.ShapeDtypeStruct((ka, nb), F32),
        name=name, compiler_params=_cp("parallel", "arbitrary"))(a, b)


def _out_fwd(x, ya, yb, yc, yd, wo):
    L = x.shape[0]

    def body(x_ref, a_ref, b_ref, c_ref, d_ref, w_ref, o_ref):
        acc = x_ref[...]
        for j, ref in enumerate((a_ref, b_ref, c_ref, d_ref)):
            acc = acc + jnp.dot(ref[...].astype(BF), w_ref[j * BR:(j + 1) * BR, :], preferred_element_type=F32)
        o_ref[...] = acc

    def rows(wd):
        return pl.BlockSpec((TL, wd), lambda i: (i, 0))

    return pl.pallas_call(
        body, grid=(L // TL,),
        in_specs=[rows(D_MODEL), rows(BR), rows(BR), rows(BR), rows(BR),
                  pl.BlockSpec((D_MODEL, D_MODEL), lambda i: (0, 0))],
        out_specs=rows(D_MODEL), out_shape=jax.ShapeDtypeStruct((L, D_MODEL), F32),
        name="out_fwd", compiler_params=_cp("parallel"))(x, ya, yb, yc, yd, wo)


def _out_bwd_x(dxo, wo):
    L = dxo.shape[0]

    def body(d_ref, w_ref, o_ref):
        o_ref[...] = lax.dot_general(d_ref[...].astype(BF), w_ref[...], (((1,), (1,)), ((), ())),
                                     preferred_element_type=F32)

    return pl.pallas_call(
        body, grid=(L // TL,),
        in_specs=[pl.BlockSpec((TL, D_MODEL), lambda i: (i, 0)), pl.BlockSpec((D_MODEL, D_MODEL), lambda i: (0, 0))],
        out_specs=pl.BlockSpec((TL, D_MODEL), lambda i: (i, 0)),
        out_shape=jax.ShapeDtypeStruct((L, D_MODEL), F32),
        name="out_bwd_x", compiler_params=_cp("parallel"))(dxo, wo)


def _loss_bwd(x, g, tgt):
    L = x.shape[0]

    def f(xv, gv, tv):
        err = _rmsnorm(xv, gv) - tv
        return 0.5 * jnp.sum(jnp.mean(err * err, axis=-1, keepdims=True), axis=0, keepdims=True)

    def body(x_ref, g_ref, t_ref, loss_ref, dx_ref, dg_ref):
        tv = t_ref[...]
        loss, vj = jax.vjp(lambda a, b: f(a, b, tv), x_ref[...], g_ref[...])
        dx, dg = vj(jnp.ones((1, 1), F32))
        dx_ref[...] = dx

        @pl.when(pl.program_id(0) == 0)
        def _():
            dg_ref[...] = jnp.zeros_like(dg_ref)
            loss_ref[...] = jnp.zeros_like(loss_ref)

        dg_ref[...] += dg
        loss_ref[...] += jnp.broadcast_to(loss, loss_ref.shape)

    return pl.pallas_call(
        body, grid=(L // TL,),
        in_specs=[pl.BlockSpec((TL, D_MODEL), lambda i: (i, 0)), pl.BlockSpec((1, D_MODEL), lambda i: (0, 0)),
                  pl.BlockSpec((TL, D_MODEL), lambda i: (i, 0))],
        out_specs=[pl.BlockSpec((8, 128), lambda i: (0, 0)), pl.BlockSpec((TL, D_MODEL), lambda i: (i, 0)),
                   pl.BlockSpec((1, D_MODEL), lambda i: (0, 0))],
        out_shape=[jax.ShapeDtypeStruct((8, 128), F32), jax.ShapeDtypeStruct((L, D_MODEL), F32),
                   jax.ShapeDtypeStruct((1, D_MODEL), F32)],
        name="loss_bwd", compiler_params=_cp("arbitrary"))(x, g, tgt)


def _a_pre(val, gate):
    return val * _sigmoid(gate)


def _a_post(acc, az, cb, lng, lnb, pw, pwb):
    t = acc + cb
    mu = jnp.mean(t, axis=-1, keepdims=True)
    xc = t - mu
    ln = xc * lax.rsqrt(jnp.mean(xc * xc, axis=-1, keepdims=True) + EPS) * lng + lnb
    return (_mm(_silu(ln), pw) + pwb) * _silu(az)


def _halo_map(tl, halo, col):
    r = tl // halo
    return lambda i: (jnp.maximum(i * r - 1, 0), col)


def _a_fwd(proj, cw, cb, lng, lnb, pw, pwb):
    L = proj.shape[0]

    def body(vg_ref, az_ref, hvg_ref, cw_ref, cb_ref, lng_ref, lnb_ref, pw_ref, pwb_ref, o_ref):
        keep = (pl.program_id(0) > 0).astype(F32)
        a_h = _a_pre(hvg_ref[:, 0:BR], hvg_ref[:, BR:2 * BR]) * keep
        a_t = _a_pre(vg_ref[:, 0:BR], vg_ref[:, BR:2 * BR])
        ext = jnp.concatenate([a_h, a_t], axis=0)
        acc = _conv_taps(ext, cw_ref, HALO_A, K_A, TL)
        o_ref[...] = _a_post(acc, az_ref[...], cb_ref[...], lng_ref[...], lnb_ref[...], pw_ref[...], pwb_ref[...])

    vec = pl.BlockSpec((1, BR), lambda i: (0, 0))
    return pl.pallas_call(
        body, grid=(L // TL,),
        in_specs=[pl.BlockSpec((TL, 2 * BR), lambda i: (i, 0)), pl.BlockSpec((TL, BR), lambda i: (i, 2)),
                  pl.BlockSpec((HALO_A, 2 * BR), _halo_map(TL, HALO_A, 0)),
                  pl.BlockSpec((HALO_A, BR), lambda i: (0, 0)), vec, vec, vec,
                  pl.BlockSpec((BR, BR), lambda i: (0, 0)), vec],
        out_specs=pl.BlockSpec((TL, BR), lambda i: (i, 0)),
        out_shape=jax.ShapeDtypeStruct((L, BR), F32),
        name="a_fwd", compiler_params=_cp("parallel"))(proj, proj, proj, cw, cb, lng, lnb, pw, pwb)


def _a_bwd(proj, dmix, cw, cb, lng, lnb, pw, pwb):
    L = proj.shape[0]
    n = L // TL

    def body(vg_ref, az_ref, hvg_ref, dy_ref, cw_ref, cb_ref, lng_ref, lnb_ref, pw_ref, pwb_ref,
             dp_ref, dcw_ref, dcb_ref, dlng_ref, dlnb_ref, dpw_ref, dpwb_ref, carry_ref):
        i = pl.program_id(0)

        @pl.when(i == 0)
        def _():
            carry_ref[...] = jnp.zeros_like(carry_ref)
            for r in (dcw_ref, dcb_ref, dlng_ref, dlnb_ref, dpw_ref, dpwb_ref):
                r[...] = jnp.zeros_like(r)

        keep = (i < n - 1).astype(F32)
        val, gate = vg_ref[:, 0:BR], vg_ref[:, BR:2 * BR]
        a_h = _a_pre(hvg_ref[:, 0:BR], hvg_ref[:, BR:2 * BR]) * keep
        a_t, vj_pre = jax.vjp(_a_pre, val, gate)
        ext = jnp.concatenate([a_h, a_t], axis=0)
        acc = _conv_taps(ext, cw_ref, HALO_A, K_A, TL)
        _, vj_post = jax.vjp(_a_post, acc, az_ref[...], cb_ref[...], lng_ref[...], lnb_ref[...], pw_ref[...],
                             pwb_ref[...])
        dacc, daz, dcb, dlng, dlnb, dpw, dpwb = vj_post(dy_ref[...])
        dext = _conv_taps_bwd(ext, cw_ref, dcw_ref, dacc, HALO_A, K_A, TL)
        da = _add_tail(dext[HALO_A:], carry_ref[...])
        carry_ref[...] = dext[:HALO_A]
        dval, dgate = vj_pre(da)
        dp_ref[:, 0:BR] = dval
        dp_ref[:, BR:2 * BR] = dgate
        dp_ref[:, 2 * BR:3 * BR] = daz
        dcb_ref[...] += dcb
        dlng_ref[...] += dlng
        dlnb_ref[...] += dlnb
        dpw_ref[...] += dpw
        dpwb_ref[...] += dpwb

    rev = lambda i: n - 1 - i
    vec = pl.BlockSpec((1, BR), lambda i: (0, 0))
    hmap = _halo_map(TL, HALO_A, 0)
    return pl.pallas_call(
        body, grid=(n,),
        in_specs=[pl.BlockSpec((TL, 2 * BR), lambda i: (rev(i), 0)), pl.BlockSpec((TL, BR), lambda i: (rev(i), 2)),
                  pl.BlockSpec((HALO_A, 2 * BR), lambda i: hmap(rev(i))),
                  pl.BlockSpec((TL, BR), lambda i: (rev(i), 0)),
                  pl.BlockSpec((HALO_A, BR), lambda i: (0, 0)), vec, vec, vec,
                  pl.BlockSpec((BR, BR), lambda i: (0, 0)), vec],
        out_specs=[pl.BlockSpec((TL, W_A), lambda i: (rev(i), 0)),
                   pl.BlockSpec((HALO_A, BR), lambda i: (0, 0)), vec, vec, vec,
                   pl.BlockSpec((BR, BR), lambda i: (0, 0)), vec],
        out_shape=[jax.ShapeDtypeStruct((L, W_A), F32), jax.ShapeDtypeStruct((HALO_A, BR), F32)]
        + [jax.ShapeDtypeStruct((1, BR), F32)] * 3
        + [jax.ShapeDtypeStruct((BR, BR), F32), jax.ShapeDtypeStruct((1, BR), F32)],
        scratch_shapes=[pltpu.VMEM((HALO_A, BR), F32)],
        name="a_bwd", compiler_params=_cp("arbitrary"))(proj, proj, proj, dmix, cw, cb, lng, lnb, pw, pwb)


def _c_pre(cg, xc):
    return cg * xc


def _c_post(acc, bg, cz):
    return bg * acc * _silu(cz)


def _c_fwd(proj, cw):
    L = proj.shape[0]

    def body(bg_ref, cx_ref, cz_ref, hcx_ref, cw_ref, o_ref):
        keep = (pl.program_id(0) > 0).astype(F32)
        p_h = _c_pre(hcx_ref[:, 0:BR], hcx_ref[:, BR:2 * BR]) * keep
        p_t = _c_pre(cx_ref[:, 0:BR], cx_ref[:, BR:2 * BR])
        ext = jnp.concatenate([p_h, p_t], axis=0)
        acc = _conv_taps(ext, cw_ref, HALO_S, K_C, TL)
        o_ref[...] = _c_post(acc, bg_ref[...], cz_ref[...])

    return pl.pallas_call(
        body, grid=(L // TL,),
        in_specs=[pl.BlockSpec((TL, BR), lambda i: (i, 5)), pl.BlockSpec((TL, 2 * BR), lambda i: (i, 3)),
                  pl.BlockSpec((TL, BR), lambda i: (i, 8)),
                  pl.BlockSpec((HALO_S, 2 * BR), _halo_map(TL, HALO_S, 3)),
                  pl.BlockSpec((HALO_S, BR), lambda i: (0, 0))],
        out_specs=pl.BlockSpec((TL, BR), lambda i: (i, 0)),
        out_shape=jax.ShapeDtypeStruct((L, BR), F32),
        name="c_fwd", compiler_params=_cp("parallel"))(proj, proj, proj, proj, cw)


def _c_bwd(proj, dmix, cw):
    L = proj.shape[0]
    n = L // TL

    def body(bg_ref, cx_ref, cz_ref, hcx_ref, dy_ref, cw_ref, dp_ref, dcw_ref, carry_ref):
        i = pl.program_id(0)

        @pl.when(i == 0)
        def _():
            carry_ref[...] = jnp.zeros_like(carry_ref)
            dcw_ref[...] = jnp.zeros_like(dcw_ref)

        keep = (i < n - 1).astype(F32)
        p_h = _c_pre(hcx_ref[:, 0:BR], hcx_ref[:, BR:2 * BR]) * keep
        p_t, vj_pre = jax.vjp(_c_pre, cx_ref[:, 0:BR], cx_ref[:, BR:2 * BR])
        ext = jnp.concatenate([p_h, p_t], axis=0)
        acc = _conv_taps(ext, cw_ref, HALO_S, K_C, TL)
        _, vj_post = jax.vjp(_c_post, acc, bg_ref[...], cz_ref[...])
        dacc, dbg, dcz = vj_post(dy_ref[...])
        dext = _conv_taps_bwd(ext, cw_ref, dcw_ref, dacc, HALO_S, K_C, TL)
        dp = _add_tail(dext[HALO_S:], carry_ref[...])
        carry_ref[...] = dext[:HALO_S]
        dcg, dxc = vj_pre(dp)
        dp_ref[:, 0:BR] = dbg
        dp_ref[:, BR:2 * BR] = dcg
        dp_ref[:, 2 * BR:3 * BR] = dxc
        dp_ref[:, 3 * BR:4 * BR] = dcz

    rev = lambda i: n - 1 - i
    hmap = _halo_map(TL, HALO_S, 3)
    return pl.pallas_call(
        body, grid=(n,),
        in_specs=[pl.BlockSpec((TL, BR), lambda i: (rev(i), 5)), pl.BlockSpec((TL, 2 * BR), lambda i: (rev(i), 3)),
                  pl.BlockSpec((TL, BR), lambda i: (rev(i), 8)),
                  pl.BlockSpec((HALO_S, 2 * BR), lambda i: hmap(rev(i))),
                  pl.BlockSpec((TL, BR), lambda i: (rev(i), 2)),
                  pl.BlockSpec((HALO_S, BR), lambda i: (0, 0))],
        out_specs=[pl.BlockSpec((TL, W_C), lambda i: (rev(i), 0)), pl.BlockSpec((HALO_S, BR), lambda i: (0, 0))],
        out_shape=[jax.ShapeDtypeStruct((L, W_C), F32), jax.ShapeDtypeStruct((HALO_S, BR), F32)],
        scratch_shapes=[pltpu.VMEM((HALO_S, BR), F32)],
        name="c_bwd", compiler_params=_cp("arbitrary"))(proj, proj, proj, proj, dmix, cw)


def _s5_prep_fn(lre, lim, ldt, bre, bim, cre, cim):
    grp = lax.broadcasted_iota(jnp.int32, (128, NS), 0)
    lane = lax.broadcasted_iota(jnp.int32, (128, NS), 1)
    expand = (grp == lane // S5_P).astype(F32)
    dt = jnp.exp(_dot_hi(jnp.broadcast_to(ldt, (8, 128)), expand)[0:1])
    lr = jnp.minimum(lre, -1e-4)
    mag = jnp.exp(lr * dt)
    ar = mag * jnp.cos(lim * dt)
    ai = mag * jnp.sin(lim * dt)
    den = lr * lr + lim * lim
    fr = ((ar - 1.0) * lr + ai * lim) / den
    fi = (ai * lr - (ar - 1.0) * lim) / den
    bbr = fr * bre - fi * bim
    bbi = fr * bim + fi * bre
    row = lax.broadcasted_iota(jnp.int32, (BR, NS), 0)
    col = lax.broadcasted_iota(jnp.int32, (BR, NS), 1)
    blk = (row // S5_H == col // S5_P).astype(F32)

    def embed(t):
        return jnp.concatenate([t] * S5_G, axis=0) * blk

    bemb = jnp.concatenate([embed(bbr), embed(bbi)], axis=1)
    cemb = jnp.concatenate([embed(cre), embed(-cim)], axis=1)
    return ar, ai, bemb, cemb


def _s5_prep(lre, lim, ldt, bre, bim, cre, cim):
    def body(*refs):
        outs = _s5_prep_fn(*[r[...] for r in refs[:7]])
        for r, o in zip(refs[7:], outs):
            r[...] = o

    return pl.pallas_call(
        body,
        out_shape=[jax.ShapeDtypeStruct((1, NS), F32)] * 2 + [jax.ShapeDtypeStruct((BR, 2 * NS), F32)] * 2,
        name="s5_prep", compiler_params=pltpu.CompilerParams(vmem_limit_bytes=VMEM_LIMIT),
    )(lre, lim, ldt, bre, bim, cre, cim)


def _s5_prep_bwd(lre, lim, ldt, bre, bim, cre, cim, dar, dai, dbemb, dcemb):
    def body(*refs):
        _, vj = jax.vjp(_s5_prep_fn, *[r[...] for r in refs[:7]])
        grads = vj(tuple(r[...] for r in refs[7:11]))
        for r, o in zip(refs[11:], grads):
            r[...] = o

    return pl.pallas_call(
        body,
        out_shape=[jax.ShapeDtypeStruct((1, NS), F32)] * 2 + [jax.ShapeDtypeStruct((1, 128), F32)]
        + [jax.ShapeDtypeStruct((S5_H, NS), F32)] * 4,
        name="s5_prep_bwd", compiler_params=pltpu.CompilerParams(vmem_limit_bytes=VMEM_LIMIT),
    )(lre, lim, ldt, bre, bim, cre, cim, dar, dai, dbemb, dcemb)


def _s5_scan(xr, xi, ar, ai, reverse):
    n = xr.shape[0]
    row = lax.broadcasted_iota(jnp.int32, (n, 1), 0)
    pr, pi = ar, ai
    d = 1
    while d < n:
        if reverse:
            m = row < n - d
            sr = jnp.where(m, _roll(xr, n - d), 0.0)
            si = jnp.where(m, _roll(xi, n - d), 0.0)
        else:
            m = row >= d
            sr = jnp.where(m, _roll(xr, d), 0.0)
            si = jnp.where(m, _roll(xi, d), 0.0)
        xr, xi = xr + pr * sr - pi * si, xi + pr * si + pi * sr
        pr, pi = pr * pr - pi * pi, 2.0 * pr * pi
        d *= 2
    return xr, xi


def _s5_states(u, bemb_b, ar, ai, sin_r, sin_i):
    bu = jnp.dot(u.astype(BF), bemb_b, preferred_element_type=F32)
    first = lax.broadcasted_iota(jnp.int32, (u.shape[0], 1), 0) == 0
    xr = bu[:, :NS] + jnp.where(first, ar * sin_r - ai * sin_i, 0.0)
    xi = bu[:, NS:] + jnp.where(first, ar * sin_i + ai * sin_r, 0.0)
    return _s5_scan(xr, xi, ar, ai, False)


def _b_post(yssm, u, bz, dsk, gw, gb):
    z = jax.nn.gelu(yssm + dsk * u)
    return z * _sigmoid(_mm(z, gw) + gb) * _silu(bz)


def _b_fwd(proj, ar, ai, bemb, cemb, dsk, gw, gb):
    L = proj.shape[0]
    n = L // TL

    def body(u_ref, bz_ref, ar_ref, ai_ref, be_ref, ce_ref, dsk_ref, gw_ref, gb_ref, o_ref, sin_ref, carry_ref):
        @pl.when(pl.program_id(0) == 0)
        def _():
            carry_ref[...] = jnp.zeros_like(carry_ref)

        sin = carry_ref[...]
        sin_ref[0] = sin
        u = u_ref[...]
        sr, si = _s5_states(u, be_ref[...].astype(BF), ar_ref[...], ai_ref[...], sin[:, :NS], sin[:, NS:])
        carry_ref[:, :NS] = sr[TL - 1:TL]
        carry_ref[:, NS:] = si[TL - 1:TL]
        s = jnp.concatenate([sr, si], axis=1).astype(BF)
        yssm = lax.dot_general(s, ce_ref[...].astype(BF), (((1,), (1,)), ((), ())), preferred_element_type=F32)
        o_ref[...] = _b_post(yssm, u, bz_ref[...], dsk_ref[...], gw_ref[...], gb_ref[...])

    vec = pl.BlockSpec((1, BR), lambda i: (0, 0))
    svec = pl.BlockSpec((1, NS), lambda i: (0, 0))
    emb = pl.BlockSpec((BR, 2 * NS), lambda i: (0, 0))
    return pl.pallas_call(
        body, grid=(n,),
        in_specs=[pl.BlockSpec((TL, BR), lambda i: (i, 3)), pl.BlockSpec((TL, BR), lambda i: (i, 4)),
                  svec, svec, emb, emb, vec, pl.BlockSpec((BR, BR), lambda i: (0, 0)), vec],
        out_specs=[pl.BlockSpec((TL, BR), lambda i: (i, 0)), pl.BlockSpec((1, 1, 2 * NS), lambda i: (i, 0, 0))],
        out_shape=[jax.ShapeDtypeStruct((L, BR), F32), jax.ShapeDtypeStruct((n, 1, 2 * NS), F32)],
        scratch_shapes=[pltpu.VMEM((1, 2 * NS), F32)],
        name="b_fwd", compiler_params=_cp("arbitrary"))(proj, proj, ar, ai, bemb, cemb, dsk, gw, gb)


def _b_bwd(proj, dmix, sin_all, ar, ai, bemb, cemb, dsk, gw, gb):
    L = proj.shape[0]
    n = L // TL

    def body(u_ref, bz_ref, dy_ref, sin_ref, ar_ref, ai_ref, be_ref, ce_ref, dsk_ref, gw_ref, gb_ref,
             dp_ref, dar_ref, dai_ref, dbe_ref, dce_ref, ddsk_ref, dgw_ref, dgb_ref, carry_ref):
        i = pl.program_id(0)

        @pl.when(i == 0)
        def _():
            carry_ref[...] = jnp.zeros_like(carry_ref)
            for r in (dar_ref, dai_ref, dbe_ref, dce_ref, ddsk_ref, dgw_ref, dgb_ref):
                r[...] = jnp.zeros_like(r)

        u = u_ref[...]
        ar, ai = ar_ref[...], ai_ref[...]
        be_b, ce_b = be_ref[...].astype(BF), ce_ref[...].astype(BF)
        sin = sin_ref[0]
        sr, si = _s5_states(u, be_b, ar, ai, sin[:, :NS], sin[:, NS:])
        s_b = jnp.concatenate([sr, si], axis=1).astype(BF)
        yssm = lax.dot_general(s_b, ce_b, (((1,), (1,)), ((), ())), preferred_element_type=F32)
        _, vj = jax.vjp(_b_post, yssm, u, bz_ref[...], dsk_ref[...], gw_ref[...], gb_ref[...])
        dyssm, du, dbz, ddsk, dgw, dgb = vj(dy_ref[...])
        dy_b = dyssm.astype(BF)
        dce_ref[...] += lax.dot_general(dy_b, s_b, (((0,), (0,)), ((), ())), preferred_element_type=F32)
        gs = jnp.dot(dy_b, ce_b, preferred_element_type=F32)
        last = lax.broadcasted_iota(jnp.int32, (TL, 1), 0) == TL - 1
        cr, ci = carry_ref[:, :NS], carry_ref[:, NS:]
        gr = gs[:, :NS] + jnp.where(last, ar * cr + ai * ci, 0.0)
        gi = gs[:, NS:] + jnp.where(last, ar * ci - ai * cr, 0.0)
        dsr, dsi = _s5_scan(gr, gi, ar, -ai, True)
        carry_ref[:, :NS] = dsr[0:1]
        carry_ref[:, NS:] = dsi[0:1]
        first = lax.broadcasted_iota(jnp.int32, (TL, 1), 0) == 0
        pr = jnp.where(first, sin[:, :NS], _roll(sr, 1))
        pi = jnp.where(first, sin[:, NS:], _roll(si, 1))
        dar_ref[...] += jnp.sum(dsr * pr + dsi * pi, axis=0, keepdims=True)
        dai_ref[...] += jnp.sum(dsi * pr - dsr * pi, axis=0, keepdims=True)
        ds_b = jnp.concatenate([dsr, dsi], axis=1).astype(BF)
        dbe_ref[...] += lax.dot_general(u.astype(BF), ds_b, (((0,), (0,)), ((), ())), preferred_element_type=F32)
        du = du + lax.dot_general(ds_b, be_b, (((1,), (1,)), ((), ())), preferred_element_type=F32)
        dp_ref[:, 0:BR] = du
        dp_ref[:, BR:2 * BR] = dbz
        ddsk_ref[...] += ddsk
        dgw_ref[...] += dgw
        dgb_ref[...] += dgb

    rev = lambda i: n - 1 - i
    vec = pl.BlockSpec((1, BR), lambda i: (0, 0))
    svec = pl.BlockSpec((1, NS), lambda i: (0, 0))
    emb = pl.BlockSpec((BR, 2 * NS), lambda i: (0, 0))
    mat = pl.BlockSpec((BR, BR), lambda i: (0, 0))
    return pl.pallas_call(
        body, grid=(n,),
        in_specs=[pl.BlockSpec((TL, BR), lambda i: (rev(i), 3)), pl.BlockSpec((TL, BR), lambda i: (rev(i), 4)),
                  pl.BlockSpec((TL, BR), lambda i: (rev(i), 1)),
                  pl.BlockSpec((1, 1, 2 * NS), lambda i: (rev(i), 0, 0)),
                  svec, svec, emb, emb, vec, mat, vec],
        out_specs=[pl.BlockSpec((TL, W_B), lambda i: (rev(i), 0)), svec, svec, emb, emb, vec, mat, vec],
        out_shape=[jax.ShapeDtypeStruct((L, W_B), F32)] + [jax.ShapeDtypeStruct((1, NS), F32)] * 2
        + [jax.ShapeDtypeStruct((BR, 2 * NS), F32)] * 2
        + [jax.ShapeDtypeStruct((1, BR), F32), jax.ShapeDtypeStruct((BR, BR), F32), jax.ShapeDtypeStruct((1, BR), F32)],
        scratch_shapes=[pltpu.VMEM((1, 2 * NS), F32)],
        name="b_bwd", compiler_params=_cp("arbitrary"))(proj, proj, dmix, sin_all, ar, ai, bemb, cemb, dsk, gw, gb)


def _d_post(cq, ab, dz, s0, s1, s2, s3, p1, p2, ng):
    c = cq.shape[0]
    qkv = _silu(cq)
    gall = -jnp.exp(p1) * jax.nn.softplus(ab + p2)
    ball = _sigmoid(ab)
    row = lax.broadcasted_iota(jnp.int32, (c, c), 0)
    col = lax.broadcasted_iota(jnp.int32, (c, c), 1)
    causal = row >= col
    strict = row > col
    gc_all = _dot_hi(causal.astype(F32), gall)
    gc_t = gc_all.T
    outs, news = [], []
    for h, s in enumerate((s0, s1, s2, s3)):
        q = qkv[:, h * DN_D:(h + 1) * DN_D]
        k = qkv[:, BR + h * DN_D:BR + (h + 1) * DN_D]
        v = qkv[:, 2 * BR + h * DN_D:2 * BR + (h + 1) * DN_D]
        q = q * lax.rsqrt(jnp.sum(q * q, axis=-1, keepdims=True) + EPS) * (DN_D ** -0.5)
        k = k * lax.rsqrt(jnp.sum(k * k, axis=-1, keepdims=True) + EPS)
        gcol = gc_all[:, h:h + 1]
        grow = gc_t[h:h + 1, :]
        beta = ball[:, DN_H + h:DN_H + h + 1]
        decay = jnp.where(causal, jnp.exp(jnp.where(causal, gcol - grow, 0.0)), 0.0)
        kb = k * beta
        lm = jnp.where(strict, _mm_nt(kb, k) * decay, 0.0)
        ainv = _unit_lower_inv(lm)
        egc = jnp.exp(gcol)
        uu = _dot_hi(ainv, v * beta)
        ww = _dot_hi(ainv, kb * egc)
        attn = _mm_nt(q, k) * decay
        glast = gcol[c - 1:c, :]
        kd = k * jnp.exp(glast - gcol)
        vnew = uu - _mm(ww, s)
        o = _mm(q * egc, s) + _mm(attn, vnew)
        news.append(s * jnp.exp(glast) + _mm_tn(kd, vnew))
        outs.append(o * lax.rsqrt(jnp.mean(o * o, axis=-1, keepdims=True) + EPS) * ng)
    yd = jnp.concatenate(outs, axis=1) * _silu(dz)
    return (yd, *news)


def _d_fwd(proj, cw, p1, p2, ng):
    L = proj.shape[0]
    n = L // DN_C

    def body(qkv_ref, ab_ref, dz_ref, hq_ref, cw_ref, p1_ref, p2_ref, ng_ref, o_ref, sall_ref, s_ref):
        i = pl.program_id(0)

        @pl.when(i == 0)
        def _():
            s_ref[...] = jnp.zeros_like(s_ref)

        keep = (i > 0).astype(F32)
        ext = jnp.concatenate([hq_ref[...] * keep, qkv_ref[...]], axis=0)
        cq = _conv_taps(ext, cw_ref, HALO_S, K_DN, DN_C)
        st = [s_ref[h] for h in range(DN_H)]
        for h in range(DN_H):
            sall_ref[0, h] = st[h]
        out = _d_post(cq, ab_ref[...], dz_ref[...], *st, p1_ref[...], p2_ref[...], ng_ref[...])
        o_ref[...] = out[0]
        for h in range(DN_H):
            s_ref[h] = out[1 + h]

    return pl.pallas_call(
        body, grid=(n,),
        in_specs=[pl.BlockSpec((DN_C, 3 * BR), lambda i: (i, 3)), pl.BlockSpec((DN_C, 128), lambda i: (i, 26)),
                  pl.BlockSpec((DN_C, BR), lambda i: (i, 12)),
                  pl.BlockSpec((HALO_S, 3 * BR), _halo_map(DN_C, HALO_S, 3)),
                  pl.BlockSpec((HALO_S, 3 * BR), lambda i: (0, 0)),
                  pl.BlockSpec((1, 128), lambda i: (0, 0)), pl.BlockSpec((1, 128), lambda i: (0, 0)),
                  pl.BlockSpec((1, DN_D), lambda i: (0, 0))],
        out_specs=[pl.BlockSpec((DN_C, BR), lambda i: (i, 0)),
                   pl.BlockSpec((1, DN_H, DN_D, DN_D), lambda i: (i, 0, 0, 0))],
        out_shape=[jax.ShapeDtypeStruct((L, BR), F32), jax.ShapeDtypeStruct((n, DN_H, DN_D, DN_D), F32)],
        scratch_shapes=[pltpu.VMEM((DN_H, DN_D, DN_D), F32)],
        name="d_fwd", compiler_params=_cp("arbitrary"))(proj, proj, proj, proj, cw, p1, p2, ng)


def _d_bwd(proj, dmix, sall, cw, p1, p2, ng):
    L = proj.shape[0]
    n = L // DN_C

    def body(qkv_ref, ab_ref, dz_ref, hq_ref, dy_ref, sall_ref, cw_ref, p1_ref, p2_ref, ng_ref,
             dp_ref, dcw_ref, dp1_ref, dp2_ref, dng_ref, ds_ref, carry_ref):
        i = pl.program_id(0)

        @pl.when(i == 0)
        def _():
            ds_ref[...] = jnp.zeros_like(ds_ref)
            carry_ref[...] = jnp.zeros_like(carry_ref)
            for r in (dcw_ref, dp1_ref, dp2_ref, dng_ref):
                r[...] = jnp.zeros_like(r)

        keep = (i < n - 1).astype(F32)
        ext = jnp.concatenate([hq_ref[...] * keep, qkv_ref[...]], axis=0)
        cq = _conv_taps(ext, cw_ref, HALO_S, K_DN, DN_C)
        st = [sall_ref[0, h] for h in range(DN_H)]
        _, vj = jax.vjp(_d_post, cq, ab_ref[...], dz_ref[...], *st, p1_ref[...], p2_ref[...], ng_ref[...])
        grads = vj((dy_ref[...], *[ds_ref[h] for h in range(DN_H)]))
        dcq, dab, ddz = grads[0], grads[1], grads[2]
        for h in range(DN_H):
            ds_ref[h] = grads[3 + h]
        dp1_ref[...] += grads[7]
        dp2_ref[...] += grads[8]
        dng_ref[...] += grads[9]
        dext = _conv_taps_bwd(ext, cw_ref, dcw_ref, dcq, HALO_S, K_DN, DN_C)
        dp_ref[:, 0:3 * BR] = _add_tail(dext[HALO_S:], carry_ref[...])
        carry_ref[...] = dext[:HALO_S]
        dp_ref[:, 3 * BR:4 * BR] = ddz
        dp_ref[:, 4 * BR:4 * BR + 128] = dab

    rev = lambda i: n - 1 - i
    hmap = _halo_map(DN_C, HALO_S, 3)
    v128 = pl.BlockSpec((1, 128), lambda i: (0, 0))
    return pl.pallas_call(
        body, grid=(n,),
        in_specs=[pl.BlockSpec((DN_C, 3 * BR), lambda i: (rev(i), 3)), pl.BlockSpec((DN_C, 128), lambda i: (rev(i), 26)),
                  pl.BlockSpec((DN_C, BR), lambda i: (rev(i), 12)),
                  pl.BlockSpec((HALO_S, 3 * BR), lambda i: hmap(rev(i))),
                  pl.BlockSpec((DN_C, BR), lambda i: (rev(i), 3)),
                  pl.BlockSpec((1, DN_H, DN_D, DN_D), lambda i: (rev(i), 0, 0, 0)),
                  pl.BlockSpec((HALO_S, 3 * BR), lambda i: (0, 0)), v128, v128,
                  pl.BlockSpec((1, DN_D), lambda i: (0, 0))],
        out_specs=[pl.BlockSpec((DN_C, W_D), lambda i: (rev(i), 0)),
                   pl.BlockSpec((HALO_S, 3 * BR), lambda i: (0, 0)), v128, v128,
                   pl.BlockSpec((1, DN_D), lambda i: (0, 0))],
        out_shape=[jax.ShapeDtypeStruct((L, W_D), F32), jax.ShapeDtypeStruct((HALO_S, 3 * BR), F32),
                   jax.ShapeDtypeStruct((1, 128), F32), jax.ShapeDtypeStruct((1, 128), F32),
                   jax.ShapeDtypeStruct((1, DN_D), F32)],
        scratch_shapes=[pltpu.VMEM((DN_H, DN_D, DN_D), F32), pltpu.VMEM((HALO_S, 3 * BR), F32)],
        name="d_bwd", compiler_params=_cp("arbitrary"))(proj, proj, proj, proj, dmix, sall, cw, p1, p2, ng)


def _pick_rows(rows, cap):
    best = 8
    for t in range(8, cap + 1, 8):
        if rows % t == 0:
            best = t
    return best


def _adamw(w, g, m, v, name):
    rows, wd = w.shape
    tr = _pick_rows(rows, 512)
    c1 = 1.0 - ADAM_B1 ** ADAM_STEP
    c2 = 1.0 - ADAM_B2 ** ADAM_STEP

    def body(w_ref, g_ref, m_ref, v_ref, d_ref, mo_ref, vo_ref):
        gv = g_ref[...]
        mn = ADAM_B1 * m_ref[...] + (1.0 - ADAM_B1) * gv
        vn = ADAM_B2 * v_ref[...] + (1.0 - ADAM_B2) * (gv * gv)
        d_ref[...] = -ADAM_LR * ((mn / c1) / (jnp.sqrt(vn / c2) + ADAM_EPS) + ADAM_WD * w_ref[...])
        mo_ref[...] = mn
        vo_ref[...] = vn

    spec = pl.BlockSpec((tr, wd), lambda i: (i, 0))
    return pl.pallas_call(
        body, grid=(rows // tr,), in_specs=[spec] * 4, out_specs=[spec] * 3,
        out_shape=[jax.ShapeDtypeStruct((rows, wd), F32)] * 3,
        name=name, compiler_params=_cp("parallel"))(w, g, m, v)


def _sum_slots(r, name):
    n, rows, wd = r.shape
    tr = _pick_rows(rows, 384)

    def body(r_ref, o_ref):
        acc = r_ref[0]
        for j in range(1, n):
            acc = acc + r_ref[j]
        o_ref[...] = acc

    return pl.pallas_call(
        body, grid=(rows // tr,),
        in_specs=[pl.BlockSpec((n, tr, wd), lambda i: (0, i, 0))],
        out_specs=pl.BlockSpec((tr, wd), lambda i: (i, 0)),
        out_shape=jax.ShapeDtypeStruct((rows, wd), F32),
        name=name, compiler_params=_cp("parallel"))(r)


AXES = ("x", "y", "c")


def _group_peer(axes, k):
    pos = {a: lax.axis_index(a) for a in AXES}
    idx = 0
    for a in axes:
        idx = idx * 2 + pos[a]
    peer = dict(pos)
    for b, a in enumerate(reversed(axes)):
        if (k >> b) & 1:
            peer[a] = 1 - pos[a]
    return idx, tuple(peer[a] for a in AXES)


def _all_gather(x, axes, name):
    n = 2 ** len(axes)

    def body(x_ref, o_ref, send_sems, recv_sems, local_sem):
        idx, _ = _group_peer(axes, 0)
        mine = pltpu.make_async_copy(x_ref, o_ref.at[idx], local_sem)
        mine.start()
        copies = []
        for k in range(1, n):
            _, peer = _group_peer(axes, k)
            cp = pltpu.make_async_remote_copy(src_ref=x_ref, dst_ref=o_ref.at[idx], send_sem=send_sems.at[k],
                                              recv_sem=recv_sems.at[k], device_id=peer, device_id_type=MESH)
            cp.start()
            copies.append(cp)
        for cp in copies:
            cp.wait_send()
        for k in range(1, n):
            pltpu.make_async_remote_copy(src_ref=x_ref, dst_ref=o_ref.at[idx], send_sem=send_sems.at[k],
                                         recv_sem=recv_sems.at[k], device_id=_group_peer(axes, k)[1],
                                         device_id_type=MESH).wait_recv()
        mine.wait()

    return pl.pallas_call(
        body, out_shape=jax.ShapeDtypeStruct((n,) + x.shape, x.dtype),
        in_specs=[pl.BlockSpec(memory_space=pl.ANY)], out_specs=pl.BlockSpec(memory_space=pl.ANY),
        scratch_shapes=[pltpu.SemaphoreType.DMA((n,)), pltpu.SemaphoreType.DMA((n,)), pltpu.SemaphoreType.DMA],
        name=name)(x)


def _all_to_all(x, name):
    n = 8

    def body(x_ref, o_ref, send_sems, recv_sems, local_sem):
        me, _ = _group_peer(AXES, 0)
        mine = pltpu.make_async_copy(x_ref.at[me], o_ref.at[me], local_sem)
        mine.start()
        copies = []
        for k in range(1, n):
            _, peer = _group_peer(AXES, k)
            cp = pltpu.make_async_remote_copy(src_ref=x_ref.at[me ^ k], dst_ref=o_ref.at[me],
                                              send_sem=send_sems.at[k], recv_sem=recv_sems.at[k],
                                              device_id=peer, device_id_type=MESH)
            cp.start()
            copies.append(cp)
        for cp in copies:
            cp.wait_send()
        for k in range(1, n):
            pltpu.make_async_remote_copy(src_ref=x_ref.at[me ^ k], dst_ref=o_ref.at[me],
                                         send_sem=send_sems.at[k], recv_sem=recv_sems.at[k],
                                         device_id=_group_peer(AXES, k)[1], device_id_type=MESH).wait_recv()
        mine.wait()

    return pl.pallas_call(
        body, out_shape=jax.ShapeDtypeStruct(x.shape, x.dtype),
        in_specs=[pl.BlockSpec(memory_space=pl.ANY)], out_specs=pl.BlockSpec(memory_space=pl.ANY),
        scratch_shapes=[pltpu.SemaphoreType.DMA((n,)), pltpu.SemaphoreType.DMA((n,)), pltpu.SemaphoreType.DMA],
        name=name)(x)


SHARDED = (("w_in", 2), ("a_conv_w", 2), ("a_pw_w", 1), ("s5_glu_w", 1), ("c_conv_w", 2), ("d_conv_w", 2),
           ("w_out", 1))
REPLICATED = ("norm_g", "a_conv_b", "a_ln_g", "a_ln_b", "a_pw_b", "s5_lambda_re", "s5_lambda_im", "s5_b_re",
              "s5_b_im", "s5_c_re", "s5_c_im", "s5_d", "s5_log_dt", "s5_glu_b", "d_a_log", "d_dt_bias",
              "d_norm_g", "final_g")
WEIGHTS = ("norm_g", "w_in", "a_conv_w", "a_conv_b", "a_ln_g", "a_ln_b", "a_pw_w", "a_pw_b", "s5_lambda_re",
           "s5_lambda_im", "s5_b_re", "s5_b_im", "s5_c_re", "s5_c_im", "s5_d", "s5_log_dt", "s5_glu_w",
           "s5_glu_b", "c_conv_w", "d_conv_w", "d_a_log", "d_dt_bias", "d_norm_g", "w_out", "final_g")
LANES = 1024


def _pack(arrs, rows):
    flat = jnp.concatenate([a.reshape(-1) for a in arrs])
    return jnp.pad(flat, (0, rows * LANES - flat.shape[0])).reshape(rows, LANES)


def _unpack(slab, shapes):
    flat = slab.reshape(-1)
    out, off = [], 0
    for s in shapes:
        size = 1
        for d in s:
            size *= d
        out.append(flat[off:off + size].reshape(s))
        off += size
    return out


def _rows_for(shapes, mult):
    total = 0
    for s in shapes:
        size = 1
        for d in s:
            size *= d
        total += size
    rows = -(-total // LANES)
    return -(-rows // mult) * mult


def _row(v, width=None):
    v = v.reshape(1, -1)
    return v if width is None else jnp.pad(v, ((0, 0), (0, width - v.shape[1])))


def _pad_rows(w, rows):
    return jnp.pad(w, ((0, rows - w.shape[0]), (0, 0)))


def _permute_in(w):
    return jnp.concatenate([w[:, :3072], w[:, 3080:N_IN], w[:, 3072:3080],
                            jnp.zeros((w.shape[0], N_INP - N_IN), w.dtype)], axis=1)


def _unpermute_in(dw):
    return jnp.concatenate([dw[:, :3072], dw[:, 3328:3336], dw[:, 3072:3328]], axis=1)


def _layer_fwd(x, p):
    proj, h = _proj_fwd(x, p["norm_g"], p["wp"])
    ya = _a_fwd(proj, p["a_cw"], p["a_cb"], p["a_lng"], p["a_lnb"], p["a_pw"], p["a_pwb"])
    ar, ai, bemb, cemb = _s5_prep(*p["s5"])
    yb, sin_all = _b_fwd(proj, ar, ai, bemb, cemb, p["s5_d"], p["glu_w"], p["glu_b"])
    yc = _c_fwd(proj, p["c_cw"])
    yd, sall = _d_fwd(proj, p["d_cw"], p["d_p1"], p["d_p2"], p["d_ng"])
    xo = _out_fwd(x, ya, yb, yc, yd, p["wo"])
    return xo, dict(x=x, proj=proj, h=h, ys=(ya, yb, yc, yd), sin_all=sin_all, sall=sall,
                    s5=(ar, ai, bemb, cemb))


def _layer_bwd(dxo, p, r):
    proj = r["proj"]
    ar, ai, bemb, cemb = r["s5"]
    dmix = _out_bwd_x(dxo, p["wo"])
    dwo = jnp.concatenate([_mm_tn_acc(y, dxo, 512, "dwout") for y in r["ys"]], axis=0)
    dpa, dcw_a, dcb, dlng, dlnb, dpw, dpwb = _a_bwd(proj, dmix, p["a_cw"], p["a_cb"], p["a_lng"], p["a_lnb"],
                                                     p["a_pw"], p["a_pwb"])
    dpb, dar, dai, dbe, dce, ddsk, dgw, dgb = _b_bwd(proj, dmix, r["sin_all"], ar, ai, bemb, cemb, p["s5_d"],
                                                     p["glu_w"], p["glu_b"])
    dlre, dlim, dldt, dbre, dbim, dcre, dcim = _s5_prep_bwd(*p["s5"], dar, dai, dbe, dce)
    dpc, dcw_c = _c_bwd(proj, dmix, p["c_cw"])
    dpd, dcw_d, dp1, dp2, dng = _d_bwd(proj, dmix, r["sall"], p["d_cw"], p["d_p1"], p["d_p2"], p["d_ng"])
    dx, dg = _proj_bwd_x(r["x"], p["norm_g"], dpa, dpb, dpc, dpd, p["wp"], dxo)
    dwp = jnp.concatenate([_mm_tn_acc(r["h"], dpa, 256, "dwin_a"), _mm_tn_acc(r["h"], dpb, 256, "dwin_b"),
                           _mm_tn_acc(r["h"], dpc, 256, "dwin_c"), _mm_tn_acc(r["h"], dpd, 384, "dwin_d")], axis=1)

    def unrows(t, perm):
        return jnp.transpose(t.reshape(S5_H, S5_G, S5_P), perm)

    grads = dict(
        norm_g=dg.reshape(-1), w_in=_unpermute_in(dwp), a_conv_w=dcw_a[:K_A], a_conv_b=dcb.reshape(-1),
        a_ln_g=dlng.reshape(-1), a_ln_b=dlnb.reshape(-1), a_pw_w=dpw, a_pw_b=dpwb.reshape(-1),
        s5_lambda_re=dlre.reshape(S5_G, S5_P), s5_lambda_im=dlim.reshape(S5_G, S5_P),
        s5_b_re=unrows(dbre, (1, 2, 0)), s5_b_im=unrows(dbim, (1, 2, 0)),
        s5_c_re=unrows(dcre, (1, 0, 2)), s5_c_im=unrows(dcim, (1, 0, 2)),
        s5_d=ddsk.reshape(-1), s5_log_dt=dldt[0, :S5_G], s5_glu_w=dgw, s5_glu_b=dgb.reshape(-1),
        c_conv_w=dcw_c[:K_C], d_conv_w=dcw_d[:K_DN], d_a_log=dp1[0, :DN_H], d_dt_bias=dp2[0, :DN_H],
        d_norm_g=dng.reshape(-1), w_out=dwo)
    return dx, grads


def _layer_params(full, l):
    return dict(
        norm_g=_row(full["norm_g"][l]), wp=_permute_in(full["w_in"][l]).astype(BF),
        a_cw=_pad_rows(full["a_conv_w"][l], HALO_A), a_cb=_row(full["a_conv_b"][l]),
        a_lng=_row(full["a_ln_g"][l]), a_lnb=_row(full["a_ln_b"][l]), a_pw=full["a_pw_w"][l],
        a_pwb=_row(full["a_pw_b"][l]),
        s5=(_row(full["s5_lambda_re"][l]), _row(full["s5_lambda_im"][l]), _row(full["s5_log_dt"][l], 128),
            jnp.transpose(full["s5_b_re"][l], (2, 0, 1)).reshape(S5_H, NS),
            jnp.transpose(full["s5_b_im"][l], (2, 0, 1)).reshape(S5_H, NS),
            jnp.transpose(full["s5_c_re"][l], (1, 0, 2)).reshape(S5_H, NS),
            jnp.transpose(full["s5_c_im"][l], (1, 0, 2)).reshape(S5_H, NS)),
        s5_d=_row(full["s5_d"][l]), glu_w=full["s5_glu_w"][l], glu_b=_row(full["s5_glu_b"][l]),
        c_cw=_pad_rows(full["c_conv_w"][l], HALO_S), d_cw=_pad_rows(full["d_conv_w"][l], HALO_S),
        d_p1=_row(full["d_a_log"][l], 128), d_p2=_row(full["d_dt_bias"][l], 128),
        d_ng=_row(full["d_norm_g"][l]), wo=full["w_out"][l].astype(BF))


def _local_step(xs, tgt, full):
    params = [_layer_params(full, l) for l in range(DEPTH)]
    saved = []
    h = xs
    for l in range(DEPTH):
        h, r = _layer_fwd(h, params[l])
        saved.append(r)
    loss_tile, dx, dfg = _loss_bwd(h, _row(full["final_g"]), tgt)
    layer_grads = [None] * DEPTH
    for l in reversed(range(DEPTH)):
        dx, layer_grads[l] = _layer_bwd(dx, params[l], saved[l])
    grads = {n: jnp.stack([layer_grads[l][n] for l in range(DEPTH)]) for n in WEIGHTS if n != "final_g"}
    grads["final_g"] = dfg.reshape(-1)
    return loss_tile, dx, grads


def kernel(x, norm_g, w_in, a_conv_w, a_conv_b, a_ln_g, a_ln_b, a_pw_w, a_pw_b, s5_lambda_re, s5_lambda_im, s5_b_re, s5_b_im, s5_c_re, s5_c_im, s5_d, s5_log_dt, s5_glu_w, s5_glu_b, c_conv_w, d_conv_w, d_a_log, d_dt_bias, d_norm_g, w_out, final_g, loss_target, m_norm_g, m_w_in, m_a_conv_w, m_a_conv_b, m_a_ln_g, m_a_ln_b, m_a_pw_w, m_a_pw_b, m_s5_lambda_re, m_s5_lambda_im, m_s5_b_re, m_s5_b_im, m_s5_c_re, m_s5_c_im, m_s5_d, m_s5_log_dt, m_s5_glu_w, m_s5_glu_b, m_c_conv_w, m_d_conv_w, m_d_a_log, m_d_dt_bias, m_d_norm_g, m_w_out, m_final_g, v_norm_g, v_w_in, v_a_conv_w, v_a_conv_b, v_a_ln_g, v_a_ln_b, v_a_pw_w, v_a_pw_b, v_s5_lambda_re, v_s5_lambda_im, v_s5_b_re, v_s5_b_im, v_s5_c_re, v_s5_c_im, v_s5_d, v_s5_log_dt, v_s5_glu_w, v_s5_glu_b, v_c_conv_w, v_d_conv_w, v_d_a_log, v_d_dt_bias, v_d_norm_g, v_w_out, v_final_g):
    given = dict(locals())
    w = {n: given[n] for n in WEIGHTS}
    m = {n: given["m_" + n] for n in WEIGHTS}
    v = {n: given["v_" + n] for n in WEIGHTS}
    xs, tgt = x[0], loss_target[0]

    sh_names = [n for n, _ in SHARDED]
    sh_shapes = [w[n].shape for n in sh_names]
    sh_rows = _rows_for(sh_shapes, 16)
    gathered = _all_gather(_pack([w[n] for n in sh_names], sh_rows), ("x", "y"), "gather_weights")
    full = dict(w)
    parts = [_unpack(gathered[j], sh_shapes) for j in range(4)]
    for i, (n, ax) in enumerate(SHARDED):
        full[n] = jnp.concatenate([parts[j][i] for j in range(4)], axis=ax)

    loss_tile, dx, grads = _local_step(xs, tgt, full)

    slots = []
    for j in range(4):
        sl = [lax.slice_in_dim(grads[n], j * w[n].shape[ax], (j + 1) * w[n].shape[ax], axis=ax) for n, ax in SHARDED]
        slots.append(_pack(sl, sh_rows))
    sh_slab = jnp.stack(slots).reshape(8, sh_rows // 2, LANES)
    half = _sum_slots(_all_to_all(sh_slab, "scatter_sharded"), "sum_sharded")
    g_sh = _all_gather(half, ("c",), "gather_halves").reshape(sh_rows, LANES)

    rp_shapes = [w[n].shape for n in REPLICATED] + [(1,)]
    rp_rows = _rows_for(rp_shapes, 64)
    rp_slab = _pack([grads[n] for n in REPLICATED] + [loss_tile[0, 0:1]], rp_rows).reshape(8, rp_rows // 8, LANES)
    part = _sum_slots(_all_to_all(rp_slab, "scatter_replicated"), "sum_replicated")
    g_rp = _all_gather(part, AXES, "gather_replicated").reshape(rp_rows, LANES)

    d_sh, m_sh, v_sh = _adamw(_pack([w[n] for n in sh_names], sh_rows), g_sh,
                              _pack([m[n] for n in sh_names], sh_rows), _pack([v[n] for n in sh_names], sh_rows),
                              "adamw_sharded")
    zero = jnp.zeros((1,), F32)
    d_rp, m_rp, v_rp = _adamw(_pack([w[n] for n in REPLICATED] + [zero], rp_rows), g_rp,
                              _pack([m[n] for n in REPLICATED] + [zero], rp_rows),
                              _pack([v[n] for n in REPLICATED] + [zero], rp_rows), "adamw_replicated")

    out = {}
    for key, sh, rp in (("grad", g_sh, g_rp), ("delta", d_sh, d_rp), ("new_m", m_sh, m_rp), ("new_v", v_sh, v_rp)):
        for n, t in zip(sh_names, _unpack(sh, sh_shapes)):
            out[key + "_" + n] = t
        for n, t in zip(REPLICATED, _unpack(rp, rp_shapes[:-1])):
            out[key + "_" + n] = t
    loss = _unpack(g_rp, rp_shapes)[-1].reshape(())
    return (loss, dx[None], *[out["grad_" + n] for n in WEIGHTS], *[out["delta_" + n] for n in WEIGHTS],
            *[out["new_m_" + n] for n in WEIGHTS], *[out["new_v_" + n] for n in WEIGHTS])
```

```python
import functools

import jax
import jax.numpy as jnp
from jax import lax
from jax.experimental import pallas as pl
from jax.experimental.pallas import tpu as pltpu

F32, BF = jnp.float32, jnp.bfloat16
HI = lax.Precision.HIGHEST
MESH = pl.DeviceIdType.MESH

D_MODEL = 1024
BR = 256
DEPTH = 4
N_IN = 3336
N_INP = 3456
COL_A, COL_B, COL_C, COL_D = 0, 768, 1280, 2304
W_A, W_B, W_C, W_D = 768, 512, 1024, 1152
S5_G, S5_H, S5_P = 16, 16, 64
NS = S5_G * S5_P
DN_H, DN_D, DN_C = 4, 64, 64
K_A, K_C, K_DN = 31, 3, 4
HALO_A, HALO_S = 32, 8
EPS = 1e-6
TL = 256
VMEM_LIMIT = 56 * 1024 * 1024

ADAM_LR, ADAM_B1, ADAM_B2, ADAM_EPS, ADAM_WD, ADAM_STEP = 0.001, 0.9, 0.999, 1e-08, 0.01, 10


def _cp(*sem):
    return pltpu.CompilerParams(dimension_semantics=sem, vmem_limit_bytes=VMEM_LIMIT)


def _sigmoid(x):
    return jax.nn.sigmoid(x)


def _silu(x):
    return x * jax.nn.sigmoid(x)


def _rmsnorm(x, g):
    return x * lax.rsqrt(jnp.mean(x * x, axis=-1, keepdims=True) + EPS) * g


@jax.custom_vjp
def _mm(a, w):
    return jnp.dot(a.astype(BF), w.astype(BF), preferred_element_type=F32)


def _mm_f(a, w):
    return _mm(a, w), (a, w)


def _mm_b(res, g):
    a, w = res
    gb = g.astype(BF)
    da = lax.dot_general(gb, w.astype(BF), (((1,), (1,)), ((), ())), preferred_element_type=F32)
    dw = lax.dot_general(a.astype(BF), gb, (((0,), (0,)), ((), ())), preferred_element_type=F32)
    return da, dw


_mm.defvjp(_mm_f, _mm_b)


@jax.custom_vjp
def _mm_nt(a, b):
    return lax.dot_general(a.astype(BF), b.astype(BF), (((1,), (1,)), ((), ())), preferred_element_type=F32)


def _mm_nt_f(a, b):
    return _mm_nt(a, b), (a, b)


def _mm_nt_b(res, g):
    a, b = res
    gb = g.astype(BF)
    da = jnp.dot(gb, b.astype(BF), preferred_element_type=F32)
    db = lax.dot_general(gb, a.astype(BF), (((0,), (0,)), ((), ())), preferred_element_type=F32)
    return da, db


_mm_nt.defvjp(_mm_nt_f, _mm_nt_b)


@jax.custom_vjp
def _mm_tn(a, b):
    return lax.dot_general(a.astype(BF), b.astype(BF), (((0,), (0,)), ((), ())), preferred_element_type=F32)


def _mm_tn_f(a, b):
    return _mm_tn(a, b), (a, b)


def _mm_tn_b(res, g):
    a, b = res
    gb = g.astype(BF)
    da = lax.dot_general(b.astype(BF), gb, (((1,), (1,)), ((), ())), preferred_element_type=F32)
    db = jnp.dot(a.astype(BF), gb, preferred_element_type=F32)
    return da, db


_mm_tn.defvjp(_mm_tn_f, _mm_tn_b)


def _dot_hi(a, b):
    return jnp.dot(a, b, precision=HI, preferred_element_type=F32)


def _split(a):
    hi = a.astype(BF)
    return hi, (a - hi.astype(F32)).astype(BF)


def _dot3(a, b, dims=(((1,), (0,)), ((), ()))):
    ah, al = _split(a)
    bh, bl = _split(b)
    d = functools.partial(lax.dot_general, dimension_numbers=dims, preferred_element_type=F32)
    return d(ah, bh) + d(ah, bl) + d(al, bh)


@jax.custom_vjp
def _mm3(a, b):
    return _dot3(a, b)


def _mm3_f(a, b):
    return _dot3(a, b), (a, b)


def _mm3_b(res, g):
    a, b = res
    return _dot3(g, b, (((1,), (1,)), ((), ()))), _dot3(a, g, (((0,), (0,)), ((), ())))


_mm3.defvjp(_mm3_f, _mm3_b)


@jax.custom_vjp
def _unit_lower_inv(lms):
    n = lms[0].shape[0]
    row = lax.broadcasted_iota(jnp.int32, lms[0].shape, 0)
    col = lax.broadcasted_iota(jnp.int32, lms[0].shape, 1)
    eye = (row == col).astype(F32)
    accs = [eye - lm for lm in lms]
    pws = list(lms)
    k = 2
    while k < n:
        pws = [_dot3(p, p) for p in pws]
        accs = [a + _dot3(a, p) for a, p in zip(accs, pws)]
        k *= 2
    return tuple(accs)


def _uli_f(lms):
    a = _unit_lower_inv(lms)
    return a, a


def _uli_b(a, g):
    ats = [x.T for x in a]
    tmp = [_dot3(at, gi) for at, gi in zip(ats, g)]
    return (tuple(-_dot3(t, at) for t, at in zip(tmp, ats)),)


_unit_lower_inv.defvjp(_uli_f, _uli_b)


def _roll(x, s):
    n = x.shape[0]
    s = s % n
    return x if s == 0 else pltpu.roll(x, s, 0)


def _conv_taps(ext, w_ref, halo, k_taps, tl):
    acc = None
    for k in range(k_taps):
        term = _roll(ext, (k_taps - 1) - k)[halo:halo + tl] * w_ref[k:k + 1, :]
        acc = term if acc is None else acc + term
    return acc


def _conv_taps_bwd(ext, w_ref, dw_ref, dacc, halo, k_taps, tl):
    dpad = jnp.concatenate([dacc, jnp.zeros((halo, dacc.shape[1]), F32)], axis=0)
    dext = None
    for k in range(k_taps):
        r = _roll(ext, (k_taps - 1) - k)[halo:halo + tl]
        dw_ref[k:k + 1, :] += jnp.sum(r * dacc, axis=0, keepdims=True)
        term = _roll(dpad, halo - (k_taps - 1) + k) * w_ref[k:k + 1, :]
        dext = term if dext is None else dext + term
    return dext


def _add_tail(x, tail):
    tl, h = x.shape[0], tail.shape[0]
    return x + jnp.concatenate([jnp.zeros((tl - h, x.shape[1]), F32), tail], axis=0)


def _proj_fwd(x, g, wp):
    L = x.shape[0]

    def body(x_ref, g_ref, w_ref, p_ref, h_ref):
        hb = _rmsnorm(x_ref[...], g_ref[...]).astype(BF)
        h_ref[...] = hb
        p_ref[...] = jnp.dot(hb, w_ref[...], preferred_element_type=F32)

    return pl.pallas_call(
        body, grid=(L // TL,),
        in_specs=[pl.BlockSpec((TL, D_MODEL), lambda i: (i, 0)),
                  pl.BlockSpec((1, D_MODEL), lambda i: (0, 0)),
                  pl.BlockSpec((D_MODEL, N_INP), lambda i: (0, 0))],
        out_specs=[pl.BlockSpec((TL, N_INP), lambda i: (i, 0)),
                   pl.BlockSpec((TL, D_MODEL), lambda i: (i, 0))],
        out_shape=[jax.ShapeDtypeStruct((L, N_INP), F32), jax.ShapeDtypeStruct((L, D_MODEL), BF)],
        name="proj_fwd", compiler_params=_cp("parallel"))(x, g, wp)


def _proj_bwd_x(x, g, dpa, dpb, dpc, dpd, wp, dxo):
    L = x.shape[0]

    def body(x_ref, g_ref, a_ref, b_ref, c_ref, d_ref, w_ref, dxo_ref, dx_ref, dg_ref):
        dh = None
        for ref, c0, wd in ((a_ref, COL_A, W_A), (b_ref, COL_B, W_B), (c_ref, COL_C, W_C), (d_ref, COL_D, W_D)):
            t = lax.dot_general(ref[...].astype(BF), w_ref[:, c0:c0 + wd], (((1,), (1,)), ((), ())),
                                preferred_element_type=F32)
            dh = t if dh is None else dh + t
        _, vj = jax.vjp(_rmsnorm, x_ref[...], g_ref[...])
        dx, dg = vj(dh)
        dx_ref[...] = dxo_ref[...] + dx

        @pl.when(pl.program_id(0) == 0)
        def _():
            dg_ref[...] = jnp.zeros_like(dg_ref)

        dg_ref[...] += dg

    def rows(wd):
        return pl.BlockSpec((TL, wd), lambda i: (i, 0))

    return pl.pallas_call(
        body, grid=(L // TL,),
        in_specs=[rows(D_MODEL), pl.BlockSpec((1, D_MODEL), lambda i: (0, 0)),
                  rows(W_A), rows(W_B), rows(W_C), rows(W_D),
                  pl.BlockSpec((D_MODEL, N_INP), lambda i: (0, 0)), rows(D_MODEL)],
        out_specs=[rows(D_MODEL), pl.BlockSpec((1, D_MODEL), lambda i: (0, 0))],
        out_shape=[jax.ShapeDtypeStruct((L, D_MODEL), F32), jax.ShapeDtypeStruct((1, D_MODEL), F32)],
        name="proj_bwd_x", compiler_params=_cp("arbitrary"))(x, g, dpa, dpb, dpc, dpd, wp, dxo)


def _mm_tn_acc(a, b, tn, name):
    L, ka = a.shape
    nb = b.shape[1]
    tk = min(512, L)

    def body(a_ref, b_ref, o_ref):
        @pl.when(pl.program_id(1) == 0)
        def _():
            o_ref[...] = jnp.zeros_like(o_ref)

        o_ref[...] += lax.dot_general(a_ref[...].astype(BF), b_ref[...].astype(BF), (((0,), (0,)), ((), ())),
                                      preferred_element_type=F32)

    return pl.pallas_call(
        body, grid=(nb // tn, L // tk),
        in_specs=[pl.BlockSpec((tk, ka), lambda j, t: (t, 0)), pl.BlockSpec((tk, tn), lambda j, t: (t, j))],
        out_specs=pl.BlockSpec((ka, tn), lambda j, t: (0, j)),
        out_shape=jax.ShapeDtypeStruct((ka, nb), F32),
        name=name, compiler_params=_cp("parallel", "arbitrary"))(a, b)


def _dwout(ya, yb, yc, yd, dxo):
    L = dxo.shape[0]
    tk, tn = min(512, L), 512

    def body(a_ref, b_ref, c_ref, d_ref, g_ref, o_ref):
        @pl.when(pl.program_id(1) == 0)
        def _():
            o_ref[...] = jnp.zeros_like(o_ref)

        gb = g_ref[...].astype(BF)
        for j, ref in enumerate((a_ref, b_ref, c_ref, d_ref)):
            o_ref[j * BR:(j + 1) * BR, :] += lax.dot_general(ref[...].astype(BF), gb, (((0,), (0,)), ((), ())),
                                                             preferred_element_type=F32)

    ys = pl.BlockSpec((tk, BR), lambda j, t: (t, 0))
    return pl.pallas_call(
        body, grid=(D_MODEL // tn, L // tk),
        in_specs=[ys, ys, ys, ys, pl.BlockSpec((tk, tn), lambda j, t: (t, j))],
        out_specs=pl.BlockSpec((D_MODEL, tn), lambda j, t: (0, j)),
        out_shape=jax.ShapeDtypeStruct((D_MODEL, D_MODEL), F32),
        name="dwout", compiler_params=_cp("parallel", "arbitrary"))(ya, yb, yc, yd, dxo)


def _out_fwd(x, ya, yb, yc, yd, wo):
    L = x.shape[0]

    def body(x_ref, a_ref, b_ref, c_ref, d_ref, w_ref, o_ref):
        acc = x_ref[...]
        for j, ref in enumerate((a_ref, b_ref, c_ref, d_ref)):
            acc = acc + jnp.dot(ref[...].astype(BF), w_ref[j * BR:(j + 1) * BR, :], preferred_element_type=F32)
        o_ref[...] = acc

    def rows(wd):
        return pl.BlockSpec((TL, wd), lambda i: (i, 0))

    return pl.pallas_call(
        body, grid=(L // TL,),
        in_specs=[rows(D_MODEL), rows(BR), rows(BR), rows(BR), rows(BR),
                  pl.BlockSpec((D_MODEL, D_MODEL), lambda i: (0, 0))],
        out_specs=rows(D_MODEL), out_shape=jax.ShapeDtypeStruct((L, D_MODEL), F32),
        name="out_fwd", compiler_params=_cp("parallel"))(x, ya, yb, yc, yd, wo)


def _out_bwd_x(dxo, wo):
    L = dxo.shape[0]

    def body(d_ref, w_ref, o_ref):
        o_ref[...] = lax.dot_general(d_ref[...].astype(BF), w_ref[...], (((1,), (1,)), ((), ())),
                                     preferred_element_type=F32)

    return pl.pallas_call(
        body, grid=(L // TL,),
        in_specs=[pl.BlockSpec((TL, D_MODEL), lambda i: (i, 0)), pl.BlockSpec((D_MODEL, D_MODEL), lambda i: (0, 0))],
        out_specs=pl.BlockSpec((TL, D_MODEL), lambda i: (i, 0)),
        out_shape=jax.ShapeDtypeStruct((L, D_MODEL), F32),
        name="out_bwd_x", compiler_params=_cp("parallel"))(dxo, wo)


def _loss_bwd(x, g, tgt):
    L = x.shape[0]

    def f(xv, gv, tv):
        err = _rmsnorm(xv, gv) - tv
        return 0.5 * jnp.sum(jnp.mean(err * err, axis=-1, keepdims=True), axis=0, keepdims=True)

    def body(x_ref, g_ref, t_ref, loss_ref, dx_ref, dg_ref):
        tv = t_ref[...]
        loss, vj = jax.vjp(lambda a, b: f(a, b, tv), x_ref[...], g_ref[...])
        dx, dg = vj(jnp.ones((1, 1), F32))
        dx_ref[...] = dx

        @pl.when(pl.program_id(0) == 0)
        def _():
            dg_ref[...] = jnp.zeros_like(dg_ref)
            loss_ref[...] = jnp.zeros_like(loss_ref)

        dg_ref[...] += dg
        loss_ref[...] += jnp.broadcast_to(loss, loss_ref.shape)

    return pl.pallas_call(
        body, grid=(L // TL,),
        in_specs=[pl.BlockSpec((TL, D_MODEL), lambda i: (i, 0)), pl.BlockSpec((1, D_MODEL), lambda i: (0, 0)),
                  pl.BlockSpec((TL, D_MODEL), lambda i: (i, 0))],
        out_specs=[pl.BlockSpec((8, 128), lambda i: (0, 0)), pl.BlockSpec((TL, D_MODEL), lambda i: (i, 0)),
                   pl.BlockSpec((1, D_MODEL), lambda i: (0, 0))],
        out_shape=[jax.ShapeDtypeStruct((8, 128), F32), jax.ShapeDtypeStruct((L, D_MODEL), F32),
                   jax.ShapeDtypeStruct((1, D_MODEL), F32)],
        name="loss_bwd", compiler_params=_cp("arbitrary"))(x, g, tgt)


def _a_pre(val, gate):
    return val * _sigmoid(gate)


def _a_post(acc, az, cb, lng, lnb, pw, pwb):
    t = acc + cb
    mu = jnp.mean(t, axis=-1, keepdims=True)
    xc = t - mu
    ln = xc * lax.rsqrt(jnp.mean(xc * xc, axis=-1, keepdims=True) + EPS) * lng + lnb
    return (_mm(_silu(ln), pw) + pwb) * _silu(az)


def _halo_map(tl, halo, col):
    r = tl // halo
    return lambda i: (jnp.maximum(i * r - 1, 0), col)


def _a_fwd(proj, cw, cb, lng, lnb, pw, pwb):
    L = proj.shape[0]

    def body(vg_ref, az_ref, hvg_ref, cw_ref, cb_ref, lng_ref, lnb_ref, pw_ref, pwb_ref, o_ref):
        keep = (pl.program_id(0) > 0).astype(F32)
        a_h = _a_pre(hvg_ref[:, 0:BR], hvg_ref[:, BR:2 * BR]) * keep
        a_t = _a_pre(vg_ref[:, 0:BR], vg_ref[:, BR:2 * BR])
        ext = jnp.concatenate([a_h, a_t], axis=0)
        acc = _conv_taps(ext, cw_ref, HALO_A, K_A, TL)
        o_ref[...] = _a_post(acc, az_ref[...], cb_ref[...], lng_ref[...], lnb_ref[...], pw_ref[...], pwb_ref[...])

    vec = pl.BlockSpec((1, BR), lambda i: (0, 0))
    return pl.pallas_call(
        body, grid=(L // TL,),
        in_specs=[pl.BlockSpec((TL, 2 * BR), lambda i: (i, 0)), pl.BlockSpec((TL, BR), lambda i: (i, 2)),
                  pl.BlockSpec((HALO_A, 2 * BR), _halo_map(TL, HALO_A, 0)),
                  pl.BlockSpec((HALO_A, BR), lambda i: (0, 0)), vec, vec, vec,
                  pl.BlockSpec((BR, BR), lambda i: (0, 0)), vec],
        out_specs=pl.BlockSpec((TL, BR), lambda i: (i, 0)),
        out_shape=jax.ShapeDtypeStruct((L, BR), F32),
        name="a_fwd", compiler_params=_cp("parallel"))(proj, proj, proj, cw, cb, lng, lnb, pw, pwb)


def _a_bwd(proj, dmix, cw, cb, lng, lnb, pw, pwb):
    L = proj.shape[0]
    n = L // TL

    def body(vg_ref, az_ref, hvg_ref, dy_ref, cw_ref, cb_ref, lng_ref, lnb_ref, pw_ref, pwb_ref,
             dp_ref, dcw_ref, dcb_ref, dlng_ref, dlnb_ref, dpw_ref, dpwb_ref, carry_ref):
        i = pl.program_id(0)

        @pl.when(i == 0)
        def _():
            carry_ref[...] = jnp.zeros_like(carry_ref)
            for r in (dcw_ref, dcb_ref, dlng_ref, dlnb_ref, dpw_ref, dpwb_ref):
                r[...] = jnp.zeros_like(r)

        keep = (i < n - 1).astype(F32)
        val, gate = vg_ref[:, 0:BR], vg_ref[:, BR:2 * BR]
        a_h = _a_pre(hvg_ref[:, 0:BR], hvg_ref[:, BR:2 * BR]) * keep
        a_t, vj_pre = jax.vjp(_a_pre, val, gate)
        ext = jnp.concatenate([a_h, a_t], axis=0)
        acc = _conv_taps(ext, cw_ref, HALO_A, K_A, TL)
        _, vj_post = jax.vjp(_a_post, acc, az_ref[...], cb_ref[...], lng_ref[...], lnb_ref[...], pw_ref[...],
                             pwb_ref[...])
        dacc, daz, dcb, dlng, dlnb, dpw, dpwb = vj_post(dy_ref[...])
        dext = _conv_taps_bwd(ext, cw_ref, dcw_ref, dacc, HALO_A, K_A, TL)
        da = _add_tail(dext[HALO_A:], carry_ref[...])
        carry_ref[...] = dext[:HALO_A]
        dval, dgate = vj_pre(da)
        dp_ref[:, 0:BR] = dval
        dp_ref[:, BR:2 * BR] = dgate
        dp_ref[:, 2 * BR:3 * BR] = daz
        dcb_ref[...] += dcb
        dlng_ref[...] += dlng
        dlnb_ref[...] += dlnb
        dpw_ref[...] += dpw
        dpwb_ref[...] += dpwb

    rev = lambda i: n - 1 - i
    vec = pl.BlockSpec((1, BR), lambda i: (0, 0))
    hmap = _halo_map(TL, HALO_A, 0)
    return pl.pallas_call(
        body, grid=(n,),
        in_specs=[pl.BlockSpec((TL, 2 * BR), lambda i: (rev(i), 0)), pl.BlockSpec((TL, BR), lambda i: (rev(i), 2)),
                  pl.BlockSpec((HALO_A, 2 * BR), lambda i: hmap(rev(i))),
                  pl.BlockSpec((TL, BR), lambda i: (rev(i), 0)),
                  pl.BlockSpec((HALO_A, BR), lambda i: (0, 0)), vec, vec, vec,
                  pl.BlockSpec((BR, BR), lambda i: (0, 0)), vec],
        out_specs=[pl.BlockSpec((TL, W_A), lambda i: (rev(i), 0)),
                   pl.BlockSpec((HALO_A, BR), lambda i: (0, 0)), vec, vec, vec,
                   pl.BlockSpec((BR, BR), lambda i: (0, 0)), vec],
        out_shape=[jax.ShapeDtypeStruct((L, W_A), F32), jax.ShapeDtypeStruct((HALO_A, BR), F32)]
        + [jax.ShapeDtypeStruct((1, BR), F32)] * 3
        + [jax.ShapeDtypeStruct((BR, BR), F32), jax.ShapeDtypeStruct((1, BR), F32)],
        scratch_shapes=[pltpu.VMEM((HALO_A, BR), F32)],
        name="a_bwd", compiler_params=_cp("arbitrary"))(proj, proj, proj, dmix, cw, cb, lng, lnb, pw, pwb)


def _c_pre(cg, xc):
    return cg * xc


def _c_post(acc, bg, cz):
    return bg * acc * _silu(cz)


def _c_fwd(proj, cw):
    L = proj.shape[0]

    def body(bg_ref, cx_ref, cz_ref, hcx_ref, cw_ref, o_ref):
        keep = (pl.program_id(0) > 0).astype(F32)
        p_h = _c_pre(hcx_ref[:, 0:BR], hcx_ref[:, BR:2 * BR]) * keep
        p_t = _c_pre(cx_ref[:, 0:BR], cx_ref[:, BR:2 * BR])
        ext = jnp.concatenate([p_h, p_t], axis=0)
        acc = _conv_taps(ext, cw_ref, HALO_S, K_C, TL)
        o_ref[...] = _c_post(acc, bg_ref[...], cz_ref[...])

    return pl.pallas_call(
        body, grid=(L // TL,),
        in_specs=[pl.BlockSpec((TL, BR), lambda i: (i, 5)), pl.BlockSpec((TL, 2 * BR), lambda i: (i, 3)),
                  pl.BlockSpec((TL, BR), lambda i: (i, 8)),
                  pl.BlockSpec((HALO_S, 2 * BR), _halo_map(TL, HALO_S, 3)),
                  pl.BlockSpec((HALO_S, BR), lambda i: (0, 0))],
        out_specs=pl.BlockSpec((TL, BR), lambda i: (i, 0)),
        out_shape=jax.ShapeDtypeStruct((L, BR), F32),
        name="c_fwd", compiler_params=_cp("parallel"))(proj, proj, proj, proj, cw)


def _c_bwd(proj, dmix, cw):
    L = proj.shape[0]
    n = L // TL

    def body(bg_ref, cx_ref, cz_ref, hcx_ref, dy_ref, cw_ref, dp_ref, dcw_ref, carry_ref):
        i = pl.program_id(0)

        @pl.when(i == 0)
        def _():
            carry_ref[...] = jnp.zeros_like(carry_ref)
            dcw_ref[...] = jnp.zeros_like(dcw_ref)

        keep = (i < n - 1).astype(F32)
        p_h = _c_pre(hcx_ref[:, 0:BR], hcx_ref[:, BR:2 * BR]) * keep
        p_t, vj_pre = jax.vjp(_c_pre, cx_ref[:, 0:BR], cx_ref[:, BR:2 * BR])
        ext = jnp.concatenate([p_h, p_t], axis=0)
        acc = _conv_taps(ext, cw_ref, HALO_S, K_C, TL)
        _, vj_post = jax.vjp(_c_post, acc, bg_ref[...], cz_ref[...])
        dacc, dbg, dcz = vj_post(dy_ref[...])
        dext = _conv_taps_bwd(ext, cw_ref, dcw_ref, dacc, HALO_S, K_C, TL)
        dp = _add_tail(dext[HALO_S:], carry_ref[...])
        carry_ref[...] = dext[:HALO_S]
        dcg, dxc = vj_pre(dp)
        dp_ref[:, 0:BR] = dbg
        dp_ref[:, BR:2 * BR] = dcg
        dp_ref[:, 2 * BR:3 * BR] = dxc
        dp_ref[:, 3 * BR:4 * BR] = dcz

    rev = lambda i: n - 1 - i
    hmap = _halo_map(TL, HALO_S, 3)
    return pl.pallas_call(
        body, grid=(n,),
        in_specs=[pl.BlockSpec((TL, BR), lambda i: (rev(i), 5)), pl.BlockSpec((TL, 2 * BR), lambda i: (rev(i), 3)),
                  pl.BlockSpec((TL, BR), lambda i: (rev(i), 8)),
                  pl.BlockSpec((HALO_S, 2 * BR), lambda i: hmap(rev(i))),
                  pl.BlockSpec((TL, BR), lambda i: (rev(i), 2)),
                  pl.BlockSpec((HALO_S, BR), lambda i: (0, 0))],
        out_specs=[pl.BlockSpec((TL, W_C), lambda i: (rev(i), 0)), pl.BlockSpec((HALO_S, BR), lambda i: (0, 0))],
        out_shape=[jax.ShapeDtypeStruct((L, W_C), F32), jax.ShapeDtypeStruct((HALO_S, BR), F32)],
        scratch_shapes=[pltpu.VMEM((HALO_S, BR), F32)],
        name="c_bwd", compiler_params=_cp("arbitrary"))(proj, proj, proj, proj, dmix, cw)


def _s5_prep_fn(lre, lim, ldt, bre, bim, cre, cim):
    grp = lax.broadcasted_iota(jnp.int32, (128, NS), 0)
    lane = lax.broadcasted_iota(jnp.int32, (128, NS), 1)
    expand = (grp == lane // S5_P).astype(F32)
    dt = jnp.exp(_dot_hi(jnp.broadcast_to(ldt, (8, 128)), expand)[0:1])
    lr = jnp.minimum(lre, -1e-4)
    mag = jnp.exp(lr * dt)
    ar = mag * jnp.cos(lim * dt)
    ai = mag * jnp.sin(lim * dt)
    den = lr * lr + lim * lim
    fr = ((ar - 1.0) * lr + ai * lim) / den
    fi = (ai * lr - (ar - 1.0) * lim) / den
    bbr = fr * bre - fi * bim
    bbi = fr * bim + fi * bre
    row = lax.broadcasted_iota(jnp.int32, (BR, NS), 0)
    col = lax.broadcasted_iota(jnp.int32, (BR, NS), 1)
    blk = (row // S5_H == col // S5_P).astype(F32)

    def embed(t):
        return jnp.concatenate([t] * S5_G, axis=0) * blk

    bemb = jnp.concatenate([embed(bbr), embed(bbi)], axis=1)
    cemb = jnp.concatenate([embed(cre), embed(-cim)], axis=1)
    return ar, ai, bemb, cemb


def _s5_prep(lre, lim, ldt, bre, bim, cre, cim):
    def body(*refs):
        outs = _s5_prep_fn(*[r[...] for r in refs[:7]])
        for r, o in zip(refs[7:], outs):
            r[...] = o

    return pl.pallas_call(
        body,
        out_shape=[jax.ShapeDtypeStruct((1, NS), F32)] * 2 + [jax.ShapeDtypeStruct((BR, 2 * NS), F32)] * 2,
        name="s5_prep", compiler_params=pltpu.CompilerParams(vmem_limit_bytes=VMEM_LIMIT),
    )(lre, lim, ldt, bre, bim, cre, cim)


def _s5_prep_bwd(lre, lim, ldt, bre, bim, cre, cim, dar, dai, dbemb, dcemb):
    def body(*refs):
        _, vj = jax.vjp(_s5_prep_fn, *[r[...] for r in refs[:7]])
        grads = vj(tuple(r[...] for r in refs[7:11]))
        for r, o in zip(refs[11:], grads):
            r[...] = o

    return pl.pallas_call(
        body,
        out_shape=[jax.ShapeDtypeStruct((1, NS), F32)] * 2 + [jax.ShapeDtypeStruct((1, 128), F32)]
        + [jax.ShapeDtypeStruct((S5_H, NS), F32)] * 4,
        name="s5_prep_bwd", compiler_params=pltpu.CompilerParams(vmem_limit_bytes=VMEM_LIMIT),
    )(lre, lim, ldt, bre, bim, cre, cim, dar, dai, dbemb, dcemb)


def _s5_scan(xr, xi, ar, ai, reverse):
    n = xr.shape[0]
    row = lax.broadcasted_iota(jnp.int32, (n, 1), 0)
    pr, pi = ar, ai
    d = 1
    while d < n:
        if reverse:
            m = row < n - d
            sr = jnp.where(m, _roll(xr, n - d), 0.0)
            si = jnp.where(m, _roll(xi, n - d), 0.0)
        else:
            m = row >= d
            sr = jnp.where(m, _roll(xr, d), 0.0)
            si = jnp.where(m, _roll(xi, d), 0.0)
        xr, xi = xr + pr * sr - pi * si, xi + pr * si + pi * sr
        pr, pi = pr * pr - pi * pi, 2.0 * pr * pi
        d *= 2
    return xr, xi


def _s5_states(u, bemb_b, ar, ai, sin_r, sin_i):
    bu = jnp.dot(u.astype(BF), bemb_b, preferred_element_type=F32)
    first = lax.broadcasted_iota(jnp.int32, (u.shape[0], 1), 0) == 0
    xr = bu[:, :NS] + jnp.where(first, ar * sin_r - ai * sin_i, 0.0)
    xi = bu[:, NS:] + jnp.where(first, ar * sin_i + ai * sin_r, 0.0)
    return _s5_scan(xr, xi, ar, ai, False)


def _b_post(yssm, u, bz, dsk, gw, gb):
    z = jax.nn.gelu(yssm + dsk * u)
    return z * _sigmoid(_mm(z, gw) + gb) * _silu(bz)


def _b_fwd(proj, ar, ai, bemb, cemb, dsk, gw, gb):
    L = proj.shape[0]
    n = L // TL

    def body(u_ref, bz_ref, ar_ref, ai_ref, be_ref, ce_ref, dsk_ref, gw_ref, gb_ref, o_ref, sin_ref, carry_ref):
        @pl.when(pl.program_id(0) == 0)
        def _():
            carry_ref[...] = jnp.zeros_like(carry_ref)

        sin = carry_ref[...]
        sin_ref[0] = sin
        u = u_ref[...]
        sr, si = _s5_states(u, be_ref[...].astype(BF), ar_ref[...], ai_ref[...], sin[:, :NS], sin[:, NS:])
        carry_ref[:, :NS] = sr[TL - 1:TL]
        carry_ref[:, NS:] = si[TL - 1:TL]
        s = jnp.concatenate([sr, si], axis=1).astype(BF)
        yssm = lax.dot_general(s, ce_ref[...].astype(BF), (((1,), (1,)), ((), ())), preferred_element_type=F32)
        o_ref[...] = _b_post(yssm, u, bz_ref[...], dsk_ref[...], gw_ref[...], gb_ref[...])

    vec = pl.BlockSpec((1, BR), lambda i: (0, 0))
    svec = pl.BlockSpec((1, NS), lambda i: (0, 0))
    emb = pl.BlockSpec((BR, 2 * NS), lambda i: (0, 0))
    return pl.pallas_call(
        body, grid=(n,),
        in_specs=[pl.BlockSpec((TL, BR), lambda i: (i, 3)), pl.BlockSpec((TL, BR), lambda i: (i, 4)),
                  svec, svec, emb, emb, vec, pl.BlockSpec((BR, BR), lambda i: (0, 0)), vec],
        out_specs=[pl.BlockSpec((TL, BR), lambda i: (i, 0)), pl.BlockSpec((1, 1, 2 * NS), lambda i: (i, 0, 0))],
        out_shape=[jax.ShapeDtypeStruct((L, BR), F32), jax.ShapeDtypeStruct((n, 1, 2 * NS), F32)],
        scratch_shapes=[pltpu.VMEM((1, 2 * NS), F32)],
        name="b_fwd", compiler_params=_cp("arbitrary"))(proj, proj, ar, ai, bemb, cemb, dsk, gw, gb)


def _b_bwd(proj, dmix, sin_all, ar, ai, bemb, cemb, dsk, gw, gb):
    L = proj.shape[0]
    n = L // TL

    def body(u_ref, bz_ref, dy_ref, sin_ref, ar_ref, ai_ref, be_ref, ce_ref, dsk_ref, gw_ref, gb_ref,
             dp_ref, dar_ref, dai_ref, dbe_ref, dce_ref, ddsk_ref, dgw_ref, dgb_ref, carry_ref):
        i = pl.program_id(0)

        @pl.when(i == 0)
        def _():
            carry_ref[...] = jnp.zeros_like(carry_ref)
            for r in (dar_ref, dai_ref, dbe_ref, dce_ref, ddsk_ref, dgw_ref, dgb_ref):
                r[...] = jnp.zeros_like(r)

        u = u_ref[...]
        ar, ai = ar_ref[...], ai_ref[...]
        be_b, ce_b = be_ref[...].astype(BF), ce_ref[...].astype(BF)
        sin = sin_ref[0]
        sr, si = _s5_states(u, be_b, ar, ai, sin[:, :NS], sin[:, NS:])
        s_b = jnp.concatenate([sr, si], axis=1).astype(BF)
        yssm = lax.dot_general(s_b, ce_b, (((1,), (1,)), ((), ())), preferred_element_type=F32)
        _, vj = jax.vjp(_b_post, yssm, u, bz_ref[...], dsk_ref[...], gw_ref[...], gb_ref[...])
        dyssm, du, dbz, ddsk, dgw, dgb = vj(dy_ref[...])
        dy_b = dyssm.astype(BF)
        dce_ref[...] += lax.dot_general(dy_b, s_b, (((0,), (0,)), ((), ())), preferred_element_type=F32)
        gs = jnp.dot(dy_b, ce_b, preferred_element_type=F32)
        last = lax.broadcasted_iota(jnp.int32, (TL, 1), 0) == TL - 1
        cr, ci = carry_ref[:, :NS], carry_ref[:, NS:]
        gr = gs[:, :NS] + jnp.where(last, ar * cr + ai * ci, 0.0)
        gi = gs[:, NS:] + jnp.where(last, ar * ci - ai * cr, 0.0)
        dsr, dsi = _s5_scan(gr, gi, ar, -ai, True)
        carry_ref[:, :NS] = dsr[0:1]
        carry_ref[:, NS:] = dsi[0:1]
        first = lax.broadcasted_iota(jnp.int32, (TL, 1), 0) == 0
        pr = jnp.where(first, sin[:, :NS], _roll(sr, 1))
        pi = jnp.where(first, sin[:, NS:], _roll(si, 1))
        dar_ref[...] += jnp.sum(dsr * pr + dsi * pi, axis=0, keepdims=True)
        dai_ref[...] += jnp.sum(dsi * pr - dsr * pi, axis=0, keepdims=True)
        ds_b = jnp.concatenate([dsr, dsi], axis=1).astype(BF)
        dbe_ref[...] += lax.dot_general(u.astype(BF), ds_b, (((0,), (0,)), ((), ())), preferred_element_type=F32)
        du = du + lax.dot_general(ds_b, be_b, (((1,), (1,)), ((), ())), preferred_element_type=F32)
        dp_ref[:, 0:BR] = du
        dp_ref[:, BR:2 * BR] = dbz
        ddsk_ref[...] += ddsk
        dgw_ref[...] += dgw
        dgb_ref[...] += dgb

    rev = lambda i: n - 1 - i
    vec = pl.BlockSpec((1, BR), lambda i: (0, 0))
    svec = pl.BlockSpec((1, NS), lambda i: (0, 0))
    emb = pl.BlockSpec((BR, 2 * NS), lambda i: (0, 0))
    mat = pl.BlockSpec((BR, BR), lambda i: (0, 0))
    return pl.pallas_call(
        body, grid=(n,),
        in_specs=[pl.BlockSpec((TL, BR), lambda i: (rev(i), 3)), pl.BlockSpec((TL, BR), lambda i: (rev(i), 4)),
                  pl.BlockSpec((TL, BR), lambda i: (rev(i), 1)),
                  pl.BlockSpec((1, 1, 2 * NS), lambda i: (rev(i), 0, 0)),
                  svec, svec, emb, emb, vec, mat, vec],
        out_specs=[pl.BlockSpec((TL, W_B), lambda i: (rev(i), 0)), svec, svec, emb, emb, vec, mat, vec],
        out_shape=[jax.ShapeDtypeStruct((L, W_B), F32)] + [jax.ShapeDtypeStruct((1, NS), F32)] * 2
        + [jax.ShapeDtypeStruct((BR, 2 * NS), F32)] * 2
        + [jax.ShapeDtypeStruct((1, BR), F32), jax.ShapeDtypeStruct((BR, BR), F32), jax.ShapeDtypeStruct((1, BR), F32)],
        scratch_shapes=[pltpu.VMEM((1, 2 * NS), F32)],
        name="b_bwd", compiler_params=_cp("arbitrary"))(proj, proj, dmix, sin_all, ar, ai, bemb, cemb, dsk, gw, gb)


def _d_post(cq, ab, dz, s0, s1, s2, s3, p1, p2, ng):
    c = cq.shape[0]
    qkv = _silu(cq)
    gall = -jnp.exp(p1) * jax.nn.softplus(ab + p2)
    ball = _sigmoid(ab)
    row = lax.broadcasted_iota(jnp.int32, (c, c), 0)
    col = lax.broadcasted_iota(jnp.int32, (c, c), 1)
    causal = row >= col
    strict = row > col
    gc_all = _dot_hi(causal.astype(F32), gall)
    gc_t = gc_all.T
    heads = range(DN_H)
    st = (s0, s1, s2, s3)
    q = [qkv[:, h * DN_D:(h + 1) * DN_D] for h in heads]
    k = [qkv[:, BR + h * DN_D:BR + (h + 1) * DN_D] for h in heads]
    v = [qkv[:, 2 * BR + h * DN_D:2 * BR + (h + 1) * DN_D] for h in heads]
    q = [t * lax.rsqrt(jnp.sum(t * t, axis=-1, keepdims=True) + EPS) * (DN_D ** -0.5) for t in q]
    k = [t * lax.rsqrt(jnp.sum(t * t, axis=-1, keepdims=True) + EPS) for t in k]
    gcol = [gc_all[:, h:h + 1] for h in heads]
    beta = [ball[:, DN_H + h:DN_H + h + 1] for h in heads]
    decay = [jnp.where(causal, jnp.exp(jnp.where(causal, gcol[h] - gc_t[h:h + 1, :], 0.0)), 0.0) for h in heads]
    kb = [k[h] * beta[h] for h in heads]
    lm = [jnp.where(strict, _mm_nt(kb[h], k[h]) * decay[h], 0.0) for h in heads]
    ainv = _unit_lower_inv(tuple(lm))
    egc = [jnp.exp(g) for g in gcol]
    uw = [_mm3(ainv[h], jnp.concatenate([v[h] * beta[h], kb[h] * egc[h]], axis=1)) for h in heads]
    attn = [_mm_nt(q[h], k[h]) * decay[h] for h in heads]
    glast = [g[c - 1:c, :] for g in gcol]
    kd = [k[h] * jnp.exp(glast[h] - gcol[h]) for h in heads]
    vnew = [uw[h][:, :DN_D] - _mm(uw[h][:, DN_D:], st[h]) for h in heads]
    o = [_mm(q[h] * egc[h], st[h]) + _mm(attn[h], vnew[h]) for h in heads]
    news = [st[h] * jnp.exp(glast[h]) + _mm_tn(kd[h], vnew[h]) for h in heads]
    outs = [t * lax.rsqrt(jnp.mean(t * t, axis=-1, keepdims=True) + EPS) * ng for t in o]
    yd = jnp.concatenate(outs, axis=1) * _silu(dz)
    return (yd, *news)


def _d_fwd(proj, cw, p1, p2, ng):
    L = proj.shape[0]
    n = L // DN_C

    def body(qkv_ref, ab_ref, dz_ref, hq_ref, cw_ref, p1_ref, p2_ref, ng_ref, o_ref, sall_ref, s_ref):
        i = pl.program_id(0)

        @pl.when(i == 0)
        def _():
            s_ref[...] = jnp.zeros_like(s_ref)

        keep = (i > 0).astype(F32)
        ext = jnp.concatenate([hq_ref[...] * keep, qkv_ref[...]], axis=0)
        cq = _conv_taps(ext, cw_ref, HALO_S, K_DN, DN_C)
        st = [s_ref[h] for h in range(DN_H)]
        for h in range(DN_H):
            sall_ref[0, h] = st[h]
        out = _d_post(cq, ab_ref[...], dz_ref[...], *st, p1_ref[...], p2_ref[...], ng_ref[...])
        o_ref[...] = out[0]
        for h in range(DN_H):
            s_ref[h] = out[1 + h]

    return pl.pallas_call(
        body, grid=(n,),
        in_specs=[pl.BlockSpec((DN_C, 3 * BR), lambda i: (i, 3)), pl.BlockSpec((DN_C, 128), lambda i: (i, 26)),
                  pl.BlockSpec((DN_C, BR), lambda i: (i, 12)),
                  pl.BlockSpec((HALO_S, 3 * BR), _halo_map(DN_C, HALO_S, 3)),
                  pl.BlockSpec((HALO_S, 3 * BR), lambda i: (0, 0)),
                  pl.BlockSpec((1, 128), lambda i: (0, 0)), pl.BlockSpec((1, 128), lambda i: (0, 0)),
                  pl.BlockSpec((1, DN_D), lambda i: (0, 0))],
        out_specs=[pl.BlockSpec((DN_C, BR), lambda i: (i, 0)),
                   pl.BlockSpec((1, DN_H, DN_D, DN_D), lambda i: (i, 0, 0, 0))],
        out_shape=[jax.ShapeDtypeStruct((L, BR), F32), jax.ShapeDtypeStruct((n, DN_H, DN_D, DN_D), F32)],
        scratch_shapes=[pltpu.VMEM((DN_H, DN_D, DN_D), F32)],
        name="d_fwd", compiler_params=_cp("arbitrary"))(proj, proj, proj, proj, cw, p1, p2, ng)


def _d_bwd(proj, dmix, sall, cw, p1, p2, ng):
    L = proj.shape[0]
    n = L // DN_C

    def body(qkv_ref, ab_ref, dz_ref, hq_ref, dy_ref, sall_ref, cw_ref, p1_ref, p2_ref, ng_ref,
             dp_ref, dcw_ref, dp1_ref, dp2_ref, dng_ref, ds_ref, carry_ref):
        i = pl.program_id(0)

        @pl.when(i == 0)
        def _():
            ds_ref[...] = jnp.zeros_like(ds_ref)
            carry_ref[...] = jnp.zeros_like(carry_ref)
            for r in (dcw_ref, dp1_ref, dp2_ref, dng_ref):
                r[...] = jnp.zeros_like(r)

        keep = (i < n - 1).astype(F32)
        ext = jnp.concatenate([hq_ref[...] * keep, qkv_ref[...]], axis=0)
        cq = _conv_taps(ext, cw_ref, HALO_S, K_DN, DN_C)
        st = [sall_ref[0, h] for h in range(DN_H)]
        _, vj = jax.vjp(_d_post, cq, ab_ref[...], dz_ref[...], *st, p1_ref[...], p2_ref[...], ng_ref[...])
        grads = vj((dy_ref[...], *[ds_ref[h] for h in range(DN_H)]))
        dcq, dab, ddz = grads[0], grads[1], grads[2]
        for h in range(DN_H):
            ds_ref[h] = grads[3 + h]
        dp1_ref[...] += grads[7]
        dp2_ref[...] += grads[8]
        dng_ref[...] += grads[9]
        dext = _conv_taps_bwd(ext, cw_ref, dcw_ref, dcq, HALO_S, K_DN, DN_C)
        dp_ref[:, 0:3 * BR] = _add_tail(dext[HALO_S:], carry_ref[...])
        carry_ref[...] = dext[:HALO_S]
        dp_ref[:, 3 * BR:4 * BR] = ddz
        dp_ref[:, 4 * BR:4 * BR + 128] = dab

    rev = lambda i: n - 1 - i
    hmap = _halo_map(DN_C, HALO_S, 3)
    v128 = pl.BlockSpec((1, 128), lambda i: (0, 0))
    return pl.pallas_call(
        body, grid=(n,),
        in_specs=[pl.BlockSpec((DN_C, 3 * BR), lambda i: (rev(i), 3)), pl.BlockSpec((DN_C, 128), lambda i: (rev(i), 26)),
                  pl.BlockSpec((DN_C, BR), lambda i: (rev(i), 12)),
                  pl.BlockSpec((HALO_S, 3 * BR), lambda i: hmap(rev(i))),
                  pl.BlockSpec((DN_C, BR), lambda i: (rev(i), 3)),
                  pl.BlockSpec((1, DN_H, DN_D, DN_D), lambda i: (rev(i), 0, 0, 0)),
                  pl.BlockSpec((HALO_S, 3 * BR), lambda i: (0, 0)), v128, v128,
                  pl.BlockSpec((1, DN_D), lambda i: (0, 0))],
        out_specs=[pl.BlockSpec((DN_C, W_D), lambda i: (rev(i), 0)),
                   pl.BlockSpec((HALO_S, 3 * BR), lambda i: (0, 0)), v128, v128,
                   pl.BlockSpec((1, DN_D), lambda i: (0, 0))],
        out_shape=[jax.ShapeDtypeStruct((L, W_D), F32), jax.ShapeDtypeStruct((HALO_S, 3 * BR), F32),
                   jax.ShapeDtypeStruct((1, 128), F32), jax.ShapeDtypeStruct((1, 128), F32),
                   jax.ShapeDtypeStruct((1, DN_D), F32)],
        scratch_shapes=[pltpu.VMEM((DN_H, DN_D, DN_D), F32), pltpu.VMEM((HALO_S, 3 * BR), F32)],
        name="d_bwd", compiler_params=_cp("arbitrary"))(proj, proj, proj, proj, dmix, sall, cw, p1, p2, ng)


def _pick_rows(rows, cap):
    best = 8
    for t in range(8, cap + 1, 8):
        if rows % t == 0:
            best = t
    return best


def _adamw(w, g, m, v, name):
    rows, wd = w.shape
    tr = _pick_rows(rows, 512)
    c1 = 1.0 - ADAM_B1 ** ADAM_STEP
    c2 = 1.0 - ADAM_B2 ** ADAM_STEP

    def body(w_ref, g_ref, m_ref, v_ref, d_ref, mo_ref, vo_ref):
        gv = g_ref[...]
        mn = ADAM_B1 * m_ref[...] + (1.0 - ADAM_B1) * gv
        vn = ADAM_B2 * v_ref[...] + (1.0 - ADAM_B2) * (gv * gv)
        d_ref[...] = -ADAM_LR * ((mn / c1) / (jnp.sqrt(vn / c2) + ADAM_EPS) + ADAM_WD * w_ref[...])
        mo_ref[...] = mn
        vo_ref[...] = vn

    spec = pl.BlockSpec((tr, wd), lambda i: (i, 0))
    return pl.pallas_call(
        body, grid=(rows // tr,), in_specs=[spec] * 4, out_specs=[spec] * 3,
        out_shape=[jax.ShapeDtypeStruct((rows, wd), F32)] * 3,
        name=name, compiler_params=_cp("parallel"))(w, g, m, v)


def _sum_slots(r, name):
    n, rows, wd = r.shape
    tr = _pick_rows(rows, 384)

    def body(r_ref, o_ref):
        acc = r_ref[0].astype(F32)
        for j in range(1, n):
            acc = acc + r_ref[j].astype(F32)
        o_ref[...] = acc

    return pl.pallas_call(
        body, grid=(rows // tr,),
        in_specs=[pl.BlockSpec((n, tr, wd), lambda i: (0, i, 0))],
        out_specs=pl.BlockSpec((tr, wd), lambda i: (i, 0)),
        out_shape=jax.ShapeDtypeStruct((rows, wd), F32),
        name=name, compiler_params=_cp("parallel"))(r)


AXES = ("x", "y", "c")


def _group_peer(axes, k):
    pos = {a: lax.axis_index(a) for a in AXES}
    idx = 0
    for a in axes:
        idx = idx * 2 + pos[a]
    peer = dict(pos)
    for b, a in enumerate(reversed(axes)):
        if (k >> b) & 1:
            peer[a] = 1 - pos[a]
    return idx, tuple(peer[a] for a in AXES)


def _exchange(xs, axes, scatter, name):
    n = 2 ** len(axes)
    na = len(xs)

    def body(*refs):
        x_refs, o_refs = refs[:na], refs[na:2 * na]
        send_sems, recv_sems, local_sems = refs[2 * na:]
        me, _ = _group_peer(axes, 0)

        def copy(a, k):
            src = x_refs[a].at[me ^ k] if scatter else x_refs[a]
            return pltpu.make_async_remote_copy(src_ref=src, dst_ref=o_refs[a].at[me], send_sem=send_sems.at[a, k],
                                                recv_sem=recv_sems.at[a, k], device_id=_group_peer(axes, k)[1],
                                                device_id_type=MESH)

        mine = [pltpu.make_async_copy(x_refs[a].at[me] if scatter else x_refs[a], o_refs[a].at[me], local_sems.at[a])
                for a in range(na)]
        for cp in mine:
            cp.start()
        copies = [copy(a, k) for k in range(1, n) for a in range(na)]
        for cp in copies:
            cp.start()
        for cp in copies:
            cp.wait_send()
        for cp in copies:
            cp.wait_recv()
        for cp in mine:
            cp.wait()

    return pl.pallas_call(
        body,
        out_shape=[jax.ShapeDtypeStruct(x.shape if scatter else (n,) + x.shape, x.dtype) for x in xs],
        in_specs=[pl.BlockSpec(memory_space=pl.ANY)] * na, out_specs=[pl.BlockSpec(memory_space=pl.ANY)] * na,
        scratch_shapes=[pltpu.SemaphoreType.DMA((na, n)), pltpu.SemaphoreType.DMA((na, n)),
                        pltpu.SemaphoreType.DMA((na,))],
        name=name)(*xs)


SHARDED_SMALL = (("a_conv_w", 2), ("a_pw_w", 1), ("s5_glu_w", 1), ("c_conv_w", 2), ("d_conv_w", 2))
REPLICATED = ("norm_g", "a_conv_b", "a_ln_g", "a_ln_b", "a_pw_b", "s5_lambda_re", "s5_lambda_im", "s5_b_re",
              "s5_b_im", "s5_c_re", "s5_c_im", "s5_d", "s5_log_dt", "s5_glu_b", "d_a_log", "d_dt_bias",
              "d_norm_g", "final_g")
WEIGHTS = ("norm_g", "w_in", "a_conv_w", "a_conv_b", "a_ln_g", "a_ln_b", "a_pw_w", "a_pw_b", "s5_lambda_re",
           "s5_lambda_im", "s5_b_re", "s5_b_im", "s5_c_re", "s5_c_im", "s5_d", "s5_log_dt", "s5_glu_w",
           "s5_glu_b", "c_conv_w", "d_conv_w", "d_a_log", "d_dt_bias", "d_norm_g", "w_out", "final_g")
LANES = 1024


def _pack(arrs, rows):
    flat = jnp.concatenate([a.reshape(-1) for a in arrs])
    return jnp.pad(flat, (0, rows * LANES - flat.shape[0])).reshape(rows, LANES)


def _unpack(slab, shapes):
    flat = slab.reshape(-1)
    out, off = [], 0
    for s in shapes:
        size = 1
        for d in s:
            size *= d
        out.append(flat[off:off + size].reshape(s))
        off += size
    return out


def _rows_for(shapes, mult):
    total = 0
    for s in shapes:
        size = 1
        for d in s:
            size *= d
        total += size
    rows = -(-total // LANES)
    return -(-rows // mult) * mult


def _row(v, width=None):
    v = v.reshape(1, -1)
    return v if width is None else jnp.pad(v, ((0, 0), (0, width - v.shape[1])))


def _pad_rows(w, rows):
    return jnp.pad(w, ((0, rows - w.shape[0]), (0, 0)))


def _permute_in(w):
    return jnp.concatenate([w[:, :3072], w[:, 3080:N_IN], w[:, 3072:3080],
                            jnp.zeros((w.shape[0], N_INP - N_IN), w.dtype)], axis=1)


def _layer_fwd(x, p):
    proj, h = _proj_fwd(x, p["norm_g"], p["wp"])
    ya = _a_fwd(proj, p["a_cw"], p["a_cb"], p["a_lng"], p["a_lnb"], p["a_pw"], p["a_pwb"])
    ar, ai, bemb, cemb = _s5_prep(*p["s5"])
    yb, sin_all = _b_fwd(proj, ar, ai, bemb, cemb, p["s5_d"], p["glu_w"], p["glu_b"])
    yc = _c_fwd(proj, p["c_cw"])
    yd, sall = _d_fwd(proj, p["d_cw"], p["d_p1"], p["d_p2"], p["d_ng"])
    xo = _out_fwd(x, ya, yb, yc, yd, p["wo"])
    return xo, dict(x=x, proj=proj, h=h, ys=(ya, yb, yc, yd), sin_all=sin_all, sall=sall,
                    s5=(ar, ai, bemb, cemb))


def _layer_bwd(dxo, p, r):
    proj = r["proj"]
    ar, ai, bemb, cemb = r["s5"]
    dmix = _out_bwd_x(dxo, p["wo"])
    dwo = _dwout(*r["ys"], dxo)
    dpa, dcw_a, dcb, dlng, dlnb, dpw, dpwb = _a_bwd(proj, dmix, p["a_cw"], p["a_cb"], p["a_lng"], p["a_lnb"],
                                                     p["a_pw"], p["a_pwb"])
    dpb, dar, dai, dbe, dce, ddsk, dgw, dgb = _b_bwd(proj, dmix, r["sin_all"], ar, ai, bemb, cemb, p["s5_d"],
                                                     p["glu_w"], p["glu_b"])
    dlre, dlim, dldt, dbre, dbim, dcre, dcim = _s5_prep_bwd(*p["s5"], dar, dai, dbe, dce)
    dpc, dcw_c = _c_bwd(proj, dmix, p["c_cw"])
    dpd, dcw_d, dp1, dp2, dng = _d_bwd(proj, dmix, r["sall"], p["d_cw"], p["d_p1"], p["d_p2"], p["d_ng"])
    dx, dg = _proj_bwd_x(r["x"], p["norm_g"], dpa, dpb, dpc, dpd, p["wp"], dxo)
    dwd = _mm_tn_acc(r["h"], dpd, 384, "dwin_d")
    dwin = jnp.concatenate([_mm_tn_acc(r["h"], dpa, 256, "dwin_a"), _mm_tn_acc(r["h"], dpb, 256, "dwin_b"),
                            _mm_tn_acc(r["h"], dpc, 256, "dwin_c"), dwd[:, :3 * BR],
                            dwd[:, 4 * BR:4 * BR + 2 * DN_H], dwd[:, 3 * BR:4 * BR]], axis=1)

    def unrows(t, perm):
        return jnp.transpose(t.reshape(S5_H, S5_G, S5_P), perm)

    grads = dict(
        norm_g=dg.reshape(-1), w_in=dwin, a_conv_w=dcw_a[:K_A], a_conv_b=dcb.reshape(-1),
        a_ln_g=dlng.reshape(-1), a_ln_b=dlnb.reshape(-1), a_pw_w=dpw, a_pw_b=dpwb.reshape(-1),
        s5_lambda_re=dlre.reshape(S5_G, S5_P), s5_lambda_im=dlim.reshape(S5_G, S5_P),
        s5_b_re=unrows(dbre, (1, 2, 0)), s5_b_im=unrows(dbim, (1, 2, 0)),
        s5_c_re=unrows(dcre, (1, 0, 2)), s5_c_im=unrows(dcim, (1, 0, 2)),
        s5_d=ddsk.reshape(-1), s5_log_dt=dldt[0, :S5_G], s5_glu_w=dgw, s5_glu_b=dgb.reshape(-1),
        c_conv_w=dcw_c[:K_C], d_conv_w=dcw_d[:K_DN], d_a_log=dp1[0, :DN_H], d_dt_bias=dp2[0, :DN_H],
        d_norm_g=dng.reshape(-1), w_out=dwo)
    return dx, grads


def _layer_params(full, wp, wo, l):
    return dict(
        norm_g=_row(full["norm_g"][l]), wp=wp,
        a_cw=_pad_rows(full["a_conv_w"][l], HALO_A), a_cb=_row(full["a_conv_b"][l]),
        a_lng=_row(full["a_ln_g"][l]), a_lnb=_row(full["a_ln_b"][l]), a_pw=full["a_pw_w"][l],
        a_pwb=_row(full["a_pw_b"][l]),
        s5=(_row(full["s5_lambda_re"][l]), _row(full["s5_lambda_im"][l]), _row(full["s5_log_dt"][l], 128),
            jnp.transpose(full["s5_b_re"][l], (2, 0, 1)).reshape(S5_H, NS),
            jnp.transpose(full["s5_b_im"][l], (2, 0, 1)).reshape(S5_H, NS),
            jnp.transpose(full["s5_c_re"][l], (1, 0, 2)).reshape(S5_H, NS),
            jnp.transpose(full["s5_c_im"][l], (1, 0, 2)).reshape(S5_H, NS)),
        s5_d=_row(full["s5_d"][l]), glu_w=full["s5_glu_w"][l], glu_b=_row(full["s5_glu_b"][l]),
        c_cw=_pad_rows(full["c_conv_w"][l], HALO_S), d_cw=_pad_rows(full["d_conv_w"][l], HALO_S),
        d_p1=_row(full["d_a_log"][l], 128), d_p2=_row(full["d_dt_bias"][l], 128),
        d_ng=_row(full["d_norm_g"][l]), wo=wo)


def _local_step(xs, tgt, full, wps, wos):
    params = [_layer_params(full, wps[l], wos[l], l) for l in range(DEPTH)]
    saved = []
    h = xs
    for l in range(DEPTH):
        h, r = _layer_fwd(h, params[l])
        saved.append(r)
    loss_tile, dx, dfg = _loss_bwd(h, _row(full["final_g"]), tgt)
    layer_grads = [None] * DEPTH
    for l in reversed(range(DEPTH)):
        dx, layer_grads[l] = _layer_bwd(dx, params[l], saved[l])
    grads = {n: jnp.stack([layer_grads[l][n] for l in range(DEPTH)]) for n in WEIGHTS if n != "final_g"}
    grads["final_g"] = dfg.reshape(-1)
    return loss_tile, dx, grads


def kernel(x, norm_g, w_in, a_conv_w, a_conv_b, a_ln_g, a_ln_b, a_pw_w, a_pw_b, s5_lambda_re, s5_lambda_im, s5_b_re, s5_b_im, s5_c_re, s5_c_im, s5_d, s5_log_dt, s5_glu_w, s5_glu_b, c_conv_w, d_conv_w, d_a_log, d_dt_bias, d_norm_g, w_out, final_g, loss_target, m_norm_g, m_w_in, m_a_conv_w, m_a_conv_b, m_a_ln_g, m_a_ln_b, m_a_pw_w, m_a_pw_b, m_s5_lambda_re, m_s5_lambda_im, m_s5_b_re, m_s5_b_im, m_s5_c_re, m_s5_c_im, m_s5_d, m_s5_log_dt, m_s5_glu_w, m_s5_glu_b, m_c_conv_w, m_d_conv_w, m_d_a_log, m_d_dt_bias, m_d_norm_g, m_w_out, m_final_g, v_norm_g, v_w_in, v_a_conv_w, v_a_conv_b, v_a_ln_g, v_a_ln_b, v_a_pw_w, v_a_pw_b, v_s5_lambda_re, v_s5_lambda_im, v_s5_b_re, v_s5_b_im, v_s5_c_re, v_s5_c_im, v_s5_d, v_s5_log_dt, v_s5_glu_w, v_s5_glu_b, v_c_conv_w, v_d_conv_w, v_d_a_log, v_d_dt_bias, v_d_norm_g, v_w_out, v_final_g):
    given = dict(locals())
    w = {n: given[n] for n in WEIGHTS}
    m = {n: given["m_" + n] for n in WEIGHTS}
    v = {n: given["v_" + n] for n in WEIGHTS}
    xs, tgt = x[0], loss_target[0]

    n_in, n_out = w["w_in"].shape[2], w["w_out"].shape[1]
    sm_names = [n for n, _ in SHARDED_SMALL]
    sm_shapes = [w[n].shape for n in sm_names]
    sm_rows = _rows_for(sm_shapes, 16)
    g_in, g_out, g_sm = _exchange(
        [w["w_in"].astype(BF).reshape(DEPTH * D_MODEL, n_in), w["w_out"].astype(BF).reshape(DEPTH * n_out, D_MODEL),
         _pack([w[n] for n in sm_names], sm_rows)], ("x", "y"), False, "gather_weights")
    full = dict(w)
    parts = [_unpack(g_sm[j], sm_shapes) for j in range(4)]
    for i, (n, ax) in enumerate(SHARDED_SMALL):
        full[n] = jnp.concatenate([parts[j][i] for j in range(4)], axis=ax)
    wps = [_permute_in(jnp.concatenate([g_in[j, l * D_MODEL:(l + 1) * D_MODEL] for j in range(4)], axis=1))
           for l in range(DEPTH)]
    wos = [jnp.concatenate([g_out[j, l * n_out:(l + 1) * n_out] for j in range(4)], axis=0) for l in range(DEPTH)]

    loss_tile, dx, grads = _local_step(xs, tgt, full, wps, wos)

    s_in = jnp.transpose(grads["w_in"].reshape(2, 2, D_MODEL, 4, n_in), (3, 0, 1, 2, 4)).astype(BF)
    s_out = jnp.transpose(grads["w_out"].reshape(2, 2, 4, n_out, D_MODEL), (2, 0, 1, 3, 4)).astype(BF)
    slots = []
    for j in range(4):
        sl = [lax.slice_in_dim(grads[n], j * w[n].shape[ax], (j + 1) * w[n].shape[ax], axis=ax)
              for n, ax in SHARDED_SMALL]
        slots.append(_pack(sl, sm_rows))
    rp_shapes = [w[n].shape for n in REPLICATED] + [(1,)]
    rp_rows = _rows_for(rp_shapes, 64)
    r_in, r_out, r_sm, r_rp = _exchange(
        [s_in.reshape(8, 2 * D_MODEL, n_in), s_out.reshape(8, 2 * n_out, D_MODEL),
         jnp.stack(slots).reshape(8, sm_rows // 2, LANES),
         _pack([grads[n] for n in REPLICATED] + [loss_tile[0, 0:1]], rp_rows).reshape(8, rp_rows // 8, LANES)],
        AXES, True, "scatter_grads")
    h_in, h_out, h_sm = _exchange(
        [_sum_slots(r_in, "sum_w_in"), _sum_slots(r_out, "sum_w_out"), _sum_slots(r_sm, "sum_small")],
        ("c",), False, "gather_halves")
    (g_rp,) = _exchange([_sum_slots(r_rp, "sum_replicated")], AXES, False, "gather_replicated")
    g_rp = g_rp.reshape(rp_rows, LANES)
    g_sm = h_sm.reshape(sm_rows, LANES)

    out = {}

    def put(name, shape, res):
        for key, t in zip(("delta", "new_m", "new_v"), res):
            out[key + "_" + name] = t.reshape(shape)

    for name, g2 in (("w_in", h_in.reshape(DEPTH * D_MODEL, n_in)), ("w_out", h_out.reshape(DEPTH * n_out, D_MODEL))):
        shape = w[name].shape
        out["grad_" + name] = g2.reshape(shape)
        put(name, shape, _adamw(w[name].reshape(g2.shape), g2, m[name].reshape(g2.shape), v[name].reshape(g2.shape),
                                "adamw_" + name))
    zero = jnp.zeros((1,), F32)
    res_sm = _adamw(_pack([w[n] for n in sm_names], sm_rows), g_sm, _pack([m[n] for n in sm_names], sm_rows),
                    _pack([v[n] for n in sm_names], sm_rows), "adamw_small")
    res_rp = _adamw(_pack([w[n] for n in REPLICATED] + [zero], rp_rows), g_rp,
                    _pack([m[n] for n in REPLICATED] + [zero], rp_rows),
                    _pack([v[n] for n in REPLICATED] + [zero], rp_rows), "adamw_replicated")
    for key, sm, rp in (("grad", g_sm, g_rp), ("delta", res_sm[0], res_rp[0]), ("new_m", res_sm[1], res_rp[1]),
                        ("new_v", res_sm[2], res_rp[2])):
        for n, t in zip(sm_names, _unpack(sm, sm_shapes)):
            out[key + "_" + n] = t
        for n, t in zip(REPLICATED, _unpack(rp, rp_shapes[:-1])):
            out[key + "_" + n] = t
    loss = _unpack(g_rp, rp_shapes)[-1].reshape(())
    return (loss, dx[None], *[out["grad_" + n] for n in WEIGHTS], *[out["delta_" + n] for n in WEIGHTS],
            *[out["new_m_" + n] for n in WEIGHTS], *[out["new_v_" + n] for n in WEIGHTS])
```

```python
import functools

import jax
import jax.numpy as jnp
from jax import lax
from jax.experimental import pallas as pl
from jax.experimental.pallas import tpu as pltpu

F32, BF = jnp.float32, jnp.bfloat16
HI = lax.Precision.HIGHEST
MESH = pl.DeviceIdType.MESH

D_MODEL = 1024
BR = 256
DEPTH = 4
N_IN = 3336
N_INP = 3456
COL_A, COL_B, COL_C, COL_D = 0, 768, 1280, 2304
W_A, W_B, W_C, W_D = 768, 512, 1024, 1152
S5_G, S5_H, S5_P = 16, 16, 64
NS = S5_G * S5_P
DN_H, DN_D, DN_C = 4, 64, 64
K_A, K_C, K_DN = 31, 3, 4
HALO_A, HALO_S = 32, 8
EPS = 1e-6
TL = 256
VMEM_LIMIT = 56 * 1024 * 1024

ADAM_LR, ADAM_B1, ADAM_B2, ADAM_EPS, ADAM_WD, ADAM_STEP = 0.001, 0.9, 0.999, 1e-08, 0.01, 10


def _cp(*sem):
    return pltpu.CompilerParams(dimension_semantics=sem, vmem_limit_bytes=VMEM_LIMIT)


def _sigmoid(x):
    return jax.nn.sigmoid(x)


def _silu(x):
    return x * jax.nn.sigmoid(x)


def _rmsnorm(x, g):
    return x * lax.rsqrt(jnp.mean(x * x, axis=-1, keepdims=True) + EPS) * g


@jax.custom_vjp
def _mm(a, w):
    return jnp.dot(a.astype(BF), w.astype(BF), preferred_element_type=F32)


def _mm_f(a, w):
    return _mm(a, w), (a, w)


def _mm_b(res, g):
    a, w = res
    gb = g.astype(BF)
    da = lax.dot_general(gb, w.astype(BF), (((1,), (1,)), ((), ())), preferred_element_type=F32)
    dw = lax.dot_general(a.astype(BF), gb, (((0,), (0,)), ((), ())), preferred_element_type=F32)
    return da, dw


_mm.defvjp(_mm_f, _mm_b)


@jax.custom_vjp
def _mm_nt(a, b):
    return lax.dot_general(a.astype(BF), b.astype(BF), (((1,), (1,)), ((), ())), preferred_element_type=F32)


def _mm_nt_f(a, b):
    return _mm_nt(a, b), (a, b)


def _mm_nt_b(res, g):
    a, b = res
    gb = g.astype(BF)
    da = jnp.dot(gb, b.astype(BF), preferred_element_type=F32)
    db = lax.dot_general(gb, a.astype(BF), (((0,), (0,)), ((), ())), preferred_element_type=F32)
    return da, db


_mm_nt.defvjp(_mm_nt_f, _mm_nt_b)


@jax.custom_vjp
def _mm_tn(a, b):
    return lax.dot_general(a.astype(BF), b.astype(BF), (((0,), (0,)), ((), ())), preferred_element_type=F32)


def _mm_tn_f(a, b):
    return _mm_tn(a, b), (a, b)


def _mm_tn_b(res, g):
    a, b = res
    gb = g.astype(BF)
    da = lax.dot_general(b.astype(BF), gb, (((1,), (1,)), ((), ())), preferred_element_type=F32)
    db = jnp.dot(a.astype(BF), gb, preferred_element_type=F32)
    return da, db


_mm_tn.defvjp(_mm_tn_f, _mm_tn_b)


def _dot_hi(a, b):
    return jnp.dot(a, b, precision=HI, preferred_element_type=F32)


def _split(a):
    hi = a.astype(BF)
    return hi, (a - hi.astype(F32)).astype(BF)


def _dot3(a, b, dims=(((1,), (0,)), ((), ()))):
    ah, al = _split(a)
    bh, bl = _split(b)
    d = functools.partial(lax.dot_general, dimension_numbers=dims, preferred_element_type=F32)
    return d(ah, bh) + d(ah, bl) + d(al, bh)


@jax.custom_vjp
def _mm3(a, b):
    return _dot3(a, b)


def _mm3_f(a, b):
    return _dot3(a, b), (a, b)


def _mm3_b(res, g):
    a, b = res
    return _dot3(g, b, (((1,), (1,)), ((), ()))), _dot3(a, g, (((0,), (0,)), ((), ())))


_mm3.defvjp(_mm3_f, _mm3_b)


@jax.custom_vjp
def _unit_lower_inv(lms):
    n = lms[0].shape[0]
    row = lax.broadcasted_iota(jnp.int32, lms[0].shape, 0)
    col = lax.broadcasted_iota(jnp.int32, lms[0].shape, 1)
    eye = (row == col).astype(F32)
    accs = [eye - lm for lm in lms]
    pws = list(lms)
    k = 2
    while k < n:
        pws = [_dot3(p, p) for p in pws]
        accs = [a + _dot3(a, p) for a, p in zip(accs, pws)]
        k *= 2
    return tuple(accs)


def _uli_f(lms):
    a = _unit_lower_inv(lms)
    return a, a


def _uli_b(a, g):
    ats = [x.T for x in a]
    tmp = [_dot3(at, gi) for at, gi in zip(ats, g)]
    return (tuple(-_dot3(t, at) for t, at in zip(tmp, ats)),)


_unit_lower_inv.defvjp(_uli_f, _uli_b)


def _roll(x, s):
    n = x.shape[0]
    s = s % n
    return x if s == 0 else pltpu.roll(x, s, 0)


def _conv_taps(ext, w_ref, halo, k_taps, tl):
    acc = None
    for k in range(k_taps):
        term = _roll(ext, (k_taps - 1) - k)[halo:halo + tl] * w_ref[k:k + 1, :]
        acc = term if acc is None else acc + term
    return acc


def _conv_taps_bwd(ext, w_ref, dw_ref, dacc, halo, k_taps, tl):
    dpad = jnp.concatenate([dacc, jnp.zeros((halo, dacc.shape[1]), F32)], axis=0)
    dext = None
    for k in range(k_taps):
        r = _roll(ext, (k_taps - 1) - k)[halo:halo + tl]
        dw_ref[k:k + 1, :] += jnp.sum(r * dacc, axis=0, keepdims=True)
        term = _roll(dpad, halo - (k_taps - 1) + k) * w_ref[k:k + 1, :]
        dext = term if dext is None else dext + term
    return dext


def _add_tail(x, tail):
    tl, h = x.shape[0], tail.shape[0]
    return x + jnp.concatenate([jnp.zeros((tl - h, x.shape[1]), F32), tail], axis=0)


def _proj_fwd(x, g, wp):
    L = x.shape[0]

    def body(x_ref, g_ref, w_ref, p_ref, h_ref):
        hb = _rmsnorm(x_ref[...], g_ref[...]).astype(BF)
        h_ref[...] = hb
        p_ref[...] = jnp.dot(hb, w_ref[...], preferred_element_type=F32)

    return pl.pallas_call(
        body, grid=(L // TL,),
        in_specs=[pl.BlockSpec((TL, D_MODEL), lambda i: (i, 0)),
                  pl.BlockSpec((1, D_MODEL), lambda i: (0, 0)),
                  pl.BlockSpec((D_MODEL, N_INP), lambda i: (0, 0))],
        out_specs=[pl.BlockSpec((TL, N_INP), lambda i: (i, 0)),
                   pl.BlockSpec((TL, D_MODEL), lambda i: (i, 0))],
        out_shape=[jax.ShapeDtypeStruct((L, N_INP), F32), jax.ShapeDtypeStruct((L, D_MODEL), BF)],
        name="proj_fwd", compiler_params=_cp("parallel"))(x, g, wp)


def _proj_bwd_x(x, g, dpa, dpb, dpc, dpd, wp, dxo):
    L = x.shape[0]

    def body(x_ref, g_ref, a_ref, b_ref, c_ref, d_ref, w_ref, dxo_ref, dx_ref, dg_ref):
        dh = None
        for ref, c0, wd in ((a_ref, COL_A, W_A), (b_ref, COL_B, W_B), (c_ref, COL_C, W_C), (d_ref, COL_D, W_D)):
            t = lax.dot_general(ref[...].astype(BF), w_ref[:, c0:c0 + wd], (((1,), (1,)), ((), ())),
                                preferred_element_type=F32)
            dh = t if dh is None else dh + t
        _, vj = jax.vjp(_rmsnorm, x_ref[...], g_ref[...])
        dx, dg = vj(dh)
        dx_ref[...] = dxo_ref[...] + dx

        @pl.when(pl.program_id(0) == 0)
        def _():
            dg_ref[...] = jnp.zeros_like(dg_ref)

        dg_ref[...] += dg

    def rows(wd):
        return pl.BlockSpec((TL, wd), lambda i: (i, 0))

    return pl.pallas_call(
        body, grid=(L // TL,),
        in_specs=[rows(D_MODEL), pl.BlockSpec((1, D_MODEL), lambda i: (0, 0)),
                  rows(W_A), rows(W_B), rows(W_C), rows(W_D),
                  pl.BlockSpec((D_MODEL, N_INP), lambda i: (0, 0)), rows(D_MODEL)],
        out_specs=[rows(D_MODEL), pl.BlockSpec((1, D_MODEL), lambda i: (0, 0))],
        out_shape=[jax.ShapeDtypeStruct((L, D_MODEL), F32), jax.ShapeDtypeStruct((1, D_MODEL), F32)],
        name="proj_bwd_x", compiler_params=_cp("arbitrary"))(x, g, dpa, dpb, dpc, dpd, wp, dxo)


def _mm_tn_acc(a, b, tn, name):
    L, ka = a.shape
    nb = b.shape[1]
    tk = min(512, L)

    def body(a_ref, b_ref, o_ref):
        @pl.when(pl.program_id(1) == 0)
        def _():
            o_ref[...] = jnp.zeros_like(o_ref)

        o_ref[...] += lax.dot_general(a_ref[...].astype(BF), b_ref[...].astype(BF), (((0,), (0,)), ((), ())),
                                      preferred_element_type=F32)

    return pl.pallas_call(
        body, grid=(nb // tn, L // tk),
        in_specs=[pl.BlockSpec((tk, ka), lambda j, t: (t, 0)), pl.BlockSpec((tk, tn), lambda j, t: (t, j))],
        out_specs=pl.BlockSpec((ka, tn), lambda j, t: (0, j)),
        out_shape=jax.ShapeDtypeStruct((ka, nb), F32),
        name=name, compiler_params=_cp("parallel", "arbitrary"))(a, b)


def _dwout(ya, yb, yc, yd, dxo):
    L = dxo.shape[0]
    tk, tn = min(512, L), 512

    def body(a_ref, b_ref, c_ref, d_ref, g_ref, o_ref):
        @pl.when(pl.program_id(1) == 0)
        def _():
            o_ref[...] = jnp.zeros_like(o_ref)

        gb = g_ref[...].astype(BF)
        for j, ref in enumerate((a_ref, b_ref, c_ref, d_ref)):
            o_ref[j * BR:(j + 1) * BR, :] += lax.dot_general(ref[...].astype(BF), gb, (((0,), (0,)), ((), ())),
                                                             preferred_element_type=F32)

    ys = pl.BlockSpec((tk, BR), lambda j, t: (t, 0))
    return pl.pallas_call(
        body, grid=(D_MODEL // tn, L // tk),
        in_specs=[ys, ys, ys, ys, pl.BlockSpec((tk, tn), lambda j, t: (t, j))],
        out_specs=pl.BlockSpec((D_MODEL, tn), lambda j, t: (0, j)),
        out_shape=jax.ShapeDtypeStruct((D_MODEL, D_MODEL), F32),
        name="dwout", compiler_params=_cp("parallel", "arbitrary"))(ya, yb, yc, yd, dxo)


def _out_fwd(x, ya, yb, yc, yd, wo):
    L = x.shape[0]

    def body(x_ref, a_ref, b_ref, c_ref, d_ref, w_ref, o_ref):
        acc = x_ref[...]
        for j, ref in enumerate((a_ref, b_ref, c_ref, d_ref)):
            acc = acc + jnp.dot(ref[...].astype(BF), w_ref[j * BR:(j + 1) * BR, :], preferred_element_type=F32)
        o_ref[...] = acc

    def rows(wd):
        return pl.BlockSpec((TL, wd), lambda i: (i, 0))

    return pl.pallas_call(
        body, grid=(L // TL,),
        in_specs=[rows(D_MODEL), rows(BR), rows(BR), rows(BR), rows(BR),
                  pl.BlockSpec((D_MODEL, D_MODEL), lambda i: (0, 0))],
        out_specs=rows(D_MODEL), out_shape=jax.ShapeDtypeStruct((L, D_MODEL), F32),
        name="out_fwd", compiler_params=_cp("parallel"))(x, ya, yb, yc, yd, wo)


def _out_bwd_x(dxo, wo):
    L = dxo.shape[0]

    def body(d_ref, w_ref, o_ref):
        o_ref[...] = lax.dot_general(d_ref[...].astype(BF), w_ref[...], (((1,), (1,)), ((), ())),
                                     preferred_element_type=F32)

    return pl.pallas_call(
        body, grid=(L // TL,),
        in_specs=[pl.BlockSpec((TL, D_MODEL), lambda i: (i, 0)), pl.BlockSpec((D_MODEL, D_MODEL), lambda i: (0, 0))],
        out_specs=pl.BlockSpec((TL, D_MODEL), lambda i: (i, 0)),
        out_shape=jax.ShapeDtypeStruct((L, D_MODEL), F32),
        name="out_bwd_x", compiler_params=_cp("parallel"))(dxo, wo)


def _loss_bwd(x, g, tgt):
    L = x.shape[0]

    def f(xv, gv, tv):
        err = _rmsnorm(xv, gv) - tv
        return 0.5 * jnp.sum(jnp.mean(err * err, axis=-1, keepdims=True), axis=0, keepdims=True)

    def body(x_ref, g_ref, t_ref, loss_ref, dx_ref, dg_ref):
        tv = t_ref[...]
        loss, vj = jax.vjp(lambda a, b: f(a, b, tv), x_ref[...], g_ref[...])
        dx, dg = vj(jnp.ones((1, 1), F32))
        dx_ref[...] = dx

        @pl.when(pl.program_id(0) == 0)
        def _():
            dg_ref[...] = jnp.zeros_like(dg_ref)
            loss_ref[...] = jnp.zeros_like(loss_ref)

        dg_ref[...] += dg
        loss_ref[...] += jnp.broadcast_to(loss, loss_ref.shape)

    return pl.pallas_call(
        body, grid=(L // TL,),
        in_specs=[pl.BlockSpec((TL, D_MODEL), lambda i: (i, 0)), pl.BlockSpec((1, D_MODEL), lambda i: (0, 0)),
                  pl.BlockSpec((TL, D_MODEL), lambda i: (i, 0))],
        out_specs=[pl.BlockSpec((8, 128), lambda i: (0, 0)), pl.BlockSpec((TL, D_MODEL), lambda i: (i, 0)),
                   pl.BlockSpec((1, D_MODEL), lambda i: (0, 0))],
        out_shape=[jax.ShapeDtypeStruct((8, 128), F32), jax.ShapeDtypeStruct((L, D_MODEL), F32),
                   jax.ShapeDtypeStruct((1, D_MODEL), F32)],
        name="loss_bwd", compiler_params=_cp("arbitrary"))(x, g, tgt)


def _a_pre(val, gate):
    return val * _sigmoid(gate)


def _a_post(acc, az, cb, lng, lnb, pw, pwb):
    t = acc + cb
    mu = jnp.mean(t, axis=-1, keepdims=True)
    xc = t - mu
    ln = xc * lax.rsqrt(jnp.mean(xc * xc, axis=-1, keepdims=True) + EPS) * lng + lnb
    return (_mm(_silu(ln), pw) + pwb) * _silu(az)


def _halo_map(tl, halo, col):
    r = tl // halo
    return lambda i: (jnp.maximum(i * r - 1, 0), col)


def _a_fwd(proj, cw, cb, lng, lnb, pw, pwb):
    L = proj.shape[0]

    def body(vg_ref, az_ref, hvg_ref, cw_ref, cb_ref, lng_ref, lnb_ref, pw_ref, pwb_ref, o_ref):
        keep = (pl.program_id(0) > 0).astype(F32)
        a_h = _a_pre(hvg_ref[:, 0:BR], hvg_ref[:, BR:2 * BR]) * keep
        a_t = _a_pre(vg_ref[:, 0:BR], vg_ref[:, BR:2 * BR])
        ext = jnp.concatenate([a_h, a_t], axis=0)
        acc = _conv_taps(ext, cw_ref, HALO_A, K_A, TL)
        o_ref[...] = _a_post(acc, az_ref[...], cb_ref[...], lng_ref[...], lnb_ref[...], pw_ref[...], pwb_ref[...])

    vec = pl.BlockSpec((1, BR), lambda i: (0, 0))
    return pl.pallas_call(
        body, grid=(L // TL,),
        in_specs=[pl.BlockSpec((TL, 2 * BR), lambda i: (i, 0)), pl.BlockSpec((TL, BR), lambda i: (i, 2)),
                  pl.BlockSpec((HALO_A, 2 * BR), _halo_map(TL, HALO_A, 0)),
                  pl.BlockSpec((HALO_A, BR), lambda i: (0, 0)), vec, vec, vec,
                  pl.BlockSpec((BR, BR), lambda i: (0, 0)), vec],
        out_specs=pl.BlockSpec((TL, BR), lambda i: (i, 0)),
        out_shape=jax.ShapeDtypeStruct((L, BR), F32),
        name="a_fwd", compiler_params=_cp("parallel"))(proj, proj, proj, cw, cb, lng, lnb, pw, pwb)


def _a_bwd(proj, dmix, cw, cb, lng, lnb, pw, pwb):
    L = proj.shape[0]
    n = L // TL

    def body(vg_ref, az_ref, hvg_ref, dy_ref, cw_ref, cb_ref, lng_ref, lnb_ref, pw_ref, pwb_ref,
             dp_ref, dcw_ref, dcb_ref, dlng_ref, dlnb_ref, dpw_ref, dpwb_ref, carry_ref):
        i = pl.program_id(0)

        @pl.when(i == 0)
        def _():
            carry_ref[...] = jnp.zeros_like(carry_ref)
            for r in (dcw_ref, dcb_ref, dlng_ref, dlnb_ref, dpw_ref, dpwb_ref):
                r[...] = jnp.zeros_like(r)

        keep = (i < n - 1).astype(F32)
        val, gate = vg_ref[:, 0:BR], vg_ref[:, BR:2 * BR]
        a_h = _a_pre(hvg_ref[:, 0:BR], hvg_ref[:, BR:2 * BR]) * keep
        a_t, vj_pre = jax.vjp(_a_pre, val, gate)
        ext = jnp.concatenate([a_h, a_t], axis=0)
        acc = _conv_taps(ext, cw_ref, HALO_A, K_A, TL)
        _, vj_post = jax.vjp(_a_post, acc, az_ref[...], cb_ref[...], lng_ref[...], lnb_ref[...], pw_ref[...],
                             pwb_ref[...])
        dacc, daz, dcb, dlng, dlnb, dpw, dpwb = vj_post(dy_ref[...])
        dext = _conv_taps_bwd(ext, cw_ref, dcw_ref, dacc, HALO_A, K_A, TL)
        da = _add_tail(dext[HALO_A:], carry_ref[...])
        carry_ref[...] = dext[:HALO_A]
        dval, dgate = vj_pre(da)
        dp_ref[:, 0:BR] = dval
        dp_ref[:, BR:2 * BR] = dgate
        dp_ref[:, 2 * BR:3 * BR] = daz
        dcb_ref[...] += dcb
        dlng_ref[...] += dlng
        dlnb_ref[...] += dlnb
        dpw_ref[...] += dpw
        dpwb_ref[...] += dpwb

    rev = lambda i: n - 1 - i
    vec = pl.BlockSpec((1, BR), lambda i: (0, 0))
    hmap = _halo_map(TL, HALO_A, 0)
    return pl.pallas_call(
        body, grid=(n,),
        in_specs=[pl.BlockSpec((TL, 2 * BR), lambda i: (rev(i), 0)), pl.BlockSpec((TL, BR), lambda i: (rev(i), 2)),
                  pl.BlockSpec((HALO_A, 2 * BR), lambda i: hmap(rev(i))),
                  pl.BlockSpec((TL, BR), lambda i: (rev(i), 0)),
                  pl.BlockSpec((HALO_A, BR), lambda i: (0, 0)), vec, vec, vec,
                  pl.BlockSpec((BR, BR), lambda i: (0, 0)), vec],
        out_specs=[pl.BlockSpec((TL, W_A), lambda i: (rev(i), 0)),
                   pl.BlockSpec((HALO_A, BR), lambda i: (0, 0)), vec, vec, vec,
                   pl.BlockSpec((BR, BR), lambda i: (0, 0)), vec],
        out_shape=[jax.ShapeDtypeStruct((L, W_A), F32), jax.ShapeDtypeStruct((HALO_A, BR), F32)]
        + [jax.ShapeDtypeStruct((1, BR), F32)] * 3
        + [jax.ShapeDtypeStruct((BR, BR), F32), jax.ShapeDtypeStruct((1, BR), F32)],
        scratch_shapes=[pltpu.VMEM((HALO_A, BR), F32)],
        name="a_bwd", compiler_params=_cp("arbitrary"))(proj, proj, proj, dmix, cw, cb, lng, lnb, pw, pwb)


def _c_pre(cg, xc):
    return cg * xc


def _c_post(acc, bg, cz):
    return bg * acc * _silu(cz)


def _c_fwd(proj, cw):
    L = proj.shape[0]

    def body(bg_ref, cx_ref, cz_ref, hcx_ref, cw_ref, o_ref):
        keep = (pl.program_id(0) > 0).astype(F32)
        p_h = _c_pre(hcx_ref[:, 0:BR], hcx_ref[:, BR:2 * BR]) * keep
        p_t = _c_pre(cx_ref[:, 0:BR], cx_ref[:, BR:2 * BR])
        ext = jnp.concatenate([p_h, p_t], axis=0)
        acc = _conv_taps(ext, cw_ref, HALO_S, K_C, TL)
        o_ref[...] = _c_post(acc, bg_ref[...], cz_ref[...])

    return pl.pallas_call(
        body, grid=(L // TL,),
        in_specs=[pl.BlockSpec((TL, BR), lambda i: (i, 5)), pl.BlockSpec((TL, 2 * BR), lambda i: (i, 3)),
                  pl.BlockSpec((TL, BR), lambda i: (i, 8)),
                  pl.BlockSpec((HALO_S, 2 * BR), _halo_map(TL, HALO_S, 3)),
                  pl.BlockSpec((HALO_S, BR), lambda i: (0, 0))],
        out_specs=pl.BlockSpec((TL, BR), lambda i: (i, 0)),
        out_shape=jax.ShapeDtypeStruct((L, BR), F32),
        name="c_fwd", compiler_params=_cp("parallel"))(proj, proj, proj, proj, cw)


def _c_bwd(proj, dmix, cw):
    L = proj.shape[0]
    n = L // TL

    def body(bg_ref, cx_ref, cz_ref, hcx_ref, dy_ref, cw_ref, dp_ref, dcw_ref, carry_ref):
        i = pl.program_id(0)

        @pl.when(i == 0)
        def _():
            carry_ref[...] = jnp.zeros_like(carry_ref)
            dcw_ref[...] = jnp.zeros_like(dcw_ref)

        keep = (i < n - 1).astype(F32)
        p_h = _c_pre(hcx_ref[:, 0:BR], hcx_ref[:, BR:2 * BR]) * keep
        p_t, vj_pre = jax.vjp(_c_pre, cx_ref[:, 0:BR], cx_ref[:, BR:2 * BR])
        ext = jnp.concatenate([p_h, p_t], axis=0)
        acc = _conv_taps(ext, cw_ref, HALO_S, K_C, TL)
        _, vj_post = jax.vjp(_c_post, acc, bg_ref[...], cz_ref[...])
        dacc, dbg, dcz = vj_post(dy_ref[...])
        dext = _conv_taps_bwd(ext, cw_ref, dcw_ref, dacc, HALO_S, K_C, TL)
        dp = _add_tail(dext[HALO_S:], carry_ref[...])
        carry_ref[...] = dext[:HALO_S]
        dcg, dxc = vj_pre(dp)
        dp_ref[:, 0:BR] = dbg
        dp_ref[:, BR:2 * BR] = dcg
        dp_ref[:, 2 * BR:3 * BR] = dxc
        dp_ref[:, 3 * BR:4 * BR] = dcz

    rev = lambda i: n - 1 - i
    hmap = _halo_map(TL, HALO_S, 3)
    return pl.pallas_call(
        body, grid=(n,),
        in_specs=[pl.BlockSpec((TL, BR), lambda i: (rev(i), 5)), pl.BlockSpec((TL, 2 * BR), lambda i: (rev(i), 3)),
                  pl.BlockSpec((TL, BR), lambda i: (rev(i), 8)),
                  pl.BlockSpec((HALO_S, 2 * BR), lambda i: hmap(rev(i))),
                  pl.BlockSpec((TL, BR), lambda i: (rev(i), 2)),
                  pl.BlockSpec((HALO_S, BR), lambda i: (0, 0))],
        out_specs=[pl.BlockSpec((TL, W_C), lambda i: (rev(i), 0)), pl.BlockSpec((HALO_S, BR), lambda i: (0, 0))],
        out_shape=[jax.ShapeDtypeStruct((L, W_C), F32), jax.ShapeDtypeStruct((HALO_S, BR), F32)],
        scratch_shapes=[pltpu.VMEM((HALO_S, BR), F32)],
        name="c_bwd", compiler_params=_cp("arbitrary"))(proj, proj, proj, proj, dmix, cw)


def _s5_prep_fn(lre, lim, ldt, bre, bim, cre, cim):
    grp = lax.broadcasted_iota(jnp.int32, (128, NS), 0)
    lane = lax.broadcasted_iota(jnp.int32, (128, NS), 1)
    expand = (grp == lane // S5_P).astype(F32)
    dt = jnp.exp(_dot_hi(jnp.broadcast_to(ldt, (8, 128)), expand)[0:1])
    lr = jnp.minimum(lre, -1e-4)
    mag = jnp.exp(lr * dt)
    ar = mag * jnp.cos(lim * dt)
    ai = mag * jnp.sin(lim * dt)
    den = lr * lr + lim * lim
    fr = ((ar - 1.0) * lr + ai * lim) / den
    fi = (ai * lr - (ar - 1.0) * lim) / den
    bbr = fr * bre - fi * bim
    bbi = fr * bim + fi * bre
    row = lax.broadcasted_iota(jnp.int32, (BR, NS), 0)
    col = lax.broadcasted_iota(jnp.int32, (BR, NS), 1)
    blk = (row // S5_H == col // S5_P).astype(F32)

    def embed(t):
        return jnp.concatenate([t] * S5_G, axis=0) * blk

    bemb = jnp.concatenate([embed(bbr), embed(bbi)], axis=1)
    cemb = jnp.concatenate([embed(cre), embed(-cim)], axis=1)
    return ar, ai, bemb, cemb


def _s5_prep(lre, lim, ldt, bre, bim, cre, cim):
    def body(*refs):
        outs = _s5_prep_fn(*[r[...] for r in refs[:7]])
        for r, o in zip(refs[7:], outs):
            r[...] = o

    return pl.pallas_call(
        body,
        out_shape=[jax.ShapeDtypeStruct((1, NS), F32)] * 2 + [jax.ShapeDtypeStruct((BR, 2 * NS), F32)] * 2,
        name="s5_prep", compiler_params=pltpu.CompilerParams(vmem_limit_bytes=VMEM_LIMIT),
    )(lre, lim, ldt, bre, bim, cre, cim)


def _s5_prep_bwd(lre, lim, ldt, bre, bim, cre, cim, dar, dai, dbemb, dcemb):
    def body(*refs):
        _, vj = jax.vjp(_s5_prep_fn, *[r[...] for r in refs[:7]])
        grads = vj(tuple(r[...] for r in refs[7:11]))
        for r, o in zip(refs[11:], grads):
            r[...] = o

    return pl.pallas_call(
        body,
        out_shape=[jax.ShapeDtypeStruct((1, NS), F32)] * 2 + [jax.ShapeDtypeStruct((1, 128), F32)]
        + [jax.ShapeDtypeStruct((S5_H, NS), F32)] * 4,
        name="s5_prep_bwd", compiler_params=pltpu.CompilerParams(vmem_limit_bytes=VMEM_LIMIT),
    )(lre, lim, ldt, bre, bim, cre, cim, dar, dai, dbemb, dcemb)


def _s5_scan(xr, xi, ar, ai, reverse):
    n = xr.shape[0]
    row = lax.broadcasted_iota(jnp.int32, (n, 1), 0)
    pr, pi = ar, ai
    d = 1
    while d < n:
        if reverse:
            m = row < n - d
            sr = jnp.where(m, _roll(xr, n - d), 0.0)
            si = jnp.where(m, _roll(xi, n - d), 0.0)
        else:
            m = row >= d
            sr = jnp.where(m, _roll(xr, d), 0.0)
            si = jnp.where(m, _roll(xi, d), 0.0)
        xr, xi = xr + pr * sr - pi * si, xi + pr * si + pi * sr
        pr, pi = pr * pr - pi * pi, 2.0 * pr * pi
        d *= 2
    return xr, xi


def _s5_states(u, bemb_b, ar, ai, sin_r, sin_i):
    bu = jnp.dot(u.astype(BF), bemb_b, preferred_element_type=F32)
    first = lax.broadcasted_iota(jnp.int32, (u.shape[0], 1), 0) == 0
    xr = bu[:, :NS] + jnp.where(first, ar * sin_r - ai * sin_i, 0.0)
    xi = bu[:, NS:] + jnp.where(first, ar * sin_i + ai * sin_r, 0.0)
    return _s5_scan(xr, xi, ar, ai, False)


def _b_post(yssm, u, bz, dsk, gw, gb):
    z = jax.nn.gelu(yssm + dsk * u)
    return z * _sigmoid(_mm(z, gw) + gb) * _silu(bz)


def _b_fwd(proj, ar, ai, bemb, cemb, dsk, gw, gb):
    L = proj.shape[0]
    n = L // TL

    def body(u_ref, bz_ref, ar_ref, ai_ref, be_ref, ce_ref, dsk_ref, gw_ref, gb_ref, o_ref, sin_ref, carry_ref):
        @pl.when(pl.program_id(0) == 0)
        def _():
            carry_ref[...] = jnp.zeros_like(carry_ref)

        sin = carry_ref[...]
        sin_ref[0] = sin
        u = u_ref[...]
        sr, si = _s5_states(u, be_ref[...].astype(BF), ar_ref[...], ai_ref[...], sin[:, :NS], sin[:, NS:])
        carry_ref[:, :NS] = sr[TL - 1:TL]
        carry_ref[:, NS:] = si[TL - 1:TL]
        s = jnp.concatenate([sr, si], axis=1).astype(BF)
        yssm = lax.dot_general(s, ce_ref[...].astype(BF), (((1,), (1,)), ((), ())), preferred_element_type=F32)
        o_ref[...] = _b_post(yssm, u, bz_ref[...], dsk_ref[...], gw_ref[...], gb_ref[...])

    vec = pl.BlockSpec((1, BR), lambda i: (0, 0))
    svec = pl.BlockSpec((1, NS), lambda i: (0, 0))
    emb = pl.BlockSpec((BR, 2 * NS), lambda i: (0, 0))
    return pl.pallas_call(
        body, grid=(n,),
        in_specs=[pl.BlockSpec((TL, BR), lambda i: (i, 3)), pl.BlockSpec((TL, BR), lambda i: (i, 4)),
                  svec, svec, emb, emb, vec, pl.BlockSpec((BR, BR), lambda i: (0, 0)), vec],
        out_specs=[pl.BlockSpec((TL, BR), lambda i: (i, 0)), pl.BlockSpec((1, 1, 2 * NS), lambda i: (i, 0, 0))],
        out_shape=[jax.ShapeDtypeStruct((L, BR), F32), jax.ShapeDtypeStruct((n, 1, 2 * NS), F32)],
        scratch_shapes=[pltpu.VMEM((1, 2 * NS), F32)],
        name="b_fwd", compiler_params=_cp("arbitrary"))(proj, proj, ar, ai, bemb, cemb, dsk, gw, gb)


def _b_bwd(proj, dmix, sin_all, ar, ai, bemb, cemb, dsk, gw, gb):
    L = proj.shape[0]
    n = L // TL

    def body(u_ref, bz_ref, dy_ref, sin_ref, ar_ref, ai_ref, be_ref, ce_ref, dsk_ref, gw_ref, gb_ref,
             dp_ref, dar_ref, dai_ref, dbe_ref, dce_ref, ddsk_ref, dgw_ref, dgb_ref, carry_ref):
        i = pl.program_id(0)

        @pl.when(i == 0)
        def _():
            carry_ref[...] = jnp.zeros_like(carry_ref)
            for r in (dar_ref, dai_ref, dbe_ref, dce_ref, ddsk_ref, dgw_ref, dgb_ref):
                r[...] = jnp.zeros_like(r)

        u = u_ref[...]
        ar, ai = ar_ref[...], ai_ref[...]
        be_b, ce_b = be_ref[...].astype(BF), ce_ref[...].astype(BF)
        sin = sin_ref[0]
        sr, si = _s5_states(u, be_b, ar, ai, sin[:, :NS], sin[:, NS:])
        s_b = jnp.concatenate([sr, si], axis=1).astype(BF)
        yssm = lax.dot_general(s_b, ce_b, (((1,), (1,)), ((), ())), preferred_element_type=F32)
        _, vj = jax.vjp(_b_post, yssm, u, bz_ref[...], dsk_ref[...], gw_ref[...], gb_ref[...])
        dyssm, du, dbz, ddsk, dgw, dgb = vj(dy_ref[...])
        dy_b = dyssm.astype(BF)
        dce_ref[...] += lax.dot_general(dy_b, s_b, (((0,), (0,)), ((), ())), preferred_element_type=F32)
        gs = jnp.dot(dy_b, ce_b, preferred_element_type=F32)
        last = lax.broadcasted_iota(jnp.int32, (TL, 1), 0) == TL - 1
        cr, ci = carry_ref[:, :NS], carry_ref[:, NS:]
        gr = gs[:, :NS] + jnp.where(last, ar * cr + ai * ci, 0.0)
        gi = gs[:, NS:] + jnp.where(last, ar * ci - ai * cr, 0.0)
        dsr, dsi = _s5_scan(gr, gi, ar, -ai, True)
        carry_ref[:, :NS] = dsr[0:1]
        carry_ref[:, NS:] = dsi[0:1]
        first = lax.broadcasted_iota(jnp.int32, (TL, 1), 0) == 0
        pr = jnp.where(first, sin[:, :NS], _roll(sr, 1))
        pi = jnp.where(first, sin[:, NS:], _roll(si, 1))
        dar_ref[...] += jnp.sum(dsr * pr + dsi * pi, axis=0, keepdims=True)
        dai_ref[...] += jnp.sum(dsi * pr - dsr * pi, axis=0, keepdims=True)
        ds_b = jnp.concatenate([dsr, dsi], axis=1).astype(BF)
        dbe_ref[...] += lax.dot_general(u.astype(BF), ds_b, (((0,), (0,)), ((), ())), preferred_element_type=F32)
        du = du + lax.dot_general(ds_b, be_b, (((1,), (1,)), ((), ())), preferred_element_type=F32)
        dp_ref[:, 0:BR] = du
        dp_ref[:, BR:2 * BR] = dbz
        ddsk_ref[...] += ddsk
        dgw_ref[...] += dgw
        dgb_ref[...] += dgb

    rev = lambda i: n - 1 - i
    vec = pl.BlockSpec((1, BR), lambda i: (0, 0))
    svec = pl.BlockSpec((1, NS), lambda i: (0, 0))
    emb = pl.BlockSpec((BR, 2 * NS), lambda i: (0, 0))
    mat = pl.BlockSpec((BR, BR), lambda i: (0, 0))
    return pl.pallas_call(
        body, grid=(n,),
        in_specs=[pl.BlockSpec((TL, BR), lambda i: (rev(i), 3)), pl.BlockSpec((TL, BR), lambda i: (rev(i), 4)),
                  pl.BlockSpec((TL, BR), lambda i: (rev(i), 1)),
                  pl.BlockSpec((1, 1, 2 * NS), lambda i: (rev(i), 0, 0)),
                  svec, svec, emb, emb, vec, mat, vec],
        out_specs=[pl.BlockSpec((TL, W_B), lambda i: (rev(i), 0)), svec, svec, emb, emb, vec, mat, vec],
        out_shape=[jax.ShapeDtypeStruct((L, W_B), F32)] + [jax.ShapeDtypeStruct((1, NS), F32)] * 2
        + [jax.ShapeDtypeStruct((BR, 2 * NS), F32)] * 2
        + [jax.ShapeDtypeStruct((1, BR), F32), jax.ShapeDtypeStruct((BR, BR), F32), jax.ShapeDtypeStruct((1, BR), F32)],
        scratch_shapes=[pltpu.VMEM((1, 2 * NS), F32)],
        name="b_bwd", compiler_params=_cp("arbitrary"))(proj, proj, dmix, sin_all, ar, ai, bemb, cemb, dsk, gw, gb)


def _d_post(cq, ab, dz, s0, s1, s2, s3, p1, p2, ng):
    c = cq.shape[0]
    qkv = _silu(cq)
    gall = -jnp.exp(p1) * jax.nn.softplus(ab + p2)
    ball = _sigmoid(ab)
    row = lax.broadcasted_iota(jnp.int32, (c, c), 0)
    col = lax.broadcasted_iota(jnp.int32, (c, c), 1)
    causal = row >= col
    strict = row > col
    gc_all = _dot_hi(causal.astype(F32), gall)
    gc_t = gc_all.T
    heads = range(DN_H)
    st = (s0, s1, s2, s3)
    q = [qkv[:, h * DN_D:(h + 1) * DN_D] for h in heads]
    k = [qkv[:, BR + h * DN_D:BR + (h + 1) * DN_D] for h in heads]
    v = [qkv[:, 2 * BR + h * DN_D:2 * BR + (h + 1) * DN_D] for h in heads]
    q = [t * lax.rsqrt(jnp.sum(t * t, axis=-1, keepdims=True) + EPS) * (DN_D ** -0.5) for t in q]
    k = [t * lax.rsqrt(jnp.sum(t * t, axis=-1, keepdims=True) + EPS) for t in k]
    gcol = [gc_all[:, h:h + 1] for h in heads]
    beta = [ball[:, DN_H + h:DN_H + h + 1] for h in heads]
    decay = [jnp.where(causal, jnp.exp(jnp.where(causal, gcol[h] - gc_t[h:h + 1, :], 0.0)), 0.0) for h in heads]
    kb = [k[h] * beta[h] for h in heads]
    lm = [jnp.where(strict, _mm_nt(kb[h], k[h]) * decay[h], 0.0) for h in heads]
    ainv = _unit_lower_inv(tuple(lm))
    egc = [jnp.exp(g) for g in gcol]
    uw = [_mm3(ainv[h], jnp.concatenate([v[h] * beta[h], kb[h] * egc[h]], axis=1)) for h in heads]
    attn = [_mm_nt(q[h], k[h]) * decay[h] for h in heads]
    glast = [g[c - 1:c, :] for g in gcol]
    kd = [k[h] * jnp.exp(glast[h] - gcol[h]) for h in heads]
    vnew = [uw[h][:, :DN_D] - _mm(uw[h][:, DN_D:], st[h]) for h in heads]
    o = [_mm(q[h] * egc[h], st[h]) + _mm(attn[h], vnew[h]) for h in heads]
    news = [st[h] * jnp.exp(glast[h]) + _mm_tn(kd[h], vnew[h]) for h in heads]
    outs = [t * lax.rsqrt(jnp.mean(t * t, axis=-1, keepdims=True) + EPS) * ng for t in o]
    yd = jnp.concatenate(outs, axis=1) * _silu(dz)
    return (yd, *news)


def _d_fwd(proj, cw, p1, p2, ng, exchange=None):
    L = proj.shape[0]
    n = L // DN_C

    def body(qkv_ref, ab_ref, dz_ref, hq_ref, cw_ref, p1_ref, p2_ref, ng_ref, o_ref, sall_ref, s_ref):
        i = pl.program_id(0)

        @pl.when(i == 0)
        def _():
            s_ref[...] = jnp.zeros_like(s_ref)

        keep = (i > 0).astype(F32)
        ext = jnp.concatenate([hq_ref[...] * keep, qkv_ref[...]], axis=0)
        cq = _conv_taps(ext, cw_ref, HALO_S, K_DN, DN_C)
        st = [s_ref[h] for h in range(DN_H)]
        for h in range(DN_H):
            sall_ref[0, h] = st[h]
        out = _d_post(cq, ab_ref[...], dz_ref[...], *st, p1_ref[...], p2_ref[...], ng_ref[...])
        o_ref[...] = out[0]
        for h in range(DN_H):
            s_ref[h] = out[1 + h]

    return _sweep_with_exchange(
        body, n,
        in_specs=[pl.BlockSpec((DN_C, 3 * BR), lambda i: (i, 3)), pl.BlockSpec((DN_C, 128), lambda i: (i, 26)),
                  pl.BlockSpec((DN_C, BR), lambda i: (i, 12)),
                  pl.BlockSpec((HALO_S, 3 * BR), _halo_map(DN_C, HALO_S, 3)),
                  pl.BlockSpec((HALO_S, 3 * BR), lambda i: (0, 0)),
                  pl.BlockSpec((1, 128), lambda i: (0, 0)), pl.BlockSpec((1, 128), lambda i: (0, 0)),
                  pl.BlockSpec((1, DN_D), lambda i: (0, 0))],
        out_specs=[pl.BlockSpec((DN_C, BR), lambda i: (i, 0)),
                   pl.BlockSpec((1, DN_H, DN_D, DN_D), lambda i: (i, 0, 0, 0))],
        out_shape=[jax.ShapeDtypeStruct((L, BR), F32), jax.ShapeDtypeStruct((n, DN_H, DN_D, DN_D), F32)],
        scratch_shapes=[pltpu.VMEM((DN_H, DN_D, DN_D), F32)],
        args=(proj, proj, proj, proj, cw, p1, p2, ng), exchange=exchange, name="d_fwd")


def _sweep_with_exchange(body, steps, in_specs, out_specs, out_shape, scratch_shapes, args, exchange, name):
    if exchange is None:
        res = pl.pallas_call(body, grid=(steps,), in_specs=in_specs, out_specs=out_specs, out_shape=out_shape,
                             scratch_shapes=scratch_shapes, name=name, compiler_params=_cp("arbitrary"))(*args)
        return res, None
    xs, axes, mode = exchange
    ex = _Exchange(xs, axes, mode)
    ni, no, ns, na = len(in_specs), len(out_specs), len(scratch_shapes), ex.na

    def carried(*refs):
        ins, xin = refs[:ni], refs[ni:ni + na]
        outs, xout = refs[ni + na:ni + na + no], refs[ni + na + no:ni + 2 * na + no]
        scr, sems = refs[ni + 2 * na + no:ni + 2 * na + no + ns], refs[ni + 2 * na + no + ns:]

        @pl.when(pl.program_id(0) == 0)
        def _():
            ex.start(xin, xout, sems)

        body(*ins, *outs, *scr)

        @pl.when(pl.program_id(0) == steps - 1)
        def _():
            ex.wait(xin, xout, sems)

    res = pl.pallas_call(carried, grid=(steps,), in_specs=list(in_specs) + ex.in_specs,
                         out_specs=list(out_specs) + ex.out_specs, out_shape=list(out_shape) + ex.out_shape,
                         scratch_shapes=list(scratch_shapes) + ex.scratch_shapes, name=name + "_x",
                         compiler_params=_cp("arbitrary"))(*args, *xs)
    return res[:no], res[no:]


def _d_bwd(proj, dmix, sall, cw, p1, p2, ng, exchange=None):
    L = proj.shape[0]
    n = L // DN_C

    def body(qkv_ref, ab_ref, dz_ref, hq_ref, dy_ref, sall_ref, cw_ref, p1_ref, p2_ref, ng_ref,
             dp_ref, dcw_ref, dp1_ref, dp2_ref, dng_ref, ds_ref, carry_ref):
        i = pl.program_id(0)

        @pl.when(i == 0)
        def _():
            ds_ref[...] = jnp.zeros_like(ds_ref)
            carry_ref[...] = jnp.zeros_like(carry_ref)
            for r in (dcw_ref, dp1_ref, dp2_ref, dng_ref):
                r[...] = jnp.zeros_like(r)

        keep = (i < n - 1).astype(F32)
        ext = jnp.concatenate([hq_ref[...] * keep, qkv_ref[...]], axis=0)
        cq = _conv_taps(ext, cw_ref, HALO_S, K_DN, DN_C)
        st = [sall_ref[0, h] for h in range(DN_H)]
        _, vj = jax.vjp(_d_post, cq, ab_ref[...], dz_ref[...], *st, p1_ref[...], p2_ref[...], ng_ref[...])
        grads = vj((dy_ref[...], *[ds_ref[h] for h in range(DN_H)]))
        dcq, dab, ddz = grads[0], grads[1], grads[2]
        for h in range(DN_H):
            ds_ref[h] = grads[3 + h]
        dp1_ref[...] += grads[7]
        dp2_ref[...] += grads[8]
        dng_ref[...] += grads[9]
        dext = _conv_taps_bwd(ext, cw_ref, dcw_ref, dcq, HALO_S, K_DN, DN_C)
        dp_ref[:, 0:3 * BR] = _add_tail(dext[HALO_S:], carry_ref[...])
        carry_ref[...] = dext[:HALO_S]
        dp_ref[:, 3 * BR:4 * BR] = ddz
        dp_ref[:, 4 * BR:4 * BR + 128] = dab

    rev = lambda i: n - 1 - i
    hmap = _halo_map(DN_C, HALO_S, 3)
    v128 = pl.BlockSpec((1, 128), lambda i: (0, 0))
    return _sweep_with_exchange(
        body, n,
        in_specs=[pl.BlockSpec((DN_C, 3 * BR), lambda i: (rev(i), 3)), pl.BlockSpec((DN_C, 128), lambda i: (rev(i), 26)),
                  pl.BlockSpec((DN_C, BR), lambda i: (rev(i), 12)),
                  pl.BlockSpec((HALO_S, 3 * BR), lambda i: hmap(rev(i))),
                  pl.BlockSpec((DN_C, BR), lambda i: (rev(i), 3)),
                  pl.BlockSpec((1, DN_H, DN_D, DN_D), lambda i: (rev(i), 0, 0, 0)),
                  pl.BlockSpec((HALO_S, 3 * BR), lambda i: (0, 0)), v128, v128,
                  pl.BlockSpec((1, DN_D), lambda i: (0, 0))],
        out_specs=[pl.BlockSpec((DN_C, W_D), lambda i: (rev(i), 0)),
                   pl.BlockSpec((HALO_S, 3 * BR), lambda i: (0, 0)), v128, v128,
                   pl.BlockSpec((1, DN_D), lambda i: (0, 0))],
        out_shape=[jax.ShapeDtypeStruct((L, W_D), F32), jax.ShapeDtypeStruct((HALO_S, 3 * BR), F32),
                   jax.ShapeDtypeStruct((1, 128), F32), jax.ShapeDtypeStruct((1, 128), F32),
                   jax.ShapeDtypeStruct((1, DN_D), F32)],
        scratch_shapes=[pltpu.VMEM((DN_H, DN_D, DN_D), F32), pltpu.VMEM((HALO_S, 3 * BR), F32)],
        args=(proj, proj, proj, proj, dmix, sall, cw, p1, p2, ng), exchange=exchange, name="d_bwd")


def _pick_rows(rows, cap):
    best = 8
    for t in range(8, cap + 1, 8):
        if rows % t == 0:
            best = t
    return best


def _adamw(w, g, m, v, name):
    rows, wd = w.shape
    tr = _pick_rows(rows, 512)
    c1 = 1.0 - ADAM_B1 ** ADAM_STEP
    c2 = 1.0 - ADAM_B2 ** ADAM_STEP

    def body(w_ref, g_ref, m_ref, v_ref, d_ref, mo_ref, vo_ref):
        gv = g_ref[...]
        mn = ADAM_B1 * m_ref[...] + (1.0 - ADAM_B1) * gv
        vn = ADAM_B2 * v_ref[...] + (1.0 - ADAM_B2) * (gv * gv)
        d_ref[...] = -ADAM_LR * ((mn / c1) / (jnp.sqrt(vn / c2) + ADAM_EPS) + ADAM_WD * w_ref[...])
        mo_ref[...] = mn
        vo_ref[...] = vn

    spec = pl.BlockSpec((tr, wd), lambda i: (i, 0))
    return pl.pallas_call(
        body, grid=(rows // tr,), in_specs=[spec] * 4, out_specs=[spec] * 3,
        out_shape=[jax.ShapeDtypeStruct((rows, wd), F32)] * 3,
        name=name, compiler_params=_cp("parallel"))(w, g, m, v)


def _sum_slots(r, name):
    n, rows, wd = r.shape
    tr = _pick_rows(rows, 384)

    def body(r_ref, o_ref):
        acc = r_ref[0].astype(F32)
        for j in range(1, n):
            acc = acc + r_ref[j].astype(F32)
        o_ref[...] = acc

    return pl.pallas_call(
        body, grid=(rows // tr,),
        in_specs=[pl.BlockSpec((n, tr, wd), lambda i: (0, i, 0))],
        out_specs=pl.BlockSpec((tr, wd), lambda i: (i, 0)),
        out_shape=jax.ShapeDtypeStruct((rows, wd), F32),
        name=name, compiler_params=_cp("parallel"))(r)


AXES = ("x", "y", "c")


def _group_peer(axes, k):
    pos = {a: lax.axis_index(a) for a in AXES}
    idx = 0
    for a in axes:
        idx = idx * 2 + pos[a]
    peer = dict(pos)
    for b, a in enumerate(reversed(axes)):
        if (k >> b) & 1:
            peer[a] = 1 - pos[a]
    return idx, tuple(peer[a] for a in AXES)


MAX_CHUNKS = 4


class _Exchange:
    def __init__(self, xs, axes, mode):
        self.axes, self.mode, self.na, self.n = axes, mode, len(xs), 2 ** len(axes)
        n = self.n
        self.out_shape, self.pieces = [], []
        for x in xs:
            if mode == "gather":
                shape, lead = (n,) + x.shape, x.shape[0]
            elif mode == "scatter":
                shape, lead = x.shape, x.shape[1]
            else:
                shape, lead = (x.shape[0], n * x.shape[1], x.shape[2]), x.shape[0]
            self.out_shape.append(jax.ShapeDtypeStruct(shape, x.dtype))
            big = x.size * x.dtype.itemsize >= (1 << 20)
            if mode == "rows":
                self.pieces.append(lead if lead <= MAX_CHUNKS else 1)
            else:
                self.pieces.append(MAX_CHUNKS if big and lead % (16 * MAX_CHUNKS) == 0 else 1)
        self.in_specs = [pl.BlockSpec(memory_space=pl.ANY)] * self.na
        self.out_specs = [pl.BlockSpec(memory_space=pl.ANY)] * self.na
        self.scratch_shapes = [pltpu.SemaphoreType.DMA((self.na, MAX_CHUNKS, n)),
                               pltpu.SemaphoreType.DMA((self.na, MAX_CHUNKS, n)),
                               pltpu.SemaphoreType.DMA((self.na, MAX_CHUNKS))]

    def _copies(self, x_refs, o_refs, send_sems, recv_sems, local_sems):
        me, _ = _group_peer(self.axes, 0)
        local, remote = [], []
        for a, (x, o) in enumerate(zip(x_refs, o_refs)):
            for c in range(self.pieces[a]):
                if self.mode == "rows":
                    r = x.shape[1]
                    b = slice(None) if self.pieces[a] == 1 else pl.ds(c, 1)
                    src = lambda k, x=x, b=b: x.at[b]
                    dst = o.at[b, pl.ds(me * r, r)]
                else:
                    lead = x.shape[1] if self.mode == "scatter" else x.shape[0]
                    rs = pl.ds(c * (lead // self.pieces[a]), lead // self.pieces[a])
                    if self.mode == "scatter":
                        src = lambda k, x=x, rs=rs: x.at[me ^ k, rs]
                    else:
                        src = lambda k, x=x, rs=rs: x.at[rs]
                    dst = o.at[me, rs]
                local.append(pltpu.make_async_copy(src(0), dst, local_sems.at[a, c]))
                for k in range(1, self.n):
                    remote.append(pltpu.make_async_remote_copy(
                        src_ref=src(k), dst_ref=dst, send_sem=send_sems.at[a, c, k], recv_sem=recv_sems.at[a, c, k],
                        device_id=_group_peer(self.axes, k)[1], device_id_type=MESH))
        return local, remote

    def start(self, x_refs, o_refs, sems):
        local, remote = self._copies(x_refs, o_refs, *sems)
        for cp in local + remote:
            cp.start()

    def wait(self, x_refs, o_refs, sems):
        local, remote = self._copies(x_refs, o_refs, *sems)
        for cp in remote:
            cp.wait_send()
        for cp in remote:
            cp.wait_recv()
        for cp in local:
            cp.wait()


def _exchange(xs, axes, mode, name):
    ex = _Exchange(xs, axes, mode)
    na = ex.na

    def body(*refs):
        ex.start(refs[:na], refs[na:2 * na], refs[2 * na:])
        ex.wait(refs[:na], refs[na:2 * na], refs[2 * na:])

    return pl.pallas_call(body, out_shape=ex.out_shape, in_specs=ex.in_specs, out_specs=ex.out_specs,
                          scratch_shapes=ex.scratch_shapes, name=name)(*xs)


SHARDED_SMALL = (("a_conv_w", 2), ("a_pw_w", 1), ("s5_glu_w", 1), ("c_conv_w", 2), ("d_conv_w", 2))
REPLICATED = ("norm_g", "a_conv_b", "a_ln_g", "a_ln_b", "a_pw_b", "s5_lambda_re", "s5_lambda_im", "s5_b_re",
              "s5_b_im", "s5_c_re", "s5_c_im", "s5_d", "s5_log_dt", "s5_glu_b", "d_a_log", "d_dt_bias",
              "d_norm_g", "final_g")
WEIGHTS = ("norm_g", "w_in", "a_conv_w", "a_conv_b", "a_ln_g", "a_ln_b", "a_pw_w", "a_pw_b", "s5_lambda_re",
           "s5_lambda_im", "s5_b_re", "s5_b_im", "s5_c_re", "s5_c_im", "s5_d", "s5_log_dt", "s5_glu_w",
           "s5_glu_b", "c_conv_w", "d_conv_w", "d_a_log", "d_dt_bias", "d_norm_g", "w_out", "final_g")
LANES = 1024


def _size(shape):
    size = 1
    for d in shape:
        size *= d
    return size


def _slab_rows(shape):
    return -(-_size(shape) // (8 * LANES)) * 8


def _pack(arrs, rows):
    parts = []
    for a in arrs:
        r = _slab_rows(a.shape)
        parts.append(jnp.pad(a.reshape(-1), (0, r * LANES - a.size)).reshape(r, LANES))
    used = sum(p.shape[0] for p in parts)
    if rows > used:
        parts.append(jnp.zeros((rows - used, LANES), parts[0].dtype))
    return jnp.concatenate(parts, axis=0)


def _unpack(slab, shapes):
    out, off = [], 0
    for s in shapes:
        r = _slab_rows(s)
        out.append(slab[off:off + r].reshape(-1)[:_size(s)].reshape(s))
        off += r
    return out


def _rows_for(shapes, mult):
    rows = sum(_slab_rows(s) for s in shapes)
    return -(-rows // mult) * mult


def _row(v, width=None):
    v = v.reshape(1, -1)
    return v if width is None else jnp.pad(v, ((0, 0), (0, width - v.shape[1])))


def _pad_rows(w, rows):
    return jnp.pad(w, ((0, rows - w.shape[0]), (0, 0)))


def _permute_in(w):
    return jnp.concatenate([w[:, :3072], w[:, 3080:N_IN], w[:, 3072:3080],
                            jnp.zeros((w.shape[0], N_INP - N_IN), w.dtype)], axis=1)


def _layer_fwd(x, p, exchange=None):
    proj, h = _proj_fwd(x, p["norm_g"], p["wp"])
    ya = _a_fwd(proj, p["a_cw"], p["a_cb"], p["a_lng"], p["a_lnb"], p["a_pw"], p["a_pwb"])
    ar, ai, bemb, cemb = _s5_prep(*p["s5"])
    yb, sin_all = _b_fwd(proj, ar, ai, bemb, cemb, p["s5_d"], p["glu_w"], p["glu_b"])
    yc = _c_fwd(proj, p["c_cw"])
    (yd, sall), got = _d_fwd(proj, p["d_cw"], p["d_p1"], p["d_p2"], p["d_ng"], exchange)
    xo = _out_fwd(x, ya, yb, yc, yd, p["wo"])
    return xo, dict(x=x, proj=proj, h=h, ys=(ya, yb, yc, yd), sin_all=sin_all, sall=sall,
                    s5=(ar, ai, bemb, cemb)), got


def _layer_bwd(dxo, p, r, exchange=None):
    proj = r["proj"]
    ar, ai, bemb, cemb = r["s5"]
    dmix = _out_bwd_x(dxo, p["wo"])
    dwo = _dwout(*r["ys"], dxo)
    dpa, dcw_a, dcb, dlng, dlnb, dpw, dpwb = _a_bwd(proj, dmix, p["a_cw"], p["a_cb"], p["a_lng"], p["a_lnb"],
                                                     p["a_pw"], p["a_pwb"])
    dpb, dar, dai, dbe, dce, ddsk, dgw, dgb = _b_bwd(proj, dmix, r["sin_all"], ar, ai, bemb, cemb, p["s5_d"],
                                                     p["glu_w"], p["glu_b"])
    dlre, dlim, dldt, dbre, dbim, dcre, dcim = _s5_prep_bwd(*p["s5"], dar, dai, dbe, dce)
    dpc, dcw_c = _c_bwd(proj, dmix, p["c_cw"])
    (dpd, dcw_d, dp1, dp2, dng), got = _d_bwd(proj, dmix, r["sall"], p["d_cw"], p["d_p1"], p["d_p2"], p["d_ng"],
                                              exchange)
    dx, dg = _proj_bwd_x(r["x"], p["norm_g"], dpa, dpb, dpc, dpd, p["wp"], dxo)
    dwd = _mm_tn_acc(r["h"], dpd, 384, "dwin_d")
    dwin = jnp.concatenate([_mm_tn_acc(r["h"], dpa, 256, "dwin_a"), _mm_tn_acc(r["h"], dpb, 256, "dwin_b"),
                            _mm_tn_acc(r["h"], dpc, 256, "dwin_c"), dwd[:, :3 * BR],
                            dwd[:, 4 * BR:4 * BR + 2 * DN_H], dwd[:, 3 * BR:4 * BR]], axis=1)

    def unrows(t, perm):
        return jnp.transpose(t.reshape(S5_H, S5_G, S5_P), perm)

    grads = dict(
        norm_g=dg.reshape(-1), w_in=dwin, a_conv_w=dcw_a[:K_A], a_conv_b=dcb.reshape(-1),
        a_ln_g=dlng.reshape(-1), a_ln_b=dlnb.reshape(-1), a_pw_w=dpw, a_pw_b=dpwb.reshape(-1),
        s5_lambda_re=dlre.reshape(S5_G, S5_P), s5_lambda_im=dlim.reshape(S5_G, S5_P),
        s5_b_re=unrows(dbre, (1, 2, 0)), s5_b_im=unrows(dbim, (1, 2, 0)),
        s5_c_re=unrows(dcre, (1, 0, 2)), s5_c_im=unrows(dcim, (1, 0, 2)),
        s5_d=ddsk.reshape(-1), s5_log_dt=dldt[0, :S5_G], s5_glu_w=dgw, s5_glu_b=dgb.reshape(-1),
        c_conv_w=dcw_c[:K_C], d_conv_w=dcw_d[:K_DN], d_a_log=dp1[0, :DN_H], d_dt_bias=dp2[0, :DN_H],
        d_norm_g=dng.reshape(-1), w_out=dwo)
    return dx, grads, got


def _layer_params(full, wp, wo, l):
    return dict(
        norm_g=_row(full["norm_g"][l]), wp=wp,
        a_cw=_pad_rows(full["a_conv_w"][l], HALO_A), a_cb=_row(full["a_conv_b"][l]),
        a_lng=_row(full["a_ln_g"][l]), a_lnb=_row(full["a_ln_b"][l]), a_pw=full["a_pw_w"][l],
        a_pwb=_row(full["a_pw_b"][l]),
        s5=(_row(full["s5_lambda_re"][l]), _row(full["s5_lambda_im"][l]), _row(full["s5_log_dt"][l], 128),
            jnp.transpose(full["s5_b_re"][l], (2, 0, 1)).reshape(S5_H, NS),
            jnp.transpose(full["s5_b_im"][l], (2, 0, 1)).reshape(S5_H, NS),
            jnp.transpose(full["s5_c_re"][l], (1, 0, 2)).reshape(S5_H, NS),
            jnp.transpose(full["s5_c_im"][l], (1, 0, 2)).reshape(S5_H, NS)),
        s5_d=_row(full["s5_d"][l]), glu_w=full["s5_glu_w"][l], glu_b=_row(full["s5_glu_b"][l]),
        c_cw=_pad_rows(full["c_conv_w"][l], HALO_S), d_cw=_pad_rows(full["d_conv_w"][l], HALO_S),
        d_p1=_row(full["d_a_log"][l], 128), d_p2=_row(full["d_dt_bias"][l], 128),
        d_ng=_row(full["d_norm_g"][l]), wo=wo)


def kernel(x, norm_g, w_in, a_conv_w, a_conv_b, a_ln_g, a_ln_b, a_pw_w, a_pw_b, s5_lambda_re, s5_lambda_im, s5_b_re, s5_b_im, s5_c_re, s5_c_im, s5_d, s5_log_dt, s5_glu_w, s5_glu_b, c_conv_w, d_conv_w, d_a_log, d_dt_bias, d_norm_g, w_out, final_g, loss_target, m_norm_g, m_w_in, m_a_conv_w, m_a_conv_b, m_a_ln_g, m_a_ln_b, m_a_pw_w, m_a_pw_b, m_s5_lambda_re, m_s5_lambda_im, m_s5_b_re, m_s5_b_im, m_s5_c_re, m_s5_c_im, m_s5_d, m_s5_log_dt, m_s5_glu_w, m_s5_glu_b, m_c_conv_w, m_d_conv_w, m_d_a_log, m_d_dt_bias, m_d_norm_g, m_w_out, m_final_g, v_norm_g, v_w_in, v_a_conv_w, v_a_conv_b, v_a_ln_g, v_a_ln_b, v_a_pw_w, v_a_pw_b, v_s5_lambda_re, v_s5_lambda_im, v_s5_b_re, v_s5_b_im, v_s5_c_re, v_s5_c_im, v_s5_d, v_s5_log_dt, v_s5_glu_w, v_s5_glu_b, v_c_conv_w, v_d_conv_w, v_d_a_log, v_d_dt_bias, v_d_norm_g, v_w_out, v_final_g):
    given = dict(locals())
    w = {n: given[n] for n in WEIGHTS}
    m = {n: given["m_" + n] for n in WEIGHTS}
    v = {n: given["v_" + n] for n in WEIGHTS}
    xs, tgt = x[0], loss_target[0]

    n_in, n_out = w["w_in"].shape[2], w["w_out"].shape[1]
    sm_names = [n for n, _ in SHARDED_SMALL]
    sm_shapes = [w[n].shape for n in sm_names]
    sm_rows = _rows_for(sm_shapes, 16)
    win_b, wout_b = w["w_in"].astype(BF), w["w_out"].astype(BF)
    g_in, g_out, g_sm = _exchange([win_b[0], wout_b[0], _pack([w[n] for n in sm_names], sm_rows)],
                                  ("x", "y"), "gather", "gather_first")
    full = dict(w)
    parts = [_unpack(g_sm[j], sm_shapes) for j in range(4)]
    for i, (n, ax) in enumerate(SHARDED_SMALL):
        full[n] = jnp.concatenate([parts[j][i] for j in range(4)], axis=ax)

    saved = []
    h = xs
    for l in range(DEPTH):
        p = _layer_params(full, _permute_in(jnp.concatenate([g_in[j] for j in range(4)], axis=1)),
                          jnp.concatenate([g_out[j] for j in range(4)], axis=0), l)
        nxt = ([win_b[l + 1], wout_b[l + 1]], ("x", "y"), "gather") if l + 1 < DEPTH else None
        h, r, got = _layer_fwd(h, p, nxt)
        saved.append((p, r))
        if got is not None:
            g_in, g_out = got
    loss_tile, dx, dfg = _loss_bwd(h, _row(full["final_g"]), tgt)

    def big_slots(g):
        s_in = jnp.stack([g["w_in"][:, j * n_in:(j + 1) * n_in].astype(BF) for j in range(4)])
        return [s_in.reshape(8, D_MODEL // 2, n_in), g["w_out"].astype(BF).reshape(8, n_out // 2, D_MODEL)]

    layer_grads, recv, pending = [None] * DEPTH, [None] * DEPTH, None
    for l in reversed(range(DEPTH)):
        p, r = saved[l]
        dx, layer_grads[l], got = _layer_bwd(dx, p, r, None if pending is None else (pending, AXES, "scatter"))
        if got is not None:
            recv[l + 1] = got
        pending = big_slots(layer_grads[l])
    grads = {n: jnp.stack([layer_grads[l][n] for l in range(DEPTH)]) for n in WEIGHTS
             if n not in ("final_g", "w_in", "w_out")}
    grads["final_g"] = dfg.reshape(-1)
    slots = []
    for j in range(4):
        sl = [lax.slice_in_dim(grads[n], j * w[n].shape[ax], (j + 1) * w[n].shape[ax], axis=ax)
              for n, ax in SHARDED_SMALL]
        slots.append(_pack(sl, sm_rows))
    rp_shapes = [w[n].shape for n in REPLICATED] + [(1,)]
    rp_rows = _rows_for(rp_shapes, 64)
    r_in0, r_out0, r_sm, r_rp = _exchange(
        pending + [jnp.stack(slots).reshape(8, sm_rows // 2, LANES),
                   _pack([grads[n] for n in REPLICATED] + [loss_tile[0, 0:1]], rp_rows).reshape(8, rp_rows // 8, LANES)],
        AXES, "scatter", "scatter_last")
    recv[0] = (r_in0, r_out0)
    h_in, h_out, h_sm = _exchange(
        [jnp.stack([_sum_slots(recv[l][0], "sum_w_in") for l in range(DEPTH)]),
         jnp.stack([_sum_slots(recv[l][1], "sum_w_out") for l in range(DEPTH)]),
         _sum_slots(r_sm, "sum_small")[None]], ("c",), "rows", "gather_halves")
    (g_rp,) = _exchange([_sum_slots(r_rp, "sum_replicated")], AXES, "gather", "gather_replicated")
    g_rp = g_rp.reshape(rp_rows, LANES)
    g_sm = h_sm.reshape(sm_rows, LANES)

    out = {}

    def put(name, shape, res):
        for key, t in zip(("delta", "new_m", "new_v"), res):
            out[key + "_" + name] = t.reshape(shape)

    for name, g2 in (("w_in", h_in.reshape(DEPTH * D_MODEL, n_in)), ("w_out", h_out.reshape(DEPTH * n_out, D_MODEL))):
        shape = w[name].shape
        out["grad_" + name] = g2.reshape(shape)
        put(name, shape, _adamw(w[name].reshape(g2.shape), g2, m[name].reshape(g2.shape), v[name].reshape(g2.shape),
                                "adamw_" + name))
    zero = jnp.zeros((1,), F32)
    res_sm = _adamw(_pack([w[n] for n in sm_names], sm_rows), g_sm, _pack([m[n] for n in sm_names], sm_rows),
                    _pack([v[n] for n in sm_names], sm_rows), "adamw_small")
    res_rp = _adamw(_pack([w[n] for n in REPLICATED] + [zero], rp_rows), g_rp,
                    _pack([m[n] for n in REPLICATED] + [zero], rp_rows),
                    _pack([v[n] for n in REPLICATED] + [zero], rp_rows), "adamw_replicated")
    for key, sm, rp in (("grad", g_sm, g_rp), ("delta", res_sm[0], res_rp[0]), ("new_m", res_sm[1], res_rp[1]),
                        ("new_v", res_sm[2], res_rp[2])):
        for n, t in zip(sm_names, _unpack(sm, sm_shapes)):
            out[key + "_" + n] = t
        for n, t in zip(REPLICATED, _unpack(rp, rp_shapes[:-1])):
            out[key + "_" + n] = t
    loss = _unpack(g_rp, rp_shapes)[-1].reshape(())
    return (loss, dx[None], *[out["grad_" + n] for n in WEIGHTS], *[out["delta_" + n] for n in WEIGHTS],
            *[out["new_m_" + n] for n in WEIGHTS], *[out["new_v_" + n] for n in WEIGHTS])
```

```python
import functools

import jax
import jax.numpy as jnp
from jax import lax
from jax.experimental import pallas as pl
from jax.experimental.pallas import tpu as pltpu

F32, BF = jnp.float32, jnp.bfloat16
HI = lax.Precision.HIGHEST
MESH = pl.DeviceIdType.MESH

D_MODEL = 1024
BR = 256
DEPTH = 4
N_IN = 3336
N_INP = 3456
COL_A, COL_B, COL_C, COL_D = 0, 768, 1280, 2304
W_A, W_B, W_C, W_D = 768, 512, 1024, 1152
S5_G, S5_H, S5_P = 16, 16, 64
NS = S5_G * S5_P
DN_H, DN_D, DN_C = 4, 64, 64
K_A, K_C, K_DN = 31, 3, 4
HALO_A, HALO_S = 32, 8
EPS = 1e-6
TL = 256
VMEM_LIMIT = 56 * 1024 * 1024

ADAM_LR, ADAM_B1, ADAM_B2, ADAM_EPS, ADAM_WD, ADAM_STEP = 0.001, 0.9, 0.999, 1e-08, 0.01, 10


def _cp(*sem):
    return pltpu.CompilerParams(dimension_semantics=sem, vmem_limit_bytes=VMEM_LIMIT)


def _sigmoid(x):
    return jax.nn.sigmoid(x)


def _silu(x):
    return x * jax.nn.sigmoid(x)


def _rmsnorm(x, g):
    return x * lax.rsqrt(jnp.mean(x * x, axis=-1, keepdims=True) + EPS) * g


@jax.custom_vjp
def _mm(a, w):
    return jnp.dot(a.astype(BF), w.astype(BF), preferred_element_type=F32)


def _mm_f(a, w):
    return _mm(a, w), (a, w)


def _mm_b(res, g):
    a, w = res
    gb = g.astype(BF)
    da = lax.dot_general(gb, w.astype(BF), (((1,), (1,)), ((), ())), preferred_element_type=F32)
    dw = lax.dot_general(a.astype(BF), gb, (((0,), (0,)), ((), ())), preferred_element_type=F32)
    return da, dw


_mm.defvjp(_mm_f, _mm_b)


@jax.custom_vjp
def _mm_nt(a, b):
    return lax.dot_general(a.astype(BF), b.astype(BF), (((1,), (1,)), ((), ())), preferred_element_type=F32)


def _mm_nt_f(a, b):
    return _mm_nt(a, b), (a, b)


def _mm_nt_b(res, g):
    a, b = res
    gb = g.astype(BF)
    da = jnp.dot(gb, b.astype(BF), preferred_element_type=F32)
    db = lax.dot_general(gb, a.astype(BF), (((0,), (0,)), ((), ())), preferred_element_type=F32)
    return da, db


_mm_nt.defvjp(_mm_nt_f, _mm_nt_b)


@jax.custom_vjp
def _mm_tn(a, b):
    return lax.dot_general(a.astype(BF), b.astype(BF), (((0,), (0,)), ((), ())), preferred_element_type=F32)


def _mm_tn_f(a, b):
    return _mm_tn(a, b), (a, b)


def _mm_tn_b(res, g):
    a, b = res
    gb = g.astype(BF)
    da = lax.dot_general(b.astype(BF), gb, (((1,), (1,)), ((), ())), preferred_element_type=F32)
    db = jnp.dot(a.astype(BF), gb, preferred_element_type=F32)
    return da, db


_mm_tn.defvjp(_mm_tn_f, _mm_tn_b)


def _dot_hi(a, b):
    return jnp.dot(a, b, precision=HI, preferred_element_type=F32)


def _split(a):
    hi = a.astype(BF)
    return hi, (a - hi.astype(F32)).astype(BF)


def _dot3(a, b, dims=(((1,), (0,)), ((), ()))):
    ah, al = _split(a)
    bh, bl = _split(b)
    d = functools.partial(lax.dot_general, dimension_numbers=dims, preferred_element_type=F32)
    return d(ah, bh) + d(ah, bl) + d(al, bh)


@jax.custom_vjp
def _mm3(a, b):
    return _dot3(a, b)


def _mm3_f(a, b):
    return _dot3(a, b), (a, b)


def _mm3_b(res, g):
    a, b = res
    return _dot3(g, b, (((1,), (1,)), ((), ()))), _dot3(a, g, (((0,), (0,)), ((), ())))


_mm3.defvjp(_mm3_f, _mm3_b)


@jax.custom_vjp
def _unit_lower_inv(lms):
    n = lms[0].shape[0]
    row = lax.broadcasted_iota(jnp.int32, lms[0].shape, 0)
    col = lax.broadcasted_iota(jnp.int32, lms[0].shape, 1)
    eye = (row == col).astype(F32)
    accs = [eye - lm for lm in lms]
    pws = list(lms)
    k = 2
    while k < n:
        pws = [_dot3(p, p) for p in pws]
        accs = [a + _dot3(a, p) for a, p in zip(accs, pws)]
        k *= 2
    return tuple(accs)


def _uli_f(lms):
    a = _unit_lower_inv(lms)
    return a, a


def _uli_b(a, g):
    ats = [x.T for x in a]
    tmp = [_dot3(at, gi) for at, gi in zip(ats, g)]
    return (tuple(-_dot3(t, at) for t, at in zip(tmp, ats)),)


_unit_lower_inv.defvjp(_uli_f, _uli_b)


def _roll(x, s):
    n = x.shape[0]
    s = s % n
    return x if s == 0 else pltpu.roll(x, s, 0)


def _conv_taps(ext, w_ref, halo, k_taps, tl):
    acc = None
    for k in range(k_taps):
        term = _roll(ext, (k_taps - 1) - k)[halo:halo + tl] * w_ref[k:k + 1, :]
        acc = term if acc is None else acc + term
    return acc


def _conv_taps_bwd(ext, w_ref, dw_ref, dacc, halo, k_taps, tl):
    dpad = jnp.concatenate([dacc, jnp.zeros((halo, dacc.shape[1]), F32)], axis=0)
    dext = None
    for k in range(k_taps):
        r = _roll(ext, (k_taps - 1) - k)[halo:halo + tl]
        dw_ref[k:k + 1, :] += jnp.sum(r * dacc, axis=0, keepdims=True)
        term = _roll(dpad, halo - (k_taps - 1) + k) * w_ref[k:k + 1, :]
        dext = term if dext is None else dext + term
    return dext


def _add_tail(x, tail):
    tl, h = x.shape[0], tail.shape[0]
    return x + jnp.concatenate([jnp.zeros((tl - h, x.shape[1]), F32), tail], axis=0)


def _proj_fwd(x, g, wp):
    L = x.shape[0]

    def body(x_ref, g_ref, w_ref, p_ref, h_ref):
        hb = _rmsnorm(x_ref[...], g_ref[...]).astype(BF)
        h_ref[...] = hb
        p_ref[...] = jnp.dot(hb, w_ref[...], preferred_element_type=F32)

    return pl.pallas_call(
        body, grid=(L // TL,),
        in_specs=[pl.BlockSpec((TL, D_MODEL), lambda i: (i, 0)),
                  pl.BlockSpec((1, D_MODEL), lambda i: (0, 0)),
                  pl.BlockSpec((D_MODEL, N_INP), lambda i: (0, 0))],
        out_specs=[pl.BlockSpec((TL, N_INP), lambda i: (i, 0)),
                   pl.BlockSpec((TL, D_MODEL), lambda i: (i, 0))],
        out_shape=[jax.ShapeDtypeStruct((L, N_INP), F32), jax.ShapeDtypeStruct((L, D_MODEL), BF)],
        name="proj_fwd", compiler_params=_cp("parallel"))(x, g, wp)


def _proj_bwd_x(x, g, dpa, dpb, dpc, dpd, wp, dxo, exchange=None):
    L = x.shape[0]

    def body(x_ref, g_ref, a_ref, b_ref, c_ref, d_ref, w_ref, dxo_ref, dx_ref, dg_ref):
        dh = None
        for ref, c0, wd in ((a_ref, COL_A, W_A), (b_ref, COL_B, W_B), (c_ref, COL_C, W_C), (d_ref, COL_D, W_D)):
            t = lax.dot_general(ref[...].astype(BF), w_ref[:, c0:c0 + wd], (((1,), (1,)), ((), ())),
                                preferred_element_type=F32)
            dh = t if dh is None else dh + t
        _, vj = jax.vjp(_rmsnorm, x_ref[...], g_ref[...])
        dx, dg = vj(dh)
        dx_ref[...] = dxo_ref[...] + dx

        @pl.when(pl.program_id(0) == 0)
        def _():
            dg_ref[...] = jnp.zeros_like(dg_ref)

        dg_ref[...] += dg

    def rows(wd):
        return pl.BlockSpec((TL, wd), lambda i: (i, 0))

    return _sweep_with_exchange(
        body, L // TL,
        in_specs=[rows(D_MODEL), pl.BlockSpec((1, D_MODEL), lambda i: (0, 0)),
                  rows(W_A), rows(W_B), rows(W_C), rows(W_D),
                  pl.BlockSpec((D_MODEL, N_INP), lambda i: (0, 0)), rows(D_MODEL)],
        out_specs=[rows(D_MODEL), pl.BlockSpec((1, D_MODEL), lambda i: (0, 0))],
        out_shape=[jax.ShapeDtypeStruct((L, D_MODEL), F32), jax.ShapeDtypeStruct((1, D_MODEL), F32)],
        scratch_shapes=[], args=(x, g, dpa, dpb, dpc, dpd, wp, dxo), exchange=exchange, name="proj_bwd_x")


def _dwin(h, dpa, dpb, dpc, dpd):
    L = h.shape[0]

    def body(h_ref, a_ref, b_ref, c_ref, d_ref, oa_ref, ob_ref, oc_ref, od_ref):
        outs = (oa_ref, ob_ref, oc_ref, od_ref)

        @pl.when(pl.program_id(0) == 0)
        def _():
            for o in outs:
                o[...] = jnp.zeros_like(o)

        ht = h_ref[...].T
        for ref, o in zip((a_ref, b_ref, c_ref, d_ref), outs):
            o[...] += jnp.dot(ht, ref[...].astype(BF), preferred_element_type=F32)

    def rows(wd):
        return pl.BlockSpec((TL, wd), lambda i: (i, 0))

    def whole(wd):
        return pl.BlockSpec((D_MODEL, wd), lambda i: (0, 0))

    widths = (W_A, W_B, W_C, W_D)
    return pl.pallas_call(
        body, grid=(L // TL,),
        in_specs=[rows(D_MODEL)] + [rows(wd) for wd in widths],
        out_specs=[whole(wd) for wd in widths],
        out_shape=[jax.ShapeDtypeStruct((D_MODEL, wd), F32) for wd in widths],
        name="dwin", compiler_params=_cp("arbitrary"))(h, dpa, dpb, dpc, dpd)


def _dwout(ya, yb, yc, yd, dxo):
    L = dxo.shape[0]
    tk, tn = min(512, L), 512

    def body(a_ref, b_ref, c_ref, d_ref, g_ref, o_ref):
        @pl.when(pl.program_id(1) == 0)
        def _():
            o_ref[...] = jnp.zeros_like(o_ref)

        gb = g_ref[...].astype(BF)
        for j, ref in enumerate((a_ref, b_ref, c_ref, d_ref)):
            o_ref[j * BR:(j + 1) * BR, :] += lax.dot_general(ref[...].astype(BF), gb, (((0,), (0,)), ((), ())),
                                                             preferred_element_type=F32)

    ys = pl.BlockSpec((tk, BR), lambda j, t: (t, 0))
    return pl.pallas_call(
        body, grid=(D_MODEL // tn, L // tk),
        in_specs=[ys, ys, ys, ys, pl.BlockSpec((tk, tn), lambda j, t: (t, j))],
        out_specs=pl.BlockSpec((D_MODEL, tn), lambda j, t: (0, j)),
        out_shape=jax.ShapeDtypeStruct((D_MODEL, D_MODEL), F32),
        name="dwout", compiler_params=_cp("parallel", "arbitrary"))(ya, yb, yc, yd, dxo)


def _out_fwd(x, ya, yb, yc, yd, wo):
    L = x.shape[0]

    def body(x_ref, a_ref, b_ref, c_ref, d_ref, w_ref, o_ref):
        acc = x_ref[...]
        for j, ref in enumerate((a_ref, b_ref, c_ref, d_ref)):
            acc = acc + jnp.dot(ref[...].astype(BF), w_ref[j * BR:(j + 1) * BR, :], preferred_element_type=F32)
        o_ref[...] = acc

    def rows(wd):
        return pl.BlockSpec((TL, wd), lambda i: (i, 0))

    return pl.pallas_call(
        body, grid=(L // TL,),
        in_specs=[rows(D_MODEL), rows(BR), rows(BR), rows(BR), rows(BR),
                  pl.BlockSpec((D_MODEL, D_MODEL), lambda i: (0, 0))],
        out_specs=rows(D_MODEL), out_shape=jax.ShapeDtypeStruct((L, D_MODEL), F32),
        name="out_fwd", compiler_params=_cp("parallel"))(x, ya, yb, yc, yd, wo)


def _out_bwd_x(dxo, wo):
    L = dxo.shape[0]

    def body(d_ref, w_ref, o_ref):
        o_ref[...] = lax.dot_general(d_ref[...].astype(BF), w_ref[...], (((1,), (1,)), ((), ())),
                                     preferred_element_type=F32)

    return pl.pallas_call(
        body, grid=(L // TL,),
        in_specs=[pl.BlockSpec((TL, D_MODEL), lambda i: (i, 0)), pl.BlockSpec((D_MODEL, D_MODEL), lambda i: (0, 0))],
        out_specs=pl.BlockSpec((TL, D_MODEL), lambda i: (i, 0)),
        out_shape=jax.ShapeDtypeStruct((L, D_MODEL), F32),
        name="out_bwd_x", compiler_params=_cp("parallel"))(dxo, wo)


def _loss_bwd(x, g, tgt):
    L = x.shape[0]

    def f(xv, gv, tv):
        err = _rmsnorm(xv, gv) - tv
        return 0.5 * jnp.sum(jnp.mean(err * err, axis=-1, keepdims=True), axis=0, keepdims=True)

    def body(x_ref, g_ref, t_ref, loss_ref, dx_ref, dg_ref):
        tv = t_ref[...]
        loss, vj = jax.vjp(lambda a, b: f(a, b, tv), x_ref[...], g_ref[...])
        dx, dg = vj(jnp.ones((1, 1), F32))
        dx_ref[...] = dx

        @pl.when(pl.program_id(0) == 0)
        def _():
            dg_ref[...] = jnp.zeros_like(dg_ref)
            loss_ref[...] = jnp.zeros_like(loss_ref)

        dg_ref[...] += dg
        loss_ref[...] += jnp.broadcast_to(loss, loss_ref.shape)

    return pl.pallas_call(
        body, grid=(L // TL,),
        in_specs=[pl.BlockSpec((TL, D_MODEL), lambda i: (i, 0)), pl.BlockSpec((1, D_MODEL), lambda i: (0, 0)),
                  pl.BlockSpec((TL, D_MODEL), lambda i: (i, 0))],
        out_specs=[pl.BlockSpec((8, 128), lambda i: (0, 0)), pl.BlockSpec((TL, D_MODEL), lambda i: (i, 0)),
                   pl.BlockSpec((1, D_MODEL), lambda i: (0, 0))],
        out_shape=[jax.ShapeDtypeStruct((8, 128), F32), jax.ShapeDtypeStruct((L, D_MODEL), F32),
                   jax.ShapeDtypeStruct((1, D_MODEL), F32)],
        name="loss_bwd", compiler_params=_cp("arbitrary"))(x, g, tgt)


def _a_pre(val, gate):
    return val * _sigmoid(gate)


def _a_post(acc, az, cb, lng, lnb, pw, pwb):
    t = acc + cb
    mu = jnp.mean(t, axis=-1, keepdims=True)
    xc = t - mu
    ln = xc * lax.rsqrt(jnp.mean(xc * xc, axis=-1, keepdims=True) + EPS) * lng + lnb
    return (_mm(_silu(ln), pw) + pwb) * _silu(az)


def _halo_map(tl, halo, col):
    r = tl // halo
    return lambda i: (jnp.maximum(i * r - 1, 0), col)


def _a_fwd(proj, cw, cb, lng, lnb, pw, pwb):
    L = proj.shape[0]

    def body(vg_ref, az_ref, hvg_ref, cw_ref, cb_ref, lng_ref, lnb_ref, pw_ref, pwb_ref, o_ref):
        keep = (pl.program_id(0) > 0).astype(F32)
        a_h = _a_pre(hvg_ref[:, 0:BR], hvg_ref[:, BR:2 * BR]) * keep
        a_t = _a_pre(vg_ref[:, 0:BR], vg_ref[:, BR:2 * BR])
        ext = jnp.concatenate([a_h, a_t], axis=0)
        acc = _conv_taps(ext, cw_ref, HALO_A, K_A, TL)
        o_ref[...] = _a_post(acc, az_ref[...], cb_ref[...], lng_ref[...], lnb_ref[...], pw_ref[...], pwb_ref[...])

    vec = pl.BlockSpec((1, BR), lambda i: (0, 0))
    return pl.pallas_call(
        body, grid=(L // TL,),
        in_specs=[pl.BlockSpec((TL, 2 * BR), lambda i: (i, 0)), pl.BlockSpec((TL, BR), lambda i: (i, 2)),
                  pl.BlockSpec((HALO_A, 2 * BR), _halo_map(TL, HALO_A, 0)),
                  pl.BlockSpec((HALO_A, BR), lambda i: (0, 0)), vec, vec, vec,
                  pl.BlockSpec((BR, BR), lambda i: (0, 0)), vec],
        out_specs=pl.BlockSpec((TL, BR), lambda i: (i, 0)),
        out_shape=jax.ShapeDtypeStruct((L, BR), F32),
        name="a_fwd", compiler_params=_cp("parallel"))(proj, proj, proj, cw, cb, lng, lnb, pw, pwb)


def _a_bwd(proj, dmix, cw, cb, lng, lnb, pw, pwb):
    L = proj.shape[0]
    n = L // TL

    def body(vg_ref, az_ref, hvg_ref, dy_ref, cw_ref, cb_ref, lng_ref, lnb_ref, pw_ref, pwb_ref,
             dp_ref, dcw_ref, dcb_ref, dlng_ref, dlnb_ref, dpw_ref, dpwb_ref, carry_ref):
        i = pl.program_id(0)

        @pl.when(i == 0)
        def _():
            carry_ref[...] = jnp.zeros_like(carry_ref)
            for r in (dcw_ref, dcb_ref, dlng_ref, dlnb_ref, dpw_ref, dpwb_ref):
                r[...] = jnp.zeros_like(r)

        keep = (i < n - 1).astype(F32)
        val, gate = vg_ref[:, 0:BR], vg_ref[:, BR:2 * BR]
        a_h = _a_pre(hvg_ref[:, 0:BR], hvg_ref[:, BR:2 * BR]) * keep
        a_t, vj_pre = jax.vjp(_a_pre, val, gate)
        ext = jnp.concatenate([a_h, a_t], axis=0)
        acc = _conv_taps(ext, cw_ref, HALO_A, K_A, TL)
        _, vj_post = jax.vjp(_a_post, acc, az_ref[...], cb_ref[...], lng_ref[...], lnb_ref[...], pw_ref[...],
                             pwb_ref[...])
        dacc, daz, dcb, dlng, dlnb, dpw, dpwb = vj_post(dy_ref[...])
        dext = _conv_taps_bwd(ext, cw_ref, dcw_ref, dacc, HALO_A, K_A, TL)
        da = _add_tail(dext[HALO_A:], carry_ref[...])
        carry_ref[...] = dext[:HALO_A]
        dval, dgate = vj_pre(da)
        dp_ref[:, 0:BR] = dval
        dp_ref[:, BR:2 * BR] = dgate
        dp_ref[:, 2 * BR:3 * BR] = daz
        dcb_ref[...] += dcb
        dlng_ref[...] += dlng
        dlnb_ref[...] += dlnb
        dpw_ref[...] += dpw
        dpwb_ref[...] += dpwb

    rev = lambda i: n - 1 - i
    vec = pl.BlockSpec((1, BR), lambda i: (0, 0))
    hmap = _halo_map(TL, HALO_A, 0)
    return pl.pallas_call(
        body, grid=(n,),
        in_specs=[pl.BlockSpec((TL, 2 * BR), lambda i: (rev(i), 0)), pl.BlockSpec((TL, BR), lambda i: (rev(i), 2)),
                  pl.BlockSpec((HALO_A, 2 * BR), lambda i: hmap(rev(i))),
                  pl.BlockSpec((TL, BR), lambda i: (rev(i), 0)),
                  pl.BlockSpec((HALO_A, BR), lambda i: (0, 0)), vec, vec, vec,
                  pl.BlockSpec((BR, BR), lambda i: (0, 0)), vec],
        out_specs=[pl.BlockSpec((TL, W_A), lambda i: (rev(i), 0)),
                   pl.BlockSpec((HALO_A, BR), lambda i: (0, 0)), vec, vec, vec,
                   pl.BlockSpec((BR, BR), lambda i: (0, 0)), vec],
        out_shape=[jax.ShapeDtypeStruct((L, W_A), F32), jax.ShapeDtypeStruct((HALO_A, BR), F32)]
        + [jax.ShapeDtypeStruct((1, BR), F32)] * 3
        + [jax.ShapeDtypeStruct((BR, BR), F32), jax.ShapeDtypeStruct((1, BR), F32)],
        scratch_shapes=[pltpu.VMEM((HALO_A, BR), F32)],
        name="a_bwd", compiler_params=_cp("arbitrary"))(proj, proj, proj, dmix, cw, cb, lng, lnb, pw, pwb)


def _c_pre(cg, xc):
    return cg * xc


def _c_post(acc, bg, cz):
    return bg * acc * _silu(cz)


def _c_fwd(proj, cw):
    L = proj.shape[0]

    def body(bg_ref, cx_ref, cz_ref, hcx_ref, cw_ref, o_ref):
        keep = (pl.program_id(0) > 0).astype(F32)
        p_h = _c_pre(hcx_ref[:, 0:BR], hcx_ref[:, BR:2 * BR]) * keep
        p_t = _c_pre(cx_ref[:, 0:BR], cx_ref[:, BR:2 * BR])
        ext = jnp.concatenate([p_h, p_t], axis=0)
        acc = _conv_taps(ext, cw_ref, HALO_S, K_C, TL)
        o_ref[...] = _c_post(acc, bg_ref[...], cz_ref[...])

    return pl.pallas_call(
        body, grid=(L // TL,),
        in_specs=[pl.BlockSpec((TL, BR), lambda i: (i, 5)), pl.BlockSpec((TL, 2 * BR), lambda i: (i, 3)),
                  pl.BlockSpec((TL, BR), lambda i: (i, 8)),
                  pl.BlockSpec((HALO_S, 2 * BR), _halo_map(TL, HALO_S, 3)),
                  pl.BlockSpec((HALO_S, BR), lambda i: (0, 0))],
        out_specs=pl.BlockSpec((TL, BR), lambda i: (i, 0)),
        out_shape=jax.ShapeDtypeStruct((L, BR), F32),
        name="c_fwd", compiler_params=_cp("parallel"))(proj, proj, proj, proj, cw)


def _c_bwd(proj, dmix, cw):
    L = proj.shape[0]
    n = L // TL

    def body(bg_ref, cx_ref, cz_ref, hcx_ref, dy_ref, cw_ref, dp_ref, dcw_ref, carry_ref):
        i = pl.program_id(0)

        @pl.when(i == 0)
        def _():
            carry_ref[...] = jnp.zeros_like(carry_ref)
            dcw_ref[...] = jnp.zeros_like(dcw_ref)

        keep = (i < n - 1).astype(F32)
        p_h = _c_pre(hcx_ref[:, 0:BR], hcx_ref[:, BR:2 * BR]) * keep
        p_t, vj_pre = jax.vjp(_c_pre, cx_ref[:, 0:BR], cx_ref[:, BR:2 * BR])
        ext = jnp.concatenate([p_h, p_t], axis=0)
        acc = _conv_taps(ext, cw_ref, HALO_S, K_C, TL)
        _, vj_post = jax.vjp(_c_post, acc, bg_ref[...], cz_ref[...])
        dacc, dbg, dcz = vj_post(dy_ref[...])
        dext = _conv_taps_bwd(ext, cw_ref, dcw_ref, dacc, HALO_S, K_C, TL)
        dp = _add_tail(dext[HALO_S:], carry_ref[...])
        carry_ref[...] = dext[:HALO_S]
        dcg, dxc = vj_pre(dp)
        dp_ref[:, 0:BR] = dbg
        dp_ref[:, BR:2 * BR] = dcg
        dp_ref[:, 2 * BR:3 * BR] = dxc
        dp_ref[:, 3 * BR:4 * BR] = dcz

    rev = lambda i: n - 1 - i
    hmap = _halo_map(TL, HALO_S, 3)
    return pl.pallas_call(
        body, grid=(n,),
        in_specs=[pl.BlockSpec((TL, BR), lambda i: (rev(i), 5)), pl.BlockSpec((TL, 2 * BR), lambda i: (rev(i), 3)),
                  pl.BlockSpec((TL, BR), lambda i: (rev(i), 8)),
                  pl.BlockSpec((HALO_S, 2 * BR), lambda i: hmap(rev(i))),
                  pl.BlockSpec((TL, BR), lambda i: (rev(i), 2)),
                  pl.BlockSpec((HALO_S, BR), lambda i: (0, 0))],
        out_specs=[pl.BlockSpec((TL, W_C), lambda i: (rev(i), 0)), pl.BlockSpec((HALO_S, BR), lambda i: (0, 0))],
        out_shape=[jax.ShapeDtypeStruct((L, W_C), F32), jax.ShapeDtypeStruct((HALO_S, BR), F32)],
        scratch_shapes=[pltpu.VMEM((HALO_S, BR), F32)],
        name="c_bwd", compiler_params=_cp("arbitrary"))(proj, proj, proj, proj, dmix, cw)


def _s5_prep_fn(lre, lim, ldt, bre, bim, cre, cim):
    grp = lax.broadcasted_iota(jnp.int32, (128, NS), 0)
    lane = lax.broadcasted_iota(jnp.int32, (128, NS), 1)
    expand = (grp == lane // S5_P).astype(F32)
    dt = jnp.exp(_dot_hi(jnp.broadcast_to(ldt, (8, 128)), expand)[0:1])
    lr = jnp.minimum(lre, -1e-4)
    mag = jnp.exp(lr * dt)
    ar = mag * jnp.cos(lim * dt)
    ai = mag * jnp.sin(lim * dt)
    den = lr * lr + lim * lim
    fr = ((ar - 1.0) * lr + ai * lim) / den
    fi = (ai * lr - (ar - 1.0) * lim) / den
    bbr = fr * bre - fi * bim
    bbi = fr * bim + fi * bre
    row = lax.broadcasted_iota(jnp.int32, (BR, NS), 0)
    col = lax.broadcasted_iota(jnp.int32, (BR, NS), 1)
    blk = (row // S5_H == col // S5_P).astype(F32)

    def embed(t):
        return jnp.concatenate([t] * S5_G, axis=0) * blk

    bemb = jnp.concatenate([embed(bbr), embed(bbi)], axis=1)
    cemb = jnp.concatenate([embed(cre), embed(-cim)], axis=1)
    return ar, ai, bemb, cemb


def _s5_prep(lre, lim, ldt, bre, bim, cre, cim):
    def body(*refs):
        outs = _s5_prep_fn(*[r[...] for r in refs[:7]])
        for r, o in zip(refs[7:], outs):
            r[...] = o

    return pl.pallas_call(
        body,
        out_shape=[jax.ShapeDtypeStruct((1, NS), F32)] * 2 + [jax.ShapeDtypeStruct((BR, 2 * NS), F32)] * 2,
        name="s5_prep", compiler_params=pltpu.CompilerParams(vmem_limit_bytes=VMEM_LIMIT),
    )(lre, lim, ldt, bre, bim, cre, cim)


def _s5_prep_bwd(lre, lim, ldt, bre, bim, cre, cim, dar, dai, dbemb, dcemb):
    def body(*refs):
        _, vj = jax.vjp(_s5_prep_fn, *[r[...] for r in refs[:7]])
        grads = vj(tuple(r[...] for r in refs[7:11]))
        for r, o in zip(refs[11:], grads):
            r[...] = o

    return pl.pallas_call(
        body,
        out_shape=[jax.ShapeDtypeStruct((1, NS), F32)] * 2 + [jax.ShapeDtypeStruct((1, 128), F32)]
        + [jax.ShapeDtypeStruct((S5_H, NS), F32)] * 4,
        name="s5_prep_bwd", compiler_params=pltpu.CompilerParams(vmem_limit_bytes=VMEM_LIMIT),
    )(lre, lim, ldt, bre, bim, cre, cim, dar, dai, dbemb, dcemb)


def _s5_scan(xr, xi, ar, ai, reverse):
    n = xr.shape[0]
    row = lax.broadcasted_iota(jnp.int32, (n, 1), 0)
    pr, pi = ar, ai
    d = 1
    while d < n:
        if d % 8:
            if reverse:
                m = row < n - d
                sr = jnp.where(m, _roll(xr, n - d), 0.0)
                si = jnp.where(m, _roll(xi, n - d), 0.0)
            else:
                m = row >= d
                sr = jnp.where(m, _roll(xr, d), 0.0)
                si = jnp.where(m, _roll(xi, d), 0.0)
            xr, xi = xr + pr * sr - pi * si, xi + pr * si + pi * sr
        elif reverse:
            sr, si = xr[d:], xi[d:]
            xr, xi = (jnp.concatenate([xr[:n - d] + pr * sr - pi * si, xr[n - d:]], axis=0),
                      jnp.concatenate([xi[:n - d] + pr * si + pi * sr, xi[n - d:]], axis=0))
        else:
            sr, si = xr[:n - d], xi[:n - d]
            xr, xi = (jnp.concatenate([xr[:d], xr[d:] + pr * sr - pi * si], axis=0),
                      jnp.concatenate([xi[:d], xi[d:] + pr * si + pi * sr], axis=0))
        pr, pi = pr * pr - pi * pi, 2.0 * pr * pi
        d *= 2
    return xr, xi


def _s5_states(u, bemb_b, ar, ai, sin_r, sin_i):
    bu = jnp.dot(u.astype(BF), bemb_b, preferred_element_type=F32)
    first = lax.broadcasted_iota(jnp.int32, (u.shape[0], 1), 0) == 0
    xr = bu[:, :NS] + jnp.where(first, ar * sin_r - ai * sin_i, 0.0)
    xi = bu[:, NS:] + jnp.where(first, ar * sin_i + ai * sin_r, 0.0)
    return _s5_scan(xr, xi, ar, ai, False)


def _b_post(yssm, u, bz, dsk, gw, gb):
    z = jax.nn.gelu(yssm + dsk * u)
    return z * _sigmoid(_mm(z, gw) + gb) * _silu(bz)


def _b_fwd(proj, ar, ai, bemb, cemb, dsk, gw, gb):
    L = proj.shape[0]
    n = L // TL

    def body(u_ref, bz_ref, ar_ref, ai_ref, be_ref, ce_ref, dsk_ref, gw_ref, gb_ref, o_ref, sin_ref, carry_ref):
        @pl.when(pl.program_id(0) == 0)
        def _():
            carry_ref[...] = jnp.zeros_like(carry_ref)

        sin = carry_ref[...]
        sin_ref[0] = sin
        u = u_ref[...]
        sr, si = _s5_states(u, be_ref[...].astype(BF), ar_ref[...], ai_ref[...], sin[:, :NS], sin[:, NS:])
        carry_ref[:, :NS] = sr[TL - 1:TL]
        carry_ref[:, NS:] = si[TL - 1:TL]
        s = jnp.concatenate([sr, si], axis=1).astype(BF)
        yssm = lax.dot_general(s, ce_ref[...].astype(BF), (((1,), (1,)), ((), ())), preferred_element_type=F32)
        o_ref[...] = _b_post(yssm, u, bz_ref[...], dsk_ref[...], gw_ref[...], gb_ref[...])

    vec = pl.BlockSpec((1, BR), lambda i: (0, 0))
    svec = pl.BlockSpec((1, NS), lambda i: (0, 0))
    emb = pl.BlockSpec((BR, 2 * NS), lambda i: (0, 0))
    return pl.pallas_call(
        body, grid=(n,),
        in_specs=[pl.BlockSpec((TL, BR), lambda i: (i, 3)), pl.BlockSpec((TL, BR), lambda i: (i, 4)),
                  svec, svec, emb, emb, vec, pl.BlockSpec((BR, BR), lambda i: (0, 0)), vec],
        out_specs=[pl.BlockSpec((TL, BR), lambda i: (i, 0)), pl.BlockSpec((1, 1, 2 * NS), lambda i: (i, 0, 0))],
        out_shape=[jax.ShapeDtypeStruct((L, BR), F32), jax.ShapeDtypeStruct((n, 1, 2 * NS), F32)],
        scratch_shapes=[pltpu.VMEM((1, 2 * NS), F32)],
        name="b_fwd", compiler_params=_cp("arbitrary"))(proj, proj, ar, ai, bemb, cemb, dsk, gw, gb)


def _b_bwd(proj, dmix, sin_all, ar, ai, bemb, cemb, dsk, gw, gb, exchange=None):
    L = proj.shape[0]
    n = L // TL

    def body(u_ref, bz_ref, dy_ref, sin_ref, ar_ref, ai_ref, be_ref, ce_ref, dsk_ref, gw_ref, gb_ref,
             dp_ref, dar_ref, dai_ref, dbe_ref, dce_ref, ddsk_ref, dgw_ref, dgb_ref, carry_ref):
        i = pl.program_id(0)

        @pl.when(i == 0)
        def _():
            carry_ref[...] = jnp.zeros_like(carry_ref)
            for r in (dar_ref, dai_ref, dbe_ref, dce_ref, ddsk_ref, dgw_ref, dgb_ref):
                r[...] = jnp.zeros_like(r)

        u = u_ref[...]
        ar, ai = ar_ref[...], ai_ref[...]
        be_b, ce_b = be_ref[...].astype(BF), ce_ref[...].astype(BF)
        sin = sin_ref[0]
        sr, si = _s5_states(u, be_b, ar, ai, sin[:, :NS], sin[:, NS:])
        s_b = jnp.concatenate([sr, si], axis=1).astype(BF)
        yssm = lax.dot_general(s_b, ce_b, (((1,), (1,)), ((), ())), preferred_element_type=F32)
        _, vj = jax.vjp(_b_post, yssm, u, bz_ref[...], dsk_ref[...], gw_ref[...], gb_ref[...])
        dyssm, du, dbz, ddsk, dgw, dgb = vj(dy_ref[...])
        dy_b = dyssm.astype(BF)
        dce_ref[...] += lax.dot_general(dy_b, s_b, (((0,), (0,)), ((), ())), preferred_element_type=F32)
        gs = jnp.dot(dy_b, ce_b, preferred_element_type=F32)
        last = lax.broadcasted_iota(jnp.int32, (TL, 1), 0) == TL - 1
        cr, ci = carry_ref[:, :NS], carry_ref[:, NS:]
        gr = gs[:, :NS] + jnp.where(last, ar * cr + ai * ci, 0.0)
        gi = gs[:, NS:] + jnp.where(last, ar * ci - ai * cr, 0.0)
        dsr, dsi = _s5_scan(gr, gi, ar, -ai, True)
        carry_ref[:, :NS] = dsr[0:1]
        carry_ref[:, NS:] = dsi[0:1]
        first = lax.broadcasted_iota(jnp.int32, (TL, 1), 0) == 0
        pr = jnp.where(first, sin[:, :NS], _roll(sr, 1))
        pi = jnp.where(first, sin[:, NS:], _roll(si, 1))
        dar_ref[...] += jnp.sum(dsr * pr + dsi * pi, axis=0, keepdims=True)
        dai_ref[...] += jnp.sum(dsi * pr - dsr * pi, axis=0, keepdims=True)
        ds_b = jnp.concatenate([dsr, dsi], axis=1).astype(BF)
        dbe_ref[...] += lax.dot_general(u.astype(BF), ds_b, (((0,), (0,)), ((), ())), preferred_element_type=F32)
        du = du + lax.dot_general(ds_b, be_b, (((1,), (1,)), ((), ())), preferred_element_type=F32)
        dp_ref[:, 0:BR] = du
        dp_ref[:, BR:2 * BR] = dbz
        ddsk_ref[...] += ddsk
        dgw_ref[...] += dgw
        dgb_ref[...] += dgb

    rev = lambda i: n - 1 - i
    vec = pl.BlockSpec((1, BR), lambda i: (0, 0))
    svec = pl.BlockSpec((1, NS), lambda i: (0, 0))
    emb = pl.BlockSpec((BR, 2 * NS), lambda i: (0, 0))
    mat = pl.BlockSpec((BR, BR), lambda i: (0, 0))
    return _sweep_with_exchange(
        body, n,
        in_specs=[pl.BlockSpec((TL, BR), lambda i: (rev(i), 3)), pl.BlockSpec((TL, BR), lambda i: (rev(i), 4)),
                  pl.BlockSpec((TL, BR), lambda i: (rev(i), 1)),
                  pl.BlockSpec((1, 1, 2 * NS), lambda i: (rev(i), 0, 0)),
                  svec, svec, emb, emb, vec, mat, vec],
        out_specs=[pl.BlockSpec((TL, W_B), lambda i: (rev(i), 0)), svec, svec, emb, emb, vec, mat, vec],
        out_shape=[jax.ShapeDtypeStruct((L, W_B), F32)] + [jax.ShapeDtypeStruct((1, NS), F32)] * 2
        + [jax.ShapeDtypeStruct((BR, 2 * NS), F32)] * 2
        + [jax.ShapeDtypeStruct((1, BR), F32), jax.ShapeDtypeStruct((BR, BR), F32), jax.ShapeDtypeStruct((1, BR), F32)],
        scratch_shapes=[pltpu.VMEM((1, 2 * NS), F32)],
        args=(proj, proj, dmix, sin_all, ar, ai, bemb, cemb, dsk, gw, gb), exchange=exchange, name="b_bwd")


def _d_post(cq, ab, dz, s0, s1, s2, s3, p1, p2, ng):
    c = cq.shape[0]
    qkv = _silu(cq)
    gall = -jnp.exp(p1) * jax.nn.softplus(ab + p2)
    ball = _sigmoid(ab)
    row = lax.broadcasted_iota(jnp.int32, (c, c), 0)
    col = lax.broadcasted_iota(jnp.int32, (c, c), 1)
    causal = row >= col
    strict = row > col
    gc_all = _dot_hi(causal.astype(F32), gall)
    gc_t = gc_all.T
    heads = range(DN_H)
    st = (s0, s1, s2, s3)
    q = [qkv[:, h * DN_D:(h + 1) * DN_D] for h in heads]
    k = [qkv[:, BR + h * DN_D:BR + (h + 1) * DN_D] for h in heads]
    v = [qkv[:, 2 * BR + h * DN_D:2 * BR + (h + 1) * DN_D] for h in heads]
    q = [t * lax.rsqrt(jnp.sum(t * t, axis=-1, keepdims=True) + EPS) * (DN_D ** -0.5) for t in q]
    k = [t * lax.rsqrt(jnp.sum(t * t, axis=-1, keepdims=True) + EPS) for t in k]
    gcol = [gc_all[:, h:h + 1] for h in heads]
    beta = [ball[:, DN_H + h:DN_H + h + 1] for h in heads]
    decay = [jnp.where(causal, jnp.exp(jnp.where(causal, gcol[h] - gc_t[h:h + 1, :], 0.0)), 0.0) for h in heads]
    kb = [k[h] * beta[h] for h in heads]
    lm = [jnp.where(strict, _mm_nt(kb[h], k[h]) * decay[h], 0.0) for h in heads]
    ainv = _unit_lower_inv(tuple(lm))
    egc = [jnp.exp(g) for g in gcol]
    uw = [_mm3(ainv[h], jnp.concatenate([v[h] * beta[h], kb[h] * egc[h]], axis=1)) for h in heads]
    attn = [_mm_nt(q[h], k[h]) * decay[h] for h in heads]
    glast = [g[c - 1:c, :] for g in gcol]
    kd = [k[h] * jnp.exp(glast[h] - gcol[h]) for h in heads]
    vnew = [uw[h][:, :DN_D] - _mm(uw[h][:, DN_D:], st[h]) for h in heads]
    o = [_mm(q[h] * egc[h], st[h]) + _mm(attn[h], vnew[h]) for h in heads]
    news = [st[h] * jnp.exp(glast[h]) + _mm_tn(kd[h], vnew[h]) for h in heads]
    outs = [t * lax.rsqrt(jnp.mean(t * t, axis=-1, keepdims=True) + EPS) * ng for t in o]
    yd = jnp.concatenate(outs, axis=1) * _silu(dz)
    return (yd, *news)


def _d_fwd(proj, cw, p1, p2, ng, exchange=None):
    L = proj.shape[0]
    n = L // DN_C

    def body(qkv_ref, ab_ref, dz_ref, hq_ref, cw_ref, p1_ref, p2_ref, ng_ref, o_ref, sall_ref, s_ref):
        i = pl.program_id(0)

        @pl.when(i == 0)
        def _():
            s_ref[...] = jnp.zeros_like(s_ref)

        keep = (i > 0).astype(F32)
        ext = jnp.concatenate([hq_ref[...] * keep, qkv_ref[...]], axis=0)
        cq = _conv_taps(ext, cw_ref, HALO_S, K_DN, DN_C)
        st = [s_ref[h] for h in range(DN_H)]
        for h in range(DN_H):
            sall_ref[0, h] = st[h]
        out = _d_post(cq, ab_ref[...], dz_ref[...], *st, p1_ref[...], p2_ref[...], ng_ref[...])
        o_ref[...] = out[0]
        for h in range(DN_H):
            s_ref[h] = out[1 + h]

    return _sweep_with_exchange(
        body, n,
        in_specs=[pl.BlockSpec((DN_C, 3 * BR), lambda i: (i, 3)), pl.BlockSpec((DN_C, 128), lambda i: (i, 26)),
                  pl.BlockSpec((DN_C, BR), lambda i: (i, 12)),
                  pl.BlockSpec((HALO_S, 3 * BR), _halo_map(DN_C, HALO_S, 3)),
                  pl.BlockSpec((HALO_S, 3 * BR), lambda i: (0, 0)),
                  pl.BlockSpec((1, 128), lambda i: (0, 0)), pl.BlockSpec((1, 128), lambda i: (0, 0)),
                  pl.BlockSpec((1, DN_D), lambda i: (0, 0))],
        out_specs=[pl.BlockSpec((DN_C, BR), lambda i: (i, 0)),
                   pl.BlockSpec((1, DN_H, DN_D, DN_D), lambda i: (i, 0, 0, 0))],
        out_shape=[jax.ShapeDtypeStruct((L, BR), F32), jax.ShapeDtypeStruct((n, DN_H, DN_D, DN_D), F32)],
        scratch_shapes=[pltpu.VMEM((DN_H, DN_D, DN_D), F32)],
        args=(proj, proj, proj, proj, cw, p1, p2, ng), exchange=exchange, name="d_fwd")


def _sweep_with_exchange(body, steps, in_specs, out_specs, out_shape, scratch_shapes, args, exchange, name):
    if exchange is None:
        res = pl.pallas_call(body, grid=(steps,), in_specs=in_specs, out_specs=out_specs, out_shape=out_shape,
                             scratch_shapes=scratch_shapes, name=name, compiler_params=_cp("arbitrary"))(*args)
        return res, None
    xs, axes, mode = exchange
    ex = _Exchange(xs, axes, mode)
    ni, no, ns, na = len(in_specs), len(out_specs), len(scratch_shapes), ex.na

    def carried(*refs):
        ins, xin = refs[:ni], refs[ni:ni + na]
        outs, xout = refs[ni + na:ni + na + no], refs[ni + na + no:ni + 2 * na + no]
        scr, sems = refs[ni + 2 * na + no:ni + 2 * na + no + ns], refs[ni + 2 * na + no + ns:]

        @pl.when(pl.program_id(0) == 0)
        def _():
            ex.start(xin, xout, sems)

        body(*ins, *outs, *scr)

        @pl.when(pl.program_id(0) == steps - 1)
        def _():
            ex.wait(xin, xout, sems)

    res = pl.pallas_call(carried, grid=(steps,), in_specs=list(in_specs) + ex.in_specs,
                         out_specs=list(out_specs) + ex.out_specs, out_shape=list(out_shape) + ex.out_shape,
                         scratch_shapes=list(scratch_shapes) + ex.scratch_shapes, name=name + "_x",
                         compiler_params=_cp("arbitrary"))(*args, *xs)
    return res[:no], res[no:]


def _d_bwd(proj, dmix, sall, cw, p1, p2, ng, exchange=None):
    L = proj.shape[0]
    n = L // DN_C

    def body(qkv_ref, ab_ref, dz_ref, hq_ref, dy_ref, sall_ref, cw_ref, p1_ref, p2_ref, ng_ref,
             dp_ref, dcw_ref, dp1_ref, dp2_ref, dng_ref, ds_ref, carry_ref):
        i = pl.program_id(0)

        @pl.when(i == 0)
        def _():
            ds_ref[...] = jnp.zeros_like(ds_ref)
            carry_ref[...] = jnp.zeros_like(carry_ref)
            for r in (dcw_ref, dp1_ref, dp2_ref, dng_ref):
                r[...] = jnp.zeros_like(r)

        keep = (i < n - 1).astype(F32)
        ext = jnp.concatenate([hq_ref[...] * keep, qkv_ref[...]], axis=0)
        cq = _conv_taps(ext, cw_ref, HALO_S, K_DN, DN_C)
        st = [sall_ref[0, h] for h in range(DN_H)]
        _, vj = jax.vjp(_d_post, cq, ab_ref[...], dz_ref[...], *st, p1_ref[...], p2_ref[...], ng_ref[...])
        grads = vj((dy_ref[...], *[ds_ref[h] for h in range(DN_H)]))
        dcq, dab, ddz = grads[0], grads[1], grads[2]
        for h in range(DN_H):
            ds_ref[h] = grads[3 + h]
        dp1_ref[...] += grads[7]
        dp2_ref[...] += grads[8]
        dng_ref[...] += grads[9]
        dext = _conv_taps_bwd(ext, cw_ref, dcw_ref, dcq, HALO_S, K_DN, DN_C)
        dp_ref[:, 0:3 * BR] = _add_tail(dext[HALO_S:], carry_ref[...])
        carry_ref[...] = dext[:HALO_S]
        dp_ref[:, 3 * BR:4 * BR] = ddz
        dp_ref[:, 4 * BR:4 * BR + 128] = dab

    rev = lambda i: n - 1 - i
    hmap = _halo_map(DN_C, HALO_S, 3)
    v128 = pl.BlockSpec((1, 128), lambda i: (0, 0))
    return _sweep_with_exchange(
        body, n,
        in_specs=[pl.BlockSpec((DN_C, 3 * BR), lambda i: (rev(i), 3)), pl.BlockSpec((DN_C, 128), lambda i: (rev(i), 26)),
                  pl.BlockSpec((DN_C, BR), lambda i: (rev(i), 12)),
                  pl.BlockSpec((HALO_S, 3 * BR), lambda i: hmap(rev(i))),
                  pl.BlockSpec((DN_C, BR), lambda i: (rev(i), 3)),
                  pl.BlockSpec((1, DN_H, DN_D, DN_D), lambda i: (rev(i), 0, 0, 0)),
                  pl.BlockSpec((HALO_S, 3 * BR), lambda i: (0, 0)), v128, v128,
                  pl.BlockSpec((1, DN_D), lambda i: (0, 0))],
        out_specs=[pl.BlockSpec((DN_C, W_D), lambda i: (rev(i), 0)),
                   pl.BlockSpec((HALO_S, 3 * BR), lambda i: (0, 0)), v128, v128,
                   pl.BlockSpec((1, DN_D), lambda i: (0, 0))],
        out_shape=[jax.ShapeDtypeStruct((L, W_D), F32), jax.ShapeDtypeStruct((HALO_S, 3 * BR), F32),
                   jax.ShapeDtypeStruct((1, 128), F32), jax.ShapeDtypeStruct((1, 128), F32),
                   jax.ShapeDtypeStruct((1, DN_D), F32)],
        scratch_shapes=[pltpu.VMEM((DN_H, DN_D, DN_D), F32), pltpu.VMEM((HALO_S, 3 * BR), F32)],
        args=(proj, proj, proj, proj, dmix, sall, cw, p1, p2, ng), exchange=exchange, name="d_bwd")


def _pick_rows(rows, cap):
    best = 8
    for t in range(8, cap + 1, 8):
        if rows % t == 0:
            best = t
    return best


def _adamw(w, g, m, v, name):
    rows, wd = w.shape
    tr = _pick_rows(rows, 512)
    c1 = 1.0 - ADAM_B1 ** ADAM_STEP
    c2 = 1.0 - ADAM_B2 ** ADAM_STEP

    def body(w_ref, g_ref, m_ref, v_ref, d_ref, mo_ref, vo_ref):
        gv = g_ref[...]
        mn = ADAM_B1 * m_ref[...] + (1.0 - ADAM_B1) * gv
        vn = ADAM_B2 * v_ref[...] + (1.0 - ADAM_B2) * (gv * gv)
        d_ref[...] = -ADAM_LR * ((mn / c1) / (jnp.sqrt(vn / c2) + ADAM_EPS) + ADAM_WD * w_ref[...])
        mo_ref[...] = mn
        vo_ref[...] = vn

    spec = pl.BlockSpec((tr, wd), lambda i: (i, 0))
    return pl.pallas_call(
        body, grid=(rows // tr,), in_specs=[spec] * 4, out_specs=[spec] * 3,
        out_shape=[jax.ShapeDtypeStruct((rows, wd), F32)] * 3,
        name=name, compiler_params=_cp("parallel"))(w, g, m, v)


def _sum_slots(r, name):
    n, rows, wd = r.shape
    tr = _pick_rows(rows, 384)

    def body(r_ref, o_ref):
        acc = r_ref[0].astype(F32)
        for j in range(1, n):
            acc = acc + r_ref[j].astype(F32)
        o_ref[...] = acc

    return pl.pallas_call(
        body, grid=(rows // tr,),
        in_specs=[pl.BlockSpec((n, tr, wd), lambda i: (0, i, 0))],
        out_specs=pl.BlockSpec((tr, wd), lambda i: (i, 0)),
        out_shape=jax.ShapeDtypeStruct((rows, wd), F32),
        name=name, compiler_params=_cp("parallel"))(r)


AXES = ("x", "y", "c")


def _group_peer(axes, k):
    pos = {a: lax.axis_index(a) for a in AXES}
    idx = 0
    for a in axes:
        idx = idx * 2 + pos[a]
    peer = dict(pos)
    for b, a in enumerate(reversed(axes)):
        if (k >> b) & 1:
            peer[a] = 1 - pos[a]
    return idx, tuple(peer[a] for a in AXES)


MAX_CHUNKS = 4


class _Exchange:
    def __init__(self, xs, axes, mode):
        self.axes, self.mode, self.na, self.n = axes, mode, len(xs), 2 ** len(axes)
        n = self.n
        self.out_shape, self.pieces = [], []
        for x in xs:
            if mode == "gather":
                shape, lead = (n,) + x.shape, x.shape[0]
            elif mode == "scatter":
                shape, lead = x.shape, x.shape[1]
            else:
                shape, lead = (x.shape[0], n * x.shape[1], x.shape[2]), x.shape[0]
            self.out_shape.append(jax.ShapeDtypeStruct(shape, x.dtype))
            big = x.size * x.dtype.itemsize >= (1 << 20)
            if mode == "rows":
                self.pieces.append(lead if lead <= MAX_CHUNKS else 1)
            else:
                self.pieces.append(MAX_CHUNKS if big and lead % (16 * MAX_CHUNKS) == 0 else 1)
        self.in_specs = [pl.BlockSpec(memory_space=pl.ANY)] * self.na
        self.out_specs = [pl.BlockSpec(memory_space=pl.ANY)] * self.na
        self.scratch_shapes = [pltpu.SemaphoreType.DMA((self.na, MAX_CHUNKS, n)),
                               pltpu.SemaphoreType.DMA((self.na, MAX_CHUNKS, n)),
                               pltpu.SemaphoreType.DMA((self.na, MAX_CHUNKS))]

    def _copies(self, x_refs, o_refs, send_sems, recv_sems, local_sems):
        me, _ = _group_peer(self.axes, 0)
        local, remote = [], []
        for a, (x, o) in enumerate(zip(x_refs, o_refs)):
            for c in range(self.pieces[a]):
                if self.mode == "rows":
                    r = x.shape[1]
                    b = slice(None) if self.pieces[a] == 1 else pl.ds(c, 1)
                    src = lambda k, x=x, b=b: x.at[b]
                    dst = o.at[b, pl.ds(me * r, r)]
                else:
                    lead = x.shape[1] if self.mode == "scatter" else x.shape[0]
                    rs = pl.ds(c * (lead // self.pieces[a]), lead // self.pieces[a])
                    if self.mode == "scatter":
                        src = lambda k, x=x, rs=rs: x.at[me ^ k, rs]
                    else:
                        src = lambda k, x=x, rs=rs: x.at[rs]
                    dst = o.at[me, rs]
                local.append(pltpu.make_async_copy(src(0), dst, local_sems.at[a, c]))
                for k in range(1, self.n):
                    remote.append(pltpu.make_async_remote_copy(
                        src_ref=src(k), dst_ref=dst, send_sem=send_sems.at[a, c, k], recv_sem=recv_sems.at[a, c, k],
                        device_id=_group_peer(self.axes, k)[1], device_id_type=MESH))
        return local, remote

    def start(self, x_refs, o_refs, sems):
        local, remote = self._copies(x_refs, o_refs, *sems)
        for cp in local + remote:
            cp.start()

    def wait(self, x_refs, o_refs, sems):
        local, remote = self._copies(x_refs, o_refs, *sems)
        for cp in remote:
            cp.wait_send()
        for cp in remote:
            cp.wait_recv()
        for cp in local:
            cp.wait()


def _exchange(xs, axes, mode, name):
    ex = _Exchange(xs, axes, mode)
    na = ex.na

    def body(*refs):
        ex.start(refs[:na], refs[na:2 * na], refs[2 * na:])
        ex.wait(refs[:na], refs[na:2 * na], refs[2 * na:])

    return pl.pallas_call(body, out_shape=ex.out_shape, in_specs=ex.in_specs, out_specs=ex.out_specs,
                          scratch_shapes=ex.scratch_shapes, name=name)(*xs)


SHARDED_SMALL = (("a_conv_w", 2), ("a_pw_w", 1), ("s5_glu_w", 1), ("c_conv_w", 2), ("d_conv_w", 2))
REPLICATED = ("norm_g", "a_conv_b", "a_ln_g", "a_ln_b", "a_pw_b", "s5_lambda_re", "s5_lambda_im", "s5_b_re",
              "s5_b_im", "s5_c_re", "s5_c_im", "s5_d", "s5_log_dt", "s5_glu_b", "d_a_log", "d_dt_bias",
              "d_norm_g", "final_g")
WEIGHTS = ("norm_g", "w_in", "a_conv_w", "a_conv_b", "a_ln_g", "a_ln_b", "a_pw_w", "a_pw_b", "s5_lambda_re",
           "s5_lambda_im", "s5_b_re", "s5_b_im", "s5_c_re", "s5_c_im", "s5_d", "s5_log_dt", "s5_glu_w",
           "s5_glu_b", "c_conv_w", "d_conv_w", "d_a_log", "d_dt_bias", "d_norm_g", "w_out", "final_g")
LANES = 1024


def _size(shape):
    size = 1
    for d in shape:
        size *= d
    return size


def _slab_rows(shape):
    return -(-_size(shape) // (8 * LANES)) * 8


def _pack(arrs, rows):
    parts = []
    for a in arrs:
        r = _slab_rows(a.shape)
        parts.append(jnp.pad(a.reshape(-1), (0, r * LANES - a.size)).reshape(r, LANES))
    used = sum(p.shape[0] for p in parts)
    if rows > used:
        parts.append(jnp.zeros((rows - used, LANES), parts[0].dtype))
    return jnp.concatenate(parts, axis=0)


def _unpack(slab, shapes):
    out, off = [], 0
    for s in shapes:
        r = _slab_rows(s)
        out.append(slab[off:off + r].reshape(-1)[:_size(s)].reshape(s))
        off += r
    return out


def _rows_for(shapes, mult):
    rows = sum(_slab_rows(s) for s in shapes)
    return -(-rows // mult) * mult


def _row(v, width=None):
    v = v.reshape(1, -1)
    return v if width is None else jnp.pad(v, ((0, 0), (0, width - v.shape[1])))


def _pad_rows(w, rows):
    return jnp.pad(w, ((0, rows - w.shape[0]), (0, 0)))


def _permute_in(w):
    return jnp.concatenate([w[:, :3072], w[:, 3080:N_IN], w[:, 3072:3080],
                            jnp.zeros((w.shape[0], N_INP - N_IN), w.dtype)], axis=1)


def _layer_fwd(x, p, exchange=None):
    proj, h = _proj_fwd(x, p["norm_g"], p["wp"])
    ya = _a_fwd(proj, p["a_cw"], p["a_cb"], p["a_lng"], p["a_lnb"], p["a_pw"], p["a_pwb"])
    ar, ai, bemb, cemb = _s5_prep(*p["s5"])
    yb, sin_all = _b_fwd(proj, ar, ai, bemb, cemb, p["s5_d"], p["glu_w"], p["glu_b"])
    yc = _c_fwd(proj, p["c_cw"])
    (yd, sall), got = _d_fwd(proj, p["d_cw"], p["d_p1"], p["d_p2"], p["d_ng"], exchange)
    xo = _out_fwd(x, ya, yb, yc, yd, p["wo"])
    return xo, dict(x=x, proj=proj, h=h, ys=(ya, yb, yc, yd), sin_all=sin_all, sall=sall,
                    s5=(ar, ai, bemb, cemb)), got


def _layer_bwd(dxo, p, r, exchanges):
    proj = r["proj"]
    ar, ai, bemb, cemb = r["s5"]
    dmix = _out_bwd_x(dxo, p["wo"])
    dwo = _dwout(*r["ys"], dxo)
    dpa, dcw_a, dcb, dlng, dlnb, dpw, dpwb = _a_bwd(proj, dmix, p["a_cw"], p["a_cb"], p["a_lng"], p["a_lnb"],
                                                     p["a_pw"], p["a_pwb"])
    got = {}
    (dpb, dar, dai, dbe, dce, ddsk, dgw, dgb), got["b"] = _b_bwd(
        proj, dmix, r["sin_all"], ar, ai, bemb, cemb, p["s5_d"], p["glu_w"], p["glu_b"], exchanges.get("b"))
    dlre, dlim, dldt, dbre, dbim, dcre, dcim = _s5_prep_bwd(*p["s5"], dar, dai, dbe, dce)
    dpc, dcw_c = _c_bwd(proj, dmix, p["c_cw"])
    (dpd, dcw_d, dp1, dp2, dng), got["d"] = _d_bwd(proj, dmix, r["sall"], p["d_cw"], p["d_p1"], p["d_p2"],
                                                   p["d_ng"], exchanges.get("d"))
    ex_proj = exchanges.get("proj")
    if callable(ex_proj):
        ex_proj = ex_proj(got["d"])
    (dx, dg), got["proj"] = _proj_bwd_x(r["x"], p["norm_g"], dpa, dpb, dpc, dpd, p["wp"], dxo, ex_proj)
    dwa, dwb, dwc, dwd = _dwin(r["h"], dpa, dpb, dpc, dpd)
    dwin = jnp.concatenate([dwa, dwb, dwc, dwd[:, :3 * BR], dwd[:, 4 * BR:4 * BR + 2 * DN_H],
                            dwd[:, 3 * BR:4 * BR]], axis=1)

    def unrows(t, perm):
        return jnp.transpose(t.reshape(S5_H, S5_G, S5_P), perm)

    grads = dict(
        norm_g=dg.reshape(-1), w_in=dwin, a_conv_w=dcw_a[:K_A], a_conv_b=dcb.reshape(-1),
        a_ln_g=dlng.reshape(-1), a_ln_b=dlnb.reshape(-1), a_pw_w=dpw, a_pw_b=dpwb.reshape(-1),
        s5_lambda_re=dlre.reshape(S5_G, S5_P), s5_lambda_im=dlim.reshape(S5_G, S5_P),
        s5_b_re=unrows(dbre, (1, 2, 0)), s5_b_im=unrows(dbim, (1, 2, 0)),
        s5_c_re=unrows(dcre, (1, 0, 2)), s5_c_im=unrows(dcim, (1, 0, 2)),
        s5_d=ddsk.reshape(-1), s5_log_dt=dldt[0, :S5_G], s5_glu_w=dgw, s5_glu_b=dgb.reshape(-1),
        c_conv_w=dcw_c[:K_C], d_conv_w=dcw_d[:K_DN], d_a_log=dp1[0, :DN_H], d_dt_bias=dp2[0, :DN_H],
        d_norm_g=dng.reshape(-1), w_out=dwo)
    return dx, grads, got


def _layer_params(full, wp, wo, l):
    return dict(
        norm_g=_row(full["norm_g"][l]), wp=wp,
        a_cw=_pad_rows(full["a_conv_w"][l], HALO_A), a_cb=_row(full["a_conv_b"][l]),
        a_lng=_row(full["a_ln_g"][l]), a_lnb=_row(full["a_ln_b"][l]), a_pw=full["a_pw_w"][l],
        a_pwb=_row(full["a_pw_b"][l]),
        s5=(_row(full["s5_lambda_re"][l]), _row(full["s5_lambda_im"][l]), _row(full["s5_log_dt"][l], 128),
            jnp.transpose(full["s5_b_re"][l], (2, 0, 1)).reshape(S5_H, NS),
            jnp.transpose(full["s5_b_im"][l], (2, 0, 1)).reshape(S5_H, NS),
            jnp.transpose(full["s5_c_re"][l], (1, 0, 2)).reshape(S5_H, NS),
            jnp.transpose(full["s5_c_im"][l], (1, 0, 2)).reshape(S5_H, NS)),
        s5_d=_row(full["s5_d"][l]), glu_w=full["s5_glu_w"][l], glu_b=_row(full["s5_glu_b"][l]),
        c_cw=_pad_rows(full["c_conv_w"][l], HALO_S), d_cw=_pad_rows(full["d_conv_w"][l], HALO_S),
        d_p1=_row(full["d_a_log"][l], 128), d_p2=_row(full["d_dt_bias"][l], 128),
        d_ng=_row(full["d_norm_g"][l]), wo=wo)


def kernel(x, norm_g, w_in, a_conv_w, a_conv_b, a_ln_g, a_ln_b, a_pw_w, a_pw_b, s5_lambda_re, s5_lambda_im, s5_b_re, s5_b_im, s5_c_re, s5_c_im, s5_d, s5_log_dt, s5_glu_w, s5_glu_b, c_conv_w, d_conv_w, d_a_log, d_dt_bias, d_norm_g, w_out, final_g, loss_target, m_norm_g, m_w_in, m_a_conv_w, m_a_conv_b, m_a_ln_g, m_a_ln_b, m_a_pw_w, m_a_pw_b, m_s5_lambda_re, m_s5_lambda_im, m_s5_b_re, m_s5_b_im, m_s5_c_re, m_s5_c_im, m_s5_d, m_s5_log_dt, m_s5_glu_w, m_s5_glu_b, m_c_conv_w, m_d_conv_w, m_d_a_log, m_d_dt_bias, m_d_norm_g, m_w_out, m_final_g, v_norm_g, v_w_in, v_a_conv_w, v_a_conv_b, v_a_ln_g, v_a_ln_b, v_a_pw_w, v_a_pw_b, v_s5_lambda_re, v_s5_lambda_im, v_s5_b_re, v_s5_b_im, v_s5_c_re, v_s5_c_im, v_s5_d, v_s5_log_dt, v_s5_glu_w, v_s5_glu_b, v_c_conv_w, v_d_conv_w, v_d_a_log, v_d_dt_bias, v_d_norm_g, v_w_out, v_final_g):
    given = dict(locals())
    w = {n: given[n] for n in WEIGHTS}
    m = {n: given["m_" + n] for n in WEIGHTS}
    v = {n: given["v_" + n] for n in WEIGHTS}
    xs, tgt = x[0], loss_target[0]

    n_in, n_out = w["w_in"].shape[2], w["w_out"].shape[1]
    sm_names = [n for n, _ in SHARDED_SMALL]
    sm_shapes = [w[n].shape for n in sm_names]
    sm_rows = _rows_for(sm_shapes, 16)
    win_b, wout_b = w["w_in"].astype(BF), w["w_out"].astype(BF)
    g_in, g_out, g_sm = _exchange([win_b[0], wout_b[0], _pack([w[n] for n in sm_names], sm_rows)],
                                  ("x", "y"), "gather", "gather_first")
    full = dict(w)
    parts = [_unpack(g_sm[j], sm_shapes) for j in range(4)]
    for i, (n, ax) in enumerate(SHARDED_SMALL):
        full[n] = jnp.concatenate([parts[j][i] for j in range(4)], axis=ax)

    saved = []
    h = xs
    for l in range(DEPTH):
        p = _layer_params(full, _permute_in(jnp.concatenate([g_in[j] for j in range(4)], axis=1)),
                          jnp.concatenate([g_out[j] for j in range(4)], axis=0), l)
        nxt = ([win_b[l + 1], wout_b[l + 1]], ("x", "y"), "gather") if l + 1 < DEPTH else None
        h, r, got = _layer_fwd(h, p, nxt)
        saved.append((p, r))
        if got is not None:
            g_in, g_out = got
    loss_tile, dx, dfg = _loss_bwd(h, _row(full["final_g"]), tgt)

    def big_slots(g):
        s_in = jnp.stack([g["w_in"][:, j * n_in:(j + 1) * n_in].astype(BF) for j in range(4)])
        return [s_in.reshape(8, D_MODEL // 2, n_in), g["w_out"].astype(BF).reshape(8, n_out // 2, D_MODEL)]

    def halves(rv):
        return [_sum_slots(rv[0], "sum_w_in")[None], _sum_slots(rv[1], "sum_w_out")[None]]

    layer_grads, summed, pending, arrived = [None] * DEPTH, [None] * DEPTH, None, {}
    for l in reversed(range(DEPTH)):
        p, r = saved[l]
        ex = {}
        if pending is not None:
            ex["d"] = (pending, AXES, "scatter")
        if l + 2 in arrived:
            ex["b"] = (halves(arrived.pop(l + 2)), ("c",), "rows")
        if l == 0:
            ex["proj"] = lambda came: (halves(came), ("c",), "rows")
        dx, layer_grads[l], got = _layer_bwd(dx, p, r, ex)
        if got["b"] is not None:
            summed[l + 2] = got["b"]
        if got["proj"] is not None:
            summed[1] = got["proj"]
        elif got["d"] is not None:
            arrived[l + 1] = got["d"]
        pending = big_slots(layer_grads[l])
    grads = {n: jnp.stack([layer_grads[l][n] for l in range(DEPTH)]) for n in WEIGHTS
             if n not in ("final_g", "w_in", "w_out")}
    grads["final_g"] = dfg.reshape(-1)
    slots = []
    for j in range(4):
        sl = [lax.slice_in_dim(grads[n], j * w[n].shape[ax], (j + 1) * w[n].shape[ax], axis=ax)
              for n, ax in SHARDED_SMALL]
        slots.append(_pack(sl, sm_rows))
    rp_shapes = [w[n].shape for n in REPLICATED] + [(1,)]
    rp_rows = _rows_for(rp_shapes, 64)
    r_in0, r_out0, r_sm, r_rp = _exchange(
        pending + [jnp.stack(slots).reshape(8, sm_rows // 2, LANES),
                   _pack([grads[n] for n in REPLICATED] + [loss_tile[0, 0:1]], rp_rows).reshape(8, rp_rows // 8, LANES)],
        AXES, "scatter", "scatter_last")
    summed[0] = _exchange(halves((r_in0, r_out0)) + [_sum_slots(r_sm, "sum_small")[None]], ("c",), "rows",
                          "gather_halves")
    h_sm = summed[0][2]
    h_in = jnp.concatenate([summed[l][0] for l in range(DEPTH)], axis=0)
    h_out = jnp.concatenate([summed[l][1] for l in range(DEPTH)], axis=0)
    (g_rp,) = _exchange([_sum_slots(r_rp, "sum_replicated")], AXES, "gather", "gather_replicated")
    g_rp = g_rp.reshape(rp_rows, LANES)
    g_sm = h_sm.reshape(sm_rows, LANES)

    out = {}

    def put(name, shape, res):
        for key, t in zip(("delta", "new_m", "new_v"), res):
            out[key + "_" + name] = t.reshape(shape)

    for name, g2 in (("w_in", h_in.reshape(DEPTH * D_MODEL, n_in)), ("w_out", h_out.reshape(DEPTH * n_out, D_MODEL))):
        shape = w[name].shape
        out["grad_" + name] = g2.reshape(shape)
        put(name, shape, _adamw(w[name].reshape(g2.shape), g2, m[name].reshape(g2.shape), v[name].reshape(g2.shape),
                                "adamw_" + name))
    zero = jnp.zeros((1,), F32)
    res_sm = _adamw(_pack([w[n] for n in sm_names], sm_rows), g_sm, _pack([m[n] for n in sm_names], sm_rows),
                    _pack([v[n] for n in sm_names], sm_rows), "adamw_small")
    res_rp = _adamw(_pack([w[n] for n in REPLICATED] + [zero], rp_rows), g_rp,
                    _pack([m[n] for n in REPLICATED] + [zero], rp_rows),
                    _pack([v[n] for n in REPLICATED] + [zero], rp_rows), "adamw_replicated")
    for key, sm, rp in (("grad", g_sm, g_rp), ("delta", res_sm[0], res_rp[0]), ("new_m", res_sm[1], res_rp[1]),
                        ("new_v", res_sm[2], res_rp[2])):
        for n, t in zip(sm_names, _unpack(sm, sm_shapes)):
            out[key + "_" + n] = t
        for n, t in zip(REPLICATED, _unpack(rp, rp_shapes[:-1])):
            out[key + "_" + n] = t
    loss = _unpack(g_rp, rp_shapes)[-1].reshape(())
    return (loss, dx[None], *[out["grad_" + n] for n in WEIGHTS], *[out["delta_" + n] for n in WEIGHTS],
            *[out["new_m_" + n] for n in WEIGHTS], *[out["new_v_" + n] for n in WEIGHTS])
```

```python
import functools

import jax
import jax.numpy as jnp
from jax import lax
from jax.experimental import pallas as pl
from jax.experimental.pallas import tpu as pltpu

F32, BF = jnp.float32, jnp.bfloat16
HI = lax.Precision.HIGHEST
MESH = pl.DeviceIdType.MESH

D_MODEL = 1024
BR = 256
DEPTH = 4
N_IN = 3336
N_INP = 3456
COL_A, COL_B, COL_C, COL_D = 0, 768, 1280, 2304
W_A, W_B, W_C, W_D = 768, 512, 1024, 1152
S5_G, S5_H, S5_P = 16, 16, 64
NS = S5_G * S5_P
DN_H, DN_D, DN_C = 4, 64, 64
DN_G_FWD, DN_G_BWD = 2, 1
K_A, K_C, K_DN = 31, 3, 4
HALO_A, HALO_S = 32, 8
EPS = 1e-6
TL = 256
VMEM_LIMIT = 56 * 1024 * 1024

ADAM_LR, ADAM_B1, ADAM_B2, ADAM_EPS, ADAM_WD, ADAM_STEP = 0.001, 0.9, 0.999, 1e-08, 0.01, 10


def _cp(*sem):
    return pltpu.CompilerParams(dimension_semantics=sem, vmem_limit_bytes=VMEM_LIMIT)


def _sigmoid(x):
    return jax.nn.sigmoid(x)


def _silu(x):
    return x * jax.nn.sigmoid(x)


def _rmsnorm(x, g):
    return x * lax.rsqrt(jnp.mean(x * x, axis=-1, keepdims=True) + EPS) * g


@jax.custom_vjp
def _mm(a, w):
    return jnp.dot(a.astype(BF), w.astype(BF), preferred_element_type=F32)


def _mm_f(a, w):
    return _mm(a, w), (a, w)


def _mm_b(res, g):
    a, w = res
    gb = g.astype(BF)
    da = lax.dot_general(gb, w.astype(BF), (((1,), (1,)), ((), ())), preferred_element_type=F32)
    dw = lax.dot_general(a.astype(BF), gb, (((0,), (0,)), ((), ())), preferred_element_type=F32)
    return da, dw


_mm.defvjp(_mm_f, _mm_b)


@jax.custom_vjp
def _mm_nt(a, b):
    return lax.dot_general(a.astype(BF), b.astype(BF), (((1,), (1,)), ((), ())), preferred_element_type=F32)


def _mm_nt_f(a, b):
    return _mm_nt(a, b), (a, b)


def _mm_nt_b(res, g):
    a, b = res
    gb = g.astype(BF)
    da = jnp.dot(gb, b.astype(BF), preferred_element_type=F32)
    db = lax.dot_general(gb, a.astype(BF), (((0,), (0,)), ((), ())), preferred_element_type=F32)
    return da, db


_mm_nt.defvjp(_mm_nt_f, _mm_nt_b)


@jax.custom_vjp
def _mm_tn(a, b):
    return lax.dot_general(a.astype(BF), b.astype(BF), (((0,), (0,)), ((), ())), preferred_element_type=F32)


def _mm_tn_f(a, b):
    return _mm_tn(a, b), (a, b)


def _mm_tn_b(res, g):
    a, b = res
    gb = g.astype(BF)
    da = lax.dot_general(b.astype(BF), gb, (((1,), (1,)), ((), ())), preferred_element_type=F32)
    db = jnp.dot(a.astype(BF), gb, preferred_element_type=F32)
    return da, db


_mm_tn.defvjp(_mm_tn_f, _mm_tn_b)


def _dot_hi(a, b):
    return jnp.dot(a, b, precision=HI, preferred_element_type=F32)


def _split(a):
    hi = a.astype(BF)
    return hi, (a - hi.astype(F32)).astype(BF)


def _dot3(a, b, dims=(((1,), (0,)), ((), ()))):
    ah, al = _split(a)
    bh, bl = _split(b)
    d = functools.partial(lax.dot_general, dimension_numbers=dims, preferred_element_type=F32)
    return d(ah, bh) + d(ah, bl) + d(al, bh)


@jax.custom_vjp
def _mm3(a, b):
    return _dot3(a, b)


def _mm3_f(a, b):
    return _dot3(a, b), (a, b)


def _mm3_b(res, g):
    a, b = res
    return _dot3(g, b, (((1,), (1,)), ((), ()))), _dot3(a, g, (((0,), (0,)), ((), ())))


_mm3.defvjp(_mm3_f, _mm3_b)


@jax.custom_vjp
def _unit_lower_inv(lms):
    n = lms[0].shape[0]
    row = lax.broadcasted_iota(jnp.int32, lms[0].shape, 0)
    col = lax.broadcasted_iota(jnp.int32, lms[0].shape, 1)
    eye = (row == col).astype(F32)
    accs = [eye - lm for lm in lms]
    pws = list(lms)
    k = 2
    while k < n:
        pws = [_dot3(p, p) for p in pws]
        accs = [a + _dot3(a, p) for a, p in zip(accs, pws)]
        k *= 2
    return tuple(accs)


def _uli_f(lms):
    a = _unit_lower_inv(lms)
    return a, a


def _uli_b(a, g):
    ats = [x.T for x in a]
    tmp = [_dot3(at, gi) for at, gi in zip(ats, g)]
    return (tuple(-_dot3(t, at) for t, at in zip(tmp, ats)),)


_unit_lower_inv.defvjp(_uli_f, _uli_b)


@jax.custom_vjp
def _known_inverse(lms, inv):
    return inv


def _ki_f(lms, inv):
    return inv, inv


def _ki_b(a, g):
    return _uli_b(a, g)[0], tuple(jnp.zeros_like(x) for x in a)


_known_inverse.defvjp(_ki_f, _ki_b)


def _roll(x, s):
    n = x.shape[0]
    s = s % n
    return x if s == 0 else pltpu.roll(x, s, 0)


def _conv_taps(ext, w_ref, halo, k_taps, tl):
    acc = None
    for k in range(k_taps):
        term = _roll(ext, (k_taps - 1) - k)[halo:halo + tl] * w_ref[k:k + 1, :]
        acc = term if acc is None else acc + term
    return acc


def _conv_taps_bwd(ext, w_ref, dw_ref, dacc, halo, k_taps, tl):
    dpad = jnp.concatenate([dacc, jnp.zeros((halo, dacc.shape[1]), F32)], axis=0)
    dext = None
    for k in range(k_taps):
        r = _roll(ext, (k_taps - 1) - k)[halo:halo + tl]
        dw_ref[k:k + 1, :] += jnp.sum(r * dacc, axis=0, keepdims=True)
        term = _roll(dpad, halo - (k_taps - 1) + k) * w_ref[k:k + 1, :]
        dext = term if dext is None else dext + term
    return dext


def _add_tail(x, tail):
    tl, h = x.shape[0], tail.shape[0]
    return x + jnp.concatenate([jnp.zeros((tl - h, x.shape[1]), F32), tail], axis=0)


def _proj_fwd(x, g, wp):
    L = x.shape[0]

    def body(x_ref, g_ref, w_ref, p_ref, h_ref):
        hb = _rmsnorm(x_ref[...], g_ref[...]).astype(BF)
        h_ref[...] = hb
        p_ref[...] = jnp.dot(hb, w_ref[...], preferred_element_type=F32)

    return pl.pallas_call(
        body, grid=(L // TL,),
        in_specs=[pl.BlockSpec((TL, D_MODEL), lambda i: (i, 0)),
                  pl.BlockSpec((1, D_MODEL), lambda i: (0, 0)),
                  pl.BlockSpec((D_MODEL, N_INP), lambda i: (0, 0))],
        out_specs=[pl.BlockSpec((TL, N_INP), lambda i: (i, 0)),
                   pl.BlockSpec((TL, D_MODEL), lambda i: (i, 0))],
        out_shape=[jax.ShapeDtypeStruct((L, N_INP), F32), jax.ShapeDtypeStruct((L, D_MODEL), BF)],
        name="proj_fwd", compiler_params=_cp("parallel"))(x, g, wp)


def _proj_bwd_x(x, g, dpa, dpb, dpc, dpd, wp, dxo, exchange=None):
    L = x.shape[0]

    def body(x_ref, g_ref, a_ref, b_ref, c_ref, d_ref, w_ref, dxo_ref, dx_ref, dg_ref):
        dh = None
        for ref, c0, wd in ((a_ref, COL_A, W_A), (b_ref, COL_B, W_B), (c_ref, COL_C, W_C), (d_ref, COL_D, W_D)):
            t = lax.dot_general(ref[...].astype(BF), w_ref[:, c0:c0 + wd], (((1,), (1,)), ((), ())),
                                preferred_element_type=F32)
            dh = t if dh is None else dh + t
        _, vj = jax.vjp(_rmsnorm, x_ref[...], g_ref[...])
        dx, dg = vj(dh)
        dx_ref[...] = dxo_ref[...] + dx

        @pl.when(pl.program_id(0) == 0)
        def _():
            dg_ref[...] = jnp.zeros_like(dg_ref)

        dg_ref[...] += dg

    def rows(wd):
        return pl.BlockSpec((TL, wd), lambda i: (i, 0))

    return _sweep_with_exchange(
        body, L // TL,
        in_specs=[rows(D_MODEL), pl.BlockSpec((1, D_MODEL), lambda i: (0, 0)),
                  rows(W_A), rows(W_B), rows(W_C), rows(W_D),
                  pl.BlockSpec((D_MODEL, N_INP), lambda i: (0, 0)), rows(D_MODEL)],
        out_specs=[rows(D_MODEL), pl.BlockSpec((1, D_MODEL), lambda i: (0, 0))],
        out_shape=[jax.ShapeDtypeStruct((L, D_MODEL), F32), jax.ShapeDtypeStruct((1, D_MODEL), F32)],
        scratch_shapes=[], args=(x, g, dpa, dpb, dpc, dpd, wp, dxo), exchange=exchange, name="proj_bwd_x")


def _dwin(h, dpa, dpb, dpc, dpd):
    L = h.shape[0]

    def body(h_ref, a_ref, b_ref, c_ref, d_ref, oa_ref, ob_ref, oc_ref, od_ref):
        outs = (oa_ref, ob_ref, oc_ref, od_ref)

        @pl.when(pl.program_id(0) == 0)
        def _():
            for o in outs:
                o[...] = jnp.zeros_like(o)

        ht = h_ref[...].T
        for ref, o in zip((a_ref, b_ref, c_ref, d_ref), outs):
            o[...] += jnp.dot(ht, ref[...].astype(BF), preferred_element_type=F32)

    def rows(wd):
        return pl.BlockSpec((TL, wd), lambda i: (i, 0))

    def whole(wd):
        return pl.BlockSpec((D_MODEL, wd), lambda i: (0, 0))

    widths = (W_A, W_B, W_C, W_D)
    return pl.pallas_call(
        body, grid=(L // TL,),
        in_specs=[rows(D_MODEL)] + [rows(wd) for wd in widths],
        out_specs=[whole(wd) for wd in widths],
        out_shape=[jax.ShapeDtypeStruct((D_MODEL, wd), F32) for wd in widths],
        name="dwin", compiler_params=_cp("arbitrary"))(h, dpa, dpb, dpc, dpd)


def _dwout(ya, yb, yc, yd, dxo):
    L = dxo.shape[0]
    tk, tn = min(512, L), 512

    def body(a_ref, b_ref, c_ref, d_ref, g_ref, o_ref):
        @pl.when(pl.program_id(1) == 0)
        def _():
            o_ref[...] = jnp.zeros_like(o_ref)

        gb = g_ref[...].astype(BF)
        for j, ref in enumerate((a_ref, b_ref, c_ref, d_ref)):
            o_ref[j * BR:(j + 1) * BR, :] += lax.dot_general(ref[...].astype(BF), gb, (((0,), (0,)), ((), ())),
                                                             preferred_element_type=F32)

    ys = pl.BlockSpec((tk, BR), lambda j, t: (t, 0))
    return pl.pallas_call(
        body, grid=(D_MODEL // tn, L // tk),
        in_specs=[ys, ys, ys, ys, pl.BlockSpec((tk, tn), lambda j, t: (t, j))],
        out_specs=pl.BlockSpec((D_MODEL, tn), lambda j, t: (0, j)),
        out_shape=jax.ShapeDtypeStruct((D_MODEL, D_MODEL), F32),
        name="dwout", compiler_params=_cp("parallel", "arbitrary"))(ya, yb, yc, yd, dxo)


def _out_fwd(x, ya, yb, yc, yd, wo):
    L = x.shape[0]

    def body(x_ref, a_ref, b_ref, c_ref, d_ref, w_ref, o_ref):
        acc = x_ref[...]
        for j, ref in enumerate((a_ref, b_ref, c_ref, d_ref)):
            acc = acc + jnp.dot(ref[...].astype(BF), w_ref[j * BR:(j + 1) * BR, :], preferred_element_type=F32)
        o_ref[...] = acc

    def rows(wd):
        return pl.BlockSpec((TL, wd), lambda i: (i, 0))

    return pl.pallas_call(
        body, grid=(L // TL,),
        in_specs=[rows(D_MODEL), rows(BR), rows(BR), rows(BR), rows(BR),
                  pl.BlockSpec((D_MODEL, D_MODEL), lambda i: (0, 0))],
        out_specs=rows(D_MODEL), out_shape=jax.ShapeDtypeStruct((L, D_MODEL), F32),
        name="out_fwd", compiler_params=_cp("parallel"))(x, ya, yb, yc, yd, wo)


def _out_bwd_x(dxo, wo):
    L = dxo.shape[0]

    def body(d_ref, w_ref, o_ref):
        o_ref[...] = lax.dot_general(d_ref[...].astype(BF), w_ref[...], (((1,), (1,)), ((), ())),
                                     preferred_element_type=F32)

    return pl.pallas_call(
        body, grid=(L // TL,),
        in_specs=[pl.BlockSpec((TL, D_MODEL), lambda i: (i, 0)), pl.BlockSpec((D_MODEL, D_MODEL), lambda i: (0, 0))],
        out_specs=pl.BlockSpec((TL, D_MODEL), lambda i: (i, 0)),
        out_shape=jax.ShapeDtypeStruct((L, D_MODEL), F32),
        name="out_bwd_x", compiler_params=_cp("parallel"))(dxo, wo)


def _loss_bwd(x, g, tgt):
    L = x.shape[0]

    def f(xv, gv, tv):
        err = _rmsnorm(xv, gv) - tv
        return 0.5 * jnp.sum(jnp.mean(err * err, axis=-1, keepdims=True), axis=0, keepdims=True)

    def body(x_ref, g_ref, t_ref, loss_ref, dx_ref, dg_ref):
        tv = t_ref[...]
        loss, vj = jax.vjp(lambda a, b: f(a, b, tv), x_ref[...], g_ref[...])
        dx, dg = vj(jnp.ones((1, 1), F32))
        dx_ref[...] = dx

        @pl.when(pl.program_id(0) == 0)
        def _():
            dg_ref[...] = jnp.zeros_like(dg_ref)
            loss_ref[...] = jnp.zeros_like(loss_ref)

        dg_ref[...] += dg
        loss_ref[...] += jnp.broadcast_to(loss, loss_ref.shape)

    return pl.pallas_call(
        body, grid=(L // TL,),
        in_specs=[pl.BlockSpec((TL, D_MODEL), lambda i: (i, 0)), pl.BlockSpec((1, D_MODEL), lambda i: (0, 0)),
                  pl.BlockSpec((TL, D_MODEL), lambda i: (i, 0))],
        out_specs=[pl.BlockSpec((8, 128), lambda i: (0, 0)), pl.BlockSpec((TL, D_MODEL), lambda i: (i, 0)),
                   pl.BlockSpec((1, D_MODEL), lambda i: (0, 0))],
        out_shape=[jax.ShapeDtypeStruct((8, 128), F32), jax.ShapeDtypeStruct((L, D_MODEL), F32),
                   jax.ShapeDtypeStruct((1, D_MODEL), F32)],
        name="loss_bwd", compiler_params=_cp("arbitrary"))(x, g, tgt)


def _a_pre(val, gate):
    return val * _sigmoid(gate)


def _a_post(acc, az, cb, lng, lnb, pw, pwb):
    t = acc + cb
    mu = jnp.mean(t, axis=-1, keepdims=True)
    xc = t - mu
    ln = xc * lax.rsqrt(jnp.mean(xc * xc, axis=-1, keepdims=True) + EPS) * lng + lnb
    return (_mm(_silu(ln), pw) + pwb) * _silu(az)


def _halo_map(tl, halo, col):
    r = tl // halo
    return lambda i: (jnp.maximum(i * r - 1, 0), col)


def _a_fwd(proj, cw, cb, lng, lnb, pw, pwb):
    L = proj.shape[0]

    def body(vg_ref, az_ref, hvg_ref, cw_ref, cb_ref, lng_ref, lnb_ref, pw_ref, pwb_ref, o_ref):
        keep = (pl.program_id(0) > 0).astype(F32)
        a_h = _a_pre(hvg_ref[:, 0:BR], hvg_ref[:, BR:2 * BR]) * keep
        a_t = _a_pre(vg_ref[:, 0:BR], vg_ref[:, BR:2 * BR])
        ext = jnp.concatenate([a_h, a_t], axis=0)
        acc = _conv_taps(ext, cw_ref, HALO_A, K_A, TL)
        o_ref[...] = _a_post(acc, az_ref[...], cb_ref[...], lng_ref[...], lnb_ref[...], pw_ref[...], pwb_ref[...])

    vec = pl.BlockSpec((1, BR), lambda i: (0, 0))
    return pl.pallas_call(
        body, grid=(L // TL,),
        in_specs=[pl.BlockSpec((TL, 2 * BR), lambda i: (i, 0)), pl.BlockSpec((TL, BR), lambda i: (i, 2)),
                  pl.BlockSpec((HALO_A, 2 * BR), _halo_map(TL, HALO_A, 0)),
                  pl.BlockSpec((HALO_A, BR), lambda i: (0, 0)), vec, vec, vec,
                  pl.BlockSpec((BR, BR), lambda i: (0, 0)), vec],
        out_specs=pl.BlockSpec((TL, BR), lambda i: (i, 0)),
        out_shape=jax.ShapeDtypeStruct((L, BR), F32),
        name="a_fwd", compiler_params=_cp("parallel"))(proj, proj, proj, cw, cb, lng, lnb, pw, pwb)


def _a_bwd(proj, dmix, cw, cb, lng, lnb, pw, pwb):
    L = proj.shape[0]
    n = L // TL

    def body(vg_ref, az_ref, hvg_ref, dy_ref, cw_ref, cb_ref, lng_ref, lnb_ref, pw_ref, pwb_ref,
             dp_ref, dcw_ref, dcb_ref, dlng_ref, dlnb_ref, dpw_ref, dpwb_ref, carry_ref):
        i = pl.program_id(0)

        @pl.when(i == 0)
        def _():
            carry_ref[...] = jnp.zeros_like(carry_ref)
            for r in (dcw_ref, dcb_ref, dlng_ref, dlnb_ref, dpw_ref, dpwb_ref):
                r[...] = jnp.zeros_like(r)

        keep = (i < n - 1).astype(F32)
        val, gate = vg_ref[:, 0:BR], vg_ref[:, BR:2 * BR]
        a_h = _a_pre(hvg_ref[:, 0:BR], hvg_ref[:, BR:2 * BR]) * keep
        a_t, vj_pre = jax.vjp(_a_pre, val, gate)
        ext = jnp.concatenate([a_h, a_t], axis=0)
        acc = _conv_taps(ext, cw_ref, HALO_A, K_A, TL)
        _, vj_post = jax.vjp(_a_post, acc, az_ref[...], cb_ref[...], lng_ref[...], lnb_ref[...], pw_ref[...],
                             pwb_ref[...])
        dacc, daz, dcb, dlng, dlnb, dpw, dpwb = vj_post(dy_ref[...])
        dext = _conv_taps_bwd(ext, cw_ref, dcw_ref, dacc, HALO_A, K_A, TL)
        da = _add_tail(dext[HALO_A:], carry_ref[...])
        carry_ref[...] = dext[:HALO_A]
        dval, dgate = vj_pre(da)
        dp_ref[:, 0:BR] = dval
        dp_ref[:, BR:2 * BR] = dgate
        dp_ref[:, 2 * BR:3 * BR] = daz
        dcb_ref[...] += dcb
        dlng_ref[...] += dlng
        dlnb_ref[...] += dlnb
        dpw_ref[...] += dpw
        dpwb_ref[...] += dpwb

    rev = lambda i: n - 1 - i
    vec = pl.BlockSpec((1, BR), lambda i: (0, 0))
    hmap = _halo_map(TL, HALO_A, 0)
    return pl.pallas_call(
        body, grid=(n,),
        in_specs=[pl.BlockSpec((TL, 2 * BR), lambda i: (rev(i), 0)), pl.BlockSpec((TL, BR), lambda i: (rev(i), 2)),
                  pl.BlockSpec((HALO_A, 2 * BR), lambda i: hmap(rev(i))),
                  pl.BlockSpec((TL, BR), lambda i: (rev(i), 0)),
                  pl.BlockSpec((HALO_A, BR), lambda i: (0, 0)), vec, vec, vec,
                  pl.BlockSpec((BR, BR), lambda i: (0, 0)), vec],
        out_specs=[pl.BlockSpec((TL, W_A), lambda i: (rev(i), 0)),
                   pl.BlockSpec((HALO_A, BR), lambda i: (0, 0)), vec, vec, vec,
                   pl.BlockSpec((BR, BR), lambda i: (0, 0)), vec],
        out_shape=[jax.ShapeDtypeStruct((L, W_A), F32), jax.ShapeDtypeStruct((HALO_A, BR), F32)]
        + [jax.ShapeDtypeStruct((1, BR), F32)] * 3
        + [jax.ShapeDtypeStruct((BR, BR), F32), jax.ShapeDtypeStruct((1, BR), F32)],
        scratch_shapes=[pltpu.VMEM((HALO_A, BR), F32)],
        name="a_bwd", compiler_params=_cp("arbitrary"))(proj, proj, proj, dmix, cw, cb, lng, lnb, pw, pwb)


def _c_pre(cg, xc):
    return cg * xc


def _c_post(acc, bg, cz):
    return bg * acc * _silu(cz)


def _c_fwd(proj, cw):
    L = proj.shape[0]

    def body(bg_ref, cx_ref, cz_ref, hcx_ref, cw_ref, o_ref):
        keep = (pl.program_id(0) > 0).astype(F32)
        p_h = _c_pre(hcx_ref[:, 0:BR], hcx_ref[:, BR:2 * BR]) * keep
        p_t = _c_pre(cx_ref[:, 0:BR], cx_ref[:, BR:2 * BR])
        ext = jnp.concatenate([p_h, p_t], axis=0)
        acc = _conv_taps(ext, cw_ref, HALO_S, K_C, TL)
        o_ref[...] = _c_post(acc, bg_ref[...], cz_ref[...])

    return pl.pallas_call(
        body, grid=(L // TL,),
        in_specs=[pl.BlockSpec((TL, BR), lambda i: (i, 5)), pl.BlockSpec((TL, 2 * BR), lambda i: (i, 3)),
                  pl.BlockSpec((TL, BR), lambda i: (i, 8)),
                  pl.BlockSpec((HALO_S, 2 * BR), _halo_map(TL, HALO_S, 3)),
                  pl.BlockSpec((HALO_S, BR), lambda i: (0, 0))],
        out_specs=pl.BlockSpec((TL, BR), lambda i: (i, 0)),
        out_shape=jax.ShapeDtypeStruct((L, BR), F32),
        name="c_fwd", compiler_params=_cp("parallel"))(proj, proj, proj, proj, cw)


def _c_bwd(proj, dmix, cw):
    L = proj.shape[0]
    n = L // TL

    def body(bg_ref, cx_ref, cz_ref, hcx_ref, dy_ref, cw_ref, dp_ref, dcw_ref, carry_ref):
        i = pl.program_id(0)

        @pl.when(i == 0)
        def _():
            carry_ref[...] = jnp.zeros_like(carry_ref)
            dcw_ref[...] = jnp.zeros_like(dcw_ref)

        keep = (i < n - 1).astype(F32)
        p_h = _c_pre(hcx_ref[:, 0:BR], hcx_ref[:, BR:2 * BR]) * keep
        p_t, vj_pre = jax.vjp(_c_pre, cx_ref[:, 0:BR], cx_ref[:, BR:2 * BR])
        ext = jnp.concatenate([p_h, p_t], axis=0)
        acc = _conv_taps(ext, cw_ref, HALO_S, K_C, TL)
        _, vj_post = jax.vjp(_c_post, acc, bg_ref[...], cz_ref[...])
        dacc, dbg, dcz = vj_post(dy_ref[...])
        dext = _conv_taps_bwd(ext, cw_ref, dcw_ref, dacc, HALO_S, K_C, TL)
        dp = _add_tail(dext[HALO_S:], carry_ref[...])
        carry_ref[...] = dext[:HALO_S]
        dcg, dxc = vj_pre(dp)
        dp_ref[:, 0:BR] = dbg
        dp_ref[:, BR:2 * BR] = dcg
        dp_ref[:, 2 * BR:3 * BR] = dxc
        dp_ref[:, 3 * BR:4 * BR] = dcz

    rev = lambda i: n - 1 - i
    hmap = _halo_map(TL, HALO_S, 3)
    return pl.pallas_call(
        body, grid=(n,),
        in_specs=[pl.BlockSpec((TL, BR), lambda i: (rev(i), 5)), pl.BlockSpec((TL, 2 * BR), lambda i: (rev(i), 3)),
                  pl.BlockSpec((TL, BR), lambda i: (rev(i), 8)),
                  pl.BlockSpec((HALO_S, 2 * BR), lambda i: hmap(rev(i))),
                  pl.BlockSpec((TL, BR), lambda i: (rev(i), 2)),
                  pl.BlockSpec((HALO_S, BR), lambda i: (0, 0))],
        out_specs=[pl.BlockSpec((TL, W_C), lambda i: (rev(i), 0)), pl.BlockSpec((HALO_S, BR), lambda i: (0, 0))],
        out_shape=[jax.ShapeDtypeStruct((L, W_C), F32), jax.ShapeDtypeStruct((HALO_S, BR), F32)],
        scratch_shapes=[pltpu.VMEM((HALO_S, BR), F32)],
        name="c_bwd", compiler_params=_cp("arbitrary"))(proj, proj, proj, proj, dmix, cw)


def _s5_prep_fn(lre, lim, ldt, bre, bim, cre, cim):
    grp = lax.broadcasted_iota(jnp.int32, (128, NS), 0)
    lane = lax.broadcasted_iota(jnp.int32, (128, NS), 1)
    expand = (grp == lane // S5_P).astype(F32)
    dt = jnp.exp(_dot_hi(jnp.broadcast_to(ldt, (8, 128)), expand)[0:1])
    lr = jnp.minimum(lre, -1e-4)
    mag = jnp.exp(lr * dt)
    ar = mag * jnp.cos(lim * dt)
    ai = mag * jnp.sin(lim * dt)
    den = lr * lr + lim * lim
    fr = ((ar - 1.0) * lr + ai * lim) / den
    fi = (ai * lr - (ar - 1.0) * lim) / den
    bbr = fr * bre - fi * bim
    bbi = fr * bim + fi * bre
    row = lax.broadcasted_iota(jnp.int32, (BR, NS), 0)
    col = lax.broadcasted_iota(jnp.int32, (BR, NS), 1)
    blk = (row // S5_H == col // S5_P).astype(F32)

    def embed(t):
        return jnp.concatenate([t] * S5_G, axis=0) * blk

    bemb = jnp.concatenate([embed(bbr), embed(bbi)], axis=1)
    cemb = jnp.concatenate([embed(cre), embed(-cim)], axis=1)
    return ar, ai, bemb, cemb


def _s5_prep(lre, lim, ldt, bre, bim, cre, cim):
    def body(*refs):
        outs = _s5_prep_fn(*[r[...] for r in refs[:7]])
        for r, o in zip(refs[7:], outs):
            r[...] = o

    return pl.pallas_call(
        body,
        out_shape=[jax.ShapeDtypeStruct((1, NS), F32)] * 2 + [jax.ShapeDtypeStruct((BR, 2 * NS), F32)] * 2,
        name="s5_prep", compiler_params=pltpu.CompilerParams(vmem_limit_bytes=VMEM_LIMIT),
    )(lre, lim, ldt, bre, bim, cre, cim)


def _s5_prep_bwd(lre, lim, ldt, bre, bim, cre, cim, dar, dai, dbemb, dcemb):
    def body(*refs):
        _, vj = jax.vjp(_s5_prep_fn, *[r[...] for r in refs[:7]])
        grads = vj(tuple(r[...] for r in refs[7:11]))
        for r, o in zip(refs[11:], grads):
            r[...] = o

    return pl.pallas_call(
        body,
        out_shape=[jax.ShapeDtypeStruct((1, NS), F32)] * 2 + [jax.ShapeDtypeStruct((1, 128), F32)]
        + [jax.ShapeDtypeStruct((S5_H, NS), F32)] * 4,
        name="s5_prep_bwd", compiler_params=pltpu.CompilerParams(vmem_limit_bytes=VMEM_LIMIT),
    )(lre, lim, ldt, bre, bim, cre, cim, dar, dai, dbemb, dcemb)


def _s5_scan(xr, xi, ar, ai, reverse):
    n = xr.shape[0]
    row = lax.broadcasted_iota(jnp.int32, (n, 1), 0)
    pr, pi = ar, ai
    d = 1
    while d < n:
        if d % 8:
            if reverse:
                m = row < n - d
                sr = jnp.where(m, _roll(xr, n - d), 0.0)
                si = jnp.where(m, _roll(xi, n - d), 0.0)
            else:
                m = row >= d
                sr = jnp.where(m, _roll(xr, d), 0.0)
                si = jnp.where(m, _roll(xi, d), 0.0)
            xr, xi = xr + pr * sr - pi * si, xi + pr * si + pi * sr
        elif reverse:
            sr, si = xr[d:], xi[d:]
            xr, xi = (jnp.concatenate([xr[:n - d] + pr * sr - pi * si, xr[n - d:]], axis=0),
                      jnp.concatenate([xi[:n - d] + pr * si + pi * sr, xi[n - d:]], axis=0))
        else:
            sr, si = xr[:n - d], xi[:n - d]
            xr, xi = (jnp.concatenate([xr[:d], xr[d:] + pr * sr - pi * si], axis=0),
                      jnp.concatenate([xi[:d], xi[d:] + pr * si + pi * sr], axis=0))
        pr, pi = pr * pr - pi * pi, 2.0 * pr * pi
        d *= 2
    return xr, xi


def _s5_states(u, bemb_b, ar, ai, sin_r, sin_i):
    bu = jnp.dot(u.astype(BF), bemb_b, preferred_element_type=F32)
    first = lax.broadcasted_iota(jnp.int32, (u.shape[0], 1), 0) == 0
    xr = bu[:, :NS] + jnp.where(first, ar * sin_r - ai * sin_i, 0.0)
    xi = bu[:, NS:] + jnp.where(first, ar * sin_i + ai * sin_r, 0.0)
    return _s5_scan(xr, xi, ar, ai, False)


def _b_post(yssm, u, bz, dsk, gw, gb):
    z = jax.nn.gelu(yssm + dsk * u)
    return z * _sigmoid(_mm(z, gw) + gb) * _silu(bz)


def _b_fwd(proj, ar, ai, bemb, cemb, dsk, gw, gb):
    L = proj.shape[0]
    n = L // TL

    def body(u_ref, bz_ref, ar_ref, ai_ref, be_ref, ce_ref, dsk_ref, gw_ref, gb_ref, o_ref, sin_ref, carry_ref):
        @pl.when(pl.program_id(0) == 0)
        def _():
            carry_ref[...] = jnp.zeros_like(carry_ref)

        sin = carry_ref[...]
        sin_ref[0] = sin
        u = u_ref[...]
        sr, si = _s5_states(u, be_ref[...].astype(BF), ar_ref[...], ai_ref[...], sin[:, :NS], sin[:, NS:])
        carry_ref[:, :NS] = sr[TL - 1:TL]
        carry_ref[:, NS:] = si[TL - 1:TL]
        s = jnp.concatenate([sr, si], axis=1).astype(BF)
        yssm = lax.dot_general(s, ce_ref[...].astype(BF), (((1,), (1,)), ((), ())), preferred_element_type=F32)
        o_ref[...] = _b_post(yssm, u, bz_ref[...], dsk_ref[...], gw_ref[...], gb_ref[...])

    vec = pl.BlockSpec((1, BR), lambda i: (0, 0))
    svec = pl.BlockSpec((1, NS), lambda i: (0, 0))
    emb = pl.BlockSpec((BR, 2 * NS), lambda i: (0, 0))
    return pl.pallas_call(
        body, grid=(n,),
        in_specs=[pl.BlockSpec((TL, BR), lambda i: (i, 3)), pl.BlockSpec((TL, BR), lambda i: (i, 4)),
                  svec, svec, emb, emb, vec, pl.BlockSpec((BR, BR), lambda i: (0, 0)), vec],
        out_specs=[pl.BlockSpec((TL, BR), lambda i: (i, 0)), pl.BlockSpec((1, 1, 2 * NS), lambda i: (i, 0, 0))],
        out_shape=[jax.ShapeDtypeStruct((L, BR), F32), jax.ShapeDtypeStruct((n, 1, 2 * NS), F32)],
        scratch_shapes=[pltpu.VMEM((1, 2 * NS), F32)],
        name="b_fwd", compiler_params=_cp("arbitrary"))(proj, proj, ar, ai, bemb, cemb, dsk, gw, gb)


def _b_bwd(proj, dmix, sin_all, ar, ai, bemb, cemb, dsk, gw, gb, exchange=None):
    L = proj.shape[0]
    n = L // TL

    def body(u_ref, bz_ref, dy_ref, sin_ref, ar_ref, ai_ref, be_ref, ce_ref, dsk_ref, gw_ref, gb_ref,
             dp_ref, dar_ref, dai_ref, dbe_ref, dce_ref, ddsk_ref, dgw_ref, dgb_ref, carry_ref):
        i = pl.program_id(0)

        @pl.when(i == 0)
        def _():
            carry_ref[...] = jnp.zeros_like(carry_ref)
            for r in (dar_ref, dai_ref, dbe_ref, dce_ref, ddsk_ref, dgw_ref, dgb_ref):
                r[...] = jnp.zeros_like(r)

        u = u_ref[...]
        ar, ai = ar_ref[...], ai_ref[...]
        be_b, ce_b = be_ref[...].astype(BF), ce_ref[...].astype(BF)
        sin = sin_ref[0]
        sr, si = _s5_states(u, be_b, ar, ai, sin[:, :NS], sin[:, NS:])
        s_b = jnp.concatenate([sr, si], axis=1).astype(BF)
        yssm = lax.dot_general(s_b, ce_b, (((1,), (1,)), ((), ())), preferred_element_type=F32)
        _, vj = jax.vjp(_b_post, yssm, u, bz_ref[...], dsk_ref[...], gw_ref[...], gb_ref[...])
        dyssm, du, dbz, ddsk, dgw, dgb = vj(dy_ref[...])
        dy_b = dyssm.astype(BF)
        dce_ref[...] += lax.dot_general(dy_b, s_b, (((0,), (0,)), ((), ())), preferred_element_type=F32)
        gs = jnp.dot(dy_b, ce_b, preferred_element_type=F32)
        last = lax.broadcasted_iota(jnp.int32, (TL, 1), 0) == TL - 1
        cr, ci = carry_ref[:, :NS], carry_ref[:, NS:]
        gr = gs[:, :NS] + jnp.where(last, ar * cr + ai * ci, 0.0)
        gi = gs[:, NS:] + jnp.where(last, ar * ci - ai * cr, 0.0)
        dsr, dsi = _s5_scan(gr, gi, ar, -ai, True)
        carry_ref[:, :NS] = dsr[0:1]
        carry_ref[:, NS:] = dsi[0:1]
        first = lax.broadcasted_iota(jnp.int32, (TL, 1), 0) == 0
        pr = jnp.where(first, sin[:, :NS], _roll(sr, 1))
        pi = jnp.where(first, sin[:, NS:], _roll(si, 1))
        dar_ref[...] += jnp.sum(dsr * pr + dsi * pi, axis=0, keepdims=True)
        dai_ref[...] += jnp.sum(dsi * pr - dsr * pi, axis=0, keepdims=True)
        ds_b = jnp.concatenate([dsr, dsi], axis=1).astype(BF)
        dbe_ref[...] += lax.dot_general(u.astype(BF), ds_b, (((0,), (0,)), ((), ())), preferred_element_type=F32)
        du = du + lax.dot_general(ds_b, be_b, (((1,), (1,)), ((), ())), preferred_element_type=F32)
        dp_ref[:, 0:BR] = du
        dp_ref[:, BR:2 * BR] = dbz
        ddsk_ref[...] += ddsk
        dgw_ref[...] += dgw
        dgb_ref[...] += dgb

    rev = lambda i: n - 1 - i
    vec = pl.BlockSpec((1, BR), lambda i: (0, 0))
    svec = pl.BlockSpec((1, NS), lambda i: (0, 0))
    emb = pl.BlockSpec((BR, 2 * NS), lambda i: (0, 0))
    mat = pl.BlockSpec((BR, BR), lambda i: (0, 0))
    return _sweep_with_exchange(
        body, n,
        in_specs=[pl.BlockSpec((TL, BR), lambda i: (rev(i), 3)), pl.BlockSpec((TL, BR), lambda i: (rev(i), 4)),
                  pl.BlockSpec((TL, BR), lambda i: (rev(i), 1)),
                  pl.BlockSpec((1, 1, 2 * NS), lambda i: (rev(i), 0, 0)),
                  svec, svec, emb, emb, vec, mat, vec],
        out_specs=[pl.BlockSpec((TL, W_B), lambda i: (rev(i), 0)), svec, svec, emb, emb, vec, mat, vec],
        out_shape=[jax.ShapeDtypeStruct((L, W_B), F32)] + [jax.ShapeDtypeStruct((1, NS), F32)] * 2
        + [jax.ShapeDtypeStruct((BR, 2 * NS), F32)] * 2
        + [jax.ShapeDtypeStruct((1, BR), F32), jax.ShapeDtypeStruct((BR, BR), F32), jax.ShapeDtypeStruct((1, BR), F32)],
        scratch_shapes=[pltpu.VMEM((1, 2 * NS), F32)],
        args=(proj, proj, dmix, sin_all, ar, ai, bemb, cemb, dsk, gw, gb), exchange=exchange, name="b_bwd")


def _d_post(cq, ab, dz, s0, s1, s2, s3, p1, p2, ng, known=None):
    c = cq.shape[0]
    qkv = _silu(cq)
    gall = -jnp.exp(p1) * jax.nn.softplus(ab + p2)
    ball = _sigmoid(ab)
    row = lax.broadcasted_iota(jnp.int32, (c, c), 0)
    col = lax.broadcasted_iota(jnp.int32, (c, c), 1)
    causal = row >= col
    strict = row > col
    gc_all = _dot_hi(causal.astype(F32), gall)
    gc_t = gc_all.T
    heads = range(DN_H)
    st = (s0, s1, s2, s3)
    q = [qkv[:, h * DN_D:(h + 1) * DN_D] for h in heads]
    k = [qkv[:, BR + h * DN_D:BR + (h + 1) * DN_D] for h in heads]
    v = [qkv[:, 2 * BR + h * DN_D:2 * BR + (h + 1) * DN_D] for h in heads]
    q = [t * lax.rsqrt(jnp.sum(t * t, axis=-1, keepdims=True) + EPS) * (DN_D ** -0.5) for t in q]
    k = [t * lax.rsqrt(jnp.sum(t * t, axis=-1, keepdims=True) + EPS) for t in k]
    gcol = [gc_all[:, h:h + 1] for h in heads]
    beta = [ball[:, DN_H + h:DN_H + h + 1] for h in heads]
    decay = [jnp.where(causal, jnp.exp(jnp.where(causal, gcol[h] - gc_t[h:h + 1, :], 0.0)), 0.0) for h in heads]
    kb = [k[h] * beta[h] for h in heads]
    lm = [jnp.where(strict, _mm_nt(kb[h], k[h]) * decay[h], 0.0) for h in heads]
    ainv = _unit_lower_inv(tuple(lm)) if known is None else _known_inverse(tuple(lm), known)
    egc = [jnp.exp(g) for g in gcol]
    uw = [_mm3(ainv[h], jnp.concatenate([v[h] * beta[h], kb[h] * egc[h]], axis=1)) for h in heads]
    attn = [_mm_nt(q[h], k[h]) * decay[h] for h in heads]
    glast = [g[c - 1:c, :] for g in gcol]
    kd = [k[h] * jnp.exp(glast[h] - gcol[h]) for h in heads]
    vnew = [uw[h][:, :DN_D] - _mm(uw[h][:, DN_D:], st[h]) for h in heads]
    o = [_mm(q[h] * egc[h], st[h]) + _mm(attn[h], vnew[h]) for h in heads]
    news = [st[h] * jnp.exp(glast[h]) + _mm_tn(kd[h], vnew[h]) for h in heads]
    outs = [t * lax.rsqrt(jnp.mean(t * t, axis=-1, keepdims=True) + EPS) * ng for t in o]
    yd = jnp.concatenate(outs, axis=1) * _silu(dz)
    return (yd, *news), ainv


def _d_fwd(proj, cw, p1, p2, ng, exchange=None):
    L = proj.shape[0]
    DN_G, DN_T = DN_G_FWD, DN_G_FWD * DN_C
    n = L // DN_T

    def body(qkv_ref, ab_ref, dz_ref, hq_ref, cw_ref, p1_ref, p2_ref, ng_ref, o_ref, sall_ref, inv_ref, s_ref):
        i = pl.program_id(0)

        @pl.when(i == 0)
        def _():
            s_ref[...] = jnp.zeros_like(s_ref)

        keep = (i > 0).astype(F32)
        ext = jnp.concatenate([hq_ref[...] * keep, qkv_ref[...]], axis=0)
        cq = _conv_taps(ext, cw_ref, HALO_S, K_DN, DN_T)
        st = [s_ref[h] for h in range(DN_H)]
        for g in range(DN_G):
            rows = slice(g * DN_C, (g + 1) * DN_C)
            for h in range(DN_H):
                sall_ref[g, h] = st[h]
            out, ainv = _d_post(cq[rows], ab_ref[rows, :], dz_ref[rows, :], *st, p1_ref[...], p2_ref[...],
                                ng_ref[...])
            o_ref[rows, :] = out[0]
            st = list(out[1:])
            for h in range(DN_H):
                inv_ref[g, h] = ainv[h]
        for h in range(DN_H):
            s_ref[h] = st[h]

    per_chunk = pl.BlockSpec((DN_G, DN_H, DN_D, DN_D), lambda i: (i, 0, 0, 0))
    return _sweep_with_exchange(
        body, n,
        in_specs=[pl.BlockSpec((DN_T, 3 * BR), lambda i: (i, 3)), pl.BlockSpec((DN_T, 128), lambda i: (i, 26)),
                  pl.BlockSpec((DN_T, BR), lambda i: (i, 12)),
                  pl.BlockSpec((HALO_S, 3 * BR), _halo_map(DN_T, HALO_S, 3)),
                  pl.BlockSpec((HALO_S, 3 * BR), lambda i: (0, 0)),
                  pl.BlockSpec((1, 128), lambda i: (0, 0)), pl.BlockSpec((1, 128), lambda i: (0, 0)),
                  pl.BlockSpec((1, DN_D), lambda i: (0, 0))],
        out_specs=[pl.BlockSpec((DN_T, BR), lambda i: (i, 0)), per_chunk, per_chunk],
        out_shape=[jax.ShapeDtypeStruct((L, BR), F32)]
        + [jax.ShapeDtypeStruct((L // DN_C, DN_H, DN_D, DN_D), F32)] * 2,
        scratch_shapes=[pltpu.VMEM((DN_H, DN_D, DN_D), F32)],
        args=(proj, proj, proj, proj, cw, p1, p2, ng), exchange=exchange, name="d_fwd")


def _sweep_with_exchange(body, steps, in_specs, out_specs, out_shape, scratch_shapes, args, exchange, name):
    if exchange is None:
        res = pl.pallas_call(body, grid=(steps,), in_specs=in_specs, out_specs=out_specs, out_shape=out_shape,
                             scratch_shapes=scratch_shapes, name=name, compiler_params=_cp("arbitrary"))(*args)
        return res, None
    xs, axes, mode = exchange
    ex = _Exchange(xs, axes, mode)
    ni, no, ns, na = len(in_specs), len(out_specs), len(scratch_shapes), ex.na

    def carried(*refs):
        ins, xin = refs[:ni], refs[ni:ni + na]
        outs, xout = refs[ni + na:ni + na + no], refs[ni + na + no:ni + 2 * na + no]
        scr, sems = refs[ni + 2 * na + no:ni + 2 * na + no + ns], refs[ni + 2 * na + no + ns:]

        @pl.when(pl.program_id(0) == 0)
        def _():
            ex.start(xin, xout, sems)

        body(*ins, *outs, *scr)

        @pl.when(pl.program_id(0) == steps - 1)
        def _():
            ex.wait(xin, xout, sems)

    res = pl.pallas_call(carried, grid=(steps,), in_specs=list(in_specs) + ex.in_specs,
                         out_specs=list(out_specs) + ex.out_specs, out_shape=list(out_shape) + ex.out_shape,
                         scratch_shapes=list(scratch_shapes) + ex.scratch_shapes, name=name + "_x",
                         compiler_params=_cp("arbitrary"))(*args, *xs)
    return res[:no], res[no:]


def _d_bwd(proj, dmix, sall, inv, cw, p1, p2, ng, exchange=None):
    L = proj.shape[0]
    DN_G, DN_T = DN_G_BWD, DN_G_BWD * DN_C
    n = L // DN_T

    def body(qkv_ref, ab_ref, dz_ref, hq_ref, dy_ref, sall_ref, inv_ref, cw_ref, p1_ref, p2_ref, ng_ref,
             dp_ref, dcw_ref, dp1_ref, dp2_ref, dng_ref, ds_ref, carry_ref):
        i = pl.program_id(0)

        @pl.when(i == 0)
        def _():
            ds_ref[...] = jnp.zeros_like(ds_ref)
            carry_ref[...] = jnp.zeros_like(carry_ref)
            for r in (dcw_ref, dp1_ref, dp2_ref, dng_ref):
                r[...] = jnp.zeros_like(r)

        keep = (i < n - 1).astype(F32)
        ext = jnp.concatenate([hq_ref[...] * keep, qkv_ref[...]], axis=0)
        cq = _conv_taps(ext, cw_ref, HALO_S, K_DN, DN_T)
        ds = [ds_ref[h] for h in range(DN_H)]
        dcq = [None] * DN_G
        for g in reversed(range(DN_G)):
            rows = slice(g * DN_C, (g + 1) * DN_C)
            st = [sall_ref[g, h] for h in range(DN_H)]
            known = tuple(inv_ref[g, h] for h in range(DN_H))
            _, vj = jax.vjp(lambda *a: _d_post(*a, known=known)[0], cq[rows], ab_ref[rows, :], dz_ref[rows, :], *st,
                            p1_ref[...], p2_ref[...], ng_ref[...])
            grads = vj((dy_ref[rows, :], *ds))
            dcq[g] = grads[0]
            dp_ref[rows, 3 * BR:4 * BR] = grads[2]
            dp_ref[rows, 4 * BR:4 * BR + 128] = grads[1]
            ds = list(grads[3:3 + DN_H])
            dp1_ref[...] += grads[7]
            dp2_ref[...] += grads[8]
            dng_ref[...] += grads[9]
        for h in range(DN_H):
            ds_ref[h] = ds[h]
        dext = _conv_taps_bwd(ext, cw_ref, dcw_ref, jnp.concatenate(dcq, axis=0), HALO_S, K_DN, DN_T)
        dp_ref[:, 0:3 * BR] = _add_tail(dext[HALO_S:], carry_ref[...])
        carry_ref[...] = dext[:HALO_S]

    rev = lambda i: n - 1 - i
    hmap = _halo_map(DN_T, HALO_S, 3)
    v128 = pl.BlockSpec((1, 128), lambda i: (0, 0))
    per_chunk = pl.BlockSpec((DN_G, DN_H, DN_D, DN_D), lambda i: (rev(i), 0, 0, 0))
    return _sweep_with_exchange(
        body, n,
        in_specs=[pl.BlockSpec((DN_T, 3 * BR), lambda i: (rev(i), 3)), pl.BlockSpec((DN_T, 128), lambda i: (rev(i), 26)),
                  pl.BlockSpec((DN_T, BR), lambda i: (rev(i), 12)),
                  pl.BlockSpec((HALO_S, 3 * BR), lambda i: hmap(rev(i))),
                  pl.BlockSpec((DN_T, BR), lambda i: (rev(i), 3)), per_chunk, per_chunk,
                  pl.BlockSpec((HALO_S, 3 * BR), lambda i: (0, 0)), v128, v128,
                  pl.BlockSpec((1, DN_D), lambda i: (0, 0))],
        out_specs=[pl.BlockSpec((DN_T, W_D), lambda i: (rev(i), 0)),
                   pl.BlockSpec((HALO_S, 3 * BR), lambda i: (0, 0)), v128, v128,
                   pl.BlockSpec((1, DN_D), lambda i: (0, 0))],
        out_shape=[jax.ShapeDtypeStruct((L, W_D), F32), jax.ShapeDtypeStruct((HALO_S, 3 * BR), F32),
                   jax.ShapeDtypeStruct((1, 128), F32), jax.ShapeDtypeStruct((1, 128), F32),
                   jax.ShapeDtypeStruct((1, DN_D), F32)],
        scratch_shapes=[pltpu.VMEM((DN_H, DN_D, DN_D), F32), pltpu.VMEM((HALO_S, 3 * BR), F32)],
        args=(proj, proj, proj, proj, dmix, sall, inv, cw, p1, p2, ng), exchange=exchange, name="d_bwd")


def _pick_rows(rows, cap):
    best = 8
    for t in range(8, cap + 1, 8):
        if rows % t == 0:
            best = t
    return best


def _adamw(w, g, m, v, name):
    rows, wd = w.shape
    tr = _pick_rows(rows, 512)
    c1 = 1.0 - ADAM_B1 ** ADAM_STEP
    c2 = 1.0 - ADAM_B2 ** ADAM_STEP

    def body(w_ref, g_ref, m_ref, v_ref, d_ref, mo_ref, vo_ref):
        gv = g_ref[...]
        mn = ADAM_B1 * m_ref[...] + (1.0 - ADAM_B1) * gv
        vn = ADAM_B2 * v_ref[...] + (1.0 - ADAM_B2) * (gv * gv)
        d_ref[...] = -ADAM_LR * ((mn / c1) / (jnp.sqrt(vn / c2) + ADAM_EPS) + ADAM_WD * w_ref[...])
        mo_ref[...] = mn
        vo_ref[...] = vn

    spec = pl.BlockSpec((tr, wd), lambda i: (i, 0))
    return pl.pallas_call(
        body, grid=(rows // tr,), in_specs=[spec] * 4, out_specs=[spec] * 3,
        out_shape=[jax.ShapeDtypeStruct((rows, wd), F32)] * 3,
        name=name, compiler_params=_cp("parallel"))(w, g, m, v)


def _sum_slots(r, name):
    n, rows, wd = r.shape
    tr = _pick_rows(rows, 384)

    def body(r_ref, o_ref):
        acc = r_ref[0].astype(F32)
        for j in range(1, n):
            acc = acc + r_ref[j].astype(F32)
        o_ref[...] = acc

    return pl.pallas_call(
        body, grid=(rows // tr,),
        in_specs=[pl.BlockSpec((n, tr, wd), lambda i: (0, i, 0))],
        out_specs=pl.BlockSpec((tr, wd), lambda i: (i, 0)),
        out_shape=jax.ShapeDtypeStruct((rows, wd), F32),
        name=name, compiler_params=_cp("parallel"))(r)


AXES = ("x", "y", "c")


def _group_peer(axes, k):
    pos = {a: lax.axis_index(a) for a in AXES}
    idx = 0
    for a in axes:
        idx = idx * 2 + pos[a]
    peer = dict(pos)
    for b, a in enumerate(reversed(axes)):
        if (k >> b) & 1:
            peer[a] = 1 - pos[a]
    return idx, tuple(peer[a] for a in AXES)


MAX_CHUNKS = 4


class _Exchange:
    def __init__(self, xs, axes, mode):
        self.axes, self.mode, self.na, self.n = axes, mode, len(xs), 2 ** len(axes)
        n = self.n
        self.out_shape, self.pieces = [], []
        for x in xs:
            if mode == "gather":
                shape, lead = (n,) + x.shape, x.shape[0]
            elif mode == "scatter":
                shape, lead = x.shape, x.shape[1]
            else:
                shape, lead = (x.shape[0], n * x.shape[1], x.shape[2]), x.shape[0]
            self.out_shape.append(jax.ShapeDtypeStruct(shape, x.dtype))
            big = x.size * x.dtype.itemsize >= (1 << 20)
            if mode == "rows":
                self.pieces.append(lead if lead <= MAX_CHUNKS else 1)
            else:
                self.pieces.append(MAX_CHUNKS if big and lead % (16 * MAX_CHUNKS) == 0 else 1)
        self.in_specs = [pl.BlockSpec(memory_space=pl.ANY)] * self.na
        self.out_specs = [pl.BlockSpec(memory_space=pl.ANY)] * self.na
        self.scratch_shapes = [pltpu.SemaphoreType.DMA((self.na, MAX_CHUNKS, n)),
                               pltpu.SemaphoreType.DMA((self.na, MAX_CHUNKS, n)),
                               pltpu.SemaphoreType.DMA((self.na, MAX_CHUNKS))]

    def _copies(self, x_refs, o_refs, send_sems, recv_sems, local_sems):
        me, _ = _group_peer(self.axes, 0)
        local, remote = [], []
        for a, (x, o) in enumerate(zip(x_refs, o_refs)):
            for c in range(self.pieces[a]):
                if self.mode == "rows":
                    r = x.shape[1]
                    b = slice(None) if self.pieces[a] == 1 else pl.ds(c, 1)
                    src = lambda k, x=x, b=b: x.at[b]
                    dst = o.at[b, pl.ds(me * r, r)]
                else:
                    lead = x.shape[1] if self.mode == "scatter" else x.shape[0]
                    rs = pl.ds(c * (lead // self.pieces[a]), lead // self.pieces[a])
                    if self.mode == "scatter":
                        src = lambda k, x=x, rs=rs: x.at[me ^ k, rs]
                    else:
                        src = lambda k, x=x, rs=rs: x.at[rs]
                    dst = o.at[me, rs]
                local.append(pltpu.make_async_copy(src(0), dst, local_sems.at[a, c]))
                for k in range(1, self.n):
                    remote.append(pltpu.make_async_remote_copy(
                        src_ref=src(k), dst_ref=dst, send_sem=send_sems.at[a, c, k], recv_sem=recv_sems.at[a, c, k],
                        device_id=_group_peer(self.axes, k)[1], device_id_type=MESH))
        return local, remote

    def start(self, x_refs, o_refs, sems):
        local, remote = self._copies(x_refs, o_refs, *sems)
        for cp in local + remote:
            cp.start()

    def wait(self, x_refs, o_refs, sems):
        local, remote = self._copies(x_refs, o_refs, *sems)
        for cp in remote:
            cp.wait_send()
        for cp in remote:
            cp.wait_recv()
        for cp in local:
            cp.wait()


def _exchange(xs, axes, mode, name):
    ex = _Exchange(xs, axes, mode)
    na = ex.na

    def body(*refs):
        ex.start(refs[:na], refs[na:2 * na], refs[2 * na:])
        ex.wait(refs[:na], refs[na:2 * na], refs[2 * na:])

    return pl.pallas_call(body, out_shape=ex.out_shape, in_specs=ex.in_specs, out_specs=ex.out_specs,
                          scratch_shapes=ex.scratch_shapes, name=name)(*xs)


SHARDED_SMALL = (("a_conv_w", 2), ("a_pw_w", 1), ("s5_glu_w", 1), ("c_conv_w", 2), ("d_conv_w", 2))
REPLICATED = ("norm_g", "a_conv_b", "a_ln_g", "a_ln_b", "a_pw_b", "s5_lambda_re", "s5_lambda_im", "s5_b_re",
              "s5_b_im", "s5_c_re", "s5_c_im", "s5_d", "s5_log_dt", "s5_glu_b", "d_a_log", "d_dt_bias",
              "d_norm_g", "final_g")
WEIGHTS = ("norm_g", "w_in", "a_conv_w", "a_conv_b", "a_ln_g", "a_ln_b", "a_pw_w", "a_pw_b", "s5_lambda_re",
           "s5_lambda_im", "s5_b_re", "s5_b_im", "s5_c_re", "s5_c_im", "s5_d", "s5_log_dt", "s5_glu_w",
           "s5_glu_b", "c_conv_w", "d_conv_w", "d_a_log", "d_dt_bias", "d_norm_g", "w_out", "final_g")
LANES = 1024


def _size(shape):
    size = 1
    for d in shape:
        size *= d
    return size


def _slab_rows(shape):
    return -(-_size(shape) // (8 * LANES)) * 8


def _pack(arrs, rows):
    parts = []
    for a in arrs:
        r = _slab_rows(a.shape)
        parts.append(jnp.pad(a.reshape(-1), (0, r * LANES - a.size)).reshape(r, LANES))
    used = sum(p.shape[0] for p in parts)
    if rows > used:
        parts.append(jnp.zeros((rows - used, LANES), parts[0].dtype))
    return jnp.concatenate(parts, axis=0)


def _unpack(slab, shapes):
    out, off = [], 0
    for s in shapes:
        r = _slab_rows(s)
        out.append(slab[off:off + r].reshape(-1)[:_size(s)].reshape(s))
        off += r
    return out


def _rows_for(shapes, mult):
    rows = sum(_slab_rows(s) for s in shapes)
    return -(-rows // mult) * mult


def _row(v, width=None):
    v = v.reshape(1, -1)
    return v if width is None else jnp.pad(v, ((0, 0), (0, width - v.shape[1])))


def _pad_rows(w, rows):
    return jnp.pad(w, ((0, rows - w.shape[0]), (0, 0)))


def _permute_in(w):
    return jnp.concatenate([w[:, :3072], w[:, 3080:N_IN], w[:, 3072:3080],
                            jnp.zeros((w.shape[0], N_INP - N_IN), w.dtype)], axis=1)


def _layer_fwd(x, p, exchange=None):
    proj, h = _proj_fwd(x, p["norm_g"], p["wp"])
    ya = _a_fwd(proj, p["a_cw"], p["a_cb"], p["a_lng"], p["a_lnb"], p["a_pw"], p["a_pwb"])
    ar, ai, bemb, cemb = _s5_prep(*p["s5"])
    yb, sin_all = _b_fwd(proj, ar, ai, bemb, cemb, p["s5_d"], p["glu_w"], p["glu_b"])
    yc = _c_fwd(proj, p["c_cw"])
    (yd, sall, inv), got = _d_fwd(proj, p["d_cw"], p["d_p1"], p["d_p2"], p["d_ng"], exchange)
    xo = _out_fwd(x, ya, yb, yc, yd, p["wo"])
    return xo, dict(x=x, proj=proj, h=h, ys=(ya, yb, yc, yd), sin_all=sin_all, sall=sall, inv=inv,
                    s5=(ar, ai, bemb, cemb)), got


def _layer_bwd(dxo, p, r, exchanges):
    proj = r["proj"]
    ar, ai, bemb, cemb = r["s5"]
    dmix = _out_bwd_x(dxo, p["wo"])
    dwo = _dwout(*r["ys"], dxo)
    dpa, dcw_a, dcb, dlng, dlnb, dpw, dpwb = _a_bwd(proj, dmix, p["a_cw"], p["a_cb"], p["a_lng"], p["a_lnb"],
                                                     p["a_pw"], p["a_pwb"])
    got = {}
    (dpb, dar, dai, dbe, dce, ddsk, dgw, dgb), got["b"] = _b_bwd(
        proj, dmix, r["sin_all"], ar, ai, bemb, cemb, p["s5_d"], p["glu_w"], p["glu_b"], exchanges.get("b"))
    dlre, dlim, dldt, dbre, dbim, dcre, dcim = _s5_prep_bwd(*p["s5"], dar, dai, dbe, dce)
    dpc, dcw_c = _c_bwd(proj, dmix, p["c_cw"])
    (dpd, dcw_d, dp1, dp2, dng), got["d"] = _d_bwd(proj, dmix, r["sall"], r["inv"], p["d_cw"], p["d_p1"],
                                                   p["d_p2"], p["d_ng"], exchanges.get("d"))
    ex_proj = exchanges.get("proj")
    if callable(ex_proj):
        ex_proj = ex_proj(got["d"])
    (dx, dg), got["proj"] = _proj_bwd_x(r["x"], p["norm_g"], dpa, dpb, dpc, dpd, p["wp"], dxo, ex_proj)
    dwa, dwb, dwc, dwd = _dwin(r["h"], dpa, dpb, dpc, dpd)
    dwin = jnp.concatenate([dwa, dwb, dwc, dwd[:, :3 * BR], dwd[:, 4 * BR:4 * BR + 2 * DN_H],
                            dwd[:, 3 * BR:4 * BR]], axis=1)

    def unrows(t, perm):
        return jnp.transpose(t.reshape(S5_H, S5_G, S5_P), perm)

    grads = dict(
        norm_g=dg.reshape(-1), w_in=dwin, a_conv_w=dcw_a[:K_A], a_conv_b=dcb.reshape(-1),
        a_ln_g=dlng.reshape(-1), a_ln_b=dlnb.reshape(-1), a_pw_w=dpw, a_pw_b=dpwb.reshape(-1),
        s5_lambda_re=dlre.reshape(S5_G, S5_P), s5_lambda_im=dlim.reshape(S5_G, S5_P),
        s5_b_re=unrows(dbre, (1, 2, 0)), s5_b_im=unrows(dbim, (1, 2, 0)),
        s5_c_re=unrows(dcre, (1, 0, 2)), s5_c_im=unrows(dcim, (1, 0, 2)),
        s5_d=ddsk.reshape(-1), s5_log_dt=dldt[0, :S5_G], s5_glu_w=dgw, s5_glu_b=dgb.reshape(-1),
        c_conv_w=dcw_c[:K_C], d_conv_w=dcw_d[:K_DN], d_a_log=dp1[0, :DN_H], d_dt_bias=dp2[0, :DN_H],
        d_norm_g=dng.reshape(-1), w_out=dwo)
    return dx, grads, got


def _layer_params(full, wp, wo, l):
    return dict(
        norm_g=_row(full["norm_g"][l]), wp=wp,
        a_cw=_pad_rows(full["a_conv_w"][l], HALO_A), a_cb=_row(full["a_conv_b"][l]),
        a_lng=_row(full["a_ln_g"][l]), a_lnb=_row(full["a_ln_b"][l]), a_pw=full["a_pw_w"][l],
        a_pwb=_row(full["a_pw_b"][l]),
        s5=(_row(full["s5_lambda_re"][l]), _row(full["s5_lambda_im"][l]), _row(full["s5_log_dt"][l], 128),
            jnp.transpose(full["s5_b_re"][l], (2, 0, 1)).reshape(S5_H, NS),
            jnp.transpose(full["s5_b_im"][l], (2, 0, 1)).reshape(S5_H, NS),
            jnp.transpose(full["s5_c_re"][l], (1, 0, 2)).reshape(S5_H, NS),
            jnp.transpose(full["s5_c_im"][l], (1, 0, 2)).reshape(S5_H, NS)),
        s5_d=_row(full["s5_d"][l]), glu_w=full["s5_glu_w"][l], glu_b=_row(full["s5_glu_b"][l]),
        c_cw=_pad_rows(full["c_conv_w"][l], HALO_S), d_cw=_pad_rows(full["d_conv_w"][l], HALO_S),
        d_p1=_row(full["d_a_log"][l], 128), d_p2=_row(full["d_dt_bias"][l], 128),
        d_ng=_row(full["d_norm_g"][l]), wo=wo)


def kernel(x, norm_g, w_in, a_conv_w, a_conv_b, a_ln_g, a_ln_b, a_pw_w, a_pw_b, s5_lambda_re, s5_lambda_im, s5_b_re, s5_b_im, s5_c_re, s5_c_im, s5_d, s5_log_dt, s5_glu_w, s5_glu_b, c_conv_w, d_conv_w, d_a_log, d_dt_bias, d_norm_g, w_out, final_g, loss_target, m_norm_g, m_w_in, m_a_conv_w, m_a_conv_b, m_a_ln_g, m_a_ln_b, m_a_pw_w, m_a_pw_b, m_s5_lambda_re, m_s5_lambda_im, m_s5_b_re, m_s5_b_im, m_s5_c_re, m_s5_c_im, m_s5_d, m_s5_log_dt, m_s5_glu_w, m_s5_glu_b, m_c_conv_w, m_d_conv_w, m_d_a_log, m_d_dt_bias, m_d_norm_g, m_w_out, m_final_g, v_norm_g, v_w_in, v_a_conv_w, v_a_conv_b, v_a_ln_g, v_a_ln_b, v_a_pw_w, v_a_pw_b, v_s5_lambda_re, v_s5_lambda_im, v_s5_b_re, v_s5_b_im, v_s5_c_re, v_s5_c_im, v_s5_d, v_s5_log_dt, v_s5_glu_w, v_s5_glu_b, v_c_conv_w, v_d_conv_w, v_d_a_log, v_d_dt_bias, v_d_norm_g, v_w_out, v_final_g):
    given = dict(locals())
    w = {n: given[n] for n in WEIGHTS}
    m = {n: given["m_" + n] for n in WEIGHTS}
    v = {n: given["v_" + n] for n in WEIGHTS}
    xs, tgt = x[0], loss_target[0]

    n_in, n_out = w["w_in"].shape[2], w["w_out"].shape[1]
    sm_names = [n for n, _ in SHARDED_SMALL]
    sm_shapes = [w[n].shape for n in sm_names]
    sm_rows = _rows_for(sm_shapes, 16)
    win_b, wout_b = w["w_in"].astype(BF), w["w_out"].astype(BF)
    g_in, g_out, g_sm = _exchange([win_b[0], wout_b[0], _pack([w[n] for n in sm_names], sm_rows)],
                                  ("x", "y"), "gather", "gather_first")
    full = dict(w)
    parts = [_unpack(g_sm[j], sm_shapes) for j in range(4)]
    for i, (n, ax) in enumerate(SHARDED_SMALL):
        full[n] = jnp.concatenate([parts[j][i] for j in range(4)], axis=ax)

    saved = []
    h = xs
    for l in range(DEPTH):
        p = _layer_params(full, _permute_in(jnp.concatenate([g_in[j] for j in range(4)], axis=1)),
                          jnp.concatenate([g_out[j] for j in range(4)], axis=0), l)
        nxt = ([win_b[l + 1], wout_b[l + 1]], ("x", "y"), "gather") if l + 1 < DEPTH else None
        h, r, got = _layer_fwd(h, p, nxt)
        saved.append((p, r))
        if got is not None:
            g_in, g_out = got
    loss_tile, dx, dfg = _loss_bwd(h, _row(full["final_g"]), tgt)

    def big_slots(g):
        s_in = jnp.stack([g["w_in"][:, j * n_in:(j + 1) * n_in].astype(BF) for j in range(4)])
        return [s_in.reshape(8, D_MODEL // 2, n_in), g["w_out"].astype(BF).reshape(8, n_out // 2, D_MODEL)]

    def halves(rv):
        return [_sum_slots(rv[0], "sum_w_in")[None], _sum_slots(rv[1], "sum_w_out")[None]]

    layer_grads, summed, pending, arrived = [None] * DEPTH, [None] * DEPTH, None, {}
    for l in reversed(range(DEPTH)):
        p, r = saved[l]
        ex = {}
        if pending is not None:
            ex["d"] = (pending, AXES, "scatter")
        if l + 2 in arrived:
            ex["b"] = (halves(arrived.pop(l + 2)), ("c",), "rows")
        if l == 0:
            ex["proj"] = lambda came: (halves(came), ("c",), "rows")
        dx, layer_grads[l], got = _layer_bwd(dx, p, r, ex)
        if got["b"] is not None:
            summed[l + 2] = got["b"]
        if got["proj"] is not None:
            summed[1] = got["proj"]
        elif got["d"] is not None:
            arrived[l + 1] = got["d"]
        pending = big_slots(layer_grads[l])
    grads = {n: jnp.stack([layer_grads[l][n] for l in range(DEPTH)]) for n in WEIGHTS
             if n not in ("final_g", "w_in", "w_out")}
    grads["final_g"] = dfg.reshape(-1)
    slots = []
    for j in range(4):
        sl = [lax.slice_in_dim(grads[n], j * w[n].shape[ax], (j + 1) * w[n].shape[ax], axis=ax)
              for n, ax in SHARDED_SMALL]
        slots.append(_pack(sl, sm_rows))
    rp_shapes = [w[n].shape for n in REPLICATED] + [(1,)]
    rp_rows = _rows_for(rp_shapes, 64)
    r_in0, r_out0, r_sm, r_rp = _exchange(
        pending + [jnp.stack(slots).reshape(8, sm_rows // 2, LANES),
                   _pack([grads[n] for n in REPLICATED] + [loss_tile[0, 0:1]], rp_rows).reshape(8, rp_rows // 8, LANES)],
        AXES, "scatter", "scatter_last")
    summed[0] = _exchange(halves((r_in0, r_out0)) + [_sum_slots(r_sm, "sum_small")[None]], ("c",), "rows",
                          "gather_halves")
    h_sm = summed[0][2]
    h_in = jnp.concatenate([summed[l][0] for l in range(DEPTH)], axis=0)
    h_out = jnp.concatenate([summed[l][1] for l in range(DEPTH)], axis=0)
    (g_rp,) = _exchange([_sum_slots(r_rp, "sum_replicated")], AXES, "gather", "gather_replicated")
    g_rp = g_rp.reshape(rp_rows, LANES)
    g_sm = h_sm.reshape(sm_rows, LANES)

    out = {}

    def put(name, shape, res):
        for key, t in zip(("delta", "new_m", "new_v"), res):
            out[key + "_" + name] = t.reshape(shape)

    for name, g2 in (("w_in", h_in.reshape(DEPTH * D_MODEL, n_in)), ("w_out", h_out.reshape(DEPTH * n_out, D_MODEL))):
        shape = w[name].shape
        out["grad_" + name] = g2.reshape(shape)
        put(name, shape, _adamw(w[name].reshape(g2.shape), g2, m[name].reshape(g2.shape), v[name].reshape(g2.shape),
                                "adamw_" + name))
    zero = jnp.zeros((1,), F32)
    res_sm = _adamw(_pack([w[n] for n in sm_names], sm_rows), g_sm, _pack([m[n] for n in sm_names], sm_rows),
                    _pack([v[n] for n in sm_names], sm_rows), "adamw_small")
    res_rp = _adamw(_pack([w[n] for n in REPLICATED] + [zero], rp_rows), g_rp,
                    _pack([m[n] for n in REPLICATED] + [zero], rp_rows),
                    _pack([v[n] for n in REPLICATED] + [zero], rp_rows), "adamw_replicated")
    for key, sm, rp in (("grad", g_sm, g_rp), ("delta", res_sm[0], res_rp[0]), ("new_m", res_sm[1], res_rp[1]),
                        ("new_v", res_sm[2], res_rp[2])):
        for n, t in zip(sm_names, _unpack(sm, sm_shapes)):
            out[key + "_" + n] = t
        for n, t in zip(REPLICATED, _unpack(rp, rp_shapes[:-1])):
            out[key + "_" + n] = t
    loss = _unpack(g_rp, rp_shapes)[-1].reshape(())
    return (loss, dx[None], *[out["grad_" + n] for n in WEIGHTS], *[out["delta_" + n] for n in WEIGHTS],
            *[out["new_m_" + n] for n in WEIGHTS], *[out["new_v_" + n] for n in WEIGHTS])
```

```python
import functools

import jax
import jax.numpy as jnp
from jax import lax
from jax.experimental import pallas as pl
from jax.experimental.pallas import tpu as pltpu

F32, BF = jnp.float32, jnp.bfloat16
HI = lax.Precision.HIGHEST
MESH = pl.DeviceIdType.MESH

D_MODEL = 1024
BR = 256
DEPTH = 4
N_IN = 3336
N_INP = 3456
COL_A, COL_B, COL_C, COL_D = 0, 768, 1280, 2304
W_A, W_B, W_C, W_D = 768, 512, 1024, 1152
S5_G, S5_H, S5_P = 16, 16, 64
NS = S5_G * S5_P
DN_H, DN_D, DN_C = 4, 64, 64
DN_G_FWD, DN_G_BWD = 2, 1
K_A, K_C, K_DN = 31, 3, 4
HALO_A, HALO_S = 32, 8
EPS = 1e-6
TL = 256
VMEM_LIMIT = 56 * 1024 * 1024

ADAM_LR, ADAM_B1, ADAM_B2, ADAM_EPS, ADAM_WD, ADAM_STEP = 0.001, 0.9, 0.999, 1e-08, 0.01, 10


def _cp(*sem):
    return pltpu.CompilerParams(dimension_semantics=sem, vmem_limit_bytes=VMEM_LIMIT)


def _sigmoid(x):
    return jax.nn.sigmoid(x)


def _silu(x):
    return x * jax.nn.sigmoid(x)


def _rmsnorm(x, g):
    return x * lax.rsqrt(jnp.mean(x * x, axis=-1, keepdims=True) + EPS) * g


@jax.custom_vjp
def _mm(a, w):
    return jnp.dot(a.astype(BF), w.astype(BF), preferred_element_type=F32)


def _mm_f(a, w):
    return _mm(a, w), (a, w)


def _mm_b(res, g):
    a, w = res
    gb = g.astype(BF)
    da = lax.dot_general(gb, w.astype(BF), (((1,), (1,)), ((), ())), preferred_element_type=F32)
    dw = lax.dot_general(a.astype(BF), gb, (((0,), (0,)), ((), ())), preferred_element_type=F32)
    return da, dw


_mm.defvjp(_mm_f, _mm_b)


@jax.custom_vjp
def _mm_nt(a, b):
    return lax.dot_general(a.astype(BF), b.astype(BF), (((1,), (1,)), ((), ())), preferred_element_type=F32)


def _mm_nt_f(a, b):
    return _mm_nt(a, b), (a, b)


def _mm_nt_b(res, g):
    a, b = res
    gb = g.astype(BF)
    da = jnp.dot(gb, b.astype(BF), preferred_element_type=F32)
    db = lax.dot_general(gb, a.astype(BF), (((0,), (0,)), ((), ())), preferred_element_type=F32)
    return da, db


_mm_nt.defvjp(_mm_nt_f, _mm_nt_b)


@jax.custom_vjp
def _mm_tn(a, b):
    return lax.dot_general(a.astype(BF), b.astype(BF), (((0,), (0,)), ((), ())), preferred_element_type=F32)


def _mm_tn_f(a, b):
    return _mm_tn(a, b), (a, b)


def _mm_tn_b(res, g):
    a, b = res
    gb = g.astype(BF)
    da = lax.dot_general(b.astype(BF), gb, (((1,), (1,)), ((), ())), preferred_element_type=F32)
    db = jnp.dot(a.astype(BF), gb, preferred_element_type=F32)
    return da, db


_mm_tn.defvjp(_mm_tn_f, _mm_tn_b)


def _dot_hi(a, b):
    return jnp.dot(a, b, precision=HI, preferred_element_type=F32)


def _split(a):
    hi = a.astype(BF)
    return hi, (a - hi.astype(F32)).astype(BF)


def _dot3(a, b, dims=(((1,), (0,)), ((), ()))):
    ah, al = _split(a)
    bh, bl = _split(b)
    d = functools.partial(lax.dot_general, dimension_numbers=dims, preferred_element_type=F32)
    return d(ah, bh) + d(ah, bl) + d(al, bh)


@jax.custom_vjp
def _mm3(a, b):
    return _dot3(a, b)


def _mm3_f(a, b):
    return _dot3(a, b), (a, b)


def _mm3_b(res, g):
    a, b = res
    return _dot3(g, b, (((1,), (1,)), ((), ()))), _dot3(a, g, (((0,), (0,)), ((), ())))


_mm3.defvjp(_mm3_f, _mm3_b)


@jax.custom_vjp
def _unit_lower_inv(lms):
    n = lms[0].shape[0]
    row = lax.broadcasted_iota(jnp.int32, lms[0].shape, 0)
    col = lax.broadcasted_iota(jnp.int32, lms[0].shape, 1)
    eye = (row == col).astype(F32)
    accs = [eye - lm for lm in lms]
    pws = list(lms)
    k = 2
    while k < n:
        pws = [_dot3(p, p) for p in pws]
        accs = [a + _dot3(a, p) for a, p in zip(accs, pws)]
        k *= 2
    return tuple(accs)


def _uli_f(lms):
    a = _unit_lower_inv(lms)
    return a, a


def _uli_b(a, g):
    ats = [x.T for x in a]
    tmp = [_dot3(at, gi) for at, gi in zip(ats, g)]
    return (tuple(-_dot3(t, at) for t, at in zip(tmp, ats)),)


_unit_lower_inv.defvjp(_uli_f, _uli_b)


@jax.custom_vjp
def _known_inverse(lms, inv):
    return inv


def _ki_f(lms, inv):
    return inv, inv


def _ki_b(a, g):
    return _uli_b(a, g)[0], tuple(jnp.zeros_like(x) for x in a)


_known_inverse.defvjp(_ki_f, _ki_b)


def _roll(x, s):
    n = x.shape[0]
    s = s % n
    return x if s == 0 else pltpu.roll(x, s, 0)


def _conv_taps(ext, w_ref, halo, k_taps, tl):
    acc = None
    for k in range(k_taps):
        term = _roll(ext, (k_taps - 1) - k)[halo:halo + tl] * w_ref[k:k + 1, :]
        acc = term if acc is None else acc + term
    return acc


def _conv_taps_bwd(ext, w_ref, dw_ref, dacc, halo, k_taps, tl):
    dpad = jnp.concatenate([dacc, jnp.zeros((halo, dacc.shape[1]), F32)], axis=0)
    dext = None
    for k in range(k_taps):
        r = _roll(ext, (k_taps - 1) - k)[halo:halo + tl]
        dw_ref[k:k + 1, :] += jnp.sum(r * dacc, axis=0, keepdims=True)
        term = _roll(dpad, halo - (k_taps - 1) + k) * w_ref[k:k + 1, :]
        dext = term if dext is None else dext + term
    return dext


def _add_tail(x, tail):
    tl, h = x.shape[0], tail.shape[0]
    return x + jnp.concatenate([jnp.zeros((tl - h, x.shape[1]), F32), tail], axis=0)


def _proj_fwd(x, g, wp):
    L = x.shape[0]

    def body(x_ref, g_ref, w_ref, p_ref, h_ref):
        hb = _rmsnorm(x_ref[...], g_ref[...]).astype(BF)
        h_ref[...] = hb
        p_ref[...] = jnp.dot(hb, w_ref[...], preferred_element_type=F32)

    return pl.pallas_call(
        body, grid=(L // TL,),
        in_specs=[pl.BlockSpec((TL, D_MODEL), lambda i: (i, 0)),
                  pl.BlockSpec((1, D_MODEL), lambda i: (0, 0)),
                  pl.BlockSpec((D_MODEL, N_INP), lambda i: (0, 0))],
        out_specs=[pl.BlockSpec((TL, N_INP), lambda i: (i, 0)),
                   pl.BlockSpec((TL, D_MODEL), lambda i: (i, 0))],
        out_shape=[jax.ShapeDtypeStruct((L, N_INP), F32), jax.ShapeDtypeStruct((L, D_MODEL), BF)],
        name="proj_fwd", compiler_params=_cp("parallel"))(x, g, wp)


def _proj_bwd_x(x, g, dpa, dpb, dpc, dpd, wp, dxo, exchange=None):
    L = x.shape[0]

    def body(x_ref, g_ref, a_ref, b_ref, c_ref, d_ref, w_ref, dxo_ref, dx_ref, dg_ref):
        dh = None
        for ref, c0, wd in ((a_ref, COL_A, W_A), (b_ref, COL_B, W_B), (c_ref, COL_C, W_C), (d_ref, COL_D, W_D)):
            t = lax.dot_general(ref[...].astype(BF), w_ref[:, c0:c0 + wd], (((1,), (1,)), ((), ())),
                                preferred_element_type=F32)
            dh = t if dh is None else dh + t
        _, vj = jax.vjp(_rmsnorm, x_ref[...], g_ref[...])
        dx, dg = vj(dh)
        dx_ref[...] = dxo_ref[...] + dx

        @pl.when(pl.program_id(0) == 0)
        def _():
            dg_ref[...] = jnp.zeros_like(dg_ref)

        dg_ref[...] += dg

    def rows(wd):
        return pl.BlockSpec((TL, wd), lambda i: (i, 0))

    return _sweep_with_exchange(
        body, L // TL,
        in_specs=[rows(D_MODEL), pl.BlockSpec((1, D_MODEL), lambda i: (0, 0)),
                  rows(W_A), rows(W_B), rows(W_C), rows(W_D),
                  pl.BlockSpec((D_MODEL, N_INP), lambda i: (0, 0)), rows(D_MODEL)],
        out_specs=[rows(D_MODEL), pl.BlockSpec((1, D_MODEL), lambda i: (0, 0))],
        out_shape=[jax.ShapeDtypeStruct((L, D_MODEL), F32), jax.ShapeDtypeStruct((1, D_MODEL), F32)],
        scratch_shapes=[], args=(x, g, dpa, dpb, dpc, dpd, wp, dxo), exchange=exchange, name="proj_bwd_x")


def _dwin(h, dpa, dpb, dpc, dpd):
    L = h.shape[0]

    def body(h_ref, a_ref, b_ref, c_ref, d_ref, oa_ref, ob_ref, oc_ref, od_ref):
        outs = (oa_ref, ob_ref, oc_ref, od_ref)

        @pl.when(pl.program_id(0) == 0)
        def _():
            for o in outs:
                o[...] = jnp.zeros_like(o)

        ht = h_ref[...].T
        for ref, o in zip((a_ref, b_ref, c_ref, d_ref), outs):
            o[...] += jnp.dot(ht, ref[...].astype(BF), preferred_element_type=F32)

    def rows(wd):
        return pl.BlockSpec((TL, wd), lambda i: (i, 0))

    def whole(wd):
        return pl.BlockSpec((D_MODEL, wd), lambda i: (0, 0))

    widths = (W_A, W_B, W_C, W_D)
    return pl.pallas_call(
        body, grid=(L // TL,),
        in_specs=[rows(D_MODEL)] + [rows(wd) for wd in widths],
        out_specs=[whole(wd) for wd in widths],
        out_shape=[jax.ShapeDtypeStruct((D_MODEL, wd), F32) for wd in widths],
        name="dwin", compiler_params=_cp("arbitrary"))(h, dpa, dpb, dpc, dpd)


def _dwout(ya, yb, yc, yd, dxo):
    L = dxo.shape[0]
    tk, tn = min(512, L), 512

    def body(a_ref, b_ref, c_ref, d_ref, g_ref, o_ref):
        @pl.when(pl.program_id(1) == 0)
        def _():
            o_ref[...] = jnp.zeros_like(o_ref)

        gb = g_ref[...].astype(BF)
        for j, ref in enumerate((a_ref, b_ref, c_ref, d_ref)):
            o_ref[j * BR:(j + 1) * BR, :] += lax.dot_general(ref[...].astype(BF), gb, (((0,), (0,)), ((), ())),
                                                             preferred_element_type=F32)

    ys = pl.BlockSpec((tk, BR), lambda j, t: (t, 0))
    return pl.pallas_call(
        body, grid=(D_MODEL // tn, L // tk),
        in_specs=[ys, ys, ys, ys, pl.BlockSpec((tk, tn), lambda j, t: (t, j))],
        out_specs=pl.BlockSpec((D_MODEL, tn), lambda j, t: (0, j)),
        out_shape=jax.ShapeDtypeStruct((D_MODEL, D_MODEL), F32),
        name="dwout", compiler_params=_cp("parallel", "arbitrary"))(ya, yb, yc, yd, dxo)


def _out_fwd(x, ya, yb, yc, yd, wo):
    L = x.shape[0]

    def body(x_ref, a_ref, b_ref, c_ref, d_ref, w_ref, o_ref):
        acc = x_ref[...]
        for j, ref in enumerate((a_ref, b_ref, c_ref, d_ref)):
            acc = acc + jnp.dot(ref[...].astype(BF), w_ref[j * BR:(j + 1) * BR, :], preferred_element_type=F32)
        o_ref[...] = acc

    def rows(wd):
        return pl.BlockSpec((TL, wd), lambda i: (i, 0))

    return pl.pallas_call(
        body, grid=(L // TL,),
        in_specs=[rows(D_MODEL), rows(BR), rows(BR), rows(BR), rows(BR),
                  pl.BlockSpec((D_MODEL, D_MODEL), lambda i: (0, 0))],
        out_specs=rows(D_MODEL), out_shape=jax.ShapeDtypeStruct((L, D_MODEL), F32),
        name="out_fwd", compiler_params=_cp("parallel"))(x, ya, yb, yc, yd, wo)


def _out_bwd_x(dxo, wo):
    L = dxo.shape[0]

    def body(d_ref, w_ref, o_ref):
        o_ref[...] = lax.dot_general(d_ref[...].astype(BF), w_ref[...], (((1,), (1,)), ((), ())),
                                     preferred_element_type=F32)

    return pl.pallas_call(
        body, grid=(L // TL,),
        in_specs=[pl.BlockSpec((TL, D_MODEL), lambda i: (i, 0)), pl.BlockSpec((D_MODEL, D_MODEL), lambda i: (0, 0))],
        out_specs=pl.BlockSpec((TL, D_MODEL), lambda i: (i, 0)),
        out_shape=jax.ShapeDtypeStruct((L, D_MODEL), F32),
        name="out_bwd_x", compiler_params=_cp("parallel"))(dxo, wo)


def _loss_bwd(x, g, tgt):
    L = x.shape[0]

    def f(xv, gv, tv):
        err = _rmsnorm(xv, gv) - tv
        return 0.5 * jnp.sum(jnp.mean(err * err, axis=-1, keepdims=True), axis=0, keepdims=True)

    def body(x_ref, g_ref, t_ref, loss_ref, dx_ref, dg_ref):
        tv = t_ref[...]
        loss, vj = jax.vjp(lambda a, b: f(a, b, tv), x_ref[...], g_ref[...])
        dx, dg = vj(jnp.ones((1, 1), F32))
        dx_ref[...] = dx

        @pl.when(pl.program_id(0) == 0)
        def _():
            dg_ref[...] = jnp.zeros_like(dg_ref)
            loss_ref[...] = jnp.zeros_like(loss_ref)

        dg_ref[...] += dg
        loss_ref[...] += jnp.broadcast_to(loss, loss_ref.shape)

    return pl.pallas_call(
        body, grid=(L // TL,),
        in_specs=[pl.BlockSpec((TL, D_MODEL), lambda i: (i, 0)), pl.BlockSpec((1, D_MODEL), lambda i: (0, 0)),
                  pl.BlockSpec((TL, D_MODEL), lambda i: (i, 0))],
        out_specs=[pl.BlockSpec((8, 128), lambda i: (0, 0)), pl.BlockSpec((TL, D_MODEL), lambda i: (i, 0)),
                   pl.BlockSpec((1, D_MODEL), lambda i: (0, 0))],
        out_shape=[jax.ShapeDtypeStruct((8, 128), F32), jax.ShapeDtypeStruct((L, D_MODEL), F32),
                   jax.ShapeDtypeStruct((1, D_MODEL), F32)],
        name="loss_bwd", compiler_params=_cp("arbitrary"))(x, g, tgt)


def _a_pre(val, gate):
    return val * _sigmoid(gate)


def _a_post(acc, az, cb, lng, lnb, pw, pwb):
    t = acc + cb
    mu = jnp.mean(t, axis=-1, keepdims=True)
    xc = t - mu
    ln = xc * lax.rsqrt(jnp.mean(xc * xc, axis=-1, keepdims=True) + EPS) * lng + lnb
    return (_mm(_silu(ln), pw) + pwb) * _silu(az)


def _halo_map(tl, halo, col):
    r = tl // halo
    return lambda i: (jnp.maximum(i * r - 1, 0), col)


def _a_fwd(proj, cw, cb, lng, lnb, pw, pwb):
    L = proj.shape[0]

    def body(vg_ref, az_ref, hvg_ref, cw_ref, cb_ref, lng_ref, lnb_ref, pw_ref, pwb_ref, o_ref):
        keep = (pl.program_id(0) > 0).astype(F32)
        a_h = _a_pre(hvg_ref[:, 0:BR], hvg_ref[:, BR:2 * BR]) * keep
        a_t = _a_pre(vg_ref[:, 0:BR], vg_ref[:, BR:2 * BR])
        ext = jnp.concatenate([a_h, a_t], axis=0)
        acc = _conv_taps(ext, cw_ref, HALO_A, K_A, TL)
        o_ref[...] = _a_post(acc, az_ref[...], cb_ref[...], lng_ref[...], lnb_ref[...], pw_ref[...], pwb_ref[...])

    vec = pl.BlockSpec((1, BR), lambda i: (0, 0))
    return pl.pallas_call(
        body, grid=(L // TL,),
        in_specs=[pl.BlockSpec((TL, 2 * BR), lambda i: (i, 0)), pl.BlockSpec((TL, BR), lambda i: (i, 2)),
                  pl.BlockSpec((HALO_A, 2 * BR), _halo_map(TL, HALO_A, 0)),
                  pl.BlockSpec((HALO_A, BR), lambda i: (0, 0)), vec, vec, vec,
                  pl.BlockSpec((BR, BR), lambda i: (0, 0)), vec],
        out_specs=pl.BlockSpec((TL, BR), lambda i: (i, 0)),
        out_shape=jax.ShapeDtypeStruct((L, BR), F32),
        name="a_fwd", compiler_params=_cp("parallel"))(proj, proj, proj, cw, cb, lng, lnb, pw, pwb)


def _a_bwd(proj, dmix, cw, cb, lng, lnb, pw, pwb):
    L = proj.shape[0]
    n = L // TL

    def body(vg_ref, az_ref, hvg_ref, dy_ref, cw_ref, cb_ref, lng_ref, lnb_ref, pw_ref, pwb_ref,
             dp_ref, dcw_ref, dcb_ref, dlng_ref, dlnb_ref, dpw_ref, dpwb_ref, carry_ref):
        i = pl.program_id(0)

        @pl.when(i == 0)
        def _():
            carry_ref[...] = jnp.zeros_like(carry_ref)
            for r in (dcw_ref, dcb_ref, dlng_ref, dlnb_ref, dpw_ref, dpwb_ref):
                r[...] = jnp.zeros_like(r)

        keep = (i < n - 1).astype(F32)
        val, gate = vg_ref[:, 0:BR], vg_ref[:, BR:2 * BR]
        a_h = _a_pre(hvg_ref[:, 0:BR], hvg_ref[:, BR:2 * BR]) * keep
        a_t, vj_pre = jax.vjp(_a_pre, val, gate)
        ext = jnp.concatenate([a_h, a_t], axis=0)
        acc = _conv_taps(ext, cw_ref, HALO_A, K_A, TL)
        _, vj_post = jax.vjp(_a_post, acc, az_ref[...], cb_ref[...], lng_ref[...], lnb_ref[...], pw_ref[...],
                             pwb_ref[...])
        dacc, daz, dcb, dlng, dlnb, dpw, dpwb = vj_post(dy_ref[...])
        dext = _conv_taps_bwd(ext, cw_ref, dcw_ref, dacc, HALO_A, K_A, TL)
        da = _add_tail(dext[HALO_A:], carry_ref[...])
        carry_ref[...] = dext[:HALO_A]
        dval, dgate = vj_pre(da)
        dp_ref[:, 0:BR] = dval
        dp_ref[:, BR:2 * BR] = dgate
        dp_ref[:, 2 * BR:3 * BR] = daz
        dcb_ref[...] += dcb
        dlng_ref[...] += dlng
        dlnb_ref[...] += dlnb
        dpw_ref[...] += dpw
        dpwb_ref[...] += dpwb

    rev = lambda i: n - 1 - i
    vec = pl.BlockSpec((1, BR), lambda i: (0, 0))
    hmap = _halo_map(TL, HALO_A, 0)
    return pl.pallas_call(
        body, grid=(n,),
        in_specs=[pl.BlockSpec((TL, 2 * BR), lambda i: (rev(i), 0)), pl.BlockSpec((TL, BR), lambda i: (rev(i), 2)),
                  pl.BlockSpec((HALO_A, 2 * BR), lambda i: hmap(rev(i))),
                  pl.BlockSpec((TL, BR), lambda i: (rev(i), 0)),
                  pl.BlockSpec((HALO_A, BR), lambda i: (0, 0)), vec, vec, vec,
                  pl.BlockSpec((BR, BR), lambda i: (0, 0)), vec],
        out_specs=[pl.BlockSpec((TL, W_A), lambda i: (rev(i), 0)),
                   pl.BlockSpec((HALO_A, BR), lambda i: (0, 0)), vec, vec, vec,
                   pl.BlockSpec((BR, BR), lambda i: (0, 0)), vec],
        out_shape=[jax.ShapeDtypeStruct((L, W_A), F32), jax.ShapeDtypeStruct((HALO_A, BR), F32)]
        + [jax.ShapeDtypeStruct((1, BR), F32)] * 3
        + [jax.ShapeDtypeStruct((BR, BR), F32), jax.ShapeDtypeStruct((1, BR), F32)],
        scratch_shapes=[pltpu.VMEM((HALO_A, BR), F32)],
        name="a_bwd", compiler_params=_cp("arbitrary"))(proj, proj, proj, dmix, cw, cb, lng, lnb, pw, pwb)


def _c_pre(cg, xc):
    return cg * xc


def _c_post(acc, bg, cz):
    return bg * acc * _silu(cz)


def _c_fwd(proj, cw):
    L = proj.shape[0]

    def body(bg_ref, cx_ref, cz_ref, hcx_ref, cw_ref, o_ref):
        keep = (pl.program_id(0) > 0).astype(F32)
        p_h = _c_pre(hcx_ref[:, 0:BR], hcx_ref[:, BR:2 * BR]) * keep
        p_t = _c_pre(cx_ref[:, 0:BR], cx_ref[:, BR:2 * BR])
        ext = jnp.concatenate([p_h, p_t], axis=0)
        acc = _conv_taps(ext, cw_ref, HALO_S, K_C, TL)
        o_ref[...] = _c_post(acc, bg_ref[...], cz_ref[...])

    return pl.pallas_call(
        body, grid=(L // TL,),
        in_specs=[pl.BlockSpec((TL, BR), lambda i: (i, 5)), pl.BlockSpec((TL, 2 * BR), lambda i: (i, 3)),
                  pl.BlockSpec((TL, BR), lambda i: (i, 8)),
                  pl.BlockSpec((HALO_S, 2 * BR), _halo_map(TL, HALO_S, 3)),
                  pl.BlockSpec((HALO_S, BR), lambda i: (0, 0))],
        out_specs=pl.BlockSpec((TL, BR), lambda i: (i, 0)),
        out_shape=jax.ShapeDtypeStruct((L, BR), F32),
        name="c_fwd", compiler_params=_cp("parallel"))(proj, proj, proj, proj, cw)


def _c_bwd(proj, dmix, cw):
    L = proj.shape[0]
    n = L // TL

    def body(bg_ref, cx_ref, cz_ref, hcx_ref, dy_ref, cw_ref, dp_ref, dcw_ref, carry_ref):
        i = pl.program_id(0)

        @pl.when(i == 0)
        def _():
            carry_ref[...] = jnp.zeros_like(carry_ref)
            dcw_ref[...] = jnp.zeros_like(dcw_ref)

        keep = (i < n - 1).astype(F32)
        p_h = _c_pre(hcx_ref[:, 0:BR], hcx_ref[:, BR:2 * BR]) * keep
        p_t, vj_pre = jax.vjp(_c_pre, cx_ref[:, 0:BR], cx_ref[:, BR:2 * BR])
        ext = jnp.concatenate([p_h, p_t], axis=0)
        acc = _conv_taps(ext, cw_ref, HALO_S, K_C, TL)
        _, vj_post = jax.vjp(_c_post, acc, bg_ref[...], cz_ref[...])
        dacc, dbg, dcz = vj_post(dy_ref[...])
        dext = _conv_taps_bwd(ext, cw_ref, dcw_ref, dacc, HALO_S, K_C, TL)
        dp = _add_tail(dext[HALO_S:], carry_ref[...])
        carry_ref[...] = dext[:HALO_S]
        dcg, dxc = vj_pre(dp)
        dp_ref[:, 0:BR] = dbg
        dp_ref[:, BR:2 * BR] = dcg
        dp_ref[:, 2 * BR:3 * BR] = dxc
        dp_ref[:, 3 * BR:4 * BR] = dcz

    rev = lambda i: n - 1 - i
    hmap = _halo_map(TL, HALO_S, 3)
    return pl.pallas_call(
        body, grid=(n,),
        in_specs=[pl.BlockSpec((TL, BR), lambda i: (rev(i), 5)), pl.BlockSpec((TL, 2 * BR), lambda i: (rev(i), 3)),
                  pl.BlockSpec((TL, BR), lambda i: (rev(i), 8)),
                  pl.BlockSpec((HALO_S, 2 * BR), lambda i: hmap(rev(i))),
                  pl.BlockSpec((TL, BR), lambda i: (rev(i), 2)),
                  pl.BlockSpec((HALO_S, BR), lambda i: (0, 0))],
        out_specs=[pl.BlockSpec((TL, W_C), lambda i: (rev(i), 0)), pl.BlockSpec((HALO_S, BR), lambda i: (0, 0))],
        out_shape=[jax.ShapeDtypeStruct((L, W_C), F32), jax.ShapeDtypeStruct((HALO_S, BR), F32)],
        scratch_shapes=[pltpu.VMEM((HALO_S, BR), F32)],
        name="c_bwd", compiler_params=_cp("arbitrary"))(proj, proj, proj, proj, dmix, cw)


def _s5_prep_fn(lre, lim, ldt, bre, bim, cre, cim):
    grp = lax.broadcasted_iota(jnp.int32, (128, NS), 0)
    lane = lax.broadcasted_iota(jnp.int32, (128, NS), 1)
    expand = (grp == lane // S5_P).astype(F32)
    dt = jnp.exp(_dot_hi(jnp.broadcast_to(ldt, (8, 128)), expand)[0:1])
    lr = jnp.minimum(lre, -1e-4)
    mag = jnp.exp(lr * dt)
    ar = mag * jnp.cos(lim * dt)
    ai = mag * jnp.sin(lim * dt)
    den = lr * lr + lim * lim
    fr = ((ar - 1.0) * lr + ai * lim) / den
    fi = (ai * lr - (ar - 1.0) * lim) / den
    bbr = fr * bre - fi * bim
    bbi = fr * bim + fi * bre
    row = lax.broadcasted_iota(jnp.int32, (BR, NS), 0)
    col = lax.broadcasted_iota(jnp.int32, (BR, NS), 1)
    blk = (row // S5_H == col // S5_P).astype(F32)

    def embed(t):
        return jnp.concatenate([t] * S5_G, axis=0) * blk

    bemb = jnp.concatenate([embed(bbr), embed(bbi)], axis=1)
    cemb = jnp.concatenate([embed(cre), embed(-cim)], axis=1)
    return ar, ai, bemb, cemb


def _s5_prep(lre, lim, ldt, bre, bim, cre, cim):
    def body(*refs):
        outs = _s5_prep_fn(*[r[...] for r in refs[:7]])
        for r, o in zip(refs[7:], outs):
            r[...] = o

    return pl.pallas_call(
        body,
        out_shape=[jax.ShapeDtypeStruct((1, NS), F32)] * 2 + [jax.ShapeDtypeStruct((BR, 2 * NS), F32)] * 2,
        name="s5_prep", compiler_params=pltpu.CompilerParams(vmem_limit_bytes=VMEM_LIMIT),
    )(lre, lim, ldt, bre, bim, cre, cim)


def _s5_prep_bwd(lre, lim, ldt, bre, bim, cre, cim, dar, dai, dbemb, dcemb):
    def body(*refs):
        _, vj = jax.vjp(_s5_prep_fn, *[r[...] for r in refs[:7]])
        grads = vj(tuple(r[...] for r in refs[7:11]))
        for r, o in zip(refs[11:], grads):
            r[...] = o

    return pl.pallas_call(
        body,
        out_shape=[jax.ShapeDtypeStruct((1, NS), F32)] * 2 + [jax.ShapeDtypeStruct((1, 128), F32)]
        + [jax.ShapeDtypeStruct((S5_H, NS), F32)] * 4,
        name="s5_prep_bwd", compiler_params=pltpu.CompilerParams(vmem_limit_bytes=VMEM_LIMIT),
    )(lre, lim, ldt, bre, bim, cre, cim, dar, dai, dbemb, dcemb)


def _s5_scan(xr, xi, ar, ai, reverse):
    n = xr.shape[0]
    row = lax.broadcasted_iota(jnp.int32, (n, 1), 0)
    pr, pi = ar, ai
    d = 1
    while d < n:
        if d % 8:
            if reverse:
                m = row < n - d
                sr = jnp.where(m, _roll(xr, n - d), 0.0)
                si = jnp.where(m, _roll(xi, n - d), 0.0)
            else:
                m = row >= d
                sr = jnp.where(m, _roll(xr, d), 0.0)
                si = jnp.where(m, _roll(xi, d), 0.0)
            xr, xi = xr + pr * sr - pi * si, xi + pr * si + pi * sr
        elif reverse:
            sr, si = xr[d:], xi[d:]
            xr, xi = (jnp.concatenate([xr[:n - d] + pr * sr - pi * si, xr[n - d:]], axis=0),
                      jnp.concatenate([xi[:n - d] + pr * si + pi * sr, xi[n - d:]], axis=0))
        else:
            sr, si = xr[:n - d], xi[:n - d]
            xr, xi = (jnp.concatenate([xr[:d], xr[d:] + pr * sr - pi * si], axis=0),
                      jnp.concatenate([xi[:d], xi[d:] + pr * si + pi * sr], axis=0))
        pr, pi = pr * pr - pi * pi, 2.0 * pr * pi
        d *= 2
    return xr, xi


def _s5_states(u, bemb_b, ar, ai, sin_r, sin_i):
    bu = jnp.dot(u.astype(BF), bemb_b, preferred_element_type=F32)
    first = lax.broadcasted_iota(jnp.int32, (u.shape[0], 1), 0) == 0
    xr = bu[:, :NS] + jnp.where(first, ar * sin_r - ai * sin_i, 0.0)
    xi = bu[:, NS:] + jnp.where(first, ar * sin_i + ai * sin_r, 0.0)
    return _s5_scan(xr, xi, ar, ai, False)


def _b_post(yssm, u, bz, dsk, gw, gb):
    z = jax.nn.gelu(yssm + dsk * u)
    return z * _sigmoid(_mm(z, gw) + gb) * _silu(bz)


def _b_fwd(proj, ar, ai, bemb, cemb, dsk, gw, gb):
    L = proj.shape[0]
    n = L // TL

    def body(u_ref, bz_ref, ar_ref, ai_ref, be_ref, ce_ref, dsk_ref, gw_ref, gb_ref, o_ref, sin_ref, carry_ref):
        @pl.when(pl.program_id(0) == 0)
        def _():
            carry_ref[...] = jnp.zeros_like(carry_ref)

        sin = carry_ref[...]
        sin_ref[0] = sin
        u = u_ref[...]
        sr, si = _s5_states(u, be_ref[...].astype(BF), ar_ref[...], ai_ref[...], sin[:, :NS], sin[:, NS:])
        carry_ref[:, :NS] = sr[TL - 1:TL]
        carry_ref[:, NS:] = si[TL - 1:TL]
        s = jnp.concatenate([sr, si], axis=1).astype(BF)
        yssm = lax.dot_general(s, ce_ref[...].astype(BF), (((1,), (1,)), ((), ())), preferred_element_type=F32)
        o_ref[...] = _b_post(yssm, u, bz_ref[...], dsk_ref[...], gw_ref[...], gb_ref[...])

    vec = pl.BlockSpec((1, BR), lambda i: (0, 0))
    svec = pl.BlockSpec((1, NS), lambda i: (0, 0))
    emb = pl.BlockSpec((BR, 2 * NS), lambda i: (0, 0))
    return pl.pallas_call(
        body, grid=(n,),
        in_specs=[pl.BlockSpec((TL, BR), lambda i: (i, 3)), pl.BlockSpec((TL, BR), lambda i: (i, 4)),
                  svec, svec, emb, emb, vec, pl.BlockSpec((BR, BR), lambda i: (0, 0)), vec],
        out_specs=[pl.BlockSpec((TL, BR), lambda i: (i, 0)), pl.BlockSpec((1, 1, 2 * NS), lambda i: (i, 0, 0))],
        out_shape=[jax.ShapeDtypeStruct((L, BR), F32), jax.ShapeDtypeStruct((n, 1, 2 * NS), F32)],
        scratch_shapes=[pltpu.VMEM((1, 2 * NS), F32)],
        name="b_fwd", compiler_params=_cp("arbitrary"))(proj, proj, ar, ai, bemb, cemb, dsk, gw, gb)


def _b_bwd(proj, dmix, sin_all, ar, ai, bemb, cemb, dsk, gw, gb, exchange=None):
    L = proj.shape[0]
    n = L // TL

    def body(u_ref, bz_ref, dy_ref, sin_ref, ar_ref, ai_ref, be_ref, ce_ref, dsk_ref, gw_ref, gb_ref,
             dp_ref, dar_ref, dai_ref, dbe_ref, dce_ref, ddsk_ref, dgw_ref, dgb_ref, carry_ref):
        i = pl.program_id(0)

        @pl.when(i == 0)
        def _():
            carry_ref[...] = jnp.zeros_like(carry_ref)
            for r in (dar_ref, dai_ref, dbe_ref, dce_ref, ddsk_ref, dgw_ref, dgb_ref):
                r[...] = jnp.zeros_like(r)

        u = u_ref[...]
        ar, ai = ar_ref[...], ai_ref[...]
        be_b, ce_b = be_ref[...].astype(BF), ce_ref[...].astype(BF)
        sin = sin_ref[0]
        sr, si = _s5_states(u, be_b, ar, ai, sin[:, :NS], sin[:, NS:])
        s_b = jnp.concatenate([sr, si], axis=1).astype(BF)
        yssm = lax.dot_general(s_b, ce_b, (((1,), (1,)), ((), ())), preferred_element_type=F32)
        _, vj = jax.vjp(_b_post, yssm, u, bz_ref[...], dsk_ref[...], gw_ref[...], gb_ref[...])
        dyssm, du, dbz, ddsk, dgw, dgb = vj(dy_ref[...])
        dy_b = dyssm.astype(BF)
        dce_ref[...] += lax.dot_general(dy_b, s_b, (((0,), (0,)), ((), ())), preferred_element_type=F32)
        gs = jnp.dot(dy_b, ce_b, preferred_element_type=F32)
        last = lax.broadcasted_iota(jnp.int32, (TL, 1), 0) == TL - 1
        cr, ci = carry_ref[:, :NS], carry_ref[:, NS:]
        gr = gs[:, :NS] + jnp.where(last, ar * cr + ai * ci, 0.0)
        gi = gs[:, NS:] + jnp.where(last, ar * ci - ai * cr, 0.0)
        dsr, dsi = _s5_scan(gr, gi, ar, -ai, True)
        carry_ref[:, :NS] = dsr[0:1]
        carry_ref[:, NS:] = dsi[0:1]
        first = lax.broadcasted_iota(jnp.int32, (TL, 1), 0) == 0
        pr = jnp.where(first, sin[:, :NS], _roll(sr, 1))
        pi = jnp.where(first, sin[:, NS:], _roll(si, 1))
        dar_ref[...] += jnp.sum(dsr * pr + dsi * pi, axis=0, keepdims=True)
        dai_ref[...] += jnp.sum(dsi * pr - dsr * pi, axis=0, keepdims=True)
        ds_b = jnp.concatenate([dsr, dsi], axis=1).astype(BF)
        dbe_ref[...] += lax.dot_general(u.astype(BF), ds_b, (((0,), (0,)), ((), ())), preferred_element_type=F32)
        du = du + lax.dot_general(ds_b, be_b, (((1,), (1,)), ((), ())), preferred_element_type=F32)
        dp_ref[:, 0:BR] = du
        dp_ref[:, BR:2 * BR] = dbz
        ddsk_ref[...] += ddsk
        dgw_ref[...] += dgw
        dgb_ref[...] += dgb

    rev = lambda i: n - 1 - i
    vec = pl.BlockSpec((1, BR), lambda i: (0, 0))
    svec = pl.BlockSpec((1, NS), lambda i: (0, 0))
    emb = pl.BlockSpec((BR, 2 * NS), lambda i: (0, 0))
    mat = pl.BlockSpec((BR, BR), lambda i: (0, 0))
    return _sweep_with_exchange(
        body, n,
        in_specs=[pl.BlockSpec((TL, BR), lambda i: (rev(i), 3)), pl.BlockSpec((TL, BR), lambda i: (rev(i), 4)),
                  pl.BlockSpec((TL, BR), lambda i: (rev(i), 1)),
                  pl.BlockSpec((1, 1, 2 * NS), lambda i: (rev(i), 0, 0)),
                  svec, svec, emb, emb, vec, mat, vec],
        out_specs=[pl.BlockSpec((TL, W_B), lambda i: (rev(i), 0)), svec, svec, emb, emb, vec, mat, vec],
        out_shape=[jax.ShapeDtypeStruct((L, W_B), F32)] + [jax.ShapeDtypeStruct((1, NS), F32)] * 2
        + [jax.ShapeDtypeStruct((BR, 2 * NS), F32)] * 2
        + [jax.ShapeDtypeStruct((1, BR), F32), jax.ShapeDtypeStruct((BR, BR), F32), jax.ShapeDtypeStruct((1, BR), F32)],
        scratch_shapes=[pltpu.VMEM((1, 2 * NS), F32)],
        args=(proj, proj, dmix, sin_all, ar, ai, bemb, cemb, dsk, gw, gb), exchange=exchange, name="b_bwd")


def _d_post(cq, ab, dz, s0, s1, s2, s3, p1, p2, ng, known=None):
    c = cq.shape[0]
    qkv = _silu(cq)
    gall = -jnp.exp(p1) * jax.nn.softplus(ab + p2)
    ball = _sigmoid(ab)
    row = lax.broadcasted_iota(jnp.int32, (c, c), 0)
    col = lax.broadcasted_iota(jnp.int32, (c, c), 1)
    causal = row >= col
    strict = row > col
    gc_all = _dot_hi(causal.astype(F32), gall)
    gc_t = gc_all.T
    heads = range(DN_H)
    st = (s0, s1, s2, s3)
    q = [qkv[:, h * DN_D:(h + 1) * DN_D] for h in heads]
    k = [qkv[:, BR + h * DN_D:BR + (h + 1) * DN_D] for h in heads]
    v = [qkv[:, 2 * BR + h * DN_D:2 * BR + (h + 1) * DN_D] for h in heads]
    q = [t * lax.rsqrt(jnp.sum(t * t, axis=-1, keepdims=True) + EPS) * (DN_D ** -0.5) for t in q]
    k = [t * lax.rsqrt(jnp.sum(t * t, axis=-1, keepdims=True) + EPS) for t in k]
    gcol = [gc_all[:, h:h + 1] for h in heads]
    beta = [ball[:, DN_H + h:DN_H + h + 1] for h in heads]
    decay = [jnp.where(causal, jnp.exp(jnp.where(causal, gcol[h] - gc_t[h:h + 1, :], 0.0)), 0.0) for h in heads]
    kb = [k[h] * beta[h] for h in heads]
    lm = [jnp.where(strict, _mm_nt(kb[h], k[h]) * decay[h], 0.0) for h in heads]
    ainv = _unit_lower_inv(tuple(lm)) if known is None else _known_inverse(tuple(lm), known)
    egc = [jnp.exp(g) for g in gcol]
    uw = [_mm3(ainv[h], jnp.concatenate([v[h] * beta[h], kb[h] * egc[h]], axis=1)) for h in heads]
    attn = [_mm_nt(q[h], k[h]) * decay[h] for h in heads]
    glast = [g[c - 1:c, :] for g in gcol]
    kd = [k[h] * jnp.exp(glast[h] - gcol[h]) for h in heads]
    vnew = [uw[h][:, :DN_D] - _mm(uw[h][:, DN_D:], st[h]) for h in heads]
    o = [_mm(q[h] * egc[h], st[h]) + _mm(attn[h], vnew[h]) for h in heads]
    news = [st[h] * jnp.exp(glast[h]) + _mm_tn(kd[h], vnew[h]) for h in heads]
    outs = [t * lax.rsqrt(jnp.mean(t * t, axis=-1, keepdims=True) + EPS) * ng for t in o]
    yd = jnp.concatenate(outs, axis=1) * _silu(dz)
    return (yd, *news), ainv


def _d_fwd(proj, cw, p1, p2, ng, exchange=None):
    L = proj.shape[0]
    DN_G, DN_T = DN_G_FWD, DN_G_FWD * DN_C
    n = L // DN_T

    def body(qkv_ref, ab_ref, dz_ref, hq_ref, cw_ref, p1_ref, p2_ref, ng_ref, o_ref, sall_ref, inv_ref, s_ref):
        i = pl.program_id(0)

        @pl.when(i == 0)
        def _():
            s_ref[...] = jnp.zeros_like(s_ref)

        keep = (i > 0).astype(F32)
        ext = jnp.concatenate([hq_ref[...] * keep, qkv_ref[...]], axis=0)
        cq = _conv_taps(ext, cw_ref, HALO_S, K_DN, DN_T)
        st = [s_ref[h] for h in range(DN_H)]
        for g in range(DN_G):
            rows = slice(g * DN_C, (g + 1) * DN_C)
            for h in range(DN_H):
                sall_ref[g, h] = st[h]
            out, ainv = _d_post(cq[rows], ab_ref[rows, :], dz_ref[rows, :], *st, p1_ref[...], p2_ref[...],
                                ng_ref[...])
            o_ref[rows, :] = out[0]
            st = list(out[1:])
            for h in range(DN_H):
                inv_ref[g, h] = ainv[h]
        for h in range(DN_H):
            s_ref[h] = st[h]

    per_chunk = pl.BlockSpec((DN_G, DN_H, DN_D, DN_D), lambda i: (i, 0, 0, 0))
    return _sweep_with_exchange(
        body, n,
        in_specs=[pl.BlockSpec((DN_T, 3 * BR), lambda i: (i, 3)), pl.BlockSpec((DN_T, 128), lambda i: (i, 26)),
                  pl.BlockSpec((DN_T, BR), lambda i: (i, 12)),
                  pl.BlockSpec((HALO_S, 3 * BR), _halo_map(DN_T, HALO_S, 3)),
                  pl.BlockSpec((HALO_S, 3 * BR), lambda i: (0, 0)),
                  pl.BlockSpec((1, 128), lambda i: (0, 0)), pl.BlockSpec((1, 128), lambda i: (0, 0)),
                  pl.BlockSpec((1, DN_D), lambda i: (0, 0))],
        out_specs=[pl.BlockSpec((DN_T, BR), lambda i: (i, 0)), per_chunk, per_chunk],
        out_shape=[jax.ShapeDtypeStruct((L, BR), F32)]
        + [jax.ShapeDtypeStruct((L // DN_C, DN_H, DN_D, DN_D), F32)] * 2,
        scratch_shapes=[pltpu.VMEM((DN_H, DN_D, DN_D), F32)],
        args=(proj, proj, proj, proj, cw, p1, p2, ng), exchange=exchange, name="d_fwd")


def _sweep_with_exchange(body, steps, in_specs, out_specs, out_shape, scratch_shapes, args, exchange, name):
    if exchange is None:
        res = pl.pallas_call(body, grid=(steps,), in_specs=in_specs, out_specs=out_specs, out_shape=out_shape,
                             scratch_shapes=scratch_shapes, name=name, compiler_params=_cp("arbitrary"))(*args)
        return res, None
    xs, axes, mode = exchange
    ex = _Exchange(xs, axes, mode)
    ni, no, ns, na = len(in_specs), len(out_specs), len(scratch_shapes), ex.na

    def carried(*refs):
        ins, xin = refs[:ni], refs[ni:ni + na]
        outs, xout = refs[ni + na:ni + na + no], refs[ni + na + no:ni + 2 * na + no]
        scr, sems = refs[ni + 2 * na + no:ni + 2 * na + no + ns], refs[ni + 2 * na + no + ns:]

        @pl.when(pl.program_id(0) == 0)
        def _():
            ex.start(xin, xout, sems)

        body(*ins, *outs, *scr)

        @pl.when(pl.program_id(0) == steps - 1)
        def _():
            ex.wait(xin, xout, sems)

    res = pl.pallas_call(carried, grid=(steps,), in_specs=list(in_specs) + ex.in_specs,
                         out_specs=list(out_specs) + ex.out_specs, out_shape=list(out_shape) + ex.out_shape,
                         scratch_shapes=list(scratch_shapes) + ex.scratch_shapes, name=name + "_x",
                         compiler_params=_cp("arbitrary"))(*args, *xs)
    return res[:no], res[no:]


def _d_bwd(proj, dmix, sall, inv, cw, p1, p2, ng, exchange=None):
    L = proj.shape[0]
    DN_G, DN_T = DN_G_BWD, DN_G_BWD * DN_C
    n = L // DN_T

    def body(qkv_ref, ab_ref, dz_ref, hq_ref, dy_ref, sall_ref, inv_ref, cw_ref, p1_ref, p2_ref, ng_ref,
             dp_ref, dcw_ref, dp1_ref, dp2_ref, dng_ref, ds_ref, carry_ref):
        i = pl.program_id(0)

        @pl.when(i == 0)
        def _():
            ds_ref[...] = jnp.zeros_like(ds_ref)
            carry_ref[...] = jnp.zeros_like(carry_ref)
            for r in (dcw_ref, dp1_ref, dp2_ref, dng_ref):
                r[...] = jnp.zeros_like(r)

        keep = (i < n - 1).astype(F32)
        ext = jnp.concatenate([hq_ref[...] * keep, qkv_ref[...]], axis=0)
        cq = _conv_taps(ext, cw_ref, HALO_S, K_DN, DN_T)
        ds = [ds_ref[h] for h in range(DN_H)]
        dcq = [None] * DN_G
        for g in reversed(range(DN_G)):
            rows = slice(g * DN_C, (g + 1) * DN_C)
            st = [sall_ref[g, h] for h in range(DN_H)]
            known = tuple(inv_ref[g, h] for h in range(DN_H))
            _, vj = jax.vjp(lambda *a: _d_post(*a, known=known)[0], cq[rows], ab_ref[rows, :], dz_ref[rows, :], *st,
                            p1_ref[...], p2_ref[...], ng_ref[...])
            grads = vj((dy_ref[rows, :], *ds))
            dcq[g] = grads[0]
            dp_ref[rows, 3 * BR:4 * BR] = grads[2]
            dp_ref[rows, 4 * BR:4 * BR + 128] = grads[1]
            ds = list(grads[3:3 + DN_H])
            dp1_ref[...] += grads[7]
            dp2_ref[...] += grads[8]
            dng_ref[...] += grads[9]
        for h in range(DN_H):
            ds_ref[h] = ds[h]
        dext = _conv_taps_bwd(ext, cw_ref, dcw_ref, jnp.concatenate(dcq, axis=0), HALO_S, K_DN, DN_T)
        dp_ref[:, 0:3 * BR] = _add_tail(dext[HALO_S:], carry_ref[...])
        carry_ref[...] = dext[:HALO_S]

    rev = lambda i: n - 1 - i
    hmap = _halo_map(DN_T, HALO_S, 3)
    v128 = pl.BlockSpec((1, 128), lambda i: (0, 0))
    per_chunk = pl.BlockSpec((DN_G, DN_H, DN_D, DN_D), lambda i: (rev(i), 0, 0, 0))
    return _sweep_with_exchange(
        body, n,
        in_specs=[pl.BlockSpec((DN_T, 3 * BR), lambda i: (rev(i), 3)), pl.BlockSpec((DN_T, 128), lambda i: (rev(i), 26)),
                  pl.BlockSpec((DN_T, BR), lambda i: (rev(i), 12)),
                  pl.BlockSpec((HALO_S, 3 * BR), lambda i: hmap(rev(i))),
                  pl.BlockSpec((DN_T, BR), lambda i: (rev(i), 3)), per_chunk, per_chunk,
                  pl.BlockSpec((HALO_S, 3 * BR), lambda i: (0, 0)), v128, v128,
                  pl.BlockSpec((1, DN_D), lambda i: (0, 0))],
        out_specs=[pl.BlockSpec((DN_T, W_D), lambda i: (rev(i), 0)),
                   pl.BlockSpec((HALO_S, 3 * BR), lambda i: (0, 0)), v128, v128,
                   pl.BlockSpec((1, DN_D), lambda i: (0, 0))],
        out_shape=[jax.ShapeDtypeStruct((L, W_D), F32), jax.ShapeDtypeStruct((HALO_S, 3 * BR), F32),
                   jax.ShapeDtypeStruct((1, 128), F32), jax.ShapeDtypeStruct((1, 128), F32),
                   jax.ShapeDtypeStruct((1, DN_D), F32)],
        scratch_shapes=[pltpu.VMEM((DN_H, DN_D, DN_D), F32), pltpu.VMEM((HALO_S, 3 * BR), F32)],
        args=(proj, proj, proj, proj, dmix, sall, inv, cw, p1, p2, ng), exchange=exchange, name="d_bwd")


def _pick_rows(rows, cap):
    best = 8
    for t in range(8, cap + 1, 8):
        if rows % t == 0:
            best = t
    return best


def _adamw(w, g, m, v, name, lead=None):
    rows, rest = w.shape[0], w.shape[1:]
    tr = _pick_rows(rows, 512) if lead is None else lead
    c1 = 1.0 - ADAM_B1 ** ADAM_STEP
    c2 = 1.0 - ADAM_B2 ** ADAM_STEP

    def body(w_ref, g_ref, m_ref, v_ref, d_ref, mo_ref, vo_ref):
        gv = g_ref[...]
        mn = ADAM_B1 * m_ref[...] + (1.0 - ADAM_B1) * gv
        vn = ADAM_B2 * v_ref[...] + (1.0 - ADAM_B2) * (gv * gv)
        d_ref[...] = -ADAM_LR * ((mn / c1) / (jnp.sqrt(vn / c2) + ADAM_EPS) + ADAM_WD * w_ref[...])
        mo_ref[...] = mn
        vo_ref[...] = vn

    spec = pl.BlockSpec((tr,) + rest, lambda i: (i,) + (0,) * len(rest))
    return pl.pallas_call(
        body, grid=(rows // tr,), in_specs=[spec] * 4, out_specs=[spec] * 3,
        out_shape=[jax.ShapeDtypeStruct(w.shape, F32)] * 3,
        name=name, compiler_params=_cp("parallel"))(w, g, m, v)


def _sum_slots(r, name):
    n, rows, wd = r.shape
    tr = _pick_rows(rows, 384)

    def body(r_ref, o_ref):
        acc = r_ref[0].astype(F32)
        for j in range(1, n):
            acc = acc + r_ref[j].astype(F32)
        o_ref[...] = acc

    return pl.pallas_call(
        body, grid=(rows // tr,),
        in_specs=[pl.BlockSpec((n, tr, wd), lambda i: (0, i, 0))],
        out_specs=pl.BlockSpec((tr, wd), lambda i: (i, 0)),
        out_shape=jax.ShapeDtypeStruct((rows, wd), F32),
        name=name, compiler_params=_cp("parallel"))(r)


AXES = ("x", "y", "c")


def _group_peer(axes, k):
    pos = {a: lax.axis_index(a) for a in AXES}
    idx = 0
    for a in axes:
        idx = idx * 2 + pos[a]
    peer = dict(pos)
    for b, a in enumerate(reversed(axes)):
        if (k >> b) & 1:
            peer[a] = 1 - pos[a]
    return idx, tuple(peer[a] for a in AXES)


MAX_CHUNKS = 4


class _Exchange:
    def __init__(self, xs, axes, mode):
        self.axes, self.mode, self.na, self.n = axes, mode, len(xs), 2 ** len(axes)
        n = self.n
        self.out_shape, self.pieces = [], []
        for x in xs:
            if mode == "gather":
                shape, lead = (n,) + x.shape, x.shape[0]
            elif mode == "scatter":
                shape, lead = x.shape, x.shape[1]
            else:
                shape, lead = (x.shape[0], n * x.shape[1], x.shape[2]), x.shape[0]
            self.out_shape.append(jax.ShapeDtypeStruct(shape, x.dtype))
            big = x.size * x.dtype.itemsize >= (1 << 20)
            if mode == "rows":
                self.pieces.append(lead if lead <= MAX_CHUNKS else 1)
            else:
                self.pieces.append(MAX_CHUNKS if big and lead % (16 * MAX_CHUNKS) == 0 else 1)
        self.in_specs = [pl.BlockSpec(memory_space=pl.ANY)] * self.na
        self.out_specs = [pl.BlockSpec(memory_space=pl.ANY)] * self.na
        self.scratch_shapes = [pltpu.SemaphoreType.DMA((self.na, MAX_CHUNKS, n)),
                               pltpu.SemaphoreType.DMA((self.na, MAX_CHUNKS, n)),
                               pltpu.SemaphoreType.DMA((self.na, MAX_CHUNKS))]

    def _copies(self, x_refs, o_refs, send_sems, recv_sems, local_sems):
        me, _ = _group_peer(self.axes, 0)
        local, remote = [], []
        for a, (x, o) in enumerate(zip(x_refs, o_refs)):
            for c in range(self.pieces[a]):
                if self.mode == "rows":
                    r = x.shape[1]
                    b = slice(None) if self.pieces[a] == 1 else pl.ds(c, 1)
                    src = lambda k, x=x, b=b: x.at[b]
                    dst = o.at[b, pl.ds(me * r, r)]
                else:
                    lead = x.shape[1] if self.mode == "scatter" else x.shape[0]
                    rs = pl.ds(c * (lead // self.pieces[a]), lead // self.pieces[a])
                    if self.mode == "scatter":
                        src = lambda k, x=x, rs=rs: x.at[me ^ k, rs]
                    else:
                        src = lambda k, x=x, rs=rs: x.at[rs]
                    dst = o.at[me, rs]
                local.append(pltpu.make_async_copy(src(0), dst, local_sems.at[a, c]))
                for k in range(1, self.n):
                    remote.append(pltpu.make_async_remote_copy(
                        src_ref=src(k), dst_ref=dst, send_sem=send_sems.at[a, c, k], recv_sem=recv_sems.at[a, c, k],
                        device_id=_group_peer(self.axes, k)[1], device_id_type=MESH))
        return local, remote

    def start(self, x_refs, o_refs, sems):
        local, remote = self._copies(x_refs, o_refs, *sems)
        for cp in local + remote:
            cp.start()

    def wait(self, x_refs, o_refs, sems):
        local, remote = self._copies(x_refs, o_refs, *sems)
        for cp in remote:
            cp.wait_send()
        for cp in remote:
            cp.wait_recv()
        for cp in local:
            cp.wait()


def _exchange(xs, axes, mode, name):
    ex = _Exchange(xs, axes, mode)
    na = ex.na

    def body(*refs):
        ex.start(refs[:na], refs[na:2 * na], refs[2 * na:])
        ex.wait(refs[:na], refs[na:2 * na], refs[2 * na:])

    return pl.pallas_call(body, out_shape=ex.out_shape, in_specs=ex.in_specs, out_specs=ex.out_specs,
                          scratch_shapes=ex.scratch_shapes, name=name)(*xs)


SHARDED_SMALL = (("a_conv_w", 2), ("a_pw_w", 1), ("s5_glu_w", 1), ("c_conv_w", 2), ("d_conv_w", 2))
REPLICATED = ("norm_g", "a_conv_b", "a_ln_g", "a_ln_b", "a_pw_b", "s5_lambda_re", "s5_lambda_im", "s5_b_re",
              "s5_b_im", "s5_c_re", "s5_c_im", "s5_d", "s5_log_dt", "s5_glu_b", "d_a_log", "d_dt_bias",
              "d_norm_g", "final_g")
WEIGHTS = ("norm_g", "w_in", "a_conv_w", "a_conv_b", "a_ln_g", "a_ln_b", "a_pw_w", "a_pw_b", "s5_lambda_re",
           "s5_lambda_im", "s5_b_re", "s5_b_im", "s5_c_re", "s5_c_im", "s5_d", "s5_log_dt", "s5_glu_w",
           "s5_glu_b", "c_conv_w", "d_conv_w", "d_a_log", "d_dt_bias", "d_norm_g", "w_out", "final_g")
LANES = 1024


def _size(shape):
    size = 1
    for d in shape:
        size *= d
    return size


def _slab_rows(shape):
    return -(-_size(shape) // (8 * LANES)) * 8


def _pack(arrs, rows):
    parts = []
    for a in arrs:
        r = _slab_rows(a.shape)
        parts.append(jnp.pad(a.reshape(-1), (0, r * LANES - a.size)).reshape(r, LANES))
    used = sum(p.shape[0] for p in parts)
    if rows > used:
        parts.append(jnp.zeros((rows - used, LANES), parts[0].dtype))
    return jnp.concatenate(parts, axis=0)


def _unpack(slab, shapes):
    out, off = [], 0
    for s in shapes:
        r = _slab_rows(s)
        out.append(slab[off:off + r].reshape(-1)[:_size(s)].reshape(s))
        off += r
    return out


def _rows_for(shapes, mult):
    rows = sum(_slab_rows(s) for s in shapes)
    return -(-rows // mult) * mult


def _row(v, width=None):
    v = v.reshape(1, -1)
    return v if width is None else jnp.pad(v, ((0, 0), (0, width - v.shape[1])))


def _pad_rows(w, rows):
    return jnp.pad(w, ((0, rows - w.shape[0]), (0, 0)))


def _permute_in(w):
    return jnp.concatenate([w[:, :3072], w[:, 3080:N_IN], w[:, 3072:3080],
                            jnp.zeros((w.shape[0], N_INP - N_IN), w.dtype)], axis=1)


def _layer_fwd(x, p, exchange=None):
    proj, h = _proj_fwd(x, p["norm_g"], p["wp"])
    ya = _a_fwd(proj, p["a_cw"], p["a_cb"], p["a_lng"], p["a_lnb"], p["a_pw"], p["a_pwb"])
    ar, ai, bemb, cemb = _s5_prep(*p["s5"])
    yb, sin_all = _b_fwd(proj, ar, ai, bemb, cemb, p["s5_d"], p["glu_w"], p["glu_b"])
    yc = _c_fwd(proj, p["c_cw"])
    (yd, sall, inv), got = _d_fwd(proj, p["d_cw"], p["d_p1"], p["d_p2"], p["d_ng"], exchange)
    xo = _out_fwd(x, ya, yb, yc, yd, p["wo"])
    return xo, dict(x=x, proj=proj, h=h, ys=(ya, yb, yc, yd), sin_all=sin_all, sall=sall, inv=inv,
                    s5=(ar, ai, bemb, cemb)), got


def _layer_bwd(dxo, p, r, exchanges):
    proj = r["proj"]
    ar, ai, bemb, cemb = r["s5"]
    dmix = _out_bwd_x(dxo, p["wo"])
    dwo = _dwout(*r["ys"], dxo)
    dpa, dcw_a, dcb, dlng, dlnb, dpw, dpwb = _a_bwd(proj, dmix, p["a_cw"], p["a_cb"], p["a_lng"], p["a_lnb"],
                                                     p["a_pw"], p["a_pwb"])
    got = {}
    (dpb, dar, dai, dbe, dce, ddsk, dgw, dgb), got["b"] = _b_bwd(
        proj, dmix, r["sin_all"], ar, ai, bemb, cemb, p["s5_d"], p["glu_w"], p["glu_b"], exchanges.get("b"))
    dlre, dlim, dldt, dbre, dbim, dcre, dcim = _s5_prep_bwd(*p["s5"], dar, dai, dbe, dce)
    dpc, dcw_c = _c_bwd(proj, dmix, p["c_cw"])
    (dpd, dcw_d, dp1, dp2, dng), got["d"] = _d_bwd(proj, dmix, r["sall"], r["inv"], p["d_cw"], p["d_p1"],
                                                   p["d_p2"], p["d_ng"], exchanges.get("d"))
    ex_proj = exchanges.get("proj")
    if callable(ex_proj):
        ex_proj = ex_proj(got["d"])
    (dx, dg), got["proj"] = _proj_bwd_x(r["x"], p["norm_g"], dpa, dpb, dpc, dpd, p["wp"], dxo, ex_proj)
    dwa, dwb, dwc, dwd = _dwin(r["h"], dpa, dpb, dpc, dpd)
    dwin = jnp.concatenate([dwa, dwb, dwc, dwd[:, :3 * BR], dwd[:, 4 * BR:4 * BR + 2 * DN_H],
                            dwd[:, 3 * BR:4 * BR]], axis=1)

    def unrows(t, perm):
        return jnp.transpose(t.reshape(S5_H, S5_G, S5_P), perm)

    grads = dict(
        norm_g=dg.reshape(-1), w_in=dwin, a_conv_w=dcw_a[:K_A], a_conv_b=dcb.reshape(-1),
        a_ln_g=dlng.reshape(-1), a_ln_b=dlnb.reshape(-1), a_pw_w=dpw, a_pw_b=dpwb.reshape(-1),
        s5_lambda_re=dlre.reshape(S5_G, S5_P), s5_lambda_im=dlim.reshape(S5_G, S5_P),
        s5_b_re=unrows(dbre, (1, 2, 0)), s5_b_im=unrows(dbim, (1, 2, 0)),
        s5_c_re=unrows(dcre, (1, 0, 2)), s5_c_im=unrows(dcim, (1, 0, 2)),
        s5_d=ddsk.reshape(-1), s5_log_dt=dldt[0, :S5_G], s5_glu_w=dgw, s5_glu_b=dgb.reshape(-1),
        c_conv_w=dcw_c[:K_C], d_conv_w=dcw_d[:K_DN], d_a_log=dp1[0, :DN_H], d_dt_bias=dp2[0, :DN_H],
        d_norm_g=dng.reshape(-1), w_out=dwo)
    return dx, grads, got


def _layer_params(full, wp, wo, l):
    return dict(
        norm_g=_row(full["norm_g"][l]), wp=wp,
        a_cw=_pad_rows(full["a_conv_w"][l], HALO_A), a_cb=_row(full["a_conv_b"][l]),
        a_lng=_row(full["a_ln_g"][l]), a_lnb=_row(full["a_ln_b"][l]), a_pw=full["a_pw_w"][l],
        a_pwb=_row(full["a_pw_b"][l]),
        s5=(_row(full["s5_lambda_re"][l]), _row(full["s5_lambda_im"][l]), _row(full["s5_log_dt"][l], 128),
            jnp.transpose(full["s5_b_re"][l], (2, 0, 1)).reshape(S5_H, NS),
            jnp.transpose(full["s5_b_im"][l], (2, 0, 1)).reshape(S5_H, NS),
            jnp.transpose(full["s5_c_re"][l], (1, 0, 2)).reshape(S5_H, NS),
            jnp.transpose(full["s5_c_im"][l], (1, 0, 2)).reshape(S5_H, NS)),
        s5_d=_row(full["s5_d"][l]), glu_w=full["s5_glu_w"][l], glu_b=_row(full["s5_glu_b"][l]),
        c_cw=_pad_rows(full["c_conv_w"][l], HALO_S), d_cw=_pad_rows(full["d_conv_w"][l], HALO_S),
        d_p1=_row(full["d_a_log"][l], 128), d_p2=_row(full["d_dt_bias"][l], 128),
        d_ng=_row(full["d_norm_g"][l]), wo=wo)


def kernel(x, norm_g, w_in, a_conv_w, a_conv_b, a_ln_g, a_ln_b, a_pw_w, a_pw_b, s5_lambda_re, s5_lambda_im, s5_b_re, s5_b_im, s5_c_re, s5_c_im, s5_d, s5_log_dt, s5_glu_w, s5_glu_b, c_conv_w, d_conv_w, d_a_log, d_dt_bias, d_norm_g, w_out, final_g, loss_target, m_norm_g, m_w_in, m_a_conv_w, m_a_conv_b, m_a_ln_g, m_a_ln_b, m_a_pw_w, m_a_pw_b, m_s5_lambda_re, m_s5_lambda_im, m_s5_b_re, m_s5_b_im, m_s5_c_re, m_s5_c_im, m_s5_d, m_s5_log_dt, m_s5_glu_w, m_s5_glu_b, m_c_conv_w, m_d_conv_w, m_d_a_log, m_d_dt_bias, m_d_norm_g, m_w_out, m_final_g, v_norm_g, v_w_in, v_a_conv_w, v_a_conv_b, v_a_ln_g, v_a_ln_b, v_a_pw_w, v_a_pw_b, v_s5_lambda_re, v_s5_lambda_im, v_s5_b_re, v_s5_b_im, v_s5_c_re, v_s5_c_im, v_s5_d, v_s5_log_dt, v_s5_glu_w, v_s5_glu_b, v_c_conv_w, v_d_conv_w, v_d_a_log, v_d_dt_bias, v_d_norm_g, v_w_out, v_final_g):
    given = dict(locals())
    w = {n: given[n] for n in WEIGHTS}
    m = {n: given["m_" + n] for n in WEIGHTS}
    v = {n: given["v_" + n] for n in WEIGHTS}
    xs, tgt = x[0], loss_target[0]

    n_in, n_out = w["w_in"].shape[2], w["w_out"].shape[1]
    sm_names = [n for n, _ in SHARDED_SMALL]
    sm_shapes = [w[n].shape for n in sm_names]
    sm_rows = _rows_for(sm_shapes, 16)
    win_b, wout_b = w["w_in"].astype(BF), w["w_out"].astype(BF)
    g_in, g_out, g_sm = _exchange([win_b[0], wout_b[0], _pack([w[n] for n in sm_names], sm_rows)],
                                  ("x", "y"), "gather", "gather_first")
    full = dict(w)
    parts = [_unpack(g_sm[j], sm_shapes) for j in range(4)]
    for i, (n, ax) in enumerate(SHARDED_SMALL):
        full[n] = jnp.concatenate([parts[j][i] for j in range(4)], axis=ax)

    saved = []
    h = xs
    for l in range(DEPTH):
        p = _layer_params(full, _permute_in(jnp.concatenate([g_in[j] for j in range(4)], axis=1)),
                          jnp.concatenate([g_out[j] for j in range(4)], axis=0), l)
        nxt = ([win_b[l + 1], wout_b[l + 1]], ("x", "y"), "gather") if l + 1 < DEPTH else None
        h, r, got = _layer_fwd(h, p, nxt)
        saved.append((p, r))
        if got is not None:
            g_in, g_out = got
    loss_tile, dx, dfg = _loss_bwd(h, _row(full["final_g"]), tgt)

    def big_slots(g):
        s_in = jnp.stack([g["w_in"][:, j * n_in:(j + 1) * n_in].astype(BF) for j in range(4)])
        return [s_in.reshape(8, D_MODEL // 2, n_in), g["w_out"].astype(BF).reshape(8, n_out // 2, D_MODEL)]

    def halves(rv):
        return [_sum_slots(rv[0], "sum_w_in")[None], _sum_slots(rv[1], "sum_w_out")[None]]

    layer_grads, summed, pending, arrived = [None] * DEPTH, [None] * DEPTH, None, {}
    for l in reversed(range(DEPTH)):
        p, r = saved[l]
        ex = {}
        if pending is not None:
            ex["d"] = (pending, AXES, "scatter")
        if l + 2 in arrived:
            ex["b"] = (halves(arrived.pop(l + 2)), ("c",), "rows")
        if l == 0:
            ex["proj"] = lambda came: (halves(came), ("c",), "rows")
        dx, layer_grads[l], got = _layer_bwd(dx, p, r, ex)
        if got["b"] is not None:
            summed[l + 2] = got["b"]
        if got["proj"] is not None:
            summed[1] = got["proj"]
        elif got["d"] is not None:
            arrived[l + 1] = got["d"]
        pending = big_slots(layer_grads[l])
    grads = {n: jnp.stack([layer_grads[l][n] for l in range(DEPTH)]) for n in WEIGHTS
             if n not in ("final_g", "w_in", "w_out")}
    grads["final_g"] = dfg.reshape(-1)
    slots = []
    for j in range(4):
        sl = [lax.slice_in_dim(grads[n], j * w[n].shape[ax], (j + 1) * w[n].shape[ax], axis=ax)
              for n, ax in SHARDED_SMALL]
        slots.append(_pack(sl, sm_rows))
    rp_shapes = [w[n].shape for n in REPLICATED] + [(1,)]
    rp_rows = _rows_for(rp_shapes, 64)
    r_in0, r_out0, r_sm, r_rp = _exchange(
        pending + [jnp.stack(slots).reshape(8, sm_rows // 2, LANES),
                   _pack([grads[n] for n in REPLICATED] + [loss_tile[0, 0:1]], rp_rows).reshape(8, rp_rows // 8, LANES)],
        AXES, "scatter", "scatter_last")
    summed[0] = _exchange(halves((r_in0, r_out0)) + [_sum_slots(r_sm, "sum_small")[None]], ("c",), "rows",
                          "gather_halves")
    h_sm = summed[0][2]
    h_in = jnp.concatenate([summed[l][0] for l in range(DEPTH)], axis=0)
    h_out = jnp.concatenate([summed[l][1] for l in range(DEPTH)], axis=0)
    (g_rp,) = _exchange([_sum_slots(r_rp, "sum_replicated")], AXES, "gather", "gather_replicated")
    g_rp = g_rp.reshape(rp_rows, LANES)
    g_sm = h_sm.reshape(sm_rows, LANES)

    out = {}

    def put(name, shape, res):
        for key, t in zip(("delta", "new_m", "new_v"), res):
            out[key + "_" + name] = t.reshape(shape)

    g2 = h_out.reshape(DEPTH * n_out, D_MODEL)
    out["grad_w_out"] = g2.reshape(w["w_out"].shape)
    put("w_out", w["w_out"].shape, _adamw(w["w_out"].reshape(g2.shape), g2, m["w_out"].reshape(g2.shape),
                                          v["w_out"].reshape(g2.shape), "adamw_w_out"))
    cm = lambda a: jnp.transpose(a, (2, 0, 1))
    rm = lambda a: jnp.transpose(a, (1, 2, 0))
    g3 = cm(h_in)
    out["grad_w_in"] = rm(g3)
    for key, t in zip(("delta", "new_m", "new_v"),
                      _adamw(cm(w["w_in"]), g3, cm(m["w_in"]), cm(v["w_in"]), "adamw_w_in", lead=n_in // 6)):
        out[key + "_w_in"] = rm(t)
    zero = jnp.zeros((1,), F32)
    res_sm = _adamw(_pack([w[n] for n in sm_names], sm_rows), g_sm, _pack([m[n] for n in sm_names], sm_rows),
                    _pack([v[n] for n in sm_names], sm_rows), "adamw_small")
    res_rp = _adamw(_pack([w[n] for n in REPLICATED] + [zero], rp_rows), g_rp,
                    _pack([m[n] for n in REPLICATED] + [zero], rp_rows),
                    _pack([v[n] for n in REPLICATED] + [zero], rp_rows), "adamw_replicated")
    for key, sm, rp in (("grad", g_sm, g_rp), ("delta", res_sm[0], res_rp[0]), ("new_m", res_sm[1], res_rp[1]),
                        ("new_v", res_sm[2], res_rp[2])):
        for n, t in zip(sm_names, _unpack(sm, sm_shapes)):
            out[key + "_" + n] = t
        for n, t in zip(REPLICATED, _unpack(rp, rp_shapes[:-1])):
            out[key + "_" + n] = t
    loss = _unpack(g_rp, rp_shapes)[-1].reshape(())
    return (loss, dx[None], *[out["grad_" + n] for n in WEIGHTS], *[out["delta_" + n] for n in WEIGHTS],
            *[out["new_m_" + n] for n in WEIGHTS], *[out["new_v_" + n] for n in WEIGHTS])
```

```python
import functools

import jax
import jax.numpy as jnp
from jax import lax
from jax.experimental import pallas as pl
from jax.experimental.pallas import tpu as pltpu

F32, BF = jnp.float32, jnp.bfloat16
HI = lax.Precision.HIGHEST
MESH = pl.DeviceIdType.MESH

D_MODEL = 1024
BR = 256
DEPTH = 4
N_IN = 3336
N_INP = 3456
COL_A, COL_B, COL_C, COL_D = 0, 768, 1280, 2304
W_A, W_B, W_C, W_D = 768, 512, 1024, 1152
S5_G, S5_H, S5_P = 16, 16, 64
NS = S5_G * S5_P
DN_H, DN_D, DN_C = 4, 64, 64
DN_G_FWD, DN_G_BWD = 2, 2
DN_P = DN_H // 2
DN_STATE = (DN_P, 2 * DN_D, 2 * DN_D)
DN_INV = (DN_P, DN_C, 2 * DN_D)
K_A, K_C, K_DN = 31, 3, 4
HALO_A, HALO_S = 32, 8
EPS = 1e-6
TL = 256
VMEM_LIMIT = 56 * 1024 * 1024

ADAM_LR, ADAM_B1, ADAM_B2, ADAM_EPS, ADAM_WD, ADAM_STEP = 0.001, 0.9, 0.999, 1e-08, 0.01, 10


def _cp(*sem):
    return pltpu.CompilerParams(dimension_semantics=sem, vmem_limit_bytes=VMEM_LIMIT)


def _sigmoid(x):
    return jax.nn.sigmoid(x)


def _silu(x):
    return x * jax.nn.sigmoid(x)


def _rmsnorm(x, g):
    return x * lax.rsqrt(jnp.mean(x * x, axis=-1, keepdims=True) + EPS) * g


@jax.custom_vjp
def _mm(a, w):
    return jnp.dot(a.astype(BF), w.astype(BF), preferred_element_type=F32)


def _mm_f(a, w):
    return _mm(a, w), (a, w)


def _mm_b(res, g):
    a, w = res
    gb = g.astype(BF)
    da = lax.dot_general(gb, w.astype(BF), (((1,), (1,)), ((), ())), preferred_element_type=F32)
    dw = lax.dot_general(a.astype(BF), gb, (((0,), (0,)), ((), ())), preferred_element_type=F32)
    return da, dw


_mm.defvjp(_mm_f, _mm_b)


@jax.custom_vjp
def _mm_nt(a, b):
    return lax.dot_general(a.astype(BF), b.astype(BF), (((1,), (1,)), ((), ())), preferred_element_type=F32)


def _mm_nt_f(a, b):
    return _mm_nt(a, b), (a, b)


def _mm_nt_b(res, g):
    a, b = res
    gb = g.astype(BF)
    da = jnp.dot(gb, b.astype(BF), preferred_element_type=F32)
    db = lax.dot_general(gb, a.astype(BF), (((0,), (0,)), ((), ())), preferred_element_type=F32)
    return da, db


_mm_nt.defvjp(_mm_nt_f, _mm_nt_b)


@jax.custom_vjp
def _mm_tn(a, b):
    return lax.dot_general(a.astype(BF), b.astype(BF), (((0,), (0,)), ((), ())), preferred_element_type=F32)


def _mm_tn_f(a, b):
    return _mm_tn(a, b), (a, b)


def _mm_tn_b(res, g):
    a, b = res
    gb = g.astype(BF)
    da = lax.dot_general(b.astype(BF), gb, (((1,), (1,)), ((), ())), preferred_element_type=F32)
    db = jnp.dot(a.astype(BF), gb, preferred_element_type=F32)
    return da, db


_mm_tn.defvjp(_mm_tn_f, _mm_tn_b)


def _dot_hi(a, b):
    return jnp.dot(a, b, precision=HI, preferred_element_type=F32)


def _split(a):
    hi = a.astype(BF)
    return hi, (a - hi.astype(F32)).astype(BF)


def _dot3(a, b, dims=(((1,), (0,)), ((), ()))):
    ah, al = _split(a)
    bh, bl = _split(b)
    d = functools.partial(lax.dot_general, dimension_numbers=dims, preferred_element_type=F32)
    return d(ah, bh) + d(ah, bl) + d(al, bh)


@jax.custom_vjp
def _mm3(a, b):
    return _dot3(a, b)


def _mm3_f(a, b):
    return _dot3(a, b), (a, b)


def _mm3_b(res, g):
    a, b = res
    return _dot3(g, b, (((1,), (1,)), ((), ()))), _dot3(a, g, (((0,), (0,)), ((), ())))


_mm3.defvjp(_mm3_f, _mm3_b)


def _roll(x, s):
    n = x.shape[0]
    s = s % n
    return x if s == 0 else pltpu.roll(x, s, 0)


def _conv_taps(ext, w_ref, halo, k_taps, tl):
    acc = None
    for k in range(k_taps):
        term = _roll(ext, (k_taps - 1) - k)[halo:halo + tl] * w_ref[k:k + 1, :]
        acc = term if acc is None else acc + term
    return acc


def _conv_taps_bwd(ext, w_ref, dw_ref, dacc, halo, k_taps, tl):
    dpad = jnp.concatenate([dacc, jnp.zeros((halo, dacc.shape[1]), F32)], axis=0)
    dext = None
    for k in range(k_taps):
        r = _roll(ext, (k_taps - 1) - k)[halo:halo + tl]
        dw_ref[k:k + 1, :] += jnp.sum(r * dacc, axis=0, keepdims=True)
        term = _roll(dpad, halo - (k_taps - 1) + k) * w_ref[k:k + 1, :]
        dext = term if dext is None else dext + term
    return dext


def _add_tail(x, tail):
    tl, h = x.shape[0], tail.shape[0]
    return x + jnp.concatenate([jnp.zeros((tl - h, x.shape[1]), F32), tail], axis=0)


def _proj_fwd(x, g, wp):
    L = x.shape[0]

    def body(x_ref, g_ref, w_ref, p_ref, h_ref):
        hb = _rmsnorm(x_ref[...], g_ref[...]).astype(BF)
        h_ref[...] = hb
        p_ref[...] = jnp.dot(hb, w_ref[...], preferred_element_type=F32)

    return pl.pallas_call(
        body, grid=(L // TL,),
        in_specs=[pl.BlockSpec((TL, D_MODEL), lambda i: (i, 0)),
                  pl.BlockSpec((1, D_MODEL), lambda i: (0, 0)),
                  pl.BlockSpec((D_MODEL, N_INP), lambda i: (0, 0))],
        out_specs=[pl.BlockSpec((TL, N_INP), lambda i: (i, 0)),
                   pl.BlockSpec((TL, D_MODEL), lambda i: (i, 0))],
        out_shape=[jax.ShapeDtypeStruct((L, N_INP), F32), jax.ShapeDtypeStruct((L, D_MODEL), BF)],
        name="proj_fwd", compiler_params=_cp("parallel"))(x, g, wp)


def _proj_bwd_x(x, g, dpa, dpb, dpc, dpd, wp, dxo, exchange=None):
    L = x.shape[0]

    def body(x_ref, g_ref, a_ref, b_ref, c_ref, d_ref, w_ref, dxo_ref, dx_ref, dg_ref):
        dh = None
        for ref, c0, wd in ((a_ref, COL_A, W_A), (b_ref, COL_B, W_B), (c_ref, COL_C, W_C), (d_ref, COL_D, W_D)):
            t = lax.dot_general(ref[...].astype(BF), w_ref[:, c0:c0 + wd], (((1,), (1,)), ((), ())),
                                preferred_element_type=F32)
            dh = t if dh is None else dh + t
        _, vj = jax.vjp(_rmsnorm, x_ref[...], g_ref[...])
        dx, dg = vj(dh)
        dx_ref[...] = dxo_ref[...] + dx

        @pl.when(pl.program_id(0) == 0)
        def _():
            dg_ref[...] = jnp.zeros_like(dg_ref)

        dg_ref[...] += dg

    def rows(wd):
        return pl.BlockSpec((TL, wd), lambda i: (i, 0))

    return _sweep_with_exchange(
        body, L // TL,
        in_specs=[rows(D_MODEL), pl.BlockSpec((1, D_MODEL), lambda i: (0, 0)),
                  rows(W_A), rows(W_B), rows(W_C), rows(W_D),
                  pl.BlockSpec((D_MODEL, N_INP), lambda i: (0, 0)), rows(D_MODEL)],
        out_specs=[rows(D_MODEL), pl.BlockSpec((1, D_MODEL), lambda i: (0, 0))],
        out_shape=[jax.ShapeDtypeStruct((L, D_MODEL), F32), jax.ShapeDtypeStruct((1, D_MODEL), F32)],
        scratch_shapes=[], args=(x, g, dpa, dpb, dpc, dpd, wp, dxo), exchange=exchange, name="proj_bwd_x")


def _dwin(h, dpa, dpb, dpc, dpd):
    L = h.shape[0]

    def body(h_ref, a_ref, b_ref, c_ref, d_ref, oa_ref, ob_ref, oc_ref, od_ref):
        outs = (oa_ref, ob_ref, oc_ref, od_ref)

        @pl.when(pl.program_id(0) == 0)
        def _():
            for o in outs:
                o[...] = jnp.zeros_like(o)

        ht = h_ref[...].T
        for ref, o in zip((a_ref, b_ref, c_ref, d_ref), outs):
            o[...] += jnp.dot(ht, ref[...].astype(BF), preferred_element_type=F32)

    def rows(wd):
        return pl.BlockSpec((TL, wd), lambda i: (i, 0))

    def whole(wd):
        return pl.BlockSpec((D_MODEL, wd), lambda i: (0, 0))

    widths = (W_A, W_B, W_C, W_D)
    return pl.pallas_call(
        body, grid=(L // TL,),
        in_specs=[rows(D_MODEL)] + [rows(wd) for wd in widths],
        out_specs=[whole(wd) for wd in widths],
        out_shape=[jax.ShapeDtypeStruct((D_MODEL, wd), F32) for wd in widths],
        name="dwin", compiler_params=_cp("arbitrary"))(h, dpa, dpb, dpc, dpd)


def _dwout(ya, yb, yc, yd, dxo):
    L = dxo.shape[0]
    tk, tn = min(512, L), 512

    def body(a_ref, b_ref, c_ref, d_ref, g_ref, o_ref):
        @pl.when(pl.program_id(1) == 0)
        def _():
            o_ref[...] = jnp.zeros_like(o_ref)

        gb = g_ref[...].astype(BF)
        for j, ref in enumerate((a_ref, b_ref, c_ref, d_ref)):
            o_ref[j * BR:(j + 1) * BR, :] += lax.dot_general(ref[...].astype(BF), gb, (((0,), (0,)), ((), ())),
                                                             preferred_element_type=F32)

    ys = pl.BlockSpec((tk, BR), lambda j, t: (t, 0))
    return pl.pallas_call(
        body, grid=(D_MODEL // tn, L // tk),
        in_specs=[ys, ys, ys, ys, pl.BlockSpec((tk, tn), lambda j, t: (t, j))],
        out_specs=pl.BlockSpec((D_MODEL, tn), lambda j, t: (0, j)),
        out_shape=jax.ShapeDtypeStruct((D_MODEL, D_MODEL), F32),
        name="dwout", compiler_params=_cp("parallel", "arbitrary"))(ya, yb, yc, yd, dxo)


def _out_fwd(x, ya, yb, yc, yd, wo):
    L = x.shape[0]

    def body(x_ref, a_ref, b_ref, c_ref, d_ref, w_ref, o_ref):
        acc = x_ref[...]
        for j, ref in enumerate((a_ref, b_ref, c_ref, d_ref)):
            acc = acc + jnp.dot(ref[...].astype(BF), w_ref[j * BR:(j + 1) * BR, :], preferred_element_type=F32)
        o_ref[...] = acc

    def rows(wd):
        return pl.BlockSpec((TL, wd), lambda i: (i, 0))

    return pl.pallas_call(
        body, grid=(L // TL,),
        in_specs=[rows(D_MODEL), rows(BR), rows(BR), rows(BR), rows(BR),
                  pl.BlockSpec((D_MODEL, D_MODEL), lambda i: (0, 0))],
        out_specs=rows(D_MODEL), out_shape=jax.ShapeDtypeStruct((L, D_MODEL), F32),
        name="out_fwd", compiler_params=_cp("parallel"))(x, ya, yb, yc, yd, wo)


def _out_bwd_x(dxo, wo):
    L = dxo.shape[0]

    def body(d_ref, w_ref, o_ref):
        o_ref[...] = lax.dot_general(d_ref[...].astype(BF), w_ref[...], (((1,), (1,)), ((), ())),
                                     preferred_element_type=F32)

    return pl.pallas_call(
        body, grid=(L // TL,),
        in_specs=[pl.BlockSpec((TL, D_MODEL), lambda i: (i, 0)), pl.BlockSpec((D_MODEL, D_MODEL), lambda i: (0, 0))],
        out_specs=pl.BlockSpec((TL, D_MODEL), lambda i: (i, 0)),
        out_shape=jax.ShapeDtypeStruct((L, D_MODEL), F32),
        name="out_bwd_x", compiler_params=_cp("parallel"))(dxo, wo)


def _loss_bwd(x, g, tgt):
    L = x.shape[0]

    def f(xv, gv, tv):
        err = _rmsnorm(xv, gv) - tv
        return 0.5 * jnp.sum(jnp.mean(err * err, axis=-1, keepdims=True), axis=0, keepdims=True)

    def body(x_ref, g_ref, t_ref, loss_ref, dx_ref, dg_ref):
        tv = t_ref[...]
        loss, vj = jax.vjp(lambda a, b: f(a, b, tv), x_ref[...], g_ref[...])
        dx, dg = vj(jnp.ones((1, 1), F32))
        dx_ref[...] = dx

        @pl.when(pl.program_id(0) == 0)
        def _():
            dg_ref[...] = jnp.zeros_like(dg_ref)
            loss_ref[...] = jnp.zeros_like(loss_ref)

        dg_ref[...] += dg
        loss_ref[...] += jnp.broadcast_to(loss, loss_ref.shape)

    return pl.pallas_call(
        body, grid=(L // TL,),
        in_specs=[pl.BlockSpec((TL, D_MODEL), lambda i: (i, 0)), pl.BlockSpec((1, D_MODEL), lambda i: (0, 0)),
                  pl.BlockSpec((TL, D_MODEL), lambda i: (i, 0))],
        out_specs=[pl.BlockSpec((8, 128), lambda i: (0, 0)), pl.BlockSpec((TL, D_MODEL), lambda i: (i, 0)),
                   pl.BlockSpec((1, D_MODEL), lambda i: (0, 0))],
        out_shape=[jax.ShapeDtypeStruct((8, 128), F32), jax.ShapeDtypeStruct((L, D_MODEL), F32),
                   jax.ShapeDtypeStruct((1, D_MODEL), F32)],
        name="loss_bwd", compiler_params=_cp("arbitrary"))(x, g, tgt)


def _a_pre(val, gate):
    return val * _sigmoid(gate)


def _a_post(acc, az, cb, lng, lnb, pw, pwb):
    t = acc + cb
    mu = jnp.mean(t, axis=-1, keepdims=True)
    xc = t - mu
    ln = xc * lax.rsqrt(jnp.mean(xc * xc, axis=-1, keepdims=True) + EPS) * lng + lnb
    return (_mm(_silu(ln), pw) + pwb) * _silu(az)


def _halo_map(tl, halo, col):
    r = tl // halo
    return lambda i: (jnp.maximum(i * r - 1, 0), col)


def _a_fwd(proj, cw, cb, lng, lnb, pw, pwb):
    L = proj.shape[0]

    def body(vg_ref, az_ref, hvg_ref, cw_ref, cb_ref, lng_ref, lnb_ref, pw_ref, pwb_ref, o_ref):
        keep = (pl.program_id(0) > 0).astype(F32)
        a_h = _a_pre(hvg_ref[:, 0:BR], hvg_ref[:, BR:2 * BR]) * keep
        a_t = _a_pre(vg_ref[:, 0:BR], vg_ref[:, BR:2 * BR])
        ext = jnp.concatenate([a_h, a_t], axis=0)
        acc = _conv_taps(ext, cw_ref, HALO_A, K_A, TL)
        o_ref[...] = _a_post(acc, az_ref[...], cb_ref[...], lng_ref[...], lnb_ref[...], pw_ref[...], pwb_ref[...])

    vec = pl.BlockSpec((1, BR), lambda i: (0, 0))
    return pl.pallas_call(
        body, grid=(L // TL,),
        in_specs=[pl.BlockSpec((TL, 2 * BR), lambda i: (i, 0)), pl.BlockSpec((TL, BR), lambda i: (i, 2)),
                  pl.BlockSpec((HALO_A, 2 * BR), _halo_map(TL, HALO_A, 0)),
                  pl.BlockSpec((HALO_A, BR), lambda i: (0, 0)), vec, vec, vec,
                  pl.BlockSpec((BR, BR), lambda i: (0, 0)), vec],
        out_specs=pl.BlockSpec((TL, BR), lambda i: (i, 0)),
        out_shape=jax.ShapeDtypeStruct((L, BR), F32),
        name="a_fwd", compiler_params=_cp("parallel"))(proj, proj, proj, cw, cb, lng, lnb, pw, pwb)


def _a_bwd(proj, dmix, cw, cb, lng, lnb, pw, pwb):
    L = proj.shape[0]
    n = L // TL

    def body(vg_ref, az_ref, hvg_ref, dy_ref, cw_ref, cb_ref, lng_ref, lnb_ref, pw_ref, pwb_ref,
             dp_ref, dcw_ref, dcb_ref, dlng_ref, dlnb_ref, dpw_ref, dpwb_ref, carry_ref):
        i = pl.program_id(0)

        @pl.when(i == 0)
        def _():
            carry_ref[...] = jnp.zeros_like(carry_ref)
            for r in (dcw_ref, dcb_ref, dlng_ref, dlnb_ref, dpw_ref, dpwb_ref):
                r[...] = jnp.zeros_like(r)

        keep = (i < n - 1).astype(F32)
        val, gate = vg_ref[:, 0:BR], vg_ref[:, BR:2 * BR]
        a_h = _a_pre(hvg_ref[:, 0:BR], hvg_ref[:, BR:2 * BR]) * keep
        a_t, vj_pre = jax.vjp(_a_pre, val, gate)
        ext = jnp.concatenate([a_h, a_t], axis=0)
        acc = _conv_taps(ext, cw_ref, HALO_A, K_A, TL)
        _, vj_post = jax.vjp(_a_post, acc, az_ref[...], cb_ref[...], lng_ref[...], lnb_ref[...], pw_ref[...],
                             pwb_ref[...])
        dacc, daz, dcb, dlng, dlnb, dpw, dpwb = vj_post(dy_ref[...])
        dext = _conv_taps_bwd(ext, cw_ref, dcw_ref, dacc, HALO_A, K_A, TL)
        da = _add_tail(dext[HALO_A:], carry_ref[...])
        carry_ref[...] = dext[:HALO_A]
        dval, dgate = vj_pre(da)
        dp_ref[:, 0:BR] = dval
        dp_ref[:, BR:2 * BR] = dgate
        dp_ref[:, 2 * BR:3 * BR] = daz
        dcb_ref[...] += dcb
        dlng_ref[...] += dlng
        dlnb_ref[...] += dlnb
        dpw_ref[...] += dpw
        dpwb_ref[...] += dpwb

    rev = lambda i: n - 1 - i
    vec = pl.BlockSpec((1, BR), lambda i: (0, 0))
    hmap = _halo_map(TL, HALO_A, 0)
    return pl.pallas_call(
        body, grid=(n,),
        in_specs=[pl.BlockSpec((TL, 2 * BR), lambda i: (rev(i), 0)), pl.BlockSpec((TL, BR), lambda i: (rev(i), 2)),
                  pl.BlockSpec((HALO_A, 2 * BR), lambda i: hmap(rev(i))),
                  pl.BlockSpec((TL, BR), lambda i: (rev(i), 0)),
                  pl.BlockSpec((HALO_A, BR), lambda i: (0, 0)), vec, vec, vec,
                  pl.BlockSpec((BR, BR), lambda i: (0, 0)), vec],
        out_specs=[pl.BlockSpec((TL, W_A), lambda i: (rev(i), 0)),
                   pl.BlockSpec((HALO_A, BR), lambda i: (0, 0)), vec, vec, vec,
                   pl.BlockSpec((BR, BR), lambda i: (0, 0)), vec],
        out_shape=[jax.ShapeDtypeStruct((L, W_A), F32), jax.ShapeDtypeStruct((HALO_A, BR), F32)]
        + [jax.ShapeDtypeStruct((1, BR), F32)] * 3
        + [jax.ShapeDtypeStruct((BR, BR), F32), jax.ShapeDtypeStruct((1, BR), F32)],
        scratch_shapes=[pltpu.VMEM((HALO_A, BR), F32)],
        name="a_bwd", compiler_params=_cp("arbitrary"))(proj, proj, proj, dmix, cw, cb, lng, lnb, pw, pwb)


def _c_pre(cg, xc):
    return cg * xc


def _c_post(acc, bg, cz):
    return bg * acc * _silu(cz)


def _c_fwd(proj, cw):
    L = proj.shape[0]

    def body(bg_ref, cx_ref, cz_ref, hcx_ref, cw_ref, o_ref):
        keep = (pl.program_id(0) > 0).astype(F32)
        p_h = _c_pre(hcx_ref[:, 0:BR], hcx_ref[:, BR:2 * BR]) * keep
        p_t = _c_pre(cx_ref[:, 0:BR], cx_ref[:, BR:2 * BR])
        ext = jnp.concatenate([p_h, p_t], axis=0)
        acc = _conv_taps(ext, cw_ref, HALO_S, K_C, TL)
        o_ref[...] = _c_post(acc, bg_ref[...], cz_ref[...])

    return pl.pallas_call(
        body, grid=(L // TL,),
        in_specs=[pl.BlockSpec((TL, BR), lambda i: (i, 5)), pl.BlockSpec((TL, 2 * BR), lambda i: (i, 3)),
                  pl.BlockSpec((TL, BR), lambda i: (i, 8)),
                  pl.BlockSpec((HALO_S, 2 * BR), _halo_map(TL, HALO_S, 3)),
                  pl.BlockSpec((HALO_S, BR), lambda i: (0, 0))],
        out_specs=pl.BlockSpec((TL, BR), lambda i: (i, 0)),
        out_shape=jax.ShapeDtypeStruct((L, BR), F32),
        name="c_fwd", compiler_params=_cp("parallel"))(proj, proj, proj, proj, cw)


def _c_bwd(proj, dmix, cw):
    L = proj.shape[0]
    n = L // TL

    def body(bg_ref, cx_ref, cz_ref, hcx_ref, dy_ref, cw_ref, dp_ref, dcw_ref, carry_ref):
        i = pl.program_id(0)

        @pl.when(i == 0)
        def _():
            carry_ref[...] = jnp.zeros_like(carry_ref)
            dcw_ref[...] = jnp.zeros_like(dcw_ref)

        keep = (i < n - 1).astype(F32)
        p_h = _c_pre(hcx_ref[:, 0:BR], hcx_ref[:, BR:2 * BR]) * keep
        p_t, vj_pre = jax.vjp(_c_pre, cx_ref[:, 0:BR], cx_ref[:, BR:2 * BR])
        ext = jnp.concatenate([p_h, p_t], axis=0)
        acc = _conv_taps(ext, cw_ref, HALO_S, K_C, TL)
        _, vj_post = jax.vjp(_c_post, acc, bg_ref[...], cz_ref[...])
        dacc, dbg, dcz = vj_post(dy_ref[...])
        dext = _conv_taps_bwd(ext, cw_ref, dcw_ref, dacc, HALO_S, K_C, TL)
        dp = _add_tail(dext[HALO_S:], carry_ref[...])
        carry_ref[...] = dext[:HALO_S]
        dcg, dxc = vj_pre(dp)
        dp_ref[:, 0:BR] = dbg
        dp_ref[:, BR:2 * BR] = dcg
        dp_ref[:, 2 * BR:3 * BR] = dxc
        dp_ref[:, 3 * BR:4 * BR] = dcz

    rev = lambda i: n - 1 - i
    hmap = _halo_map(TL, HALO_S, 3)
    return pl.pallas_call(
        body, grid=(n,),
        in_specs=[pl.BlockSpec((TL, BR), lambda i: (rev(i), 5)), pl.BlockSpec((TL, 2 * BR), lambda i: (rev(i), 3)),
                  pl.BlockSpec((TL, BR), lambda i: (rev(i), 8)),
                  pl.BlockSpec((HALO_S, 2 * BR), lambda i: hmap(rev(i))),
                  pl.BlockSpec((TL, BR), lambda i: (rev(i), 2)),
                  pl.BlockSpec((HALO_S, BR), lambda i: (0, 0))],
        out_specs=[pl.BlockSpec((TL, W_C), lambda i: (rev(i), 0)), pl.BlockSpec((HALO_S, BR), lambda i: (0, 0))],
        out_shape=[jax.ShapeDtypeStruct((L, W_C), F32), jax.ShapeDtypeStruct((HALO_S, BR), F32)],
        scratch_shapes=[pltpu.VMEM((HALO_S, BR), F32)],
        name="c_bwd", compiler_params=_cp("arbitrary"))(proj, proj, proj, proj, dmix, cw)


def _s5_prep_fn(lre, lim, ldt, bre, bim, cre, cim):
    grp = lax.broadcasted_iota(jnp.int32, (128, NS), 0)
    lane = lax.broadcasted_iota(jnp.int32, (128, NS), 1)
    expand = (grp == lane // S5_P).astype(F32)
    dt = jnp.exp(_dot_hi(jnp.broadcast_to(ldt, (8, 128)), expand)[0:1])
    lr = jnp.minimum(lre, -1e-4)
    mag = jnp.exp(lr * dt)
    ar = mag * jnp.cos(lim * dt)
    ai = mag * jnp.sin(lim * dt)
    den = lr * lr + lim * lim
    fr = ((ar - 1.0) * lr + ai * lim) / den
    fi = (ai * lr - (ar - 1.0) * lim) / den
    bbr = fr * bre - fi * bim
    bbi = fr * bim + fi * bre
    row = lax.broadcasted_iota(jnp.int32, (BR, NS), 0)
    col = lax.broadcasted_iota(jnp.int32, (BR, NS), 1)
    blk = (row // S5_H == col // S5_P).astype(F32)

    def embed(t):
        return jnp.concatenate([t] * S5_G, axis=0) * blk

    bemb = jnp.concatenate([embed(bbr), embed(bbi)], axis=1)
    cemb = jnp.concatenate([embed(cre), embed(-cim)], axis=1)
    return ar, ai, bemb, cemb


def _s5_prep(lre, lim, ldt, bre, bim, cre, cim):
    def body(*refs):
        outs = _s5_prep_fn(*[r[...] for r in refs[:7]])
        for r, o in zip(refs[7:], outs):
            r[...] = o

    return pl.pallas_call(
        body,
        out_shape=[jax.ShapeDtypeStruct((1, NS), F32)] * 2 + [jax.ShapeDtypeStruct((BR, 2 * NS), F32)] * 2,
        name="s5_prep", compiler_params=pltpu.CompilerParams(vmem_limit_bytes=VMEM_LIMIT),
    )(lre, lim, ldt, bre, bim, cre, cim)


def _s5_prep_bwd(lre, lim, ldt, bre, bim, cre, cim, dar, dai, dbemb, dcemb):
    def body(*refs):
        _, vj = jax.vjp(_s5_prep_fn, *[r[...] for r in refs[:7]])
        grads = vj(tuple(r[...] for r in refs[7:11]))
        for r, o in zip(refs[11:], grads):
            r[...] = o

    return pl.pallas_call(
        body,
        out_shape=[jax.ShapeDtypeStruct((1, NS), F32)] * 2 + [jax.ShapeDtypeStruct((1, 128), F32)]
        + [jax.ShapeDtypeStruct((S5_H, NS), F32)] * 4,
        name="s5_prep_bwd", compiler_params=pltpu.CompilerParams(vmem_limit_bytes=VMEM_LIMIT),
    )(lre, lim, ldt, bre, bim, cre, cim, dar, dai, dbemb, dcemb)


def _s5_scan(xr, xi, ar, ai, reverse):
    n = xr.shape[0]
    row = lax.broadcasted_iota(jnp.int32, (n, 1), 0)
    pr, pi = ar, ai
    d = 1
    while d < n:
        if d % 8:
            if reverse:
                m = row < n - d
                sr = jnp.where(m, _roll(xr, n - d), 0.0)
                si = jnp.where(m, _roll(xi, n - d), 0.0)
            else:
                m = row >= d
                sr = jnp.where(m, _roll(xr, d), 0.0)
                si = jnp.where(m, _roll(xi, d), 0.0)
            xr, xi = xr + pr * sr - pi * si, xi + pr * si + pi * sr
        elif reverse:
            sr, si = xr[d:], xi[d:]
            xr, xi = (jnp.concatenate([xr[:n - d] + pr * sr - pi * si, xr[n - d:]], axis=0),
                      jnp.concatenate([xi[:n - d] + pr * si + pi * sr, xi[n - d:]], axis=0))
        else:
            sr, si = xr[:n - d], xi[:n - d]
            xr, xi = (jnp.concatenate([xr[:d], xr[d:] + pr * sr - pi * si], axis=0),
                      jnp.concatenate([xi[:d], xi[d:] + pr * si + pi * sr], axis=0))
        pr, pi = pr * pr - pi * pi, 2.0 * pr * pi
        d *= 2
    return xr, xi


def _s5_states(u, bemb_b, ar, ai, sin_r, sin_i):
    bu = jnp.dot(u.astype(BF), bemb_b, preferred_element_type=F32)
    first = lax.broadcasted_iota(jnp.int32, (u.shape[0], 1), 0) == 0
    xr = bu[:, :NS] + jnp.where(first, ar * sin_r - ai * sin_i, 0.0)
    xi = bu[:, NS:] + jnp.where(first, ar * sin_i + ai * sin_r, 0.0)
    return _s5_scan(xr, xi, ar, ai, False)


def _b_post(yssm, u, bz, dsk, gw, gb):
    z = jax.nn.gelu(yssm + dsk * u)
    return z * _sigmoid(_mm(z, gw) + gb) * _silu(bz)


def _b_fwd(proj, ar, ai, bemb, cemb, dsk, gw, gb):
    L = proj.shape[0]
    n = L // TL

    def body(u_ref, bz_ref, ar_ref, ai_ref, be_ref, ce_ref, dsk_ref, gw_ref, gb_ref, o_ref, sin_ref, carry_ref):
        @pl.when(pl.program_id(0) == 0)
        def _():
            carry_ref[...] = jnp.zeros_like(carry_ref)

        sin = carry_ref[...]
        sin_ref[0] = sin
        u = u_ref[...]
        sr, si = _s5_states(u, be_ref[...].astype(BF), ar_ref[...], ai_ref[...], sin[:, :NS], sin[:, NS:])
        carry_ref[:, :NS] = sr[TL - 1:TL]
        carry_ref[:, NS:] = si[TL - 1:TL]
        s = jnp.concatenate([sr, si], axis=1).astype(BF)
        yssm = lax.dot_general(s, ce_ref[...].astype(BF), (((1,), (1,)), ((), ())), preferred_element_type=F32)
        o_ref[...] = _b_post(yssm, u, bz_ref[...], dsk_ref[...], gw_ref[...], gb_ref[...])

    vec = pl.BlockSpec((1, BR), lambda i: (0, 0))
    svec = pl.BlockSpec((1, NS), lambda i: (0, 0))
    emb = pl.BlockSpec((BR, 2 * NS), lambda i: (0, 0))
    return pl.pallas_call(
        body, grid=(n,),
        in_specs=[pl.BlockSpec((TL, BR), lambda i: (i, 3)), pl.BlockSpec((TL, BR), lambda i: (i, 4)),
                  svec, svec, emb, emb, vec, pl.BlockSpec((BR, BR), lambda i: (0, 0)), vec],
        out_specs=[pl.BlockSpec((TL, BR), lambda i: (i, 0)), pl.BlockSpec((1, 1, 2 * NS), lambda i: (i, 0, 0))],
        out_shape=[jax.ShapeDtypeStruct((L, BR), F32), jax.ShapeDtypeStruct((n, 1, 2 * NS), F32)],
        scratch_shapes=[pltpu.VMEM((1, 2 * NS), F32)],
        name="b_fwd", compiler_params=_cp("arbitrary"))(proj, proj, ar, ai, bemb, cemb, dsk, gw, gb)


def _b_bwd(proj, dmix, sin_all, ar, ai, bemb, cemb, dsk, gw, gb, exchange=None):
    L = proj.shape[0]
    n = L // TL

    def body(u_ref, bz_ref, dy_ref, sin_ref, ar_ref, ai_ref, be_ref, ce_ref, dsk_ref, gw_ref, gb_ref,
             dp_ref, dar_ref, dai_ref, dbe_ref, dce_ref, ddsk_ref, dgw_ref, dgb_ref, carry_ref):
        i = pl.program_id(0)

        @pl.when(i == 0)
        def _():
            carry_ref[...] = jnp.zeros_like(carry_ref)
            for r in (dar_ref, dai_ref, dbe_ref, dce_ref, ddsk_ref, dgw_ref, dgb_ref):
                r[...] = jnp.zeros_like(r)

        u = u_ref[...]
        ar, ai = ar_ref[...], ai_ref[...]
        be_b, ce_b = be_ref[...].astype(BF), ce_ref[...].astype(BF)
        sin = sin_ref[0]
        sr, si = _s5_states(u, be_b, ar, ai, sin[:, :NS], sin[:, NS:])
        s_b = jnp.concatenate([sr, si], axis=1).astype(BF)
        yssm = lax.dot_general(s_b, ce_b, (((1,), (1,)), ((), ())), preferred_element_type=F32)
        _, vj = jax.vjp(_b_post, yssm, u, bz_ref[...], dsk_ref[...], gw_ref[...], gb_ref[...])
        dyssm, du, dbz, ddsk, dgw, dgb = vj(dy_ref[...])
        dy_b = dyssm.astype(BF)
        dce_ref[...] += lax.dot_general(dy_b, s_b, (((0,), (0,)), ((), ())), preferred_element_type=F32)
        gs = jnp.dot(dy_b, ce_b, preferred_element_type=F32)
        last = lax.broadcasted_iota(jnp.int32, (TL, 1), 0) == TL - 1
        cr, ci = carry_ref[:, :NS], carry_ref[:, NS:]
        gr = gs[:, :NS] + jnp.where(last, ar * cr + ai * ci, 0.0)
        gi = gs[:, NS:] + jnp.where(last, ar * ci - ai * cr, 0.0)
        dsr, dsi = _s5_scan(gr, gi, ar, -ai, True)
        carry_ref[:, :NS] = dsr[0:1]
        carry_ref[:, NS:] = dsi[0:1]
        first = lax.broadcasted_iota(jnp.int32, (TL, 1), 0) == 0
        pr = jnp.where(first, sin[:, :NS], _roll(sr, 1))
        pi = jnp.where(first, sin[:, NS:], _roll(si, 1))
        dar_ref[...] += jnp.sum(dsr * pr + dsi * pi, axis=0, keepdims=True)
        dai_ref[...] += jnp.sum(dsi * pr - dsr * pi, axis=0, keepdims=True)
        ds_b = jnp.concatenate([dsr, dsi], axis=1).astype(BF)
        dbe_ref[...] += lax.dot_general(u.astype(BF), ds_b, (((0,), (0,)), ((), ())), preferred_element_type=F32)
        du = du + lax.dot_general(ds_b, be_b, (((1,), (1,)), ((), ())), preferred_element_type=F32)
        dp_ref[:, 0:BR] = du
        dp_ref[:, BR:2 * BR] = dbz
        ddsk_ref[...] += ddsk
        dgw_ref[...] += dgw
        dgb_ref[...] += dgb

    rev = lambda i: n - 1 - i
    vec = pl.BlockSpec((1, BR), lambda i: (0, 0))
    svec = pl.BlockSpec((1, NS), lambda i: (0, 0))
    emb = pl.BlockSpec((BR, 2 * NS), lambda i: (0, 0))
    mat = pl.BlockSpec((BR, BR), lambda i: (0, 0))
    return _sweep_with_exchange(
        body, n,
        in_specs=[pl.BlockSpec((TL, BR), lambda i: (rev(i), 3)), pl.BlockSpec((TL, BR), lambda i: (rev(i), 4)),
                  pl.BlockSpec((TL, BR), lambda i: (rev(i), 1)),
                  pl.BlockSpec((1, 1, 2 * NS), lambda i: (rev(i), 0, 0)),
                  svec, svec, emb, emb, vec, mat, vec],
        out_specs=[pl.BlockSpec((TL, W_B), lambda i: (rev(i), 0)), svec, svec, emb, emb, vec, mat, vec],
        out_shape=[jax.ShapeDtypeStruct((L, W_B), F32)] + [jax.ShapeDtypeStruct((1, NS), F32)] * 2
        + [jax.ShapeDtypeStruct((BR, 2 * NS), F32)] * 2
        + [jax.ShapeDtypeStruct((1, BR), F32), jax.ShapeDtypeStruct((BR, BR), F32), jax.ShapeDtypeStruct((1, BR), F32)],
        scratch_shapes=[pltpu.VMEM((1, 2 * NS), F32)],
        args=(proj, proj, dmix, sin_all, ar, ai, bemb, cemb, dsk, gw, gb), exchange=exchange, name="b_bwd")


def _half_masks(rows):
    lane = lax.broadcasted_iota(jnp.int32, (rows, 2 * DN_D), 1)
    return lane < DN_D, lane >= DN_D


def _bd(x):
    left, right = _half_masks(x.shape[0])
    return jnp.concatenate([jnp.where(left, x, 0.0), jnp.where(right, x, 0.0)], axis=0)


@jax.custom_vjp
def _segsum(x):
    r = lax.broadcasted_iota(jnp.int32, (2 * DN_D, 2 * DN_D), 0) // DN_D
    c = lax.broadcasted_iota(jnp.int32, (2 * DN_D, 2 * DN_D), 1) // DN_D
    ones = (r == c).astype(BF)
    hi, lo = _split(x)
    return jnp.dot(hi, ones, preferred_element_type=F32) + jnp.dot(lo, ones, preferred_element_type=F32)


_segsum.defvjp(lambda x: (_segsum(x), None), lambda _, g: (_segsum(g),))


def _pair_t(x):
    t = _bd(x).T
    return t[:DN_D] + t[DN_D:]


@jax.custom_vjp
def _pair_inv(lms):
    n = DN_D
    row = lax.broadcasted_iota(jnp.int32, (n, 2 * n), 0)
    col = lax.broadcasted_iota(jnp.int32, (n, 2 * n), 1) % n
    eye = (row == col).astype(F32)
    accs = [eye - lm for lm in lms]
    pws = list(lms)
    k = 2
    while k < n:
        pws = [_dot3(p, _bd(p)) for p in pws]
        accs = [a + _dot3(a, _bd(p)) for a, p in zip(accs, pws)]
        k *= 2
    return tuple(accs)


def _pi_b(a, g):
    ats = [_pair_t(x) for x in a]
    tmp = [_dot3(at, _bd(gi)) for at, gi in zip(ats, g)]
    return (tuple(-_dot3(t, _bd(at)) for t, at in zip(tmp, ats)),)


def _pi_f(lms):
    a = _pair_inv(lms)
    return a, a


_pair_inv.defvjp(_pi_f, _pi_b)


@jax.custom_vjp
def _pair_known_inverse(lms, inv):
    return inv


_pair_known_inverse.defvjp(lambda lms, inv: (inv, inv),
                           lambda a, g: (_pi_b(a, g)[0], tuple(jnp.zeros_like(x) for x in a)))


def _d_post(cq, ab, dz, sb0, sb1, p1, p2, ng, known=None):
    c = cq.shape[0]
    pairs = range(DN_H // 2)
    qkv = _silu(cq)
    gall = -jnp.exp(p1) * jax.nn.softplus(ab + p2)
    ball = _sigmoid(ab)
    left, _ = _half_masks(c)
    row = lax.broadcasted_iota(jnp.int32, (c, 2 * c), 0)
    col = lax.broadcasted_iota(jnp.int32, (c, 2 * c), 1) % c
    causal, strict = row >= col, row > col
    sq = lax.broadcasted_iota(jnp.int32, (c, c), 0) >= lax.broadcasted_iota(jnp.int32, (c, c), 1)
    gc_all = _dot_hi(sq.astype(F32), gall)
    gc_t = gc_all.T
    bdm = (lax.broadcasted_iota(jnp.int32, (2 * c, 2 * c), 0) // c
           == lax.broadcasted_iota(jnp.int32, (2 * c, 2 * c), 1) // c).astype(F32)
    sbd = (sb0, sb1)

    def two(t, base, p):
        return t[:, base + 2 * p * DN_D:base + 2 * (p + 1) * DN_D]

    def per_head(t, off, p):
        return jnp.where(left, t[:, off + 2 * p:off + 2 * p + 1], t[:, off + 2 * p + 1:off + 2 * p + 2])

    q = [two(qkv, 0, p) for p in pairs]
    k = [two(qkv, BR, p) for p in pairs]
    v = [two(qkv, 2 * BR, p) for p in pairs]
    q = [t * lax.rsqrt(_segsum(t * t) + EPS) * (DN_D ** -0.5) for t in q]
    k = [t * lax.rsqrt(_segsum(t * t) + EPS) for t in k]
    g2 = [per_head(gc_all, 0, p) for p in pairs]
    beta = [per_head(ball, DN_H, p) for p in pairs]
    grow = [jnp.concatenate([gc_t[2 * p:2 * p + 1, :], gc_t[2 * p + 1:2 * p + 2, :]], axis=1) for p in pairs]
    decay = [jnp.where(causal, jnp.exp(jnp.where(causal, g2[p] - grow[p], 0.0)), 0.0) for p in pairs]
    kb = [k[p] * beta[p] for p in pairs]
    kbd = [_bd(t) for t in k]
    lm = [jnp.where(strict, _mm_nt(kb[p], kbd[p]) * decay[p], 0.0) for p in pairs]
    ainv = _pair_inv(tuple(lm)) if known is None else _pair_known_inverse(tuple(lm), known)
    egc = [jnp.exp(t) for t in g2]
    uw = [_mm3(ainv[p], jnp.concatenate([_bd(v[p] * beta[p]), _bd(kb[p] * egc[p])], axis=1)) for p in pairs]
    attn = [_mm_nt(q[p], kbd[p]) * decay[p] for p in pairs]
    glast = [t[c - 1:c, :] for t in g2]
    kd = [k[p] * jnp.exp(glast[p] - g2[p]) for p in pairs]
    vnew = [uw[p][:, :2 * DN_D] - _mm(uw[p][:, 2 * DN_D:], sbd[p]) for p in pairs]
    o = [_mm(q[p] * egc[p], sbd[p]) + _mm(attn[p], _bd(vnew[p])) for p in pairs]
    nbd = [sbd[p] * jnp.exp(glast[p]) + _mm_tn(kd[p], vnew[p]) * bdm for p in pairs]
    ng2 = jnp.concatenate([ng, ng], axis=1)
    outs = [t * lax.rsqrt(_segsum(t * t) * (1.0 / DN_D) + EPS) * ng2 for t in o]
    yd = jnp.concatenate(outs, axis=1) * _silu(dz)
    return (yd, *nbd), ainv


def _d_fwd(proj, cw, p1, p2, ng, exchange=None):
    L = proj.shape[0]
    DN_G, DN_T = DN_G_FWD, DN_G_FWD * DN_C
    n = L // DN_T

    def body(qkv_ref, ab_ref, dz_ref, hq_ref, cw_ref, p1_ref, p2_ref, ng_ref, o_ref, sall_ref, inv_ref, s_ref):
        i = pl.program_id(0)

        @pl.when(i == 0)
        def _():
            s_ref[...] = jnp.zeros_like(s_ref)

        keep = (i > 0).astype(F32)
        ext = jnp.concatenate([hq_ref[...] * keep, qkv_ref[...]], axis=0)
        cq = _conv_taps(ext, cw_ref, HALO_S, K_DN, DN_T)
        st = [s_ref[p] for p in range(DN_P)]
        for g in range(DN_G):
            rows = slice(g * DN_C, (g + 1) * DN_C)
            for p in range(DN_P):
                sall_ref[g, p] = st[p]
            out, ainv = _d_post(cq[rows], ab_ref[rows, :], dz_ref[rows, :], *st, p1_ref[...], p2_ref[...],
                                ng_ref[...])
            o_ref[rows, :] = out[0]
            st = list(out[1:])
            for p in range(DN_P):
                inv_ref[g, p] = ainv[p]
        for p in range(DN_P):
            s_ref[p] = st[p]

    return _sweep_with_exchange(
        body, n,
        in_specs=[pl.BlockSpec((DN_T, 3 * BR), lambda i: (i, 3)), pl.BlockSpec((DN_T, 128), lambda i: (i, 26)),
                  pl.BlockSpec((DN_T, BR), lambda i: (i, 12)),
                  pl.BlockSpec((HALO_S, 3 * BR), _halo_map(DN_T, HALO_S, 3)),
                  pl.BlockSpec((HALO_S, 3 * BR), lambda i: (0, 0)),
                  pl.BlockSpec((1, 128), lambda i: (0, 0)), pl.BlockSpec((1, 128), lambda i: (0, 0)),
                  pl.BlockSpec((1, DN_D), lambda i: (0, 0))],
        out_specs=[pl.BlockSpec((DN_T, BR), lambda i: (i, 0)),
                   pl.BlockSpec((DN_G,) + DN_STATE, lambda i: (i, 0, 0, 0)),
                   pl.BlockSpec((DN_G,) + DN_INV, lambda i: (i, 0, 0, 0))],
        out_shape=[jax.ShapeDtypeStruct((L, BR), F32), jax.ShapeDtypeStruct((L // DN_C,) + DN_STATE, F32),
                   jax.ShapeDtypeStruct((L // DN_C,) + DN_INV, F32)],
        scratch_shapes=[pltpu.VMEM(DN_STATE, F32)],
        args=(proj, proj, proj, proj, cw, p1, p2, ng), exchange=exchange, name="d_fwd")


def _sweep_with_exchange(body, steps, in_specs, out_specs, out_shape, scratch_shapes, args, exchange, name):
    if exchange is None:
        res = pl.pallas_call(body, grid=(steps,), in_specs=in_specs, out_specs=out_specs, out_shape=out_shape,
                             scratch_shapes=scratch_shapes, name=name, compiler_params=_cp("arbitrary"))(*args)
        return res, None
    xs, axes, mode = exchange
    ex = _Exchange(xs, axes, mode)
    ni, no, ns, na = len(in_specs), len(out_specs), len(scratch_shapes), ex.na

    def carried(*refs):
        ins, xin = refs[:ni], refs[ni:ni + na]
        outs, xout = refs[ni + na:ni + na + no], refs[ni + na + no:ni + 2 * na + no]
        scr, sems = refs[ni + 2 * na + no:ni + 2 * na + no + ns], refs[ni + 2 * na + no + ns:]

        @pl.when(pl.program_id(0) == 0)
        def _():
            ex.start(xin, xout, sems)

        body(*ins, *outs, *scr)

        @pl.when(pl.program_id(0) == steps - 1)
        def _():
            ex.wait(xin, xout, sems)

    res = pl.pallas_call(carried, grid=(steps,), in_specs=list(in_specs) + ex.in_specs,
                         out_specs=list(out_specs) + ex.out_specs, out_shape=list(out_shape) + ex.out_shape,
                         scratch_shapes=list(scratch_shapes) + ex.scratch_shapes, name=name + "_x",
                         compiler_params=_cp("arbitrary"))(*args, *xs)
    return res[:no], res[no:]


def _d_bwd(proj, dmix, sall, inv, cw, p1, p2, ng, exchange=None):
    L = proj.shape[0]
    DN_G, DN_T = DN_G_BWD, DN_G_BWD * DN_C
    n = L // DN_T

    def body(qkv_ref, ab_ref, dz_ref, hq_ref, dy_ref, sall_ref, inv_ref, cw_ref, p1_ref, p2_ref, ng_ref,
             dp_ref, dcw_ref, dp1_ref, dp2_ref, dng_ref, ds_ref, carry_ref):
        i = pl.program_id(0)

        @pl.when(i == 0)
        def _():
            ds_ref[...] = jnp.zeros_like(ds_ref)
            carry_ref[...] = jnp.zeros_like(carry_ref)
            for r in (dcw_ref, dp1_ref, dp2_ref, dng_ref):
                r[...] = jnp.zeros_like(r)

        keep = (i < n - 1).astype(F32)
        ext = jnp.concatenate([hq_ref[...] * keep, qkv_ref[...]], axis=0)
        cq = _conv_taps(ext, cw_ref, HALO_S, K_DN, DN_T)
        ds = [ds_ref[p] for p in range(DN_P)]
        dcq = [None] * DN_G
        for g in reversed(range(DN_G)):
            rows = slice(g * DN_C, (g + 1) * DN_C)
            st = [sall_ref[g, p] for p in range(DN_P)]
            known = tuple(inv_ref[g, p] for p in range(DN_P))
            _, vj = jax.vjp(lambda *a: _d_post(*a, known=known)[0], cq[rows], ab_ref[rows, :], dz_ref[rows, :], *st,
                            p1_ref[...], p2_ref[...], ng_ref[...])
            grads = vj((dy_ref[rows, :], *ds))
            dcq[g] = grads[0]
            dp_ref[rows, 3 * BR:4 * BR] = grads[2]
            dp_ref[rows, 4 * BR:4 * BR + 128] = grads[1]
            ds = list(grads[3:3 + DN_P])
            dp1_ref[...] += grads[3 + DN_P]
            dp2_ref[...] += grads[4 + DN_P]
            dng_ref[...] += grads[5 + DN_P]
        for p in range(DN_P):
            ds_ref[p] = ds[p]
        dext = _conv_taps_bwd(ext, cw_ref, dcw_ref, jnp.concatenate(dcq, axis=0), HALO_S, K_DN, DN_T)
        dp_ref[:, 0:3 * BR] = _add_tail(dext[HALO_S:], carry_ref[...])
        carry_ref[...] = dext[:HALO_S]

    rev = lambda i: n - 1 - i
    hmap = _halo_map(DN_T, HALO_S, 3)
    v128 = pl.BlockSpec((1, 128), lambda i: (0, 0))
    return _sweep_with_exchange(
        body, n,
        in_specs=[pl.BlockSpec((DN_T, 3 * BR), lambda i: (rev(i), 3)), pl.BlockSpec((DN_T, 128), lambda i: (rev(i), 26)),
                  pl.BlockSpec((DN_T, BR), lambda i: (rev(i), 12)),
                  pl.BlockSpec((HALO_S, 3 * BR), lambda i: hmap(rev(i))),
                  pl.BlockSpec((DN_T, BR), lambda i: (rev(i), 3)),
                  pl.BlockSpec((DN_G,) + DN_STATE, lambda i: (rev(i), 0, 0, 0)),
                  pl.BlockSpec((DN_G,) + DN_INV, lambda i: (rev(i), 0, 0, 0)),
                  pl.BlockSpec((HALO_S, 3 * BR), lambda i: (0, 0)), v128, v128,
                  pl.BlockSpec((1, DN_D), lambda i: (0, 0))],
        out_specs=[pl.BlockSpec((DN_T, W_D), lambda i: (rev(i), 0)),
                   pl.BlockSpec((HALO_S, 3 * BR), lambda i: (0, 0)), v128, v128,
                   pl.BlockSpec((1, DN_D), lambda i: (0, 0))],
        out_shape=[jax.ShapeDtypeStruct((L, W_D), F32), jax.ShapeDtypeStruct((HALO_S, 3 * BR), F32),
                   jax.ShapeDtypeStruct((1, 128), F32), jax.ShapeDtypeStruct((1, 128), F32),
                   jax.ShapeDtypeStruct((1, DN_D), F32)],
        scratch_shapes=[pltpu.VMEM(DN_STATE, F32), pltpu.VMEM((HALO_S, 3 * BR), F32)],
        args=(proj, proj, proj, proj, dmix, sall, inv, cw, p1, p2, ng), exchange=exchange, name="d_bwd")


def _pick_rows(rows, cap):
    best = 8
    for t in range(8, cap + 1, 8):
        if rows % t == 0:
            best = t
    return best


def _adamw(w, g, m, v, name, lead=None):
    rows, rest = w.shape[0], w.shape[1:]
    tr = _pick_rows(rows, 512) if lead is None else lead
    c1 = 1.0 - ADAM_B1 ** ADAM_STEP
    c2 = 1.0 - ADAM_B2 ** ADAM_STEP

    def body(w_ref, g_ref, m_ref, v_ref, d_ref, mo_ref, vo_ref):
        gv = g_ref[...]
        mn = ADAM_B1 * m_ref[...] + (1.0 - ADAM_B1) * gv
        vn = ADAM_B2 * v_ref[...] + (1.0 - ADAM_B2) * (gv * gv)
        d_ref[...] = -ADAM_LR * ((mn / c1) / (jnp.sqrt(vn / c2) + ADAM_EPS) + ADAM_WD * w_ref[...])
        mo_ref[...] = mn
        vo_ref[...] = vn

    spec = pl.BlockSpec((tr,) + rest, lambda i: (i,) + (0,) * len(rest))
    return pl.pallas_call(
        body, grid=(rows // tr,), in_specs=[spec] * 4, out_specs=[spec] * 3,
        out_shape=[jax.ShapeDtypeStruct(w.shape, F32)] * 3,
        name=name, compiler_params=_cp("parallel"))(w, g, m, v)


def _sum_slots(r, name):
    n, rows, wd = r.shape
    tr = _pick_rows(rows, 384)

    def body(r_ref, o_ref):
        acc = r_ref[0].astype(F32)
        for j in range(1, n):
            acc = acc + r_ref[j].astype(F32)
        o_ref[...] = acc

    return pl.pallas_call(
        body, grid=(rows // tr,),
        in_specs=[pl.BlockSpec((n, tr, wd), lambda i: (0, i, 0))],
        out_specs=pl.BlockSpec((tr, wd), lambda i: (i, 0)),
        out_shape=jax.ShapeDtypeStruct((rows, wd), F32),
        name=name, compiler_params=_cp("parallel"))(r)


AXES = ("x", "y", "c")


def _group_peer(axes, k):
    pos = {a: lax.axis_index(a) for a in AXES}
    idx = 0
    for a in axes:
        idx = idx * 2 + pos[a]
    peer = dict(pos)
    for b, a in enumerate(reversed(axes)):
        if (k >> b) & 1:
            peer[a] = 1 - pos[a]
    return idx, tuple(peer[a] for a in AXES)


MAX_CHUNKS = 4


class _Exchange:
    def __init__(self, xs, axes, mode):
        self.axes, self.mode, self.na, self.n = axes, mode, len(xs), 2 ** len(axes)
        n = self.n
        self.out_shape, self.pieces = [], []
        for x in xs:
            if mode == "gather":
                shape, lead = (n,) + x.shape, x.shape[0]
            elif mode == "scatter":
                shape, lead = x.shape, x.shape[1]
            else:
                shape, lead = (x.shape[0], n * x.shape[1], x.shape[2]), x.shape[0]
            self.out_shape.append(jax.ShapeDtypeStruct(shape, x.dtype))
            big = x.size * x.dtype.itemsize >= (1 << 20)
            if mode == "rows":
                self.pieces.append(lead if lead <= MAX_CHUNKS else 1)
            else:
                self.pieces.append(MAX_CHUNKS if big and lead % (16 * MAX_CHUNKS) == 0 else 1)
        self.in_specs = [pl.BlockSpec(memory_space=pl.ANY)] * self.na
        self.out_specs = [pl.BlockSpec(memory_space=pl.ANY)] * self.na
        self.scratch_shapes = [pltpu.SemaphoreType.DMA((self.na, MAX_CHUNKS, n)),
                               pltpu.SemaphoreType.DMA((self.na, MAX_CHUNKS, n)),
                               pltpu.SemaphoreType.DMA((self.na, MAX_CHUNKS))]

    def _copies(self, x_refs, o_refs, send_sems, recv_sems, local_sems):
        me, _ = _group_peer(self.axes, 0)
        local, remote = [], []
        for a, (x, o) in enumerate(zip(x_refs, o_refs)):
            for c in range(self.pieces[a]):
                if self.mode == "rows":
                    r = x.shape[1]
                    b = slice(None) if self.pieces[a] == 1 else pl.ds(c, 1)
                    src = lambda k, x=x, b=b: x.at[b]
                    dst = o.at[b, pl.ds(me * r, r)]
                else:
                    lead = x.shape[1] if self.mode == "scatter" else x.shape[0]
                    rs = pl.ds(c * (lead // self.pieces[a]), lead // self.pieces[a])
                    if self.mode == "scatter":
                        src = lambda k, x=x, rs=rs: x.at[me ^ k, rs]
                    else:
                        src = lambda k, x=x, rs=rs: x.at[rs]
                    dst = o.at[me, rs]
                local.append(pltpu.make_async_copy(src(0), dst, local_sems.at[a, c]))
                for k in range(1, self.n):
                    remote.append(pltpu.make_async_remote_copy(
                        src_ref=src(k), dst_ref=dst, send_sem=send_sems.at[a, c, k], recv_sem=recv_sems.at[a, c, k],
                        device_id=_group_peer(self.axes, k)[1], device_id_type=MESH))
        return local, remote

    def start(self, x_refs, o_refs, sems):
        local, remote = self._copies(x_refs, o_refs, *sems)
        for cp in local + remote:
            cp.start()

    def wait(self, x_refs, o_refs, sems):
        local, remote = self._copies(x_refs, o_refs, *sems)
        for cp in remote:
            cp.wait_send()
        for cp in remote:
            cp.wait_recv()
        for cp in local:
            cp.wait()


def _exchange(xs, axes, mode, name):
    ex = _Exchange(xs, axes, mode)
    na = ex.na

    def body(*refs):
        ex.start(refs[:na], refs[na:2 * na], refs[2 * na:])
        ex.wait(refs[:na], refs[na:2 * na], refs[2 * na:])

    return pl.pallas_call(body, out_shape=ex.out_shape, in_specs=ex.in_specs, out_specs=ex.out_specs,
                          scratch_shapes=ex.scratch_shapes, name=name)(*xs)


SHARDED_SMALL = (("a_conv_w", 2), ("a_pw_w", 1), ("s5_glu_w", 1), ("c_conv_w", 2), ("d_conv_w", 2))
REPLICATED = ("norm_g", "a_conv_b", "a_ln_g", "a_ln_b", "a_pw_b", "s5_lambda_re", "s5_lambda_im", "s5_b_re",
              "s5_b_im", "s5_c_re", "s5_c_im", "s5_d", "s5_log_dt", "s5_glu_b", "d_a_log", "d_dt_bias",
              "d_norm_g", "final_g")
WEIGHTS = ("norm_g", "w_in", "a_conv_w", "a_conv_b", "a_ln_g", "a_ln_b", "a_pw_w", "a_pw_b", "s5_lambda_re",
           "s5_lambda_im", "s5_b_re", "s5_b_im", "s5_c_re", "s5_c_im", "s5_d", "s5_log_dt", "s5_glu_w",
           "s5_glu_b", "c_conv_w", "d_conv_w", "d_a_log", "d_dt_bias", "d_norm_g", "w_out", "final_g")
LANES = 1024


def _size(shape):
    size = 1
    for d in shape:
        size *= d
    return size


def _slab_rows(shape):
    return -(-_size(shape) // (8 * LANES)) * 8


def _pack(arrs, rows):
    parts = []
    for a in arrs:
        r = _slab_rows(a.shape)
        parts.append(jnp.pad(a.reshape(-1), (0, r * LANES - a.size)).reshape(r, LANES))
    used = sum(p.shape[0] for p in parts)
    if rows > used:
        parts.append(jnp.zeros((rows - used, LANES), parts[0].dtype))
    return jnp.concatenate(parts, axis=0)


def _unpack(slab, shapes):
    out, off = [], 0
    for s in shapes:
        r = _slab_rows(s)
        out.append(slab[off:off + r].reshape(-1)[:_size(s)].reshape(s))
        off += r
    return out


def _rows_for(shapes, mult):
    rows = sum(_slab_rows(s) for s in shapes)
    return -(-rows // mult) * mult


def _row(v, width=None):
    v = v.reshape(1, -1)
    return v if width is None else jnp.pad(v, ((0, 0), (0, width - v.shape[1])))


def _pad_rows(w, rows):
    return jnp.pad(w, ((0, rows - w.shape[0]), (0, 0)))


def _permute_in(w):
    return jnp.concatenate([w[:, :3072], w[:, 3080:N_IN], w[:, 3072:3080],
                            jnp.zeros((w.shape[0], N_INP - N_IN), w.dtype)], axis=1)


def _layer_fwd(x, p, exchange=None):
    proj, h = _proj_fwd(x, p["norm_g"], p["wp"])
    ya = _a_fwd(proj, p["a_cw"], p["a_cb"], p["a_lng"], p["a_lnb"], p["a_pw"], p["a_pwb"])
    ar, ai, bemb, cemb = _s5_prep(*p["s5"])
    yb, sin_all = _b_fwd(proj, ar, ai, bemb, cemb, p["s5_d"], p["glu_w"], p["glu_b"])
    yc = _c_fwd(proj, p["c_cw"])
    (yd, sall, inv), got = _d_fwd(proj, p["d_cw"], p["d_p1"], p["d_p2"], p["d_ng"], exchange)
    xo = _out_fwd(x, ya, yb, yc, yd, p["wo"])
    return xo, dict(x=x, proj=proj, h=h, ys=(ya, yb, yc, yd), sin_all=sin_all, sall=sall, inv=inv,
                    s5=(ar, ai, bemb, cemb)), got


def _layer_bwd(dxo, p, r, exchanges):
    proj = r["proj"]
    ar, ai, bemb, cemb = r["s5"]
    dmix = _out_bwd_x(dxo, p["wo"])
    dwo = _dwout(*r["ys"], dxo)
    dpa, dcw_a, dcb, dlng, dlnb, dpw, dpwb = _a_bwd(proj, dmix, p["a_cw"], p["a_cb"], p["a_lng"], p["a_lnb"],
                                                     p["a_pw"], p["a_pwb"])
    got = {}
    (dpb, dar, dai, dbe, dce, ddsk, dgw, dgb), got["b"] = _b_bwd(
        proj, dmix, r["sin_all"], ar, ai, bemb, cemb, p["s5_d"], p["glu_w"], p["glu_b"], exchanges.get("b"))
    dlre, dlim, dldt, dbre, dbim, dcre, dcim = _s5_prep_bwd(*p["s5"], dar, dai, dbe, dce)
    dpc, dcw_c = _c_bwd(proj, dmix, p["c_cw"])
    (dpd, dcw_d, dp1, dp2, dng), got["d"] = _d_bwd(proj, dmix, r["sall"], r["inv"], p["d_cw"], p["d_p1"],
                                                   p["d_p2"], p["d_ng"], exchanges.get("d"))
    ex_proj = exchanges.get("proj")
    if callable(ex_proj):
        ex_proj = ex_proj(got["d"])
    (dx, dg), got["proj"] = _proj_bwd_x(r["x"], p["norm_g"], dpa, dpb, dpc, dpd, p["wp"], dxo, ex_proj)
    dwa, dwb, dwc, dwd = _dwin(r["h"], dpa, dpb, dpc, dpd)
    dwin = jnp.concatenate([dwa, dwb, dwc, dwd[:, :3 * BR], dwd[:, 4 * BR:4 * BR + 2 * DN_H],
                            dwd[:, 3 * BR:4 * BR]], axis=1)

    def unrows(t, perm):
        return jnp.transpose(t.reshape(S5_H, S5_G, S5_P), perm)

    grads = dict(
        norm_g=dg.reshape(-1), w_in=dwin, a_conv_w=dcw_a[:K_A], a_conv_b=dcb.reshape(-1),
        a_ln_g=dlng.reshape(-1), a_ln_b=dlnb.reshape(-1), a_pw_w=dpw, a_pw_b=dpwb.reshape(-1),
        s5_lambda_re=dlre.reshape(S5_G, S5_P), s5_lambda_im=dlim.reshape(S5_G, S5_P),
        s5_b_re=unrows(dbre, (1, 2, 0)), s5_b_im=unrows(dbim, (1, 2, 0)),
        s5_c_re=unrows(dcre, (1, 0, 2)), s5_c_im=unrows(dcim, (1, 0, 2)),
        s5_d=ddsk.reshape(-1), s5_log_dt=dldt[0, :S5_G], s5_glu_w=dgw, s5_glu_b=dgb.reshape(-1),
        c_conv_w=dcw_c[:K_C], d_conv_w=dcw_d[:K_DN], d_a_log=dp1[0, :DN_H], d_dt_bias=dp2[0, :DN_H],
        d_norm_g=dng.reshape(-1), w_out=dwo)
    return dx, grads, got


def _layer_params(full, wp, wo, l):
    return dict(
        norm_g=_row(full["norm_g"][l]), wp=wp,
        a_cw=_pad_rows(full["a_conv_w"][l], HALO_A), a_cb=_row(full["a_conv_b"][l]),
        a_lng=_row(full["a_ln_g"][l]), a_lnb=_row(full["a_ln_b"][l]), a_pw=full["a_pw_w"][l],
        a_pwb=_row(full["a_pw_b"][l]),
        s5=(_row(full["s5_lambda_re"][l]), _row(full["s5_lambda_im"][l]), _row(full["s5_log_dt"][l], 128),
            jnp.transpose(full["s5_b_re"][l], (2, 0, 1)).reshape(S5_H, NS),
            jnp.transpose(full["s5_b_im"][l], (2, 0, 1)).reshape(S5_H, NS),
            jnp.transpose(full["s5_c_re"][l], (1, 0, 2)).reshape(S5_H, NS),
            jnp.transpose(full["s5_c_im"][l], (1, 0, 2)).reshape(S5_H, NS)),
        s5_d=_row(full["s5_d"][l]), glu_w=full["s5_glu_w"][l], glu_b=_row(full["s5_glu_b"][l]),
        c_cw=_pad_rows(full["c_conv_w"][l], HALO_S), d_cw=_pad_rows(full["d_conv_w"][l], HALO_S),
        d_p1=_row(full["d_a_log"][l], 128), d_p2=_row(full["d_dt_bias"][l], 128),
        d_ng=_row(full["d_norm_g"][l]), wo=wo)


def kernel(x, norm_g, w_in, a_conv_w, a_conv_b, a_ln_g, a_ln_b, a_pw_w, a_pw_b, s5_lambda_re, s5_lambda_im, s5_b_re, s5_b_im, s5_c_re, s5_c_im, s5_d, s5_log_dt, s5_glu_w, s5_glu_b, c_conv_w, d_conv_w, d_a_log, d_dt_bias, d_norm_g, w_out, final_g, loss_target, m_norm_g, m_w_in, m_a_conv_w, m_a_conv_b, m_a_ln_g, m_a_ln_b, m_a_pw_w, m_a_pw_b, m_s5_lambda_re, m_s5_lambda_im, m_s5_b_re, m_s5_b_im, m_s5_c_re, m_s5_c_im, m_s5_d, m_s5_log_dt, m_s5_glu_w, m_s5_glu_b, m_c_conv_w, m_d_conv_w, m_d_a_log, m_d_dt_bias, m_d_norm_g, m_w_out, m_final_g, v_norm_g, v_w_in, v_a_conv_w, v_a_conv_b, v_a_ln_g, v_a_ln_b, v_a_pw_w, v_a_pw_b, v_s5_lambda_re, v_s5_lambda_im, v_s5_b_re, v_s5_b_im, v_s5_c_re, v_s5_c_im, v_s5_d, v_s5_log_dt, v_s5_glu_w, v_s5_glu_b, v_c_conv_w, v_d_conv_w, v_d_a_log, v_d_dt_bias, v_d_norm_g, v_w_out, v_final_g):
    given = dict(locals())
    w = {n: given[n] for n in WEIGHTS}
    m = {n: given["m_" + n] for n in WEIGHTS}
    v = {n: given["v_" + n] for n in WEIGHTS}
    xs, tgt = x[0], loss_target[0]

    n_in, n_out = w["w_in"].shape[2], w["w_out"].shape[1]
    sm_names = [n for n, _ in SHARDED_SMALL]
    sm_shapes = [w[n].shape for n in sm_names]
    sm_rows = _rows_for(sm_shapes, 16)
    win_b, wout_b = w["w_in"].astype(BF), w["w_out"].astype(BF)
    g_in, g_out, g_sm = _exchange([win_b[0], wout_b[0], _pack([w[n] for n in sm_names], sm_rows)],
                                  ("x", "y"), "gather", "gather_first")
    full = dict(w)
    parts = [_unpack(g_sm[j], sm_shapes) for j in range(4)]
    for i, (n, ax) in enumerate(SHARDED_SMALL):
        full[n] = jnp.concatenate([parts[j][i] for j in range(4)], axis=ax)

    saved = []
    h = xs
    for l in range(DEPTH):
        p = _layer_params(full, _permute_in(jnp.concatenate([g_in[j] for j in range(4)], axis=1)),
                          jnp.concatenate([g_out[j] for j in range(4)], axis=0), l)
        nxt = ([win_b[l + 1], wout_b[l + 1]], ("x", "y"), "gather") if l + 1 < DEPTH else None
        h, r, got = _layer_fwd(h, p, nxt)
        saved.append((p, r))
        if got is not None:
            g_in, g_out = got
    loss_tile, dx, dfg = _loss_bwd(h, _row(full["final_g"]), tgt)

    def big_slots(g):
        s_in = jnp.stack([g["w_in"][:, j * n_in:(j + 1) * n_in].astype(BF) for j in range(4)])
        return [s_in.reshape(8, D_MODEL // 2, n_in), g["w_out"].astype(BF).reshape(8, n_out // 2, D_MODEL)]

    def halves(rv):
        return [_sum_slots(rv[0], "sum_w_in")[None], _sum_slots(rv[1], "sum_w_out")[None]]

    layer_grads, summed, pending, arrived = [None] * DEPTH, [None] * DEPTH, None, {}
    for l in reversed(range(DEPTH)):
        p, r = saved[l]
        ex = {}
        if pending is not None:
            ex["d"] = (pending, AXES, "scatter")
        if l + 2 in arrived:
            ex["b"] = (halves(arrived.pop(l + 2)), ("c",), "rows")
        if l == 0:
            ex["proj"] = lambda came: (halves(came), ("c",), "rows")
        dx, layer_grads[l], got = _layer_bwd(dx, p, r, ex)
        if got["b"] is not None:
            summed[l + 2] = got["b"]
        if got["proj"] is not None:
            summed[1] = got["proj"]
        elif got["d"] is not None:
            arrived[l + 1] = got["d"]
        pending = big_slots(layer_grads[l])
    grads = {n: jnp.stack([layer_grads[l][n] for l in range(DEPTH)]) for n in WEIGHTS
             if n not in ("final_g", "w_in", "w_out")}
    grads["final_g"] = dfg.reshape(-1)
    slots = []
    for j in range(4):
        sl = [lax.slice_in_dim(grads[n], j * w[n].shape[ax], (j + 1) * w[n].shape[ax], axis=ax)
              for n, ax in SHARDED_SMALL]
        slots.append(_pack(sl, sm_rows))
    rp_shapes = [w[n].shape for n in REPLICATED] + [(1,)]
    rp_rows = _rows_for(rp_shapes, 64)
    r_in0, r_out0, r_sm, r_rp = _exchange(
        pending + [jnp.stack(slots).reshape(8, sm_rows // 2, LANES),
                   _pack([grads[n] for n in REPLICATED] + [loss_tile[0, 0:1]], rp_rows).reshape(8, rp_rows // 8, LANES)],
        AXES, "scatter", "scatter_last")
    summed[0] = _exchange(halves((r_in0, r_out0)) + [_sum_slots(r_sm, "sum_small")[None]], ("c",), "rows",
                          "gather_halves")
    h_sm = summed[0][2]
    h_in = jnp.concatenate([summed[l][0] for l in range(DEPTH)], axis=0)
    h_out = jnp.concatenate([summed[l][1] for l in range(DEPTH)], axis=0)
    (g_rp,) = _exchange([_sum_slots(r_rp, "sum_replicated")], AXES, "gather", "gather_replicated")
    g_rp = g_rp.reshape(rp_rows, LANES)
    g_sm = h_sm.reshape(sm_rows, LANES)

    out = {}

    def put(name, shape, res):
        for key, t in zip(("delta", "new_m", "new_v"), res):
            out[key + "_" + name] = t.reshape(shape)

    g2 = h_out.reshape(DEPTH * n_out, D_MODEL)
    out["grad_w_out"] = g2.reshape(w["w_out"].shape)
    put("w_out", w["w_out"].shape, _adamw(w["w_out"].reshape(g2.shape), g2, m["w_out"].reshape(g2.shape),
                                          v["w_out"].reshape(g2.shape), "adamw_w_out"))
    cm = lambda a: jnp.transpose(a, (2, 0, 1))
    rm = lambda a: jnp.transpose(a, (1, 2, 0))
    g3 = cm(h_in)
    out["grad_w_in"] = rm(g3)
    for key, t in zip(("delta", "new_m", "new_v"),
                      _adamw(cm(w["w_in"]), g3, cm(m["w_in"]), cm(v["w_in"]), "adamw_w_in", lead=n_in // 6)):
        out[key + "_w_in"] = rm(t)
    zero = jnp.zeros((1,), F32)
    res_sm = _adamw(_pack([w[n] for n in sm_names], sm_rows), g_sm, _pack([m[n] for n in sm_names], sm_rows),
                    _pack([v[n] for n in sm_names], sm_rows), "adamw_small")
    res_rp = _adamw(_pack([w[n] for n in REPLICATED] + [zero], rp_rows), g_rp,
                    _pack([m[n] for n in REPLICATED] + [zero], rp_rows),
                    _pack([v[n] for n in REPLICATED] + [zero], rp_rows), "adamw_replicated")
    for key, sm, rp in (("grad", g_sm, g_rp), ("delta", res_sm[0], res_rp[0]), ("new_m", res_sm[1], res_rp[1]),
                        ("new_v", res_sm[2], res_rp[2])):
        for n, t in zip(sm_names, _unpack(sm, sm_shapes)):
            out[key + "_" + n] = t
        for n, t in zip(REPLICATED, _unpack(rp, rp_shapes[:-1])):
            out[key + "_" + n] = t
    loss = _unpack(g_rp, rp_shapes)[-1].reshape(())
    return (loss, dx[None], *[out["grad_" + n] for n in WEIGHTS], *[out["delta_" + n] for n in WEIGHTS],
            *[out["new_m_" + n] for n in WEIGHTS], *[out["new_v_" + n] for n in WEIGHTS])
```

```python
import functools

import jax
import jax.numpy as jnp
from jax import lax
from jax.experimental import pallas as pl
from jax.experimental.pallas import tpu as pltpu

F32, BF = jnp.float32, jnp.bfloat16
HI = lax.Precision.HIGHEST
MESH = pl.DeviceIdType.MESH

D_MODEL = 1024
BR = 256
DEPTH = 4
N_IN = 3336
N_INP = 3456
COL_A, COL_B, COL_C, COL_D = 0, 768, 1280, 2304
W_A, W_B, W_C, W_D = 768, 512, 1024, 1152
S5_G, S5_H, S5_P = 16, 16, 64
NS = S5_G * S5_P
DN_H, DN_D, DN_C = 4, 64, 64
DN_G_FWD, DN_G_BWD = 8, 4
DN_P = DN_H // 2
DN_STATE = (DN_P, 2 * DN_D, 2 * DN_D)
DN_INV = (DN_P, DN_C, 2 * DN_D)
K_A, K_C, K_DN = 31, 3, 4
HALO_A, HALO_S = 32, 8
EPS = 1e-6
TL = 256
VMEM_LIMIT = 56 * 1024 * 1024

ADAM_LR, ADAM_B1, ADAM_B2, ADAM_EPS, ADAM_WD, ADAM_STEP = 0.001, 0.9, 0.999, 1e-08, 0.01, 10


def _cp(*sem):
    return pltpu.CompilerParams(dimension_semantics=sem, vmem_limit_bytes=VMEM_LIMIT)


def _sigmoid(x):
    return jax.nn.sigmoid(x)


def _silu(x):
    return x * jax.nn.sigmoid(x)


def _rmsnorm(x, g):
    return x * lax.rsqrt(jnp.mean(x * x, axis=-1, keepdims=True) + EPS) * g


@jax.custom_vjp
def _mm(a, w):
    return jnp.dot(a.astype(BF), w.astype(BF), preferred_element_type=F32)


def _mm_f(a, w):
    return _mm(a, w), (a, w)


def _mm_b(res, g):
    a, w = res
    gb = g.astype(BF)
    da = lax.dot_general(gb, w.astype(BF), (((1,), (1,)), ((), ())), preferred_element_type=F32)
    dw = lax.dot_general(a.astype(BF), gb, (((0,), (0,)), ((), ())), preferred_element_type=F32)
    return da, dw


_mm.defvjp(_mm_f, _mm_b)


@jax.custom_vjp
def _mm_nt(a, b):
    return lax.dot_general(a.astype(BF), b.astype(BF), (((1,), (1,)), ((), ())), preferred_element_type=F32)


def _mm_nt_f(a, b):
    return _mm_nt(a, b), (a, b)


def _mm_nt_b(res, g):
    a, b = res
    gb = g.astype(BF)
    da = jnp.dot(gb, b.astype(BF), preferred_element_type=F32)
    db = lax.dot_general(gb, a.astype(BF), (((0,), (0,)), ((), ())), preferred_element_type=F32)
    return da, db


_mm_nt.defvjp(_mm_nt_f, _mm_nt_b)


@jax.custom_vjp
def _mm_tn(a, b):
    return lax.dot_general(a.astype(BF), b.astype(BF), (((0,), (0,)), ((), ())), preferred_element_type=F32)


def _mm_tn_f(a, b):
    return _mm_tn(a, b), (a, b)


def _mm_tn_b(res, g):
    a, b = res
    gb = g.astype(BF)
    da = lax.dot_general(b.astype(BF), gb, (((1,), (1,)), ((), ())), preferred_element_type=F32)
    db = jnp.dot(a.astype(BF), gb, preferred_element_type=F32)
    return da, db


_mm_tn.defvjp(_mm_tn_f, _mm_tn_b)


def _dot_hi(a, b):
    return jnp.dot(a, b, precision=HI, preferred_element_type=F32)


def _split(a):
    hi = a.astype(BF)
    return hi, (a - hi.astype(F32)).astype(BF)


def _dot3(a, b, dims=(((1,), (0,)), ((), ()))):
    ah, al = _split(a)
    bh, bl = _split(b)
    d = functools.partial(lax.dot_general, dimension_numbers=dims, preferred_element_type=F32)
    return d(ah, bh) + d(ah, bl) + d(al, bh)


@jax.custom_vjp
def _mm3(a, b):
    return _dot3(a, b)


def _mm3_f(a, b):
    return _dot3(a, b), (a, b)


def _mm3_b(res, g):
    a, b = res
    return _dot3(g, b, (((1,), (1,)), ((), ()))), _dot3(a, g, (((0,), (0,)), ((), ())))


_mm3.defvjp(_mm3_f, _mm3_b)


def _roll(x, s):
    n = x.shape[0]
    s = s % n
    return x if s == 0 else pltpu.roll(x, s, 0)


def _conv_taps(ext, w_ref, halo, k_taps, tl):
    acc = None
    for k in range(k_taps):
        term = _roll(ext, (k_taps - 1) - k)[halo:halo + tl] * w_ref[k:k + 1, :]
        acc = term if acc is None else acc + term
    return acc


def _conv_taps_bwd(ext, w_ref, dw_ref, dacc, halo, k_taps, tl):
    dpad = jnp.concatenate([dacc, jnp.zeros((halo, dacc.shape[1]), F32)], axis=0)
    dext = None
    for k in range(k_taps):
        r = _roll(ext, (k_taps - 1) - k)[halo:halo + tl]
        dw_ref[k:k + 1, :] += jnp.sum(r * dacc, axis=0, keepdims=True)
        term = _roll(dpad, halo - (k_taps - 1) + k) * w_ref[k:k + 1, :]
        dext = term if dext is None else dext + term
    return dext


def _add_tail(x, tail):
    tl, h = x.shape[0], tail.shape[0]
    return x + jnp.concatenate([jnp.zeros((tl - h, x.shape[1]), F32), tail], axis=0)


def _proj_fwd(x, g, wp):
    L = x.shape[0]

    def body(x_ref, g_ref, w_ref, p_ref, h_ref):
        hb = _rmsnorm(x_ref[...], g_ref[...]).astype(BF)
        h_ref[...] = hb
        p_ref[...] = jnp.dot(hb, w_ref[...], preferred_element_type=F32)

    return pl.pallas_call(
        body, grid=(L // TL,),
        in_specs=[pl.BlockSpec((TL, D_MODEL), lambda i: (i, 0)),
                  pl.BlockSpec((1, D_MODEL), lambda i: (0, 0)),
                  pl.BlockSpec((D_MODEL, N_INP), lambda i: (0, 0))],
        out_specs=[pl.BlockSpec((TL, N_INP), lambda i: (i, 0)),
                   pl.BlockSpec((TL, D_MODEL), lambda i: (i, 0))],
        out_shape=[jax.ShapeDtypeStruct((L, N_INP), F32), jax.ShapeDtypeStruct((L, D_MODEL), BF)],
        name="proj_fwd", compiler_params=_cp("parallel"))(x, g, wp)


def _proj_bwd_x(x, g, dpa, dpb, dpc, dpd, wp, dxo, exchange=None):
    L = x.shape[0]

    def body(x_ref, g_ref, a_ref, b_ref, c_ref, d_ref, w_ref, dxo_ref, dx_ref, dg_ref):
        dh = None
        for ref, c0, wd in ((a_ref, COL_A, W_A), (b_ref, COL_B, W_B), (c_ref, COL_C, W_C), (d_ref, COL_D, W_D)):
            t = lax.dot_general(ref[...].astype(BF), w_ref[:, c0:c0 + wd], (((1,), (1,)), ((), ())),
                                preferred_element_type=F32)
            dh = t if dh is None else dh + t
        _, vj = jax.vjp(_rmsnorm, x_ref[...], g_ref[...])
        dx, dg = vj(dh)
        dx_ref[...] = dxo_ref[...] + dx

        @pl.when(pl.program_id(0) == 0)
        def _():
            dg_ref[...] = jnp.zeros_like(dg_ref)

        dg_ref[...] += dg

    def rows(wd):
        return pl.BlockSpec((TL, wd), lambda i: (i, 0))

    return _sweep_with_exchange(
        body, L // TL,
        in_specs=[rows(D_MODEL), pl.BlockSpec((1, D_MODEL), lambda i: (0, 0)),
                  rows(W_A), rows(W_B), rows(W_C), rows(W_D),
                  pl.BlockSpec((D_MODEL, N_INP), lambda i: (0, 0)), rows(D_MODEL)],
        out_specs=[rows(D_MODEL), pl.BlockSpec((1, D_MODEL), lambda i: (0, 0))],
        out_shape=[jax.ShapeDtypeStruct((L, D_MODEL), F32), jax.ShapeDtypeStruct((1, D_MODEL), F32)],
        scratch_shapes=[], args=(x, g, dpa, dpb, dpc, dpd, wp, dxo), exchange=exchange, name="proj_bwd_x")


def _dwin(h, dpa, dpb, dpc, dpd):
    L = h.shape[0]

    def body(h_ref, a_ref, b_ref, c_ref, d_ref, oa_ref, ob_ref, oc_ref, od_ref):
        outs = (oa_ref, ob_ref, oc_ref, od_ref)

        @pl.when(pl.program_id(0) == 0)
        def _():
            for o in outs:
                o[...] = jnp.zeros_like(o)

        ht = h_ref[...].T
        for ref, o in zip((a_ref, b_ref, c_ref, d_ref), outs):
            o[...] += jnp.dot(ht, ref[...].astype(BF), preferred_element_type=F32)

    def rows(wd):
        return pl.BlockSpec((TL, wd), lambda i: (i, 0))

    def whole(wd):
        return pl.BlockSpec((D_MODEL, wd), lambda i: (0, 0))

    widths = (W_A, W_B, W_C, W_D)
    return pl.pallas_call(
        body, grid=(L // TL,),
        in_specs=[rows(D_MODEL)] + [rows(wd) for wd in widths],
        out_specs=[whole(wd) for wd in widths],
        out_shape=[jax.ShapeDtypeStruct((D_MODEL, wd), F32) for wd in widths],
        name="dwin", compiler_params=_cp("arbitrary"))(h, dpa, dpb, dpc, dpd)


def _dwout(ya, yb, yc, yd, dxo):
    L = dxo.shape[0]
    tk, tn = min(512, L), 512

    def body(a_ref, b_ref, c_ref, d_ref, g_ref, o_ref):
        @pl.when(pl.program_id(1) == 0)
        def _():
            o_ref[...] = jnp.zeros_like(o_ref)

        gb = g_ref[...].astype(BF)
        for j, ref in enumerate((a_ref, b_ref, c_ref, d_ref)):
            o_ref[j * BR:(j + 1) * BR, :] += lax.dot_general(ref[...].astype(BF), gb, (((0,), (0,)), ((), ())),
                                                             preferred_element_type=F32)

    ys = pl.BlockSpec((tk, BR), lambda j, t: (t, 0))
    return pl.pallas_call(
        body, grid=(D_MODEL // tn, L // tk),
        in_specs=[ys, ys, ys, ys, pl.BlockSpec((tk, tn), lambda j, t: (t, j))],
        out_specs=pl.BlockSpec((D_MODEL, tn), lambda j, t: (0, j)),
        out_shape=jax.ShapeDtypeStruct((D_MODEL, D_MODEL), F32),
        name="dwout", compiler_params=_cp("parallel", "arbitrary"))(ya, yb, yc, yd, dxo)


def _out_fwd(x, ya, yb, yc, yd, wo):
    L = x.shape[0]

    def body(x_ref, a_ref, b_ref, c_ref, d_ref, w_ref, o_ref):
        acc = x_ref[...]
        for j, ref in enumerate((a_ref, b_ref, c_ref, d_ref)):
            acc = acc + jnp.dot(ref[...].astype(BF), w_ref[j * BR:(j + 1) * BR, :], preferred_element_type=F32)
        o_ref[...] = acc

    def rows(wd):
        return pl.BlockSpec((TL, wd), lambda i: (i, 0))

    return pl.pallas_call(
        body, grid=(L // TL,),
        in_specs=[rows(D_MODEL), rows(BR), rows(BR), rows(BR), rows(BR),
                  pl.BlockSpec((D_MODEL, D_MODEL), lambda i: (0, 0))],
        out_specs=rows(D_MODEL), out_shape=jax.ShapeDtypeStruct((L, D_MODEL), F32),
        name="out_fwd", compiler_params=_cp("parallel"))(x, ya, yb, yc, yd, wo)


def _out_bwd_x(dxo, wo):
    L = dxo.shape[0]

    def body(d_ref, w_ref, o_ref):
        o_ref[...] = lax.dot_general(d_ref[...].astype(BF), w_ref[...], (((1,), (1,)), ((), ())),
                                     preferred_element_type=F32)

    return pl.pallas_call(
        body, grid=(L // TL,),
        in_specs=[pl.BlockSpec((TL, D_MODEL), lambda i: (i, 0)), pl.BlockSpec((D_MODEL, D_MODEL), lambda i: (0, 0))],
        out_specs=pl.BlockSpec((TL, D_MODEL), lambda i: (i, 0)),
        out_shape=jax.ShapeDtypeStruct((L, D_MODEL), F32),
        name="out_bwd_x", compiler_params=_cp("parallel"))(dxo, wo)


def _loss_bwd(x, g, tgt):
    L = x.shape[0]

    def f(xv, gv, tv):
        err = _rmsnorm(xv, gv) - tv
        return 0.5 * jnp.sum(jnp.mean(err * err, axis=-1, keepdims=True), axis=0, keepdims=True)

    def body(x_ref, g_ref, t_ref, loss_ref, dx_ref, dg_ref):
        tv = t_ref[...]
        loss, vj = jax.vjp(lambda a, b: f(a, b, tv), x_ref[...], g_ref[...])
        dx, dg = vj(jnp.ones((1, 1), F32))
        dx_ref[...] = dx

        @pl.when(pl.program_id(0) == 0)
        def _():
            dg_ref[...] = jnp.zeros_like(dg_ref)
            loss_ref[...] = jnp.zeros_like(loss_ref)

        dg_ref[...] += dg
        loss_ref[...] += jnp.broadcast_to(loss, loss_ref.shape)

    return pl.pallas_call(
        body, grid=(L // TL,),
        in_specs=[pl.BlockSpec((TL, D_MODEL), lambda i: (i, 0)), pl.BlockSpec((1, D_MODEL), lambda i: (0, 0)),
                  pl.BlockSpec((TL, D_MODEL), lambda i: (i, 0))],
        out_specs=[pl.BlockSpec((8, 128), lambda i: (0, 0)), pl.BlockSpec((TL, D_MODEL), lambda i: (i, 0)),
                   pl.BlockSpec((1, D_MODEL), lambda i: (0, 0))],
        out_shape=[jax.ShapeDtypeStruct((8, 128), F32), jax.ShapeDtypeStruct((L, D_MODEL), F32),
                   jax.ShapeDtypeStruct((1, D_MODEL), F32)],
        name="loss_bwd", compiler_params=_cp("arbitrary"))(x, g, tgt)


def _a_pre(val, gate):
    return val * _sigmoid(gate)


def _a_post(acc, az, cb, lng, lnb, pw, pwb):
    t = acc + cb
    mu = jnp.mean(t, axis=-1, keepdims=True)
    xc = t - mu
    ln = xc * lax.rsqrt(jnp.mean(xc * xc, axis=-1, keepdims=True) + EPS) * lng + lnb
    return (_mm(_silu(ln), pw) + pwb) * _silu(az)


def _halo_map(tl, halo, col):
    r = tl // halo
    return lambda i: (jnp.maximum(i * r - 1, 0), col)


def _a_fwd(proj, cw, cb, lng, lnb, pw, pwb):
    L = proj.shape[0]

    def body(vg_ref, az_ref, hvg_ref, cw_ref, cb_ref, lng_ref, lnb_ref, pw_ref, pwb_ref, o_ref):
        keep = (pl.program_id(0) > 0).astype(F32)
        a_h = _a_pre(hvg_ref[:, 0:BR], hvg_ref[:, BR:2 * BR]) * keep
        a_t = _a_pre(vg_ref[:, 0:BR], vg_ref[:, BR:2 * BR])
        ext = jnp.concatenate([a_h, a_t], axis=0)
        acc = _conv_taps(ext, cw_ref, HALO_A, K_A, TL)
        o_ref[...] = _a_post(acc, az_ref[...], cb_ref[...], lng_ref[...], lnb_ref[...], pw_ref[...], pwb_ref[...])

    vec = pl.BlockSpec((1, BR), lambda i: (0, 0))
    return pl.pallas_call(
        body, grid=(L // TL,),
        in_specs=[pl.BlockSpec((TL, 2 * BR), lambda i: (i, 0)), pl.BlockSpec((TL, BR), lambda i: (i, 2)),
                  pl.BlockSpec((HALO_A, 2 * BR), _halo_map(TL, HALO_A, 0)),
                  pl.BlockSpec((HALO_A, BR), lambda i: (0, 0)), vec, vec, vec,
                  pl.BlockSpec((BR, BR), lambda i: (0, 0)), vec],
        out_specs=pl.BlockSpec((TL, BR), lambda i: (i, 0)),
        out_shape=jax.ShapeDtypeStruct((L, BR), F32),
        name="a_fwd", compiler_params=_cp("parallel"))(proj, proj, proj, cw, cb, lng, lnb, pw, pwb)


def _a_bwd(proj, dmix, cw, cb, lng, lnb, pw, pwb):
    L = proj.shape[0]
    n = L // TL

    def body(vg_ref, az_ref, hvg_ref, dy_ref, cw_ref, cb_ref, lng_ref, lnb_ref, pw_ref, pwb_ref,
             dp_ref, dcw_ref, dcb_ref, dlng_ref, dlnb_ref, dpw_ref, dpwb_ref, carry_ref):
        i = pl.program_id(0)

        @pl.when(i == 0)
        def _():
            carry_ref[...] = jnp.zeros_like(carry_ref)
            for r in (dcw_ref, dcb_ref, dlng_ref, dlnb_ref, dpw_ref, dpwb_ref):
                r[...] = jnp.zeros_like(r)

        keep = (i < n - 1).astype(F32)
        val, gate = vg_ref[:, 0:BR], vg_ref[:, BR:2 * BR]
        a_h = _a_pre(hvg_ref[:, 0:BR], hvg_ref[:, BR:2 * BR]) * keep
        a_t, vj_pre = jax.vjp(_a_pre, val, gate)
        ext = jnp.concatenate([a_h, a_t], axis=0)
        acc = _conv_taps(ext, cw_ref, HALO_A, K_A, TL)
        _, vj_post = jax.vjp(_a_post, acc, az_ref[...], cb_ref[...], lng_ref[...], lnb_ref[...], pw_ref[...],
                             pwb_ref[...])
        dacc, daz, dcb, dlng, dlnb, dpw, dpwb = vj_post(dy_ref[...])
        dext = _conv_taps_bwd(ext, cw_ref, dcw_ref, dacc, HALO_A, K_A, TL)
        da = _add_tail(dext[HALO_A:], carry_ref[...])
        carry_ref[...] = dext[:HALO_A]
        dval, dgate = vj_pre(da)
        dp_ref[:, 0:BR] = dval
        dp_ref[:, BR:2 * BR] = dgate
        dp_ref[:, 2 * BR:3 * BR] = daz
        dcb_ref[...] += dcb
        dlng_ref[...] += dlng
        dlnb_ref[...] += dlnb
        dpw_ref[...] += dpw
        dpwb_ref[...] += dpwb

    rev = lambda i: n - 1 - i
    vec = pl.BlockSpec((1, BR), lambda i: (0, 0))
    hmap = _halo_map(TL, HALO_A, 0)
    return pl.pallas_call(
        body, grid=(n,),
        in_specs=[pl.BlockSpec((TL, 2 * BR), lambda i: (rev(i), 0)), pl.BlockSpec((TL, BR), lambda i: (rev(i), 2)),
                  pl.BlockSpec((HALO_A, 2 * BR), lambda i: hmap(rev(i))),
                  pl.BlockSpec((TL, BR), lambda i: (rev(i), 0)),
                  pl.BlockSpec((HALO_A, BR), lambda i: (0, 0)), vec, vec, vec,
                  pl.BlockSpec((BR, BR), lambda i: (0, 0)), vec],
        out_specs=[pl.BlockSpec((TL, W_A), lambda i: (rev(i), 0)),
                   pl.BlockSpec((HALO_A, BR), lambda i: (0, 0)), vec, vec, vec,
                   pl.BlockSpec((BR, BR), lambda i: (0, 0)), vec],
        out_shape=[jax.ShapeDtypeStruct((L, W_A), F32), jax.ShapeDtypeStruct((HALO_A, BR), F32)]
        + [jax.ShapeDtypeStruct((1, BR), F32)] * 3
        + [jax.ShapeDtypeStruct((BR, BR), F32), jax.ShapeDtypeStruct((1, BR), F32)],
        scratch_shapes=[pltpu.VMEM((HALO_A, BR), F32)],
        name="a_bwd", compiler_params=_cp("arbitrary"))(proj, proj, proj, dmix, cw, cb, lng, lnb, pw, pwb)


def _c_pre(cg, xc):
    return cg * xc


def _c_post(acc, bg, cz):
    return bg * acc * _silu(cz)


def _c_fwd(proj, cw):
    L = proj.shape[0]

    def body(bg_ref, cx_ref, cz_ref, hcx_ref, cw_ref, o_ref):
        keep = (pl.program_id(0) > 0).astype(F32)
        p_h = _c_pre(hcx_ref[:, 0:BR], hcx_ref[:, BR:2 * BR]) * keep
        p_t = _c_pre(cx_ref[:, 0:BR], cx_ref[:, BR:2 * BR])
        ext = jnp.concatenate([p_h, p_t], axis=0)
        acc = _conv_taps(ext, cw_ref, HALO_S, K_C, TL)
        o_ref[...] = _c_post(acc, bg_ref[...], cz_ref[...])

    return pl.pallas_call(
        body, grid=(L // TL,),
        in_specs=[pl.BlockSpec((TL, BR), lambda i: (i, 5)), pl.BlockSpec((TL, 2 * BR), lambda i: (i, 3)),
                  pl.BlockSpec((TL, BR), lambda i: (i, 8)),
                  pl.BlockSpec((HALO_S, 2 * BR), _halo_map(TL, HALO_S, 3)),
                  pl.BlockSpec((HALO_S, BR), lambda i: (0, 0))],
        out_specs=pl.BlockSpec((TL, BR), lambda i: (i, 0)),
        out_shape=jax.ShapeDtypeStruct((L, BR), F32),
        name="c_fwd", compiler_params=_cp("parallel"))(proj, proj, proj, proj, cw)


def _c_bwd(proj, dmix, cw):
    L = proj.shape[0]
    n = L // TL

    def body(bg_ref, cx_ref, cz_ref, hcx_ref, dy_ref, cw_ref, dp_ref, dcw_ref, carry_ref):
        i = pl.program_id(0)

        @pl.when(i == 0)
        def _():
            carry_ref[...] = jnp.zeros_like(carry_ref)
            dcw_ref[...] = jnp.zeros_like(dcw_ref)

        keep = (i < n - 1).astype(F32)
        p_h = _c_pre(hcx_ref[:, 0:BR], hcx_ref[:, BR:2 * BR]) * keep
        p_t, vj_pre = jax.vjp(_c_pre, cx_ref[:, 0:BR], cx_ref[:, BR:2 * BR])
        ext = jnp.concatenate([p_h, p_t], axis=0)
        acc = _conv_taps(ext, cw_ref, HALO_S, K_C, TL)
        _, vj_post = jax.vjp(_c_post, acc, bg_ref[...], cz_ref[...])
        dacc, dbg, dcz = vj_post(dy_ref[...])
        dext = _conv_taps_bwd(ext, cw_ref, dcw_ref, dacc, HALO_S, K_C, TL)
        dp = _add_tail(dext[HALO_S:], carry_ref[...])
        carry_ref[...] = dext[:HALO_S]
        dcg, dxc = vj_pre(dp)
        dp_ref[:, 0:BR] = dbg
        dp_ref[:, BR:2 * BR] = dcg
        dp_ref[:, 2 * BR:3 * BR] = dxc
        dp_ref[:, 3 * BR:4 * BR] = dcz

    rev = lambda i: n - 1 - i
    hmap = _halo_map(TL, HALO_S, 3)
    return pl.pallas_call(
        body, grid=(n,),
        in_specs=[pl.BlockSpec((TL, BR), lambda i: (rev(i), 5)), pl.BlockSpec((TL, 2 * BR), lambda i: (rev(i), 3)),
                  pl.BlockSpec((TL, BR), lambda i: (rev(i), 8)),
                  pl.BlockSpec((HALO_S, 2 * BR), lambda i: hmap(rev(i))),
                  pl.BlockSpec((TL, BR), lambda i: (rev(i), 2)),
                  pl.BlockSpec((HALO_S, BR), lambda i: (0, 0))],
        out_specs=[pl.BlockSpec((TL, W_C), lambda i: (rev(i), 0)), pl.BlockSpec((HALO_S, BR), lambda i: (0, 0))],
        out_shape=[jax.ShapeDtypeStruct((L, W_C), F32), jax.ShapeDtypeStruct((HALO_S, BR), F32)],
        scratch_shapes=[pltpu.VMEM((HALO_S, BR), F32)],
        name="c_bwd", compiler_params=_cp("arbitrary"))(proj, proj, proj, proj, dmix, cw)


def _s5_prep_fn(lre, lim, ldt, bre, bim, cre, cim):
    grp = lax.broadcasted_iota(jnp.int32, (128, NS), 0)
    lane = lax.broadcasted_iota(jnp.int32, (128, NS), 1)
    expand = (grp == lane // S5_P).astype(F32)
    dt = jnp.exp(_dot_hi(jnp.broadcast_to(ldt, (8, 128)), expand)[0:1])
    lr = jnp.minimum(lre, -1e-4)
    mag = jnp.exp(lr * dt)
    ar = mag * jnp.cos(lim * dt)
    ai = mag * jnp.sin(lim * dt)
    den = lr * lr + lim * lim
    fr = ((ar - 1.0) * lr + ai * lim) / den
    fi = (ai * lr - (ar - 1.0) * lim) / den
    bbr = fr * bre - fi * bim
    bbi = fr * bim + fi * bre
    row = lax.broadcasted_iota(jnp.int32, (BR, NS), 0)
    col = lax.broadcasted_iota(jnp.int32, (BR, NS), 1)
    blk = (row // S5_H == col // S5_P).astype(F32)

    def embed(t):
        return jnp.concatenate([t] * S5_G, axis=0) * blk

    bemb = jnp.concatenate([embed(bbr), embed(bbi)], axis=1)
    cemb = jnp.concatenate([embed(cre), embed(-cim)], axis=1)
    return ar, ai, bemb, cemb


def _s5_prep(lre, lim, ldt, bre, bim, cre, cim):
    def body(*refs):
        outs = _s5_prep_fn(*[r[...] for r in refs[:7]])
        for r, o in zip(refs[7:], outs):
            r[...] = o

    return pl.pallas_call(
        body,
        out_shape=[jax.ShapeDtypeStruct((1, NS), F32)] * 2 + [jax.ShapeDtypeStruct((BR, 2 * NS), F32)] * 2,
        name="s5_prep", compiler_params=pltpu.CompilerParams(vmem_limit_bytes=VMEM_LIMIT),
    )(lre, lim, ldt, bre, bim, cre, cim)


def _s5_prep_bwd(lre, lim, ldt, bre, bim, cre, cim, dar, dai, dbemb, dcemb):
    def body(*refs):
        _, vj = jax.vjp(_s5_prep_fn, *[r[...] for r in refs[:7]])
        grads = vj(tuple(r[...] for r in refs[7:11]))
        for r, o in zip(refs[11:], grads):
            r[...] = o

    return pl.pallas_call(
        body,
        out_shape=[jax.ShapeDtypeStruct((1, NS), F32)] * 2 + [jax.ShapeDtypeStruct((1, 128), F32)]
        + [jax.ShapeDtypeStruct((S5_H, NS), F32)] * 4,
        name="s5_prep_bwd", compiler_params=pltpu.CompilerParams(vmem_limit_bytes=VMEM_LIMIT),
    )(lre, lim, ldt, bre, bim, cre, cim, dar, dai, dbemb, dcemb)


def _s5_scan(xr, xi, ar, ai, reverse):
    n = xr.shape[0]
    row = lax.broadcasted_iota(jnp.int32, (n, 1), 0)
    pr, pi = ar, ai
    d = 1
    while d < n:
        if d % 8:
            if reverse:
                m = row < n - d
                sr = jnp.where(m, _roll(xr, n - d), 0.0)
                si = jnp.where(m, _roll(xi, n - d), 0.0)
            else:
                m = row >= d
                sr = jnp.where(m, _roll(xr, d), 0.0)
                si = jnp.where(m, _roll(xi, d), 0.0)
            xr, xi = xr + pr * sr - pi * si, xi + pr * si + pi * sr
        elif reverse:
            sr, si = xr[d:], xi[d:]
            xr, xi = (jnp.concatenate([xr[:n - d] + pr * sr - pi * si, xr[n - d:]], axis=0),
                      jnp.concatenate([xi[:n - d] + pr * si + pi * sr, xi[n - d:]], axis=0))
        else:
            sr, si = xr[:n - d], xi[:n - d]
            xr, xi = (jnp.concatenate([xr[:d], xr[d:] + pr * sr - pi * si], axis=0),
                      jnp.concatenate([xi[:d], xi[d:] + pr * si + pi * sr], axis=0))
        pr, pi = pr * pr - pi * pi, 2.0 * pr * pi
        d *= 2
    return xr, xi


def _s5_states(u, bemb_b, ar, ai, sin_r, sin_i):
    bu = jnp.dot(u.astype(BF), bemb_b, preferred_element_type=F32)
    first = lax.broadcasted_iota(jnp.int32, (u.shape[0], 1), 0) == 0
    xr = bu[:, :NS] + jnp.where(first, ar * sin_r - ai * sin_i, 0.0)
    xi = bu[:, NS:] + jnp.where(first, ar * sin_i + ai * sin_r, 0.0)
    return _s5_scan(xr, xi, ar, ai, False)


def _b_post(yssm, u, bz, dsk, gw, gb):
    z = jax.nn.gelu(yssm + dsk * u)
    return z * _sigmoid(_mm(z, gw) + gb) * _silu(bz)


def _b_fwd(proj, ar, ai, bemb, cemb, dsk, gw, gb):
    L = proj.shape[0]
    n = L // TL

    def body(u_ref, bz_ref, ar_ref, ai_ref, be_ref, ce_ref, dsk_ref, gw_ref, gb_ref, o_ref, sin_ref, carry_ref):
        @pl.when(pl.program_id(0) == 0)
        def _():
            carry_ref[...] = jnp.zeros_like(carry_ref)

        sin = carry_ref[...]
        sin_ref[0] = sin
        u = u_ref[...]
        sr, si = _s5_states(u, be_ref[...].astype(BF), ar_ref[...], ai_ref[...], sin[:, :NS], sin[:, NS:])
        carry_ref[:, :NS] = sr[TL - 1:TL]
        carry_ref[:, NS:] = si[TL - 1:TL]
        s = jnp.concatenate([sr, si], axis=1).astype(BF)
        yssm = lax.dot_general(s, ce_ref[...].astype(BF), (((1,), (1,)), ((), ())), preferred_element_type=F32)
        o_ref[...] = _b_post(yssm, u, bz_ref[...], dsk_ref[...], gw_ref[...], gb_ref[...])

    vec = pl.BlockSpec((1, BR), lambda i: (0, 0))
    svec = pl.BlockSpec((1, NS), lambda i: (0, 0))
    emb = pl.BlockSpec((BR, 2 * NS), lambda i: (0, 0))
    return pl.pallas_call(
        body, grid=(n,),
        in_specs=[pl.BlockSpec((TL, BR), lambda i: (i, 3)), pl.BlockSpec((TL, BR), lambda i: (i, 4)),
                  svec, svec, emb, emb, vec, pl.BlockSpec((BR, BR), lambda i: (0, 0)), vec],
        out_specs=[pl.BlockSpec((TL, BR), lambda i: (i, 0)), pl.BlockSpec((1, 1, 2 * NS), lambda i: (i, 0, 0))],
        out_shape=[jax.ShapeDtypeStruct((L, BR), F32), jax.ShapeDtypeStruct((n, 1, 2 * NS), F32)],
        scratch_shapes=[pltpu.VMEM((1, 2 * NS), F32)],
        name="b_fwd", compiler_params=_cp("arbitrary"))(proj, proj, ar, ai, bemb, cemb, dsk, gw, gb)


def _b_bwd(proj, dmix, sin_all, ar, ai, bemb, cemb, dsk, gw, gb, exchange=None):
    L = proj.shape[0]
    n = L // TL

    def body(u_ref, bz_ref, dy_ref, sin_ref, ar_ref, ai_ref, be_ref, ce_ref, dsk_ref, gw_ref, gb_ref,
             dp_ref, dar_ref, dai_ref, dbe_ref, dce_ref, ddsk_ref, dgw_ref, dgb_ref, carry_ref):
        i = pl.program_id(0)

        @pl.when(i == 0)
        def _():
            carry_ref[...] = jnp.zeros_like(carry_ref)
            for r in (dar_ref, dai_ref, dbe_ref, dce_ref, ddsk_ref, dgw_ref, dgb_ref):
                r[...] = jnp.zeros_like(r)

        u = u_ref[...]
        ar, ai = ar_ref[...], ai_ref[...]
        be_b, ce_b = be_ref[...].astype(BF), ce_ref[...].astype(BF)
        sin = sin_ref[0]
        sr, si = _s5_states(u, be_b, ar, ai, sin[:, :NS], sin[:, NS:])
        s_b = jnp.concatenate([sr, si], axis=1).astype(BF)
        yssm = lax.dot_general(s_b, ce_b, (((1,), (1,)), ((), ())), preferred_element_type=F32)
        _, vj = jax.vjp(_b_post, yssm, u, bz_ref[...], dsk_ref[...], gw_ref[...], gb_ref[...])
        dyssm, du, dbz, ddsk, dgw, dgb = vj(dy_ref[...])
        dy_b = dyssm.astype(BF)
        dce_ref[...] += lax.dot_general(dy_b, s_b, (((0,), (0,)), ((), ())), preferred_element_type=F32)
        gs = jnp.dot(dy_b, ce_b, preferred_element_type=F32)
        last = lax.broadcasted_iota(jnp.int32, (TL, 1), 0) == TL - 1
        cr, ci = carry_ref[:, :NS], carry_ref[:, NS:]
        gr = gs[:, :NS] + jnp.where(last, ar * cr + ai * ci, 0.0)
        gi = gs[:, NS:] + jnp.where(last, ar * ci - ai * cr, 0.0)
        dsr, dsi = _s5_scan(gr, gi, ar, -ai, True)
        carry_ref[:, :NS] = dsr[0:1]
        carry_ref[:, NS:] = dsi[0:1]
        first = lax.broadcasted_iota(jnp.int32, (TL, 1), 0) == 0
        pr = jnp.where(first, sin[:, :NS], _roll(sr, 1))
        pi = jnp.where(first, sin[:, NS:], _roll(si, 1))
        dar_ref[...] += jnp.sum(dsr * pr + dsi * pi, axis=0, keepdims=True)
        dai_ref[...] += jnp.sum(dsi * pr - dsr * pi, axis=0, keepdims=True)
        ds_b = jnp.concatenate([dsr, dsi], axis=1).astype(BF)
        dbe_ref[...] += lax.dot_general(u.astype(BF), ds_b, (((0,), (0,)), ((), ())), preferred_element_type=F32)
        du = du + lax.dot_general(ds_b, be_b, (((1,), (1,)), ((), ())), preferred_element_type=F32)
        dp_ref[:, 0:BR] = du
        dp_ref[:, BR:2 * BR] = dbz
        ddsk_ref[...] += ddsk
        dgw_ref[...] += dgw
        dgb_ref[...] += dgb

    rev = lambda i: n - 1 - i
    vec = pl.BlockSpec((1, BR), lambda i: (0, 0))
    svec = pl.BlockSpec((1, NS), lambda i: (0, 0))
    emb = pl.BlockSpec((BR, 2 * NS), lambda i: (0, 0))
    mat = pl.BlockSpec((BR, BR), lambda i: (0, 0))
    return _sweep_with_exchange(
        body, n,
        in_specs=[pl.BlockSpec((TL, BR), lambda i: (rev(i), 3)), pl.BlockSpec((TL, BR), lambda i: (rev(i), 4)),
                  pl.BlockSpec((TL, BR), lambda i: (rev(i), 1)),
                  pl.BlockSpec((1, 1, 2 * NS), lambda i: (rev(i), 0, 0)),
                  svec, svec, emb, emb, vec, mat, vec],
        out_specs=[pl.BlockSpec((TL, W_B), lambda i: (rev(i), 0)), svec, svec, emb, emb, vec, mat, vec],
        out_shape=[jax.ShapeDtypeStruct((L, W_B), F32)] + [jax.ShapeDtypeStruct((1, NS), F32)] * 2
        + [jax.ShapeDtypeStruct((BR, 2 * NS), F32)] * 2
        + [jax.ShapeDtypeStruct((1, BR), F32), jax.ShapeDtypeStruct((BR, BR), F32), jax.ShapeDtypeStruct((1, BR), F32)],
        scratch_shapes=[pltpu.VMEM((1, 2 * NS), F32)],
        args=(proj, proj, dmix, sin_all, ar, ai, bemb, cemb, dsk, gw, gb), exchange=exchange, name="b_bwd")


def _half_masks(rows):
    lane = lax.broadcasted_iota(jnp.int32, (rows, 2 * DN_D), 1)
    return lane < DN_D, lane >= DN_D


def _bd(x):
    left, right = _half_masks(x.shape[0])
    return jnp.concatenate([jnp.where(left, x, 0.0), jnp.where(right, x, 0.0)], axis=0)


@jax.custom_vjp
def _segsum(x):
    r = lax.broadcasted_iota(jnp.int32, (2 * DN_D, 2 * DN_D), 0) // DN_D
    c = lax.broadcasted_iota(jnp.int32, (2 * DN_D, 2 * DN_D), 1) // DN_D
    ones = (r == c).astype(BF)
    hi, lo = _split(x)
    return jnp.dot(hi, ones, preferred_element_type=F32) + jnp.dot(lo, ones, preferred_element_type=F32)


_segsum.defvjp(lambda x: (_segsum(x), None), lambda _, g: (_segsum(g),))


def _pair_t(x):
    t = _bd(x).T
    return t[:DN_D] + t[DN_D:]


@jax.custom_vjp
def _pair_inv(lms):
    n = DN_D
    row = lax.broadcasted_iota(jnp.int32, (n, 2 * n), 0)
    col = lax.broadcasted_iota(jnp.int32, (n, 2 * n), 1) % n
    eye = (row == col).astype(F32)
    accs = [eye - lm for lm in lms]
    pws = list(lms)
    k = 2
    while k < n:
        pws = [_dot3(p, _bd(p)) for p in pws]
        accs = [a + _dot3(a, _bd(p)) for a, p in zip(accs, pws)]
        k *= 2
    return tuple(accs)


def _pi_b(a, g):
    ats = [_pair_t(x) for x in a]
    tmp = [_dot3(at, _bd(gi)) for at, gi in zip(ats, g)]
    return (tuple(-_dot3(t, _bd(at)) for t, at in zip(tmp, ats)),)


def _pi_f(lms):
    a = _pair_inv(lms)
    return a, a


_pair_inv.defvjp(_pi_f, _pi_b)


@jax.custom_vjp
def _pair_known_inverse(lms, inv):
    return inv


_pair_known_inverse.defvjp(lambda lms, inv: (inv, inv),
                           lambda a, g: (_pi_b(a, g)[0], tuple(jnp.zeros_like(x) for x in a)))


def _d_tile(cq, ab, dz, sb0, sb1, p1, p2, ng, known=None):
    c = DN_C
    chunks = range(cq.shape[0] // c)
    units = [(g, p) for g in chunks for p in range(DN_P)]
    n = range(len(units))
    qkv = _silu(cq)
    gall = -jnp.exp(p1) * jax.nn.softplus(ab + p2)
    ball = _sigmoid(ab)
    left, _ = _half_masks(c)
    row = lax.broadcasted_iota(jnp.int32, (c, 2 * c), 0)
    col = lax.broadcasted_iota(jnp.int32, (c, 2 * c), 1) % c
    causal, strict = row >= col, row > col
    sq = lax.broadcasted_iota(jnp.int32, (c, c), 0) >= lax.broadcasted_iota(jnp.int32, (c, c), 1)
    gc_all = [_dot_hi(sq.astype(F32), gall[g * c:(g + 1) * c]) for g in chunks]
    gc_t = [t.T for t in gc_all]
    bdm = (lax.broadcasted_iota(jnp.int32, (2 * c, 2 * c), 0) // c
           == lax.broadcasted_iota(jnp.int32, (2 * c, 2 * c), 1) // c).astype(F32)

    def two(t, base, g, p):
        return t[g * c:(g + 1) * c, base + 2 * p * DN_D:base + 2 * (p + 1) * DN_D]

    def per_head(t, off, p):
        return jnp.where(left, t[:, off + 2 * p:off + 2 * p + 1], t[:, off + 2 * p + 1:off + 2 * p + 2])

    q = [two(qkv, 0, g, p) for g, p in units]
    k = [two(qkv, BR, g, p) for g, p in units]
    v = [two(qkv, 2 * BR, g, p) for g, p in units]
    q = [t * lax.rsqrt(_segsum(t * t) + EPS) * (DN_D ** -0.5) for t in q]
    k = [t * lax.rsqrt(_segsum(t * t) + EPS) for t in k]
    g2 = [per_head(gc_all[g], 0, p) for g, p in units]
    beta = [per_head(ball[g * c:(g + 1) * c], DN_H, p) for g, p in units]
    grow = [jnp.concatenate([gc_t[g][2 * p:2 * p + 1, :], gc_t[g][2 * p + 1:2 * p + 2, :]], axis=1) for g, p in units]
    decay = [jnp.where(causal, jnp.exp(jnp.where(causal, g2[u] - grow[u], 0.0)), 0.0) for u in n]
    kb = [k[u] * beta[u] for u in n]
    kbd = [_bd(t) for t in k]
    lm = [jnp.where(strict, _mm_nt(kb[u], kbd[u]) * decay[u], 0.0) for u in n]
    ainv = _pair_inv(tuple(lm)) if known is None else _pair_known_inverse(tuple(lm), known)
    egc = [jnp.exp(t) for t in g2]
    uw = [_mm3(ainv[u], jnp.concatenate([_bd(v[u] * beta[u]), _bd(kb[u] * egc[u])], axis=1)) for u in n]
    attn = [_mm_nt(q[u], kbd[u]) * decay[u] for u in n]
    glast = [t[c - 1:c, :] for t in g2]
    kd = [k[u] * jnp.exp(glast[u] - g2[u]) for u in n]
    qd = [q[u] * egc[u] for u in n]
    ng2 = jnp.concatenate([ng, ng], axis=1)
    sbd, starts, outs = [sb0, sb1], [], []
    for g in chunks:
        starts.append(tuple(sbd))
        us = [g * DN_P + p for p in range(DN_P)]
        vnew = [uw[u][:, :2 * DN_D] - _mm(uw[u][:, 2 * DN_D:], sbd[p]) for p, u in enumerate(us)]
        o = [_mm(qd[u], sbd[p]) + _mm(attn[u], _bd(vnew[p])) for p, u in enumerate(us)]
        sbd = [sbd[p] * jnp.exp(glast[u]) + _mm_tn(kd[u], vnew[p]) * bdm for p, u in enumerate(us)]
        outs.append(jnp.concatenate([t * lax.rsqrt(_segsum(t * t) * (1.0 / DN_D) + EPS) * ng2 for t in o], axis=1))
    yd = jnp.concatenate(outs, axis=0) * _silu(dz)
    return (yd, *sbd), (starts, ainv)


def _d_fwd(proj, cw, p1, p2, ng, exchange=None):
    L = proj.shape[0]
    DN_G, DN_T = DN_G_FWD, DN_G_FWD * DN_C
    n = L // DN_T

    def body(qkv_ref, ab_ref, dz_ref, hq_ref, cw_ref, p1_ref, p2_ref, ng_ref, o_ref, sall_ref, inv_ref, s_ref):
        i = pl.program_id(0)

        @pl.when(i == 0)
        def _():
            s_ref[...] = jnp.zeros_like(s_ref)

        keep = (i > 0).astype(F32)
        ext = jnp.concatenate([hq_ref[...] * keep, qkv_ref[...]], axis=0)
        cq = _conv_taps(ext, cw_ref, HALO_S, K_DN, DN_T)
        out, (starts, ainv) = _d_tile(cq, ab_ref[...], dz_ref[...], s_ref[0], s_ref[1], p1_ref[...], p2_ref[...],
                                      ng_ref[...])
        o_ref[...] = out[0]
        for p in range(DN_P):
            s_ref[p] = out[1 + p]
            for g in range(DN_G):
                sall_ref[g, p] = starts[g][p]
                inv_ref[g, p] = ainv[g * DN_P + p]

    return _sweep_with_exchange(
        body, n,
        in_specs=[pl.BlockSpec((DN_T, 3 * BR), lambda i: (i, 3)), pl.BlockSpec((DN_T, 128), lambda i: (i, 26)),
                  pl.BlockSpec((DN_T, BR), lambda i: (i, 12)),
                  pl.BlockSpec((HALO_S, 3 * BR), _halo_map(DN_T, HALO_S, 3)),
                  pl.BlockSpec((HALO_S, 3 * BR), lambda i: (0, 0)),
                  pl.BlockSpec((1, 128), lambda i: (0, 0)), pl.BlockSpec((1, 128), lambda i: (0, 0)),
                  pl.BlockSpec((1, DN_D), lambda i: (0, 0))],
        out_specs=[pl.BlockSpec((DN_T, BR), lambda i: (i, 0)),
                   pl.BlockSpec((DN_G,) + DN_STATE, lambda i: (i, 0, 0, 0)),
                   pl.BlockSpec((DN_G,) + DN_INV, lambda i: (i, 0, 0, 0))],
        out_shape=[jax.ShapeDtypeStruct((L, BR), F32), jax.ShapeDtypeStruct((L // DN_C,) + DN_STATE, F32),
                   jax.ShapeDtypeStruct((L // DN_C,) + DN_INV, F32)],
        scratch_shapes=[pltpu.VMEM(DN_STATE, F32)],
        args=(proj, proj, proj, proj, cw, p1, p2, ng), exchange=exchange, name="d_fwd")


def _sweep_with_exchange(body, steps, in_specs, out_specs, out_shape, scratch_shapes, args, exchange, name):
    if exchange is None:
        res = pl.pallas_call(body, grid=(steps,), in_specs=in_specs, out_specs=out_specs, out_shape=out_shape,
                             scratch_shapes=scratch_shapes, name=name, compiler_params=_cp("arbitrary"))(*args)
        return res, None
    xs, axes, mode = exchange
    ex = _Exchange(xs, axes, mode)
    ni, no, ns, na = len(in_specs), len(out_specs), len(scratch_shapes), ex.na

    def carried(*refs):
        ins, xin = refs[:ni], refs[ni:ni + na]
        outs, xout = refs[ni + na:ni + na + no], refs[ni + na + no:ni + 2 * na + no]
        scr, sems = refs[ni + 2 * na + no:ni + 2 * na + no + ns], refs[ni + 2 * na + no + ns:]

        @pl.when(pl.program_id(0) == 0)
        def _():
            ex.start(xin, xout, sems)

        body(*ins, *outs, *scr)

        @pl.when(pl.program_id(0) == steps - 1)
        def _():
            ex.wait(xin, xout, sems)

    res = pl.pallas_call(carried, grid=(steps,), in_specs=list(in_specs) + ex.in_specs,
                         out_specs=list(out_specs) + ex.out_specs, out_shape=list(out_shape) + ex.out_shape,
                         scratch_shapes=list(scratch_shapes) + ex.scratch_shapes, name=name + "_x",
                         compiler_params=_cp("arbitrary"))(*args, *xs)
    return res[:no], res[no:]


def _d_bwd(proj, dmix, sall, inv, cw, p1, p2, ng, exchange=None):
    L = proj.shape[0]
    DN_G, DN_T = DN_G_BWD, DN_G_BWD * DN_C
    n = L // DN_T

    def body(qkv_ref, ab_ref, dz_ref, hq_ref, dy_ref, sall_ref, inv_ref, cw_ref, p1_ref, p2_ref, ng_ref,
             dp_ref, dcw_ref, dp1_ref, dp2_ref, dng_ref, ds_ref, carry_ref):
        i = pl.program_id(0)

        @pl.when(i == 0)
        def _():
            ds_ref[...] = jnp.zeros_like(ds_ref)
            carry_ref[...] = jnp.zeros_like(carry_ref)
            for r in (dcw_ref, dp1_ref, dp2_ref, dng_ref):
                r[...] = jnp.zeros_like(r)

        keep = (i < n - 1).astype(F32)
        ext = jnp.concatenate([hq_ref[...] * keep, qkv_ref[...]], axis=0)
        cq = _conv_taps(ext, cw_ref, HALO_S, K_DN, DN_T)
        known = tuple(inv_ref[g, p] for g in range(DN_G) for p in range(DN_P))
        _, vj = jax.vjp(lambda *a: _d_tile(*a, known=known)[0], cq, ab_ref[...], dz_ref[...], sall_ref[0, 0],
                        sall_ref[0, 1], p1_ref[...], p2_ref[...], ng_ref[...])
        dcq, dab, ddz, ds0, ds1, dp1, dp2, dng = vj((dy_ref[...], ds_ref[0], ds_ref[1]))
        dp_ref[:, 3 * BR:4 * BR] = ddz
        dp_ref[:, 4 * BR:4 * BR + 128] = dab
        ds_ref[0] = ds0
        ds_ref[1] = ds1
        dp1_ref[...] += dp1
        dp2_ref[...] += dp2
        dng_ref[...] += dng
        dext = _conv_taps_bwd(ext, cw_ref, dcw_ref, dcq, HALO_S, K_DN, DN_T)
        dp_ref[:, 0:3 * BR] = _add_tail(dext[HALO_S:], carry_ref[...])
        carry_ref[...] = dext[:HALO_S]

    rev = lambda i: n - 1 - i
    hmap = _halo_map(DN_T, HALO_S, 3)
    v128 = pl.BlockSpec((1, 128), lambda i: (0, 0))
    return _sweep_with_exchange(
        body, n,
        in_specs=[pl.BlockSpec((DN_T, 3 * BR), lambda i: (rev(i), 3)), pl.BlockSpec((DN_T, 128), lambda i: (rev(i), 26)),
                  pl.BlockSpec((DN_T, BR), lambda i: (rev(i), 12)),
                  pl.BlockSpec((HALO_S, 3 * BR), lambda i: hmap(rev(i))),
                  pl.BlockSpec((DN_T, BR), lambda i: (rev(i), 3)),
                  pl.BlockSpec((DN_G,) + DN_STATE, lambda i: (rev(i), 0, 0, 0)),
                  pl.BlockSpec((DN_G,) + DN_INV, lambda i: (rev(i), 0, 0, 0)),
                  pl.BlockSpec((HALO_S, 3 * BR), lambda i: (0, 0)), v128, v128,
                  pl.BlockSpec((1, DN_D), lambda i: (0, 0))],
        out_specs=[pl.BlockSpec((DN_T, W_D), lambda i: (rev(i), 0)),
                   pl.BlockSpec((HALO_S, 3 * BR), lambda i: (0, 0)), v128, v128,
                   pl.BlockSpec((1, DN_D), lambda i: (0, 0))],
        out_shape=[jax.ShapeDtypeStruct((L, W_D), F32), jax.ShapeDtypeStruct((HALO_S, 3 * BR), F32),
                   jax.ShapeDtypeStruct((1, 128), F32), jax.ShapeDtypeStruct((1, 128), F32),
                   jax.ShapeDtypeStruct((1, DN_D), F32)],
        scratch_shapes=[pltpu.VMEM(DN_STATE, F32), pltpu.VMEM((HALO_S, 3 * BR), F32)],
        args=(proj, proj, proj, proj, dmix, sall, inv, cw, p1, p2, ng), exchange=exchange, name="d_bwd")


def _pick_rows(rows, cap):
    best = 8
    for t in range(8, cap + 1, 8):
        if rows % t == 0:
            best = t
    return best


def _adamw(w, g, m, v, name, lead=None):
    rows, rest = w.shape[0], w.shape[1:]
    tr = _pick_rows(rows, 512) if lead is None else lead
    c1 = 1.0 - ADAM_B1 ** ADAM_STEP
    c2 = 1.0 - ADAM_B2 ** ADAM_STEP

    def body(w_ref, g_ref, m_ref, v_ref, d_ref, mo_ref, vo_ref):
        gv = g_ref[...]
        mn = ADAM_B1 * m_ref[...] + (1.0 - ADAM_B1) * gv
        vn = ADAM_B2 * v_ref[...] + (1.0 - ADAM_B2) * (gv * gv)
        d_ref[...] = -ADAM_LR * ((mn / c1) / (jnp.sqrt(vn / c2) + ADAM_EPS) + ADAM_WD * w_ref[...])
        mo_ref[...] = mn
        vo_ref[...] = vn

    spec = pl.BlockSpec((tr,) + rest, lambda i: (i,) + (0,) * len(rest))
    return pl.pallas_call(
        body, grid=(rows // tr,), in_specs=[spec] * 4, out_specs=[spec] * 3,
        out_shape=[jax.ShapeDtypeStruct(w.shape, F32)] * 3,
        name=name, compiler_params=_cp("parallel"))(w, g, m, v)


def _sum_slots(r, name):
    n, rows, wd = r.shape
    tr = _pick_rows(rows, 384)

    def body(r_ref, o_ref):
        acc = r_ref[0].astype(F32)
        for j in range(1, n):
            acc = acc + r_ref[j].astype(F32)
        o_ref[...] = acc

    return pl.pallas_call(
        body, grid=(rows // tr,),
        in_specs=[pl.BlockSpec((n, tr, wd), lambda i: (0, i, 0))],
        out_specs=pl.BlockSpec((tr, wd), lambda i: (i, 0)),
        out_shape=jax.ShapeDtypeStruct((rows, wd), F32),
        name=name, compiler_params=_cp("parallel"))(r)


AXES = ("x", "y", "c")


def _group_peer(axes, k):
    pos = {a: lax.axis_index(a) for a in AXES}
    idx = 0
    for a in axes:
        idx = idx * 2 + pos[a]
    peer = dict(pos)
    for b, a in enumerate(reversed(axes)):
        if (k >> b) & 1:
            peer[a] = 1 - pos[a]
    return idx, tuple(peer[a] for a in AXES)


MAX_CHUNKS = 4


class _Exchange:
    def __init__(self, xs, axes, mode):
        self.axes, self.mode, self.na, self.n = axes, mode, len(xs), 2 ** len(axes)
        n = self.n
        self.out_shape, self.pieces = [], []
        for x in xs:
            if mode == "gather":
                shape, lead = (n,) + x.shape, x.shape[0]
            elif mode == "scatter":
                shape, lead = x.shape, x.shape[1]
            else:
                shape, lead = (x.shape[0], n * x.shape[1], x.shape[2]), x.shape[0]
            self.out_shape.append(jax.ShapeDtypeStruct(shape, x.dtype))
            big = x.size * x.dtype.itemsize >= (1 << 20)
            if mode == "rows":
                self.pieces.append(lead if lead <= MAX_CHUNKS else 1)
            else:
                self.pieces.append(MAX_CHUNKS if big and lead % (16 * MAX_CHUNKS) == 0 else 1)
        self.in_specs = [pl.BlockSpec(memory_space=pl.ANY)] * self.na
        self.out_specs = [pl.BlockSpec(memory_space=pl.ANY)] * self.na
        self.scratch_shapes = [pltpu.SemaphoreType.DMA((self.na, MAX_CHUNKS, n)),
                               pltpu.SemaphoreType.DMA((self.na, MAX_CHUNKS, n)),
                               pltpu.SemaphoreType.DMA((self.na, MAX_CHUNKS))]

    def _copies(self, x_refs, o_refs, send_sems, recv_sems, local_sems):
        me, _ = _group_peer(self.axes, 0)
        local, remote = [], []
        for a, (x, o) in enumerate(zip(x_refs, o_refs)):
            for c in range(self.pieces[a]):
                if self.mode == "rows":
                    r = x.shape[1]
                    b = slice(None) if self.pieces[a] == 1 else pl.ds(c, 1)
                    src = lambda k, x=x, b=b: x.at[b]
                    dst = o.at[b, pl.ds(me * r, r)]
                else:
                    lead = x.shape[1] if self.mode == "scatter" else x.shape[0]
                    rs = pl.ds(c * (lead // self.pieces[a]), lead // self.pieces[a])
                    if self.mode == "scatter":
                        src = lambda k, x=x, rs=rs: x.at[me ^ k, rs]
                    else:
                        src = lambda k, x=x, rs=rs: x.at[rs]
                    dst = o.at[me, rs]
                local.append(pltpu.make_async_copy(src(0), dst, local_sems.at[a, c]))
                for k in range(1, self.n):
                    remote.append(pltpu.make_async_remote_copy(
                        src_ref=src(k), dst_ref=dst, send_sem=send_sems.at[a, c, k], recv_sem=recv_sems.at[a, c, k],
                        device_id=_group_peer(self.axes, k)[1], device_id_type=MESH))
        return local, remote

    def start(self, x_refs, o_refs, sems):
        local, remote = self._copies(x_refs, o_refs, *sems)
        for cp in local + remote:
            cp.start()

    def wait(self, x_refs, o_refs, sems):
        local, remote = self._copies(x_refs, o_refs, *sems)
        for cp in remote:
            cp.wait_send()
        for cp in remote:
            cp.wait_recv()
        for cp in local:
            cp.wait()


def _exchange(xs, axes, mode, name):
    ex = _Exchange(xs, axes, mode)
    na = ex.na

    def body(*refs):
        ex.start(refs[:na], refs[na:2 * na], refs[2 * na:])
        ex.wait(refs[:na], refs[na:2 * na], refs[2 * na:])

    return pl.pallas_call(body, out_shape=ex.out_shape, in_specs=ex.in_specs, out_specs=ex.out_specs,
                          scratch_shapes=ex.scratch_shapes, name=name)(*xs)


SHARDED_SMALL = (("a_conv_w", 2), ("a_pw_w", 1), ("s5_glu_w", 1), ("c_conv_w", 2), ("d_conv_w", 2))
REPLICATED = ("norm_g", "a_conv_b", "a_ln_g", "a_ln_b", "a_pw_b", "s5_lambda_re", "s5_lambda_im", "s5_b_re",
              "s5_b_im", "s5_c_re", "s5_c_im", "s5_d", "s5_log_dt", "s5_glu_b", "d_a_log", "d_dt_bias",
              "d_norm_g", "final_g")
WEIGHTS = ("norm_g", "w_in", "a_conv_w", "a_conv_b", "a_ln_g", "a_ln_b", "a_pw_w", "a_pw_b", "s5_lambda_re",
           "s5_lambda_im", "s5_b_re", "s5_b_im", "s5_c_re", "s5_c_im", "s5_d", "s5_log_dt", "s5_glu_w",
           "s5_glu_b", "c_conv_w", "d_conv_w", "d_a_log", "d_dt_bias", "d_norm_g", "w_out", "final_g")
LANES = 1024


def _size(shape):
    size = 1
    for d in shape:
        size *= d
    return size


def _slab_rows(shape):
    return -(-_size(shape) // (8 * LANES)) * 8


def _pack(arrs, rows):
    parts = []
    for a in arrs:
        r = _slab_rows(a.shape)
        parts.append(jnp.pad(a.reshape(-1), (0, r * LANES - a.size)).reshape(r, LANES))
    used = sum(p.shape[0] for p in parts)
    if rows > used:
        parts.append(jnp.zeros((rows - used, LANES), parts[0].dtype))
    return jnp.concatenate(parts, axis=0)


def _unpack(slab, shapes):
    out, off = [], 0
    for s in shapes:
        r = _slab_rows(s)
        out.append(slab[off:off + r].reshape(-1)[:_size(s)].reshape(s))
        off += r
    return out


def _rows_for(shapes, mult):
    rows = sum(_slab_rows(s) for s in shapes)
    return -(-rows // mult) * mult


def _row(v, width=None):
    v = v.reshape(1, -1)
    return v if width is None else jnp.pad(v, ((0, 0), (0, width - v.shape[1])))


def _pad_rows(w, rows):
    return jnp.pad(w, ((0, rows - w.shape[0]), (0, 0)))


def _permute_in(w):
    return jnp.concatenate([w[:, :3072], w[:, 3080:N_IN], w[:, 3072:3080],
                            jnp.zeros((w.shape[0], N_INP - N_IN), w.dtype)], axis=1)


def _layer_fwd(x, p, exchange=None):
    proj, h = _proj_fwd(x, p["norm_g"], p["wp"])
    ya = _a_fwd(proj, p["a_cw"], p["a_cb"], p["a_lng"], p["a_lnb"], p["a_pw"], p["a_pwb"])
    ar, ai, bemb, cemb = _s5_prep(*p["s5"])
    yb, sin_all = _b_fwd(proj, ar, ai, bemb, cemb, p["s5_d"], p["glu_w"], p["glu_b"])
    yc = _c_fwd(proj, p["c_cw"])
    (yd, sall, inv), got = _d_fwd(proj, p["d_cw"], p["d_p1"], p["d_p2"], p["d_ng"], exchange)
    xo = _out_fwd(x, ya, yb, yc, yd, p["wo"])
    return xo, dict(x=x, proj=proj, h=h, ys=(ya, yb, yc, yd), sin_all=sin_all, sall=sall, inv=inv,
                    s5=(ar, ai, bemb, cemb)), got


def _layer_bwd(dxo, p, r, exchanges):
    proj = r["proj"]
    ar, ai, bemb, cemb = r["s5"]
    dmix = _out_bwd_x(dxo, p["wo"])
    dwo = _dwout(*r["ys"], dxo)
    dpa, dcw_a, dcb, dlng, dlnb, dpw, dpwb = _a_bwd(proj, dmix, p["a_cw"], p["a_cb"], p["a_lng"], p["a_lnb"],
                                                     p["a_pw"], p["a_pwb"])
    got = {}
    (dpb, dar, dai, dbe, dce, ddsk, dgw, dgb), got["b"] = _b_bwd(
        proj, dmix, r["sin_all"], ar, ai, bemb, cemb, p["s5_d"], p["glu_w"], p["glu_b"], exchanges.get("b"))
    dlre, dlim, dldt, dbre, dbim, dcre, dcim = _s5_prep_bwd(*p["s5"], dar, dai, dbe, dce)
    dpc, dcw_c = _c_bwd(proj, dmix, p["c_cw"])
    (dpd, dcw_d, dp1, dp2, dng), got["d"] = _d_bwd(proj, dmix, r["sall"], r["inv"], p["d_cw"], p["d_p1"],
                                                   p["d_p2"], p["d_ng"], exchanges.get("d"))
    ex_proj = exchanges.get("proj")
    if callable(ex_proj):
        ex_proj = ex_proj(got["d"])
    (dx, dg), got["proj"] = _proj_bwd_x(r["x"], p["norm_g"], dpa, dpb, dpc, dpd, p["wp"], dxo, ex_proj)
    dwa, dwb, dwc, dwd = _dwin(r["h"], dpa, dpb, dpc, dpd)
    dwin = jnp.concatenate([dwa, dwb, dwc, dwd[:, :3 * BR], dwd[:, 4 * BR:4 * BR + 2 * DN_H],
                            dwd[:, 3 * BR:4 * BR]], axis=1)

    def unrows(t, perm):
        return jnp.transpose(t.reshape(S5_H, S5_G, S5_P), perm)

    grads = dict(
        norm_g=dg.reshape(-1), w_in=dwin, a_conv_w=dcw_a[:K_A], a_conv_b=dcb.reshape(-1),
        a_ln_g=dlng.reshape(-1), a_ln_b=dlnb.reshape(-1), a_pw_w=dpw, a_pw_b=dpwb.reshape(-1),
        s5_lambda_re=dlre.reshape(S5_G, S5_P), s5_lambda_im=dlim.reshape(S5_G, S5_P),
        s5_b_re=unrows(dbre, (1, 2, 0)), s5_b_im=unrows(dbim, (1, 2, 0)),
        s5_c_re=unrows(dcre, (1, 0, 2)), s5_c_im=unrows(dcim, (1, 0, 2)),
        s5_d=ddsk.reshape(-1), s5_log_dt=dldt[0, :S5_G], s5_glu_w=dgw, s5_glu_b=dgb.reshape(-1),
        c_conv_w=dcw_c[:K_C], d_conv_w=dcw_d[:K_DN], d_a_log=dp1[0, :DN_H], d_dt_bias=dp2[0, :DN_H],
        d_norm_g=dng.reshape(-1), w_out=dwo)
    return dx, grads, got


def _layer_params(full, wp, wo, l):
    return dict(
        norm_g=_row(full["norm_g"][l]), wp=wp,
        a_cw=_pad_rows(full["a_conv_w"][l], HALO_A), a_cb=_row(full["a_conv_b"][l]),
        a_lng=_row(full["a_ln_g"][l]), a_lnb=_row(full["a_ln_b"][l]), a_pw=full["a_pw_w"][l],
        a_pwb=_row(full["a_pw_b"][l]),
        s5=(_row(full["s5_lambda_re"][l]), _row(full["s5_lambda_im"][l]), _row(full["s5_log_dt"][l], 128),
            jnp.transpose(full["s5_b_re"][l], (2, 0, 1)).reshape(S5_H, NS),
            jnp.transpose(full["s5_b_im"][l], (2, 0, 1)).reshape(S5_H, NS),
            jnp.transpose(full["s5_c_re"][l], (1, 0, 2)).reshape(S5_H, NS),
            jnp.transpose(full["s5_c_im"][l], (1, 0, 2)).reshape(S5_H, NS)),
        s5_d=_row(full["s5_d"][l]), glu_w=full["s5_glu_w"][l], glu_b=_row(full["s5_glu_b"][l]),
        c_cw=_pad_rows(full["c_conv_w"][l], HALO_S), d_cw=_pad_rows(full["d_conv_w"][l], HALO_S),
        d_p1=_row(full["d_a_log"][l], 128), d_p2=_row(full["d_dt_bias"][l], 128),
        d_ng=_row(full["d_norm_g"][l]), wo=wo)


def kernel(x, norm_g, w_in, a_conv_w, a_conv_b, a_ln_g, a_ln_b, a_pw_w, a_pw_b, s5_lambda_re, s5_lambda_im, s5_b_re, s5_b_im, s5_c_re, s5_c_im, s5_d, s5_log_dt, s5_glu_w, s5_glu_b, c_conv_w, d_conv_w, d_a_log, d_dt_bias, d_norm_g, w_out, final_g, loss_target, m_norm_g, m_w_in, m_a_conv_w, m_a_conv_b, m_a_ln_g, m_a_ln_b, m_a_pw_w, m_a_pw_b, m_s5_lambda_re, m_s5_lambda_im, m_s5_b_re, m_s5_b_im, m_s5_c_re, m_s5_c_im, m_s5_d, m_s5_log_dt, m_s5_glu_w, m_s5_glu_b, m_c_conv_w, m_d_conv_w, m_d_a_log, m_d_dt_bias, m_d_norm_g, m_w_out, m_final_g, v_norm_g, v_w_in, v_a_conv_w, v_a_conv_b, v_a_ln_g, v_a_ln_b, v_a_pw_w, v_a_pw_b, v_s5_lambda_re, v_s5_lambda_im, v_s5_b_re, v_s5_b_im, v_s5_c_re, v_s5_c_im, v_s5_d, v_s5_log_dt, v_s5_glu_w, v_s5_glu_b, v_c_conv_w, v_d_conv_w, v_d_a_log, v_d_dt_bias, v_d_norm_g, v_w_out, v_final_g):
    given = dict(locals())
    w = {n: given[n] for n in WEIGHTS}
    m = {n: given["m_" + n] for n in WEIGHTS}
    v = {n: given["v_" + n] for n in WEIGHTS}
    xs, tgt = x[0], loss_target[0]

    n_in, n_out = w["w_in"].shape[2], w["w_out"].shape[1]
    sm_names = [n for n, _ in SHARDED_SMALL]
    sm_shapes = [w[n].shape for n in sm_names]
    sm_rows = _rows_for(sm_shapes, 16)
    win_b, wout_b = w["w_in"].astype(BF), w["w_out"].astype(BF)
    g_in, g_out, g_sm = _exchange([win_b[0], wout_b[0], _pack([w[n] for n in sm_names], sm_rows)],
                                  ("x", "y"), "gather", "gather_first")
    full = dict(w)
    parts = [_unpack(g_sm[j], sm_shapes) for j in range(4)]
    for i, (n, ax) in enumerate(SHARDED_SMALL):
        full[n] = jnp.concatenate([parts[j][i] for j in range(4)], axis=ax)

    saved = []
    h = xs
    for l in range(DEPTH):
        p = _layer_params(full, _permute_in(jnp.concatenate([g_in[j] for j in range(4)], axis=1)),
                          jnp.concatenate([g_out[j] for j in range(4)], axis=0), l)
        nxt = ([win_b[l + 1], wout_b[l + 1]], ("x", "y"), "gather") if l + 1 < DEPTH else None
        h, r, got = _layer_fwd(h, p, nxt)
        saved.append((p, r))
        if got is not None:
            g_in, g_out = got
    loss_tile, dx, dfg = _loss_bwd(h, _row(full["final_g"]), tgt)

    def big_slots(g):
        s_in = jnp.stack([g["w_in"][:, j * n_in:(j + 1) * n_in].astype(BF) for j in range(4)])
        return [s_in.reshape(8, D_MODEL // 2, n_in), g["w_out"].astype(BF).reshape(8, n_out // 2, D_MODEL)]

    def halves(rv):
        return [_sum_slots(rv[0], "sum_w_in")[None], _sum_slots(rv[1], "sum_w_out")[None]]

    layer_grads, summed, pending, arrived = [None] * DEPTH, [None] * DEPTH, None, {}
    for l in reversed(range(DEPTH)):
        p, r = saved[l]
        ex = {}
        if pending is not None:
            ex["d"] = (pending, AXES, "scatter")
        if l + 2 in arrived:
            ex["b"] = (halves(arrived.pop(l + 2)), ("c",), "rows")
        if l == 0:
            ex["proj"] = lambda came: (halves(came), ("c",), "rows")
        dx, layer_grads[l], got = _layer_bwd(dx, p, r, ex)
        if got["b"] is not None:
            summed[l + 2] = got["b"]
        if got["proj"] is not None:
            summed[1] = got["proj"]
        elif got["d"] is not None:
            arrived[l + 1] = got["d"]
        pending = big_slots(layer_grads[l])
    grads = {n: jnp.stack([layer_grads[l][n] for l in range(DEPTH)]) for n in WEIGHTS
             if n not in ("final_g", "w_in", "w_out")}
    grads["final_g"] = dfg.reshape(-1)
    slots = []
    for j in range(4):
        sl = [lax.slice_in_dim(grads[n], j * w[n].shape[ax], (j + 1) * w[n].shape[ax], axis=ax)
              for n, ax in SHARDED_SMALL]
        slots.append(_pack(sl, sm_rows))
    rp_shapes = [w[n].shape for n in REPLICATED] + [(1,)]
    rp_rows = _rows_for(rp_shapes, 64)
    r_in0, r_out0, r_sm, r_rp = _exchange(
        pending + [jnp.stack(slots).reshape(8, sm_rows // 2, LANES),
                   _pack([grads[n] for n in REPLICATED] + [loss_tile[0, 0:1]], rp_rows).reshape(8, rp_rows // 8, LANES)],
        AXES, "scatter", "scatter_last")
    summed[0] = _exchange(halves((r_in0, r_out0)) + [_sum_slots(r_sm, "sum_small")[None]], ("c",), "rows",
                          "gather_halves")
    h_sm = summed[0][2]
    h_in = jnp.concatenate([summed[l][0] for l in range(DEPTH)], axis=0)
    h_out = jnp.concatenate([summed[l][1] for l in range(DEPTH)], axis=0)
    (g_rp,) = _exchange([_sum_slots(r_rp, "sum_replicated")], AXES, "gather", "gather_replicated")
    g_rp = g_rp.reshape(rp_rows, LANES)
    g_sm = h_sm.reshape(sm_rows, LANES)

    out = {}

    def put(name, shape, res):
        for key, t in zip(("delta", "new_m", "new_v"), res):
            out[key + "_" + name] = t.reshape(shape)

    g2 = h_out.reshape(DEPTH * n_out, D_MODEL)
    out["grad_w_out"] = g2.reshape(w["w_out"].shape)
    put("w_out", w["w_out"].shape, _adamw(w["w_out"].reshape(g2.shape), g2, m["w_out"].reshape(g2.shape),
                                          v["w_out"].reshape(g2.shape), "adamw_w_out"))
    cm = lambda a: jnp.transpose(a, (2, 0, 1))
    rm = lambda a: jnp.transpose(a, (1, 2, 0))
    g3 = cm(h_in)
    out["grad_w_in"] = rm(g3)
    for key, t in zip(("delta", "new_m", "new_v"),
                      _adamw(cm(w["w_in"]), g3, cm(m["w_in"]), cm(v["w_in"]), "adamw_w_in", lead=n_in // 6)):
        out[key + "_w_in"] = rm(t)
    zero = jnp.zeros((1,), F32)
    res_sm = _adamw(_pack([w[n] for n in sm_names], sm_rows), g_sm, _pack([m[n] for n in sm_names], sm_rows),
                    _pack([v[n] for n in sm_names], sm_rows), "adamw_small")
    res_rp = _adamw(_pack([w[n] for n in REPLICATED] + [zero], rp_rows), g_rp,
                    _pack([m[n] for n in REPLICATED] + [zero], rp_rows),
                    _pack([v[n] for n in REPLICATED] + [zero], rp_rows), "adamw_replicated")
    for key, sm, rp in (("grad", g_sm, g_rp), ("delta", res_sm[0], res_rp[0]), ("new_m", res_sm[1], res_rp[1]),
                        ("new_v", res_sm[2], res_rp[2])):
        for n, t in zip(sm_names, _unpack(sm, sm_shapes)):
            out[key + "_" + n] = t
        for n, t in zip(REPLICATED, _unpack(rp, rp_shapes[:-1])):
            out[key + "_" + n] = t
    loss = _unpack(g_rp, rp_shapes)[-1].reshape(())
    return (loss, dx[None], *[out["grad_" + n] for n in WEIGHTS], *[out["delta_" + n] for n in WEIGHTS],
            *[out["new_m_" + n] for n in WEIGHTS], *[out["new_v_" + n] for n in WEIGHTS])
```

```python
import functools

import jax
import jax.numpy as jnp
from jax import lax
from jax.experimental import pallas as pl
from jax.experimental.pallas import tpu as pltpu

F32, BF = jnp.float32, jnp.bfloat16
HI = lax.Precision.HIGHEST
MESH = pl.DeviceIdType.MESH

D_MODEL = 1024
BR = 256
DEPTH = 4
N_IN = 3336
N_INP = 3456
COL_A, COL_B, COL_C, COL_D = 0, 768, 1280, 2304
W_A, W_B, W_C, W_D = 768, 512, 1024, 1152
S5_G, S5_H, S5_P = 16, 16, 64
NS = S5_G * S5_P
DN_H, DN_D, DN_C = 4, 64, 64
DN_G_FWD, DN_G_BWD = 8, 4
DN_P = DN_H // 2
DN_STATE = (DN_P, 2 * DN_D, 2 * DN_D)
DN_INV = (DN_P, DN_C, 2 * DN_D)
K_A, K_C, K_DN = 31, 3, 4
HALO_A, HALO_S = 32, 8
EPS = 1e-6
TL = 256
VMEM_LIMIT = 56 * 1024 * 1024

ADAM_LR, ADAM_B1, ADAM_B2, ADAM_EPS, ADAM_WD, ADAM_STEP = 0.001, 0.9, 0.999, 1e-08, 0.01, 10


def _cp(*sem):
    return pltpu.CompilerParams(dimension_semantics=sem, vmem_limit_bytes=VMEM_LIMIT)


def _sigmoid(x):
    return jax.nn.sigmoid(x)


def _silu(x):
    return x * jax.nn.sigmoid(x)


def _rmsnorm(x, g):
    return x * lax.rsqrt(jnp.mean(x * x, axis=-1, keepdims=True) + EPS) * g


@jax.custom_vjp
def _mm(a, w):
    return jnp.dot(a.astype(BF), w.astype(BF), preferred_element_type=F32)


def _mm_f(a, w):
    return _mm(a, w), (a, w)


def _mm_b(res, g):
    a, w = res
    gb = g.astype(BF)
    da = lax.dot_general(gb, w.astype(BF), (((1,), (1,)), ((), ())), preferred_element_type=F32)
    dw = lax.dot_general(a.astype(BF), gb, (((0,), (0,)), ((), ())), preferred_element_type=F32)
    return da, dw


_mm.defvjp(_mm_f, _mm_b)


@jax.custom_vjp
def _mm_nt(a, b):
    return lax.dot_general(a.astype(BF), b.astype(BF), (((1,), (1,)), ((), ())), preferred_element_type=F32)


def _mm_nt_f(a, b):
    return _mm_nt(a, b), (a, b)


def _mm_nt_b(res, g):
    a, b = res
    gb = g.astype(BF)
    da = jnp.dot(gb, b.astype(BF), preferred_element_type=F32)
    db = lax.dot_general(gb, a.astype(BF), (((0,), (0,)), ((), ())), preferred_element_type=F32)
    return da, db


_mm_nt.defvjp(_mm_nt_f, _mm_nt_b)


@jax.custom_vjp
def _mm_tn(a, b):
    return lax.dot_general(a.astype(BF), b.astype(BF), (((0,), (0,)), ((), ())), preferred_element_type=F32)


def _mm_tn_f(a, b):
    return _mm_tn(a, b), (a, b)


def _mm_tn_b(res, g):
    a, b = res
    gb = g.astype(BF)
    da = lax.dot_general(b.astype(BF), gb, (((1,), (1,)), ((), ())), preferred_element_type=F32)
    db = jnp.dot(a.astype(BF), gb, preferred_element_type=F32)
    return da, db


_mm_tn.defvjp(_mm_tn_f, _mm_tn_b)


def _dot_hi(a, b):
    return jnp.dot(a, b, precision=HI, preferred_element_type=F32)


def _split(a):
    hi = a.astype(BF)
    return hi, (a - hi.astype(F32)).astype(BF)


def _dot3(a, b, dims=(((1,), (0,)), ((), ()))):
    ah, al = _split(a)
    bh, bl = _split(b)
    d = functools.partial(lax.dot_general, dimension_numbers=dims, preferred_element_type=F32)
    return d(ah, bh) + d(ah, bl) + d(al, bh)


@jax.custom_vjp
def _mm3(a, b):
    return _dot3(a, b)


def _mm3_f(a, b):
    return _dot3(a, b), (a, b)


def _mm3_b(res, g):
    a, b = res
    return _dot3(g, b, (((1,), (1,)), ((), ()))), _dot3(a, g, (((0,), (0,)), ((), ())))


_mm3.defvjp(_mm3_f, _mm3_b)


def _roll(x, s):
    n = x.shape[0]
    s = s % n
    return x if s == 0 else pltpu.roll(x, s, 0)


def _conv_taps(ext, w_ref, halo, k_taps, tl):
    acc = None
    for k in range(k_taps):
        term = _roll(ext, (k_taps - 1) - k)[halo:halo + tl] * w_ref[k:k + 1, :]
        acc = term if acc is None else acc + term
    return acc


def _conv_taps_bwd(ext, w_ref, dw_ref, dacc, halo, k_taps, tl):
    dpad = jnp.concatenate([dacc, jnp.zeros((halo, dacc.shape[1]), F32)], axis=0)
    dext = None
    for k in range(k_taps):
        r = _roll(ext, (k_taps - 1) - k)[halo:halo + tl]
        dw_ref[k:k + 1, :] += jnp.sum(r * dacc, axis=0, keepdims=True)
        term = _roll(dpad, halo - (k_taps - 1) + k) * w_ref[k:k + 1, :]
        dext = term if dext is None else dext + term
    return dext


def _add_tail(x, tail):
    tl, h = x.shape[0], tail.shape[0]
    return x + jnp.concatenate([jnp.zeros((tl - h, x.shape[1]), F32), tail], axis=0)


def _proj_fwd(x, g, wp):
    L = x.shape[0]

    def body(x_ref, g_ref, w_ref, p_ref, h_ref):
        hb = _rmsnorm(x_ref[...], g_ref[...]).astype(BF)
        h_ref[...] = hb
        p_ref[...] = jnp.dot(hb, w_ref[...], preferred_element_type=F32)

    return pl.pallas_call(
        body, grid=(L // TL,),
        in_specs=[pl.BlockSpec((TL, D_MODEL), lambda i: (i, 0)),
                  pl.BlockSpec((1, D_MODEL), lambda i: (0, 0)),
                  pl.BlockSpec((D_MODEL, N_INP), lambda i: (0, 0))],
        out_specs=[pl.BlockSpec((TL, N_INP), lambda i: (i, 0)),
                   pl.BlockSpec((TL, D_MODEL), lambda i: (i, 0))],
        out_shape=[jax.ShapeDtypeStruct((L, N_INP), F32), jax.ShapeDtypeStruct((L, D_MODEL), BF)],
        name="proj_fwd", compiler_params=_cp("parallel"))(x, g, wp)


def _proj_bwd_x(x, g, dpa, dpb, dpc, dpd, wp, dxo, exchange=None):
    L = x.shape[0]

    def body(x_ref, g_ref, a_ref, b_ref, c_ref, d_ref, w_ref, dxo_ref, dx_ref, dg_ref):
        dh = None
        for ref, c0, wd in ((a_ref, COL_A, W_A), (b_ref, COL_B, W_B), (c_ref, COL_C, W_C), (d_ref, COL_D, W_D)):
            t = lax.dot_general(ref[...].astype(BF), w_ref[:, c0:c0 + wd], (((1,), (1,)), ((), ())),
                                preferred_element_type=F32)
            dh = t if dh is None else dh + t
        _, vj = jax.vjp(_rmsnorm, x_ref[...], g_ref[...])
        dx, dg = vj(dh)
        dx_ref[...] = dxo_ref[...] + dx

        @pl.when(pl.program_id(0) == 0)
        def _():
            dg_ref[...] = jnp.zeros_like(dg_ref)

        dg_ref[...] += dg

    def rows(wd):
        return pl.BlockSpec((TL, wd), lambda i: (i, 0))

    return _sweep_with_exchange(
        body, L // TL,
        in_specs=[rows(D_MODEL), pl.BlockSpec((1, D_MODEL), lambda i: (0, 0)),
                  rows(W_A), rows(W_B), rows(W_C), rows(W_D),
                  pl.BlockSpec((D_MODEL, N_INP), lambda i: (0, 0)), rows(D_MODEL)],
        out_specs=[rows(D_MODEL), pl.BlockSpec((1, D_MODEL), lambda i: (0, 0))],
        out_shape=[jax.ShapeDtypeStruct((L, D_MODEL), F32), jax.ShapeDtypeStruct((1, D_MODEL), F32)],
        scratch_shapes=[], args=(x, g, dpa, dpb, dpc, dpd, wp, dxo), exchange=exchange, name="proj_bwd_x")


def _dwin(h, dpa, dpb, dpc, dpd):
    L = h.shape[0]

    def body(h_ref, a_ref, b_ref, c_ref, d_ref, oa_ref, ob_ref, oc_ref, od_ref):
        outs = (oa_ref, ob_ref, oc_ref, od_ref)

        @pl.when(pl.program_id(0) == 0)
        def _():
            for o in outs:
                o[...] = jnp.zeros_like(o)

        ht = h_ref[...].T
        for ref, o in zip((a_ref, b_ref, c_ref, d_ref), outs):
            o[...] += jnp.dot(ht, ref[...].astype(BF), preferred_element_type=F32)

    def rows(wd):
        return pl.BlockSpec((TL, wd), lambda i: (i, 0))

    def whole(wd):
        return pl.BlockSpec((D_MODEL, wd), lambda i: (0, 0))

    widths = (W_A, W_B, W_C, W_D)
    return pl.pallas_call(
        body, grid=(L // TL,),
        in_specs=[rows(D_MODEL)] + [rows(wd) for wd in widths],
        out_specs=[whole(wd) for wd in widths],
        out_shape=[jax.ShapeDtypeStruct((D_MODEL, wd), F32) for wd in widths],
        name="dwin", compiler_params=_cp("arbitrary"))(h, dpa, dpb, dpc, dpd)


def _dwout(ya, yb, yc, yd, dxo):
    L = dxo.shape[0]
    tk, tn = min(512, L), 512

    def body(a_ref, b_ref, c_ref, d_ref, g_ref, o_ref):
        @pl.when(pl.program_id(1) == 0)
        def _():
            o_ref[...] = jnp.zeros_like(o_ref)

        gb = g_ref[...].astype(BF)
        for j, ref in enumerate((a_ref, b_ref, c_ref, d_ref)):
            o_ref[j * BR:(j + 1) * BR, :] += lax.dot_general(ref[...].astype(BF), gb, (((0,), (0,)), ((), ())),
                                                             preferred_element_type=F32)

    ys = pl.BlockSpec((tk, BR), lambda j, t: (t, 0))
    return pl.pallas_call(
        body, grid=(D_MODEL // tn, L // tk),
        in_specs=[ys, ys, ys, ys, pl.BlockSpec((tk, tn), lambda j, t: (t, j))],
        out_specs=pl.BlockSpec((D_MODEL, tn), lambda j, t: (0, j)),
        out_shape=jax.ShapeDtypeStruct((D_MODEL, D_MODEL), F32),
        name="dwout", compiler_params=_cp("parallel", "arbitrary"))(ya, yb, yc, yd, dxo)


def _out_fwd(x, ya, yb, yc, yd, wo):
    L = x.shape[0]

    def body(x_ref, a_ref, b_ref, c_ref, d_ref, w_ref, o_ref):
        acc = x_ref[...]
        for j, ref in enumerate((a_ref, b_ref, c_ref, d_ref)):
            acc = acc + jnp.dot(ref[...].astype(BF), w_ref[j * BR:(j + 1) * BR, :], preferred_element_type=F32)
        o_ref[...] = acc

    def rows(wd):
        return pl.BlockSpec((TL, wd), lambda i: (i, 0))

    return pl.pallas_call(
        body, grid=(L // TL,),
        in_specs=[rows(D_MODEL), rows(BR), rows(BR), rows(BR), rows(BR),
                  pl.BlockSpec((D_MODEL, D_MODEL), lambda i: (0, 0))],
        out_specs=rows(D_MODEL), out_shape=jax.ShapeDtypeStruct((L, D_MODEL), F32),
        name="out_fwd", compiler_params=_cp("parallel"))(x, ya, yb, yc, yd, wo)


def _out_bwd_x(dxo, wo):
    L = dxo.shape[0]

    def body(d_ref, w_ref, o_ref):
        o_ref[...] = lax.dot_general(d_ref[...].astype(BF), w_ref[...], (((1,), (1,)), ((), ())),
                                     preferred_element_type=F32)

    return pl.pallas_call(
        body, grid=(L // TL,),
        in_specs=[pl.BlockSpec((TL, D_MODEL), lambda i: (i, 0)), pl.BlockSpec((D_MODEL, D_MODEL), lambda i: (0, 0))],
        out_specs=pl.BlockSpec((TL, D_MODEL), lambda i: (i, 0)),
        out_shape=jax.ShapeDtypeStruct((L, D_MODEL), F32),
        name="out_bwd_x", compiler_params=_cp("parallel"))(dxo, wo)


def _loss_bwd(x, g, tgt):
    L = x.shape[0]

    def f(xv, gv, tv):
        err = _rmsnorm(xv, gv) - tv
        return 0.5 * jnp.sum(jnp.mean(err * err, axis=-1, keepdims=True), axis=0, keepdims=True)

    def body(x_ref, g_ref, t_ref, loss_ref, dx_ref, dg_ref):
        tv = t_ref[...]
        loss, vj = jax.vjp(lambda a, b: f(a, b, tv), x_ref[...], g_ref[...])
        dx, dg = vj(jnp.ones((1, 1), F32))
        dx_ref[...] = dx

        @pl.when(pl.program_id(0) == 0)
        def _():
            dg_ref[...] = jnp.zeros_like(dg_ref)
            loss_ref[...] = jnp.zeros_like(loss_ref)

        dg_ref[...] += dg
        loss_ref[...] += jnp.broadcast_to(loss, loss_ref.shape)

    return pl.pallas_call(
        body, grid=(L // TL,),
        in_specs=[pl.BlockSpec((TL, D_MODEL), lambda i: (i, 0)), pl.BlockSpec((1, D_MODEL), lambda i: (0, 0)),
                  pl.BlockSpec((TL, D_MODEL), lambda i: (i, 0))],
        out_specs=[pl.BlockSpec((8, 128), lambda i: (0, 0)), pl.BlockSpec((TL, D_MODEL), lambda i: (i, 0)),
                   pl.BlockSpec((1, D_MODEL), lambda i: (0, 0))],
        out_shape=[jax.ShapeDtypeStruct((8, 128), F32), jax.ShapeDtypeStruct((L, D_MODEL), F32),
                   jax.ShapeDtypeStruct((1, D_MODEL), F32)],
        name="loss_bwd", compiler_params=_cp("arbitrary"))(x, g, tgt)


def _a_pre(val, gate):
    return val * _sigmoid(gate)


def _a_post(acc, az, cb, lng, lnb, pw, pwb):
    t = acc + cb
    mu = jnp.mean(t, axis=-1, keepdims=True)
    xc = t - mu
    ln = xc * lax.rsqrt(jnp.mean(xc * xc, axis=-1, keepdims=True) + EPS) * lng + lnb
    return (_mm(_silu(ln), pw) + pwb) * _silu(az)


def _halo_map(tl, halo, col):
    r = tl // halo
    return lambda i: (jnp.maximum(i * r - 1, 0), col)


def _a_fwd(proj, cw, cb, lng, lnb, pw, pwb):
    L = proj.shape[0]

    def body(vg_ref, az_ref, hvg_ref, cw_ref, cb_ref, lng_ref, lnb_ref, pw_ref, pwb_ref, o_ref):
        keep = (pl.program_id(0) > 0).astype(F32)
        a_h = _a_pre(hvg_ref[:, 0:BR], hvg_ref[:, BR:2 * BR]) * keep
        a_t = _a_pre(vg_ref[:, 0:BR], vg_ref[:, BR:2 * BR])
        ext = jnp.concatenate([a_h, a_t], axis=0)
        acc = _conv_taps(ext, cw_ref, HALO_A, K_A, TL)
        o_ref[...] = _a_post(acc, az_ref[...], cb_ref[...], lng_ref[...], lnb_ref[...], pw_ref[...], pwb_ref[...])

    vec = pl.BlockSpec((1, BR), lambda i: (0, 0))
    return pl.pallas_call(
        body, grid=(L // TL,),
        in_specs=[pl.BlockSpec((TL, 2 * BR), lambda i: (i, 0)), pl.BlockSpec((TL, BR), lambda i: (i, 2)),
                  pl.BlockSpec((HALO_A, 2 * BR), _halo_map(TL, HALO_A, 0)),
                  pl.BlockSpec((HALO_A, BR), lambda i: (0, 0)), vec, vec, vec,
                  pl.BlockSpec((BR, BR), lambda i: (0, 0)), vec],
        out_specs=pl.BlockSpec((TL, BR), lambda i: (i, 0)),
        out_shape=jax.ShapeDtypeStruct((L, BR), F32),
        name="a_fwd", compiler_params=_cp("parallel"))(proj, proj, proj, cw, cb, lng, lnb, pw, pwb)


def _a_bwd(proj, dmix, cw, cb, lng, lnb, pw, pwb):
    L = proj.shape[0]
    n = L // TL

    def body(vg_ref, az_ref, hvg_ref, dy_ref, cw_ref, cb_ref, lng_ref, lnb_ref, pw_ref, pwb_ref,
             dp_ref, dcw_ref, dcb_ref, dlng_ref, dlnb_ref, dpw_ref, dpwb_ref, carry_ref):
        i = pl.program_id(0)

        @pl.when(i == 0)
        def _():
            carry_ref[...] = jnp.zeros_like(carry_ref)
            for r in (dcw_ref, dcb_ref, dlng_ref, dlnb_ref, dpw_ref, dpwb_ref):
                r[...] = jnp.zeros_like(r)

        keep = (i < n - 1).astype(F32)
        val, gate = vg_ref[:, 0:BR], vg_ref[:, BR:2 * BR]
        a_h = _a_pre(hvg_ref[:, 0:BR], hvg_ref[:, BR:2 * BR]) * keep
        a_t, vj_pre = jax.vjp(_a_pre, val, gate)
        ext = jnp.concatenate([a_h, a_t], axis=0)
        acc = _conv_taps(ext, cw_ref, HALO_A, K_A, TL)
        _, vj_post = jax.vjp(_a_post, acc, az_ref[...], cb_ref[...], lng_ref[...], lnb_ref[...], pw_ref[...],
                             pwb_ref[...])
        dacc, daz, dcb, dlng, dlnb, dpw, dpwb = vj_post(dy_ref[...])
        dext = _conv_taps_bwd(ext, cw_ref, dcw_ref, dacc, HALO_A, K_A, TL)
        da = _add_tail(dext[HALO_A:], carry_ref[...])
        carry_ref[...] = dext[:HALO_A]
        dval, dgate = vj_pre(da)
        dp_ref[:, 0:BR] = dval
        dp_ref[:, BR:2 * BR] = dgate
        dp_ref[:, 2 * BR:3 * BR] = daz
        dcb_ref[...] += dcb
        dlng_ref[...] += dlng
        dlnb_ref[...] += dlnb
        dpw_ref[...] += dpw
        dpwb_ref[...] += dpwb

    rev = lambda i: n - 1 - i
    vec = pl.BlockSpec((1, BR), lambda i: (0, 0))
    hmap = _halo_map(TL, HALO_A, 0)
    return pl.pallas_call(
        body, grid=(n,),
        in_specs=[pl.BlockSpec((TL, 2 * BR), lambda i: (rev(i), 0)), pl.BlockSpec((TL, BR), lambda i: (rev(i), 2)),
                  pl.BlockSpec((HALO_A, 2 * BR), lambda i: hmap(rev(i))),
                  pl.BlockSpec((TL, BR), lambda i: (rev(i), 0)),
                  pl.BlockSpec((HALO_A, BR), lambda i: (0, 0)), vec, vec, vec,
                  pl.BlockSpec((BR, BR), lambda i: (0, 0)), vec],
        out_specs=[pl.BlockSpec((TL, W_A), lambda i: (rev(i), 0)),
                   pl.BlockSpec((HALO_A, BR), lambda i: (0, 0)), vec, vec, vec,
                   pl.BlockSpec((BR, BR), lambda i: (0, 0)), vec],
        out_shape=[jax.ShapeDtypeStruct((L, W_A), F32), jax.ShapeDtypeStruct((HALO_A, BR), F32)]
        + [jax.ShapeDtypeStruct((1, BR), F32)] * 3
        + [jax.ShapeDtypeStruct((BR, BR), F32), jax.ShapeDtypeStruct((1, BR), F32)],
        scratch_shapes=[pltpu.VMEM((HALO_A, BR), F32)],
        name="a_bwd", compiler_params=_cp("arbitrary"))(proj, proj, proj, dmix, cw, cb, lng, lnb, pw, pwb)


def _c_pre(cg, xc):
    return cg * xc


def _c_post(acc, bg, cz):
    return bg * acc * _silu(cz)


def _c_fwd(proj, cw):
    L = proj.shape[0]

    def body(bg_ref, cx_ref, cz_ref, hcx_ref, cw_ref, o_ref):
        keep = (pl.program_id(0) > 0).astype(F32)
        p_h = _c_pre(hcx_ref[:, 0:BR], hcx_ref[:, BR:2 * BR]) * keep
        p_t = _c_pre(cx_ref[:, 0:BR], cx_ref[:, BR:2 * BR])
        ext = jnp.concatenate([p_h, p_t], axis=0)
        acc = _conv_taps(ext, cw_ref, HALO_S, K_C, TL)
        o_ref[...] = _c_post(acc, bg_ref[...], cz_ref[...])

    return pl.pallas_call(
        body, grid=(L // TL,),
        in_specs=[pl.BlockSpec((TL, BR), lambda i: (i, 5)), pl.BlockSpec((TL, 2 * BR), lambda i: (i, 3)),
                  pl.BlockSpec((TL, BR), lambda i: (i, 8)),
                  pl.BlockSpec((HALO_S, 2 * BR), _halo_map(TL, HALO_S, 3)),
                  pl.BlockSpec((HALO_S, BR), lambda i: (0, 0))],
        out_specs=pl.BlockSpec((TL, BR), lambda i: (i, 0)),
        out_shape=jax.ShapeDtypeStruct((L, BR), F32),
        name="c_fwd", compiler_params=_cp("parallel"))(proj, proj, proj, proj, cw)


def _c_bwd(proj, dmix, cw):
    L = proj.shape[0]
    n = L // TL

    def body(bg_ref, cx_ref, cz_ref, hcx_ref, dy_ref, cw_ref, dp_ref, dcw_ref, carry_ref):
        i = pl.program_id(0)

        @pl.when(i == 0)
        def _():
            carry_ref[...] = jnp.zeros_like(carry_ref)
            dcw_ref[...] = jnp.zeros_like(dcw_ref)

        keep = (i < n - 1).astype(F32)
        p_h = _c_pre(hcx_ref[:, 0:BR], hcx_ref[:, BR:2 * BR]) * keep
        p_t, vj_pre = jax.vjp(_c_pre, cx_ref[:, 0:BR], cx_ref[:, BR:2 * BR])
        ext = jnp.concatenate([p_h, p_t], axis=0)
        acc = _conv_taps(ext, cw_ref, HALO_S, K_C, TL)
        _, vj_post = jax.vjp(_c_post, acc, bg_ref[...], cz_ref[...])
        dacc, dbg, dcz = vj_post(dy_ref[...])
        dext = _conv_taps_bwd(ext, cw_ref, dcw_ref, dacc, HALO_S, K_C, TL)
        dp = _add_tail(dext[HALO_S:], carry_ref[...])
        carry_ref[...] = dext[:HALO_S]
        dcg, dxc = vj_pre(dp)
        dp_ref[:, 0:BR] = dbg
        dp_ref[:, BR:2 * BR] = dcg
        dp_ref[:, 2 * BR:3 * BR] = dxc
        dp_ref[:, 3 * BR:4 * BR] = dcz

    rev = lambda i: n - 1 - i
    hmap = _halo_map(TL, HALO_S, 3)
    return pl.pallas_call(
        body, grid=(n,),
        in_specs=[pl.BlockSpec((TL, BR), lambda i: (rev(i), 5)), pl.BlockSpec((TL, 2 * BR), lambda i: (rev(i), 3)),
                  pl.BlockSpec((TL, BR), lambda i: (rev(i), 8)),
                  pl.BlockSpec((HALO_S, 2 * BR), lambda i: hmap(rev(i))),
                  pl.BlockSpec((TL, BR), lambda i: (rev(i), 2)),
                  pl.BlockSpec((HALO_S, BR), lambda i: (0, 0))],
        out_specs=[pl.BlockSpec((TL, W_C), lambda i: (rev(i), 0)), pl.BlockSpec((HALO_S, BR), lambda i: (0, 0))],
        out_shape=[jax.ShapeDtypeStruct((L, W_C), F32), jax.ShapeDtypeStruct((HALO_S, BR), F32)],
        scratch_shapes=[pltpu.VMEM((HALO_S, BR), F32)],
        name="c_bwd", compiler_params=_cp("arbitrary"))(proj, proj, proj, proj, dmix, cw)


def _s5_prep_fn(lre, lim, ldt, bre, bim, cre, cim):
    grp = lax.broadcasted_iota(jnp.int32, (128, NS), 0)
    lane = lax.broadcasted_iota(jnp.int32, (128, NS), 1)
    expand = (grp == lane // S5_P).astype(F32)
    dt = jnp.exp(_dot_hi(jnp.broadcast_to(ldt, (8, 128)), expand)[0:1])
    lr = jnp.minimum(lre, -1e-4)
    mag = jnp.exp(lr * dt)
    ar = mag * jnp.cos(lim * dt)
    ai = mag * jnp.sin(lim * dt)
    den = lr * lr + lim * lim
    fr = ((ar - 1.0) * lr + ai * lim) / den
    fi = (ai * lr - (ar - 1.0) * lim) / den
    bbr = fr * bre - fi * bim
    bbi = fr * bim + fi * bre
    row = lax.broadcasted_iota(jnp.int32, (BR, NS), 0)
    col = lax.broadcasted_iota(jnp.int32, (BR, NS), 1)
    blk = (row // S5_H == col // S5_P).astype(F32)

    def embed(t):
        return jnp.concatenate([t] * S5_G, axis=0) * blk

    bemb = jnp.concatenate([embed(bbr), embed(bbi)], axis=1)
    cemb = jnp.concatenate([embed(cre), embed(-cim)], axis=1)
    return ar, ai, bemb, cemb


def _s5_prep(lre, lim, ldt, bre, bim, cre, cim):
    def body(*refs):
        outs = _s5_prep_fn(*[r[...] for r in refs[:7]])
        for r, o in zip(refs[7:], outs):
            r[...] = o

    return pl.pallas_call(
        body,
        out_shape=[jax.ShapeDtypeStruct((1, NS), F32)] * 2 + [jax.ShapeDtypeStruct((BR, 2 * NS), F32)] * 2,
        name="s5_prep", compiler_params=pltpu.CompilerParams(vmem_limit_bytes=VMEM_LIMIT),
    )(lre, lim, ldt, bre, bim, cre, cim)


def _s5_prep_bwd(lre, lim, ldt, bre, bim, cre, cim, dar, dai, dbemb, dcemb):
    def body(*refs):
        _, vj = jax.vjp(_s5_prep_fn, *[r[...] for r in refs[:7]])
        grads = vj(tuple(r[...] for r in refs[7:11]))
        for r, o in zip(refs[11:], grads):
            r[...] = o

    return pl.pallas_call(
        body,
        out_shape=[jax.ShapeDtypeStruct((1, NS), F32)] * 2 + [jax.ShapeDtypeStruct((1, 128), F32)]
        + [jax.ShapeDtypeStruct((S5_H, NS), F32)] * 4,
        name="s5_prep_bwd", compiler_params=pltpu.CompilerParams(vmem_limit_bytes=VMEM_LIMIT),
    )(lre, lim, ldt, bre, bim, cre, cim, dar, dai, dbemb, dcemb)


def _s5_scan(xr, xi, ar, ai, reverse):
    n = xr.shape[0]
    row = lax.broadcasted_iota(jnp.int32, (n, 1), 0)
    pr, pi = ar, ai
    d = 1
    while d < n:
        if d % 8:
            if reverse:
                m = row < n - d
                sr = jnp.where(m, _roll(xr, n - d), 0.0)
                si = jnp.where(m, _roll(xi, n - d), 0.0)
            else:
                m = row >= d
                sr = jnp.where(m, _roll(xr, d), 0.0)
                si = jnp.where(m, _roll(xi, d), 0.0)
            xr, xi = xr + pr * sr - pi * si, xi + pr * si + pi * sr
        elif reverse:
            sr, si = xr[d:], xi[d:]
            xr, xi = (jnp.concatenate([xr[:n - d] + pr * sr - pi * si, xr[n - d:]], axis=0),
                      jnp.concatenate([xi[:n - d] + pr * si + pi * sr, xi[n - d:]], axis=0))
        else:
            sr, si = xr[:n - d], xi[:n - d]
            xr, xi = (jnp.concatenate([xr[:d], xr[d:] + pr * sr - pi * si], axis=0),
                      jnp.concatenate([xi[:d], xi[d:] + pr * si + pi * sr], axis=0))
        pr, pi = pr * pr - pi * pi, 2.0 * pr * pi
        d *= 2
    return xr, xi


def _s5_states(u, bemb_b, ar, ai, sin_r, sin_i):
    bu = jnp.dot(u.astype(BF), bemb_b, preferred_element_type=F32)
    first = lax.broadcasted_iota(jnp.int32, (u.shape[0], 1), 0) == 0
    xr = bu[:, :NS] + jnp.where(first, ar * sin_r - ai * sin_i, 0.0)
    xi = bu[:, NS:] + jnp.where(first, ar * sin_i + ai * sin_r, 0.0)
    return _s5_scan(xr, xi, ar, ai, False)


def _b_post(yssm, u, bz, dsk, gw, gb):
    z = jax.nn.gelu(yssm + dsk * u)
    return z * _sigmoid(_mm(z, gw) + gb) * _silu(bz)


def _b_fwd(proj, ar, ai, bemb, cemb, dsk, gw, gb, exchange=None):
    L = proj.shape[0]
    n = L // TL

    def body(u_ref, bz_ref, ar_ref, ai_ref, be_ref, ce_ref, dsk_ref, gw_ref, gb_ref, o_ref, sin_ref, carry_ref):
        @pl.when(pl.program_id(0) == 0)
        def _():
            carry_ref[...] = jnp.zeros_like(carry_ref)

        sin = carry_ref[...]
        sin_ref[0] = sin
        u = u_ref[...]
        sr, si = _s5_states(u, be_ref[...].astype(BF), ar_ref[...], ai_ref[...], sin[:, :NS], sin[:, NS:])
        carry_ref[:, :NS] = sr[TL - 1:TL]
        carry_ref[:, NS:] = si[TL - 1:TL]
        s = jnp.concatenate([sr, si], axis=1).astype(BF)
        yssm = lax.dot_general(s, ce_ref[...].astype(BF), (((1,), (1,)), ((), ())), preferred_element_type=F32)
        o_ref[...] = _b_post(yssm, u, bz_ref[...], dsk_ref[...], gw_ref[...], gb_ref[...])

    vec = pl.BlockSpec((1, BR), lambda i: (0, 0))
    svec = pl.BlockSpec((1, NS), lambda i: (0, 0))
    emb = pl.BlockSpec((BR, 2 * NS), lambda i: (0, 0))
    return _sweep_with_exchange(
        body, n,
        in_specs=[pl.BlockSpec((TL, BR), lambda i: (i, 3)), pl.BlockSpec((TL, BR), lambda i: (i, 4)),
                  svec, svec, emb, emb, vec, pl.BlockSpec((BR, BR), lambda i: (0, 0)), vec],
        out_specs=[pl.BlockSpec((TL, BR), lambda i: (i, 0)), pl.BlockSpec((1, 1, 2 * NS), lambda i: (i, 0, 0))],
        out_shape=[jax.ShapeDtypeStruct((L, BR), F32), jax.ShapeDtypeStruct((n, 1, 2 * NS), F32)],
        scratch_shapes=[pltpu.VMEM((1, 2 * NS), F32)],
        args=(proj, proj, ar, ai, bemb, cemb, dsk, gw, gb), exchange=exchange, name="b_fwd")


def _b_bwd(proj, dmix, sin_all, ar, ai, bemb, cemb, dsk, gw, gb, exchange=None):
    L = proj.shape[0]
    n = L // TL

    def body(u_ref, bz_ref, dy_ref, sin_ref, ar_ref, ai_ref, be_ref, ce_ref, dsk_ref, gw_ref, gb_ref,
             dp_ref, dar_ref, dai_ref, dbe_ref, dce_ref, ddsk_ref, dgw_ref, dgb_ref, carry_ref):
        i = pl.program_id(0)

        @pl.when(i == 0)
        def _():
            carry_ref[...] = jnp.zeros_like(carry_ref)
            for r in (dar_ref, dai_ref, dbe_ref, dce_ref, ddsk_ref, dgw_ref, dgb_ref):
                r[...] = jnp.zeros_like(r)

        u = u_ref[...]
        ar, ai = ar_ref[...], ai_ref[...]
        be_b, ce_b = be_ref[...].astype(BF), ce_ref[...].astype(BF)
        sin = sin_ref[0]
        sr, si = _s5_states(u, be_b, ar, ai, sin[:, :NS], sin[:, NS:])
        s_b = jnp.concatenate([sr, si], axis=1).astype(BF)
        yssm = lax.dot_general(s_b, ce_b, (((1,), (1,)), ((), ())), preferred_element_type=F32)
        _, vj = jax.vjp(_b_post, yssm, u, bz_ref[...], dsk_ref[...], gw_ref[...], gb_ref[...])
        dyssm, du, dbz, ddsk, dgw, dgb = vj(dy_ref[...])
        dy_b = dyssm.astype(BF)
        dce_ref[...] += lax.dot_general(dy_b, s_b, (((0,), (0,)), ((), ())), preferred_element_type=F32)
        gs = jnp.dot(dy_b, ce_b, preferred_element_type=F32)
        last = lax.broadcasted_iota(jnp.int32, (TL, 1), 0) == TL - 1
        cr, ci = carry_ref[:, :NS], carry_ref[:, NS:]
        gr = gs[:, :NS] + jnp.where(last, ar * cr + ai * ci, 0.0)
        gi = gs[:, NS:] + jnp.where(last, ar * ci - ai * cr, 0.0)
        dsr, dsi = _s5_scan(gr, gi, ar, -ai, True)
        carry_ref[:, :NS] = dsr[0:1]
        carry_ref[:, NS:] = dsi[0:1]
        first = lax.broadcasted_iota(jnp.int32, (TL, 1), 0) == 0
        pr = jnp.where(first, sin[:, :NS], _roll(sr, 1))
        pi = jnp.where(first, sin[:, NS:], _roll(si, 1))
        dar_ref[...] += jnp.sum(dsr * pr + dsi * pi, axis=0, keepdims=True)
        dai_ref[...] += jnp.sum(dsi * pr - dsr * pi, axis=0, keepdims=True)
        ds_b = jnp.concatenate([dsr, dsi], axis=1).astype(BF)
        dbe_ref[...] += lax.dot_general(u.astype(BF), ds_b, (((0,), (0,)), ((), ())), preferred_element_type=F32)
        du = du + lax.dot_general(ds_b, be_b, (((1,), (1,)), ((), ())), preferred_element_type=F32)
        dp_ref[:, 0:BR] = du
        dp_ref[:, BR:2 * BR] = dbz
        ddsk_ref[...] += ddsk
        dgw_ref[...] += dgw
        dgb_ref[...] += dgb

    rev = lambda i: n - 1 - i
    vec = pl.BlockSpec((1, BR), lambda i: (0, 0))
    svec = pl.BlockSpec((1, NS), lambda i: (0, 0))
    emb = pl.BlockSpec((BR, 2 * NS), lambda i: (0, 0))
    mat = pl.BlockSpec((BR, BR), lambda i: (0, 0))
    return _sweep_with_exchange(
        body, n,
        in_specs=[pl.BlockSpec((TL, BR), lambda i: (rev(i), 3)), pl.BlockSpec((TL, BR), lambda i: (rev(i), 4)),
                  pl.BlockSpec((TL, BR), lambda i: (rev(i), 1)),
                  pl.BlockSpec((1, 1, 2 * NS), lambda i: (rev(i), 0, 0)),
                  svec, svec, emb, emb, vec, mat, vec],
        out_specs=[pl.BlockSpec((TL, W_B), lambda i: (rev(i), 0)), svec, svec, emb, emb, vec, mat, vec],
        out_shape=[jax.ShapeDtypeStruct((L, W_B), F32)] + [jax.ShapeDtypeStruct((1, NS), F32)] * 2
        + [jax.ShapeDtypeStruct((BR, 2 * NS), F32)] * 2
        + [jax.ShapeDtypeStruct((1, BR), F32), jax.ShapeDtypeStruct((BR, BR), F32), jax.ShapeDtypeStruct((1, BR), F32)],
        scratch_shapes=[pltpu.VMEM((1, 2 * NS), F32)],
        args=(proj, proj, dmix, sin_all, ar, ai, bemb, cemb, dsk, gw, gb), exchange=exchange, name="b_bwd")


def _half_masks(rows):
    lane = lax.broadcasted_iota(jnp.int32, (rows, 2 * DN_D), 1)
    return lane < DN_D, lane >= DN_D


def _bd(x):
    left, right = _half_masks(x.shape[0])
    return jnp.concatenate([jnp.where(left, x, 0.0), jnp.where(right, x, 0.0)], axis=0)


@jax.custom_vjp
def _segsum(x):
    r = lax.broadcasted_iota(jnp.int32, (2 * DN_D, 2 * DN_D), 0) // DN_D
    c = lax.broadcasted_iota(jnp.int32, (2 * DN_D, 2 * DN_D), 1) // DN_D
    ones = (r == c).astype(BF)
    hi, lo = _split(x)
    return jnp.dot(hi, ones, preferred_element_type=F32) + jnp.dot(lo, ones, preferred_element_type=F32)


_segsum.defvjp(lambda x: (_segsum(x), None), lambda _, g: (_segsum(g),))


def _pair_t(x):
    t = _bd(x).T
    return t[:DN_D] + t[DN_D:]


@jax.custom_vjp
def _pair_inv(lms):
    n = DN_D
    row = lax.broadcasted_iota(jnp.int32, (n, 2 * n), 0)
    col = lax.broadcasted_iota(jnp.int32, (n, 2 * n), 1) % n
    eye = (row == col).astype(F32)
    accs = [eye - lm for lm in lms]
    pws = list(lms)
    k = 2
    while k < n:
        pws = [_dot3(p, _bd(p)) for p in pws]
        accs = [a + _dot3(a, _bd(p)) for a, p in zip(accs, pws)]
        k *= 2
    return tuple(accs)


def _pi_b(a, g):
    ats = [_pair_t(x) for x in a]
    tmp = [_dot3(at, _bd(gi)) for at, gi in zip(ats, g)]
    return (tuple(-_dot3(t, _bd(at)) for t, at in zip(tmp, ats)),)


def _pi_f(lms):
    a = _pair_inv(lms)
    return a, a


_pair_inv.defvjp(_pi_f, _pi_b)


@jax.custom_vjp
def _pair_known_inverse(lms, inv):
    return inv


_pair_known_inverse.defvjp(lambda lms, inv: (inv, inv),
                           lambda a, g: (_pi_b(a, g)[0], tuple(jnp.zeros_like(x) for x in a)))


def _d_tile(cq, ab, dz, sb0, sb1, p1, p2, ng, known=None):
    c = DN_C
    chunks = range(cq.shape[0] // c)
    units = [(g, p) for g in chunks for p in range(DN_P)]
    n = range(len(units))
    qkv = _silu(cq)
    gall = -jnp.exp(p1) * jax.nn.softplus(ab + p2)
    ball = _sigmoid(ab)
    left, _ = _half_masks(c)
    row = lax.broadcasted_iota(jnp.int32, (c, 2 * c), 0)
    col = lax.broadcasted_iota(jnp.int32, (c, 2 * c), 1) % c
    causal, strict = row >= col, row > col
    sq = lax.broadcasted_iota(jnp.int32, (c, c), 0) >= lax.broadcasted_iota(jnp.int32, (c, c), 1)
    gc_all = [_dot_hi(sq.astype(F32), gall[g * c:(g + 1) * c]) for g in chunks]
    gc_t = [t.T for t in gc_all]
    bdm = (lax.broadcasted_iota(jnp.int32, (2 * c, 2 * c), 0) // c
           == lax.broadcasted_iota(jnp.int32, (2 * c, 2 * c), 1) // c).astype(F32)

    def two(t, base, g, p):
        return t[g * c:(g + 1) * c, base + 2 * p * DN_D:base + 2 * (p + 1) * DN_D]

    def per_head(t, off, p):
        return jnp.where(left, t[:, off + 2 * p:off + 2 * p + 1], t[:, off + 2 * p + 1:off + 2 * p + 2])

    q = [two(qkv, 0, g, p) for g, p in units]
    k = [two(qkv, BR, g, p) for g, p in units]
    v = [two(qkv, 2 * BR, g, p) for g, p in units]
    q = [t * lax.rsqrt(_segsum(t * t) + EPS) * (DN_D ** -0.5) for t in q]
    k = [t * lax.rsqrt(_segsum(t * t) + EPS) for t in k]
    g2 = [per_head(gc_all[g], 0, p) for g, p in units]
    beta = [per_head(ball[g * c:(g + 1) * c], DN_H, p) for g, p in units]
    grow = [jnp.concatenate([gc_t[g][2 * p:2 * p + 1, :], gc_t[g][2 * p + 1:2 * p + 2, :]], axis=1) for g, p in units]
    decay = [jnp.where(causal, jnp.exp(jnp.where(causal, g2[u] - grow[u], 0.0)), 0.0) for u in n]
    kb = [k[u] * beta[u] for u in n]
    kbd = [_bd(t) for t in k]
    lm = [jnp.where(strict, _mm_nt(kb[u], kbd[u]) * decay[u], 0.0) for u in n]
    ainv = _pair_inv(tuple(lm)) if known is None else _pair_known_inverse(tuple(lm), known)
    egc = [jnp.exp(t) for t in g2]
    uw = [_mm3(ainv[u], jnp.concatenate([_bd(v[u] * beta[u]), _bd(kb[u] * egc[u])], axis=1)) for u in n]
    attn = [_mm_nt(q[u], kbd[u]) * decay[u] for u in n]
    glast = [t[c - 1:c, :] for t in g2]
    kd = [k[u] * jnp.exp(glast[u] - g2[u]) for u in n]
    qd = [q[u] * egc[u] for u in n]
    ng2 = jnp.concatenate([ng, ng], axis=1)
    sbd, starts, outs = [sb0, sb1], [], []
    for g in chunks:
        starts.append(tuple(sbd))
        us = [g * DN_P + p for p in range(DN_P)]
        vnew = [uw[u][:, :2 * DN_D] - _mm(uw[u][:, 2 * DN_D:], sbd[p]) for p, u in enumerate(us)]
        o = [_mm(qd[u], sbd[p]) + _mm(attn[u], _bd(vnew[p])) for p, u in enumerate(us)]
        sbd = [sbd[p] * jnp.exp(glast[u]) + _mm_tn(kd[u], vnew[p]) * bdm for p, u in enumerate(us)]
        outs.append(jnp.concatenate([t * lax.rsqrt(_segsum(t * t) * (1.0 / DN_D) + EPS) * ng2 for t in o], axis=1))
    yd = jnp.concatenate(outs, axis=0) * _silu(dz)
    return (yd, *sbd), (starts, ainv)


def _d_fwd(proj, cw, p1, p2, ng, exchange=None):
    L = proj.shape[0]
    DN_G, DN_T = DN_G_FWD, DN_G_FWD * DN_C
    n = L // DN_T

    def body(qkv_ref, ab_ref, dz_ref, hq_ref, cw_ref, p1_ref, p2_ref, ng_ref, o_ref, sall_ref, inv_ref, s_ref):
        i = pl.program_id(0)

        @pl.when(i == 0)
        def _():
            s_ref[...] = jnp.zeros_like(s_ref)

        keep = (i > 0).astype(F32)
        ext = jnp.concatenate([hq_ref[...] * keep, qkv_ref[...]], axis=0)
        cq = _conv_taps(ext, cw_ref, HALO_S, K_DN, DN_T)
        out, (starts, ainv) = _d_tile(cq, ab_ref[...], dz_ref[...], s_ref[0], s_ref[1], p1_ref[...], p2_ref[...],
                                      ng_ref[...])
        o_ref[...] = out[0]
        for p in range(DN_P):
            s_ref[p] = out[1 + p]
            for g in range(DN_G):
                sall_ref[g, p] = starts[g][p]
                inv_ref[g, p] = ainv[g * DN_P + p]

    return _sweep_with_exchange(
        body, n,
        in_specs=[pl.BlockSpec((DN_T, 3 * BR), lambda i: (i, 3)), pl.BlockSpec((DN_T, 128), lambda i: (i, 26)),
                  pl.BlockSpec((DN_T, BR), lambda i: (i, 12)),
                  pl.BlockSpec((HALO_S, 3 * BR), _halo_map(DN_T, HALO_S, 3)),
                  pl.BlockSpec((HALO_S, 3 * BR), lambda i: (0, 0)),
                  pl.BlockSpec((1, 128), lambda i: (0, 0)), pl.BlockSpec((1, 128), lambda i: (0, 0)),
                  pl.BlockSpec((1, DN_D), lambda i: (0, 0))],
        out_specs=[pl.BlockSpec((DN_T, BR), lambda i: (i, 0)),
                   pl.BlockSpec((DN_G,) + DN_STATE, lambda i: (i, 0, 0, 0)),
                   pl.BlockSpec((DN_G,) + DN_INV, lambda i: (i, 0, 0, 0))],
        out_shape=[jax.ShapeDtypeStruct((L, BR), F32), jax.ShapeDtypeStruct((L // DN_C,) + DN_STATE, F32),
                   jax.ShapeDtypeStruct((L // DN_C,) + DN_INV, F32)],
        scratch_shapes=[pltpu.VMEM(DN_STATE, F32)],
        args=(proj, proj, proj, proj, cw, p1, p2, ng), exchange=exchange, name="d_fwd")


def _sweep_with_exchange(body, steps, in_specs, out_specs, out_shape, scratch_shapes, args, exchange, name):
    if exchange is None:
        res = pl.pallas_call(body, grid=(steps,), in_specs=in_specs, out_specs=out_specs, out_shape=out_shape,
                             scratch_shapes=scratch_shapes, name=name, compiler_params=_cp("arbitrary"))(*args)
        return res, None
    several = isinstance(exchange, list)
    exs = [_Exchange(*e) for e in (exchange if several else [exchange])]
    xs = [x for e in (exchange if several else [exchange]) for x in e[0]]
    ni, no, ns, na = len(in_specs), len(out_specs), len(scratch_shapes), len(xs)

    def carried(*refs):
        ins, xin = refs[:ni], refs[ni:ni + na]
        outs, xout = refs[ni + na:ni + na + no], refs[ni + na + no:ni + 2 * na + no]
        scr, sems = refs[ni + 2 * na + no:ni + 2 * na + no + ns], refs[ni + 2 * na + no + ns:]

        def each(fn_name):
            off = 0
            for j, ex in enumerate(exs):
                getattr(ex, fn_name)(xin[off:off + ex.na], xout[off:off + ex.na], sems[3 * j:3 * j + 3])
                off += ex.na

        @pl.when(pl.program_id(0) == 0)
        def _():
            each("start")

        body(*ins, *outs, *scr)

        @pl.when(pl.program_id(0) == steps - 1)
        def _():
            each("wait")

    res = pl.pallas_call(carried, grid=(steps,), in_specs=list(in_specs) + [s for ex in exs for s in ex.in_specs],
                         out_specs=list(out_specs) + [s for ex in exs for s in ex.out_specs],
                         out_shape=list(out_shape) + [s for ex in exs for s in ex.out_shape],
                         scratch_shapes=list(scratch_shapes) + [s for ex in exs for s in ex.scratch_shapes],
                         name=name + "_x", compiler_params=_cp("arbitrary"))(*args, *xs)
    got, off = [], no
    for ex in exs:
        got.append(res[off:off + ex.na])
        off += ex.na
    return res[:no], (got if several else got[0])


def _d_bwd(proj, dmix, sall, inv, cw, p1, p2, ng, exchange=None):
    L = proj.shape[0]
    DN_G, DN_T = DN_G_BWD, DN_G_BWD * DN_C
    n = L // DN_T

    def body(qkv_ref, ab_ref, dz_ref, hq_ref, dy_ref, sall_ref, inv_ref, cw_ref, p1_ref, p2_ref, ng_ref,
             dp_ref, dcw_ref, dp1_ref, dp2_ref, dng_ref, ds_ref, carry_ref):
        i = pl.program_id(0)

        @pl.when(i == 0)
        def _():
            ds_ref[...] = jnp.zeros_like(ds_ref)
            carry_ref[...] = jnp.zeros_like(carry_ref)
            for r in (dcw_ref, dp1_ref, dp2_ref, dng_ref):
                r[...] = jnp.zeros_like(r)

        keep = (i < n - 1).astype(F32)
        ext = jnp.concatenate([hq_ref[...] * keep, qkv_ref[...]], axis=0)
        cq = _conv_taps(ext, cw_ref, HALO_S, K_DN, DN_T)
        known = tuple(inv_ref[g, p] for g in range(DN_G) for p in range(DN_P))
        _, vj = jax.vjp(lambda *a: _d_tile(*a, known=known)[0], cq, ab_ref[...], dz_ref[...], sall_ref[0, 0],
                        sall_ref[0, 1], p1_ref[...], p2_ref[...], ng_ref[...])
        dcq, dab, ddz, ds0, ds1, dp1, dp2, dng = vj((dy_ref[...], ds_ref[0], ds_ref[1]))
        dp_ref[:, 3 * BR:4 * BR] = ddz
        dp_ref[:, 4 * BR:4 * BR + 128] = dab
        ds_ref[0] = ds0
        ds_ref[1] = ds1
        dp1_ref[...] += dp1
        dp2_ref[...] += dp2
        dng_ref[...] += dng
        dext = _conv_taps_bwd(ext, cw_ref, dcw_ref, dcq, HALO_S, K_DN, DN_T)
        dp_ref[:, 0:3 * BR] = _add_tail(dext[HALO_S:], carry_ref[...])
        carry_ref[...] = dext[:HALO_S]

    rev = lambda i: n - 1 - i
    hmap = _halo_map(DN_T, HALO_S, 3)
    v128 = pl.BlockSpec((1, 128), lambda i: (0, 0))
    return _sweep_with_exchange(
        body, n,
        in_specs=[pl.BlockSpec((DN_T, 3 * BR), lambda i: (rev(i), 3)), pl.BlockSpec((DN_T, 128), lambda i: (rev(i), 26)),
                  pl.BlockSpec((DN_T, BR), lambda i: (rev(i), 12)),
                  pl.BlockSpec((HALO_S, 3 * BR), lambda i: hmap(rev(i))),
                  pl.BlockSpec((DN_T, BR), lambda i: (rev(i), 3)),
                  pl.BlockSpec((DN_G,) + DN_STATE, lambda i: (rev(i), 0, 0, 0)),
                  pl.BlockSpec((DN_G,) + DN_INV, lambda i: (rev(i), 0, 0, 0)),
                  pl.BlockSpec((HALO_S, 3 * BR), lambda i: (0, 0)), v128, v128,
                  pl.BlockSpec((1, DN_D), lambda i: (0, 0))],
        out_specs=[pl.BlockSpec((DN_T, W_D), lambda i: (rev(i), 0)),
                   pl.BlockSpec((HALO_S, 3 * BR), lambda i: (0, 0)), v128, v128,
                   pl.BlockSpec((1, DN_D), lambda i: (0, 0))],
        out_shape=[jax.ShapeDtypeStruct((L, W_D), F32), jax.ShapeDtypeStruct((HALO_S, 3 * BR), F32),
                   jax.ShapeDtypeStruct((1, 128), F32), jax.ShapeDtypeStruct((1, 128), F32),
                   jax.ShapeDtypeStruct((1, DN_D), F32)],
        scratch_shapes=[pltpu.VMEM(DN_STATE, F32), pltpu.VMEM((HALO_S, 3 * BR), F32)],
        args=(proj, proj, proj, proj, dmix, sall, inv, cw, p1, p2, ng), exchange=exchange, name="d_bwd")


def _pick_rows(rows, cap):
    best = 8
    for t in range(8, cap + 1, 8):
        if rows % t == 0:
            best = t
    return best


def _adamw(w, g, m, v, name, lead=None):
    rows, rest = w.shape[0], w.shape[1:]
    tr = _pick_rows(rows, 512) if lead is None else lead
    c1 = 1.0 - ADAM_B1 ** ADAM_STEP
    c2 = 1.0 - ADAM_B2 ** ADAM_STEP

    def body(w_ref, g_ref, m_ref, v_ref, d_ref, mo_ref, vo_ref):
        gv = g_ref[...]
        mn = ADAM_B1 * m_ref[...] + (1.0 - ADAM_B1) * gv
        vn = ADAM_B2 * v_ref[...] + (1.0 - ADAM_B2) * (gv * gv)
        d_ref[...] = -ADAM_LR * ((mn / c1) / (jnp.sqrt(vn / c2) + ADAM_EPS) + ADAM_WD * w_ref[...])
        mo_ref[...] = mn
        vo_ref[...] = vn

    spec = pl.BlockSpec((tr,) + rest, lambda i: (i,) + (0,) * len(rest))
    return pl.pallas_call(
        body, grid=(rows // tr,), in_specs=[spec] * 4, out_specs=[spec] * 3,
        out_shape=[jax.ShapeDtypeStruct(w.shape, F32)] * 3,
        name=name, compiler_params=_cp("parallel"))(w, g, m, v)


def _sum_slots(r, name):
    n, rows, wd = r.shape
    tr = _pick_rows(rows, 384)

    def body(r_ref, o_ref):
        acc = r_ref[0].astype(F32)
        for j in range(1, n):
            acc = acc + r_ref[j].astype(F32)
        o_ref[...] = acc

    return pl.pallas_call(
        body, grid=(rows // tr,),
        in_specs=[pl.BlockSpec((n, tr, wd), lambda i: (0, i, 0))],
        out_specs=pl.BlockSpec((tr, wd), lambda i: (i, 0)),
        out_shape=jax.ShapeDtypeStruct((rows, wd), F32),
        name=name, compiler_params=_cp("parallel"))(r)


AXES = ("x", "y", "c")


def _group_peer(axes, k):
    pos = {a: lax.axis_index(a) for a in AXES}
    idx = 0
    for a in axes:
        idx = idx * 2 + pos[a]
    peer = dict(pos)
    for b, a in enumerate(reversed(axes)):
        if (k >> b) & 1:
            peer[a] = 1 - pos[a]
    return idx, tuple(peer[a] for a in AXES)


MAX_CHUNKS = 4


class _Exchange:
    def __init__(self, xs, axes, mode):
        self.axes, self.mode, self.na, self.n = axes, mode, len(xs), 2 ** len(axes)
        n = self.n
        self.out_shape, self.pieces = [], []
        for x in xs:
            if mode == "gather":
                shape, lead = (n,) + x.shape, x.shape[0]
            elif mode == "scatter":
                shape, lead = x.shape, x.shape[1]
            else:
                shape, lead = (x.shape[0], n * x.shape[1], x.shape[2]), x.shape[0]
            self.out_shape.append(jax.ShapeDtypeStruct(shape, x.dtype))
            big = x.size * x.dtype.itemsize >= (1 << 20)
            if mode == "rows":
                self.pieces.append(lead if lead <= MAX_CHUNKS else 1)
            else:
                self.pieces.append(MAX_CHUNKS if big and lead % (16 * MAX_CHUNKS) == 0 else 1)
        self.in_specs = [pl.BlockSpec(memory_space=pl.ANY)] * self.na
        self.out_specs = [pl.BlockSpec(memory_space=pl.ANY)] * self.na
        self.scratch_shapes = [pltpu.SemaphoreType.DMA((self.na, MAX_CHUNKS, n)),
                               pltpu.SemaphoreType.DMA((self.na, MAX_CHUNKS, n)),
                               pltpu.SemaphoreType.DMA((self.na, MAX_CHUNKS))]

    def _copies(self, x_refs, o_refs, send_sems, recv_sems, local_sems):
        me, _ = _group_peer(self.axes, 0)
        local, remote = [], []
        for a, (x, o) in enumerate(zip(x_refs, o_refs)):
            for c in range(self.pieces[a]):
                if self.mode == "rows":
                    r = x.shape[1]
                    b = slice(None) if self.pieces[a] == 1 else pl.ds(c, 1)
                    src = lambda k, x=x, b=b: x.at[b]
                    dst = o.at[b, pl.ds(me * r, r)]
                else:
                    lead = x.shape[1] if self.mode == "scatter" else x.shape[0]
                    rs = pl.ds(c * (lead // self.pieces[a]), lead // self.pieces[a])
                    if self.mode == "scatter":
                        src = lambda k, x=x, rs=rs: x.at[me ^ k, rs]
                    else:
                        src = lambda k, x=x, rs=rs: x.at[rs]
                    dst = o.at[me, rs]
                local.append(pltpu.make_async_copy(src(0), dst, local_sems.at[a, c]))
                for k in range(1, self.n):
                    remote.append(pltpu.make_async_remote_copy(
                        src_ref=src(k), dst_ref=dst, send_sem=send_sems.at[a, c, k], recv_sem=recv_sems.at[a, c, k],
                        device_id=_group_peer(self.axes, k)[1], device_id_type=MESH))
        return local, remote

    def start(self, x_refs, o_refs, sems):
        local, remote = self._copies(x_refs, o_refs, *sems)
        for cp in local + remote:
            cp.start()

    def wait(self, x_refs, o_refs, sems):
        local, remote = self._copies(x_refs, o_refs, *sems)
        for cp in remote:
            cp.wait_send()
        for cp in remote:
            cp.wait_recv()
        for cp in local:
            cp.wait()


def _exchange(xs, axes, mode, name):
    ex = _Exchange(xs, axes, mode)
    na = ex.na

    def body(*refs):
        ex.start(refs[:na], refs[na:2 * na], refs[2 * na:])
        ex.wait(refs[:na], refs[na:2 * na], refs[2 * na:])

    return pl.pallas_call(body, out_shape=ex.out_shape, in_specs=ex.in_specs, out_specs=ex.out_specs,
                          scratch_shapes=ex.scratch_shapes, name=name)(*xs)


SHARDED_SMALL = (("a_conv_w", 2), ("a_pw_w", 1), ("s5_glu_w", 1), ("c_conv_w", 2), ("d_conv_w", 2))
REPLICATED = ("norm_g", "a_conv_b", "a_ln_g", "a_ln_b", "a_pw_b", "s5_lambda_re", "s5_lambda_im", "s5_b_re",
              "s5_b_im", "s5_c_re", "s5_c_im", "s5_d", "s5_log_dt", "s5_glu_b", "d_a_log", "d_dt_bias",
              "d_norm_g", "final_g")
WEIGHTS = ("norm_g", "w_in", "a_conv_w", "a_conv_b", "a_ln_g", "a_ln_b", "a_pw_w", "a_pw_b", "s5_lambda_re",
           "s5_lambda_im", "s5_b_re", "s5_b_im", "s5_c_re", "s5_c_im", "s5_d", "s5_log_dt", "s5_glu_w",
           "s5_glu_b", "c_conv_w", "d_conv_w", "d_a_log", "d_dt_bias", "d_norm_g", "w_out", "final_g")
LANES = 1024


def _size(shape):
    size = 1
    for d in shape:
        size *= d
    return size


def _slab_rows(shape):
    return -(-_size(shape) // (8 * LANES)) * 8


def _pack(arrs, rows):
    parts = []
    for a in arrs:
        r = _slab_rows(a.shape)
        parts.append(jnp.pad(a.reshape(-1), (0, r * LANES - a.size)).reshape(r, LANES))
    used = sum(p.shape[0] for p in parts)
    if rows > used:
        parts.append(jnp.zeros((rows - used, LANES), parts[0].dtype))
    return jnp.concatenate(parts, axis=0)


def _unpack(slab, shapes):
    out, off = [], 0
    for s in shapes:
        r = _slab_rows(s)
        out.append(slab[off:off + r].reshape(-1)[:_size(s)].reshape(s))
        off += r
    return out


def _rows_for(shapes, mult):
    rows = sum(_slab_rows(s) for s in shapes)
    return -(-rows // mult) * mult


def _row(v, width=None):
    v = v.reshape(1, -1)
    return v if width is None else jnp.pad(v, ((0, 0), (0, width - v.shape[1])))


def _pad_rows(w, rows):
    return jnp.pad(w, ((0, rows - w.shape[0]), (0, 0)))


def _permute_in(w):
    return jnp.concatenate([w[:, :3072], w[:, 3080:N_IN], w[:, 3072:3080],
                            jnp.zeros((w.shape[0], N_INP - N_IN), w.dtype)], axis=1)


def _layer_fwd(x, p, exchanges):
    proj, h = _proj_fwd(x, p["norm_g"], p["wp"])
    ya = _a_fwd(proj, p["a_cw"], p["a_cb"], p["a_lng"], p["a_lnb"], p["a_pw"], p["a_pwb"])
    ar, ai, bemb, cemb = _s5_prep(*p["s5"])
    got = {}
    (yb, sin_all), got["b"] = _b_fwd(proj, ar, ai, bemb, cemb, p["s5_d"], p["glu_w"], p["glu_b"], exchanges.get("b"))
    yc = _c_fwd(proj, p["c_cw"])
    (yd, sall, inv), got["d"] = _d_fwd(proj, p["d_cw"], p["d_p1"], p["d_p2"], p["d_ng"], exchanges.get("d"))
    xo = _out_fwd(x, ya, yb, yc, yd, p["wo"])
    return xo, dict(x=x, proj=proj, h=h, ys=(ya, yb, yc, yd), sin_all=sin_all, sall=sall, inv=inv,
                    s5=(ar, ai, bemb, cemb)), got


def _layer_bwd(dxo, p, r, exchanges):
    proj = r["proj"]
    ar, ai, bemb, cemb = r["s5"]
    dmix = _out_bwd_x(dxo, p["wo"])
    dwo = _dwout(*r["ys"], dxo)
    dpa, dcw_a, dcb, dlng, dlnb, dpw, dpwb = _a_bwd(proj, dmix, p["a_cw"], p["a_cb"], p["a_lng"], p["a_lnb"],
                                                     p["a_pw"], p["a_pwb"])
    got = {}
    (dpb, dar, dai, dbe, dce, ddsk, dgw, dgb), got["b"] = _b_bwd(
        proj, dmix, r["sin_all"], ar, ai, bemb, cemb, p["s5_d"], p["glu_w"], p["glu_b"], exchanges.get("b"))
    dlre, dlim, dldt, dbre, dbim, dcre, dcim = _s5_prep_bwd(*p["s5"], dar, dai, dbe, dce)
    dpc, dcw_c = _c_bwd(proj, dmix, p["c_cw"])
    (dpd, dcw_d, dp1, dp2, dng), got["d"] = _d_bwd(proj, dmix, r["sall"], r["inv"], p["d_cw"], p["d_p1"],
                                                   p["d_p2"], p["d_ng"], exchanges.get("d"))
    dwa, dwb, dwc, dwd = _dwin(r["h"], dpa, dpb, dpc, dpd)
    dwin = jnp.concatenate([dwa, dwb, dwc, dwd[:, :3 * BR], dwd[:, 4 * BR:4 * BR + 2 * DN_H],
                            dwd[:, 3 * BR:4 * BR]], axis=1)
    ex_proj = exchanges.get("proj")
    if callable(ex_proj):
        ex_proj = ex_proj(got["d"], dwin, dwo)
    (dx, dg), got["proj"] = _proj_bwd_x(r["x"], p["norm_g"], dpa, dpb, dpc, dpd, p["wp"], dxo, ex_proj)

    def unrows(t, perm):
        return jnp.transpose(t.reshape(S5_H, S5_G, S5_P), perm)

    grads = dict(
        norm_g=dg.reshape(-1), w_in=dwin, a_conv_w=dcw_a[:K_A], a_conv_b=dcb.reshape(-1),
        a_ln_g=dlng.reshape(-1), a_ln_b=dlnb.reshape(-1), a_pw_w=dpw, a_pw_b=dpwb.reshape(-1),
        s5_lambda_re=dlre.reshape(S5_G, S5_P), s5_lambda_im=dlim.reshape(S5_G, S5_P),
        s5_b_re=unrows(dbre, (1, 2, 0)), s5_b_im=unrows(dbim, (1, 2, 0)),
        s5_c_re=unrows(dcre, (1, 0, 2)), s5_c_im=unrows(dcim, (1, 0, 2)),
        s5_d=ddsk.reshape(-1), s5_log_dt=dldt[0, :S5_G], s5_glu_w=dgw, s5_glu_b=dgb.reshape(-1),
        c_conv_w=dcw_c[:K_C], d_conv_w=dcw_d[:K_DN], d_a_log=dp1[0, :DN_H], d_dt_bias=dp2[0, :DN_H],
        d_norm_g=dng.reshape(-1), w_out=dwo)
    return dx, grads, got


def _layer_params(full, wp, wo, l):
    return dict(
        norm_g=_row(full["norm_g"][l]), wp=wp,
        a_cw=_pad_rows(full["a_conv_w"][l], HALO_A), a_cb=_row(full["a_conv_b"][l]),
        a_lng=_row(full["a_ln_g"][l]), a_lnb=_row(full["a_ln_b"][l]), a_pw=full["a_pw_w"][l],
        a_pwb=_row(full["a_pw_b"][l]),
        s5=(_row(full["s5_lambda_re"][l]), _row(full["s5_lambda_im"][l]), _row(full["s5_log_dt"][l], 128),
            jnp.transpose(full["s5_b_re"][l], (2, 0, 1)).reshape(S5_H, NS),
            jnp.transpose(full["s5_b_im"][l], (2, 0, 1)).reshape(S5_H, NS),
            jnp.transpose(full["s5_c_re"][l], (1, 0, 2)).reshape(S5_H, NS),
            jnp.transpose(full["s5_c_im"][l], (1, 0, 2)).reshape(S5_H, NS)),
        s5_d=_row(full["s5_d"][l]), glu_w=full["s5_glu_w"][l], glu_b=_row(full["s5_glu_b"][l]),
        c_cw=_pad_rows(full["c_conv_w"][l], HALO_S), d_cw=_pad_rows(full["d_conv_w"][l], HALO_S),
        d_p1=_row(full["d_a_log"][l], 128), d_p2=_row(full["d_dt_bias"][l], 128),
        d_ng=_row(full["d_norm_g"][l]), wo=wo)


def kernel(x, norm_g, w_in, a_conv_w, a_conv_b, a_ln_g, a_ln_b, a_pw_w, a_pw_b, s5_lambda_re, s5_lambda_im, s5_b_re, s5_b_im, s5_c_re, s5_c_im, s5_d, s5_log_dt, s5_glu_w, s5_glu_b, c_conv_w, d_conv_w, d_a_log, d_dt_bias, d_norm_g, w_out, final_g, loss_target, m_norm_g, m_w_in, m_a_conv_w, m_a_conv_b, m_a_ln_g, m_a_ln_b, m_a_pw_w, m_a_pw_b, m_s5_lambda_re, m_s5_lambda_im, m_s5_b_re, m_s5_b_im, m_s5_c_re, m_s5_c_im, m_s5_d, m_s5_log_dt, m_s5_glu_w, m_s5_glu_b, m_c_conv_w, m_d_conv_w, m_d_a_log, m_d_dt_bias, m_d_norm_g, m_w_out, m_final_g, v_norm_g, v_w_in, v_a_conv_w, v_a_conv_b, v_a_ln_g, v_a_ln_b, v_a_pw_w, v_a_pw_b, v_s5_lambda_re, v_s5_lambda_im, v_s5_b_re, v_s5_b_im, v_s5_c_re, v_s5_c_im, v_s5_d, v_s5_log_dt, v_s5_glu_w, v_s5_glu_b, v_c_conv_w, v_d_conv_w, v_d_a_log, v_d_dt_bias, v_d_norm_g, v_w_out, v_final_g):
    given = dict(locals())
    w = {n: given[n] for n in WEIGHTS}
    m = {n: given["m_" + n] for n in WEIGHTS}
    v = {n: given["v_" + n] for n in WEIGHTS}
    xs, tgt = x[0], loss_target[0]

    n_in, n_out = w["w_in"].shape[2], w["w_out"].shape[1]
    sm_names = [n for n, _ in SHARDED_SMALL]
    sm_shapes = [w[n].shape for n in sm_names]
    sm_rows = _rows_for(sm_shapes, 16)
    win_b, wout_b = w["w_in"].astype(BF), w["w_out"].astype(BF)
    g_in, g_out, g_sm = _exchange([win_b[0], wout_b[0], _pack([w[n] for n in sm_names], sm_rows)],
                                  ("x", "y"), "gather", "gather_first")
    full = dict(w)
    parts = [_unpack(g_sm[j], sm_shapes) for j in range(4)]
    for i, (n, ax) in enumerate(SHARDED_SMALL):
        full[n] = jnp.concatenate([parts[j][i] for j in range(4)], axis=ax)

    saved = []
    h = xs
    for l in range(DEPTH):
        p = _layer_params(full, _permute_in(jnp.concatenate([g_in[j] for j in range(4)], axis=1)),
                          jnp.concatenate([g_out[j] for j in range(4)], axis=0), l)
        nxt = {}
        if l + 1 < DEPTH:
            nxt = {"d": ([win_b[l + 1]], ("x", "y"), "gather"), "b": ([wout_b[l + 1]], ("x", "y"), "gather")}
        h, r, got = _layer_fwd(h, p, nxt)
        saved.append((p, r))
        if l + 1 < DEPTH:
            (g_in,), (g_out,) = got["d"], got["b"]
    loss_tile, dx, dfg = _loss_bwd(h, _row(full["final_g"]), tgt)

    def big_slots(dwin, dwo):
        s_in = jnp.stack([dwin[:, j * n_in:(j + 1) * n_in].astype(BF) for j in range(4)])
        return [s_in.reshape(8, D_MODEL // 2, n_in), dwo.astype(BF).reshape(8, n_out // 2, D_MODEL)]

    def halves(rv):
        return [_sum_slots(rv[0], "sum_w_in")[None], _sum_slots(rv[1], "sum_w_out")[None]]

    layer_grads, summed, pending, arrived = [None] * DEPTH, [None] * DEPTH, None, {}
    for l in reversed(range(DEPTH)):
        p, r = saved[l]
        ex = {}
        if pending is not None:
            ex["d"] = (pending, AXES, "scatter")
        if l + 2 in arrived:
            ex["b"] = (halves(arrived.pop(l + 2)), ("c",), "rows")
        if l == 0:
            ex["proj"] = lambda came, dwin, dwo: [(halves(came), ("c",), "rows"),
                                                  (big_slots(dwin, dwo), AXES, "scatter")]
        dx, layer_grads[l], got = _layer_bwd(dx, p, r, ex)
        if got["b"] is not None:
            summed[l + 2] = got["b"]
        if got["proj"] is not None:
            summed[1], arrived[0] = got["proj"]
        elif got["d"] is not None:
            arrived[l + 1] = got["d"]
        pending = big_slots(layer_grads[l]["w_in"], layer_grads[l]["w_out"]) if l else None
    grads = {n: jnp.stack([layer_grads[l][n] for l in range(DEPTH)]) for n in WEIGHTS
             if n not in ("final_g", "w_in", "w_out")}
    grads["final_g"] = dfg.reshape(-1)
    slots = []
    for j in range(4):
        sl = [lax.slice_in_dim(grads[n], j * w[n].shape[ax], (j + 1) * w[n].shape[ax], axis=ax)
              for n, ax in SHARDED_SMALL]
        slots.append(_pack(sl, sm_rows))
    rp_shapes = [w[n].shape for n in REPLICATED] + [(1,)]
    rp_rows = _rows_for(rp_shapes, 64)
    r_sm, r_rp = _exchange(
        [jnp.stack(slots).reshape(8, sm_rows // 2, LANES),
         _pack([grads[n] for n in REPLICATED] + [loss_tile[0, 0:1]], rp_rows).reshape(8, rp_rows // 8, LANES)],
        AXES, "scatter", "scatter_last")
    summed[0] = _exchange(halves(arrived.pop(0)) + [_sum_slots(r_sm, "sum_small")[None]], ("c",), "rows",
                          "gather_halves")
    h_sm = summed[0][2]
    h_in = jnp.concatenate([summed[l][0] for l in range(DEPTH)], axis=0)
    h_out = jnp.concatenate([summed[l][1] for l in range(DEPTH)], axis=0)
    (g_rp,) = _exchange([_sum_slots(r_rp, "sum_replicated")], AXES, "gather", "gather_replicated")
    g_rp = g_rp.reshape(rp_rows, LANES)
    g_sm = h_sm.reshape(sm_rows, LANES)

    out = {}

    def put(name, shape, res):
        for key, t in zip(("delta", "new_m", "new_v"), res):
            out[key + "_" + name] = t.reshape(shape)

    g2 = h_out.reshape(DEPTH * n_out, D_MODEL)
    out["grad_w_out"] = g2.reshape(w["w_out"].shape)
    put("w_out", w["w_out"].shape, _adamw(w["w_out"].reshape(g2.shape), g2, m["w_out"].reshape(g2.shape),
                                          v["w_out"].reshape(g2.shape), "adamw_w_out"))
    cm = lambda a: jnp.transpose(a, (2, 0, 1))
    rm = lambda a: jnp.transpose(a, (1, 2, 0))
    g3 = cm(h_in)
    out["grad_w_in"] = rm(g3)
    for key, t in zip(("delta", "new_m", "new_v"),
                      _adamw(cm(w["w_in"]), g3, cm(m["w_in"]), cm(v["w_in"]), "adamw_w_in", lead=n_in // 6)):
        out[key + "_w_in"] = rm(t)
    zero = jnp.zeros((1,), F32)
    res_sm = _adamw(_pack([w[n] for n in sm_names], sm_rows), g_sm, _pack([m[n] for n in sm_names], sm_rows),
                    _pack([v[n] for n in sm_names], sm_rows), "adamw_small")
    res_rp = _adamw(_pack([w[n] for n in REPLICATED] + [zero], rp_rows), g_rp,
                    _pack([m[n] for n in REPLICATED] + [zero], rp_rows),
                    _pack([v[n] for n in REPLICATED] + [zero], rp_rows), "adamw_replicated")
    for key, sm, rp in (("grad", g_sm, g_rp), ("delta", res_sm[0], res_rp[0]), ("new_m", res_sm[1], res_rp[1]),
                        ("new_v", res_sm[2], res_rp[2])):
        for n, t in zip(sm_names, _unpack(sm, sm_shapes)):
            out[key + "_" + n] = t
        for n, t in zip(REPLICATED, _unpack(rp, rp_shapes[:-1])):
            out[key + "_" + n] = t
    loss = _unpack(g_rp, rp_shapes)[-1].reshape(())
    return (loss, dx[None], *[out["grad_" + n] for n in WEIGHTS], *[out["delta_" + n] for n in WEIGHTS],
            *[out["new_m_" + n] for n in WEIGHTS], *[out["new_v_" + n] for n in WEIGHTS])
```

```python
import functools

import jax
import jax.numpy as jnp
from jax import lax
from jax.experimental import pallas as pl
from jax.experimental.pallas import tpu as pltpu

F32, BF = jnp.float32, jnp.bfloat16
HI = lax.Precision.HIGHEST
MESH = pl.DeviceIdType.MESH

D_MODEL = 1024
BR = 256
DEPTH = 4
N_IN = 3336
N_INP = 3456
COL_A, COL_B, COL_C, COL_D = 0, 768, 1280, 2304
W_A, W_B, W_C, W_D = 768, 512, 1024, 1152
S5_G, S5_H, S5_P = 16, 16, 64
NS = S5_G * S5_P
DN_H, DN_D, DN_C = 4, 64, 64
DN_G_FWD, DN_G_BWD = 8, 4
DN_P = DN_H // 2
DN_STATE = (DN_P, 2 * DN_D, 2 * DN_D)
DN_INV = (DN_P, DN_C, 2 * DN_D)
K_A, K_C, K_DN = 31, 3, 4
HALO_A, HALO_S = 32, 8
EPS = 1e-6
TL = 256
VMEM_LIMIT = 56 * 1024 * 1024

ADAM_LR, ADAM_B1, ADAM_B2, ADAM_EPS, ADAM_WD, ADAM_STEP = 0.001, 0.9, 0.999, 1e-08, 0.01, 10


def _cp(*sem):
    return pltpu.CompilerParams(dimension_semantics=sem, vmem_limit_bytes=VMEM_LIMIT)


def _sigmoid(x):
    return jax.nn.sigmoid(x)


def _silu(x):
    return x * jax.nn.sigmoid(x)


def _rmsnorm(x, g):
    return x * lax.rsqrt(jnp.mean(x * x, axis=-1, keepdims=True) + EPS) * g


@jax.custom_vjp
def _mm(a, w):
    return jnp.dot(a.astype(BF), w.astype(BF), preferred_element_type=F32)


def _mm_f(a, w):
    return _mm(a, w), (a, w)


def _mm_b(res, g):
    a, w = res
    gb = g.astype(BF)
    da = lax.dot_general(gb, w.astype(BF), (((1,), (1,)), ((), ())), preferred_element_type=F32)
    dw = lax.dot_general(a.astype(BF), gb, (((0,), (0,)), ((), ())), preferred_element_type=F32)
    return da, dw


_mm.defvjp(_mm_f, _mm_b)


@jax.custom_vjp
def _mm_nt(a, b):
    return lax.dot_general(a.astype(BF), b.astype(BF), (((1,), (1,)), ((), ())), preferred_element_type=F32)


def _mm_nt_f(a, b):
    return _mm_nt(a, b), (a, b)


def _mm_nt_b(res, g):
    a, b = res
    gb = g.astype(BF)
    da = jnp.dot(gb, b.astype(BF), preferred_element_type=F32)
    db = lax.dot_general(gb, a.astype(BF), (((0,), (0,)), ((), ())), preferred_element_type=F32)
    return da, db


_mm_nt.defvjp(_mm_nt_f, _mm_nt_b)


@jax.custom_vjp
def _mm_tn(a, b):
    return lax.dot_general(a.astype(BF), b.astype(BF), (((0,), (0,)), ((), ())), preferred_element_type=F32)


def _mm_tn_f(a, b):
    return _mm_tn(a, b), (a, b)


def _mm_tn_b(res, g):
    a, b = res
    gb = g.astype(BF)
    da = lax.dot_general(b.astype(BF), gb, (((1,), (1,)), ((), ())), preferred_element_type=F32)
    db = jnp.dot(a.astype(BF), gb, preferred_element_type=F32)
    return da, db


_mm_tn.defvjp(_mm_tn_f, _mm_tn_b)


def _dot_hi(a, b):
    return jnp.dot(a, b, precision=HI, preferred_element_type=F32)


def _split(a):
    hi = a.astype(BF)
    return hi, (a - hi.astype(F32)).astype(BF)


def _dot3(a, b, dims=(((1,), (0,)), ((), ()))):
    ah, al = _split(a)
    bh, bl = _split(b)
    d = functools.partial(lax.dot_general, dimension_numbers=dims, preferred_element_type=F32)
    return d(ah, bh) + d(ah, bl) + d(al, bh)


@jax.custom_vjp
def _mm3(a, b):
    return _dot3(a, b)


def _mm3_f(a, b):
    return _dot3(a, b), (a, b)


def _mm3_b(res, g):
    a, b = res
    return _dot3(g, b, (((1,), (1,)), ((), ()))), _dot3(a, g, (((0,), (0,)), ((), ())))


_mm3.defvjp(_mm3_f, _mm3_b)


def _roll(x, s):
    n = x.shape[0]
    s = s % n
    return x if s == 0 else pltpu.roll(x, s, 0)


def _conv_taps(ext, w_ref, halo, k_taps, tl):
    acc = None
    for k in range(k_taps):
        term = _roll(ext, (k_taps - 1) - k)[halo:halo + tl] * w_ref[k:k + 1, :]
        acc = term if acc is None else acc + term
    return acc


def _conv_taps_bwd(ext, w_ref, dw_ref, dacc, halo, k_taps, tl):
    dpad = jnp.concatenate([dacc, jnp.zeros((halo, dacc.shape[1]), F32)], axis=0)
    dext = None
    for k in range(k_taps):
        r = _roll(ext, (k_taps - 1) - k)[halo:halo + tl]
        dw_ref[k:k + 1, :] += jnp.sum(r * dacc, axis=0, keepdims=True)
        term = _roll(dpad, halo - (k_taps - 1) + k) * w_ref[k:k + 1, :]
        dext = term if dext is None else dext + term
    return dext


def _add_tail(x, tail):
    tl, h = x.shape[0], tail.shape[0]
    return x + jnp.concatenate([jnp.zeros((tl - h, x.shape[1]), F32), tail], axis=0)


def _proj_fwd(x, g, wp):
    L = x.shape[0]

    def body(x_ref, g_ref, w_ref, p_ref, h_ref):
        hb = _rmsnorm(x_ref[...], g_ref[...]).astype(BF)
        h_ref[...] = hb
        p_ref[...] = jnp.dot(hb, w_ref[...], preferred_element_type=F32)

    return pl.pallas_call(
        body, grid=(L // TL,),
        in_specs=[pl.BlockSpec((TL, D_MODEL), lambda i: (i, 0)),
                  pl.BlockSpec((1, D_MODEL), lambda i: (0, 0)),
                  pl.BlockSpec((D_MODEL, N_INP), lambda i: (0, 0))],
        out_specs=[pl.BlockSpec((TL, N_INP), lambda i: (i, 0)),
                   pl.BlockSpec((TL, D_MODEL), lambda i: (i, 0))],
        out_shape=[jax.ShapeDtypeStruct((L, N_INP), F32), jax.ShapeDtypeStruct((L, D_MODEL), BF)],
        name="proj_fwd", compiler_params=_cp("parallel"))(x, g, wp)


def _proj_bwd_x(x, g, dpa, dpb, dpc, dpd, wp, dxo, exchange=None):
    L = x.shape[0]

    def body(x_ref, g_ref, a_ref, b_ref, c_ref, d_ref, w_ref, dxo_ref, dx_ref, dg_ref):
        dh = None
        for ref, c0, wd in ((a_ref, COL_A, W_A), (b_ref, COL_B, W_B), (c_ref, COL_C, W_C), (d_ref, COL_D, W_D)):
            t = lax.dot_general(ref[...].astype(BF), w_ref[:, c0:c0 + wd], (((1,), (1,)), ((), ())),
                                preferred_element_type=F32)
            dh = t if dh is None else dh + t
        _, vj = jax.vjp(_rmsnorm, x_ref[...], g_ref[...])
        dx, dg = vj(dh)
        dx_ref[...] = dxo_ref[...] + dx

        @pl.when(pl.program_id(0) == 0)
        def _():
            dg_ref[...] = jnp.zeros_like(dg_ref)

        dg_ref[...] += dg

    def rows(wd):
        return pl.BlockSpec((TL, wd), lambda i: (i, 0))

    return _sweep_with_exchange(
        body, L // TL,
        in_specs=[rows(D_MODEL), pl.BlockSpec((1, D_MODEL), lambda i: (0, 0)),
                  rows(W_A), rows(W_B), rows(W_C), rows(W_D),
                  pl.BlockSpec((D_MODEL, N_INP), lambda i: (0, 0)), rows(D_MODEL)],
        out_specs=[rows(D_MODEL), pl.BlockSpec((1, D_MODEL), lambda i: (0, 0))],
        out_shape=[jax.ShapeDtypeStruct((L, D_MODEL), F32), jax.ShapeDtypeStruct((1, D_MODEL), F32)],
        scratch_shapes=[], args=(x, g, dpa, dpb, dpc, dpd, wp, dxo), exchange=exchange, name="proj_bwd_x")


def _dwin(h, dpa, dpb, dpc, dpd):
    L = h.shape[0]

    def body(h_ref, a_ref, b_ref, c_ref, d_ref, oa_ref, ob_ref, oc_ref, od_ref):
        outs = (oa_ref, ob_ref, oc_ref, od_ref)

        @pl.when(pl.program_id(0) == 0)
        def _():
            for o in outs:
                o[...] = jnp.zeros_like(o)

        ht = h_ref[...].T
        for ref, o in zip((a_ref, b_ref, c_ref, d_ref), outs):
            o[...] += jnp.dot(ht, ref[...].astype(BF), preferred_element_type=F32)

    def rows(wd):
        return pl.BlockSpec((TL, wd), lambda i: (i, 0))

    def whole(wd):
        return pl.BlockSpec((D_MODEL, wd), lambda i: (0, 0))

    widths = (W_A, W_B, W_C, W_D)
    return pl.pallas_call(
        body, grid=(L // TL,),
        in_specs=[rows(D_MODEL)] + [rows(wd) for wd in widths],
        out_specs=[whole(wd) for wd in widths],
        out_shape=[jax.ShapeDtypeStruct((D_MODEL, wd), F32) for wd in widths],
        name="dwin", compiler_params=_cp("arbitrary"))(h, dpa, dpb, dpc, dpd)


def _dwout(ya, yb, yc, yd, dxo):
    L = dxo.shape[0]
    tk, tn = min(512, L), 512

    def body(a_ref, b_ref, c_ref, d_ref, g_ref, o_ref):
        @pl.when(pl.program_id(1) == 0)
        def _():
            o_ref[...] = jnp.zeros_like(o_ref)

        gb = g_ref[...].astype(BF)
        for j, ref in enumerate((a_ref, b_ref, c_ref, d_ref)):
            o_ref[j * BR:(j + 1) * BR, :] += lax.dot_general(ref[...].astype(BF), gb, (((0,), (0,)), ((), ())),
                                                             preferred_element_type=F32)

    ys = pl.BlockSpec((tk, BR), lambda j, t: (t, 0))
    return pl.pallas_call(
        body, grid=(D_MODEL // tn, L // tk),
        in_specs=[ys, ys, ys, ys, pl.BlockSpec((tk, tn), lambda j, t: (t, j))],
        out_specs=pl.BlockSpec((D_MODEL, tn), lambda j, t: (0, j)),
        out_shape=jax.ShapeDtypeStruct((D_MODEL, D_MODEL), F32),
        name="dwout", compiler_params=_cp("parallel", "arbitrary"))(ya, yb, yc, yd, dxo)


def _out_fwd(x, ya, yb, yc, yd, wo):
    L = x.shape[0]

    def body(x_ref, a_ref, b_ref, c_ref, d_ref, w_ref, o_ref):
        acc = x_ref[...]
        for j, ref in enumerate((a_ref, b_ref, c_ref, d_ref)):
            acc = acc + jnp.dot(ref[...].astype(BF), w_ref[j * BR:(j + 1) * BR, :], preferred_element_type=F32)
        o_ref[...] = acc

    def rows(wd):
        return pl.BlockSpec((TL, wd), lambda i: (i, 0))

    return pl.pallas_call(
        body, grid=(L // TL,),
        in_specs=[rows(D_MODEL), rows(BR), rows(BR), rows(BR), rows(BR),
                  pl.BlockSpec((D_MODEL, D_MODEL), lambda i: (0, 0))],
        out_specs=rows(D_MODEL), out_shape=jax.ShapeDtypeStruct((L, D_MODEL), F32),
        name="out_fwd", compiler_params=_cp("parallel"))(x, ya, yb, yc, yd, wo)


def _out_bwd_x(dxo, wo):
    L = dxo.shape[0]

    def body(d_ref, w_ref, o_ref):
        o_ref[...] = lax.dot_general(d_ref[...].astype(BF), w_ref[...], (((1,), (1,)), ((), ())),
                                     preferred_element_type=F32)

    return pl.pallas_call(
        body, grid=(L // TL,),
        in_specs=[pl.BlockSpec((TL, D_MODEL), lambda i: (i, 0)), pl.BlockSpec((D_MODEL, D_MODEL), lambda i: (0, 0))],
        out_specs=pl.BlockSpec((TL, D_MODEL), lambda i: (i, 0)),
        out_shape=jax.ShapeDtypeStruct((L, D_MODEL), F32),
        name="out_bwd_x", compiler_params=_cp("parallel"))(dxo, wo)


def _loss_bwd(x, g, tgt):
    L = x.shape[0]

    def f(xv, gv, tv):
        err = _rmsnorm(xv, gv) - tv
        return 0.5 * jnp.sum(jnp.mean(err * err, axis=-1, keepdims=True), axis=0, keepdims=True)

    def body(x_ref, g_ref, t_ref, loss_ref, dx_ref, dg_ref):
        tv = t_ref[...]
        loss, vj = jax.vjp(lambda a, b: f(a, b, tv), x_ref[...], g_ref[...])
        dx, dg = vj(jnp.ones((1, 1), F32))
        dx_ref[...] = dx

        @pl.when(pl.program_id(0) == 0)
        def _():
            dg_ref[...] = jnp.zeros_like(dg_ref)
            loss_ref[...] = jnp.zeros_like(loss_ref)

        dg_ref[...] += dg
        loss_ref[...] += jnp.broadcast_to(loss, loss_ref.shape)

    return pl.pallas_call(
        body, grid=(L // TL,),
        in_specs=[pl.BlockSpec((TL, D_MODEL), lambda i: (i, 0)), pl.BlockSpec((1, D_MODEL), lambda i: (0, 0)),
                  pl.BlockSpec((TL, D_MODEL), lambda i: (i, 0))],
        out_specs=[pl.BlockSpec((8, 128), lambda i: (0, 0)), pl.BlockSpec((TL, D_MODEL), lambda i: (i, 0)),
                   pl.BlockSpec((1, D_MODEL), lambda i: (0, 0))],
        out_shape=[jax.ShapeDtypeStruct((8, 128), F32), jax.ShapeDtypeStruct((L, D_MODEL), F32),
                   jax.ShapeDtypeStruct((1, D_MODEL), F32)],
        name="loss_bwd", compiler_params=_cp("arbitrary"))(x, g, tgt)


def _a_pre(val, gate):
    return val * _sigmoid(gate)


def _a_post(acc, az, cb, lng, lnb, pw, pwb):
    t = acc + cb
    mu = jnp.mean(t, axis=-1, keepdims=True)
    xc = t - mu
    ln = xc * lax.rsqrt(jnp.mean(xc * xc, axis=-1, keepdims=True) + EPS) * lng + lnb
    return (_mm(_silu(ln), pw) + pwb) * _silu(az)


def _halo_map(tl, halo, col):
    r = tl // halo
    return lambda i: (jnp.maximum(i * r - 1, 0), col)


def _a_fwd(proj, cw, cb, lng, lnb, pw, pwb):
    L = proj.shape[0]

    def body(vg_ref, az_ref, hvg_ref, cw_ref, cb_ref, lng_ref, lnb_ref, pw_ref, pwb_ref, o_ref):
        keep = (pl.program_id(0) > 0).astype(F32)
        a_h = _a_pre(hvg_ref[:, 0:BR], hvg_ref[:, BR:2 * BR]) * keep
        a_t = _a_pre(vg_ref[:, 0:BR], vg_ref[:, BR:2 * BR])
        ext = jnp.concatenate([a_h, a_t], axis=0)
        acc = _conv_taps(ext, cw_ref, HALO_A, K_A, TL)
        o_ref[...] = _a_post(acc, az_ref[...], cb_ref[...], lng_ref[...], lnb_ref[...], pw_ref[...], pwb_ref[...])

    vec = pl.BlockSpec((1, BR), lambda i: (0, 0))
    return pl.pallas_call(
        body, grid=(L // TL,),
        in_specs=[pl.BlockSpec((TL, 2 * BR), lambda i: (i, 0)), pl.BlockSpec((TL, BR), lambda i: (i, 2)),
                  pl.BlockSpec((HALO_A, 2 * BR), _halo_map(TL, HALO_A, 0)),
                  pl.BlockSpec((HALO_A, BR), lambda i: (0, 0)), vec, vec, vec,
                  pl.BlockSpec((BR, BR), lambda i: (0, 0)), vec],
        out_specs=pl.BlockSpec((TL, BR), lambda i: (i, 0)),
        out_shape=jax.ShapeDtypeStruct((L, BR), F32),
        name="a_fwd", compiler_params=_cp("parallel"))(proj, proj, proj, cw, cb, lng, lnb, pw, pwb)


def _a_bwd(proj, dmix, cw, cb, lng, lnb, pw, pwb):
    L = proj.shape[0]
    n = L // TL

    def body(vg_ref, az_ref, hvg_ref, dy_ref, cw_ref, cb_ref, lng_ref, lnb_ref, pw_ref, pwb_ref,
             dp_ref, dcw_ref, dcb_ref, dlng_ref, dlnb_ref, dpw_ref, dpwb_ref, carry_ref):
        i = pl.program_id(0)

        @pl.when(i == 0)
        def _():
            carry_ref[...] = jnp.zeros_like(carry_ref)
            for r in (dcw_ref, dcb_ref, dlng_ref, dlnb_ref, dpw_ref, dpwb_ref):
                r[...] = jnp.zeros_like(r)

        keep = (i < n - 1).astype(F32)
        val, gate = vg_ref[:, 0:BR], vg_ref[:, BR:2 * BR]
        a_h = _a_pre(hvg_ref[:, 0:BR], hvg_ref[:, BR:2 * BR]) * keep
        a_t, vj_pre = jax.vjp(_a_pre, val, gate)
        ext = jnp.concatenate([a_h, a_t], axis=0)
        acc = _conv_taps(ext, cw_ref, HALO_A, K_A, TL)
        _, vj_post = jax.vjp(_a_post, acc, az_ref[...], cb_ref[...], lng_ref[...], lnb_ref[...], pw_ref[...],
                             pwb_ref[...])
        dacc, daz, dcb, dlng, dlnb, dpw, dpwb = vj_post(dy_ref[...])
        dext = _conv_taps_bwd(ext, cw_ref, dcw_ref, dacc, HALO_A, K_A, TL)
        da = _add_tail(dext[HALO_A:], carry_ref[...])
        carry_ref[...] = dext[:HALO_A]
        dval, dgate = vj_pre(da)
        dp_ref[:, 0:BR] = dval
        dp_ref[:, BR:2 * BR] = dgate
        dp_ref[:, 2 * BR:3 * BR] = daz
        dcb_ref[...] += dcb
        dlng_ref[...] += dlng
        dlnb_ref[...] += dlnb
        dpw_ref[...] += dpw
        dpwb_ref[...] += dpwb

    rev = lambda i: n - 1 - i
    vec = pl.BlockSpec((1, BR), lambda i: (0, 0))
    hmap = _halo_map(TL, HALO_A, 0)
    return pl.pallas_call(
        body, grid=(n,),
        in_specs=[pl.BlockSpec((TL, 2 * BR), lambda i: (rev(i), 0)), pl.BlockSpec((TL, BR), lambda i: (rev(i), 2)),
                  pl.BlockSpec((HALO_A, 2 * BR), lambda i: hmap(rev(i))),
                  pl.BlockSpec((TL, BR), lambda i: (rev(i), 0)),
                  pl.BlockSpec((HALO_A, BR), lambda i: (0, 0)), vec, vec, vec,
                  pl.BlockSpec((BR, BR), lambda i: (0, 0)), vec],
        out_specs=[pl.BlockSpec((TL, W_A), lambda i: (rev(i), 0)),
                   pl.BlockSpec((HALO_A, BR), lambda i: (0, 0)), vec, vec, vec,
                   pl.BlockSpec((BR, BR), lambda i: (0, 0)), vec],
        out_shape=[jax.ShapeDtypeStruct((L, W_A), F32), jax.ShapeDtypeStruct((HALO_A, BR), F32)]
        + [jax.ShapeDtypeStruct((1, BR), F32)] * 3
        + [jax.ShapeDtypeStruct((BR, BR), F32), jax.ShapeDtypeStruct((1, BR), F32)],
        scratch_shapes=[pltpu.VMEM((HALO_A, BR), F32)],
        name="a_bwd", compiler_params=_cp("arbitrary"))(proj, proj, proj, dmix, cw, cb, lng, lnb, pw, pwb)


def _c_pre(cg, xc):
    return cg * xc


def _c_post(acc, bg, cz):
    return bg * acc * _silu(cz)


def _c_fwd(proj, cw):
    L = proj.shape[0]

    def body(bg_ref, cx_ref, cz_ref, hcx_ref, cw_ref, o_ref):
        keep = (pl.program_id(0) > 0).astype(F32)
        p_h = _c_pre(hcx_ref[:, 0:BR], hcx_ref[:, BR:2 * BR]) * keep
        p_t = _c_pre(cx_ref[:, 0:BR], cx_ref[:, BR:2 * BR])
        ext = jnp.concatenate([p_h, p_t], axis=0)
        acc = _conv_taps(ext, cw_ref, HALO_S, K_C, TL)
        o_ref[...] = _c_post(acc, bg_ref[...], cz_ref[...])

    return pl.pallas_call(
        body, grid=(L // TL,),
        in_specs=[pl.BlockSpec((TL, BR), lambda i: (i, 5)), pl.BlockSpec((TL, 2 * BR), lambda i: (i, 3)),
                  pl.BlockSpec((TL, BR), lambda i: (i, 8)),
                  pl.BlockSpec((HALO_S, 2 * BR), _halo_map(TL, HALO_S, 3)),
                  pl.BlockSpec((HALO_S, BR), lambda i: (0, 0))],
        out_specs=pl.BlockSpec((TL, BR), lambda i: (i, 0)),
        out_shape=jax.ShapeDtypeStruct((L, BR), F32),
        name="c_fwd", compiler_params=_cp("parallel"))(proj, proj, proj, proj, cw)


def _c_bwd(proj, dmix, cw):
    L = proj.shape[0]
    n = L // TL

    def body(bg_ref, cx_ref, cz_ref, hcx_ref, dy_ref, cw_ref, dp_ref, dcw_ref, carry_ref):
        i = pl.program_id(0)

        @pl.when(i == 0)
        def _():
            carry_ref[...] = jnp.zeros_like(carry_ref)
            dcw_ref[...] = jnp.zeros_like(dcw_ref)

        keep = (i < n - 1).astype(F32)
        p_h = _c_pre(hcx_ref[:, 0:BR], hcx_ref[:, BR:2 * BR]) * keep
        p_t, vj_pre = jax.vjp(_c_pre, cx_ref[:, 0:BR], cx_ref[:, BR:2 * BR])
        ext = jnp.concatenate([p_h, p_t], axis=0)
        acc = _conv_taps(ext, cw_ref, HALO_S, K_C, TL)
        _, vj_post = jax.vjp(_c_post, acc, bg_ref[...], cz_ref[...])
        dacc, dbg, dcz = vj_post(dy_ref[...])
        dext = _conv_taps_bwd(ext, cw_ref, dcw_ref, dacc, HALO_S, K_C, TL)
        dp = _add_tail(dext[HALO_S:], carry_ref[...])
        carry_ref[...] = dext[:HALO_S]
        dcg, dxc = vj_pre(dp)
        dp_ref[:, 0:BR] = dbg
        dp_ref[:, BR:2 * BR] = dcg
        dp_ref[:, 2 * BR:3 * BR] = dxc
        dp_ref[:, 3 * BR:4 * BR] = dcz

    rev = lambda i: n - 1 - i
    hmap = _halo_map(TL, HALO_S, 3)
    return pl.pallas_call(
        body, grid=(n,),
        in_specs=[pl.BlockSpec((TL, BR), lambda i: (rev(i), 5)), pl.BlockSpec((TL, 2 * BR), lambda i: (rev(i), 3)),
                  pl.BlockSpec((TL, BR), lambda i: (rev(i), 8)),
                  pl.BlockSpec((HALO_S, 2 * BR), lambda i: hmap(rev(i))),
                  pl.BlockSpec((TL, BR), lambda i: (rev(i), 2)),
                  pl.BlockSpec((HALO_S, BR), lambda i: (0, 0))],
        out_specs=[pl.BlockSpec((TL, W_C), lambda i: (rev(i), 0)), pl.BlockSpec((HALO_S, BR), lambda i: (0, 0))],
        out_shape=[jax.ShapeDtypeStruct((L, W_C), F32), jax.ShapeDtypeStruct((HALO_S, BR), F32)],
        scratch_shapes=[pltpu.VMEM((HALO_S, BR), F32)],
        name="c_bwd", compiler_params=_cp("arbitrary"))(proj, proj, proj, proj, dmix, cw)


def _s5_prep_fn(lre, lim, ldt, bre, bim, cre, cim):
    grp = lax.broadcasted_iota(jnp.int32, (128, NS), 0)
    lane = lax.broadcasted_iota(jnp.int32, (128, NS), 1)
    expand = (grp == lane // S5_P).astype(F32)
    dt = jnp.exp(_dot_hi(jnp.broadcast_to(ldt, (8, 128)), expand)[0:1])
    lr = jnp.minimum(lre, -1e-4)
    mag = jnp.exp(lr * dt)
    ar = mag * jnp.cos(lim * dt)
    ai = mag * jnp.sin(lim * dt)
    den = lr * lr + lim * lim
    fr = ((ar - 1.0) * lr + ai * lim) / den
    fi = (ai * lr - (ar - 1.0) * lim) / den
    bbr = fr * bre - fi * bim
    bbi = fr * bim + fi * bre
    row = lax.broadcasted_iota(jnp.int32, (BR, NS), 0)
    col = lax.broadcasted_iota(jnp.int32, (BR, NS), 1)
    blk = (row // S5_H == col // S5_P).astype(F32)

    def embed(t):
        return jnp.concatenate([t] * S5_G, axis=0) * blk

    bemb = jnp.concatenate([embed(bbr), embed(bbi)], axis=1)
    cemb = jnp.concatenate([embed(cre), embed(-cim)], axis=1)
    return ar, ai, bemb, cemb


def _s5_prep(lre, lim, ldt, bre, bim, cre, cim):
    def body(*refs):
        outs = _s5_prep_fn(*[r[...] for r in refs[:7]])
        for r, o in zip(refs[7:], outs):
            r[...] = o

    return pl.pallas_call(
        body,
        out_shape=[jax.ShapeDtypeStruct((1, NS), F32)] * 2 + [jax.ShapeDtypeStruct((BR, 2 * NS), F32)] * 2,
        name="s5_prep", compiler_params=pltpu.CompilerParams(vmem_limit_bytes=VMEM_LIMIT),
    )(lre, lim, ldt, bre, bim, cre, cim)


def _s5_prep_bwd(lre, lim, ldt, bre, bim, cre, cim, dar, dai, dbemb, dcemb):
    def body(*refs):
        _, vj = jax.vjp(_s5_prep_fn, *[r[...] for r in refs[:7]])
        grads = vj(tuple(r[...] for r in refs[7:11]))
        for r, o in zip(refs[11:], grads):
            r[...] = o

    return pl.pallas_call(
        body,
        out_shape=[jax.ShapeDtypeStruct((1, NS), F32)] * 2 + [jax.ShapeDtypeStruct((1, 128), F32)]
        + [jax.ShapeDtypeStruct((S5_H, NS), F32)] * 4,
        name="s5_prep_bwd", compiler_params=pltpu.CompilerParams(vmem_limit_bytes=VMEM_LIMIT),
    )(lre, lim, ldt, bre, bim, cre, cim, dar, dai, dbemb, dcemb)


def _s5_scan(xr, xi, ar, ai, reverse):
    n = xr.shape[0]
    row = lax.broadcasted_iota(jnp.int32, (n, 1), 0)
    pr, pi = ar, ai
    d = 1
    while d < n:
        if d % 8:
            if reverse:
                m = row < n - d
                sr = jnp.where(m, _roll(xr, n - d), 0.0)
                si = jnp.where(m, _roll(xi, n - d), 0.0)
            else:
                m = row >= d
                sr = jnp.where(m, _roll(xr, d), 0.0)
                si = jnp.where(m, _roll(xi, d), 0.0)
            xr, xi = xr + pr * sr - pi * si, xi + pr * si + pi * sr
        elif reverse:
            sr, si = xr[d:], xi[d:]
            xr, xi = (jnp.concatenate([xr[:n - d] + pr * sr - pi * si, xr[n - d:]], axis=0),
                      jnp.concatenate([xi[:n - d] + pr * si + pi * sr, xi[n - d:]], axis=0))
        else:
            sr, si = xr[:n - d], xi[:n - d]
            xr, xi = (jnp.concatenate([xr[:d], xr[d:] + pr * sr - pi * si], axis=0),
                      jnp.concatenate([xi[:d], xi[d:] + pr * si + pi * sr], axis=0))
        pr, pi = pr * pr - pi * pi, 2.0 * pr * pi
        d *= 2
    return xr, xi


def _s5_states(u, bemb_b, ar, ai, sin_r, sin_i):
    bu = jnp.dot(u.astype(BF), bemb_b, preferred_element_type=F32)
    first = lax.broadcasted_iota(jnp.int32, (u.shape[0], 1), 0) == 0
    xr = bu[:, :NS] + jnp.where(first, ar * sin_r - ai * sin_i, 0.0)
    xi = bu[:, NS:] + jnp.where(first, ar * sin_i + ai * sin_r, 0.0)
    return _s5_scan(xr, xi, ar, ai, False)


def _b_post(yssm, u, bz, dsk, gw, gb):
    z = jax.nn.gelu(yssm + dsk * u)
    return z * _sigmoid(_mm(z, gw) + gb) * _silu(bz)


def _b_fwd(proj, ar, ai, bemb, cemb, dsk, gw, gb, exchange=None):
    L = proj.shape[0]
    n = L // TL

    def body(u_ref, bz_ref, ar_ref, ai_ref, be_ref, ce_ref, dsk_ref, gw_ref, gb_ref, o_ref, sin_ref, st_ref,
             carry_ref):
        @pl.when(pl.program_id(0) == 0)
        def _():
            carry_ref[...] = jnp.zeros_like(carry_ref)

        sin = carry_ref[...]
        sin_ref[0] = sin
        u = u_ref[...]
        sr, si = _s5_states(u, be_ref[...].astype(BF), ar_ref[...], ai_ref[...], sin[:, :NS], sin[:, NS:])
        carry_ref[:, :NS] = sr[TL - 1:TL]
        carry_ref[:, NS:] = si[TL - 1:TL]
        st_ref[:, :NS] = sr
        st_ref[:, NS:] = si
        s = jnp.concatenate([sr, si], axis=1).astype(BF)
        yssm = lax.dot_general(s, ce_ref[...].astype(BF), (((1,), (1,)), ((), ())), preferred_element_type=F32)
        o_ref[...] = _b_post(yssm, u, bz_ref[...], dsk_ref[...], gw_ref[...], gb_ref[...])

    vec = pl.BlockSpec((1, BR), lambda i: (0, 0))
    svec = pl.BlockSpec((1, NS), lambda i: (0, 0))
    emb = pl.BlockSpec((BR, 2 * NS), lambda i: (0, 0))
    return _sweep_with_exchange(
        body, n,
        in_specs=[pl.BlockSpec((TL, BR), lambda i: (i, 3)), pl.BlockSpec((TL, BR), lambda i: (i, 4)),
                  svec, svec, emb, emb, vec, pl.BlockSpec((BR, BR), lambda i: (0, 0)), vec],
        out_specs=[pl.BlockSpec((TL, BR), lambda i: (i, 0)), pl.BlockSpec((1, 1, 2 * NS), lambda i: (i, 0, 0)),
                   pl.BlockSpec((TL, 2 * NS), lambda i: (i, 0))],
        out_shape=[jax.ShapeDtypeStruct((L, BR), F32), jax.ShapeDtypeStruct((n, 1, 2 * NS), F32),
                   jax.ShapeDtypeStruct((L, 2 * NS), F32)],
        scratch_shapes=[pltpu.VMEM((1, 2 * NS), F32)],
        args=(proj, proj, ar, ai, bemb, cemb, dsk, gw, gb), exchange=exchange, name="b_fwd")


def _b_bwd(proj, dmix, sin_all, states, ar, ai, bemb, cemb, dsk, gw, gb, exchange=None):
    L = proj.shape[0]
    n = L // TL

    def body(u_ref, bz_ref, dy_ref, sin_ref, st_ref, ar_ref, ai_ref, be_ref, ce_ref, dsk_ref, gw_ref, gb_ref,
             dp_ref, dar_ref, dai_ref, dbe_ref, dce_ref, ddsk_ref, dgw_ref, dgb_ref, carry_ref):
        i = pl.program_id(0)

        @pl.when(i == 0)
        def _():
            carry_ref[...] = jnp.zeros_like(carry_ref)
            for r in (dar_ref, dai_ref, dbe_ref, dce_ref, ddsk_ref, dgw_ref, dgb_ref):
                r[...] = jnp.zeros_like(r)

        u = u_ref[...]
        ar, ai = ar_ref[...], ai_ref[...]
        be_b, ce_b = be_ref[...].astype(BF), ce_ref[...].astype(BF)
        sin = sin_ref[0]
        sr, si = st_ref[:, :NS], st_ref[:, NS:]
        s_b = st_ref[...].astype(BF)
        yssm = lax.dot_general(s_b, ce_b, (((1,), (1,)), ((), ())), preferred_element_type=F32)
        _, vj = jax.vjp(_b_post, yssm, u, bz_ref[...], dsk_ref[...], gw_ref[...], gb_ref[...])
        dyssm, du, dbz, ddsk, dgw, dgb = vj(dy_ref[...])
        dy_b = dyssm.astype(BF)
        dce_ref[...] += lax.dot_general(dy_b, s_b, (((0,), (0,)), ((), ())), preferred_element_type=F32)
        gs = jnp.dot(dy_b, ce_b, preferred_element_type=F32)
        last = lax.broadcasted_iota(jnp.int32, (TL, 1), 0) == TL - 1
        cr, ci = carry_ref[:, :NS], carry_ref[:, NS:]
        gr = gs[:, :NS] + jnp.where(last, ar * cr + ai * ci, 0.0)
        gi = gs[:, NS:] + jnp.where(last, ar * ci - ai * cr, 0.0)
        dsr, dsi = _s5_scan(gr, gi, ar, -ai, True)
        carry_ref[:, :NS] = dsr[0:1]
        carry_ref[:, NS:] = dsi[0:1]
        first = lax.broadcasted_iota(jnp.int32, (TL, 1), 0) == 0
        pr = jnp.where(first, sin[:, :NS], _roll(sr, 1))
        pi = jnp.where(first, sin[:, NS:], _roll(si, 1))
        dar_ref[...] += jnp.sum(dsr * pr + dsi * pi, axis=0, keepdims=True)
        dai_ref[...] += jnp.sum(dsi * pr - dsr * pi, axis=0, keepdims=True)
        ds_b = jnp.concatenate([dsr, dsi], axis=1).astype(BF)
        dbe_ref[...] += lax.dot_general(u.astype(BF), ds_b, (((0,), (0,)), ((), ())), preferred_element_type=F32)
        du = du + lax.dot_general(ds_b, be_b, (((1,), (1,)), ((), ())), preferred_element_type=F32)
        dp_ref[:, 0:BR] = du
        dp_ref[:, BR:2 * BR] = dbz
        ddsk_ref[...] += ddsk
        dgw_ref[...] += dgw
        dgb_ref[...] += dgb

    rev = lambda i: n - 1 - i
    vec = pl.BlockSpec((1, BR), lambda i: (0, 0))
    svec = pl.BlockSpec((1, NS), lambda i: (0, 0))
    emb = pl.BlockSpec((BR, 2 * NS), lambda i: (0, 0))
    mat = pl.BlockSpec((BR, BR), lambda i: (0, 0))
    return _sweep_with_exchange(
        body, n,
        in_specs=[pl.BlockSpec((TL, BR), lambda i: (rev(i), 3)), pl.BlockSpec((TL, BR), lambda i: (rev(i), 4)),
                  pl.BlockSpec((TL, BR), lambda i: (rev(i), 1)),
                  pl.BlockSpec((1, 1, 2 * NS), lambda i: (rev(i), 0, 0)),
                  pl.BlockSpec((TL, 2 * NS), lambda i: (rev(i), 0)),
                  svec, svec, emb, emb, vec, mat, vec],
        out_specs=[pl.BlockSpec((TL, W_B), lambda i: (rev(i), 0)), svec, svec, emb, emb, vec, mat, vec],
        out_shape=[jax.ShapeDtypeStruct((L, W_B), F32)] + [jax.ShapeDtypeStruct((1, NS), F32)] * 2
        + [jax.ShapeDtypeStruct((BR, 2 * NS), F32)] * 2
        + [jax.ShapeDtypeStruct((1, BR), F32), jax.ShapeDtypeStruct((BR, BR), F32), jax.ShapeDtypeStruct((1, BR), F32)],
        scratch_shapes=[pltpu.VMEM((1, 2 * NS), F32)],
        args=(proj, proj, dmix, sin_all, states, ar, ai, bemb, cemb, dsk, gw, gb), exchange=exchange, name="b_bwd")


def _half_masks(rows):
    lane = lax.broadcasted_iota(jnp.int32, (rows, 2 * DN_D), 1)
    return lane < DN_D, lane >= DN_D


def _bd(x):
    left, right = _half_masks(x.shape[0])
    return jnp.concatenate([jnp.where(left, x, 0.0), jnp.where(right, x, 0.0)], axis=0)


@jax.custom_vjp
def _segsum(x):
    r = lax.broadcasted_iota(jnp.int32, (2 * DN_D, 2 * DN_D), 0) // DN_D
    c = lax.broadcasted_iota(jnp.int32, (2 * DN_D, 2 * DN_D), 1) // DN_D
    ones = (r == c).astype(BF)
    hi, lo = _split(x)
    return jnp.dot(hi, ones, preferred_element_type=F32) + jnp.dot(lo, ones, preferred_element_type=F32)


_segsum.defvjp(lambda x: (_segsum(x), None), lambda _, g: (_segsum(g),))


def _pair_t(x):
    t = _bd(x).T
    return t[:DN_D] + t[DN_D:]


@jax.custom_vjp
def _pair_inv(lms):
    n = DN_D
    row = lax.broadcasted_iota(jnp.int32, (n, 2 * n), 0)
    col = lax.broadcasted_iota(jnp.int32, (n, 2 * n), 1) % n
    eye = (row == col).astype(F32)
    accs = [eye - lm for lm in lms]
    pws = list(lms)
    k = 2
    while k < n:
        pws = [_dot3(p, _bd(p)) for p in pws]
        accs = [a + _dot3(a, _bd(p)) for a, p in zip(accs, pws)]
        k *= 2
    return tuple(accs)


def _pi_b(a, g):
    ats = [_pair_t(x) for x in a]
    tmp = [_dot3(at, _bd(gi)) for at, gi in zip(ats, g)]
    return (tuple(-_dot3(t, _bd(at)) for t, at in zip(tmp, ats)),)


def _pi_f(lms):
    a = _pair_inv(lms)
    return a, a


_pair_inv.defvjp(_pi_f, _pi_b)


@jax.custom_vjp
def _pair_known_inverse(lms, inv):
    return inv


_pair_known_inverse.defvjp(lambda lms, inv: (inv, inv),
                           lambda a, g: (_pi_b(a, g)[0], tuple(jnp.zeros_like(x) for x in a)))


def _d_tile(cq, ab, dz, sb0, sb1, p1, p2, ng, known=None):
    c = DN_C
    chunks = range(cq.shape[0] // c)
    units = [(g, p) for g in chunks for p in range(DN_P)]
    n = range(len(units))
    qkv = _silu(cq)
    gall = -jnp.exp(p1) * jax.nn.softplus(ab + p2)
    ball = _sigmoid(ab)
    left, _ = _half_masks(c)
    row = lax.broadcasted_iota(jnp.int32, (c, 2 * c), 0)
    col = lax.broadcasted_iota(jnp.int32, (c, 2 * c), 1) % c
    causal, strict = row >= col, row > col
    sq = lax.broadcasted_iota(jnp.int32, (c, c), 0) >= lax.broadcasted_iota(jnp.int32, (c, c), 1)
    gc_all = [_dot_hi(sq.astype(F32), gall[g * c:(g + 1) * c]) for g in chunks]
    gc_t = [t.T for t in gc_all]
    bdm = (lax.broadcasted_iota(jnp.int32, (2 * c, 2 * c), 0) // c
           == lax.broadcasted_iota(jnp.int32, (2 * c, 2 * c), 1) // c).astype(F32)

    def two(t, base, g, p):
        return t[g * c:(g + 1) * c, base + 2 * p * DN_D:base + 2 * (p + 1) * DN_D]

    def per_head(t, off, p):
        return jnp.where(left, t[:, off + 2 * p:off + 2 * p + 1], t[:, off + 2 * p + 1:off + 2 * p + 2])

    q = [two(qkv, 0, g, p) for g, p in units]
    k = [two(qkv, BR, g, p) for g, p in units]
    v = [two(qkv, 2 * BR, g, p) for g, p in units]
    q = [t * lax.rsqrt(_segsum(t * t) + EPS) * (DN_D ** -0.5) for t in q]
    k = [t * lax.rsqrt(_segsum(t * t) + EPS) for t in k]
    g2 = [per_head(gc_all[g], 0, p) for g, p in units]
    beta = [per_head(ball[g * c:(g + 1) * c], DN_H, p) for g, p in units]
    grow = [jnp.concatenate([gc_t[g][2 * p:2 * p + 1, :], gc_t[g][2 * p + 1:2 * p + 2, :]], axis=1) for g, p in units]
    decay = [jnp.where(causal, jnp.exp(jnp.where(causal, g2[u] - grow[u], 0.0)), 0.0) for u in n]
    kb = [k[u] * beta[u] for u in n]
    kbd = [_bd(t) for t in k]
    lm = [jnp.where(strict, _mm_nt(kb[u], kbd[u]) * decay[u], 0.0) for u in n]
    ainv = _pair_inv(tuple(lm)) if known is None else _pair_known_inverse(tuple(lm), known)
    egc = [jnp.exp(t) for t in g2]
    uw = [_mm3(ainv[u], jnp.concatenate([_bd(v[u] * beta[u]), _bd(kb[u] * egc[u])], axis=1)) for u in n]
    attn = [_mm_nt(q[u], kbd[u]) * decay[u] for u in n]
    glast = [t[c - 1:c, :] for t in g2]
    kd = [k[u] * jnp.exp(glast[u] - g2[u]) for u in n]
    qd = [q[u] * egc[u] for u in n]
    ng2 = jnp.concatenate([ng, ng], axis=1)
    sbd, starts, outs = [sb0, sb1], [], []
    for g in chunks:
        starts.append(tuple(sbd))
        us = [g * DN_P + p for p in range(DN_P)]
        vnew = [uw[u][:, :2 * DN_D] - _mm(uw[u][:, 2 * DN_D:], sbd[p]) for p, u in enumerate(us)]
        o = [_mm(qd[u], sbd[p]) + _mm(attn[u], _bd(vnew[p])) for p, u in enumerate(us)]
        sbd = [sbd[p] * jnp.exp(glast[u]) + _mm_tn(kd[u], vnew[p]) * bdm for p, u in enumerate(us)]
        outs.append(jnp.concatenate([t * lax.rsqrt(_segsum(t * t) * (1.0 / DN_D) + EPS) * ng2 for t in o], axis=1))
    yd = jnp.concatenate(outs, axis=0) * _silu(dz)
    return (yd, *sbd), (starts, ainv)


def _d_fwd(proj, cw, p1, p2, ng, exchange=None):
    L = proj.shape[0]
    DN_G, DN_T = DN_G_FWD, DN_G_FWD * DN_C
    n = L // DN_T

    def body(qkv_ref, ab_ref, dz_ref, hq_ref, cw_ref, p1_ref, p2_ref, ng_ref, o_ref, sall_ref, inv_ref, s_ref):
        i = pl.program_id(0)

        @pl.when(i == 0)
        def _():
            s_ref[...] = jnp.zeros_like(s_ref)

        keep = (i > 0).astype(F32)
        ext = jnp.concatenate([hq_ref[...] * keep, qkv_ref[...]], axis=0)
        cq = _conv_taps(ext, cw_ref, HALO_S, K_DN, DN_T)
        out, (starts, ainv) = _d_tile(cq, ab_ref[...], dz_ref[...], s_ref[0], s_ref[1], p1_ref[...], p2_ref[...],
                                      ng_ref[...])
        o_ref[...] = out[0]
        for p in range(DN_P):
            s_ref[p] = out[1 + p]
            for g in range(DN_G):
                sall_ref[g, p] = starts[g][p]
                inv_ref[g, p] = ainv[g * DN_P + p]

    return _sweep_with_exchange(
        body, n,
        in_specs=[pl.BlockSpec((DN_T, 3 * BR), lambda i: (i, 3)), pl.BlockSpec((DN_T, 128), lambda i: (i, 26)),
                  pl.BlockSpec((DN_T, BR), lambda i: (i, 12)),
                  pl.BlockSpec((HALO_S, 3 * BR), _halo_map(DN_T, HALO_S, 3)),
                  pl.BlockSpec((HALO_S, 3 * BR), lambda i: (0, 0)),
                  pl.BlockSpec((1, 128), lambda i: (0, 0)), pl.BlockSpec((1, 128), lambda i: (0, 0)),
                  pl.BlockSpec((1, DN_D), lambda i: (0, 0))],
        out_specs=[pl.BlockSpec((DN_T, BR), lambda i: (i, 0)),
                   pl.BlockSpec((DN_G,) + DN_STATE, lambda i: (i, 0, 0, 0)),
                   pl.BlockSpec((DN_G,) + DN_INV, lambda i: (i, 0, 0, 0))],
        out_shape=[jax.ShapeDtypeStruct((L, BR), F32), jax.ShapeDtypeStruct((L // DN_C,) + DN_STATE, F32),
                   jax.ShapeDtypeStruct((L // DN_C,) + DN_INV, F32)],
        scratch_shapes=[pltpu.VMEM(DN_STATE, F32)],
        args=(proj, proj, proj, proj, cw, p1, p2, ng), exchange=exchange, name="d_fwd")


def _sweep_with_exchange(body, steps, in_specs, out_specs, out_shape, scratch_shapes, args, exchange, name):
    if exchange is None:
        res = pl.pallas_call(body, grid=(steps,), in_specs=in_specs, out_specs=out_specs, out_shape=out_shape,
                             scratch_shapes=scratch_shapes, name=name, compiler_params=_cp("arbitrary"))(*args)
        return res, None
    several = isinstance(exchange, list)
    exs = [_Exchange(*e) for e in (exchange if several else [exchange])]
    xs = [x for e in (exchange if several else [exchange]) for x in e[0]]
    ni, no, ns, na = len(in_specs), len(out_specs), len(scratch_shapes), len(xs)

    def carried(*refs):
        ins, xin = refs[:ni], refs[ni:ni + na]
        outs, xout = refs[ni + na:ni + na + no], refs[ni + na + no:ni + 2 * na + no]
        scr, sems = refs[ni + 2 * na + no:ni + 2 * na + no + ns], refs[ni + 2 * na + no + ns:]

        def each(fn_name):
            off = 0
            for j, ex in enumerate(exs):
                getattr(ex, fn_name)(xin[off:off + ex.na], xout[off:off + ex.na], sems[3 * j:3 * j + 3])
                off += ex.na

        @pl.when(pl.program_id(0) == 0)
        def _():
            each("start")

        body(*ins, *outs, *scr)

        @pl.when(pl.program_id(0) == steps - 1)
        def _():
            each("wait")

    res = pl.pallas_call(carried, grid=(steps,), in_specs=list(in_specs) + [s for ex in exs for s in ex.in_specs],
                         out_specs=list(out_specs) + [s for ex in exs for s in ex.out_specs],
                         out_shape=list(out_shape) + [s for ex in exs for s in ex.out_shape],
                         scratch_shapes=list(scratch_shapes) + [s for ex in exs for s in ex.scratch_shapes],
                         name=name + "_x", compiler_params=_cp("arbitrary"))(*args, *xs)
    got, off = [], no
    for ex in exs:
        got.append(res[off:off + ex.na])
        off += ex.na
    return res[:no], (got if several else got[0])


def _d_bwd(proj, dmix, sall, inv, cw, p1, p2, ng, exchange=None):
    L = proj.shape[0]
    DN_G, DN_T = DN_G_BWD, DN_G_BWD * DN_C
    n = L // DN_T

    def body(qkv_ref, ab_ref, dz_ref, hq_ref, dy_ref, sall_ref, inv_ref, cw_ref, p1_ref, p2_ref, ng_ref,
             dp_ref, dcw_ref, dp1_ref, dp2_ref, dng_ref, ds_ref, carry_ref):
        i = pl.program_id(0)

        @pl.when(i == 0)
        def _():
            ds_ref[...] = jnp.zeros_like(ds_ref)
            carry_ref[...] = jnp.zeros_like(carry_ref)
            for r in (dcw_ref, dp1_ref, dp2_ref, dng_ref):
                r[...] = jnp.zeros_like(r)

        keep = (i < n - 1).astype(F32)
        ext = jnp.concatenate([hq_ref[...] * keep, qkv_ref[...]], axis=0)
        cq = _conv_taps(ext, cw_ref, HALO_S, K_DN, DN_T)
        known = tuple(inv_ref[g, p] for g in range(DN_G) for p in range(DN_P))
        _, vj = jax.vjp(lambda *a: _d_tile(*a, known=known)[0], cq, ab_ref[...], dz_ref[...], sall_ref[0, 0],
                        sall_ref[0, 1], p1_ref[...], p2_ref[...], ng_ref[...])
        dcq, dab, ddz, ds0, ds1, dp1, dp2, dng = vj((dy_ref[...], ds_ref[0], ds_ref[1]))
        dp_ref[:, 3 * BR:4 * BR] = ddz
        dp_ref[:, 4 * BR:4 * BR + 128] = dab
        ds_ref[0] = ds0
        ds_ref[1] = ds1
        dp1_ref[...] += dp1
        dp2_ref[...] += dp2
        dng_ref[...] += dng
        dext = _conv_taps_bwd(ext, cw_ref, dcw_ref, dcq, HALO_S, K_DN, DN_T)
        dp_ref[:, 0:3 * BR] = _add_tail(dext[HALO_S:], carry_ref[...])
        carry_ref[...] = dext[:HALO_S]

    rev = lambda i: n - 1 - i
    hmap = _halo_map(DN_T, HALO_S, 3)
    v128 = pl.BlockSpec((1, 128), lambda i: (0, 0))
    return _sweep_with_exchange(
        body, n,
        in_specs=[pl.BlockSpec((DN_T, 3 * BR), lambda i: (rev(i), 3)), pl.BlockSpec((DN_T, 128), lambda i: (rev(i), 26)),
                  pl.BlockSpec((DN_T, BR), lambda i: (rev(i), 12)),
                  pl.BlockSpec((HALO_S, 3 * BR), lambda i: hmap(rev(i))),
                  pl.BlockSpec((DN_T, BR), lambda i: (rev(i), 3)),
                  pl.BlockSpec((DN_G,) + DN_STATE, lambda i: (rev(i), 0, 0, 0)),
                  pl.BlockSpec((DN_G,) + DN_INV, lambda i: (rev(i), 0, 0, 0)),
                  pl.BlockSpec((HALO_S, 3 * BR), lambda i: (0, 0)), v128, v128,
                  pl.BlockSpec((1, DN_D), lambda i: (0, 0))],
        out_specs=[pl.BlockSpec((DN_T, W_D), lambda i: (rev(i), 0)),
                   pl.BlockSpec((HALO_S, 3 * BR), lambda i: (0, 0)), v128, v128,
                   pl.BlockSpec((1, DN_D), lambda i: (0, 0))],
        out_shape=[jax.ShapeDtypeStruct((L, W_D), F32), jax.ShapeDtypeStruct((HALO_S, 3 * BR), F32),
                   jax.ShapeDtypeStruct((1, 128), F32), jax.ShapeDtypeStruct((1, 128), F32),
                   jax.ShapeDtypeStruct((1, DN_D), F32)],
        scratch_shapes=[pltpu.VMEM(DN_STATE, F32), pltpu.VMEM((HALO_S, 3 * BR), F32)],
        args=(proj, proj, proj, proj, dmix, sall, inv, cw, p1, p2, ng), exchange=exchange, name="d_bwd")


def _pick_rows(rows, cap):
    best = 8
    for t in range(8, cap + 1, 8):
        if rows % t == 0:
            best = t
    return best


def _adamw(w, g, m, v, name, lead=None):
    rows, rest = w.shape[0], w.shape[1:]
    tr = _pick_rows(rows, 512) if lead is None else lead
    c1 = 1.0 - ADAM_B1 ** ADAM_STEP
    c2 = 1.0 - ADAM_B2 ** ADAM_STEP

    def body(w_ref, g_ref, m_ref, v_ref, d_ref, mo_ref, vo_ref):
        gv = g_ref[...]
        mn = ADAM_B1 * m_ref[...] + (1.0 - ADAM_B1) * gv
        vn = ADAM_B2 * v_ref[...] + (1.0 - ADAM_B2) * (gv * gv)
        d_ref[...] = -ADAM_LR * ((mn / c1) / (jnp.sqrt(vn / c2) + ADAM_EPS) + ADAM_WD * w_ref[...])
        mo_ref[...] = mn
        vo_ref[...] = vn

    spec = pl.BlockSpec((tr,) + rest, lambda i: (i,) + (0,) * len(rest))
    return pl.pallas_call(
        body, grid=(rows // tr,), in_specs=[spec] * 4, out_specs=[spec] * 3,
        out_shape=[jax.ShapeDtypeStruct(w.shape, F32)] * 3,
        name=name, compiler_params=_cp("parallel"))(w, g, m, v)


def _sum_slots(r, name):
    n, rows, wd = r.shape
    tr = _pick_rows(rows, 384)

    def body(r_ref, o_ref):
        acc = r_ref[0].astype(F32)
        for j in range(1, n):
            acc = acc + r_ref[j].astype(F32)
        o_ref[...] = acc

    return pl.pallas_call(
        body, grid=(rows // tr,),
        in_specs=[pl.BlockSpec((n, tr, wd), lambda i: (0, i, 0))],
        out_specs=pl.BlockSpec((tr, wd), lambda i: (i, 0)),
        out_shape=jax.ShapeDtypeStruct((rows, wd), F32),
        name=name, compiler_params=_cp("parallel"))(r)


AXES = ("x", "y", "c")


def _group_peer(axes, k):
    pos = {a: lax.axis_index(a) for a in AXES}
    idx = 0
    for a in axes:
        idx = idx * 2 + pos[a]
    peer = dict(pos)
    for b, a in enumerate(reversed(axes)):
        if (k >> b) & 1:
            peer[a] = 1 - pos[a]
    return idx, tuple(peer[a] for a in AXES)


MAX_CHUNKS = 4


class _Exchange:
    def __init__(self, xs, axes, mode):
        self.axes, self.mode, self.na, self.n = axes, mode, len(xs), 2 ** len(axes)
        n = self.n
        self.out_shape, self.pieces = [], []
        for x in xs:
            if mode == "gather":
                shape, lead = (n,) + x.shape, x.shape[0]
            elif mode == "scatter":
                shape, lead = x.shape, x.shape[1]
            else:
                shape, lead = (x.shape[0], n * x.shape[1], x.shape[2]), x.shape[0]
            self.out_shape.append(jax.ShapeDtypeStruct(shape, x.dtype))
            big = x.size * x.dtype.itemsize >= (1 << 20)
            if mode == "rows":
                self.pieces.append(lead if lead <= MAX_CHUNKS else 1)
            else:
                self.pieces.append(MAX_CHUNKS if big and lead % (16 * MAX_CHUNKS) == 0 else 1)
        self.in_specs = [pl.BlockSpec(memory_space=pl.ANY)] * self.na
        self.out_specs = [pl.BlockSpec(memory_space=pl.ANY)] * self.na
        self.scratch_shapes = [pltpu.SemaphoreType.DMA((self.na, MAX_CHUNKS, n)),
                               pltpu.SemaphoreType.DMA((self.na, MAX_CHUNKS, n)),
                               pltpu.SemaphoreType.DMA((self.na, MAX_CHUNKS))]

    def _copies(self, x_refs, o_refs, send_sems, recv_sems, local_sems):
        me, _ = _group_peer(self.axes, 0)
        local, remote = [], []
        for a, (x, o) in enumerate(zip(x_refs, o_refs)):
            for c in range(self.pieces[a]):
                if self.mode == "rows":
                    r = x.shape[1]
                    b = slice(None) if self.pieces[a] == 1 else pl.ds(c, 1)
                    src = lambda k, x=x, b=b: x.at[b]
                    dst = o.at[b, pl.ds(me * r, r)]
                else:
                    lead = x.shape[1] if self.mode == "scatter" else x.shape[0]
                    rs = pl.ds(c * (lead // self.pieces[a]), lead // self.pieces[a])
                    if self.mode == "scatter":
                        src = lambda k, x=x, rs=rs: x.at[me ^ k, rs]
                    else:
                        src = lambda k, x=x, rs=rs: x.at[rs]
                    dst = o.at[me, rs]
                local.append(pltpu.make_async_copy(src(0), dst, local_sems.at[a, c]))
                for k in range(1, self.n):
                    remote.append(pltpu.make_async_remote_copy(
                        src_ref=src(k), dst_ref=dst, send_sem=send_sems.at[a, c, k], recv_sem=recv_sems.at[a, c, k],
                        device_id=_group_peer(self.axes, k)[1], device_id_type=MESH))
        return local, remote

    def start(self, x_refs, o_refs, sems):
        local, remote = self._copies(x_refs, o_refs, *sems)
        for cp in local + remote:
            cp.start()

    def wait(self, x_refs, o_refs, sems):
        local, remote = self._copies(x_refs, o_refs, *sems)
        for cp in remote:
            cp.wait_send()
        for cp in remote:
            cp.wait_recv()
        for cp in local:
            cp.wait()


def _exchange(xs, axes, mode, name):
    ex = _Exchange(xs, axes, mode)
    na = ex.na

    def body(*refs):
        ex.start(refs[:na], refs[na:2 * na], refs[2 * na:])
        ex.wait(refs[:na], refs[na:2 * na], refs[2 * na:])

    return pl.pallas_call(body, out_shape=ex.out_shape, in_specs=ex.in_specs, out_specs=ex.out_specs,
                          scratch_shapes=ex.scratch_shapes, name=name)(*xs)


SHARDED_SMALL = (("a_conv_w", 2), ("a_pw_w", 1), ("s5_glu_w", 1), ("c_conv_w", 2), ("d_conv_w", 2))
REPLICATED = ("norm_g", "a_conv_b", "a_ln_g", "a_ln_b", "a_pw_b", "s5_lambda_re", "s5_lambda_im", "s5_b_re",
              "s5_b_im", "s5_c_re", "s5_c_im", "s5_d", "s5_log_dt", "s5_glu_b", "d_a_log", "d_dt_bias",
              "d_norm_g", "final_g")
WEIGHTS = ("norm_g", "w_in", "a_conv_w", "a_conv_b", "a_ln_g", "a_ln_b", "a_pw_w", "a_pw_b", "s5_lambda_re",
           "s5_lambda_im", "s5_b_re", "s5_b_im", "s5_c_re", "s5_c_im", "s5_d", "s5_log_dt", "s5_glu_w",
           "s5_glu_b", "c_conv_w", "d_conv_w", "d_a_log", "d_dt_bias", "d_norm_g", "w_out", "final_g")
LANES = 1024


def _size(shape):
    size = 1
    for d in shape:
        size *= d
    return size


def _slab_rows(shape):
    return -(-_size(shape) // (8 * LANES)) * 8


def _pack(arrs, rows):
    parts = []
    for a in arrs:
        r = _slab_rows(a.shape)
        parts.append(jnp.pad(a.reshape(-1), (0, r * LANES - a.size)).reshape(r, LANES))
    used = sum(p.shape[0] for p in parts)
    if rows > used:
        parts.append(jnp.zeros((rows - used, LANES), parts[0].dtype))
    return jnp.concatenate(parts, axis=0)


def _unpack(slab, shapes):
    out, off = [], 0
    for s in shapes:
        r = _slab_rows(s)
        out.append(slab[off:off + r].reshape(-1)[:_size(s)].reshape(s))
        off += r
    return out


def _rows_for(shapes, mult):
    rows = sum(_slab_rows(s) for s in shapes)
    return -(-rows // mult) * mult


def _row(v, width=None):
    v = v.reshape(1, -1)
    return v if width is None else jnp.pad(v, ((0, 0), (0, width - v.shape[1])))


def _pad_rows(w, rows):
    return jnp.pad(w, ((0, rows - w.shape[0]), (0, 0)))


def _assemble_in(shards):
    n = shards[0].shape[1]
    cut = 3072 - 3 * n
    last = shards[3]
    return jnp.concatenate([shards[0], shards[1], shards[2], last[:, :cut], last[:, cut + 2 * DN_H:],
                            last[:, cut:cut + 2 * DN_H],
                            jnp.zeros((last.shape[0], N_INP - N_IN), last.dtype)], axis=1)


def _layer_fwd(x, p, exchanges):
    proj, h = _proj_fwd(x, p["norm_g"], p["wp"])
    ya = _a_fwd(proj, p["a_cw"], p["a_cb"], p["a_lng"], p["a_lnb"], p["a_pw"], p["a_pwb"])
    ar, ai, bemb, cemb = _s5_prep(*p["s5"])
    got = {}
    (yb, sin_all, states), got["b"] = _b_fwd(proj, ar, ai, bemb, cemb, p["s5_d"], p["glu_w"], p["glu_b"],
                                             exchanges.get("b"))
    yc = _c_fwd(proj, p["c_cw"])
    (yd, sall, inv), got["d"] = _d_fwd(proj, p["d_cw"], p["d_p1"], p["d_p2"], p["d_ng"], exchanges.get("d"))
    xo = _out_fwd(x, ya, yb, yc, yd, p["wo"])
    return xo, dict(x=x, proj=proj, h=h, ys=(ya, yb, yc, yd), sin_all=sin_all, states=states, sall=sall, inv=inv,
                    s5=(ar, ai, bemb, cemb)), got


def _layer_bwd(dxo, p, r, exchanges):
    proj = r["proj"]
    ar, ai, bemb, cemb = r["s5"]
    dmix = _out_bwd_x(dxo, p["wo"])
    dwo = _dwout(*r["ys"], dxo)
    dpa, dcw_a, dcb, dlng, dlnb, dpw, dpwb = _a_bwd(proj, dmix, p["a_cw"], p["a_cb"], p["a_lng"], p["a_lnb"],
                                                     p["a_pw"], p["a_pwb"])
    got = {}
    (dpb, dar, dai, dbe, dce, ddsk, dgw, dgb), got["b"] = _b_bwd(
        proj, dmix, r["sin_all"], r["states"], ar, ai, bemb, cemb, p["s5_d"], p["glu_w"], p["glu_b"],
        exchanges.get("b"))
    dlre, dlim, dldt, dbre, dbim, dcre, dcim = _s5_prep_bwd(*p["s5"], dar, dai, dbe, dce)
    dpc, dcw_c = _c_bwd(proj, dmix, p["c_cw"])
    (dpd, dcw_d, dp1, dp2, dng), got["d"] = _d_bwd(proj, dmix, r["sall"], r["inv"], p["d_cw"], p["d_p1"],
                                                   p["d_p2"], p["d_ng"], exchanges.get("d"))
    dwa, dwb, dwc, dwd = _dwin(r["h"], dpa, dpb, dpc, dpd)
    dwin = jnp.concatenate([dwa, dwb, dwc, dwd[:, :3 * BR], dwd[:, 4 * BR:4 * BR + 2 * DN_H],
                            dwd[:, 3 * BR:4 * BR]], axis=1)
    ex_proj = exchanges.get("proj")
    if callable(ex_proj):
        ex_proj = ex_proj(got["d"], dwin, dwo)
    (dx, dg), got["proj"] = _proj_bwd_x(r["x"], p["norm_g"], dpa, dpb, dpc, dpd, p["wp"], dxo, ex_proj)

    def unrows(t, perm):
        return jnp.transpose(t.reshape(S5_H, S5_G, S5_P), perm)

    grads = dict(
        norm_g=dg.reshape(-1), w_in=dwin, a_conv_w=dcw_a[:K_A], a_conv_b=dcb.reshape(-1),
        a_ln_g=dlng.reshape(-1), a_ln_b=dlnb.reshape(-1), a_pw_w=dpw, a_pw_b=dpwb.reshape(-1),
        s5_lambda_re=dlre.reshape(S5_G, S5_P), s5_lambda_im=dlim.reshape(S5_G, S5_P),
        s5_b_re=unrows(dbre, (1, 2, 0)), s5_b_im=unrows(dbim, (1, 2, 0)),
        s5_c_re=unrows(dcre, (1, 0, 2)), s5_c_im=unrows(dcim, (1, 0, 2)),
        s5_d=ddsk.reshape(-1), s5_log_dt=dldt[0, :S5_G], s5_glu_w=dgw, s5_glu_b=dgb.reshape(-1),
        c_conv_w=dcw_c[:K_C], d_conv_w=dcw_d[:K_DN], d_a_log=dp1[0, :DN_H], d_dt_bias=dp2[0, :DN_H],
        d_norm_g=dng.reshape(-1), w_out=dwo)
    return dx, grads, got


def _layer_params(full, wp, wo, l):
    return dict(
        norm_g=_row(full["norm_g"][l]), wp=wp,
        a_cw=_pad_rows(full["a_conv_w"][l], HALO_A), a_cb=_row(full["a_conv_b"][l]),
        a_lng=_row(full["a_ln_g"][l]), a_lnb=_row(full["a_ln_b"][l]), a_pw=full["a_pw_w"][l],
        a_pwb=_row(full["a_pw_b"][l]),
        s5=(_row(full["s5_lambda_re"][l]), _row(full["s5_lambda_im"][l]), _row(full["s5_log_dt"][l], 128),
            jnp.transpose(full["s5_b_re"][l], (2, 0, 1)).reshape(S5_H, NS),
            jnp.transpose(full["s5_b_im"][l], (2, 0, 1)).reshape(S5_H, NS),
            jnp.transpose(full["s5_c_re"][l], (1, 0, 2)).reshape(S5_H, NS),
            jnp.transpose(full["s5_c_im"][l], (1, 0, 2)).reshape(S5_H, NS)),
        s5_d=_row(full["s5_d"][l]), glu_w=full["s5_glu_w"][l], glu_b=_row(full["s5_glu_b"][l]),
        c_cw=_pad_rows(full["c_conv_w"][l], HALO_S), d_cw=_pad_rows(full["d_conv_w"][l], HALO_S),
        d_p1=_row(full["d_a_log"][l], 128), d_p2=_row(full["d_dt_bias"][l], 128),
        d_ng=_row(full["d_norm_g"][l]), wo=wo)


def kernel(x, norm_g, w_in, a_conv_w, a_conv_b, a_ln_g, a_ln_b, a_pw_w, a_pw_b, s5_lambda_re, s5_lambda_im, s5_b_re, s5_b_im, s5_c_re, s5_c_im, s5_d, s5_log_dt, s5_glu_w, s5_glu_b, c_conv_w, d_conv_w, d_a_log, d_dt_bias, d_norm_g, w_out, final_g, loss_target, m_norm_g, m_w_in, m_a_conv_w, m_a_conv_b, m_a_ln_g, m_a_ln_b, m_a_pw_w, m_a_pw_b, m_s5_lambda_re, m_s5_lambda_im, m_s5_b_re, m_s5_b_im, m_s5_c_re, m_s5_c_im, m_s5_d, m_s5_log_dt, m_s5_glu_w, m_s5_glu_b, m_c_conv_w, m_d_conv_w, m_d_a_log, m_d_dt_bias, m_d_norm_g, m_w_out, m_final_g, v_norm_g, v_w_in, v_a_conv_w, v_a_conv_b, v_a_ln_g, v_a_ln_b, v_a_pw_w, v_a_pw_b, v_s5_lambda_re, v_s5_lambda_im, v_s5_b_re, v_s5_b_im, v_s5_c_re, v_s5_c_im, v_s5_d, v_s5_log_dt, v_s5_glu_w, v_s5_glu_b, v_c_conv_w, v_d_conv_w, v_d_a_log, v_d_dt_bias, v_d_norm_g, v_w_out, v_final_g):
    given = dict(locals())
    w = {n: given[n] for n in WEIGHTS}
    m = {n: given["m_" + n] for n in WEIGHTS}
    v = {n: given["v_" + n] for n in WEIGHTS}
    xs, tgt = x[0], loss_target[0]

    n_in, n_out = w["w_in"].shape[2], w["w_out"].shape[1]
    sm_names = [n for n, _ in SHARDED_SMALL]
    sm_shapes = [w[n].shape for n in sm_names]
    sm_rows = _rows_for(sm_shapes, 16)
    win_b, wout_b = w["w_in"].astype(BF), w["w_out"].astype(BF)
    g_in, g_out, g_sm = _exchange([win_b[0], wout_b[0], _pack([w[n] for n in sm_names], sm_rows)],
                                  ("x", "y"), "gather", "gather_first")
    full = dict(w)
    parts = [_unpack(g_sm[j], sm_shapes) for j in range(4)]
    for i, (n, ax) in enumerate(SHARDED_SMALL):
        full[n] = jnp.concatenate([parts[j][i] for j in range(4)], axis=ax)

    saved = []
    h = xs
    for l in range(DEPTH):
        p = _layer_params(full, _assemble_in([g_in[j] for j in range(4)]),
                          jnp.concatenate([g_out[j] for j in range(4)], axis=0), l)
        nxt = {}
        if l + 1 < DEPTH:
            nxt = {"d": ([win_b[l + 1]], ("x", "y"), "gather"), "b": ([wout_b[l + 1]], ("x", "y"), "gather")}
        h, r, got = _layer_fwd(h, p, nxt)
        saved.append((p, r))
        if l + 1 < DEPTH:
            (g_in,), (g_out,) = got["d"], got["b"]
    loss_tile, dx, dfg = _loss_bwd(h, _row(full["final_g"]), tgt)

    def big_slots(dwin, dwo):
        s_in = jnp.stack([dwin[:, j * n_in:(j + 1) * n_in].astype(BF) for j in range(4)])
        return [s_in.reshape(8, D_MODEL // 2, n_in), dwo.astype(BF).reshape(8, n_out // 2, D_MODEL)]

    def halves(rv):
        return [_sum_slots(rv[0], "sum_w_in")[None], _sum_slots(rv[1], "sum_w_out")[None]]

    layer_grads, summed, pending, arrived = [None] * DEPTH, [None] * DEPTH, None, {}
    for l in reversed(range(DEPTH)):
        p, r = saved[l]
        ex = {}
        if pending is not None:
            ex["d"] = (pending, AXES, "scatter")
        if l + 2 in arrived:
            ex["b"] = (halves(arrived.pop(l + 2)), ("c",), "rows")
        if l == 0:
            ex["proj"] = lambda came, dwin, dwo: [(halves(came), ("c",), "rows"),
                                                  (big_slots(dwin, dwo), AXES, "scatter")]
        dx, layer_grads[l], got = _layer_bwd(dx, p, r, ex)
        if got["b"] is not None:
            summed[l + 2] = got["b"]
        if got["proj"] is not None:
            summed[1], arrived[0] = got["proj"]
        elif got["d"] is not None:
            arrived[l + 1] = got["d"]
        pending = big_slots(layer_grads[l]["w_in"], layer_grads[l]["w_out"]) if l else None
    grads = {n: jnp.stack([layer_grads[l][n] for l in range(DEPTH)]) for n in WEIGHTS
             if n not in ("final_g", "w_in", "w_out")}
    grads["final_g"] = dfg.reshape(-1)
    slots = []
    for j in range(4):
        sl = [lax.slice_in_dim(grads[n], j * w[n].shape[ax], (j + 1) * w[n].shape[ax], axis=ax)
              for n, ax in SHARDED_SMALL]
        slots.append(_pack(sl, sm_rows))
    rp_shapes = [w[n].shape for n in REPLICATED] + [(1,)]
    rp_rows = _rows_for(rp_shapes, 64)
    r_sm, r_rp = _exchange(
        [jnp.stack(slots).reshape(8, sm_rows // 2, LANES),
         _pack([grads[n] for n in REPLICATED] + [loss_tile[0, 0:1]], rp_rows).reshape(8, rp_rows // 8, LANES)],
        AXES, "scatter", "scatter_last")
    summed[0] = _exchange(halves(arrived.pop(0)) + [_sum_slots(r_sm, "sum_small")[None]], ("c",), "rows",
                          "gather_halves")
    h_sm = summed[0][2]
    h_in = jnp.concatenate([summed[l][0] for l in range(DEPTH)], axis=0)
    h_out = jnp.concatenate([summed[l][1] for l in range(DEPTH)], axis=0)
    (g_rp,) = _exchange([_sum_slots(r_rp, "sum_replicated")], AXES, "gather", "gather_replicated")
    g_rp = g_rp.reshape(rp_rows, LANES)
    g_sm = h_sm.reshape(sm_rows, LANES)

    out = {}

    def put(name, shape, res):
        for key, t in zip(("delta", "new_m", "new_v"), res):
            out[key + "_" + name] = t.reshape(shape)

    g2 = h_out.reshape(DEPTH * n_out, D_MODEL)
    out["grad_w_out"] = g2.reshape(w["w_out"].shape)
    put("w_out", w["w_out"].shape, _adamw(w["w_out"].reshape(g2.shape), g2, m["w_out"].reshape(g2.shape),
                                          v["w_out"].reshape(g2.shape), "adamw_w_out"))
    cm = lambda a: jnp.transpose(a, (2, 0, 1))
    rm = lambda a: jnp.transpose(a, (1, 2, 0))
    g3 = cm(h_in)
    out["grad_w_in"] = rm(g3)
    for key, t in zip(("delta", "new_m", "new_v"),
                      _adamw(cm(w["w_in"]), g3, cm(m["w_in"]), cm(v["w_in"]), "adamw_w_in", lead=n_in // 6)):
        out[key + "_w_in"] = rm(t)
    zero = jnp.zeros((1,), F32)
    res_sm = _adamw(_pack([w[n] for n in sm_names], sm_rows), g_sm, _pack([m[n] for n in sm_names], sm_rows),
                    _pack([v[n] for n in sm_names], sm_rows), "adamw_small")
    res_rp = _adamw(_pack([w[n] for n in REPLICATED] + [zero], rp_rows), g_rp,
                    _pack([m[n] for n in REPLICATED] + [zero], rp_rows),
                    _pack([v[n] for n in REPLICATED] + [zero], rp_rows), "adamw_replicated")
    for key, sm, rp in (("grad", g_sm, g_rp), ("delta", res_sm[0], res_rp[0]), ("new_m", res_sm[1], res_rp[1]),
                        ("new_v", res_sm[2], res_rp[2])):
        for n, t in zip(sm_names, _unpack(sm, sm_shapes)):
            out[key + "_" + n] = t
        for n, t in zip(REPLICATED, _unpack(rp, rp_shapes[:-1])):
            out[key + "_" + n] = t
    loss = _unpack(g_rp, rp_shapes)[-1].reshape(())
    return (loss, dx[None], *[out["grad_" + n] for n in WEIGHTS], *[out["delta_" + n] for n in WEIGHTS],
            *[out["new_m_" + n] for n in WEIGHTS], *[out["new_v_" + n] for n in WEIGHTS])
```

```python
import functools

import jax
import jax.numpy as jnp
from jax import lax
from jax.experimental import pallas as pl
from jax.experimental.pallas import tpu as pltpu

F32, BF = jnp.float32, jnp.bfloat16
HI = lax.Precision.HIGHEST
MESH = pl.DeviceIdType.MESH

D_MODEL = 1024
BR = 256
DEPTH = 4
N_IN = 3336
N_INP = 3456
COL_A, COL_B, COL_C, COL_D = 0, 768, 1280, 2304
W_A, W_B, W_C, W_D = 768, 512, 1024, 1152
S5_G, S5_H, S5_P = 16, 16, 64
NS = S5_G * S5_P
DN_H, DN_D, DN_C = 4, 64, 64
DN_G_FWD, DN_G_BWD = 8, 4
DN_P = DN_H // 2
DN_STATE = (DN_P, 2 * DN_D, 2 * DN_D)
DN_INV = (DN_P, DN_C, 2 * DN_D)
K_A, K_C, K_DN = 31, 3, 4
HALO_A, HALO_S = 32, 8
EPS = 1e-6
TL = 256
VMEM_LIMIT = 56 * 1024 * 1024

ADAM_LR, ADAM_B1, ADAM_B2, ADAM_EPS, ADAM_WD, ADAM_STEP = 0.001, 0.9, 0.999, 1e-08, 0.01, 10


def _cp(*sem):
    return pltpu.CompilerParams(dimension_semantics=sem, vmem_limit_bytes=VMEM_LIMIT)


def _sigmoid(x):
    return jax.nn.sigmoid(x)


def _silu(x):
    return x * jax.nn.sigmoid(x)


def _rmsnorm(x, g):
    return x * lax.rsqrt(jnp.mean(x * x, axis=-1, keepdims=True) + EPS) * g


@jax.custom_vjp
def _mm(a, w):
    return jnp.dot(a.astype(BF), w.astype(BF), preferred_element_type=F32)


def _mm_f(a, w):
    return _mm(a, w), (a, w)


def _mm_b(res, g):
    a, w = res
    gb = g.astype(BF)
    da = lax.dot_general(gb, w.astype(BF), (((1,), (1,)), ((), ())), preferred_element_type=F32)
    dw = lax.dot_general(a.astype(BF), gb, (((0,), (0,)), ((), ())), preferred_element_type=F32)
    return da, dw


_mm.defvjp(_mm_f, _mm_b)


@jax.custom_vjp
def _mm_nt(a, b):
    return lax.dot_general(a.astype(BF), b.astype(BF), (((1,), (1,)), ((), ())), preferred_element_type=F32)


def _mm_nt_f(a, b):
    return _mm_nt(a, b), (a, b)


def _mm_nt_b(res, g):
    a, b = res
    gb = g.astype(BF)
    da = jnp.dot(gb, b.astype(BF), preferred_element_type=F32)
    db = lax.dot_general(gb, a.astype(BF), (((0,), (0,)), ((), ())), preferred_element_type=F32)
    return da, db


_mm_nt.defvjp(_mm_nt_f, _mm_nt_b)


@jax.custom_vjp
def _mm_tn(a, b):
    return lax.dot_general(a.astype(BF), b.astype(BF), (((0,), (0,)), ((), ())), preferred_element_type=F32)


def _mm_tn_f(a, b):
    return _mm_tn(a, b), (a, b)


def _mm_tn_b(res, g):
    a, b = res
    gb = g.astype(BF)
    da = lax.dot_general(b.astype(BF), gb, (((1,), (1,)), ((), ())), preferred_element_type=F32)
    db = jnp.dot(a.astype(BF), gb, preferred_element_type=F32)
    return da, db


_mm_tn.defvjp(_mm_tn_f, _mm_tn_b)


def _dot_hi(a, b):
    return jnp.dot(a, b, precision=HI, preferred_element_type=F32)


def _split(a):
    hi = a.astype(BF)
    return hi, (a - hi.astype(F32)).astype(BF)


def _dot3(a, b, dims=(((1,), (0,)), ((), ()))):
    ah, al = _split(a)
    bh, bl = _split(b)
    d = functools.partial(lax.dot_general, dimension_numbers=dims, preferred_element_type=F32)
    return d(ah, bh) + d(ah, bl) + d(al, bh)


@jax.custom_vjp
def _mm3(a, b):
    return _dot3(a, b)


def _mm3_f(a, b):
    return _dot3(a, b), (a, b)


def _mm3_b(res, g):
    a, b = res
    return _dot3(g, b, (((1,), (1,)), ((), ()))), _dot3(a, g, (((0,), (0,)), ((), ())))


_mm3.defvjp(_mm3_f, _mm3_b)


def _roll(x, s):
    n = x.shape[0]
    s = s % n
    return x if s == 0 else pltpu.roll(x, s, 0)


def _conv_taps(ext, w_ref, halo, k_taps, tl):
    acc = None
    for k in range(k_taps):
        term = _roll(ext, (k_taps - 1) - k)[halo:halo + tl] * w_ref[k:k + 1, :]
        acc = term if acc is None else acc + term
    return acc


def _conv_taps_bwd(ext, w_ref, dw_ref, dacc, halo, k_taps, tl):
    dpad = jnp.concatenate([dacc, jnp.zeros((halo, dacc.shape[1]), F32)], axis=0)
    dext = None
    for k in range(k_taps):
        r = _roll(ext, (k_taps - 1) - k)[halo:halo + tl]
        dw_ref[k:k + 1, :] += jnp.sum(r * dacc, axis=0, keepdims=True)
        term = _roll(dpad, halo - (k_taps - 1) + k) * w_ref[k:k + 1, :]
        dext = term if dext is None else dext + term
    return dext


def _add_tail(x, tail):
    tl, h = x.shape[0], tail.shape[0]
    return x + jnp.concatenate([jnp.zeros((tl - h, x.shape[1]), F32), tail], axis=0)


def _proj_fwd(x, g, wp):
    L = x.shape[0]

    def body(x_ref, g_ref, w_ref, p_ref, h_ref):
        hb = _rmsnorm(x_ref[...], g_ref[...]).astype(BF)
        h_ref[...] = hb
        p_ref[...] = jnp.dot(hb, w_ref[...], preferred_element_type=F32)

    return pl.pallas_call(
        body, grid=(L // TL,),
        in_specs=[pl.BlockSpec((TL, D_MODEL), lambda i: (i, 0)),
                  pl.BlockSpec((1, D_MODEL), lambda i: (0, 0)),
                  pl.BlockSpec((D_MODEL, N_INP), lambda i: (0, 0))],
        out_specs=[pl.BlockSpec((TL, N_INP), lambda i: (i, 0)),
                   pl.BlockSpec((TL, D_MODEL), lambda i: (i, 0))],
        out_shape=[jax.ShapeDtypeStruct((L, N_INP), F32), jax.ShapeDtypeStruct((L, D_MODEL), BF)],
        name="proj_fwd", compiler_params=_cp("parallel"))(x, g, wp)


def _proj_bwd_x(x, g, dpa, dpb, dpc, dpd, wp, dxo, exchange=None):
    L = x.shape[0]

    def body(x_ref, g_ref, a_ref, b_ref, c_ref, d_ref, w_ref, dxo_ref, dx_ref, dg_ref):
        dh = None
        for ref, c0, wd in ((a_ref, COL_A, W_A), (b_ref, COL_B, W_B), (c_ref, COL_C, W_C), (d_ref, COL_D, W_D)):
            t = lax.dot_general(ref[...].astype(BF), w_ref[:, c0:c0 + wd], (((1,), (1,)), ((), ())),
                                preferred_element_type=F32)
            dh = t if dh is None else dh + t
        _, vj = jax.vjp(_rmsnorm, x_ref[...], g_ref[...])
        dx, dg = vj(dh)
        dx_ref[...] = dxo_ref[...] + dx

        @pl.when(pl.program_id(0) == 0)
        def _():
            dg_ref[...] = jnp.zeros_like(dg_ref)

        dg_ref[...] += dg

    def rows(wd):
        return pl.BlockSpec((TL, wd), lambda i: (i, 0))

    return _sweep_with_exchange(
        body, L // TL,
        in_specs=[rows(D_MODEL), pl.BlockSpec((1, D_MODEL), lambda i: (0, 0)),
                  rows(W_A), rows(W_B), rows(W_C), rows(W_D),
                  pl.BlockSpec((D_MODEL, N_INP), lambda i: (0, 0)), rows(D_MODEL)],
        out_specs=[rows(D_MODEL), pl.BlockSpec((1, D_MODEL), lambda i: (0, 0))],
        out_shape=[jax.ShapeDtypeStruct((L, D_MODEL), F32), jax.ShapeDtypeStruct((1, D_MODEL), F32)],
        scratch_shapes=[], args=(x, g, dpa, dpb, dpc, dpd, wp, dxo), exchange=exchange, name="proj_bwd_x")


def _dwin(h, dpa, dpb, dpc, dpd):
    L = h.shape[0]

    def body(h_ref, a_ref, b_ref, c_ref, d_ref, oa_ref, ob_ref, oc_ref, od_ref):
        outs = (oa_ref, ob_ref, oc_ref, od_ref)

        @pl.when(pl.program_id(0) == 0)
        def _():
            for o in outs:
                o[...] = jnp.zeros_like(o)

        ht = h_ref[...].T
        for ref, o in zip((a_ref, b_ref, c_ref, d_ref), outs):
            o[...] += jnp.dot(ht, ref[...].astype(BF), preferred_element_type=F32)

    def rows(wd):
        return pl.BlockSpec((TL, wd), lambda i: (i, 0))

    def whole(wd):
        return pl.BlockSpec((D_MODEL, wd), lambda i: (0, 0))

    widths = (W_A, W_B, W_C, W_D)
    return pl.pallas_call(
        body, grid=(L // TL,),
        in_specs=[rows(D_MODEL)] + [rows(wd) for wd in widths],
        out_specs=[whole(wd) for wd in widths],
        out_shape=[jax.ShapeDtypeStruct((D_MODEL, wd), F32) for wd in widths],
        name="dwin", compiler_params=_cp("arbitrary"))(h, dpa, dpb, dpc, dpd)


def _dwout(ya, yb, yc, yd, dxo):
    L = dxo.shape[0]
    tk, tn = min(512, L), 512

    def body(a_ref, b_ref, c_ref, d_ref, g_ref, o_ref):
        @pl.when(pl.program_id(1) == 0)
        def _():
            o_ref[...] = jnp.zeros_like(o_ref)

        gb = g_ref[...].astype(BF)
        for j, ref in enumerate((a_ref, b_ref, c_ref, d_ref)):
            o_ref[j * BR:(j + 1) * BR, :] += lax.dot_general(ref[...].astype(BF), gb, (((0,), (0,)), ((), ())),
                                                             preferred_element_type=F32)

    ys = pl.BlockSpec((tk, BR), lambda j, t: (t, 0))
    return pl.pallas_call(
        body, grid=(D_MODEL // tn, L // tk),
        in_specs=[ys, ys, ys, ys, pl.BlockSpec((tk, tn), lambda j, t: (t, j))],
        out_specs=pl.BlockSpec((D_MODEL, tn), lambda j, t: (0, j)),
        out_shape=jax.ShapeDtypeStruct((D_MODEL, D_MODEL), F32),
        name="dwout", compiler_params=_cp("parallel", "arbitrary"))(ya, yb, yc, yd, dxo)


def _out_fwd(x, ya, yb, yc, yd, wo):
    L = x.shape[0]

    def body(x_ref, a_ref, b_ref, c_ref, d_ref, w_ref, o_ref):
        acc = x_ref[...]
        for j, ref in enumerate((a_ref, b_ref, c_ref, d_ref)):
            acc = acc + jnp.dot(ref[...].astype(BF), w_ref[j * BR:(j + 1) * BR, :], preferred_element_type=F32)
        o_ref[...] = acc

    def rows(wd):
        return pl.BlockSpec((TL, wd), lambda i: (i, 0))

    return pl.pallas_call(
        body, grid=(L // TL,),
        in_specs=[rows(D_MODEL), rows(BR), rows(BR), rows(BR), rows(BR),
                  pl.BlockSpec((D_MODEL, D_MODEL), lambda i: (0, 0))],
        out_specs=rows(D_MODEL), out_shape=jax.ShapeDtypeStruct((L, D_MODEL), F32),
        name="out_fwd", compiler_params=_cp("parallel"))(x, ya, yb, yc, yd, wo)


def _out_bwd_x(dxo, wo):
    L = dxo.shape[0]

    def body(d_ref, w_ref, o_ref):
        o_ref[...] = lax.dot_general(d_ref[...].astype(BF), w_ref[...], (((1,), (1,)), ((), ())),
                                     preferred_element_type=F32)

    return pl.pallas_call(
        body, grid=(L // TL,),
        in_specs=[pl.BlockSpec((TL, D_MODEL), lambda i: (i, 0)), pl.BlockSpec((D_MODEL, D_MODEL), lambda i: (0, 0))],
        out_specs=pl.BlockSpec((TL, D_MODEL), lambda i: (i, 0)),
        out_shape=jax.ShapeDtypeStruct((L, D_MODEL), F32),
        name="out_bwd_x", compiler_params=_cp("parallel"))(dxo, wo)


def _loss_bwd(x, g, tgt):
    L = x.shape[0]

    def f(xv, gv, tv):
        err = _rmsnorm(xv, gv) - tv
        return 0.5 * jnp.sum(jnp.mean(err * err, axis=-1, keepdims=True), axis=0, keepdims=True)

    def body(x_ref, g_ref, t_ref, loss_ref, dx_ref, dg_ref):
        tv = t_ref[...]
        loss, vj = jax.vjp(lambda a, b: f(a, b, tv), x_ref[...], g_ref[...])
        dx, dg = vj(jnp.ones((1, 1), F32))
        dx_ref[...] = dx

        @pl.when(pl.program_id(0) == 0)
        def _():
            dg_ref[...] = jnp.zeros_like(dg_ref)
            loss_ref[...] = jnp.zeros_like(loss_ref)

        dg_ref[...] += dg
        loss_ref[...] += jnp.broadcast_to(loss, loss_ref.shape)

    return pl.pallas_call(
        body, grid=(L // TL,),
        in_specs=[pl.BlockSpec((TL, D_MODEL), lambda i: (i, 0)), pl.BlockSpec((1, D_MODEL), lambda i: (0, 0)),
                  pl.BlockSpec((TL, D_MODEL), lambda i: (i, 0))],
        out_specs=[pl.BlockSpec((8, 128), lambda i: (0, 0)), pl.BlockSpec((TL, D_MODEL), lambda i: (i, 0)),
                   pl.BlockSpec((1, D_MODEL), lambda i: (0, 0))],
        out_shape=[jax.ShapeDtypeStruct((8, 128), F32), jax.ShapeDtypeStruct((L, D_MODEL), F32),
                   jax.ShapeDtypeStruct((1, D_MODEL), F32)],
        name="loss_bwd", compiler_params=_cp("arbitrary"))(x, g, tgt)


def _a_pre(val, gate):
    return val * _sigmoid(gate)


def _a_post(acc, az, cb, lng, lnb, pw, pwb):
    t = acc + cb
    mu = jnp.mean(t, axis=-1, keepdims=True)
    xc = t - mu
    ln = xc * lax.rsqrt(jnp.mean(xc * xc, axis=-1, keepdims=True) + EPS) * lng + lnb
    return (_mm(_silu(ln), pw) + pwb) * _silu(az)


def _halo_map(tl, halo, col):
    r = tl // halo
    return lambda i: (jnp.maximum(i * r - 1, 0), col)


def _a_fwd(proj, cw, cb, lng, lnb, pw, pwb):
    L = proj.shape[0]

    def body(vg_ref, az_ref, hvg_ref, cw_ref, cb_ref, lng_ref, lnb_ref, pw_ref, pwb_ref, o_ref, acc_ref):
        keep = (pl.program_id(0) > 0).astype(F32)
        a_h = _a_pre(hvg_ref[:, 0:BR], hvg_ref[:, BR:2 * BR]) * keep
        a_t = _a_pre(vg_ref[:, 0:BR], vg_ref[:, BR:2 * BR])
        ext = jnp.concatenate([a_h, a_t], axis=0)
        acc = _conv_taps(ext, cw_ref, HALO_A, K_A, TL)
        acc_ref[...] = acc
        o_ref[...] = _a_post(acc, az_ref[...], cb_ref[...], lng_ref[...], lnb_ref[...], pw_ref[...], pwb_ref[...])

    vec = pl.BlockSpec((1, BR), lambda i: (0, 0))
    tile = pl.BlockSpec((TL, BR), lambda i: (i, 0))
    return pl.pallas_call(
        body, grid=(L // TL,),
        in_specs=[pl.BlockSpec((TL, 2 * BR), lambda i: (i, 0)), pl.BlockSpec((TL, BR), lambda i: (i, 2)),
                  pl.BlockSpec((HALO_A, 2 * BR), _halo_map(TL, HALO_A, 0)),
                  pl.BlockSpec((HALO_A, BR), lambda i: (0, 0)), vec, vec, vec,
                  pl.BlockSpec((BR, BR), lambda i: (0, 0)), vec],
        out_specs=[tile, tile],
        out_shape=[jax.ShapeDtypeStruct((L, BR), F32)] * 2,
        name="a_fwd", compiler_params=_cp("parallel"))(proj, proj, proj, cw, cb, lng, lnb, pw, pwb)


def _a_bwd(proj, dmix, conv_out, cw, cb, lng, lnb, pw, pwb):
    L = proj.shape[0]
    n = L // TL

    def body(vg_ref, az_ref, hvg_ref, dy_ref, acc_ref, cw_ref, cb_ref, lng_ref, lnb_ref, pw_ref, pwb_ref,
             dp_ref, dcw_ref, dcb_ref, dlng_ref, dlnb_ref, dpw_ref, dpwb_ref, carry_ref):
        i = pl.program_id(0)

        @pl.when(i == 0)
        def _():
            carry_ref[...] = jnp.zeros_like(carry_ref)
            for r in (dcw_ref, dcb_ref, dlng_ref, dlnb_ref, dpw_ref, dpwb_ref):
                r[...] = jnp.zeros_like(r)

        keep = (i < n - 1).astype(F32)
        val, gate = vg_ref[:, 0:BR], vg_ref[:, BR:2 * BR]
        a_h = _a_pre(hvg_ref[:, 0:BR], hvg_ref[:, BR:2 * BR]) * keep
        a_t, vj_pre = jax.vjp(_a_pre, val, gate)
        ext = jnp.concatenate([a_h, a_t], axis=0)
        _, vj_post = jax.vjp(_a_post, acc_ref[...], az_ref[...], cb_ref[...], lng_ref[...], lnb_ref[...],
                             pw_ref[...], pwb_ref[...])
        dacc, daz, dcb, dlng, dlnb, dpw, dpwb = vj_post(dy_ref[...])
        dext = _conv_taps_bwd(ext, cw_ref, dcw_ref, dacc, HALO_A, K_A, TL)
        da = _add_tail(dext[HALO_A:], carry_ref[...])
        carry_ref[...] = dext[:HALO_A]
        dval, dgate = vj_pre(da)
        dp_ref[:, 0:BR] = dval
        dp_ref[:, BR:2 * BR] = dgate
        dp_ref[:, 2 * BR:3 * BR] = daz
        dcb_ref[...] += dcb
        dlng_ref[...] += dlng
        dlnb_ref[...] += dlnb
        dpw_ref[...] += dpw
        dpwb_ref[...] += dpwb

    rev = lambda i: n - 1 - i
    vec = pl.BlockSpec((1, BR), lambda i: (0, 0))
    hmap = _halo_map(TL, HALO_A, 0)
    return pl.pallas_call(
        body, grid=(n,),
        in_specs=[pl.BlockSpec((TL, 2 * BR), lambda i: (rev(i), 0)), pl.BlockSpec((TL, BR), lambda i: (rev(i), 2)),
                  pl.BlockSpec((HALO_A, 2 * BR), lambda i: hmap(rev(i))),
                  pl.BlockSpec((TL, BR), lambda i: (rev(i), 0)), pl.BlockSpec((TL, BR), lambda i: (rev(i), 0)),
                  pl.BlockSpec((HALO_A, BR), lambda i: (0, 0)), vec, vec, vec,
                  pl.BlockSpec((BR, BR), lambda i: (0, 0)), vec],
        out_specs=[pl.BlockSpec((TL, W_A), lambda i: (rev(i), 0)),
                   pl.BlockSpec((HALO_A, BR), lambda i: (0, 0)), vec, vec, vec,
                   pl.BlockSpec((BR, BR), lambda i: (0, 0)), vec],
        out_shape=[jax.ShapeDtypeStruct((L, W_A), F32), jax.ShapeDtypeStruct((HALO_A, BR), F32)]
        + [jax.ShapeDtypeStruct((1, BR), F32)] * 3
        + [jax.ShapeDtypeStruct((BR, BR), F32), jax.ShapeDtypeStruct((1, BR), F32)],
        scratch_shapes=[pltpu.VMEM((HALO_A, BR), F32)],
        name="a_bwd", compiler_params=_cp("arbitrary"))(proj, proj, proj, dmix, conv_out, cw, cb, lng, lnb, pw, pwb)


def _c_pre(cg, xc):
    return cg * xc


def _c_post(acc, bg, cz):
    return bg * acc * _silu(cz)


def _c_fwd(proj, cw):
    L = proj.shape[0]

    def body(bg_ref, cx_ref, cz_ref, hcx_ref, cw_ref, o_ref):
        keep = (pl.program_id(0) > 0).astype(F32)
        p_h = _c_pre(hcx_ref[:, 0:BR], hcx_ref[:, BR:2 * BR]) * keep
        p_t = _c_pre(cx_ref[:, 0:BR], cx_ref[:, BR:2 * BR])
        ext = jnp.concatenate([p_h, p_t], axis=0)
        acc = _conv_taps(ext, cw_ref, HALO_S, K_C, TL)
        o_ref[...] = _c_post(acc, bg_ref[...], cz_ref[...])

    return pl.pallas_call(
        body, grid=(L // TL,),
        in_specs=[pl.BlockSpec((TL, BR), lambda i: (i, 5)), pl.BlockSpec((TL, 2 * BR), lambda i: (i, 3)),
                  pl.BlockSpec((TL, BR), lambda i: (i, 8)),
                  pl.BlockSpec((HALO_S, 2 * BR), _halo_map(TL, HALO_S, 3)),
                  pl.BlockSpec((HALO_S, BR), lambda i: (0, 0))],
        out_specs=pl.BlockSpec((TL, BR), lambda i: (i, 0)),
        out_shape=jax.ShapeDtypeStruct((L, BR), F32),
        name="c_fwd", compiler_params=_cp("parallel"))(proj, proj, proj, proj, cw)


def _c_bwd(proj, dmix, cw):
    L = proj.shape[0]
    n = L // TL

    def body(bg_ref, cx_ref, cz_ref, hcx_ref, dy_ref, cw_ref, dp_ref, dcw_ref, carry_ref):
        i = pl.program_id(0)

        @pl.when(i == 0)
        def _():
            carry_ref[...] = jnp.zeros_like(carry_ref)
            dcw_ref[...] = jnp.zeros_like(dcw_ref)

        keep = (i < n - 1).astype(F32)
        p_h = _c_pre(hcx_ref[:, 0:BR], hcx_ref[:, BR:2 * BR]) * keep
        p_t, vj_pre = jax.vjp(_c_pre, cx_ref[:, 0:BR], cx_ref[:, BR:2 * BR])
        ext = jnp.concatenate([p_h, p_t], axis=0)
        acc = _conv_taps(ext, cw_ref, HALO_S, K_C, TL)
        _, vj_post = jax.vjp(_c_post, acc, bg_ref[...], cz_ref[...])
        dacc, dbg, dcz = vj_post(dy_ref[...])
        dext = _conv_taps_bwd(ext, cw_ref, dcw_ref, dacc, HALO_S, K_C, TL)
        dp = _add_tail(dext[HALO_S:], carry_ref[...])
        carry_ref[...] = dext[:HALO_S]
        dcg, dxc = vj_pre(dp)
        dp_ref[:, 0:BR] = dbg
        dp_ref[:, BR:2 * BR] = dcg
        dp_ref[:, 2 * BR:3 * BR] = dxc
        dp_ref[:, 3 * BR:4 * BR] = dcz

    rev = lambda i: n - 1 - i
    hmap = _halo_map(TL, HALO_S, 3)
    return pl.pallas_call(
        body, grid=(n,),
        in_specs=[pl.BlockSpec((TL, BR), lambda i: (rev(i), 5)), pl.BlockSpec((TL, 2 * BR), lambda i: (rev(i), 3)),
                  pl.BlockSpec((TL, BR), lambda i: (rev(i), 8)),
                  pl.BlockSpec((HALO_S, 2 * BR), lambda i: hmap(rev(i))),
                  pl.BlockSpec((TL, BR), lambda i: (rev(i), 2)),
                  pl.BlockSpec((HALO_S, BR), lambda i: (0, 0))],
        out_specs=[pl.BlockSpec((TL, W_C), lambda i: (rev(i), 0)), pl.BlockSpec((HALO_S, BR), lambda i: (0, 0))],
        out_shape=[jax.ShapeDtypeStruct((L, W_C), F32), jax.ShapeDtypeStruct((HALO_S, BR), F32)],
        scratch_shapes=[pltpu.VMEM((HALO_S, BR), F32)],
        name="c_bwd", compiler_params=_cp("arbitrary"))(proj, proj, proj, proj, dmix, cw)


def _s5_prep_fn(lre, lim, ldt, bre, bim, cre, cim):
    grp = lax.broadcasted_iota(jnp.int32, (128, NS), 0)
    lane = lax.broadcasted_iota(jnp.int32, (128, NS), 1)
    expand = (grp == lane // S5_P).astype(F32)
    dt = jnp.exp(_dot_hi(jnp.broadcast_to(ldt, (8, 128)), expand)[0:1])
    lr = jnp.minimum(lre, -1e-4)
    mag = jnp.exp(lr * dt)
    ar = mag * jnp.cos(lim * dt)
    ai = mag * jnp.sin(lim * dt)
    den = lr * lr + lim * lim
    fr = ((ar - 1.0) * lr + ai * lim) / den
    fi = (ai * lr - (ar - 1.0) * lim) / den
    bbr = fr * bre - fi * bim
    bbi = fr * bim + fi * bre
    row = lax.broadcasted_iota(jnp.int32, (BR, NS), 0)
    col = lax.broadcasted_iota(jnp.int32, (BR, NS), 1)
    blk = (row // S5_H == col // S5_P).astype(F32)

    def embed(t):
        return jnp.concatenate([t] * S5_G, axis=0) * blk

    bemb = jnp.concatenate([embed(bbr), embed(bbi)], axis=1)
    cemb = jnp.concatenate([embed(cre), embed(-cim)], axis=1)
    return ar, ai, bemb, cemb


def _s5_prep(lre, lim, ldt, bre, bim, cre, cim):
    def body(*refs):
        outs = _s5_prep_fn(*[r[...] for r in refs[:7]])
        for r, o in zip(refs[7:], outs):
            r[...] = o

    return pl.pallas_call(
        body,
        out_shape=[jax.ShapeDtypeStruct((1, NS), F32)] * 2 + [jax.ShapeDtypeStruct((BR, 2 * NS), F32)] * 2,
        name="s5_prep", compiler_params=pltpu.CompilerParams(vmem_limit_bytes=VMEM_LIMIT),
    )(lre, lim, ldt, bre, bim, cre, cim)


def _s5_prep_bwd(lre, lim, ldt, bre, bim, cre, cim, dar, dai, dbemb, dcemb):
    def body(*refs):
        _, vj = jax.vjp(_s5_prep_fn, *[r[...] for r in refs[:7]])
        grads = vj(tuple(r[...] for r in refs[7:11]))
        for r, o in zip(refs[11:], grads):
            r[...] = o

    return pl.pallas_call(
        body,
        out_shape=[jax.ShapeDtypeStruct((1, NS), F32)] * 2 + [jax.ShapeDtypeStruct((1, 128), F32)]
        + [jax.ShapeDtypeStruct((S5_H, NS), F32)] * 4,
        name="s5_prep_bwd", compiler_params=pltpu.CompilerParams(vmem_limit_bytes=VMEM_LIMIT),
    )(lre, lim, ldt, bre, bim, cre, cim, dar, dai, dbemb, dcemb)


def _s5_scan(xr, xi, ar, ai, reverse):
    n = xr.shape[0]
    row = lax.broadcasted_iota(jnp.int32, (n, 1), 0)
    pr, pi = ar, ai
    d = 1
    while d < n:
        if d % 8:
            if reverse:
                m = row < n - d
                sr = jnp.where(m, _roll(xr, n - d), 0.0)
                si = jnp.where(m, _roll(xi, n - d), 0.0)
            else:
                m = row >= d
                sr = jnp.where(m, _roll(xr, d), 0.0)
                si = jnp.where(m, _roll(xi, d), 0.0)
            xr, xi = xr + pr * sr - pi * si, xi + pr * si + pi * sr
        elif reverse:
            sr, si = xr[d:], xi[d:]
            xr, xi = (jnp.concatenate([xr[:n - d] + pr * sr - pi * si, xr[n - d:]], axis=0),
                      jnp.concatenate([xi[:n - d] + pr * si + pi * sr, xi[n - d:]], axis=0))
        else:
            sr, si = xr[:n - d], xi[:n - d]
            xr, xi = (jnp.concatenate([xr[:d], xr[d:] + pr * sr - pi * si], axis=0),
                      jnp.concatenate([xi[:d], xi[d:] + pr * si + pi * sr], axis=0))
        pr, pi = pr * pr - pi * pi, 2.0 * pr * pi
        d *= 2
    return xr, xi


def _s5_states(u, bemb_b, ar, ai, sin_r, sin_i):
    bu = jnp.dot(u.astype(BF), bemb_b, preferred_element_type=F32)
    first = lax.broadcasted_iota(jnp.int32, (u.shape[0], 1), 0) == 0
    xr = bu[:, :NS] + jnp.where(first, ar * sin_r - ai * sin_i, 0.0)
    xi = bu[:, NS:] + jnp.where(first, ar * sin_i + ai * sin_r, 0.0)
    return _s5_scan(xr, xi, ar, ai, False)


def _b_post(yssm, u, bz, dsk, gw, gb):
    z = jax.nn.gelu(yssm + dsk * u)
    return z * _sigmoid(_mm(z, gw) + gb) * _silu(bz)


def _b_fwd(proj, ar, ai, bemb, cemb, dsk, gw, gb, exchange=None):
    L = proj.shape[0]
    n = L // TL

    def body(u_ref, bz_ref, ar_ref, ai_ref, be_ref, ce_ref, dsk_ref, gw_ref, gb_ref, o_ref, sin_ref, st_ref,
             carry_ref):
        @pl.when(pl.program_id(0) == 0)
        def _():
            carry_ref[...] = jnp.zeros_like(carry_ref)

        sin = carry_ref[...]
        sin_ref[0] = sin
        u = u_ref[...]
        sr, si = _s5_states(u, be_ref[...].astype(BF), ar_ref[...], ai_ref[...], sin[:, :NS], sin[:, NS:])
        carry_ref[:, :NS] = sr[TL - 1:TL]
        carry_ref[:, NS:] = si[TL - 1:TL]
        st_ref[:, :NS] = sr
        st_ref[:, NS:] = si
        s = jnp.concatenate([sr, si], axis=1).astype(BF)
        yssm = lax.dot_general(s, ce_ref[...].astype(BF), (((1,), (1,)), ((), ())), preferred_element_type=F32)
        o_ref[...] = _b_post(yssm, u, bz_ref[...], dsk_ref[...], gw_ref[...], gb_ref[...])

    vec = pl.BlockSpec((1, BR), lambda i: (0, 0))
    svec = pl.BlockSpec((1, NS), lambda i: (0, 0))
    emb = pl.BlockSpec((BR, 2 * NS), lambda i: (0, 0))
    return _sweep_with_exchange(
        body, n,
        in_specs=[pl.BlockSpec((TL, BR), lambda i: (i, 3)), pl.BlockSpec((TL, BR), lambda i: (i, 4)),
                  svec, svec, emb, emb, vec, pl.BlockSpec((BR, BR), lambda i: (0, 0)), vec],
        out_specs=[pl.BlockSpec((TL, BR), lambda i: (i, 0)), pl.BlockSpec((1, 1, 2 * NS), lambda i: (i, 0, 0)),
                   pl.BlockSpec((TL, 2 * NS), lambda i: (i, 0))],
        out_shape=[jax.ShapeDtypeStruct((L, BR), F32), jax.ShapeDtypeStruct((n, 1, 2 * NS), F32),
                   jax.ShapeDtypeStruct((L, 2 * NS), F32)],
        scratch_shapes=[pltpu.VMEM((1, 2 * NS), F32)],
        args=(proj, proj, ar, ai, bemb, cemb, dsk, gw, gb), exchange=exchange, name="b_fwd")


def _b_bwd(proj, dmix, sin_all, states, ar, ai, bemb, cemb, dsk, gw, gb, exchange=None):
    L = proj.shape[0]
    n = L // TL

    def body(u_ref, bz_ref, dy_ref, sin_ref, st_ref, ar_ref, ai_ref, be_ref, ce_ref, dsk_ref, gw_ref, gb_ref,
             dp_ref, dar_ref, dai_ref, dbe_ref, dce_ref, ddsk_ref, dgw_ref, dgb_ref, carry_ref):
        i = pl.program_id(0)

        @pl.when(i == 0)
        def _():
            carry_ref[...] = jnp.zeros_like(carry_ref)
            for r in (dar_ref, dai_ref, dbe_ref, dce_ref, ddsk_ref, dgw_ref, dgb_ref):
                r[...] = jnp.zeros_like(r)

        u = u_ref[...]
        ar, ai = ar_ref[...], ai_ref[...]
        be_b, ce_b = be_ref[...].astype(BF), ce_ref[...].astype(BF)
        sin = sin_ref[0]
        sr, si = st_ref[:, :NS], st_ref[:, NS:]
        s_b = st_ref[...].astype(BF)
        yssm = lax.dot_general(s_b, ce_b, (((1,), (1,)), ((), ())), preferred_element_type=F32)
        _, vj = jax.vjp(_b_post, yssm, u, bz_ref[...], dsk_ref[...], gw_ref[...], gb_ref[...])
        dyssm, du, dbz, ddsk, dgw, dgb = vj(dy_ref[...])
        dy_b = dyssm.astype(BF)
        dce_ref[...] += lax.dot_general(dy_b, s_b, (((0,), (0,)), ((), ())), preferred_element_type=F32)
        gs = jnp.dot(dy_b, ce_b, preferred_element_type=F32)
        last = lax.broadcasted_iota(jnp.int32, (TL, 1), 0) == TL - 1
        cr, ci = carry_ref[:, :NS], carry_ref[:, NS:]
        gr = gs[:, :NS] + jnp.where(last, ar * cr + ai * ci, 0.0)
        gi = gs[:, NS:] + jnp.where(last, ar * ci - ai * cr, 0.0)
        dsr, dsi = _s5_scan(gr, gi, ar, -ai, True)
        carry_ref[:, :NS] = dsr[0:1]
        carry_ref[:, NS:] = dsi[0:1]
        first = lax.broadcasted_iota(jnp.int32, (TL, 1), 0) == 0
        pr = jnp.where(first, sin[:, :NS], _roll(sr, 1))
        pi = jnp.where(first, sin[:, NS:], _roll(si, 1))
        dar_ref[...] += jnp.sum(dsr * pr + dsi * pi, axis=0, keepdims=True)
        dai_ref[...] += jnp.sum(dsi * pr - dsr * pi, axis=0, keepdims=True)
        ds_b = jnp.concatenate([dsr, dsi], axis=1).astype(BF)
        dbe_ref[...] += lax.dot_general(u.astype(BF), ds_b, (((0,), (0,)), ((), ())), preferred_element_type=F32)
        du = du + lax.dot_general(ds_b, be_b, (((1,), (1,)), ((), ())), preferred_element_type=F32)
        dp_ref[:, 0:BR] = du
        dp_ref[:, BR:2 * BR] = dbz
        ddsk_ref[...] += ddsk
        dgw_ref[...] += dgw
        dgb_ref[...] += dgb

    rev = lambda i: n - 1 - i
    vec = pl.BlockSpec((1, BR), lambda i: (0, 0))
    svec = pl.BlockSpec((1, NS), lambda i: (0, 0))
    emb = pl.BlockSpec((BR, 2 * NS), lambda i: (0, 0))
    mat = pl.BlockSpec((BR, BR), lambda i: (0, 0))
    return _sweep_with_exchange(
        body, n,
        in_specs=[pl.BlockSpec((TL, BR), lambda i: (rev(i), 3)), pl.BlockSpec((TL, BR), lambda i: (rev(i), 4)),
                  pl.BlockSpec((TL, BR), lambda i: (rev(i), 1)),
                  pl.BlockSpec((1, 1, 2 * NS), lambda i: (rev(i), 0, 0)),
                  pl.BlockSpec((TL, 2 * NS), lambda i: (rev(i), 0)),
                  svec, svec, emb, emb, vec, mat, vec],
        out_specs=[pl.BlockSpec((TL, W_B), lambda i: (rev(i), 0)), svec, svec, emb, emb, vec, mat, vec],
        out_shape=[jax.ShapeDtypeStruct((L, W_B), F32)] + [jax.ShapeDtypeStruct((1, NS), F32)] * 2
        + [jax.ShapeDtypeStruct((BR, 2 * NS), F32)] * 2
        + [jax.ShapeDtypeStruct((1, BR), F32), jax.ShapeDtypeStruct((BR, BR), F32), jax.ShapeDtypeStruct((1, BR), F32)],
        scratch_shapes=[pltpu.VMEM((1, 2 * NS), F32)],
        args=(proj, proj, dmix, sin_all, states, ar, ai, bemb, cemb, dsk, gw, gb), exchange=exchange, name="b_bwd")


def _half_masks(rows):
    lane = lax.broadcasted_iota(jnp.int32, (rows, 2 * DN_D), 1)
    return lane < DN_D, lane >= DN_D


def _bd(x):
    left, right = _half_masks(x.shape[0])
    return jnp.concatenate([jnp.where(left, x, 0.0), jnp.where(right, x, 0.0)], axis=0)


@jax.custom_vjp
def _segsum(x):
    r = lax.broadcasted_iota(jnp.int32, (2 * DN_D, 2 * DN_D), 0) // DN_D
    c = lax.broadcasted_iota(jnp.int32, (2 * DN_D, 2 * DN_D), 1) // DN_D
    ones = (r == c).astype(BF)
    hi, lo = _split(x)
    return jnp.dot(hi, ones, preferred_element_type=F32) + jnp.dot(lo, ones, preferred_element_type=F32)


_segsum.defvjp(lambda x: (_segsum(x), None), lambda _, g: (_segsum(g),))


def _pair_t(x):
    t = _bd(x).T
    return t[:DN_D] + t[DN_D:]


@jax.custom_vjp
def _pair_inv(lms):
    n = DN_D
    row = lax.broadcasted_iota(jnp.int32, (n, 2 * n), 0)
    col = lax.broadcasted_iota(jnp.int32, (n, 2 * n), 1) % n
    eye = (row == col).astype(F32)
    accs = [eye - lm for lm in lms]
    pws = list(lms)
    k = 2
    while k < n:
        pws = [_dot3(p, _bd(p)) for p in pws]
        accs = [a + _dot3(a, _bd(p)) for a, p in zip(accs, pws)]
        k *= 2
    return tuple(accs)


def _pi_b(a, g):
    ats = [_pair_t(x) for x in a]
    tmp = [_dot3(at, _bd(gi)) for at, gi in zip(ats, g)]
    return (tuple(-_dot3(t, _bd(at)) for t, at in zip(tmp, ats)),)


def _pi_f(lms):
    a = _pair_inv(lms)
    return a, a


_pair_inv.defvjp(_pi_f, _pi_b)


@jax.custom_vjp
def _pair_known_inverse(lms, inv):
    return inv


_pair_known_inverse.defvjp(lambda lms, inv: (inv, inv),
                           lambda a, g: (_pi_b(a, g)[0], tuple(jnp.zeros_like(x) for x in a)))


def _d_tile(cq, ab, dz, sb0, sb1, p1, p2, ng, known=None):
    c = DN_C
    chunks = range(cq.shape[0] // c)
    units = [(g, p) for g in chunks for p in range(DN_P)]
    n = range(len(units))
    qkv = _silu(cq)
    gall = -jnp.exp(p1) * jax.nn.softplus(ab + p2)
    ball = _sigmoid(ab)
    left, _ = _half_masks(c)
    row = lax.broadcasted_iota(jnp.int32, (c, 2 * c), 0)
    col = lax.broadcasted_iota(jnp.int32, (c, 2 * c), 1) % c
    causal, strict = row >= col, row > col
    sq = lax.broadcasted_iota(jnp.int32, (c, c), 0) >= lax.broadcasted_iota(jnp.int32, (c, c), 1)
    gc_all = [_dot_hi(sq.astype(F32), gall[g * c:(g + 1) * c]) for g in chunks]
    gc_t = [t.T for t in gc_all]
    bdm = (lax.broadcasted_iota(jnp.int32, (2 * c, 2 * c), 0) // c
           == lax.broadcasted_iota(jnp.int32, (2 * c, 2 * c), 1) // c).astype(F32)

    def two(t, base, g, p):
        return t[g * c:(g + 1) * c, base + 2 * p * DN_D:base + 2 * (p + 1) * DN_D]

    def per_head(t, off, p):
        return jnp.where(left, t[:, off + 2 * p:off + 2 * p + 1], t[:, off + 2 * p + 1:off + 2 * p + 2])

    q = [two(qkv, 0, g, p) for g, p in units]
    k = [two(qkv, BR, g, p) for g, p in units]
    v = [two(qkv, 2 * BR, g, p) for g, p in units]
    q = [t * lax.rsqrt(_segsum(t * t) + EPS) * (DN_D ** -0.5) for t in q]
    k = [t * lax.rsqrt(_segsum(t * t) + EPS) for t in k]
    g2 = [per_head(gc_all[g], 0, p) for g, p in units]
    beta = [per_head(ball[g * c:(g + 1) * c], DN_H, p) for g, p in units]
    grow = [jnp.concatenate([gc_t[g][2 * p:2 * p + 1, :], gc_t[g][2 * p + 1:2 * p + 2, :]], axis=1) for g, p in units]
    decay = [jnp.where(causal, jnp.exp(jnp.where(causal, g2[u] - grow[u], 0.0)), 0.0) for u in n]
    kb = [k[u] * beta[u] for u in n]
    kbd = [_bd(t) for t in k]
    lm = [jnp.where(strict, _mm_nt(kb[u], kbd[u]) * decay[u], 0.0) for u in n]
    ainv = _pair_inv(tuple(lm)) if known is None else _pair_known_inverse(tuple(lm), known)
    egc = [jnp.exp(t) for t in g2]
    uw = [_mm3(ainv[u], jnp.concatenate([_bd(v[u] * beta[u]), _bd(kb[u] * egc[u])], axis=1)) for u in n]
    attn = [_mm_nt(q[u], kbd[u]) * decay[u] for u in n]
    glast = [t[c - 1:c, :] for t in g2]
    kd = [k[u] * jnp.exp(glast[u] - g2[u]) for u in n]
    qd = [q[u] * egc[u] for u in n]
    ng2 = jnp.concatenate([ng, ng], axis=1)
    sbd, starts, outs = [sb0, sb1], [], []
    for g in chunks:
        starts.append(tuple(sbd))
        us = [g * DN_P + p for p in range(DN_P)]
        vnew = [uw[u][:, :2 * DN_D] - _mm(uw[u][:, 2 * DN_D:], sbd[p]) for p, u in enumerate(us)]
        o = [_mm(qd[u], sbd[p]) + _mm(attn[u], _bd(vnew[p])) for p, u in enumerate(us)]
        sbd = [sbd[p] * jnp.exp(glast[u]) + _mm_tn(kd[u], vnew[p]) * bdm for p, u in enumerate(us)]
        outs.append(jnp.concatenate([t * lax.rsqrt(_segsum(t * t) * (1.0 / DN_D) + EPS) * ng2 for t in o], axis=1))
    yd = jnp.concatenate(outs, axis=0) * _silu(dz)
    return (yd, *sbd), (starts, ainv)


def _d_fwd(proj, cw, p1, p2, ng, exchange=None):
    L = proj.shape[0]
    DN_G, DN_T = DN_G_FWD, DN_G_FWD * DN_C
    n = L // DN_T

    def body(qkv_ref, ab_ref, dz_ref, hq_ref, cw_ref, p1_ref, p2_ref, ng_ref, o_ref, sall_ref, inv_ref, s_ref):
        i = pl.program_id(0)

        @pl.when(i == 0)
        def _():
            s_ref[...] = jnp.zeros_like(s_ref)

        keep = (i > 0).astype(F32)
        ext = jnp.concatenate([hq_ref[...] * keep, qkv_ref[...]], axis=0)
        cq = _conv_taps(ext, cw_ref, HALO_S, K_DN, DN_T)
        out, (starts, ainv) = _d_tile(cq, ab_ref[...], dz_ref[...], s_ref[0], s_ref[1], p1_ref[...], p2_ref[...],
                                      ng_ref[...])
        o_ref[...] = out[0]
        for p in range(DN_P):
            s_ref[p] = out[1 + p]
            for g in range(DN_G):
                sall_ref[g, p] = starts[g][p]
                inv_ref[g, p] = ainv[g * DN_P + p]

    return _sweep_with_exchange(
        body, n,
        in_specs=[pl.BlockSpec((DN_T, 3 * BR), lambda i: (i, 3)), pl.BlockSpec((DN_T, 128), lambda i: (i, 26)),
                  pl.BlockSpec((DN_T, BR), lambda i: (i, 12)),
                  pl.BlockSpec((HALO_S, 3 * BR), _halo_map(DN_T, HALO_S, 3)),
                  pl.BlockSpec((HALO_S, 3 * BR), lambda i: (0, 0)),
                  pl.BlockSpec((1, 128), lambda i: (0, 0)), pl.BlockSpec((1, 128), lambda i: (0, 0)),
                  pl.BlockSpec((1, DN_D), lambda i: (0, 0))],
        out_specs=[pl.BlockSpec((DN_T, BR), lambda i: (i, 0)),
                   pl.BlockSpec((DN_G,) + DN_STATE, lambda i: (i, 0, 0, 0)),
                   pl.BlockSpec((DN_G,) + DN_INV, lambda i: (i, 0, 0, 0))],
        out_shape=[jax.ShapeDtypeStruct((L, BR), F32), jax.ShapeDtypeStruct((L // DN_C,) + DN_STATE, F32),
                   jax.ShapeDtypeStruct((L // DN_C,) + DN_INV, F32)],
        scratch_shapes=[pltpu.VMEM(DN_STATE, F32)],
        args=(proj, proj, proj, proj, cw, p1, p2, ng), exchange=exchange, name="d_fwd")


def _sweep_with_exchange(body, steps, in_specs, out_specs, out_shape, scratch_shapes, args, exchange, name):
    if exchange is None:
        res = pl.pallas_call(body, grid=(steps,), in_specs=in_specs, out_specs=out_specs, out_shape=out_shape,
                             scratch_shapes=scratch_shapes, name=name, compiler_params=_cp("arbitrary"))(*args)
        return res, None
    several = isinstance(exchange, list)
    exs = [_Exchange(*e) for e in (exchange if several else [exchange])]
    xs = [x for e in (exchange if several else [exchange]) for x in e[0]]
    ni, no, ns, na = len(in_specs), len(out_specs), len(scratch_shapes), len(xs)

    def carried(*refs):
        ins, xin = refs[:ni], refs[ni:ni + na]
        outs, xout = refs[ni + na:ni + na + no], refs[ni + na + no:ni + 2 * na + no]
        scr, sems = refs[ni + 2 * na + no:ni + 2 * na + no + ns], refs[ni + 2 * na + no + ns:]

        def each(fn_name):
            off = 0
            for j, ex in enumerate(exs):
                getattr(ex, fn_name)(xin[off:off + ex.na], xout[off:off + ex.na], sems[3 * j:3 * j + 3])
                off += ex.na

        @pl.when(pl.program_id(0) == 0)
        def _():
            each("start")

        body(*ins, *outs, *scr)

        @pl.when(pl.program_id(0) == steps - 1)
        def _():
            each("wait")

    res = pl.pallas_call(carried, grid=(steps,), in_specs=list(in_specs) + [s for ex in exs for s in ex.in_specs],
                         out_specs=list(out_specs) + [s for ex in exs for s in ex.out_specs],
                         out_shape=list(out_shape) + [s for ex in exs for s in ex.out_shape],
                         scratch_shapes=list(scratch_shapes) + [s for ex in exs for s in ex.scratch_shapes],
                         name=name + "_x", compiler_params=_cp("arbitrary"))(*args, *xs)
    got, off = [], no
    for ex in exs:
        got.append(res[off:off + ex.na])
        off += ex.na
    return res[:no], (got if several else got[0])


def _d_bwd(proj, dmix, sall, inv, cw, p1, p2, ng, exchange=None):
    L = proj.shape[0]
    DN_G, DN_T = DN_G_BWD, DN_G_BWD * DN_C
    n = L // DN_T

    def body(qkv_ref, ab_ref, dz_ref, hq_ref, dy_ref, sall_ref, inv_ref, cw_ref, p1_ref, p2_ref, ng_ref,
             dp_ref, dcw_ref, dp1_ref, dp2_ref, dng_ref, ds_ref, carry_ref):
        i = pl.program_id(0)

        @pl.when(i == 0)
        def _():
            ds_ref[...] = jnp.zeros_like(ds_ref)
            carry_ref[...] = jnp.zeros_like(carry_ref)
            for r in (dcw_ref, dp1_ref, dp2_ref, dng_ref):
                r[...] = jnp.zeros_like(r)

        keep = (i < n - 1).astype(F32)
        ext = jnp.concatenate([hq_ref[...] * keep, qkv_ref[...]], axis=0)
        cq = _conv_taps(ext, cw_ref, HALO_S, K_DN, DN_T)
        known = tuple(inv_ref[g, p] for g in range(DN_G) for p in range(DN_P))
        _, vj = jax.vjp(lambda *a: _d_tile(*a, known=known)[0], cq, ab_ref[...], dz_ref[...], sall_ref[0, 0],
                        sall_ref[0, 1], p1_ref[...], p2_ref[...], ng_ref[...])
        dcq, dab, ddz, ds0, ds1, dp1, dp2, dng = vj((dy_ref[...], ds_ref[0], ds_ref[1]))
        dp_ref[:, 3 * BR:4 * BR] = ddz
        dp_ref[:, 4 * BR:4 * BR + 128] = dab
        ds_ref[0] = ds0
        ds_ref[1] = ds1
        dp1_ref[...] += dp1
        dp2_ref[...] += dp2
        dng_ref[...] += dng
        dext = _conv_taps_bwd(ext, cw_ref, dcw_ref, dcq, HALO_S, K_DN, DN_T)
        dp_ref[:, 0:3 * BR] = _add_tail(dext[HALO_S:], carry_ref[...])
        carry_ref[...] = dext[:HALO_S]

    rev = lambda i: n - 1 - i
    hmap = _halo_map(DN_T, HALO_S, 3)
    v128 = pl.BlockSpec((1, 128), lambda i: (0, 0))
    return _sweep_with_exchange(
        body, n,
        in_specs=[pl.BlockSpec((DN_T, 3 * BR), lambda i: (rev(i), 3)), pl.BlockSpec((DN_T, 128), lambda i: (rev(i), 26)),
                  pl.BlockSpec((DN_T, BR), lambda i: (rev(i), 12)),
                  pl.BlockSpec((HALO_S, 3 * BR), lambda i: hmap(rev(i))),
                  pl.BlockSpec((DN_T, BR), lambda i: (rev(i), 3)),
                  pl.BlockSpec((DN_G,) + DN_STATE, lambda i: (rev(i), 0, 0, 0)),
                  pl.BlockSpec((DN_G,) + DN_INV, lambda i: (rev(i), 0, 0, 0)),
                  pl.BlockSpec((HALO_S, 3 * BR), lambda i: (0, 0)), v128, v128,
                  pl.BlockSpec((1, DN_D), lambda i: (0, 0))],
        out_specs=[pl.BlockSpec((DN_T, W_D), lambda i: (rev(i), 0)),
                   pl.BlockSpec((HALO_S, 3 * BR), lambda i: (0, 0)), v128, v128,
                   pl.BlockSpec((1, DN_D), lambda i: (0, 0))],
        out_shape=[jax.ShapeDtypeStruct((L, W_D), F32), jax.ShapeDtypeStruct((HALO_S, 3 * BR), F32),
                   jax.ShapeDtypeStruct((1, 128), F32), jax.ShapeDtypeStruct((1, 128), F32),
                   jax.ShapeDtypeStruct((1, DN_D), F32)],
        scratch_shapes=[pltpu.VMEM(DN_STATE, F32), pltpu.VMEM((HALO_S, 3 * BR), F32)],
        args=(proj, proj, proj, proj, dmix, sall, inv, cw, p1, p2, ng), exchange=exchange, name="d_bwd")


def _pick_rows(rows, cap):
    best = 8
    for t in range(8, cap + 1, 8):
        if rows % t == 0:
            best = t
    return best


def _adamw(w, g, m, v, name, lead=None):
    rows, rest = w.shape[0], w.shape[1:]
    tr = _pick_rows(rows, 512) if lead is None else lead
    c1 = 1.0 - ADAM_B1 ** ADAM_STEP
    c2 = 1.0 - ADAM_B2 ** ADAM_STEP

    def body(w_ref, g_ref, m_ref, v_ref, d_ref, mo_ref, vo_ref):
        gv = g_ref[...]
        mn = ADAM_B1 * m_ref[...] + (1.0 - ADAM_B1) * gv
        vn = ADAM_B2 * v_ref[...] + (1.0 - ADAM_B2) * (gv * gv)
        d_ref[...] = -ADAM_LR * ((mn / c1) / (jnp.sqrt(vn / c2) + ADAM_EPS) + ADAM_WD * w_ref[...])
        mo_ref[...] = mn
        vo_ref[...] = vn

    spec = pl.BlockSpec((tr,) + rest, lambda i: (i,) + (0,) * len(rest))
    return pl.pallas_call(
        body, grid=(rows // tr,), in_specs=[spec] * 4, out_specs=[spec] * 3,
        out_shape=[jax.ShapeDtypeStruct(w.shape, F32)] * 3,
        name=name, compiler_params=_cp("parallel"))(w, g, m, v)


def _sum_slots(r, name):
    n, rows, wd = r.shape
    tr = _pick_rows(rows, 384)

    def body(r_ref, o_ref):
        acc = r_ref[0].astype(F32)
        for j in range(1, n):
            acc = acc + r_ref[j].astype(F32)
        o_ref[...] = acc

    return pl.pallas_call(
        body, grid=(rows // tr,),
        in_specs=[pl.BlockSpec((n, tr, wd), lambda i: (0, i, 0))],
        out_specs=pl.BlockSpec((tr, wd), lambda i: (i, 0)),
        out_shape=jax.ShapeDtypeStruct((rows, wd), F32),
        name=name, compiler_params=_cp("parallel"))(r)


AXES = ("x", "y", "c")


def _group_peer(axes, k):
    pos = {a: lax.axis_index(a) for a in AXES}
    idx = 0
    for a in axes:
        idx = idx * 2 + pos[a]
    peer = dict(pos)
    for b, a in enumerate(reversed(axes)):
        if (k >> b) & 1:
            peer[a] = 1 - pos[a]
    return idx, tuple(peer[a] for a in AXES)


MAX_CHUNKS = 4


class _Exchange:
    def __init__(self, xs, axes, mode):
        self.axes, self.mode, self.na, self.n = axes, mode, len(xs), 2 ** len(axes)
        n = self.n
        self.out_shape, self.pieces = [], []
        for x in xs:
            if mode == "gather":
                shape, lead = (n,) + x.shape, x.shape[0]
            elif mode == "scatter":
                shape, lead = x.shape, x.shape[1]
            else:
                shape, lead = (x.shape[0], n * x.shape[1], x.shape[2]), x.shape[0]
            self.out_shape.append(jax.ShapeDtypeStruct(shape, x.dtype))
            big = x.size * x.dtype.itemsize >= (1 << 20)
            if mode == "rows":
                self.pieces.append(lead if lead <= MAX_CHUNKS else 1)
            else:
                self.pieces.append(MAX_CHUNKS if big and lead % (16 * MAX_CHUNKS) == 0 else 1)
        self.in_specs = [pl.BlockSpec(memory_space=pl.ANY)] * self.na
        self.out_specs = [pl.BlockSpec(memory_space=pl.ANY)] * self.na
        self.scratch_shapes = [pltpu.SemaphoreType.DMA((self.na, MAX_CHUNKS, n)),
                               pltpu.SemaphoreType.DMA((self.na, MAX_CHUNKS, n)),
                               pltpu.SemaphoreType.DMA((self.na, MAX_CHUNKS))]

    def _copies(self, x_refs, o_refs, send_sems, recv_sems, local_sems):
        me, _ = _group_peer(self.axes, 0)
        local, remote = [], []
        for a, (x, o) in enumerate(zip(x_refs, o_refs)):
            for c in range(self.pieces[a]):
                if self.mode == "rows":
                    r = x.shape[1]
                    b = slice(None) if self.pieces[a] == 1 else pl.ds(c, 1)
                    src = lambda k, x=x, b=b: x.at[b]
                    dst = o.at[b, pl.ds(me * r, r)]
                else:
                    lead = x.shape[1] if self.mode == "scatter" else x.shape[0]
                    rs = pl.ds(c * (lead // self.pieces[a]), lead // self.pieces[a])
                    if self.mode == "scatter":
                        src = lambda k, x=x, rs=rs: x.at[me ^ k, rs]
                    else:
                        src = lambda k, x=x, rs=rs: x.at[rs]
                    dst = o.at[me, rs]
                local.append(pltpu.make_async_copy(src(0), dst, local_sems.at[a, c]))
                for k in range(1, self.n):
                    remote.append(pltpu.make_async_remote_copy(
                        src_ref=src(k), dst_ref=dst, send_sem=send_sems.at[a, c, k], recv_sem=recv_sems.at[a, c, k],
                        device_id=_group_peer(self.axes, k)[1], device_id_type=MESH))
        return local, remote

    def start(self, x_refs, o_refs, sems):
        local, remote = self._copies(x_refs, o_refs, *sems)
        for cp in local + remote:
            cp.start()

    def wait(self, x_refs, o_refs, sems):
        local, remote = self._copies(x_refs, o_refs, *sems)
        for cp in remote:
            cp.wait_send()
        for cp in remote:
            cp.wait_recv()
        for cp in local:
            cp.wait()


def _exchange(xs, axes, mode, name):
    ex = _Exchange(xs, axes, mode)
    na = ex.na

    def body(*refs):
        ex.start(refs[:na], refs[na:2 * na], refs[2 * na:])
        ex.wait(refs[:na], refs[na:2 * na], refs[2 * na:])

    return pl.pallas_call(body, out_shape=ex.out_shape, in_specs=ex.in_specs, out_specs=ex.out_specs,
                          scratch_shapes=ex.scratch_shapes, name=name)(*xs)


SHARDED_SMALL = (("a_conv_w", 2), ("a_pw_w", 1), ("s5_glu_w", 1), ("c_conv_w", 2), ("d_conv_w", 2))
REPLICATED = ("norm_g", "a_conv_b", "a_ln_g", "a_ln_b", "a_pw_b", "s5_lambda_re", "s5_lambda_im", "s5_b_re",
              "s5_b_im", "s5_c_re", "s5_c_im", "s5_d", "s5_log_dt", "s5_glu_b", "d_a_log", "d_dt_bias",
              "d_norm_g", "final_g")
WEIGHTS = ("norm_g", "w_in", "a_conv_w", "a_conv_b", "a_ln_g", "a_ln_b", "a_pw_w", "a_pw_b", "s5_lambda_re",
           "s5_lambda_im", "s5_b_re", "s5_b_im", "s5_c_re", "s5_c_im", "s5_d", "s5_log_dt", "s5_glu_w",
           "s5_glu_b", "c_conv_w", "d_conv_w", "d_a_log", "d_dt_bias", "d_norm_g", "w_out", "final_g")
LANES = 1024


def _size(shape):
    size = 1
    for d in shape:
        size *= d
    return size


def _slab_rows(shape):
    return -(-_size(shape) // (8 * LANES)) * 8


def _pack(arrs, rows):
    parts = []
    for a in arrs:
        r = _slab_rows(a.shape)
        parts.append(jnp.pad(a.reshape(-1), (0, r * LANES - a.size)).reshape(r, LANES))
    used = sum(p.shape[0] for p in parts)
    if rows > used:
        parts.append(jnp.zeros((rows - used, LANES), parts[0].dtype))
    return jnp.concatenate(parts, axis=0)


def _unpack(slab, shapes):
    out, off = [], 0
    for s in shapes:
        r = _slab_rows(s)
        out.append(slab[off:off + r].reshape(-1)[:_size(s)].reshape(s))
        off += r
    return out


def _rows_for(shapes, mult):
    rows = sum(_slab_rows(s) for s in shapes)
    return -(-rows // mult) * mult


def _row(v, width=None):
    v = v.reshape(1, -1)
    return v if width is None else jnp.pad(v, ((0, 0), (0, width - v.shape[1])))


def _pad_rows(w, rows):
    return jnp.pad(w, ((0, rows - w.shape[0]), (0, 0)))


def _assemble_in(shards):
    n = shards[0].shape[1]
    cut = 12 * BR - 3 * n
    last = shards[3]
    return jnp.concatenate([shards[0], shards[1], shards[2], last[:, :cut], last[:, cut + 2 * DN_H:],
                            last[:, cut:cut + 2 * DN_H],
                            jnp.zeros((last.shape[0], N_INP - N_IN), last.dtype)], axis=1)


def _layer_fwd(x, p, exchanges):
    proj, h = _proj_fwd(x, p["norm_g"], p["wp"])
    ya, a_conv = _a_fwd(proj, p["a_cw"], p["a_cb"], p["a_lng"], p["a_lnb"], p["a_pw"], p["a_pwb"])
    ar, ai, bemb, cemb = _s5_prep(*p["s5"])
    got = {}
    (yb, sin_all, states), got["b"] = _b_fwd(proj, ar, ai, bemb, cemb, p["s5_d"], p["glu_w"], p["glu_b"],
                                             exchanges.get("b"))
    yc = _c_fwd(proj, p["c_cw"])
    (yd, sall, inv), got["d"] = _d_fwd(proj, p["d_cw"], p["d_p1"], p["d_p2"], p["d_ng"], exchanges.get("d"))
    xo = _out_fwd(x, ya, yb, yc, yd, p["wo"])
    return xo, dict(x=x, proj=proj, h=h, ys=(ya, yb, yc, yd), a_conv=a_conv, sin_all=sin_all, states=states,
                    sall=sall, inv=inv,
                    s5=(ar, ai, bemb, cemb)), got


def _layer_bwd(dxo, p, r, exchanges):
    proj = r["proj"]
    ar, ai, bemb, cemb = r["s5"]
    dmix = _out_bwd_x(dxo, p["wo"])
    dwo = _dwout(*r["ys"], dxo)
    dpa, dcw_a, dcb, dlng, dlnb, dpw, dpwb = _a_bwd(proj, dmix, r["a_conv"], p["a_cw"], p["a_cb"], p["a_lng"],
                                                     p["a_lnb"], p["a_pw"], p["a_pwb"])
    got = {}
    (dpb, dar, dai, dbe, dce, ddsk, dgw, dgb), got["b"] = _b_bwd(
        proj, dmix, r["sin_all"], r["states"], ar, ai, bemb, cemb, p["s5_d"], p["glu_w"], p["glu_b"],
        exchanges.get("b"))
    dlre, dlim, dldt, dbre, dbim, dcre, dcim = _s5_prep_bwd(*p["s5"], dar, dai, dbe, dce)
    dpc, dcw_c = _c_bwd(proj, dmix, p["c_cw"])
    (dpd, dcw_d, dp1, dp2, dng), got["d"] = _d_bwd(proj, dmix, r["sall"], r["inv"], p["d_cw"], p["d_p1"],
                                                   p["d_p2"], p["d_ng"], exchanges.get("d"))
    dwa, dwb, dwc, dwd = _dwin(r["h"], dpa, dpb, dpc, dpd)
    dwin = jnp.concatenate([dwa, dwb, dwc, dwd[:, :3 * BR], dwd[:, 4 * BR:4 * BR + 2 * DN_H],
                            dwd[:, 3 * BR:4 * BR]], axis=1)
    ex_proj = exchanges.get("proj")
    if callable(ex_proj):
        ex_proj = ex_proj(got["d"], dwin, dwo)
    (dx, dg), got["proj"] = _proj_bwd_x(r["x"], p["norm_g"], dpa, dpb, dpc, dpd, p["wp"], dxo, ex_proj)

    def unrows(t, perm):
        return jnp.transpose(t.reshape(S5_H, S5_G, S5_P), perm)

    grads = dict(
        norm_g=dg.reshape(-1), w_in=dwin, a_conv_w=dcw_a[:K_A], a_conv_b=dcb.reshape(-1),
        a_ln_g=dlng.reshape(-1), a_ln_b=dlnb.reshape(-1), a_pw_w=dpw, a_pw_b=dpwb.reshape(-1),
        s5_lambda_re=dlre.reshape(S5_G, S5_P), s5_lambda_im=dlim.reshape(S5_G, S5_P),
        s5_b_re=unrows(dbre, (1, 2, 0)), s5_b_im=unrows(dbim, (1, 2, 0)),
        s5_c_re=unrows(dcre, (1, 0, 2)), s5_c_im=unrows(dcim, (1, 0, 2)),
        s5_d=ddsk.reshape(-1), s5_log_dt=dldt[0, :S5_G], s5_glu_w=dgw, s5_glu_b=dgb.reshape(-1),
        c_conv_w=dcw_c[:K_C], d_conv_w=dcw_d[:K_DN], d_a_log=dp1[0, :DN_H], d_dt_bias=dp2[0, :DN_H],
        d_norm_g=dng.reshape(-1), w_out=dwo)
    return dx, grads, got


def _layer_params(full, wp, wo, l):
    return dict(
        norm_g=_row(full["norm_g"][l]), wp=wp,
        a_cw=_pad_rows(full["a_conv_w"][l], HALO_A), a_cb=_row(full["a_conv_b"][l]),
        a_lng=_row(full["a_ln_g"][l]), a_lnb=_row(full["a_ln_b"][l]), a_pw=full["a_pw_w"][l],
        a_pwb=_row(full["a_pw_b"][l]),
        s5=(_row(full["s5_lambda_re"][l]), _row(full["s5_lambda_im"][l]), _row(full["s5_log_dt"][l], 128),
            jnp.transpose(full["s5_b_re"][l], (2, 0, 1)).reshape(S5_H, NS),
            jnp.transpose(full["s5_b_im"][l], (2, 0, 1)).reshape(S5_H, NS),
            jnp.transpose(full["s5_c_re"][l], (1, 0, 2)).reshape(S5_H, NS),
            jnp.transpose(full["s5_c_im"][l], (1, 0, 2)).reshape(S5_H, NS)),
        s5_d=_row(full["s5_d"][l]), glu_w=full["s5_glu_w"][l], glu_b=_row(full["s5_glu_b"][l]),
        c_cw=_pad_rows(full["c_conv_w"][l], HALO_S), d_cw=_pad_rows(full["d_conv_w"][l], HALO_S),
        d_p1=_row(full["d_a_log"][l], 128), d_p2=_row(full["d_dt_bias"][l], 128),
        d_ng=_row(full["d_norm_g"][l]), wo=wo)


def kernel(x, norm_g, w_in, a_conv_w, a_conv_b, a_ln_g, a_ln_b, a_pw_w, a_pw_b, s5_lambda_re, s5_lambda_im, s5_b_re, s5_b_im, s5_c_re, s5_c_im, s5_d, s5_log_dt, s5_glu_w, s5_glu_b, c_conv_w, d_conv_w, d_a_log, d_dt_bias, d_norm_g, w_out, final_g, loss_target, m_norm_g, m_w_in, m_a_conv_w, m_a_conv_b, m_a_ln_g, m_a_ln_b, m_a_pw_w, m_a_pw_b, m_s5_lambda_re, m_s5_lambda_im, m_s5_b_re, m_s5_b_im, m_s5_c_re, m_s5_c_im, m_s5_d, m_s5_log_dt, m_s5_glu_w, m_s5_glu_b, m_c_conv_w, m_d_conv_w, m_d_a_log, m_d_dt_bias, m_d_norm_g, m_w_out, m_final_g, v_norm_g, v_w_in, v_a_conv_w, v_a_conv_b, v_a_ln_g, v_a_ln_b, v_a_pw_w, v_a_pw_b, v_s5_lambda_re, v_s5_lambda_im, v_s5_b_re, v_s5_b_im, v_s5_c_re, v_s5_c_im, v_s5_d, v_s5_log_dt, v_s5_glu_w, v_s5_glu_b, v_c_conv_w, v_d_conv_w, v_d_a_log, v_d_dt_bias, v_d_norm_g, v_w_out, v_final_g):
    given = dict(locals())
    w = {n: given[n] for n in WEIGHTS}
    m = {n: given["m_" + n] for n in WEIGHTS}
    v = {n: given["v_" + n] for n in WEIGHTS}
    xs, tgt = x[0], loss_target[0]

    n_in, n_out = w["w_in"].shape[2], w["w_out"].shape[1]
    sm_names = [n for n, _ in SHARDED_SMALL]
    sm_shapes = [w[n].shape for n in sm_names]
    sm_rows = _rows_for(sm_shapes, 16)
    win_b, wout_b = w["w_in"].astype(BF), w["w_out"].astype(BF)
    g_in, g_out, g_sm = _exchange([win_b[0], wout_b[0], _pack([w[n] for n in sm_names], sm_rows)],
                                  ("x", "y"), "gather", "gather_first")
    full = dict(w)
    parts = [_unpack(g_sm[j], sm_shapes) for j in range(4)]
    for i, (n, ax) in enumerate(SHARDED_SMALL):
        full[n] = jnp.concatenate([parts[j][i] for j in range(4)], axis=ax)

    saved = []
    h = xs
    for l in range(DEPTH):
        p = _layer_params(full, _assemble_in([g_in[j] for j in range(4)]),
                          jnp.concatenate([g_out[j] for j in range(4)], axis=0), l)
        nxt = {}
        if l + 1 < DEPTH:
            nxt = {"d": ([win_b[l + 1]], ("x", "y"), "gather"), "b": ([wout_b[l + 1]], ("x", "y"), "gather")}
        h, r, got = _layer_fwd(h, p, nxt)
        saved.append((p, r))
        if l + 1 < DEPTH:
            (g_in,), (g_out,) = got["d"], got["b"]
    loss_tile, dx, dfg = _loss_bwd(h, _row(full["final_g"]), tgt)

    def big_slots(dwin, dwo):
        s_in = jnp.stack([dwin[:, j * n_in:(j + 1) * n_in].astype(BF) for j in range(4)])
        return [s_in.reshape(8, D_MODEL // 2, n_in), dwo.astype(BF).reshape(8, n_out // 2, D_MODEL)]

    def halves(rv):
        return [_sum_slots(rv[0], "sum_w_in")[None], _sum_slots(rv[1], "sum_w_out")[None]]

    layer_grads, summed, pending, arrived = [None] * DEPTH, [None] * DEPTH, None, {}
    for l in reversed(range(DEPTH)):
        p, r = saved[l]
        ex = {}
        if pending is not None:
            ex["d"] = (pending, AXES, "scatter")
        if l + 2 in arrived:
            ex["b"] = (halves(arrived.pop(l + 2)), ("c",), "rows")
        if l == 0:
            ex["proj"] = lambda came, dwin, dwo: [(halves(came), ("c",), "rows"),
                                                  (big_slots(dwin, dwo), AXES, "scatter")]
        dx, layer_grads[l], got = _layer_bwd(dx, p, r, ex)
        if got["b"] is not None:
            summed[l + 2] = got["b"]
        if got["proj"] is not None:
            summed[1], arrived[0] = got["proj"]
        elif got["d"] is not None:
            arrived[l + 1] = got["d"]
        pending = big_slots(layer_grads[l]["w_in"], layer_grads[l]["w_out"]) if l else None
    grads = {n: jnp.stack([layer_grads[l][n] for l in range(DEPTH)]) for n in WEIGHTS
             if n not in ("final_g", "w_in", "w_out")}
    grads["final_g"] = dfg.reshape(-1)
    slots = []
    for j in range(4):
        sl = [lax.slice_in_dim(grads[n], j * w[n].shape[ax], (j + 1) * w[n].shape[ax], axis=ax)
              for n, ax in SHARDED_SMALL]
        slots.append(_pack(sl, sm_rows))
    rp_shapes = [w[n].shape for n in REPLICATED] + [(1,)]
    rp_rows = _rows_for(rp_shapes, 64)
    r_sm, r_rp = _exchange(
        [jnp.stack(slots).reshape(8, sm_rows // 2, LANES),
         _pack([grads[n] for n in REPLICATED] + [loss_tile[0, 0:1]], rp_rows).reshape(8, rp_rows // 8, LANES)],
        AXES, "scatter", "scatter_last")
    summed[0] = _exchange(halves(arrived.pop(0)) + [_sum_slots(r_sm, "sum_small")[None]], ("c",), "rows",
                          "gather_halves")
    h_sm = summed[0][2]
    h_in = jnp.concatenate([summed[l][0] for l in range(DEPTH)], axis=0)
    h_out = jnp.concatenate([summed[l][1] for l in range(DEPTH)], axis=0)
    (g_rp,) = _exchange([_sum_slots(r_rp, "sum_replicated")], AXES, "gather", "gather_replicated")
    g_rp = g_rp.reshape(rp_rows, LANES)
    g_sm = h_sm.reshape(sm_rows, LANES)

    out = {}

    def put(name, shape, res):
        for key, t in zip(("delta", "new_m", "new_v"), res):
            out[key + "_" + name] = t.reshape(shape)

    g2 = h_out.reshape(DEPTH * n_out, D_MODEL)
    out["grad_w_out"] = g2.reshape(w["w_out"].shape)
    put("w_out", w["w_out"].shape, _adamw(w["w_out"].reshape(g2.shape), g2, m["w_out"].reshape(g2.shape),
                                          v["w_out"].reshape(g2.shape), "adamw_w_out"))
    cm = lambda a: jnp.transpose(a, (2, 0, 1))
    rm = lambda a: jnp.transpose(a, (1, 2, 0))
    g3 = cm(h_in)
    out["grad_w_in"] = rm(g3)
    for key, t in zip(("delta", "new_m", "new_v"),
                      _adamw(cm(w["w_in"]), g3, cm(m["w_in"]), cm(v["w_in"]), "adamw_w_in", lead=n_in // 6)):
        out[key + "_w_in"] = rm(t)
    zero = jnp.zeros((1,), F32)
    res_sm = _adamw(_pack([w[n] for n in sm_names], sm_rows), g_sm, _pack([m[n] for n in sm_names], sm_rows),
                    _pack([v[n] for n in sm_names], sm_rows), "adamw_small")
    res_rp = _adamw(_pack([w[n] for n in REPLICATED] + [zero], rp_rows), g_rp,
                    _pack([m[n] for n in REPLICATED] + [zero], rp_rows),
                    _pack([v[n] for n in REPLICATED] + [zero], rp_rows), "adamw_replicated")
    for key, sm, rp in (("grad", g_sm, g_rp), ("delta", res_sm[0], res_rp[0]), ("new_m", res_sm[1], res_rp[1]),
                        ("new_v", res_sm[2], res_rp[2])):
        for n, t in zip(sm_names, _unpack(sm, sm_shapes)):
            out[key + "_" + n] = t
        for n, t in zip(REPLICATED, _unpack(rp, rp_shapes[:-1])):
            out[key + "_" + n] = t
    loss = _unpack(g_rp, rp_shapes)[-1].reshape(())
    return (loss, dx[None], *[out["grad_" + n] for n in WEIGHTS], *[out["delta_" + n] for n in WEIGHTS],
            *[out["new_m_" + n] for n in WEIGHTS], *[out["new_v_" + n] for n in WEIGHTS])
```

```python
import functools

import jax
import jax.numpy as jnp
from jax import lax
from jax.experimental import pallas as pl
from jax.experimental.pallas import tpu as pltpu

F32, BF = jnp.float32, jnp.bfloat16
HI = lax.Precision.HIGHEST
MESH = pl.DeviceIdType.MESH

D_MODEL = 1024
BR = 256
DEPTH = 4
N_IN = 3336
N_INP = 3456
COL_A, COL_B, COL_C, COL_D = 0, 768, 1280, 2304
W_A, W_B, W_C, W_D = 768, 512, 1024, 1152
S5_G, S5_H, S5_P = 16, 16, 64
NS = S5_G * S5_P
DN_H, DN_D, DN_C = 4, 64, 64
DN_G_FWD, DN_G_BWD = 8, 4
DN_P = DN_H // 2
DN_STATE = (DN_P, 2 * DN_D, 2 * DN_D)
DN_INV = (DN_P, DN_C, 2 * DN_D)
K_A, K_C, K_DN = 31, 3, 4
HALO_A, HALO_S = 32, 8
EPS = 1e-6
TL = 256
VMEM_LIMIT = 56 * 1024 * 1024

ADAM_LR, ADAM_B1, ADAM_B2, ADAM_EPS, ADAM_WD, ADAM_STEP = 0.001, 0.9, 0.999, 1e-08, 0.01, 10


def _cp(*sem):
    return pltpu.CompilerParams(dimension_semantics=sem, vmem_limit_bytes=VMEM_LIMIT)


def _sigmoid(x):
    return jax.nn.sigmoid(x)


def _silu(x):
    return x * jax.nn.sigmoid(x)


def _rmsnorm(x, g):
    return x * lax.rsqrt(jnp.mean(x * x, axis=-1, keepdims=True) + EPS) * g


@jax.custom_vjp
def _mm(a, w):
    return jnp.dot(a.astype(BF), w.astype(BF), preferred_element_type=F32)


def _mm_f(a, w):
    return _mm(a, w), (a, w)


def _mm_b(res, g):
    a, w = res
    gb = g.astype(BF)
    da = lax.dot_general(gb, w.astype(BF), (((1,), (1,)), ((), ())), preferred_element_type=F32)
    dw = lax.dot_general(a.astype(BF), gb, (((0,), (0,)), ((), ())), preferred_element_type=F32)
    return da, dw


_mm.defvjp(_mm_f, _mm_b)


@jax.custom_vjp
def _mm_nt(a, b):
    return lax.dot_general(a.astype(BF), b.astype(BF), (((1,), (1,)), ((), ())), preferred_element_type=F32)


def _mm_nt_f(a, b):
    return _mm_nt(a, b), (a, b)


def _mm_nt_b(res, g):
    a, b = res
    gb = g.astype(BF)
    da = jnp.dot(gb, b.astype(BF), preferred_element_type=F32)
    db = lax.dot_general(gb, a.astype(BF), (((0,), (0,)), ((), ())), preferred_element_type=F32)
    return da, db


_mm_nt.defvjp(_mm_nt_f, _mm_nt_b)


@jax.custom_vjp
def _mm_tn(a, b):
    return lax.dot_general(a.astype(BF), b.astype(BF), (((0,), (0,)), ((), ())), preferred_element_type=F32)


def _mm_tn_f(a, b):
    return _mm_tn(a, b), (a, b)


def _mm_tn_b(res, g):
    a, b = res
    gb = g.astype(BF)
    da = lax.dot_general(b.astype(BF), gb, (((1,), (1,)), ((), ())), preferred_element_type=F32)
    db = jnp.dot(a.astype(BF), gb, preferred_element_type=F32)
    return da, db


_mm_tn.defvjp(_mm_tn_f, _mm_tn_b)


def _dot_hi(a, b):
    return jnp.dot(a, b, precision=HI, preferred_element_type=F32)


def _split(a):
    hi = a.astype(BF)
    return hi, (a - hi.astype(F32)).astype(BF)


def _dot3(a, b, dims=(((1,), (0,)), ((), ()))):
    ah, al = _split(a)
    bh, bl = _split(b)
    d = functools.partial(lax.dot_general, dimension_numbers=dims, preferred_element_type=F32)
    return d(ah, bh) + d(ah, bl) + d(al, bh)


@jax.custom_vjp
def _mm3(a, b):
    return _dot3(a, b)


def _mm3_f(a, b):
    return _dot3(a, b), (a, b)


def _mm3_b(res, g):
    a, b = res
    return _dot3(g, b, (((1,), (1,)), ((), ()))), _dot3(a, g, (((0,), (0,)), ((), ())))


_mm3.defvjp(_mm3_f, _mm3_b)


def _roll(x, s):
    n = x.shape[0]
    s = s % n
    return x if s == 0 else pltpu.roll(x, s, 0)


def _conv_taps(ext, w_ref, halo, k_taps, tl):
    acc = None
    for k in range(k_taps):
        term = _roll(ext, (k_taps - 1) - k)[halo:halo + tl] * w_ref[k:k + 1, :]
        acc = term if acc is None else acc + term
    return acc


def _conv_taps_bwd(ext, w_ref, dw_ref, dacc, halo, k_taps, tl):
    dpad = jnp.concatenate([dacc, jnp.zeros((halo, dacc.shape[1]), F32)], axis=0)
    dext = None
    for k in range(k_taps):
        r = _roll(ext, (k_taps - 1) - k)[halo:halo + tl]
        dw_ref[k:k + 1, :] += jnp.sum(r * dacc, axis=0, keepdims=True)
        term = _roll(dpad, halo - (k_taps - 1) + k) * w_ref[k:k + 1, :]
        dext = term if dext is None else dext + term
    return dext


def _add_tail(x, tail):
    tl, h = x.shape[0], tail.shape[0]
    return x + jnp.concatenate([jnp.zeros((tl - h, x.shape[1]), F32), tail], axis=0)


def _proj_fwd(x, g, wp):
    L = x.shape[0]

    def body(x_ref, g_ref, w_ref, p_ref, h_ref):
        hb = _rmsnorm(x_ref[...], g_ref[...]).astype(BF)
        h_ref[...] = hb
        p_ref[...] = jnp.dot(hb, w_ref[...], preferred_element_type=F32)

    return pl.pallas_call(
        body, grid=(L // TL,),
        in_specs=[pl.BlockSpec((TL, D_MODEL), lambda i: (i, 0)),
                  pl.BlockSpec((1, D_MODEL), lambda i: (0, 0)),
                  pl.BlockSpec((D_MODEL, N_INP), lambda i: (0, 0))],
        out_specs=[pl.BlockSpec((TL, N_INP), lambda i: (i, 0)),
                   pl.BlockSpec((TL, D_MODEL), lambda i: (i, 0))],
        out_shape=[jax.ShapeDtypeStruct((L, N_INP), F32), jax.ShapeDtypeStruct((L, D_MODEL), BF)],
        name="proj_fwd", compiler_params=_cp("parallel"))(x, g, wp)


def _proj_bwd_x(x, g, dpa, dpb, dpc, dpd, wp, dxo, exchange=None):
    L = x.shape[0]

    def body(x_ref, g_ref, a_ref, b_ref, c_ref, d_ref, w_ref, dxo_ref, dx_ref, dg_ref):
        dh = None
        for ref, c0, wd in ((a_ref, COL_A, W_A), (b_ref, COL_B, W_B), (c_ref, COL_C, W_C), (d_ref, COL_D, W_D)):
            t = lax.dot_general(ref[...].astype(BF), w_ref[:, c0:c0 + wd], (((1,), (1,)), ((), ())),
                                preferred_element_type=F32)
            dh = t if dh is None else dh + t
        _, vj = jax.vjp(_rmsnorm, x_ref[...], g_ref[...])
        dx, dg = vj(dh)
        dx_ref[...] = dxo_ref[...] + dx

        @pl.when(pl.program_id(0) == 0)
        def _():
            dg_ref[...] = jnp.zeros_like(dg_ref)

        dg_ref[...] += dg

    def rows(wd):
        return pl.BlockSpec((TL, wd), lambda i: (i, 0))

    return _sweep_with_exchange(
        body, L // TL,
        in_specs=[rows(D_MODEL), pl.BlockSpec((1, D_MODEL), lambda i: (0, 0)),
                  rows(W_A), rows(W_B), rows(W_C), rows(W_D),
                  pl.BlockSpec((D_MODEL, N_INP), lambda i: (0, 0)), rows(D_MODEL)],
        out_specs=[rows(D_MODEL), pl.BlockSpec((1, D_MODEL), lambda i: (0, 0))],
        out_shape=[jax.ShapeDtypeStruct((L, D_MODEL), F32), jax.ShapeDtypeStruct((1, D_MODEL), F32)],
        scratch_shapes=[], args=(x, g, dpa, dpb, dpc, dpd, wp, dxo), exchange=exchange, name="proj_bwd_x")


def _dwin(h, dpa, dpb, dpc, dpd):
    L = h.shape[0]

    def body(h_ref, a_ref, b_ref, c_ref, d_ref, oa_ref, ob_ref, oc_ref, od_ref):
        outs = (oa_ref, ob_ref, oc_ref, od_ref)

        @pl.when(pl.program_id(0) == 0)
        def _():
            for o in outs:
                o[...] = jnp.zeros_like(o)

        ht = h_ref[...].T
        for ref, o in zip((a_ref, b_ref, c_ref, d_ref), outs):
            o[...] += jnp.dot(ht, ref[...].astype(BF), preferred_element_type=F32)

    def rows(wd):
        return pl.BlockSpec((TL, wd), lambda i: (i, 0))

    def whole(wd):
        return pl.BlockSpec((D_MODEL, wd), lambda i: (0, 0))

    widths = (W_A, W_B, W_C, W_D)
    return pl.pallas_call(
        body, grid=(L // TL,),
        in_specs=[rows(D_MODEL)] + [rows(wd) for wd in widths],
        out_specs=[whole(wd) for wd in widths],
        out_shape=[jax.ShapeDtypeStruct((D_MODEL, wd), F32) for wd in widths],
        name="dwin", compiler_params=_cp("arbitrary"))(h, dpa, dpb, dpc, dpd)


def _dwout(ya, yb, yc, yd, dxo):
    L = dxo.shape[0]
    tk, tn = min(512, L), 512

    def body(a_ref, b_ref, c_ref, d_ref, g_ref, o_ref):
        @pl.when(pl.program_id(1) == 0)
        def _():
            o_ref[...] = jnp.zeros_like(o_ref)

        gb = g_ref[...].astype(BF)
        for j, ref in enumerate((a_ref, b_ref, c_ref, d_ref)):
            o_ref[j * BR:(j + 1) * BR, :] += lax.dot_general(ref[...].astype(BF), gb, (((0,), (0,)), ((), ())),
                                                             preferred_element_type=F32)

    ys = pl.BlockSpec((tk, BR), lambda j, t: (t, 0))
    return pl.pallas_call(
        body, grid=(D_MODEL // tn, L // tk),
        in_specs=[ys, ys, ys, ys, pl.BlockSpec((tk, tn), lambda j, t: (t, j))],
        out_specs=pl.BlockSpec((D_MODEL, tn), lambda j, t: (0, j)),
        out_shape=jax.ShapeDtypeStruct((D_MODEL, D_MODEL), F32),
        name="dwout", compiler_params=_cp("parallel", "arbitrary"))(ya, yb, yc, yd, dxo)


def _out_fwd(x, ya, yb, yc, yd, wo):
    L = x.shape[0]

    def body(x_ref, a_ref, b_ref, c_ref, d_ref, w_ref, o_ref):
        acc = x_ref[...]
        for j, ref in enumerate((a_ref, b_ref, c_ref, d_ref)):
            acc = acc + jnp.dot(ref[...].astype(BF), w_ref[j * BR:(j + 1) * BR, :], preferred_element_type=F32)
        o_ref[...] = acc

    def rows(wd):
        return pl.BlockSpec((TL, wd), lambda i: (i, 0))

    return pl.pallas_call(
        body, grid=(L // TL,),
        in_specs=[rows(D_MODEL), rows(BR), rows(BR), rows(BR), rows(BR),
                  pl.BlockSpec((D_MODEL, D_MODEL), lambda i: (0, 0))],
        out_specs=rows(D_MODEL), out_shape=jax.ShapeDtypeStruct((L, D_MODEL), F32),
        name="out_fwd", compiler_params=_cp("parallel"))(x, ya, yb, yc, yd, wo)


def _out_bwd_x(dxo, wo):
    L = dxo.shape[0]

    def body(d_ref, w_ref, o_ref):
        o_ref[...] = lax.dot_general(d_ref[...].astype(BF), w_ref[...], (((1,), (1,)), ((), ())),
                                     preferred_element_type=F32)

    return pl.pallas_call(
        body, grid=(L // TL,),
        in_specs=[pl.BlockSpec((TL, D_MODEL), lambda i: (i, 0)), pl.BlockSpec((D_MODEL, D_MODEL), lambda i: (0, 0))],
        out_specs=pl.BlockSpec((TL, D_MODEL), lambda i: (i, 0)),
        out_shape=jax.ShapeDtypeStruct((L, D_MODEL), F32),
        name="out_bwd_x", compiler_params=_cp("parallel"))(dxo, wo)


def _loss_bwd(x, g, tgt):
    L = x.shape[0]

    def f(xv, gv, tv):
        err = _rmsnorm(xv, gv) - tv
        return 0.5 * jnp.sum(jnp.mean(err * err, axis=-1, keepdims=True), axis=0, keepdims=True)

    def body(x_ref, g_ref, t_ref, loss_ref, dx_ref, dg_ref):
        tv = t_ref[...]
        loss, vj = jax.vjp(lambda a, b: f(a, b, tv), x_ref[...], g_ref[...])
        dx, dg = vj(jnp.ones((1, 1), F32))
        dx_ref[...] = dx

        @pl.when(pl.program_id(0) == 0)
        def _():
            dg_ref[...] = jnp.zeros_like(dg_ref)
            loss_ref[...] = jnp.zeros_like(loss_ref)

        dg_ref[...] += dg
        loss_ref[...] += jnp.broadcast_to(loss, loss_ref.shape)

    return pl.pallas_call(
        body, grid=(L // TL,),
        in_specs=[pl.BlockSpec((TL, D_MODEL), lambda i: (i, 0)), pl.BlockSpec((1, D_MODEL), lambda i: (0, 0)),
                  pl.BlockSpec((TL, D_MODEL), lambda i: (i, 0))],
        out_specs=[pl.BlockSpec((8, 128), lambda i: (0, 0)), pl.BlockSpec((TL, D_MODEL), lambda i: (i, 0)),
                   pl.BlockSpec((1, D_MODEL), lambda i: (0, 0))],
        out_shape=[jax.ShapeDtypeStruct((8, 128), F32), jax.ShapeDtypeStruct((L, D_MODEL), F32),
                   jax.ShapeDtypeStruct((1, D_MODEL), F32)],
        name="loss_bwd", compiler_params=_cp("arbitrary"))(x, g, tgt)


def _a_pre(val, gate):
    return val * _sigmoid(gate)


def _a_post(acc, az, cb, lng, lnb, pw, pwb):
    t = acc + cb
    mu = jnp.mean(t, axis=-1, keepdims=True)
    xc = t - mu
    ln = xc * lax.rsqrt(jnp.mean(xc * xc, axis=-1, keepdims=True) + EPS) * lng + lnb
    return (_mm(_silu(ln), pw) + pwb) * _silu(az)


def _halo_map(tl, halo, col):
    r = tl // halo
    return lambda i: (jnp.maximum(i * r - 1, 0), col)


def _a_fwd(proj, cw, cb, lng, lnb, pw, pwb):
    L = proj.shape[0]

    def body(vg_ref, az_ref, hvg_ref, cw_ref, cb_ref, lng_ref, lnb_ref, pw_ref, pwb_ref, o_ref, acc_ref):
        keep = (pl.program_id(0) > 0).astype(F32)
        a_h = _a_pre(hvg_ref[:, 0:BR], hvg_ref[:, BR:2 * BR]) * keep
        a_t = _a_pre(vg_ref[:, 0:BR], vg_ref[:, BR:2 * BR])
        ext = jnp.concatenate([a_h, a_t], axis=0)
        acc = _conv_taps(ext, cw_ref, HALO_A, K_A, TL)
        acc_ref[...] = acc
        o_ref[...] = _a_post(acc, az_ref[...], cb_ref[...], lng_ref[...], lnb_ref[...], pw_ref[...], pwb_ref[...])

    vec = pl.BlockSpec((1, BR), lambda i: (0, 0))
    tile = pl.BlockSpec((TL, BR), lambda i: (i, 0))
    return pl.pallas_call(
        body, grid=(L // TL,),
        in_specs=[pl.BlockSpec((TL, 2 * BR), lambda i: (i, 0)), pl.BlockSpec((TL, BR), lambda i: (i, 2)),
                  pl.BlockSpec((HALO_A, 2 * BR), _halo_map(TL, HALO_A, 0)),
                  pl.BlockSpec((HALO_A, BR), lambda i: (0, 0)), vec, vec, vec,
                  pl.BlockSpec((BR, BR), lambda i: (0, 0)), vec],
        out_specs=[tile, tile],
        out_shape=[jax.ShapeDtypeStruct((L, BR), F32)] * 2,
        name="a_fwd", compiler_params=_cp("parallel"))(proj, proj, proj, cw, cb, lng, lnb, pw, pwb)


def _a_bwd(proj, dmix, conv_out, cw, cb, lng, lnb, pw, pwb):
    L = proj.shape[0]
    n = L // TL

    def body(vg_ref, az_ref, hvg_ref, dy_ref, acc_ref, cw_ref, cb_ref, lng_ref, lnb_ref, pw_ref, pwb_ref,
             dp_ref, dcw_ref, dcb_ref, dlng_ref, dlnb_ref, dpw_ref, dpwb_ref, carry_ref):
        i = pl.program_id(0)

        @pl.when(i == 0)
        def _():
            carry_ref[...] = jnp.zeros_like(carry_ref)
            for r in (dcw_ref, dcb_ref, dlng_ref, dlnb_ref, dpw_ref, dpwb_ref):
                r[...] = jnp.zeros_like(r)

        keep = (i < n - 1).astype(F32)
        val, gate = vg_ref[:, 0:BR], vg_ref[:, BR:2 * BR]
        a_h = _a_pre(hvg_ref[:, 0:BR], hvg_ref[:, BR:2 * BR]) * keep
        a_t, vj_pre = jax.vjp(_a_pre, val, gate)
        ext = jnp.concatenate([a_h, a_t], axis=0)
        _, vj_post = jax.vjp(_a_post, acc_ref[...], az_ref[...], cb_ref[...], lng_ref[...], lnb_ref[...],
                             pw_ref[...], pwb_ref[...])
        dacc, daz, dcb, dlng, dlnb, dpw, dpwb = vj_post(dy_ref[...])
        dext = _conv_taps_bwd(ext, cw_ref, dcw_ref, dacc, HALO_A, K_A, TL)
        da = _add_tail(dext[HALO_A:], carry_ref[...])
        carry_ref[...] = dext[:HALO_A]
        dval, dgate = vj_pre(da)
        dp_ref[:, 0:BR] = dval
        dp_ref[:, BR:2 * BR] = dgate
        dp_ref[:, 2 * BR:3 * BR] = daz
        dcb_ref[...] += dcb
        dlng_ref[...] += dlng
        dlnb_ref[...] += dlnb
        dpw_ref[...] += dpw
        dpwb_ref[...] += dpwb

    rev = lambda i: n - 1 - i
    vec = pl.BlockSpec((1, BR), lambda i: (0, 0))
    hmap = _halo_map(TL, HALO_A, 0)
    return pl.pallas_call(
        body, grid=(n,),
        in_specs=[pl.BlockSpec((TL, 2 * BR), lambda i: (rev(i), 0)), pl.BlockSpec((TL, BR), lambda i: (rev(i), 2)),
                  pl.BlockSpec((HALO_A, 2 * BR), lambda i: hmap(rev(i))),
                  pl.BlockSpec((TL, BR), lambda i: (rev(i), 0)), pl.BlockSpec((TL, BR), lambda i: (rev(i), 0)),
                  pl.BlockSpec((HALO_A, BR), lambda i: (0, 0)), vec, vec, vec,
                  pl.BlockSpec((BR, BR), lambda i: (0, 0)), vec],
        out_specs=[pl.BlockSpec((TL, W_A), lambda i: (rev(i), 0)),
                   pl.BlockSpec((HALO_A, BR), lambda i: (0, 0)), vec, vec, vec,
                   pl.BlockSpec((BR, BR), lambda i: (0, 0)), vec],
        out_shape=[jax.ShapeDtypeStruct((L, W_A), F32), jax.ShapeDtypeStruct((HALO_A, BR), F32)]
        + [jax.ShapeDtypeStruct((1, BR), F32)] * 3
        + [jax.ShapeDtypeStruct((BR, BR), F32), jax.ShapeDtypeStruct((1, BR), F32)],
        scratch_shapes=[pltpu.VMEM((HALO_A, BR), F32)],
        name="a_bwd", compiler_params=_cp("arbitrary"))(proj, proj, proj, dmix, conv_out, cw, cb, lng, lnb, pw, pwb)


def _c_pre(cg, xc):
    return cg * xc


def _c_post(acc, bg, cz):
    return bg * acc * _silu(cz)


def _c_fwd(proj, cw):
    L = proj.shape[0]

    def body(bg_ref, cx_ref, cz_ref, hcx_ref, cw_ref, o_ref):
        keep = (pl.program_id(0) > 0).astype(F32)
        p_h = _c_pre(hcx_ref[:, 0:BR], hcx_ref[:, BR:2 * BR]) * keep
        p_t = _c_pre(cx_ref[:, 0:BR], cx_ref[:, BR:2 * BR])
        ext = jnp.concatenate([p_h, p_t], axis=0)
        acc = _conv_taps(ext, cw_ref, HALO_S, K_C, TL)
        o_ref[...] = _c_post(acc, bg_ref[...], cz_ref[...])

    return pl.pallas_call(
        body, grid=(L // TL,),
        in_specs=[pl.BlockSpec((TL, BR), lambda i: (i, 5)), pl.BlockSpec((TL, 2 * BR), lambda i: (i, 3)),
                  pl.BlockSpec((TL, BR), lambda i: (i, 8)),
                  pl.BlockSpec((HALO_S, 2 * BR), _halo_map(TL, HALO_S, 3)),
                  pl.BlockSpec((HALO_S, BR), lambda i: (0, 0))],
        out_specs=pl.BlockSpec((TL, BR), lambda i: (i, 0)),
        out_shape=jax.ShapeDtypeStruct((L, BR), F32),
        name="c_fwd", compiler_params=_cp("parallel"))(proj, proj, proj, proj, cw)


def _c_bwd(proj, dmix, cw):
    L = proj.shape[0]
    n = L // TL

    def body(bg_ref, cx_ref, cz_ref, hcx_ref, dy_ref, cw_ref, dp_ref, dcw_ref, carry_ref):
        i = pl.program_id(0)

        @pl.when(i == 0)
        def _():
            carry_ref[...] = jnp.zeros_like(carry_ref)
            dcw_ref[...] = jnp.zeros_like(dcw_ref)

        keep = (i < n - 1).astype(F32)
        p_h = _c_pre(hcx_ref[:, 0:BR], hcx_ref[:, BR:2 * BR]) * keep
        p_t, vj_pre = jax.vjp(_c_pre, cx_ref[:, 0:BR], cx_ref[:, BR:2 * BR])
        ext = jnp.concatenate([p_h, p_t], axis=0)
        acc = _conv_taps(ext, cw_ref, HALO_S, K_C, TL)
        _, vj_post = jax.vjp(_c_post, acc, bg_ref[...], cz_ref[...])
        dacc, dbg, dcz = vj_post(dy_ref[...])
        dext = _conv_taps_bwd(ext, cw_ref, dcw_ref, dacc, HALO_S, K_C, TL)
        dp = _add_tail(dext[HALO_S:], carry_ref[...])
        carry_ref[...] = dext[:HALO_S]
        dcg, dxc = vj_pre(dp)
        dp_ref[:, 0:BR] = dbg
        dp_ref[:, BR:2 * BR] = dcg
        dp_ref[:, 2 * BR:3 * BR] = dxc
        dp_ref[:, 3 * BR:4 * BR] = dcz

    rev = lambda i: n - 1 - i
    hmap = _halo_map(TL, HALO_S, 3)
    return pl.pallas_call(
        body, grid=(n,),
        in_specs=[pl.BlockSpec((TL, BR), lambda i: (rev(i), 5)), pl.BlockSpec((TL, 2 * BR), lambda i: (rev(i), 3)),
                  pl.BlockSpec((TL, BR), lambda i: (rev(i), 8)),
                  pl.BlockSpec((HALO_S, 2 * BR), lambda i: hmap(rev(i))),
                  pl.BlockSpec((TL, BR), lambda i: (rev(i), 2)),
                  pl.BlockSpec((HALO_S, BR), lambda i: (0, 0))],
        out_specs=[pl.BlockSpec((TL, W_C), lambda i: (rev(i), 0)), pl.BlockSpec((HALO_S, BR), lambda i: (0, 0))],
        out_shape=[jax.ShapeDtypeStruct((L, W_C), F32), jax.ShapeDtypeStruct((HALO_S, BR), F32)],
        scratch_shapes=[pltpu.VMEM((HALO_S, BR), F32)],
        name="c_bwd", compiler_params=_cp("arbitrary"))(proj, proj, proj, proj, dmix, cw)


def _s5_prep_fn(lre, lim, ldt, bre, bim, cre, cim):
    grp = lax.broadcasted_iota(jnp.int32, (128, NS), 0)
    lane = lax.broadcasted_iota(jnp.int32, (128, NS), 1)
    expand = (grp == lane // S5_P).astype(F32)
    dt = jnp.exp(_dot_hi(jnp.broadcast_to(ldt, (8, 128)), expand)[0:1])
    lr = jnp.minimum(lre, -1e-4)
    mag = jnp.exp(lr * dt)
    ar = mag * jnp.cos(lim * dt)
    ai = mag * jnp.sin(lim * dt)
    den = lr * lr + lim * lim
    fr = ((ar - 1.0) * lr + ai * lim) / den
    fi = (ai * lr - (ar - 1.0) * lim) / den
    bbr = fr * bre - fi * bim
    bbi = fr * bim + fi * bre
    row = lax.broadcasted_iota(jnp.int32, (BR, NS), 0)
    col = lax.broadcasted_iota(jnp.int32, (BR, NS), 1)
    blk = (row // S5_H == col // S5_P).astype(F32)

    def embed(t):
        return jnp.concatenate([t] * S5_G, axis=0) * blk

    bemb = jnp.concatenate([embed(bbr), embed(bbi)], axis=1)
    cemb = jnp.concatenate([embed(cre), embed(-cim)], axis=1)
    return ar, ai, bemb, cemb


def _s5_prep(lre, lim, ldt, bre, bim, cre, cim):
    def body(*refs):
        outs = _s5_prep_fn(*[r[...] for r in refs[:7]])
        for r, o in zip(refs[7:], outs):
            r[...] = o

    return pl.pallas_call(
        body,
        out_shape=[jax.ShapeDtypeStruct((1, NS), F32)] * 2 + [jax.ShapeDtypeStruct((BR, 2 * NS), F32)] * 2,
        name="s5_prep", compiler_params=pltpu.CompilerParams(vmem_limit_bytes=VMEM_LIMIT),
    )(lre, lim, ldt, bre, bim, cre, cim)


def _s5_prep_bwd(lre, lim, ldt, bre, bim, cre, cim, dar, dai, dbemb, dcemb):
    def body(*refs):
        _, vj = jax.vjp(_s5_prep_fn, *[r[...] for r in refs[:7]])
        grads = vj(tuple(r[...] for r in refs[7:11]))
        for r, o in zip(refs[11:], grads):
            r[...] = o

    return pl.pallas_call(
        body,
        out_shape=[jax.ShapeDtypeStruct((1, NS), F32)] * 2 + [jax.ShapeDtypeStruct((1, 128), F32)]
        + [jax.ShapeDtypeStruct((S5_H, NS), F32)] * 4,
        name="s5_prep_bwd", compiler_params=pltpu.CompilerParams(vmem_limit_bytes=VMEM_LIMIT),
    )(lre, lim, ldt, bre, bim, cre, cim, dar, dai, dbemb, dcemb)


S5_SUB = 32


def _cmac(xr, xi, pr, pi, sr, si):
    return xr + pr * sr - pi * si, xi + pr * si + pi * sr


def _s5_scan(xr, xi, ar, ai, reverse):
    n, wd = xr.shape
    m = S5_SUB
    nsub = n // m
    rsub = lax.broadcasted_iota(jnp.int32, (n, 1), 0) % m
    pr, pi = ar, ai
    power = {}
    d = 1
    while d < m:
        power[d] = (pr, pi)
        if d % 8:
            if reverse:
                msk = rsub < m - d
                sr, si = jnp.where(msk, _roll(xr, n - d), 0.0), jnp.where(msk, _roll(xi, n - d), 0.0)
            else:
                msk = rsub >= d
                sr, si = jnp.where(msk, _roll(xr, d), 0.0), jnp.where(msk, _roll(xi, d), 0.0)
            xr, xi = _cmac(xr, xi, pr, pi, sr, si)
        else:
            x3r, x3i = xr.reshape(nsub, m, wd), xi.reshape(nsub, m, wd)
            if reverse:
                ur, ui = _cmac(x3r[:, :m - d], x3i[:, :m - d], pr, pi, x3r[:, d:], x3i[:, d:])
                x3r = jnp.concatenate([ur, x3r[:, m - d:]], axis=1)
                x3i = jnp.concatenate([ui, x3i[:, m - d:]], axis=1)
            else:
                ur, ui = _cmac(x3r[:, d:], x3i[:, d:], pr, pi, x3r[:, :m - d], x3i[:, :m - d])
                x3r = jnp.concatenate([x3r[:, :d], ur], axis=1)
                x3i = jnp.concatenate([x3i[:, :d], ui], axis=1)
            xr, xi = x3r.reshape(n, wd), x3i.reshape(n, wd)
        pr, pi = pr * pr - pi * pi, 2.0 * pr * pi
        d *= 2
    r8 = lax.broadcasted_iota(jnp.int32, (8, 1), 0)
    qr, qi = ar, ai
    tr, ti = jnp.zeros((8, wd), F32), jnp.zeros((8, wd), F32)
    for e in range(1, 9):
        sel = r8 == (8 - e if reverse else e - 1)
        tr, ti = jnp.where(sel, qr, tr), jnp.where(sel, qi, ti)
        qr, qi = qr * ar - qi * ai, qr * ai + qi * ar
    size = 8
    while size < m:
        er, ei = power[size]
        hr, hi = tr * er - ti * ei, tr * ei + ti * er
        if reverse:
            tr, ti = jnp.concatenate([hr, tr], axis=0), jnp.concatenate([hi, ti], axis=0)
        else:
            tr, ti = jnp.concatenate([tr, hr], axis=0), jnp.concatenate([ti, hi], axis=0)
        size *= 2
    order = list(reversed(range(nsub))) if reverse else list(range(nsub))
    edge = 0 if reverse else m - 1
    done_r, done_i = {}, {}
    cr = ci = None
    for j in order:
        br, bi = xr[j * m:(j + 1) * m], xi[j * m:(j + 1) * m]
        if cr is not None:
            br, bi = _cmac(br, bi, tr, ti, cr, ci)
        done_r[j], done_i[j] = br, bi
        cr, ci = br[edge:edge + 1], bi[edge:edge + 1]
    return (jnp.concatenate([done_r[j] for j in range(nsub)], axis=0),
            jnp.concatenate([done_i[j] for j in range(nsub)], axis=0))


def _s5_states(u, bemb_b, ar, ai, sin_r, sin_i):
    bu = jnp.dot(u.astype(BF), bemb_b, preferred_element_type=F32)
    first = lax.broadcasted_iota(jnp.int32, (u.shape[0], 1), 0) == 0
    xr = bu[:, :NS] + jnp.where(first, ar * sin_r - ai * sin_i, 0.0)
    xi = bu[:, NS:] + jnp.where(first, ar * sin_i + ai * sin_r, 0.0)
    return _s5_scan(xr, xi, ar, ai, False)


def _b_post(yssm, u, bz, dsk, gw, gb):
    z = jax.nn.gelu(yssm + dsk * u)
    return z * _sigmoid(_mm(z, gw) + gb) * _silu(bz)


def _b_fwd(proj, ar, ai, bemb, cemb, dsk, gw, gb, exchange=None):
    L = proj.shape[0]
    n = L // TL

    def body(u_ref, bz_ref, ar_ref, ai_ref, be_ref, ce_ref, dsk_ref, gw_ref, gb_ref, o_ref, sin_ref, st_ref,
             carry_ref):
        @pl.when(pl.program_id(0) == 0)
        def _():
            carry_ref[...] = jnp.zeros_like(carry_ref)

        sin = carry_ref[...]
        sin_ref[0] = sin
        u = u_ref[...]
        sr, si = _s5_states(u, be_ref[...].astype(BF), ar_ref[...], ai_ref[...], sin[:, :NS], sin[:, NS:])
        carry_ref[:, :NS] = sr[TL - 1:TL]
        carry_ref[:, NS:] = si[TL - 1:TL]
        st_ref[:, :NS] = sr
        st_ref[:, NS:] = si
        s = jnp.concatenate([sr, si], axis=1).astype(BF)
        yssm = lax.dot_general(s, ce_ref[...].astype(BF), (((1,), (1,)), ((), ())), preferred_element_type=F32)
        o_ref[...] = _b_post(yssm, u, bz_ref[...], dsk_ref[...], gw_ref[...], gb_ref[...])

    vec = pl.BlockSpec((1, BR), lambda i: (0, 0))
    svec = pl.BlockSpec((1, NS), lambda i: (0, 0))
    emb = pl.BlockSpec((BR, 2 * NS), lambda i: (0, 0))
    return _sweep_with_exchange(
        body, n,
        in_specs=[pl.BlockSpec((TL, BR), lambda i: (i, 3)), pl.BlockSpec((TL, BR), lambda i: (i, 4)),
                  svec, svec, emb, emb, vec, pl.BlockSpec((BR, BR), lambda i: (0, 0)), vec],
        out_specs=[pl.BlockSpec((TL, BR), lambda i: (i, 0)), pl.BlockSpec((1, 1, 2 * NS), lambda i: (i, 0, 0)),
                   pl.BlockSpec((TL, 2 * NS), lambda i: (i, 0))],
        out_shape=[jax.ShapeDtypeStruct((L, BR), F32), jax.ShapeDtypeStruct((n, 1, 2 * NS), F32),
                   jax.ShapeDtypeStruct((L, 2 * NS), F32)],
        scratch_shapes=[pltpu.VMEM((1, 2 * NS), F32)],
        args=(proj, proj, ar, ai, bemb, cemb, dsk, gw, gb), exchange=exchange, name="b_fwd")


def _b_bwd(proj, dmix, sin_all, states, ar, ai, bemb, cemb, dsk, gw, gb, exchange=None):
    L = proj.shape[0]
    n = L // TL

    def body(u_ref, bz_ref, dy_ref, sin_ref, st_ref, ar_ref, ai_ref, be_ref, ce_ref, dsk_ref, gw_ref, gb_ref,
             dp_ref, dar_ref, dai_ref, dbe_ref, dce_ref, ddsk_ref, dgw_ref, dgb_ref, carry_ref):
        i = pl.program_id(0)

        @pl.when(i == 0)
        def _():
            carry_ref[...] = jnp.zeros_like(carry_ref)
            for r in (dar_ref, dai_ref, dbe_ref, dce_ref, ddsk_ref, dgw_ref, dgb_ref):
                r[...] = jnp.zeros_like(r)

        u = u_ref[...]
        ar, ai = ar_ref[...], ai_ref[...]
        be_b, ce_b = be_ref[...].astype(BF), ce_ref[...].astype(BF)
        sin = sin_ref[0]
        sr, si = st_ref[:, :NS], st_ref[:, NS:]
        s_b = st_ref[...].astype(BF)
        yssm = lax.dot_general(s_b, ce_b, (((1,), (1,)), ((), ())), preferred_element_type=F32)
        _, vj = jax.vjp(_b_post, yssm, u, bz_ref[...], dsk_ref[...], gw_ref[...], gb_ref[...])
        dyssm, du, dbz, ddsk, dgw, dgb = vj(dy_ref[...])
        dy_b = dyssm.astype(BF)
        dce_ref[...] += lax.dot_general(dy_b, s_b, (((0,), (0,)), ((), ())), preferred_element_type=F32)
        gs = jnp.dot(dy_b, ce_b, preferred_element_type=F32)
        last = lax.broadcasted_iota(jnp.int32, (TL, 1), 0) == TL - 1
        cr, ci = carry_ref[:, :NS], carry_ref[:, NS:]
        gr = gs[:, :NS] + jnp.where(last, ar * cr + ai * ci, 0.0)
        gi = gs[:, NS:] + jnp.where(last, ar * ci - ai * cr, 0.0)
        dsr, dsi = _s5_scan(gr, gi, ar, -ai, True)
        carry_ref[:, :NS] = dsr[0:1]
        carry_ref[:, NS:] = dsi[0:1]
        first = lax.broadcasted_iota(jnp.int32, (TL, 1), 0) == 0
        pr = jnp.where(first, sin[:, :NS], _roll(sr, 1))
        pi = jnp.where(first, sin[:, NS:], _roll(si, 1))
        dar_ref[...] += jnp.sum(dsr * pr + dsi * pi, axis=0, keepdims=True)
        dai_ref[...] += jnp.sum(dsi * pr - dsr * pi, axis=0, keepdims=True)
        ds_b = jnp.concatenate([dsr, dsi], axis=1).astype(BF)
        dbe_ref[...] += lax.dot_general(u.astype(BF), ds_b, (((0,), (0,)), ((), ())), preferred_element_type=F32)
        du = du + lax.dot_general(ds_b, be_b, (((1,), (1,)), ((), ())), preferred_element_type=F32)
        dp_ref[:, 0:BR] = du
        dp_ref[:, BR:2 * BR] = dbz
        ddsk_ref[...] += ddsk
        dgw_ref[...] += dgw
        dgb_ref[...] += dgb

    rev = lambda i: n - 1 - i
    vec = pl.BlockSpec((1, BR), lambda i: (0, 0))
    svec = pl.BlockSpec((1, NS), lambda i: (0, 0))
    emb = pl.BlockSpec((BR, 2 * NS), lambda i: (0, 0))
    mat = pl.BlockSpec((BR, BR), lambda i: (0, 0))
    return _sweep_with_exchange(
        body, n,
        in_specs=[pl.BlockSpec((TL, BR), lambda i: (rev(i), 3)), pl.BlockSpec((TL, BR), lambda i: (rev(i), 4)),
                  pl.BlockSpec((TL, BR), lambda i: (rev(i), 1)),
                  pl.BlockSpec((1, 1, 2 * NS), lambda i: (rev(i), 0, 0)),
                  pl.BlockSpec((TL, 2 * NS), lambda i: (rev(i), 0)),
                  svec, svec, emb, emb, vec, mat, vec],
        out_specs=[pl.BlockSpec((TL, W_B), lambda i: (rev(i), 0)), svec, svec, emb, emb, vec, mat, vec],
        out_shape=[jax.ShapeDtypeStruct((L, W_B), F32)] + [jax.ShapeDtypeStruct((1, NS), F32)] * 2
        + [jax.ShapeDtypeStruct((BR, 2 * NS), F32)] * 2
        + [jax.ShapeDtypeStruct((1, BR), F32), jax.ShapeDtypeStruct((BR, BR), F32), jax.ShapeDtypeStruct((1, BR), F32)],
        scratch_shapes=[pltpu.VMEM((1, 2 * NS), F32)],
        args=(proj, proj, dmix, sin_all, states, ar, ai, bemb, cemb, dsk, gw, gb), exchange=exchange, name="b_bwd")


def _half_masks(rows):
    lane = lax.broadcasted_iota(jnp.int32, (rows, 2 * DN_D), 1)
    return lane < DN_D, lane >= DN_D


def _bd(x):
    left, right = _half_masks(x.shape[0])
    return jnp.concatenate([jnp.where(left, x, 0.0), jnp.where(right, x, 0.0)], axis=0)


@jax.custom_vjp
def _segsum(x):
    r = lax.broadcasted_iota(jnp.int32, (2 * DN_D, 2 * DN_D), 0) // DN_D
    c = lax.broadcasted_iota(jnp.int32, (2 * DN_D, 2 * DN_D), 1) // DN_D
    ones = (r == c).astype(BF)
    hi, lo = _split(x)
    return jnp.dot(hi, ones, preferred_element_type=F32) + jnp.dot(lo, ones, preferred_element_type=F32)


_segsum.defvjp(lambda x: (_segsum(x), None), lambda _, g: (_segsum(g),))


def _pair_t(x):
    t = _bd(x).T
    return t[:DN_D] + t[DN_D:]


@jax.custom_vjp
def _pair_inv(lms):
    n = DN_D
    row = lax.broadcasted_iota(jnp.int32, (n, 2 * n), 0)
    col = lax.broadcasted_iota(jnp.int32, (n, 2 * n), 1) % n
    eye = (row == col).astype(F32)
    accs = [eye - lm for lm in lms]
    pws = list(lms)
    k = 2
    while k < n:
        pws = [_dot3(p, _bd(p)) for p in pws]
        accs = [a + _dot3(a, _bd(p)) for a, p in zip(accs, pws)]
        k *= 2
    return tuple(accs)


def _pi_b(a, g):
    ats = [_pair_t(x) for x in a]
    tmp = [_dot3(at, _bd(gi)) for at, gi in zip(ats, g)]
    return (tuple(-_dot3(t, _bd(at)) for t, at in zip(tmp, ats)),)


def _pi_f(lms):
    a = _pair_inv(lms)
    return a, a


_pair_inv.defvjp(_pi_f, _pi_b)


@jax.custom_vjp
def _pair_known_inverse(lms, inv):
    return inv


_pair_known_inverse.defvjp(lambda lms, inv: (inv, inv),
                           lambda a, g: (_pi_b(a, g)[0], tuple(jnp.zeros_like(x) for x in a)))


def _d_tile(cq, ab, dz, sb0, sb1, p1, p2, ng, known=None):
    c = DN_C
    chunks = range(cq.shape[0] // c)
    units = [(g, p) for g in chunks for p in range(DN_P)]
    n = range(len(units))
    qkv = _silu(cq)
    gall = -jnp.exp(p1) * jax.nn.softplus(ab + p2)
    ball = _sigmoid(ab)
    left, _ = _half_masks(c)
    row = lax.broadcasted_iota(jnp.int32, (c, 2 * c), 0)
    col = lax.broadcasted_iota(jnp.int32, (c, 2 * c), 1) % c
    causal, strict = row >= col, row > col
    sq = lax.broadcasted_iota(jnp.int32, (c, c), 0) >= lax.broadcasted_iota(jnp.int32, (c, c), 1)
    gc_all = [_dot_hi(sq.astype(F32), gall[g * c:(g + 1) * c]) for g in chunks]
    gc_t = [t.T for t in gc_all]
    bdm = (lax.broadcasted_iota(jnp.int32, (2 * c, 2 * c), 0) // c
           == lax.broadcasted_iota(jnp.int32, (2 * c, 2 * c), 1) // c).astype(F32)

    def two(t, base, g, p):
        return t[g * c:(g + 1) * c, base + 2 * p * DN_D:base + 2 * (p + 1) * DN_D]

    def per_head(t, off, p):
        return jnp.where(left, t[:, off + 2 * p:off + 2 * p + 1], t[:, off + 2 * p + 1:off + 2 * p + 2])

    q = [two(qkv, 0, g, p) for g, p in units]
    k = [two(qkv, BR, g, p) for g, p in units]
    v = [two(qkv, 2 * BR, g, p) for g, p in units]
    q = [t * lax.rsqrt(_segsum(t * t) + EPS) * (DN_D ** -0.5) for t in q]
    k = [t * lax.rsqrt(_segsum(t * t) + EPS) for t in k]
    g2 = [per_head(gc_all[g], 0, p) for g, p in units]
    beta = [per_head(ball[g * c:(g + 1) * c], DN_H, p) for g, p in units]
    grow = [jnp.concatenate([gc_t[g][2 * p:2 * p + 1, :], gc_t[g][2 * p + 1:2 * p + 2, :]], axis=1) for g, p in units]
    decay = [jnp.where(causal, jnp.exp(jnp.where(causal, g2[u] - grow[u], 0.0)), 0.0) for u in n]
    kb = [k[u] * beta[u] for u in n]
    kbd = [_bd(t) for t in k]
    lm = [jnp.where(strict, _mm_nt(kb[u], kbd[u]) * decay[u], 0.0) for u in n]
    ainv = _pair_inv(tuple(lm)) if known is None else _pair_known_inverse(tuple(lm), known)
    egc = [jnp.exp(t) for t in g2]
    uw = [_mm3(ainv[u], jnp.concatenate([_bd(v[u] * beta[u]), _bd(kb[u] * egc[u])], axis=1)) for u in n]
    attn = [_mm_nt(q[u], kbd[u]) * decay[u] for u in n]
    glast = [t[c - 1:c, :] for t in g2]
    kd = [k[u] * jnp.exp(glast[u] - g2[u]) for u in n]
    qd = [q[u] * egc[u] for u in n]
    ng2 = jnp.concatenate([ng, ng], axis=1)
    sbd, starts, outs = [sb0, sb1], [], []
    for g in chunks:
        starts.append(tuple(sbd))
        us = [g * DN_P + p for p in range(DN_P)]
        vnew = [uw[u][:, :2 * DN_D] - _mm(uw[u][:, 2 * DN_D:], sbd[p]) for p, u in enumerate(us)]
        o = [_mm(qd[u], sbd[p]) + _mm(attn[u], _bd(vnew[p])) for p, u in enumerate(us)]
        sbd = [sbd[p] * jnp.exp(glast[u]) + _mm_tn(kd[u], vnew[p]) * bdm for p, u in enumerate(us)]
        outs.append(jnp.concatenate([t * lax.rsqrt(_segsum(t * t) * (1.0 / DN_D) + EPS) * ng2 for t in o], axis=1))
    yd = jnp.concatenate(outs, axis=0) * _silu(dz)
    return (yd, *sbd), (starts, ainv)


def _d_fwd(proj, cw, p1, p2, ng, exchange=None):
    L = proj.shape[0]
    DN_G, DN_T = DN_G_FWD, DN_G_FWD * DN_C
    n = L // DN_T

    def body(qkv_ref, ab_ref, dz_ref, hq_ref, cw_ref, p1_ref, p2_ref, ng_ref, o_ref, sall_ref, inv_ref, s_ref):
        i = pl.program_id(0)

        @pl.when(i == 0)
        def _():
            s_ref[...] = jnp.zeros_like(s_ref)

        keep = (i > 0).astype(F32)
        ext = jnp.concatenate([hq_ref[...] * keep, qkv_ref[...]], axis=0)
        cq = _conv_taps(ext, cw_ref, HALO_S, K_DN, DN_T)
        out, (starts, ainv) = _d_tile(cq, ab_ref[...], dz_ref[...], s_ref[0], s_ref[1], p1_ref[...], p2_ref[...],
                                      ng_ref[...])
        o_ref[...] = out[0]
        for p in range(DN_P):
            s_ref[p] = out[1 + p]
            for g in range(DN_G):
                sall_ref[g, p] = starts[g][p]
                inv_ref[g, p] = ainv[g * DN_P + p]

    return _sweep_with_exchange(
        body, n,
        in_specs=[pl.BlockSpec((DN_T, 3 * BR), lambda i: (i, 3)), pl.BlockSpec((DN_T, 128), lambda i: (i, 26)),
                  pl.BlockSpec((DN_T, BR), lambda i: (i, 12)),
                  pl.BlockSpec((HALO_S, 3 * BR), _halo_map(DN_T, HALO_S, 3)),
                  pl.BlockSpec((HALO_S, 3 * BR), lambda i: (0, 0)),
                  pl.BlockSpec((1, 128), lambda i: (0, 0)), pl.BlockSpec((1, 128), lambda i: (0, 0)),
                  pl.BlockSpec((1, DN_D), lambda i: (0, 0))],
        out_specs=[pl.BlockSpec((DN_T, BR), lambda i: (i, 0)),
                   pl.BlockSpec((DN_G,) + DN_STATE, lambda i: (i, 0, 0, 0)),
                   pl.BlockSpec((DN_G,) + DN_INV, lambda i: (i, 0, 0, 0))],
        out_shape=[jax.ShapeDtypeStruct((L, BR), F32), jax.ShapeDtypeStruct((L // DN_C,) + DN_STATE, F32),
                   jax.ShapeDtypeStruct((L // DN_C,) + DN_INV, F32)],
        scratch_shapes=[pltpu.VMEM(DN_STATE, F32)],
        args=(proj, proj, proj, proj, cw, p1, p2, ng), exchange=exchange, name="d_fwd")


def _sweep_with_exchange(body, steps, in_specs, out_specs, out_shape, scratch_shapes, args, exchange, name):
    if exchange is None:
        res = pl.pallas_call(body, grid=(steps,), in_specs=in_specs, out_specs=out_specs, out_shape=out_shape,
                             scratch_shapes=scratch_shapes, name=name, compiler_params=_cp("arbitrary"))(*args)
        return res, None
    several = isinstance(exchange, list)
    exs = [_Exchange(*e) for e in (exchange if several else [exchange])]
    xs = [x for e in (exchange if several else [exchange]) for x in e[0]]
    ni, no, ns, na = len(in_specs), len(out_specs), len(scratch_shapes), len(xs)

    def carried(*refs):
        ins, xin = refs[:ni], refs[ni:ni + na]
        outs, xout = refs[ni + na:ni + na + no], refs[ni + na + no:ni + 2 * na + no]
        scr, sems = refs[ni + 2 * na + no:ni + 2 * na + no + ns], refs[ni + 2 * na + no + ns:]

        def each(fn_name):
            off = 0
            for j, ex in enumerate(exs):
                getattr(ex, fn_name)(xin[off:off + ex.na], xout[off:off + ex.na], sems[3 * j:3 * j + 3])
                off += ex.na

        @pl.when(pl.program_id(0) == 0)
        def _():
            each("start")

        body(*ins, *outs, *scr)

        @pl.when(pl.program_id(0) == steps - 1)
        def _():
            each("wait")

    res = pl.pallas_call(carried, grid=(steps,), in_specs=list(in_specs) + [s for ex in exs for s in ex.in_specs],
                         out_specs=list(out_specs) + [s for ex in exs for s in ex.out_specs],
                         out_shape=list(out_shape) + [s for ex in exs for s in ex.out_shape],
                         scratch_shapes=list(scratch_shapes) + [s for ex in exs for s in ex.scratch_shapes],
                         name=name + "_x", compiler_params=_cp("arbitrary"))(*args, *xs)
    got, off = [], no
    for ex in exs:
        got.append(res[off:off + ex.na])
        off += ex.na
    return res[:no], (got if several else got[0])


def _d_bwd(proj, dmix, sall, inv, cw, p1, p2, ng, exchange=None):
    L = proj.shape[0]
    DN_G, DN_T = DN_G_BWD, DN_G_BWD * DN_C
    n = L // DN_T

    def body(qkv_ref, ab_ref, dz_ref, hq_ref, dy_ref, sall_ref, inv_ref, cw_ref, p1_ref, p2_ref, ng_ref,
             dp_ref, dcw_ref, dp1_ref, dp2_ref, dng_ref, ds_ref, carry_ref):
        i = pl.program_id(0)

        @pl.when(i == 0)
        def _():
            ds_ref[...] = jnp.zeros_like(ds_ref)
            carry_ref[...] = jnp.zeros_like(carry_ref)
            for r in (dcw_ref, dp1_ref, dp2_ref, dng_ref):
                r[...] = jnp.zeros_like(r)

        keep = (i < n - 1).astype(F32)
        ext = jnp.concatenate([hq_ref[...] * keep, qkv_ref[...]], axis=0)
        cq = _conv_taps(ext, cw_ref, HALO_S, K_DN, DN_T)
        known = tuple(inv_ref[g, p] for g in range(DN_G) for p in range(DN_P))
        _, vj = jax.vjp(lambda *a: _d_tile(*a, known=known)[0], cq, ab_ref[...], dz_ref[...], sall_ref[0, 0],
                        sall_ref[0, 1], p1_ref[...], p2_ref[...], ng_ref[...])
        dcq, dab, ddz, ds0, ds1, dp1, dp2, dng = vj((dy_ref[...], ds_ref[0], ds_ref[1]))
        dp_ref[:, 3 * BR:4 * BR] = ddz
        dp_ref[:, 4 * BR:4 * BR + 128] = dab
        ds_ref[0] = ds0
        ds_ref[1] = ds1
        dp1_ref[...] += dp1
        dp2_ref[...] += dp2
        dng_ref[...] += dng
        dext = _conv_taps_bwd(ext, cw_ref, dcw_ref, dcq, HALO_S, K_DN, DN_T)
        dp_ref[:, 0:3 * BR] = _add_tail(dext[HALO_S:], carry_ref[...])
        carry_ref[...] = dext[:HALO_S]

    rev = lambda i: n - 1 - i
    hmap = _halo_map(DN_T, HALO_S, 3)
    v128 = pl.BlockSpec((1, 128), lambda i: (0, 0))
    return _sweep_with_exchange(
        body, n,
        in_specs=[pl.BlockSpec((DN_T, 3 * BR), lambda i: (rev(i), 3)), pl.BlockSpec((DN_T, 128), lambda i: (rev(i), 26)),
                  pl.BlockSpec((DN_T, BR), lambda i: (rev(i), 12)),
                  pl.BlockSpec((HALO_S, 3 * BR), lambda i: hmap(rev(i))),
                  pl.BlockSpec((DN_T, BR), lambda i: (rev(i), 3)),
                  pl.BlockSpec((DN_G,) + DN_STATE, lambda i: (rev(i), 0, 0, 0)),
                  pl.BlockSpec((DN_G,) + DN_INV, lambda i: (rev(i), 0, 0, 0)),
                  pl.BlockSpec((HALO_S, 3 * BR), lambda i: (0, 0)), v128, v128,
                  pl.BlockSpec((1, DN_D), lambda i: (0, 0))],
        out_specs=[pl.BlockSpec((DN_T, W_D), lambda i: (rev(i), 0)),
                   pl.BlockSpec((HALO_S, 3 * BR), lambda i: (0, 0)), v128, v128,
                   pl.BlockSpec((1, DN_D), lambda i: (0, 0))],
        out_shape=[jax.ShapeDtypeStruct((L, W_D), F32), jax.ShapeDtypeStruct((HALO_S, 3 * BR), F32),
                   jax.ShapeDtypeStruct((1, 128), F32), jax.ShapeDtypeStruct((1, 128), F32),
                   jax.ShapeDtypeStruct((1, DN_D), F32)],
        scratch_shapes=[pltpu.VMEM(DN_STATE, F32), pltpu.VMEM((HALO_S, 3 * BR), F32)],
        args=(proj, proj, proj, proj, dmix, sall, inv, cw, p1, p2, ng), exchange=exchange, name="d_bwd")


def _pick_rows(rows, cap):
    best = 8
    for t in range(8, cap + 1, 8):
        if rows % t == 0:
            best = t
    return best


def _adamw(w, g, m, v, name, lead=None):
    rows, rest = w.shape[0], w.shape[1:]
    tr = _pick_rows(rows, 512) if lead is None else lead
    c1 = 1.0 - ADAM_B1 ** ADAM_STEP
    c2 = 1.0 - ADAM_B2 ** ADAM_STEP

    def body(w_ref, g_ref, m_ref, v_ref, d_ref, mo_ref, vo_ref):
        gv = g_ref[...]
        mn = ADAM_B1 * m_ref[...] + (1.0 - ADAM_B1) * gv
        vn = ADAM_B2 * v_ref[...] + (1.0 - ADAM_B2) * (gv * gv)
        d_ref[...] = -ADAM_LR * ((mn / c1) / (jnp.sqrt(vn / c2) + ADAM_EPS) + ADAM_WD * w_ref[...])
        mo_ref[...] = mn
        vo_ref[...] = vn

    spec = pl.BlockSpec((tr,) + rest, lambda i: (i,) + (0,) * len(rest))
    return pl.pallas_call(
        body, grid=(rows // tr,), in_specs=[spec] * 4, out_specs=[spec] * 3,
        out_shape=[jax.ShapeDtypeStruct(w.shape, F32)] * 3,
        name=name, compiler_params=_cp("parallel"))(w, g, m, v)


def _sum_slots(r, name):
    n, rows, wd = r.shape
    tr = _pick_rows(rows, 384)

    def body(r_ref, o_ref):
        acc = r_ref[0].astype(F32)
        for j in range(1, n):
            acc = acc + r_ref[j].astype(F32)
        o_ref[...] = acc

    return pl.pallas_call(
        body, grid=(rows // tr,),
        in_specs=[pl.BlockSpec((n, tr, wd), lambda i: (0, i, 0))],
        out_specs=pl.BlockSpec((tr, wd), lambda i: (i, 0)),
        out_shape=jax.ShapeDtypeStruct((rows, wd), F32),
        name=name, compiler_params=_cp("parallel"))(r)


AXES = ("x", "y", "c")


def _group_peer(axes, k):
    pos = {a: lax.axis_index(a) for a in AXES}
    idx = 0
    for a in axes:
        idx = idx * 2 + pos[a]
    peer = dict(pos)
    for b, a in enumerate(reversed(axes)):
        if (k >> b) & 1:
            peer[a] = 1 - pos[a]
    return idx, tuple(peer[a] for a in AXES)


MAX_CHUNKS = 4


class _Exchange:
    def __init__(self, xs, axes, mode):
        self.axes, self.mode, self.na, self.n = axes, mode, len(xs), 2 ** len(axes)
        n = self.n
        self.out_shape, self.pieces = [], []
        for x in xs:
            if mode == "gather":
                shape, lead = (n,) + x.shape, x.shape[0]
            elif mode == "scatter":
                shape, lead = x.shape, x.shape[1]
            else:
                shape, lead = (x.shape[0], n * x.shape[1], x.shape[2]), x.shape[0]
            self.out_shape.append(jax.ShapeDtypeStruct(shape, x.dtype))
            big = x.size * x.dtype.itemsize >= (1 << 20)
            if mode == "rows":
                self.pieces.append(lead if lead <= MAX_CHUNKS else 1)
            else:
                self.pieces.append(MAX_CHUNKS if big and lead % (16 * MAX_CHUNKS) == 0 else 1)
        self.in_specs = [pl.BlockSpec(memory_space=pl.ANY)] * self.na
        self.out_specs = [pl.BlockSpec(memory_space=pl.ANY)] * self.na
        self.scratch_shapes = [pltpu.SemaphoreType.DMA((self.na, MAX_CHUNKS, n)),
                               pltpu.SemaphoreType.DMA((self.na, MAX_CHUNKS, n)),
                               pltpu.SemaphoreType.DMA((self.na, MAX_CHUNKS))]

    def _copies(self, x_refs, o_refs, send_sems, recv_sems, local_sems):
        me, _ = _group_peer(self.axes, 0)
        local, remote = [], []
        for a, (x, o) in enumerate(zip(x_refs, o_refs)):
            for c in range(self.pieces[a]):
                if self.mode == "rows":
                    r = x.shape[1]
                    b = slice(None) if self.pieces[a] == 1 else pl.ds(c, 1)
                    src = lambda k, x=x, b=b: x.at[b]
                    dst = o.at[b, pl.ds(me * r, r)]
                else:
                    lead = x.shape[1] if self.mode == "scatter" else x.shape[0]
                    rs = pl.ds(c * (lead // self.pieces[a]), lead // self.pieces[a])
                    if self.mode == "scatter":
                        src = lambda k, x=x, rs=rs: x.at[me ^ k, rs]
                    else:
                        src = lambda k, x=x, rs=rs: x.at[rs]
                    dst = o.at[me, rs]
                local.append(pltpu.make_async_copy(src(0), dst, local_sems.at[a, c]))
                for k in range(1, self.n):
                    remote.append(pltpu.make_async_remote_copy(
                        src_ref=src(k), dst_ref=dst, send_sem=send_sems.at[a, c, k], recv_sem=recv_sems.at[a, c, k],
                        device_id=_group_peer(self.axes, k)[1], device_id_type=MESH))
        return local, remote

    def start(self, x_refs, o_refs, sems):
        local, remote = self._copies(x_refs, o_refs, *sems)
        for cp in local + remote:
            cp.start()

    def wait(self, x_refs, o_refs, sems):
        local, remote = self._copies(x_refs, o_refs, *sems)
        for cp in remote:
            cp.wait_send()
        for cp in remote:
            cp.wait_recv()
        for cp in local:
            cp.wait()


def _exchange(xs, axes, mode, name):
    ex = _Exchange(xs, axes, mode)
    na = ex.na

    def body(*refs):
        ex.start(refs[:na], refs[na:2 * na], refs[2 * na:])
        ex.wait(refs[:na], refs[na:2 * na], refs[2 * na:])

    return pl.pallas_call(body, out_shape=ex.out_shape, in_specs=ex.in_specs, out_specs=ex.out_specs,
                          scratch_shapes=ex.scratch_shapes, name=name)(*xs)


SHARDED_SMALL = (("a_conv_w", 2), ("a_pw_w", 1), ("s5_glu_w", 1), ("c_conv_w", 2), ("d_conv_w", 2))
REPLICATED = ("norm_g", "a_conv_b", "a_ln_g", "a_ln_b", "a_pw_b", "s5_lambda_re", "s5_lambda_im", "s5_b_re",
              "s5_b_im", "s5_c_re", "s5_c_im", "s5_d", "s5_log_dt", "s5_glu_b", "d_a_log", "d_dt_bias",
              "d_norm_g", "final_g")
WEIGHTS = ("norm_g", "w_in", "a_conv_w", "a_conv_b", "a_ln_g", "a_ln_b", "a_pw_w", "a_pw_b", "s5_lambda_re",
           "s5_lambda_im", "s5_b_re", "s5_b_im", "s5_c_re", "s5_c_im", "s5_d", "s5_log_dt", "s5_glu_w",
           "s5_glu_b", "c_conv_w", "d_conv_w", "d_a_log", "d_dt_bias", "d_norm_g", "w_out", "final_g")
LANES = 1024


def _size(shape):
    size = 1
    for d in shape:
        size *= d
    return size


def _slab_rows(shape):
    return -(-_size(shape) // (8 * LANES)) * 8


def _pack(arrs, rows):
    parts = []
    for a in arrs:
        r = _slab_rows(a.shape)
        parts.append(jnp.pad(a.reshape(-1), (0, r * LANES - a.size)).reshape(r, LANES))
    used = sum(p.shape[0] for p in parts)
    if rows > used:
        parts.append(jnp.zeros((rows - used, LANES), parts[0].dtype))
    return jnp.concatenate(parts, axis=0)


def _unpack(slab, shapes):
    out, off = [], 0
    for s in shapes:
        r = _slab_rows(s)
        out.append(slab[off:off + r].reshape(-1)[:_size(s)].reshape(s))
        off += r
    return out


def _rows_for(shapes, mult):
    rows = sum(_slab_rows(s) for s in shapes)
    return -(-rows // mult) * mult


def _row(v, width=None):
    v = v.reshape(1, -1)
    return v if width is None else jnp.pad(v, ((0, 0), (0, width - v.shape[1])))


def _pad_rows(w, rows):
    return jnp.pad(w, ((0, rows - w.shape[0]), (0, 0)))


def _assemble_in(shards):
    n = shards[0].shape[1]
    cut = 12 * BR - 3 * n
    last = shards[3]
    return jnp.concatenate([shards[0], shards[1], shards[2], last[:, :cut], last[:, cut + 2 * DN_H:],
                            last[:, cut:cut + 2 * DN_H],
                            jnp.zeros((last.shape[0], N_INP - N_IN), last.dtype)], axis=1)


def _layer_fwd(x, p, exchanges):
    proj, h = _proj_fwd(x, p["norm_g"], p["wp"])
    ya, a_conv = _a_fwd(proj, p["a_cw"], p["a_cb"], p["a_lng"], p["a_lnb"], p["a_pw"], p["a_pwb"])
    ar, ai, bemb, cemb = _s5_prep(*p["s5"])
    got = {}
    (yb, sin_all, states), got["b"] = _b_fwd(proj, ar, ai, bemb, cemb, p["s5_d"], p["glu_w"], p["glu_b"],
                                             exchanges.get("b"))
    yc = _c_fwd(proj, p["c_cw"])
    (yd, sall, inv), got["d"] = _d_fwd(proj, p["d_cw"], p["d_p1"], p["d_p2"], p["d_ng"], exchanges.get("d"))
    xo = _out_fwd(x, ya, yb, yc, yd, p["wo"])
    return xo, dict(x=x, proj=proj, h=h, ys=(ya, yb, yc, yd), a_conv=a_conv, sin_all=sin_all, states=states,
                    sall=sall, inv=inv,
                    s5=(ar, ai, bemb, cemb)), got


def _layer_bwd(dxo, p, r, exchanges):
    proj = r["proj"]
    ar, ai, bemb, cemb = r["s5"]
    dmix = _out_bwd_x(dxo, p["wo"])
    dwo = _dwout(*r["ys"], dxo)
    dpa, dcw_a, dcb, dlng, dlnb, dpw, dpwb = _a_bwd(proj, dmix, r["a_conv"], p["a_cw"], p["a_cb"], p["a_lng"],
                                                     p["a_lnb"], p["a_pw"], p["a_pwb"])
    got = {}
    (dpb, dar, dai, dbe, dce, ddsk, dgw, dgb), got["b"] = _b_bwd(
        proj, dmix, r["sin_all"], r["states"], ar, ai, bemb, cemb, p["s5_d"], p["glu_w"], p["glu_b"],
        exchanges.get("b"))
    dlre, dlim, dldt, dbre, dbim, dcre, dcim = _s5_prep_bwd(*p["s5"], dar, dai, dbe, dce)
    dpc, dcw_c = _c_bwd(proj, dmix, p["c_cw"])
    (dpd, dcw_d, dp1, dp2, dng), got["d"] = _d_bwd(proj, dmix, r["sall"], r["inv"], p["d_cw"], p["d_p1"],
                                                   p["d_p2"], p["d_ng"], exchanges.get("d"))
    dwa, dwb, dwc, dwd = _dwin(r["h"], dpa, dpb, dpc, dpd)
    dwin = jnp.concatenate([dwa, dwb, dwc, dwd[:, :3 * BR], dwd[:, 4 * BR:4 * BR + 2 * DN_H],
                            dwd[:, 3 * BR:4 * BR]], axis=1)
    ex_proj = exchanges.get("proj")
    if callable(ex_proj):
        ex_proj = ex_proj(got["d"], dwin, dwo)
    (dx, dg), got["proj"] = _proj_bwd_x(r["x"], p["norm_g"], dpa, dpb, dpc, dpd, p["wp"], dxo, ex_proj)

    def unrows(t, perm):
        return jnp.transpose(t.reshape(S5_H, S5_G, S5_P), perm)

    grads = dict(
        norm_g=dg.reshape(-1), w_in=dwin, a_conv_w=dcw_a[:K_A], a_conv_b=dcb.reshape(-1),
        a_ln_g=dlng.reshape(-1), a_ln_b=dlnb.reshape(-1), a_pw_w=dpw, a_pw_b=dpwb.reshape(-1),
        s5_lambda_re=dlre.reshape(S5_G, S5_P), s5_lambda_im=dlim.reshape(S5_G, S5_P),
        s5_b_re=unrows(dbre, (1, 2, 0)), s5_b_im=unrows(dbim, (1, 2, 0)),
        s5_c_re=unrows(dcre, (1, 0, 2)), s5_c_im=unrows(dcim, (1, 0, 2)),
        s5_d=ddsk.reshape(-1), s5_log_dt=dldt[0, :S5_G], s5_glu_w=dgw, s5_glu_b=dgb.reshape(-1),
        c_conv_w=dcw_c[:K_C], d_conv_w=dcw_d[:K_DN], d_a_log=dp1[0, :DN_H], d_dt_bias=dp2[0, :DN_H],
        d_norm_g=dng.reshape(-1), w_out=dwo)
    return dx, grads, got


def _layer_params(full, wp, wo, l):
    return dict(
        norm_g=_row(full["norm_g"][l]), wp=wp,
        a_cw=_pad_rows(full["a_conv_w"][l], HALO_A), a_cb=_row(full["a_conv_b"][l]),
        a_lng=_row(full["a_ln_g"][l]), a_lnb=_row(full["a_ln_b"][l]), a_pw=full["a_pw_w"][l],
        a_pwb=_row(full["a_pw_b"][l]),
        s5=(_row(full["s5_lambda_re"][l]), _row(full["s5_lambda_im"][l]), _row(full["s5_log_dt"][l], 128),
            jnp.transpose(full["s5_b_re"][l], (2, 0, 1)).reshape(S5_H, NS),
            jnp.transpose(full["s5_b_im"][l], (2, 0, 1)).reshape(S5_H, NS),
            jnp.transpose(full["s5_c_re"][l], (1, 0, 2)).reshape(S5_H, NS),
            jnp.transpose(full["s5_c_im"][l], (1, 0, 2)).reshape(S5_H, NS)),
        s5_d=_row(full["s5_d"][l]), glu_w=full["s5_glu_w"][l], glu_b=_row(full["s5_glu_b"][l]),
        c_cw=_pad_rows(full["c_conv_w"][l], HALO_S), d_cw=_pad_rows(full["d_conv_w"][l], HALO_S),
        d_p1=_row(full["d_a_log"][l], 128), d_p2=_row(full["d_dt_bias"][l], 128),
        d_ng=_row(full["d_norm_g"][l]), wo=wo)


def kernel(x, norm_g, w_in, a_conv_w, a_conv_b, a_ln_g, a_ln_b, a_pw_w, a_pw_b, s5_lambda_re, s5_lambda_im, s5_b_re, s5_b_im, s5_c_re, s5_c_im, s5_d, s5_log_dt, s5_glu_w, s5_glu_b, c_conv_w, d_conv_w, d_a_log, d_dt_bias, d_norm_g, w_out, final_g, loss_target, m_norm_g, m_w_in, m_a_conv_w, m_a_conv_b, m_a_ln_g, m_a_ln_b, m_a_pw_w, m_a_pw_b, m_s5_lambda_re, m_s5_lambda_im, m_s5_b_re, m_s5_b_im, m_s5_c_re, m_s5_c_im, m_s5_d, m_s5_log_dt, m_s5_glu_w, m_s5_glu_b, m_c_conv_w, m_d_conv_w, m_d_a_log, m_d_dt_bias, m_d_norm_g, m_w_out, m_final_g, v_norm_g, v_w_in, v_a_conv_w, v_a_conv_b, v_a_ln_g, v_a_ln_b, v_a_pw_w, v_a_pw_b, v_s5_lambda_re, v_s5_lambda_im, v_s5_b_re, v_s5_b_im, v_s5_c_re, v_s5_c_im, v_s5_d, v_s5_log_dt, v_s5_glu_w, v_s5_glu_b, v_c_conv_w, v_d_conv_w, v_d_a_log, v_d_dt_bias, v_d_norm_g, v_w_out, v_final_g):
    given = dict(locals())
    w = {n: given[n] for n in WEIGHTS}
    m = {n: given["m_" + n] for n in WEIGHTS}
    v = {n: given["v_" + n] for n in WEIGHTS}
    xs, tgt = x[0], loss_target[0]

    n_in, n_out = w["w_in"].shape[2], w["w_out"].shape[1]
    sm_names = [n for n, _ in SHARDED_SMALL]
    sm_shapes = [w[n].shape for n in sm_names]
    sm_rows = _rows_for(sm_shapes, 16)
    win_b, wout_b = w["w_in"].astype(BF), w["w_out"].astype(BF)
    g_in, g_out, g_sm = _exchange([win_b[0], wout_b[0], _pack([w[n] for n in sm_names], sm_rows)],
                                  ("x", "y"), "gather", "gather_first")
    full = dict(w)
    parts = [_unpack(g_sm[j], sm_shapes) for j in range(4)]
    for i, (n, ax) in enumerate(SHARDED_SMALL):
        full[n] = jnp.concatenate([parts[j][i] for j in range(4)], axis=ax)

    saved = []
    h = xs
    for l in range(DEPTH):
        p = _layer_params(full, _assemble_in([g_in[j] for j in range(4)]),
                          jnp.concatenate([g_out[j] for j in range(4)], axis=0), l)
        nxt = {}
        if l + 1 < DEPTH:
            nxt = {"d": ([win_b[l + 1]], ("x", "y"), "gather"), "b": ([wout_b[l + 1]], ("x", "y"), "gather")}
        h, r, got = _layer_fwd(h, p, nxt)
        saved.append((p, r))
        if l + 1 < DEPTH:
            (g_in,), (g_out,) = got["d"], got["b"]
    loss_tile, dx, dfg = _loss_bwd(h, _row(full["final_g"]), tgt)

    def big_slots(dwin, dwo):
        s_in = jnp.stack([dwin[:, j * n_in:(j + 1) * n_in].astype(BF) for j in range(4)])
        return [s_in.reshape(8, D_MODEL // 2, n_in), dwo.astype(BF).reshape(8, n_out // 2, D_MODEL)]

    def halves(rv):
        return [_sum_slots(rv[0], "sum_w_in")[None], _sum_slots(rv[1], "sum_w_out")[None]]

    layer_grads, summed, pending, arrived = [None] * DEPTH, [None] * DEPTH, None, {}
    for l in reversed(range(DEPTH)):
        p, r = saved[l]
        ex = {}
        if pending is not None:
            ex["d"] = (pending, AXES, "scatter")
        if l + 2 in arrived:
            ex["b"] = (halves(arrived.pop(l + 2)), ("c",), "rows")
        if l == 0:
            ex["proj"] = lambda came, dwin, dwo: [(halves(came), ("c",), "rows"),
                                                  (big_slots(dwin, dwo), AXES, "scatter")]
        dx, layer_grads[l], got = _layer_bwd(dx, p, r, ex)
        if got["b"] is not None:
            summed[l + 2] = got["b"]
        if got["proj"] is not None:
            summed[1], arrived[0] = got["proj"]
        elif got["d"] is not None:
            arrived[l + 1] = got["d"]
        pending = big_slots(layer_grads[l]["w_in"], layer_grads[l]["w_out"]) if l else None
    grads = {n: jnp.stack([layer_grads[l][n] for l in range(DEPTH)]) for n in WEIGHTS
             if n not in ("final_g", "w_in", "w_out")}
    grads["final_g"] = dfg.reshape(-1)
    slots = []
    for j in range(4):
        sl = [lax.slice_in_dim(grads[n], j * w[n].shape[ax], (j + 1) * w[n].shape[ax], axis=ax)
              for n, ax in SHARDED_SMALL]
        slots.append(_pack(sl, sm_rows))
    rp_shapes = [w[n].shape for n in REPLICATED] + [(1,)]
    rp_rows = _rows_for(rp_shapes, 64)
    r_sm, r_rp = _exchange(
        [jnp.stack(slots).reshape(8, sm_rows // 2, LANES),
         _pack([grads[n] for n in REPLICATED] + [loss_tile[0, 0:1]], rp_rows).reshape(8, rp_rows // 8, LANES)],
        AXES, "scatter", "scatter_last")
    summed[0] = _exchange(halves(arrived.pop(0)) + [_sum_slots(r_sm, "sum_small")[None]], ("c",), "rows",
                          "gather_halves")
    h_sm = summed[0][2]
    h_in = jnp.concatenate([summed[l][0] for l in range(DEPTH)], axis=0)
    h_out = jnp.concatenate([summed[l][1] for l in range(DEPTH)], axis=0)
    (g_rp,) = _exchange([_sum_slots(r_rp, "sum_replicated")], AXES, "gather", "gather_replicated")
    g_rp = g_rp.reshape(rp_rows, LANES)
    g_sm = h_sm.reshape(sm_rows, LANES)

    out = {}

    def put(name, shape, res):
        for key, t in zip(("delta", "new_m", "new_v"), res):
            out[key + "_" + name] = t.reshape(shape)

    g2 = h_out.reshape(DEPTH * n_out, D_MODEL)
    out["grad_w_out"] = g2.reshape(w["w_out"].shape)
    put("w_out", w["w_out"].shape, _adamw(w["w_out"].reshape(g2.shape), g2, m["w_out"].reshape(g2.shape),
                                          v["w_out"].reshape(g2.shape), "adamw_w_out"))
    cm = lambda a: jnp.transpose(a, (2, 0, 1))
    rm = lambda a: jnp.transpose(a, (1, 2, 0))
    g3 = cm(h_in)
    out["grad_w_in"] = rm(g3)
    for key, t in zip(("delta", "new_m", "new_v"),
                      _adamw(cm(w["w_in"]), g3, cm(m["w_in"]), cm(v["w_in"]), "adamw_w_in", lead=n_in // 6)):
        out[key + "_w_in"] = rm(t)
    zero = jnp.zeros((1,), F32)
    res_sm = _adamw(_pack([w[n] for n in sm_names], sm_rows), g_sm, _pack([m[n] for n in sm_names], sm_rows),
                    _pack([v[n] for n in sm_names], sm_rows), "adamw_small")
    res_rp = _adamw(_pack([w[n] for n in REPLICATED] + [zero], rp_rows), g_rp,
                    _pack([m[n] for n in REPLICATED] + [zero], rp_rows),
                    _pack([v[n] for n in REPLICATED] + [zero], rp_rows), "adamw_replicated")
    for key, sm, rp in (("grad", g_sm, g_rp), ("delta", res_sm[0], res_rp[0]), ("new_m", res_sm[1], res_rp[1]),
                        ("new_v", res_sm[2], res_rp[2])):
        for n, t in zip(sm_names, _unpack(sm, sm_shapes)):
            out[key + "_" + n] = t
        for n, t in zip(REPLICATED, _unpack(rp, rp_shapes[:-1])):
            out[key + "_" + n] = t
    loss = _unpack(g_rp, rp_shapes)[-1].reshape(())
    return (loss, dx[None], *[out["grad_" + n] for n in WEIGHTS], *[out["delta_" + n] for n in WEIGHTS],
            *[out["new_m_" + n] for n in WEIGHTS], *[out["new_v_" + n] for n in WEIGHTS])
```

```python
import functools

import jax
import jax.numpy as jnp
from jax import lax
from jax.experimental import pallas as pl
from jax.experimental.pallas import tpu as pltpu

F32, BF = jnp.float32, jnp.bfloat16
HI = lax.Precision.HIGHEST
MESH = pl.DeviceIdType.MESH

D_MODEL = 1024
BR = 256
DEPTH = 4
N_IN = 3336
N_INP = 3456
COL_A, COL_B, COL_C, COL_D = 0, 768, 1280, 2304
W_A, W_B, W_C, W_D = 768, 512, 1024, 1152
S5_G, S5_H, S5_P = 16, 16, 64
NS = S5_G * S5_P
DN_H, DN_D, DN_C = 4, 64, 64
DN_G_FWD, DN_G_BWD = 8, 4
DN_P = DN_H // 2
DN_STATE = (DN_P, 2 * DN_D, 2 * DN_D)
DN_INV = (DN_P, DN_C, 2 * DN_D)
K_A, K_C, K_DN = 31, 3, 4
HALO_A, HALO_S = 32, 8
EPS = 1e-6
TL = 256
VMEM_LIMIT = 56 * 1024 * 1024

ADAM_LR, ADAM_B1, ADAM_B2, ADAM_EPS, ADAM_WD, ADAM_STEP = 0.001, 0.9, 0.999, 1e-08, 0.01, 10


def _cp(*sem):
    return pltpu.CompilerParams(dimension_semantics=sem, vmem_limit_bytes=VMEM_LIMIT)


def _sigmoid(x):
    return jax.nn.sigmoid(x)


def _silu(x):
    return x * jax.nn.sigmoid(x)


def _rmsnorm(x, g):
    return x * lax.rsqrt(jnp.mean(x * x, axis=-1, keepdims=True) + EPS) * g


@jax.custom_vjp
def _mm(a, w):
    return jnp.dot(a.astype(BF), w.astype(BF), preferred_element_type=F32)


def _mm_f(a, w):
    return _mm(a, w), (a, w)


def _mm_b(res, g):
    a, w = res
    gb = g.astype(BF)
    da = lax.dot_general(gb, w.astype(BF), (((1,), (1,)), ((), ())), preferred_element_type=F32)
    dw = lax.dot_general(a.astype(BF), gb, (((0,), (0,)), ((), ())), preferred_element_type=F32)
    return da, dw


_mm.defvjp(_mm_f, _mm_b)


@jax.custom_vjp
def _mm_nt(a, b):
    return lax.dot_general(a.astype(BF), b.astype(BF), (((1,), (1,)), ((), ())), preferred_element_type=F32)


def _mm_nt_f(a, b):
    return _mm_nt(a, b), (a, b)


def _mm_nt_b(res, g):
    a, b = res
    gb = g.astype(BF)
    da = jnp.dot(gb, b.astype(BF), preferred_element_type=F32)
    db = lax.dot_general(gb, a.astype(BF), (((0,), (0,)), ((), ())), preferred_element_type=F32)
    return da, db


_mm_nt.defvjp(_mm_nt_f, _mm_nt_b)


@jax.custom_vjp
def _mm_tn(a, b):
    return lax.dot_general(a.astype(BF), b.astype(BF), (((0,), (0,)), ((), ())), preferred_element_type=F32)


def _mm_tn_f(a, b):
    return _mm_tn(a, b), (a, b)


def _mm_tn_b(res, g):
    a, b = res
    gb = g.astype(BF)
    da = lax.dot_general(b.astype(BF), gb, (((1,), (1,)), ((), ())), preferred_element_type=F32)
    db = jnp.dot(a.astype(BF), gb, preferred_element_type=F32)
    return da, db


_mm_tn.defvjp(_mm_tn_f, _mm_tn_b)


def _dot_hi(a, b):
    return jnp.dot(a, b, precision=HI, preferred_element_type=F32)


def _split(a):
    hi = a.astype(BF)
    return hi, (a - hi.astype(F32)).astype(BF)


def _dot3(a, b, dims=(((1,), (0,)), ((), ()))):
    ah, al = _split(a)
    bh, bl = _split(b)
    d = functools.partial(lax.dot_general, dimension_numbers=dims, preferred_element_type=F32)
    return d(ah, bh) + d(ah, bl) + d(al, bh)


@jax.custom_vjp
def _mm3(a, b):
    return _dot3(a, b)


def _mm3_f(a, b):
    return _dot3(a, b), (a, b)


def _mm3_b(res, g):
    a, b = res
    return _dot3(g, b, (((1,), (1,)), ((), ()))), _dot3(a, g, (((0,), (0,)), ((), ())))


_mm3.defvjp(_mm3_f, _mm3_b)


def _roll(x, s):
    n = x.shape[0]
    s = s % n
    return x if s == 0 else pltpu.roll(x, s, 0)


def _conv_taps(ext, w_ref, halo, k_taps, tl):
    acc = None
    for k in range(k_taps):
        term = _roll(ext, (k_taps - 1) - k)[halo:halo + tl] * w_ref[k:k + 1, :]
        acc = term if acc is None else acc + term
    return acc


def _conv_taps_bwd(ext, w_ref, dw_ref, dacc, halo, k_taps, tl):
    dpad = jnp.concatenate([dacc, jnp.zeros((halo, dacc.shape[1]), F32)], axis=0)
    dext = None
    for k in range(k_taps):
        r = _roll(ext, (k_taps - 1) - k)[halo:halo + tl]
        dw_ref[k:k + 1, :] += jnp.sum(r * dacc, axis=0, keepdims=True)
        term = _roll(dpad, halo - (k_taps - 1) + k) * w_ref[k:k + 1, :]
        dext = term if dext is None else dext + term
    return dext


def _add_tail(x, tail):
    tl, h = x.shape[0], tail.shape[0]
    return x + jnp.concatenate([jnp.zeros((tl - h, x.shape[1]), F32), tail], axis=0)


def _proj_fwd(x, g, wp):
    L = x.shape[0]

    def body(x_ref, g_ref, w_ref, p_ref, h_ref):
        hb = _rmsnorm(x_ref[...], g_ref[...]).astype(BF)
        h_ref[...] = hb
        p_ref[...] = jnp.dot(hb, w_ref[...], preferred_element_type=F32)

    return pl.pallas_call(
        body, grid=(L // TL,),
        in_specs=[pl.BlockSpec((TL, D_MODEL), lambda i: (i, 0)),
                  pl.BlockSpec((1, D_MODEL), lambda i: (0, 0)),
                  pl.BlockSpec((D_MODEL, N_INP), lambda i: (0, 0))],
        out_specs=[pl.BlockSpec((TL, N_INP), lambda i: (i, 0)),
                   pl.BlockSpec((TL, D_MODEL), lambda i: (i, 0))],
        out_shape=[jax.ShapeDtypeStruct((L, N_INP), F32), jax.ShapeDtypeStruct((L, D_MODEL), BF)],
        name="proj_fwd", compiler_params=_cp("parallel"))(x, g, wp)


def _proj_bwd_x(x, g, dpa, dpb, dpc, dpd, wp, dxo, exchange=None):
    L = x.shape[0]

    def body(x_ref, g_ref, a_ref, b_ref, c_ref, d_ref, w_ref, dxo_ref, dx_ref, dg_ref):
        dh = None
        for ref, c0, wd in ((a_ref, COL_A, W_A), (b_ref, COL_B, W_B), (c_ref, COL_C, W_C), (d_ref, COL_D, W_D)):
            t = lax.dot_general(ref[...].astype(BF), w_ref[:, c0:c0 + wd], (((1,), (1,)), ((), ())),
                                preferred_element_type=F32)
            dh = t if dh is None else dh + t
        _, vj = jax.vjp(_rmsnorm, x_ref[...], g_ref[...])
        dx, dg = vj(dh)
        dx_ref[...] = dxo_ref[...] + dx

        @pl.when(pl.program_id(0) == 0)
        def _():
            dg_ref[...] = jnp.zeros_like(dg_ref)

        dg_ref[...] += dg

    def rows(wd):
        return pl.BlockSpec((TL, wd), lambda i: (i, 0))

    return _sweep_with_exchange(
        body, L // TL,
        in_specs=[rows(D_MODEL), pl.BlockSpec((1, D_MODEL), lambda i: (0, 0)),
                  rows(W_A), rows(W_B), rows(W_C), rows(W_D),
                  pl.BlockSpec((D_MODEL, N_INP), lambda i: (0, 0)), rows(D_MODEL)],
        out_specs=[rows(D_MODEL), pl.BlockSpec((1, D_MODEL), lambda i: (0, 0))],
        out_shape=[jax.ShapeDtypeStruct((L, D_MODEL), F32), jax.ShapeDtypeStruct((1, D_MODEL), F32)],
        scratch_shapes=[], args=(x, g, dpa, dpb, dpc, dpd, wp, dxo), exchange=exchange, name="proj_bwd_x")


def _dwin(h, dpa, dpb, dpc, dpd):
    L = h.shape[0]

    def body(h_ref, a_ref, b_ref, c_ref, d_ref, oa_ref, ob_ref, oc_ref, od_ref):
        outs = (oa_ref, ob_ref, oc_ref, od_ref)

        @pl.when(pl.program_id(0) == 0)
        def _():
            for o in outs:
                o[...] = jnp.zeros_like(o)

        ht = h_ref[...].T
        for ref, o in zip((a_ref, b_ref, c_ref, d_ref), outs):
            o[...] += jnp.dot(ht, ref[...].astype(BF), preferred_element_type=F32)

    def rows(wd):
        return pl.BlockSpec((TL, wd), lambda i: (i, 0))

    def whole(wd):
        return pl.BlockSpec((D_MODEL, wd), lambda i: (0, 0))

    widths = (W_A, W_B, W_C, W_D)
    return pl.pallas_call(
        body, grid=(L // TL,),
        in_specs=[rows(D_MODEL)] + [rows(wd) for wd in widths],
        out_specs=[whole(wd) for wd in widths],
        out_shape=[jax.ShapeDtypeStruct((D_MODEL, wd), F32) for wd in widths],
        name="dwin", compiler_params=_cp("arbitrary"))(h, dpa, dpb, dpc, dpd)


def _dwout(ya, yb, yc, yd, dxo):
    L = dxo.shape[0]
    tk, tn = min(512, L), 512

    def body(a_ref, b_ref, c_ref, d_ref, g_ref, o_ref):
        @pl.when(pl.program_id(1) == 0)
        def _():
            o_ref[...] = jnp.zeros_like(o_ref)

        gb = g_ref[...].astype(BF)
        for j, ref in enumerate((a_ref, b_ref, c_ref, d_ref)):
            o_ref[j * BR:(j + 1) * BR, :] += lax.dot_general(ref[...].astype(BF), gb, (((0,), (0,)), ((), ())),
                                                             preferred_element_type=F32)

    ys = pl.BlockSpec((tk, BR), lambda j, t: (t, 0))
    return pl.pallas_call(
        body, grid=(D_MODEL // tn, L // tk),
        in_specs=[ys, ys, ys, ys, pl.BlockSpec((tk, tn), lambda j, t: (t, j))],
        out_specs=pl.BlockSpec((D_MODEL, tn), lambda j, t: (0, j)),
        out_shape=jax.ShapeDtypeStruct((D_MODEL, D_MODEL), F32),
        name="dwout", compiler_params=_cp("parallel", "arbitrary"))(ya, yb, yc, yd, dxo)


def _out_fwd(x, ya, yb, yc, yd, wo):
    L = x.shape[0]

    def body(x_ref, a_ref, b_ref, c_ref, d_ref, w_ref, o_ref):
        acc = x_ref[...]
        for j, ref in enumerate((a_ref, b_ref, c_ref, d_ref)):
            acc = acc + jnp.dot(ref[...].astype(BF), w_ref[j * BR:(j + 1) * BR, :], preferred_element_type=F32)
        o_ref[...] = acc

    def rows(wd):
        return pl.BlockSpec((TL, wd), lambda i: (i, 0))

    return pl.pallas_call(
        body, grid=(L // TL,),
        in_specs=[rows(D_MODEL), rows(BR), rows(BR), rows(BR), rows(BR),
                  pl.BlockSpec((D_MODEL, D_MODEL), lambda i: (0, 0))],
        out_specs=rows(D_MODEL), out_shape=jax.ShapeDtypeStruct((L, D_MODEL), F32),
        name="out_fwd", compiler_params=_cp("parallel"))(x, ya, yb, yc, yd, wo)


def _out_bwd_x(dxo, wo):
    L = dxo.shape[0]

    def body(d_ref, w_ref, o_ref):
        o_ref[...] = lax.dot_general(d_ref[...].astype(BF), w_ref[...], (((1,), (1,)), ((), ())),
                                     preferred_element_type=F32)

    return pl.pallas_call(
        body, grid=(L // TL,),
        in_specs=[pl.BlockSpec((TL, D_MODEL), lambda i: (i, 0)), pl.BlockSpec((D_MODEL, D_MODEL), lambda i: (0, 0))],
        out_specs=pl.BlockSpec((TL, D_MODEL), lambda i: (i, 0)),
        out_shape=jax.ShapeDtypeStruct((L, D_MODEL), F32),
        name="out_bwd_x", compiler_params=_cp("parallel"))(dxo, wo)


def _loss_bwd(x, g, tgt):
    L = x.shape[0]

    def f(xv, gv, tv):
        err = _rmsnorm(xv, gv) - tv
        return 0.5 * jnp.sum(jnp.mean(err * err, axis=-1, keepdims=True), axis=0, keepdims=True)

    def body(x_ref, g_ref, t_ref, loss_ref, dx_ref, dg_ref):
        tv = t_ref[...]
        loss, vj = jax.vjp(lambda a, b: f(a, b, tv), x_ref[...], g_ref[...])
        dx, dg = vj(jnp.ones((1, 1), F32))
        dx_ref[...] = dx

        @pl.when(pl.program_id(0) == 0)
        def _():
            dg_ref[...] = jnp.zeros_like(dg_ref)
            loss_ref[...] = jnp.zeros_like(loss_ref)

        dg_ref[...] += dg
        loss_ref[...] += jnp.broadcast_to(loss, loss_ref.shape)

    return pl.pallas_call(
        body, grid=(L // TL,),
        in_specs=[pl.BlockSpec((TL, D_MODEL), lambda i: (i, 0)), pl.BlockSpec((1, D_MODEL), lambda i: (0, 0)),
                  pl.BlockSpec((TL, D_MODEL), lambda i: (i, 0))],
        out_specs=[pl.BlockSpec((8, 128), lambda i: (0, 0)), pl.BlockSpec((TL, D_MODEL), lambda i: (i, 0)),
                   pl.BlockSpec((1, D_MODEL), lambda i: (0, 0))],
        out_shape=[jax.ShapeDtypeStruct((8, 128), F32), jax.ShapeDtypeStruct((L, D_MODEL), F32),
                   jax.ShapeDtypeStruct((1, D_MODEL), F32)],
        name="loss_bwd", compiler_params=_cp("arbitrary"))(x, g, tgt)


def _a_pre(val, gate):
    return val * _sigmoid(gate)


def _a_post(acc, az, cb, lng, lnb, pw, pwb):
    t = acc + cb
    mu = jnp.mean(t, axis=-1, keepdims=True)
    xc = t - mu
    ln = xc * lax.rsqrt(jnp.mean(xc * xc, axis=-1, keepdims=True) + EPS) * lng + lnb
    return (_mm(_silu(ln), pw) + pwb) * _silu(az)


def _halo_map(tl, halo, col):
    r = tl // halo
    return lambda i: (jnp.maximum(i * r - 1, 0), col)


def _a_fwd(proj, cw, cb, lng, lnb, pw, pwb):
    L = proj.shape[0]

    def body(vg_ref, az_ref, hvg_ref, cw_ref, cb_ref, lng_ref, lnb_ref, pw_ref, pwb_ref, o_ref, acc_ref):
        keep = (pl.program_id(0) > 0).astype(F32)
        a_h = _a_pre(hvg_ref[:, 0:BR], hvg_ref[:, BR:2 * BR]) * keep
        a_t = _a_pre(vg_ref[:, 0:BR], vg_ref[:, BR:2 * BR])
        ext = jnp.concatenate([a_h, a_t], axis=0)
        acc = _conv_taps(ext, cw_ref, HALO_A, K_A, TL)
        acc_ref[...] = acc
        o_ref[...] = _a_post(acc, az_ref[...], cb_ref[...], lng_ref[...], lnb_ref[...], pw_ref[...], pwb_ref[...])

    vec = pl.BlockSpec((1, BR), lambda i: (0, 0))
    tile = pl.BlockSpec((TL, BR), lambda i: (i, 0))
    return pl.pallas_call(
        body, grid=(L // TL,),
        in_specs=[pl.BlockSpec((TL, 2 * BR), lambda i: (i, 0)), pl.BlockSpec((TL, BR), lambda i: (i, 2)),
                  pl.BlockSpec((HALO_A, 2 * BR), _halo_map(TL, HALO_A, 0)),
                  pl.BlockSpec((HALO_A, BR), lambda i: (0, 0)), vec, vec, vec,
                  pl.BlockSpec((BR, BR), lambda i: (0, 0)), vec],
        out_specs=[tile, tile],
        out_shape=[jax.ShapeDtypeStruct((L, BR), F32)] * 2,
        name="a_fwd", compiler_params=_cp("parallel"))(proj, proj, proj, cw, cb, lng, lnb, pw, pwb)


def _a_bwd(proj, dmix, conv_out, cw, cb, lng, lnb, pw, pwb):
    L = proj.shape[0]
    n = L // TL

    def body(vg_ref, az_ref, hvg_ref, dy_ref, acc_ref, cw_ref, cb_ref, lng_ref, lnb_ref, pw_ref, pwb_ref,
             dp_ref, dcw_ref, dcb_ref, dlng_ref, dlnb_ref, dpw_ref, dpwb_ref, carry_ref):
        i = pl.program_id(0)

        @pl.when(i == 0)
        def _():
            carry_ref[...] = jnp.zeros_like(carry_ref)
            for r in (dcw_ref, dcb_ref, dlng_ref, dlnb_ref, dpw_ref, dpwb_ref):
                r[...] = jnp.zeros_like(r)

        keep = (i < n - 1).astype(F32)
        val, gate = vg_ref[:, 0:BR], vg_ref[:, BR:2 * BR]
        a_h = _a_pre(hvg_ref[:, 0:BR], hvg_ref[:, BR:2 * BR]) * keep
        a_t, vj_pre = jax.vjp(_a_pre, val, gate)
        ext = jnp.concatenate([a_h, a_t], axis=0)
        _, vj_post = jax.vjp(_a_post, acc_ref[...], az_ref[...], cb_ref[...], lng_ref[...], lnb_ref[...],
                             pw_ref[...], pwb_ref[...])
        dacc, daz, dcb, dlng, dlnb, dpw, dpwb = vj_post(dy_ref[...])
        dext = _conv_taps_bwd(ext, cw_ref, dcw_ref, dacc, HALO_A, K_A, TL)
        da = _add_tail(dext[HALO_A:], carry_ref[...])
        carry_ref[...] = dext[:HALO_A]
        dval, dgate = vj_pre(da)
        dp_ref[:, 0:BR] = dval
        dp_ref[:, BR:2 * BR] = dgate
        dp_ref[:, 2 * BR:3 * BR] = daz
        dcb_ref[...] += dcb
        dlng_ref[...] += dlng
        dlnb_ref[...] += dlnb
        dpw_ref[...] += dpw
        dpwb_ref[...] += dpwb

    rev = lambda i: n - 1 - i
    vec = pl.BlockSpec((1, BR), lambda i: (0, 0))
    hmap = _halo_map(TL, HALO_A, 0)
    return pl.pallas_call(
        body, grid=(n,),
        in_specs=[pl.BlockSpec((TL, 2 * BR), lambda i: (rev(i), 0)), pl.BlockSpec((TL, BR), lambda i: (rev(i), 2)),
                  pl.BlockSpec((HALO_A, 2 * BR), lambda i: hmap(rev(i))),
                  pl.BlockSpec((TL, BR), lambda i: (rev(i), 0)), pl.BlockSpec((TL, BR), lambda i: (rev(i), 0)),
                  pl.BlockSpec((HALO_A, BR), lambda i: (0, 0)), vec, vec, vec,
                  pl.BlockSpec((BR, BR), lambda i: (0, 0)), vec],
        out_specs=[pl.BlockSpec((TL, W_A), lambda i: (rev(i), 0)),
                   pl.BlockSpec((HALO_A, BR), lambda i: (0, 0)), vec, vec, vec,
                   pl.BlockSpec((BR, BR), lambda i: (0, 0)), vec],
        out_shape=[jax.ShapeDtypeStruct((L, W_A), F32), jax.ShapeDtypeStruct((HALO_A, BR), F32)]
        + [jax.ShapeDtypeStruct((1, BR), F32)] * 3
        + [jax.ShapeDtypeStruct((BR, BR), F32), jax.ShapeDtypeStruct((1, BR), F32)],
        scratch_shapes=[pltpu.VMEM((HALO_A, BR), F32)],
        name="a_bwd", compiler_params=_cp("arbitrary"))(proj, proj, proj, dmix, conv_out, cw, cb, lng, lnb, pw, pwb)


def _c_pre(cg, xc):
    return cg * xc


def _c_post(acc, bg, cz):
    return bg * acc * _silu(cz)


def _c_fwd(proj, cw):
    L = proj.shape[0]

    def body(bg_ref, cx_ref, cz_ref, hcx_ref, cw_ref, o_ref):
        keep = (pl.program_id(0) > 0).astype(F32)
        p_h = _c_pre(hcx_ref[:, 0:BR], hcx_ref[:, BR:2 * BR]) * keep
        p_t = _c_pre(cx_ref[:, 0:BR], cx_ref[:, BR:2 * BR])
        ext = jnp.concatenate([p_h, p_t], axis=0)
        acc = _conv_taps(ext, cw_ref, HALO_S, K_C, TL)
        o_ref[...] = _c_post(acc, bg_ref[...], cz_ref[...])

    return pl.pallas_call(
        body, grid=(L // TL,),
        in_specs=[pl.BlockSpec((TL, BR), lambda i: (i, 5)), pl.BlockSpec((TL, 2 * BR), lambda i: (i, 3)),
                  pl.BlockSpec((TL, BR), lambda i: (i, 8)),
                  pl.BlockSpec((HALO_S, 2 * BR), _halo_map(TL, HALO_S, 3)),
                  pl.BlockSpec((HALO_S, BR), lambda i: (0, 0))],
        out_specs=pl.BlockSpec((TL, BR), lambda i: (i, 0)),
        out_shape=jax.ShapeDtypeStruct((L, BR), F32),
        name="c_fwd", compiler_params=_cp("parallel"))(proj, proj, proj, proj, cw)


def _c_bwd(proj, dmix, cw):
    L = proj.shape[0]
    n = L // TL

    def body(bg_ref, cx_ref, cz_ref, hcx_ref, dy_ref, cw_ref, dp_ref, dcw_ref, carry_ref):
        i = pl.program_id(0)

        @pl.when(i == 0)
        def _():
            carry_ref[...] = jnp.zeros_like(carry_ref)
            dcw_ref[...] = jnp.zeros_like(dcw_ref)

        keep = (i < n - 1).astype(F32)
        p_h = _c_pre(hcx_ref[:, 0:BR], hcx_ref[:, BR:2 * BR]) * keep
        p_t, vj_pre = jax.vjp(_c_pre, cx_ref[:, 0:BR], cx_ref[:, BR:2 * BR])
        ext = jnp.concatenate([p_h, p_t], axis=0)
        acc = _conv_taps(ext, cw_ref, HALO_S, K_C, TL)
        _, vj_post = jax.vjp(_c_post, acc, bg_ref[...], cz_ref[...])
        dacc, dbg, dcz = vj_post(dy_ref[...])
        dext = _conv_taps_bwd(ext, cw_ref, dcw_ref, dacc, HALO_S, K_C, TL)
        dp = _add_tail(dext[HALO_S:], carry_ref[...])
        carry_ref[...] = dext[:HALO_S]
        dcg, dxc = vj_pre(dp)
        dp_ref[:, 0:BR] = dbg
        dp_ref[:, BR:2 * BR] = dcg
        dp_ref[:, 2 * BR:3 * BR] = dxc
        dp_ref[:, 3 * BR:4 * BR] = dcz

    rev = lambda i: n - 1 - i
    hmap = _halo_map(TL, HALO_S, 3)
    return pl.pallas_call(
        body, grid=(n,),
        in_specs=[pl.BlockSpec((TL, BR), lambda i: (rev(i), 5)), pl.BlockSpec((TL, 2 * BR), lambda i: (rev(i), 3)),
                  pl.BlockSpec((TL, BR), lambda i: (rev(i), 8)),
                  pl.BlockSpec((HALO_S, 2 * BR), lambda i: hmap(rev(i))),
                  pl.BlockSpec((TL, BR), lambda i: (rev(i), 2)),
                  pl.BlockSpec((HALO_S, BR), lambda i: (0, 0))],
        out_specs=[pl.BlockSpec((TL, W_C), lambda i: (rev(i), 0)), pl.BlockSpec((HALO_S, BR), lambda i: (0, 0))],
        out_shape=[jax.ShapeDtypeStruct((L, W_C), F32), jax.ShapeDtypeStruct((HALO_S, BR), F32)],
        scratch_shapes=[pltpu.VMEM((HALO_S, BR), F32)],
        name="c_bwd", compiler_params=_cp("arbitrary"))(proj, proj, proj, proj, dmix, cw)


def _s5_prep_fn(lre, lim, ldt, bre, bim, cre, cim):
    grp = lax.broadcasted_iota(jnp.int32, (128, NS), 0)
    lane = lax.broadcasted_iota(jnp.int32, (128, NS), 1)
    expand = (grp == lane // S5_P).astype(F32)
    dt = jnp.exp(_dot_hi(jnp.broadcast_to(ldt, (8, 128)), expand)[0:1])
    lr = jnp.minimum(lre, -1e-4)
    mag = jnp.exp(lr * dt)
    ar = mag * jnp.cos(lim * dt)
    ai = mag * jnp.sin(lim * dt)
    den = lr * lr + lim * lim
    fr = ((ar - 1.0) * lr + ai * lim) / den
    fi = (ai * lr - (ar - 1.0) * lim) / den
    bbr = fr * bre - fi * bim
    bbi = fr * bim + fi * bre
    row = lax.broadcasted_iota(jnp.int32, (BR, NS), 0)
    col = lax.broadcasted_iota(jnp.int32, (BR, NS), 1)
    blk = (row // S5_H == col // S5_P).astype(F32)

    def embed(t):
        return jnp.concatenate([t] * S5_G, axis=0) * blk

    bemb = jnp.concatenate([embed(bbr), embed(bbi)], axis=1)
    cemb = jnp.concatenate([embed(cre), embed(-cim)], axis=1)
    return ar, ai, bemb, cemb


def _s5_prep(lre, lim, ldt, bre, bim, cre, cim):
    def body(*refs):
        outs = _s5_prep_fn(*[r[...] for r in refs[:7]])
        for r, o in zip(refs[7:], outs):
            r[...] = o

    return pl.pallas_call(
        body,
        out_shape=[jax.ShapeDtypeStruct((1, NS), F32)] * 2 + [jax.ShapeDtypeStruct((BR, 2 * NS), F32)] * 2,
        name="s5_prep", compiler_params=pltpu.CompilerParams(vmem_limit_bytes=VMEM_LIMIT),
    )(lre, lim, ldt, bre, bim, cre, cim)


def _s5_prep_bwd(lre, lim, ldt, bre, bim, cre, cim, dar, dai, dbemb, dcemb):
    def body(*refs):
        _, vj = jax.vjp(_s5_prep_fn, *[r[...] for r in refs[:7]])
        grads = vj(tuple(r[...] for r in refs[7:11]))
        for r, o in zip(refs[11:], grads):
            r[...] = o

    return pl.pallas_call(
        body,
        out_shape=[jax.ShapeDtypeStruct((1, NS), F32)] * 2 + [jax.ShapeDtypeStruct((1, 128), F32)]
        + [jax.ShapeDtypeStruct((S5_H, NS), F32)] * 4,
        name="s5_prep_bwd", compiler_params=pltpu.CompilerParams(vmem_limit_bytes=VMEM_LIMIT),
    )(lre, lim, ldt, bre, bim, cre, cim, dar, dai, dbemb, dcemb)


S5_SUB = 16


def _cmac(xr, xi, pr, pi, sr, si):
    return xr + pr * sr - pi * si, xi + pr * si + pi * sr


def _s5_scan(xr, xi, ar, ai, reverse):
    n, wd = xr.shape
    m = S5_SUB
    nsub = n // m
    rsub = lax.broadcasted_iota(jnp.int32, (n, 1), 0) % m
    pr, pi = ar, ai
    power = {}
    d = 1
    while d < m:
        power[d] = (pr, pi)
        if d % 8:
            if reverse:
                msk = rsub < m - d
                sr, si = jnp.where(msk, _roll(xr, n - d), 0.0), jnp.where(msk, _roll(xi, n - d), 0.0)
            else:
                msk = rsub >= d
                sr, si = jnp.where(msk, _roll(xr, d), 0.0), jnp.where(msk, _roll(xi, d), 0.0)
            xr, xi = _cmac(xr, xi, pr, pi, sr, si)
        else:
            x3r, x3i = xr.reshape(nsub, m, wd), xi.reshape(nsub, m, wd)
            if reverse:
                ur, ui = _cmac(x3r[:, :m - d], x3i[:, :m - d], pr, pi, x3r[:, d:], x3i[:, d:])
                x3r = jnp.concatenate([ur, x3r[:, m - d:]], axis=1)
                x3i = jnp.concatenate([ui, x3i[:, m - d:]], axis=1)
            else:
                ur, ui = _cmac(x3r[:, d:], x3i[:, d:], pr, pi, x3r[:, :m - d], x3i[:, :m - d])
                x3r = jnp.concatenate([x3r[:, :d], ur], axis=1)
                x3i = jnp.concatenate([x3i[:, :d], ui], axis=1)
            xr, xi = x3r.reshape(n, wd), x3i.reshape(n, wd)
        pr, pi = pr * pr - pi * pi, 2.0 * pr * pi
        d *= 2
    r8 = lax.broadcasted_iota(jnp.int32, (8, 1), 0)
    qr, qi = ar, ai
    tr, ti = jnp.zeros((8, wd), F32), jnp.zeros((8, wd), F32)
    for e in range(1, 9):
        sel = r8 == (8 - e if reverse else e - 1)
        tr, ti = jnp.where(sel, qr, tr), jnp.where(sel, qi, ti)
        qr, qi = qr * ar - qi * ai, qr * ai + qi * ar
    size = 8
    while size < m:
        er, ei = power[size]
        hr, hi = tr * er - ti * ei, tr * ei + ti * er
        if reverse:
            tr, ti = jnp.concatenate([hr, tr], axis=0), jnp.concatenate([hi, ti], axis=0)
        else:
            tr, ti = jnp.concatenate([tr, hr], axis=0), jnp.concatenate([ti, hi], axis=0)
        size *= 2
    order = list(reversed(range(nsub))) if reverse else list(range(nsub))
    edge = 0 if reverse else m - 1
    done_r, done_i = {}, {}
    cr = ci = None
    for j in order:
        br, bi = xr[j * m:(j + 1) * m], xi[j * m:(j + 1) * m]
        if cr is not None:
            br, bi = _cmac(br, bi, tr, ti, cr, ci)
        done_r[j], done_i[j] = br, bi
        cr, ci = br[edge:edge + 1], bi[edge:edge + 1]
    return (jnp.concatenate([done_r[j] for j in range(nsub)], axis=0),
            jnp.concatenate([done_i[j] for j in range(nsub)], axis=0))


def _s5_states(u, bemb_b, ar, ai, sin_r, sin_i):
    bu = jnp.dot(u.astype(BF), bemb_b, preferred_element_type=F32)
    first = lax.broadcasted_iota(jnp.int32, (u.shape[0], 1), 0) == 0
    xr = bu[:, :NS] + jnp.where(first, ar * sin_r - ai * sin_i, 0.0)
    xi = bu[:, NS:] + jnp.where(first, ar * sin_i + ai * sin_r, 0.0)
    return _s5_scan(xr, xi, ar, ai, False)


def _b_post(yssm, u, bz, dsk, gw, gb):
    z = jax.nn.gelu(yssm + dsk * u)
    return z * _sigmoid(_mm(z, gw) + gb) * _silu(bz)


def _b_fwd(proj, ar, ai, bemb, cemb, dsk, gw, gb, exchange=None):
    L = proj.shape[0]
    n = L // TL

    def body(u_ref, bz_ref, ar_ref, ai_ref, be_ref, ce_ref, dsk_ref, gw_ref, gb_ref, o_ref, sin_ref, st_ref,
             carry_ref):
        @pl.when(pl.program_id(0) == 0)
        def _():
            carry_ref[...] = jnp.zeros_like(carry_ref)

        sin = carry_ref[...]
        sin_ref[0] = sin
        u = u_ref[...]
        sr, si = _s5_states(u, be_ref[...].astype(BF), ar_ref[...], ai_ref[...], sin[:, :NS], sin[:, NS:])
        carry_ref[:, :NS] = sr[TL - 1:TL]
        carry_ref[:, NS:] = si[TL - 1:TL]
        st_ref[:, :NS] = sr
        st_ref[:, NS:] = si
        s = jnp.concatenate([sr, si], axis=1).astype(BF)
        yssm = lax.dot_general(s, ce_ref[...].astype(BF), (((1,), (1,)), ((), ())), preferred_element_type=F32)
        o_ref[...] = _b_post(yssm, u, bz_ref[...], dsk_ref[...], gw_ref[...], gb_ref[...])

    vec = pl.BlockSpec((1, BR), lambda i: (0, 0))
    svec = pl.BlockSpec((1, NS), lambda i: (0, 0))
    emb = pl.BlockSpec((BR, 2 * NS), lambda i: (0, 0))
    return _sweep_with_exchange(
        body, n,
        in_specs=[pl.BlockSpec((TL, BR), lambda i: (i, 3)), pl.BlockSpec((TL, BR), lambda i: (i, 4)),
                  svec, svec, emb, emb, vec, pl.BlockSpec((BR, BR), lambda i: (0, 0)), vec],
        out_specs=[pl.BlockSpec((TL, BR), lambda i: (i, 0)), pl.BlockSpec((1, 1, 2 * NS), lambda i: (i, 0, 0)),
                   pl.BlockSpec((TL, 2 * NS), lambda i: (i, 0))],
        out_shape=[jax.ShapeDtypeStruct((L, BR), F32), jax.ShapeDtypeStruct((n, 1, 2 * NS), F32),
                   jax.ShapeDtypeStruct((L, 2 * NS), F32)],
        scratch_shapes=[pltpu.VMEM((1, 2 * NS), F32)],
        args=(proj, proj, ar, ai, bemb, cemb, dsk, gw, gb), exchange=exchange, name="b_fwd")


def _b_bwd(proj, dmix, sin_all, states, ar, ai, bemb, cemb, dsk, gw, gb, exchange=None):
    L = proj.shape[0]
    n = L // TL

    def body(u_ref, bz_ref, dy_ref, sin_ref, st_ref, ar_ref, ai_ref, be_ref, ce_ref, dsk_ref, gw_ref, gb_ref,
             dp_ref, dar_ref, dai_ref, dbe_ref, dce_ref, ddsk_ref, dgw_ref, dgb_ref, carry_ref):
        i = pl.program_id(0)

        @pl.when(i == 0)
        def _():
            carry_ref[...] = jnp.zeros_like(carry_ref)
            for r in (dar_ref, dai_ref, dbe_ref, dce_ref, ddsk_ref, dgw_ref, dgb_ref):
                r[...] = jnp.zeros_like(r)

        u = u_ref[...]
        ar, ai = ar_ref[...], ai_ref[...]
        be_b, ce_b = be_ref[...].astype(BF), ce_ref[...].astype(BF)
        sin = sin_ref[0]
        sr, si = st_ref[:, :NS], st_ref[:, NS:]
        s_b = st_ref[...].astype(BF)
        yssm = lax.dot_general(s_b, ce_b, (((1,), (1,)), ((), ())), preferred_element_type=F32)
        _, vj = jax.vjp(_b_post, yssm, u, bz_ref[...], dsk_ref[...], gw_ref[...], gb_ref[...])
        dyssm, du, dbz, ddsk, dgw, dgb = vj(dy_ref[...])
        dy_b = dyssm.astype(BF)
        dce_ref[...] += lax.dot_general(dy_b, s_b, (((0,), (0,)), ((), ())), preferred_element_type=F32)
        gs = jnp.dot(dy_b, ce_b, preferred_element_type=F32)
        last = lax.broadcasted_iota(jnp.int32, (TL, 1), 0) == TL - 1
        cr, ci = carry_ref[:, :NS], carry_ref[:, NS:]
        gr = gs[:, :NS] + jnp.where(last, ar * cr + ai * ci, 0.0)
        gi = gs[:, NS:] + jnp.where(last, ar * ci - ai * cr, 0.0)
        dsr, dsi = _s5_scan(gr, gi, ar, -ai, True)
        carry_ref[:, :NS] = dsr[0:1]
        carry_ref[:, NS:] = dsi[0:1]
        first = lax.broadcasted_iota(jnp.int32, (TL, 1), 0) == 0
        pr = jnp.where(first, sin[:, :NS], _roll(sr, 1))
        pi = jnp.where(first, sin[:, NS:], _roll(si, 1))
        dar_ref[...] += jnp.sum(dsr * pr + dsi * pi, axis=0, keepdims=True)
        dai_ref[...] += jnp.sum(dsi * pr - dsr * pi, axis=0, keepdims=True)
        ds_b = jnp.concatenate([dsr, dsi], axis=1).astype(BF)
        dbe_ref[...] += lax.dot_general(u.astype(BF), ds_b, (((0,), (0,)), ((), ())), preferred_element_type=F32)
        du = du + lax.dot_general(ds_b, be_b, (((1,), (1,)), ((), ())), preferred_element_type=F32)
        dp_ref[:, 0:BR] = du
        dp_ref[:, BR:2 * BR] = dbz
        ddsk_ref[...] += ddsk
        dgw_ref[...] += dgw
        dgb_ref[...] += dgb

    rev = lambda i: n - 1 - i
    vec = pl.BlockSpec((1, BR), lambda i: (0, 0))
    svec = pl.BlockSpec((1, NS), lambda i: (0, 0))
    emb = pl.BlockSpec((BR, 2 * NS), lambda i: (0, 0))
    mat = pl.BlockSpec((BR, BR), lambda i: (0, 0))
    return _sweep_with_exchange(
        body, n,
        in_specs=[pl.BlockSpec((TL, BR), lambda i: (rev(i), 3)), pl.BlockSpec((TL, BR), lambda i: (rev(i), 4)),
                  pl.BlockSpec((TL, BR), lambda i: (rev(i), 1)),
                  pl.BlockSpec((1, 1, 2 * NS), lambda i: (rev(i), 0, 0)),
                  pl.BlockSpec((TL, 2 * NS), lambda i: (rev(i), 0)),
                  svec, svec, emb, emb, vec, mat, vec],
        out_specs=[pl.BlockSpec((TL, W_B), lambda i: (rev(i), 0)), svec, svec, emb, emb, vec, mat, vec],
        out_shape=[jax.ShapeDtypeStruct((L, W_B), F32)] + [jax.ShapeDtypeStruct((1, NS), F32)] * 2
        + [jax.ShapeDtypeStruct((BR, 2 * NS), F32)] * 2
        + [jax.ShapeDtypeStruct((1, BR), F32), jax.ShapeDtypeStruct((BR, BR), F32), jax.ShapeDtypeStruct((1, BR), F32)],
        scratch_shapes=[pltpu.VMEM((1, 2 * NS), F32)],
        args=(proj, proj, dmix, sin_all, states, ar, ai, bemb, cemb, dsk, gw, gb), exchange=exchange, name="b_bwd")


def _half_masks(rows):
    lane = lax.broadcasted_iota(jnp.int32, (rows, 2 * DN_D), 1)
    return lane < DN_D, lane >= DN_D


def _bd(x):
    left, right = _half_masks(x.shape[0])
    return jnp.concatenate([jnp.where(left, x, 0.0), jnp.where(right, x, 0.0)], axis=0)


@jax.custom_vjp
def _segsum(x):
    r = lax.broadcasted_iota(jnp.int32, (2 * DN_D, 2 * DN_D), 0) // DN_D
    c = lax.broadcasted_iota(jnp.int32, (2 * DN_D, 2 * DN_D), 1) // DN_D
    ones = (r == c).astype(BF)
    hi, lo = _split(x)
    return jnp.dot(hi, ones, preferred_element_type=F32) + jnp.dot(lo, ones, preferred_element_type=F32)


_segsum.defvjp(lambda x: (_segsum(x), None), lambda _, g: (_segsum(g),))


def _pair_t(x):
    t = _bd(x).T
    return t[:DN_D] + t[DN_D:]


@jax.custom_vjp
def _pair_inv(lms):
    n = DN_D
    row = lax.broadcasted_iota(jnp.int32, (n, 2 * n), 0)
    col = lax.broadcasted_iota(jnp.int32, (n, 2 * n), 1) % n
    eye = (row == col).astype(F32)
    accs = [eye - lm for lm in lms]
    pws = list(lms)
    k = 2
    while k < n:
        pws = [_dot3(p, _bd(p)) for p in pws]
        accs = [a + _dot3(a, _bd(p)) for a, p in zip(accs, pws)]
        k *= 2
    return tuple(accs)


def _pi_b(a, g):
    ats = [_pair_t(x) for x in a]
    tmp = [_dot3(at, _bd(gi)) for at, gi in zip(ats, g)]
    return (tuple(-_dot3(t, _bd(at)) for t, at in zip(tmp, ats)),)


def _pi_f(lms):
    a = _pair_inv(lms)
    return a, a


_pair_inv.defvjp(_pi_f, _pi_b)


@jax.custom_vjp
def _pair_known_inverse(lms, inv):
    return inv


_pair_known_inverse.defvjp(lambda lms, inv: (inv, inv),
                           lambda a, g: (_pi_b(a, g)[0], tuple(jnp.zeros_like(x) for x in a)))


def _d_tile(cq, ab, dz, sb0, sb1, p1, p2, ng, known=None):
    c = DN_C
    chunks = range(cq.shape[0] // c)
    units = [(g, p) for g in chunks for p in range(DN_P)]
    n = range(len(units))
    qkv = _silu(cq)
    gall = -jnp.exp(p1) * jax.nn.softplus(ab + p2)
    ball = _sigmoid(ab)
    left, _ = _half_masks(c)
    row = lax.broadcasted_iota(jnp.int32, (c, 2 * c), 0)
    col = lax.broadcasted_iota(jnp.int32, (c, 2 * c), 1) % c
    causal, strict = row >= col, row > col
    sq = lax.broadcasted_iota(jnp.int32, (c, c), 0) >= lax.broadcasted_iota(jnp.int32, (c, c), 1)
    gc_all = [_dot_hi(sq.astype(F32), gall[g * c:(g + 1) * c]) for g in chunks]
    gc_t = [t.T for t in gc_all]
    bdm = (lax.broadcasted_iota(jnp.int32, (2 * c, 2 * c), 0) // c
           == lax.broadcasted_iota(jnp.int32, (2 * c, 2 * c), 1) // c).astype(F32)

    def two(t, base, g, p):
        return t[g * c:(g + 1) * c, base + 2 * p * DN_D:base + 2 * (p + 1) * DN_D]

    def per_head(t, off, p):
        return jnp.where(left, t[:, off + 2 * p:off + 2 * p + 1], t[:, off + 2 * p + 1:off + 2 * p + 2])

    q = [two(qkv, 0, g, p) for g, p in units]
    k = [two(qkv, BR, g, p) for g, p in units]
    v = [two(qkv, 2 * BR, g, p) for g, p in units]
    q = [t * lax.rsqrt(_segsum(t * t) + EPS) * (DN_D ** -0.5) for t in q]
    k = [t * lax.rsqrt(_segsum(t * t) + EPS) for t in k]
    g2 = [per_head(gc_all[g], 0, p) for g, p in units]
    beta = [per_head(ball[g * c:(g + 1) * c], DN_H, p) for g, p in units]
    grow = [jnp.concatenate([gc_t[g][2 * p:2 * p + 1, :], gc_t[g][2 * p + 1:2 * p + 2, :]], axis=1) for g, p in units]
    decay = [jnp.where(causal, jnp.exp(jnp.where(causal, g2[u] - grow[u], 0.0)), 0.0) for u in n]
    kb = [k[u] * beta[u] for u in n]
    kbd = [_bd(t) for t in k]
    lm = [jnp.where(strict, _mm_nt(kb[u], kbd[u]) * decay[u], 0.0) for u in n]
    ainv = _pair_inv(tuple(lm)) if known is None else _pair_known_inverse(tuple(lm), known)
    egc = [jnp.exp(t) for t in g2]
    uw = [_mm3(ainv[u], jnp.concatenate([_bd(v[u] * beta[u]), _bd(kb[u] * egc[u])], axis=1)) for u in n]
    attn = [_mm_nt(q[u], kbd[u]) * decay[u] for u in n]
    glast = [t[c - 1:c, :] for t in g2]
    kd = [k[u] * jnp.exp(glast[u] - g2[u]) for u in n]
    qd = [q[u] * egc[u] for u in n]
    ng2 = jnp.concatenate([ng, ng], axis=1)
    sbd, starts, outs = [sb0, sb1], [], []
    for g in chunks:
        starts.append(tuple(sbd))
        us = [g * DN_P + p for p in range(DN_P)]
        vnew = [uw[u][:, :2 * DN_D] - _mm(uw[u][:, 2 * DN_D:], sbd[p]) for p, u in enumerate(us)]
        o = [_mm(qd[u], sbd[p]) + _mm(attn[u], _bd(vnew[p])) for p, u in enumerate(us)]
        sbd = [sbd[p] * jnp.exp(glast[u]) + _mm_tn(kd[u], vnew[p]) * bdm for p, u in enumerate(us)]
        outs.append(jnp.concatenate([t * lax.rsqrt(_segsum(t * t) * (1.0 / DN_D) + EPS) * ng2 for t in o], axis=1))
    yd = jnp.concatenate(outs, axis=0) * _silu(dz)
    return (yd, *sbd), (starts, ainv)


def _d_fwd(proj, cw, p1, p2, ng, exchange=None):
    L = proj.shape[0]
    DN_G, DN_T = DN_G_FWD, DN_G_FWD * DN_C
    n = L // DN_T

    def body(qkv_ref, ab_ref, dz_ref, hq_ref, cw_ref, p1_ref, p2_ref, ng_ref, o_ref, sall_ref, inv_ref, s_ref):
        i = pl.program_id(0)

        @pl.when(i == 0)
        def _():
            s_ref[...] = jnp.zeros_like(s_ref)

        keep = (i > 0).astype(F32)
        ext = jnp.concatenate([hq_ref[...] * keep, qkv_ref[...]], axis=0)
        cq = _conv_taps(ext, cw_ref, HALO_S, K_DN, DN_T)
        out, (starts, ainv) = _d_tile(cq, ab_ref[...], dz_ref[...], s_ref[0], s_ref[1], p1_ref[...], p2_ref[...],
                                      ng_ref[...])
        o_ref[...] = out[0]
        for p in range(DN_P):
            s_ref[p] = out[1 + p]
            for g in range(DN_G):
                sall_ref[g, p] = starts[g][p]
                inv_ref[g, p] = ainv[g * DN_P + p]

    return _sweep_with_exchange(
        body, n,
        in_specs=[pl.BlockSpec((DN_T, 3 * BR), lambda i: (i, 3)), pl.BlockSpec((DN_T, 128), lambda i: (i, 26)),
                  pl.BlockSpec((DN_T, BR), lambda i: (i, 12)),
                  pl.BlockSpec((HALO_S, 3 * BR), _halo_map(DN_T, HALO_S, 3)),
                  pl.BlockSpec((HALO_S, 3 * BR), lambda i: (0, 0)),
                  pl.BlockSpec((1, 128), lambda i: (0, 0)), pl.BlockSpec((1, 128), lambda i: (0, 0)),
                  pl.BlockSpec((1, DN_D), lambda i: (0, 0))],
        out_specs=[pl.BlockSpec((DN_T, BR), lambda i: (i, 0)),
                   pl.BlockSpec((DN_G,) + DN_STATE, lambda i: (i, 0, 0, 0)),
                   pl.BlockSpec((DN_G,) + DN_INV, lambda i: (i, 0, 0, 0))],
        out_shape=[jax.ShapeDtypeStruct((L, BR), F32), jax.ShapeDtypeStruct((L // DN_C,) + DN_STATE, F32),
                   jax.ShapeDtypeStruct((L // DN_C,) + DN_INV, F32)],
        scratch_shapes=[pltpu.VMEM(DN_STATE, F32)],
        args=(proj, proj, proj, proj, cw, p1, p2, ng), exchange=exchange, name="d_fwd")


def _sweep_with_exchange(body, steps, in_specs, out_specs, out_shape, scratch_shapes, args, exchange, name):
    if exchange is None:
        res = pl.pallas_call(body, grid=(steps,), in_specs=in_specs, out_specs=out_specs, out_shape=out_shape,
                             scratch_shapes=scratch_shapes, name=name, compiler_params=_cp("arbitrary"))(*args)
        return res, None
    several = isinstance(exchange, list)
    exs = [_Exchange(*e) for e in (exchange if several else [exchange])]
    xs = [x for e in (exchange if several else [exchange]) for x in e[0]]
    ni, no, ns, na = len(in_specs), len(out_specs), len(scratch_shapes), len(xs)

    def carried(*refs):
        ins, xin = refs[:ni], refs[ni:ni + na]
        outs, xout = refs[ni + na:ni + na + no], refs[ni + na + no:ni + 2 * na + no]
        scr, sems = refs[ni + 2 * na + no:ni + 2 * na + no + ns], refs[ni + 2 * na + no + ns:]

        def each(fn_name):
            off = 0
            for j, ex in enumerate(exs):
                getattr(ex, fn_name)(xin[off:off + ex.na], xout[off:off + ex.na], sems[3 * j:3 * j + 3])
                off += ex.na

        @pl.when(pl.program_id(0) == 0)
        def _():
            each("start")

        body(*ins, *outs, *scr)

        @pl.when(pl.program_id(0) == steps - 1)
        def _():
            each("wait")

    res = pl.pallas_call(carried, grid=(steps,), in_specs=list(in_specs) + [s for ex in exs for s in ex.in_specs],
                         out_specs=list(out_specs) + [s for ex in exs for s in ex.out_specs],
                         out_shape=list(out_shape) + [s for ex in exs for s in ex.out_shape],
                         scratch_shapes=list(scratch_shapes) + [s for ex in exs for s in ex.scratch_shapes],
                         name=name + "_x", compiler_params=_cp("arbitrary"))(*args, *xs)
    got, off = [], no
    for ex in exs:
        got.append(res[off:off + ex.na])
        off += ex.na
    return res[:no], (got if several else got[0])


def _d_bwd(proj, dmix, sall, inv, cw, p1, p2, ng, exchange=None):
    L = proj.shape[0]
    DN_G, DN_T = DN_G_BWD, DN_G_BWD * DN_C
    n = L // DN_T

    def body(qkv_ref, ab_ref, dz_ref, hq_ref, dy_ref, sall_ref, inv_ref, cw_ref, p1_ref, p2_ref, ng_ref,
             dp_ref, dcw_ref, dp1_ref, dp2_ref, dng_ref, ds_ref, carry_ref):
        i = pl.program_id(0)

        @pl.when(i == 0)
        def _():
            ds_ref[...] = jnp.zeros_like(ds_ref)
            carry_ref[...] = jnp.zeros_like(carry_ref)
            for r in (dcw_ref, dp1_ref, dp2_ref, dng_ref):
                r[...] = jnp.zeros_like(r)

        keep = (i < n - 1).astype(F32)
        ext = jnp.concatenate([hq_ref[...] * keep, qkv_ref[...]], axis=0)
        cq = _conv_taps(ext, cw_ref, HALO_S, K_DN, DN_T)
        known = tuple(inv_ref[g, p] for g in range(DN_G) for p in range(DN_P))
        _, vj = jax.vjp(lambda *a: _d_tile(*a, known=known)[0], cq, ab_ref[...], dz_ref[...], sall_ref[0, 0],
                        sall_ref[0, 1], p1_ref[...], p2_ref[...], ng_ref[...])
        dcq, dab, ddz, ds0, ds1, dp1, dp2, dng = vj((dy_ref[...], ds_ref[0], ds_ref[1]))
        dp_ref[:, 3 * BR:4 * BR] = ddz
        dp_ref[:, 4 * BR:4 * BR + 128] = dab
        ds_ref[0] = ds0
        ds_ref[1] = ds1
        dp1_ref[...] += dp1
        dp2_ref[...] += dp2
        dng_ref[...] += dng
        dext = _conv_taps_bwd(ext, cw_ref, dcw_ref, dcq, HALO_S, K_DN, DN_T)
        dp_ref[:, 0:3 * BR] = _add_tail(dext[HALO_S:], carry_ref[...])
        carry_ref[...] = dext[:HALO_S]

    rev = lambda i: n - 1 - i
    hmap = _halo_map(DN_T, HALO_S, 3)
    v128 = pl.BlockSpec((1, 128), lambda i: (0, 0))
    return _sweep_with_exchange(
        body, n,
        in_specs=[pl.BlockSpec((DN_T, 3 * BR), lambda i: (rev(i), 3)), pl.BlockSpec((DN_T, 128), lambda i: (rev(i), 26)),
                  pl.BlockSpec((DN_T, BR), lambda i: (rev(i), 12)),
                  pl.BlockSpec((HALO_S, 3 * BR), lambda i: hmap(rev(i))),
                  pl.BlockSpec((DN_T, BR), lambda i: (rev(i), 3)),
                  pl.BlockSpec((DN_G,) + DN_STATE, lambda i: (rev(i), 0, 0, 0)),
                  pl.BlockSpec((DN_G,) + DN_INV, lambda i: (rev(i), 0, 0, 0)),
                  pl.BlockSpec((HALO_S, 3 * BR), lambda i: (0, 0)), v128, v128,
                  pl.BlockSpec((1, DN_D), lambda i: (0, 0))],
        out_specs=[pl.BlockSpec((DN_T, W_D), lambda i: (rev(i), 0)),
                   pl.BlockSpec((HALO_S, 3 * BR), lambda i: (0, 0)), v128, v128,
                   pl.BlockSpec((1, DN_D), lambda i: (0, 0))],
        out_shape=[jax.ShapeDtypeStruct((L, W_D), F32), jax.ShapeDtypeStruct((HALO_S, 3 * BR), F32),
                   jax.ShapeDtypeStruct((1, 128), F32), jax.ShapeDtypeStruct((1, 128), F32),
                   jax.ShapeDtypeStruct((1, DN_D), F32)],
        scratch_shapes=[pltpu.VMEM(DN_STATE, F32), pltpu.VMEM((HALO_S, 3 * BR), F32)],
        args=(proj, proj, proj, proj, dmix, sall, inv, cw, p1, p2, ng), exchange=exchange, name="d_bwd")


def _pick_rows(rows, cap):
    best = 8
    for t in range(8, cap + 1, 8):
        if rows % t == 0:
            best = t
    return best


def _adamw(w, g, m, v, name, lead=None):
    rows, rest = w.shape[0], w.shape[1:]
    tr = _pick_rows(rows, 512) if lead is None else lead
    c1 = 1.0 - ADAM_B1 ** ADAM_STEP
    c2 = 1.0 - ADAM_B2 ** ADAM_STEP

    def body(w_ref, g_ref, m_ref, v_ref, d_ref, mo_ref, vo_ref):
        gv = g_ref[...]
        mn = ADAM_B1 * m_ref[...] + (1.0 - ADAM_B1) * gv
        vn = ADAM_B2 * v_ref[...] + (1.0 - ADAM_B2) * (gv * gv)
        d_ref[...] = -ADAM_LR * ((mn / c1) / (jnp.sqrt(vn / c2) + ADAM_EPS) + ADAM_WD * w_ref[...])
        mo_ref[...] = mn
        vo_ref[...] = vn

    spec = pl.BlockSpec((tr,) + rest, lambda i: (i,) + (0,) * len(rest))
    return pl.pallas_call(
        body, grid=(rows // tr,), in_specs=[spec] * 4, out_specs=[spec] * 3,
        out_shape=[jax.ShapeDtypeStruct(w.shape, F32)] * 3,
        name=name, compiler_params=_cp("parallel"))(w, g, m, v)


def _sum_slots(r, name):
    n, rows, wd = r.shape
    tr = _pick_rows(rows, 384)

    def body(r_ref, o_ref):
        acc = r_ref[0].astype(F32)
        for j in range(1, n):
            acc = acc + r_ref[j].astype(F32)
        o_ref[...] = acc

    return pl.pallas_call(
        body, grid=(rows // tr,),
        in_specs=[pl.BlockSpec((n, tr, wd), lambda i: (0, i, 0))],
        out_specs=pl.BlockSpec((tr, wd), lambda i: (i, 0)),
        out_shape=jax.ShapeDtypeStruct((rows, wd), F32),
        name=name, compiler_params=_cp("parallel"))(r)


AXES = ("x", "y", "c")


def _group_peer(axes, k):
    pos = {a: lax.axis_index(a) for a in AXES}
    idx = 0
    for a in axes:
        idx = idx * 2 + pos[a]
    peer = dict(pos)
    for b, a in enumerate(reversed(axes)):
        if (k >> b) & 1:
            peer[a] = 1 - pos[a]
    return idx, tuple(peer[a] for a in AXES)


MAX_CHUNKS = 4


class _Exchange:
    def __init__(self, xs, axes, mode):
        self.axes, self.mode, self.na, self.n = axes, mode, len(xs), 2 ** len(axes)
        n = self.n
        self.out_shape, self.pieces = [], []
        for x in xs:
            if mode == "gather":
                shape, lead = (n,) + x.shape, x.shape[0]
            elif mode == "scatter":
                shape, lead = x.shape, x.shape[1]
            else:
                shape, lead = (x.shape[0], n * x.shape[1], x.shape[2]), x.shape[0]
            self.out_shape.append(jax.ShapeDtypeStruct(shape, x.dtype))
            big = x.size * x.dtype.itemsize >= (1 << 20)
            if mode == "rows":
                self.pieces.append(lead if lead <= MAX_CHUNKS else 1)
            else:
                self.pieces.append(MAX_CHUNKS if big and lead % (16 * MAX_CHUNKS) == 0 else 1)
        self.in_specs = [pl.BlockSpec(memory_space=pl.ANY)] * self.na
        self.out_specs = [pl.BlockSpec(memory_space=pl.ANY)] * self.na
        self.scratch_shapes = [pltpu.SemaphoreType.DMA((self.na, MAX_CHUNKS, n)),
                               pltpu.SemaphoreType.DMA((self.na, MAX_CHUNKS, n)),
                               pltpu.SemaphoreType.DMA((self.na, MAX_CHUNKS))]

    def _copies(self, x_refs, o_refs, send_sems, recv_sems, local_sems):
        me, _ = _group_peer(self.axes, 0)
        local, remote = [], []
        for a, (x, o) in enumerate(zip(x_refs, o_refs)):
            for c in range(self.pieces[a]):
                if self.mode == "rows":
                    r = x.shape[1]
                    b = slice(None) if self.pieces[a] == 1 else pl.ds(c, 1)
                    src = lambda k, x=x, b=b: x.at[b]
                    dst = o.at[b, pl.ds(me * r, r)]
                else:
                    lead = x.shape[1] if self.mode == "scatter" else x.shape[0]
                    rs = pl.ds(c * (lead // self.pieces[a]), lead // self.pieces[a])
                    if self.mode == "scatter":
                        src = lambda k, x=x, rs=rs: x.at[me ^ k, rs]
                    else:
                        src = lambda k, x=x, rs=rs: x.at[rs]
                    dst = o.at[me, rs]
                local.append(pltpu.make_async_copy(src(0), dst, local_sems.at[a, c]))
                for k in range(1, self.n):
                    remote.append(pltpu.make_async_remote_copy(
                        src_ref=src(k), dst_ref=dst, send_sem=send_sems.at[a, c, k], recv_sem=recv_sems.at[a, c, k],
                        device_id=_group_peer(self.axes, k)[1], device_id_type=MESH))
        return local, remote

    def start(self, x_refs, o_refs, sems):
        local, remote = self._copies(x_refs, o_refs, *sems)
        for cp in local + remote:
            cp.start()

    def wait(self, x_refs, o_refs, sems):
        local, remote = self._copies(x_refs, o_refs, *sems)
        for cp in remote:
            cp.wait_send()
        for cp in remote:
            cp.wait_recv()
        for cp in local:
            cp.wait()


def _exchange(xs, axes, mode, name):
    ex = _Exchange(xs, axes, mode)
    na = ex.na

    def body(*refs):
        ex.start(refs[:na], refs[na:2 * na], refs[2 * na:])
        ex.wait(refs[:na], refs[na:2 * na], refs[2 * na:])

    return pl.pallas_call(body, out_shape=ex.out_shape, in_specs=ex.in_specs, out_specs=ex.out_specs,
                          scratch_shapes=ex.scratch_shapes, name=name)(*xs)


SHARDED_SMALL = (("a_conv_w", 2), ("a_pw_w", 1), ("s5_glu_w", 1), ("c_conv_w", 2), ("d_conv_w", 2))
REPLICATED = ("norm_g", "a_conv_b", "a_ln_g", "a_ln_b", "a_pw_b", "s5_lambda_re", "s5_lambda_im", "s5_b_re",
              "s5_b_im", "s5_c_re", "s5_c_im", "s5_d", "s5_log_dt", "s5_glu_b", "d_a_log", "d_dt_bias",
              "d_norm_g", "final_g")
WEIGHTS = ("norm_g", "w_in", "a_conv_w", "a_conv_b", "a_ln_g", "a_ln_b", "a_pw_w", "a_pw_b", "s5_lambda_re",
           "s5_lambda_im", "s5_b_re", "s5_b_im", "s5_c_re", "s5_c_im", "s5_d", "s5_log_dt", "s5_glu_w",
           "s5_glu_b", "c_conv_w", "d_conv_w", "d_a_log", "d_dt_bias", "d_norm_g", "w_out", "final_g")
LANES = 1024


def _size(shape):
    size = 1
    for d in shape:
        size *= d
    return size


def _slab_rows(shape):
    return -(-_size(shape) // (8 * LANES)) * 8


def _pack(arrs, rows):
    parts = []
    for a in arrs:
        r = _slab_rows(a.shape)
        parts.append(jnp.pad(a.reshape(-1), (0, r * LANES - a.size)).reshape(r, LANES))
    used = sum(p.shape[0] for p in parts)
    if rows > used:
        parts.append(jnp.zeros((rows - used, LANES), parts[0].dtype))
    return jnp.concatenate(parts, axis=0)


def _unpack(slab, shapes):
    out, off = [], 0
    for s in shapes:
        r = _slab_rows(s)
        out.append(slab[off:off + r].reshape(-1)[:_size(s)].reshape(s))
        off += r
    return out


def _rows_for(shapes, mult):
    rows = sum(_slab_rows(s) for s in shapes)
    return -(-rows // mult) * mult


def _row(v, width=None):
    v = v.reshape(1, -1)
    return v if width is None else jnp.pad(v, ((0, 0), (0, width - v.shape[1])))


def _pad_rows(w, rows):
    return jnp.pad(w, ((0, rows - w.shape[0]), (0, 0)))


def _assemble_in(shards):
    n = shards[0].shape[1]
    cut = 12 * BR - 3 * n
    last = shards[3]
    return jnp.concatenate([shards[0], shards[1], shards[2], last[:, :cut], last[:, cut + 2 * DN_H:],
                            last[:, cut:cut + 2 * DN_H],
                            jnp.zeros((last.shape[0], N_INP - N_IN), last.dtype)], axis=1)


def _layer_fwd(x, p, exchanges):
    proj, h = _proj_fwd(x, p["norm_g"], p["wp"])
    ya, a_conv = _a_fwd(proj, p["a_cw"], p["a_cb"], p["a_lng"], p["a_lnb"], p["a_pw"], p["a_pwb"])
    ar, ai, bemb, cemb = _s5_prep(*p["s5"])
    got = {}
    (yb, sin_all, states), got["b"] = _b_fwd(proj, ar, ai, bemb, cemb, p["s5_d"], p["glu_w"], p["glu_b"],
                                             exchanges.get("b"))
    yc = _c_fwd(proj, p["c_cw"])
    (yd, sall, inv), got["d"] = _d_fwd(proj, p["d_cw"], p["d_p1"], p["d_p2"], p["d_ng"], exchanges.get("d"))
    xo = _out_fwd(x, ya, yb, yc, yd, p["wo"])
    return xo, dict(x=x, proj=proj, h=h, ys=(ya, yb, yc, yd), a_conv=a_conv, sin_all=sin_all, states=states,
                    sall=sall, inv=inv,
                    s5=(ar, ai, bemb, cemb)), got


def _layer_bwd(dxo, p, r, exchanges):
    proj = r["proj"]
    ar, ai, bemb, cemb = r["s5"]
    dmix = _out_bwd_x(dxo, p["wo"])
    dwo = _dwout(*r["ys"], dxo)
    dpa, dcw_a, dcb, dlng, dlnb, dpw, dpwb = _a_bwd(proj, dmix, r["a_conv"], p["a_cw"], p["a_cb"], p["a_lng"],
                                                     p["a_lnb"], p["a_pw"], p["a_pwb"])
    got = {}
    (dpb, dar, dai, dbe, dce, ddsk, dgw, dgb), got["b"] = _b_bwd(
        proj, dmix, r["sin_all"], r["states"], ar, ai, bemb, cemb, p["s5_d"], p["glu_w"], p["glu_b"],
        exchanges.get("b"))
    dlre, dlim, dldt, dbre, dbim, dcre, dcim = _s5_prep_bwd(*p["s5"], dar, dai, dbe, dce)
    dpc, dcw_c = _c_bwd(proj, dmix, p["c_cw"])
    (dpd, dcw_d, dp1, dp2, dng), got["d"] = _d_bwd(proj, dmix, r["sall"], r["inv"], p["d_cw"], p["d_p1"],
                                                   p["d_p2"], p["d_ng"], exchanges.get("d"))
    dwa, dwb, dwc, dwd = _dwin(r["h"], dpa, dpb, dpc, dpd)
    dwin = jnp.concatenate([dwa, dwb, dwc, dwd[:, :3 * BR], dwd[:, 4 * BR:4 * BR + 2 * DN_H],
                            dwd[:, 3 * BR:4 * BR]], axis=1)
    ex_proj = exchanges.get("proj")
    if callable(ex_proj):
        ex_proj = ex_proj(got["d"], dwin, dwo)
    (dx, dg), got["proj"] = _proj_bwd_x(r["x"], p["norm_g"], dpa, dpb, dpc, dpd, p["wp"], dxo, ex_proj)

    def unrows(t, perm):
        return jnp.transpose(t.reshape(S5_H, S5_G, S5_P), perm)

    grads = dict(
        norm_g=dg.reshape(-1), w_in=dwin, a_conv_w=dcw_a[:K_A], a_conv_b=dcb.reshape(-1),
        a_ln_g=dlng.reshape(-1), a_ln_b=dlnb.reshape(-1), a_pw_w=dpw, a_pw_b=dpwb.reshape(-1),
        s5_lambda_re=dlre.reshape(S5_G, S5_P), s5_lambda_im=dlim.reshape(S5_G, S5_P),
        s5_b_re=unrows(dbre, (1, 2, 0)), s5_b_im=unrows(dbim, (1, 2, 0)),
        s5_c_re=unrows(dcre, (1, 0, 2)), s5_c_im=unrows(dcim, (1, 0, 2)),
        s5_d=ddsk.reshape(-1), s5_log_dt=dldt[0, :S5_G], s5_glu_w=dgw, s5_glu_b=dgb.reshape(-1),
        c_conv_w=dcw_c[:K_C], d_conv_w=dcw_d[:K_DN], d_a_log=dp1[0, :DN_H], d_dt_bias=dp2[0, :DN_H],
        d_norm_g=dng.reshape(-1), w_out=dwo)
    return dx, grads, got


def _layer_params(full, wp, wo, l):
    return dict(
        norm_g=_row(full["norm_g"][l]), wp=wp,
        a_cw=_pad_rows(full["a_conv_w"][l], HALO_A), a_cb=_row(full["a_conv_b"][l]),
        a_lng=_row(full["a_ln_g"][l]), a_lnb=_row(full["a_ln_b"][l]), a_pw=full["a_pw_w"][l],
        a_pwb=_row(full["a_pw_b"][l]),
        s5=(_row(full["s5_lambda_re"][l]), _row(full["s5_lambda_im"][l]), _row(full["s5_log_dt"][l], 128),
            jnp.transpose(full["s5_b_re"][l], (2, 0, 1)).reshape(S5_H, NS),
            jnp.transpose(full["s5_b_im"][l], (2, 0, 1)).reshape(S5_H, NS),
            jnp.transpose(full["s5_c_re"][l], (1, 0, 2)).reshape(S5_H, NS),
            jnp.transpose(full["s5_c_im"][l], (1, 0, 2)).reshape(S5_H, NS)),
        s5_d=_row(full["s5_d"][l]), glu_w=full["s5_glu_w"][l], glu_b=_row(full["s5_glu_b"][l]),
        c_cw=_pad_rows(full["c_conv_w"][l], HALO_S), d_cw=_pad_rows(full["d_conv_w"][l], HALO_S),
        d_p1=_row(full["d_a_log"][l], 128), d_p2=_row(full["d_dt_bias"][l], 128),
        d_ng=_row(full["d_norm_g"][l]), wo=wo)


def kernel(x, norm_g, w_in, a_conv_w, a_conv_b, a_ln_g, a_ln_b, a_pw_w, a_pw_b, s5_lambda_re, s5_lambda_im, s5_b_re, s5_b_im, s5_c_re, s5_c_im, s5_d, s5_log_dt, s5_glu_w, s5_glu_b, c_conv_w, d_conv_w, d_a_log, d_dt_bias, d_norm_g, w_out, final_g, loss_target, m_norm_g, m_w_in, m_a_conv_w, m_a_conv_b, m_a_ln_g, m_a_ln_b, m_a_pw_w, m_a_pw_b, m_s5_lambda_re, m_s5_lambda_im, m_s5_b_re, m_s5_b_im, m_s5_c_re, m_s5_c_im, m_s5_d, m_s5_log_dt, m_s5_glu_w, m_s5_glu_b, m_c_conv_w, m_d_conv_w, m_d_a_log, m_d_dt_bias, m_d_norm_g, m_w_out, m_final_g, v_norm_g, v_w_in, v_a_conv_w, v_a_conv_b, v_a_ln_g, v_a_ln_b, v_a_pw_w, v_a_pw_b, v_s5_lambda_re, v_s5_lambda_im, v_s5_b_re, v_s5_b_im, v_s5_c_re, v_s5_c_im, v_s5_d, v_s5_log_dt, v_s5_glu_w, v_s5_glu_b, v_c_conv_w, v_d_conv_w, v_d_a_log, v_d_dt_bias, v_d_norm_g, v_w_out, v_final_g):
    given = dict(locals())
    w = {n: given[n] for n in WEIGHTS}
    m = {n: given["m_" + n] for n in WEIGHTS}
    v = {n: given["v_" + n] for n in WEIGHTS}
    xs, tgt = x[0], loss_target[0]

    n_in, n_out = w["w_in"].shape[2], w["w_out"].shape[1]
    sm_names = [n for n, _ in SHARDED_SMALL]
    sm_shapes = [w[n].shape for n in sm_names]
    sm_rows = _rows_for(sm_shapes, 16)
    win_b, wout_b = w["w_in"].astype(BF), w["w_out"].astype(BF)
    g_in, g_out, g_sm = _exchange([win_b[0], wout_b[0], _pack([w[n] for n in sm_names], sm_rows)],
                                  ("x", "y"), "gather", "gather_first")
    full = dict(w)
    parts = [_unpack(g_sm[j], sm_shapes) for j in range(4)]
    for i, (n, ax) in enumerate(SHARDED_SMALL):
        full[n] = jnp.concatenate([parts[j][i] for j in range(4)], axis=ax)

    saved = []
    h = xs
    for l in range(DEPTH):
        p = _layer_params(full, _assemble_in([g_in[j] for j in range(4)]),
                          jnp.concatenate([g_out[j] for j in range(4)], axis=0), l)
        nxt = {}
        if l + 1 < DEPTH:
            nxt = {"d": ([win_b[l + 1]], ("x", "y"), "gather"), "b": ([wout_b[l + 1]], ("x", "y"), "gather")}
        h, r, got = _layer_fwd(h, p, nxt)
        saved.append((p, r))
        if l + 1 < DEPTH:
            (g_in,), (g_out,) = got["d"], got["b"]
    loss_tile, dx, dfg = _loss_bwd(h, _row(full["final_g"]), tgt)

    def big_slots(dwin, dwo):
        s_in = jnp.stack([dwin[:, j * n_in:(j + 1) * n_in].astype(BF) for j in range(4)])
        return [s_in.reshape(8, D_MODEL // 2, n_in), dwo.astype(BF).reshape(8, n_out // 2, D_MODEL)]

    def halves(rv):
        return [_sum_slots(rv[0], "sum_w_in")[None], _sum_slots(rv[1], "sum_w_out")[None]]

    layer_grads, summed, pending, arrived = [None] * DEPTH, [None] * DEPTH, None, {}
    for l in reversed(range(DEPTH)):
        p, r = saved[l]
        ex = {}
        if pending is not None:
            ex["d"] = (pending, AXES, "scatter")
        if l + 2 in arrived:
            ex["b"] = (halves(arrived.pop(l + 2)), ("c",), "rows")
        if l == 0:
            ex["proj"] = lambda came, dwin, dwo: [(halves(came), ("c",), "rows"),
                                                  (big_slots(dwin, dwo), AXES, "scatter")]
        dx, layer_grads[l], got = _layer_bwd(dx, p, r, ex)
        if got["b"] is not None:
            summed[l + 2] = got["b"]
        if got["proj"] is not None:
            summed[1], arrived[0] = got["proj"]
        elif got["d"] is not None:
            arrived[l + 1] = got["d"]
        pending = big_slots(layer_grads[l]["w_in"], layer_grads[l]["w_out"]) if l else None
    grads = {n: jnp.stack([layer_grads[l][n] for l in range(DEPTH)]) for n in WEIGHTS
             if n not in ("final_g", "w_in", "w_out")}
    grads["final_g"] = dfg.reshape(-1)
    slots = []
    for j in range(4):
        sl = [lax.slice_in_dim(grads[n], j * w[n].shape[ax], (j + 1) * w[n].shape[ax], axis=ax)
              for n, ax in SHARDED_SMALL]
        slots.append(_pack(sl, sm_rows))
    rp_shapes = [w[n].shape for n in REPLICATED] + [(1,)]
    rp_rows = _rows_for(rp_shapes, 64)
    r_sm, r_rp = _exchange(
        [jnp.stack(slots).reshape(8, sm_rows // 2, LANES),
         _pack([grads[n] for n in REPLICATED] + [loss_tile[0, 0:1]], rp_rows).reshape(8, rp_rows // 8, LANES)],
        AXES, "scatter", "scatter_last")
    summed[0] = _exchange(halves(arrived.pop(0)) + [_sum_slots(r_sm, "sum_small")[None]], ("c",), "rows",
                          "gather_halves")
    h_sm = summed[0][2]
    h_in = jnp.concatenate([summed[l][0] for l in range(DEPTH)], axis=0)
    h_out = jnp.concatenate([summed[l][1] for l in range(DEPTH)], axis=0)
    (g_rp,) = _exchange([_sum_slots(r_rp, "sum_replicated")], AXES, "gather", "gather_replicated")
    g_rp = g_rp.reshape(rp_rows, LANES)
    g_sm = h_sm.reshape(sm_rows, LANES)

    out = {}

    def put(name, shape, res):
        for key, t in zip(("delta", "new_m", "new_v"), res):
            out[key + "_" + name] = t.reshape(shape)

    g2 = h_out.reshape(DEPTH * n_out, D_MODEL)
    out["grad_w_out"] = g2.reshape(w["w_out"].shape)
    put("w_out", w["w_out"].shape, _adamw(w["w_out"].reshape(g2.shape), g2, m["w_out"].reshape(g2.shape),
                                          v["w_out"].reshape(g2.shape), "adamw_w_out"))
    cm = lambda a: jnp.transpose(a, (2, 0, 1))
    rm = lambda a: jnp.transpose(a, (1, 2, 0))
    g3 = cm(h_in)
    out["grad_w_in"] = rm(g3)
    for key, t in zip(("delta", "new_m", "new_v"),
                      _adamw(cm(w["w_in"]), g3, cm(m["w_in"]), cm(v["w_in"]), "adamw_w_in", lead=n_in // 6)):
        out[key + "_w_in"] = rm(t)
    zero = jnp.zeros((1,), F32)
    res_sm = _adamw(_pack([w[n] for n in sm_names], sm_rows), g_sm, _pack([m[n] for n in sm_names], sm_rows),
                    _pack([v[n] for n in sm_names], sm_rows), "adamw_small")
    res_rp = _adamw(_pack([w[n] for n in REPLICATED] + [zero], rp_rows), g_rp,
                    _pack([m[n] for n in REPLICATED] + [zero], rp_rows),
                    _pack([v[n] for n in REPLICATED] + [zero], rp_rows), "adamw_replicated")
    for key, sm, rp in (("grad", g_sm, g_rp), ("delta", res_sm[0], res_rp[0]), ("new_m", res_sm[1], res_rp[1]),
                        ("new_v", res_sm[2], res_rp[2])):
        for n, t in zip(sm_names, _unpack(sm, sm_shapes)):
            out[key + "_" + n] = t
        for n, t in zip(REPLICATED, _unpack(rp, rp_shapes[:-1])):
            out[key + "_" + n] = t
    loss = _unpack(g_rp, rp_shapes)[-1].reshape(())
    return (loss, dx[None], *[out["grad_" + n] for n in WEIGHTS], *[out["delta_" + n] for n in WEIGHTS],
            *[out["new_m_" + n] for n in WEIGHTS], *[out["new_v_" + n] for n in WEIGHTS])
```
